```python
import jax, jax.numpy as jnp
from jax import lax
import numpy as np

D_MODEL = 1024
BATCH = 8
SEQ = 2048
DEPTH = 4

CTX_LEN = 256
GRID_W = 64

N_MIXERS = 3
N_RG_LAYERS = (DEPTH + 2) // 3
N_MLA_LAYERS = (DEPTH + 1) // 3
N_MLSTM_LAYERS = DEPTH // 3

RG_WIDTH = D_MODEL
RG_BLOCKS = 16
RG_BLOCK_W = RG_WIDTH // RG_BLOCKS
RG_CONV_W = 4
RG_C = 8.0

MLA_HEADS = 16
MLA_Q_RANK = D_MODEL // 4
MLA_KV_RANK = D_MODEL // 8
MLA_NOPE = 64
MLA_ROPE = 32
MLA_V = 64
MLA_QK = MLA_NOPE + MLA_ROPE
ATTN_SCALE = MLA_QK ** -0.5
ROPE_AXIS_DIM = MLA_ROPE // 2
ROPE_BASE = 10000.0
Q_BLOCK = 128

ML_HEADS = 4
ML_DV = D_MODEL // ML_HEADS
ML_DQK = ML_DV // 2
ML_IN = 2 * ML_HEADS * ML_DQK + 2 * ML_HEADS * ML_DV + 4 * ML_HEADS
ML_CHUNK = 64
ML_M_INIT = -1e30

MOE_GROUPS = 8
MOE_PER_GROUP = 8
MOE_EXPERTS = MOE_GROUPS * MOE_PER_GROUP
MOE_TOPK = 2
MOE_FF = D_MODEL // 4
MOE_BLOCK = 128

RMS_EPS = 1e-6

kernel_name = "hybrid_flow_backbone_rglru_mla_mlstm_hmoe"


def rms_norm(x, g):
    xf = x.astype(jnp.float32)
    y = xf * lax.rsqrt(jnp.mean(xf * xf, axis=-1, keepdims=True) + RMS_EPS)
    return (y * g.astype(jnp.float32)).astype(x.dtype)


def depthwise_conv(u, w, b):
    left = RG_CONV_W // 2
    y = lax.conv_general_dilated(u, w[:, None, :].astype(u.dtype), window_strides=(1,),
                                 padding=[(left, RG_CONV_W - 1 - left)],
                                 dimension_numbers=('NWC', 'WIO', 'NWC'),
                                 feature_group_count=u.shape[-1])
    return y + b


def _lin_combine(left, right):
    a1, b1 = left
    a2, b2 = right
    return a1 * a2, a2 * b1 + b2


def rglru_scan(u, w_gate, b_gate, lam, h0):
    B, L, _ = u.shape
    ub = u.reshape(B, L, RG_BLOCKS, RG_BLOCK_W)
    g = jnp.einsum('blni,gnij->gblnj', ub, w_gate.astype(jnp.float32)).reshape(2, B, L, RG_WIDTH)
    g = g + b_gate.astype(jnp.float32)[:, None, None, :]
    r = jax.nn.sigmoid(g[0])
    i_g = jax.nn.sigmoid(g[1])
    log_a = -RG_C * r * jax.nn.softplus(-lam.astype(jnp.float32))
    a = jnp.exp(log_a)
    b = jnp.sqrt(-jnp.expm1(2.0 * log_a)) * (i_g * u)
    a_cum, b_cum = lax.associative_scan(_lin_combine, (a, b), axis=1)
    return a_cum * h0[:, None, :] + b_cum


def rglru_mixer(h_ctx, h_lat, w_in, conv_w, conv_b, gate_w, gate_b, lam, w_out):
    def branches(h):
        gate, u = jnp.split(h @ w_in, 2, axis=-1)
        return jax.nn.gelu(gate), depthwise_conv(u, conv_w, conv_b).astype(jnp.float32)

    g_ctx, u_ctx = branches(h_ctx)
    g_lat, u_lat = branches(h_lat)
    zero = jnp.zeros((h_lat.shape[0], RG_WIDTH), jnp.float32)
    hc_f = rglru_scan(u_ctx, gate_w[0], gate_b[0], lam[0], zero)
    hl_f = rglru_scan(u_lat, gate_w[0], gate_b[0], lam[0], hc_f[:, -1])
    hc_b = rglru_scan(jnp.flip(u_ctx, 1), gate_w[1], gate_b[1], lam[1], zero)
    hl_b = rglru_scan(jnp.flip(u_lat, 1), gate_w[1], gate_b[1], lam[1], hc_b[:, -1])
    y_ctx = (g_ctx * (hc_f + jnp.flip(hc_b, 1)).astype(g_ctx.dtype)) @ w_out
    y_lat = (g_lat * (hl_f + jnp.flip(hl_b, 1)).astype(g_lat.dtype)) @ w_out
    return y_ctx, y_lat


def rope_2d(t, ang_row, ang_col):
    def rot(z, ang):
        z1, z2 = jnp.split(z, 2, axis=-1)
        cs = jnp.cos(ang)[None, :, None, :].astype(z.dtype)
        sn = jnp.sin(ang)[None, :, None, :].astype(z.dtype)
        return jnp.concatenate([z1 * cs - z2 * sn, z2 * cs + z1 * sn], axis=-1)
    t_row, t_col = jnp.split(t, 2, axis=-1)
    return jnp.concatenate([rot(t_row, ang_row), rot(t_col, ang_col)], axis=-1)


def mla_project(h, w_down, q_norm, kv_norm, w_uq, w_ukv, qk_norm, angles):
    B, L, _ = h.shape
    cq, ckv, k_rope = jnp.split(h @ w_down, [MLA_Q_RANK, MLA_Q_RANK + MLA_KV_RANK], axis=-1)
    q = (rms_norm(cq, q_norm) @ w_uq).reshape(B, L, MLA_HEADS, MLA_QK)
    kv = (rms_norm(ckv, kv_norm) @ w_ukv).reshape(B, L, MLA_HEADS, MLA_NOPE + MLA_V)
    k_nope, v = jnp.split(kv, [MLA_NOPE], axis=-1)
    k = jnp.concatenate([k_nope, jnp.broadcast_to(k_rope[:, :, None, :], (B, L, MLA_HEADS, MLA_ROPE))], axis=-1)
    q = rms_norm(q, qk_norm[0])
    k = rms_norm(k, qk_norm[1])
    if angles is not None:
        ang_row, ang_col = angles
        q = jnp.concatenate([q[..., :MLA_NOPE], rope_2d(q[..., MLA_NOPE:], ang_row, ang_col)], axis=-1)
        k = jnp.concatenate([k[..., :MLA_NOPE], rope_2d(k[..., MLA_NOPE:], ang_row, ang_col)], axis=-1)
    return q, k, v


def block_attention(q, k, v):
    B, Lq, H, dq = q.shape
    qb = jnp.moveaxis(q.reshape(B, Lq // Q_BLOCK, Q_BLOCK, H, dq), 1, 0)

    def one_block(qi):
        s = jnp.einsum('bqhd,bkhd->bhqk', qi, k).astype(jnp.float32) * ATTN_SCALE
        p = jax.nn.softmax(s, axis=-1).astype(v.dtype)
        return jnp.einsum('bhqk,bkhd->bqhd', p, v)

    o = lax.map(one_block, qb)
    return jnp.moveaxis(o, 0, 1).reshape(B, Lq, H * v.shape[-1])


def mla_mixer(h_ctx, h_lat, w_down, q_norm, kv_norm, w_uq, w_ukv, qk_norm, w_o, angles):
    qc, kc, vc = mla_project(h_ctx, w_down, q_norm, kv_norm, w_uq, w_ukv, qk_norm, None)
    ql, kl, vl = mla_project(h_lat, w_down, q_norm, kv_norm, w_uq, w_ukv, qk_norm, angles)
    o_ctx = block_attention(qc, kc, vc)
    o_lat = block_attention(ql, jnp.concatenate([kc, kl], axis=1), jnp.concatenate([vc, vl], axis=1))
    return o_ctx @ w_o, o_lat @ w_o


def mlstm_chunked(q, k, v, ig, lf, state):
    B, H, L, _ = q.shape
    n_chunks = L // ML_CHUNK

    def to_chunks(t):
        return jnp.moveaxis(t.reshape(B, H, n_chunks, ML_CHUNK, *t.shape[3:]), 2, 0)

    causal = jnp.tril(jnp.ones((ML_CHUNK, ML_CHUNK), dtype=bool))

    def step(carry, inp):
        C, n, m = carry
        qc, kc, vc, igc, lfc = inp
        b = jnp.cumsum(lfc, axis=-1)
        d_log = jnp.where(causal, b[..., :, None] - b[..., None, :] + igc[..., None, :], -jnp.inf)
        inter_log = b + m[..., None]
        m_t = jnp.maximum(inter_log, jnp.max(d_log, axis=-1))
        s = jnp.einsum('bhtd,bhsd->bhts', qc, kc) * jnp.exp(d_log - m_t[..., None])
        inter = jnp.exp(inter_log - m_t)
        num = jnp.einsum('bhts,bhse->bhte', s, vc) + inter[..., None] * jnp.einsum('bhtd,bhde->bhte', qc, C)
        den = jnp.sum(s, axis=-1) + inter * jnp.einsum('bhtd,bhd->bht', qc, n)
        h = num / jnp.maximum(jnp.abs(den), jnp.exp(-m_t))[..., None]
        b_last = b[..., -1]
        w_log = b_last[..., None] - b + igc
        m_new = jnp.maximum(b_last + m, jnp.max(w_log, axis=-1))
        w = jnp.exp(w_log - m_new[..., None])
        decay = jnp.exp(b_last + m - m_new)
        C_new = decay[..., None, None] * C + jnp.einsum('bhs,bhsd,bhse->bhde', w, kc, vc)
        n_new = decay[..., None] * n + jnp.einsum('bhs,bhsd->bhd', w, kc)
        return (C_new, n_new, m_new), h

    state, h = lax.scan(step, state, tuple(to_chunks(t) for t in (q, k, v, ig, lf)))
    return jnp.moveaxis(h, 0, 2).reshape(B, H, L, v.shape[-1]), state


def mlstm_mixer(h_ctx, h_lat, w_in, gate_b, out_norm, w_out):
    def project(h):
        B, L, _ = h.shape
        q, k, v, og, gates = jnp.split(h @ w_in, [ML_HEADS * ML_DQK, 2 * ML_HEADS * ML_DQK,
                                                  2 * ML_HEADS * ML_DQK + ML_HEADS * ML_DV,
                                                  2 * ML_HEADS * ML_DQK + 2 * ML_HEADS * ML_DV], axis=-1)
        heads = lambda t: jnp.moveaxis(t.reshape(B, L, ML_HEADS, -1), 1, 2).astype(jnp.float32)
        gates = jnp.moveaxis((gates.reshape(B, L, 4, ML_HEADS) + gate_b).astype(jnp.float32), 1, 3)
        return heads(q) * ML_DQK ** -0.5, heads(k), heads(v), og, gates

    qc, kc, vc, ogc, gc = project(h_ctx)
    ql, kl, vl, ogl, gl = project(h_lat)
    B = h_lat.shape[0]
    init = (jnp.zeros((B, ML_HEADS, ML_DQK, ML_DV), jnp.float32),
            jnp.zeros((B, ML_HEADS, ML_DQK), jnp.float32),
            jnp.full((B, ML_HEADS), ML_M_INIT, jnp.float32))
    flip = lambda t: jnp.flip(t, axis=2)
    lsig = jax.nn.log_sigmoid
    hc_f, st_f = mlstm_chunked(qc, kc, vc, gc[:, 0], lsig(gc[:, 1]), init)
    hl_f, _ = mlstm_chunked(ql, kl, vl, gl[:, 0], lsig(gl[:, 1]), st_f)
    hc_b, st_b = mlstm_chunked(flip(qc), flip(kc), flip(vc), flip(gc[:, 2]), flip(lsig(gc[:, 3])), init)
    hl_b, _ = mlstm_chunked(flip(ql), flip(kl), flip(vl), flip(gl[:, 2]), flip(lsig(gl[:, 3])), st_b)

    def readout(h_sum, og):
        B_, H, L, _ = h_sum.shape
        hn = rms_norm(jnp.moveaxis(h_sum, 1, 2), out_norm.reshape(ML_HEADS, ML_DV)).reshape(B_, L, H * ML_DV)
        return (hn.astype(og.dtype) * jax.nn.sigmoid(og)) @ w_out

    return readout(hc_f + flip(hc_b), ogc), readout(hl_f + flip(hl_b), ogl)


def grouped_expert_ffn(h, expert_id, weights, w_gate_up, w_down):
    N, D = h.shape
    NK = N * MOE_TOPK
    n_blocks = -(-NK // MOE_BLOCK) + MOE_EXPERTS
    e_flat = expert_id.reshape(NK)
    order = jnp.argsort(e_flat)
    e_sorted = e_flat[order]
    tok_sorted = (order // MOE_TOPK).astype(jnp.int32)
    w_sorted = weights.reshape(NK)[order]
    counts = jnp.bincount(e_flat, length=MOE_EXPERTS)
    starts = jnp.cumsum(counts) - counts
    padded = (counts + MOE_BLOCK - 1) // MOE_BLOCK * MOE_BLOCK
    pad_end = jnp.cumsum(padded)
    pad_start = pad_end - padded
    dest = pad_start[e_sorted] + jnp.arange(NK, dtype=jnp.int32) - starts[e_sorted]
    slot_token = jnp.full((n_blocks * MOE_BLOCK,), N, jnp.int32).at[dest].set(tok_sorted)
    block_expert = jnp.minimum(jnp.searchsorted(pad_end, jnp.arange(n_blocks, dtype=jnp.int32) * MOE_BLOCK,
                                                side='right'), MOE_EXPERTS - 1)
    h_pad = jnp.concatenate([h, jnp.zeros((1, D), h.dtype)], axis=0)

    def run_block(args):
        idx, e = args
        gate, up = jnp.split(h_pad[idx] @ w_gate_up[e], 2, axis=-1)
        return (jax.nn.silu(gate) * up) @ w_down[e]

    y_slots = lax.map(run_block, (slot_token.reshape(n_blocks, MOE_BLOCK), block_expert)).reshape(-1, D)
    contrib = y_slots[dest] * w_sorted[:, None].astype(y_slots.dtype)
    return jax.ops.segment_sum(contrib, tok_sorted, num_segments=N)


def hier_moe(h, w_group, b_group, w_expert, b_expert, w_gate_up, w_down):
    N = h.shape[0]
    g_prob = jax.nn.softmax((h @ w_group + b_group).astype(jnp.float32), axis=-1)
    p_top, g_sel = lax.top_k(g_prob, 1)
    e_logits = (h @ w_expert + b_expert).astype(jnp.float32).reshape(N, MOE_GROUPS, MOE_PER_GROUP)
    e_in_group = jnp.take_along_axis(e_logits, g_sel[:, :, None], axis=1)[:, 0]
    e_top, e_sel = lax.top_k(e_in_group, MOE_TOPK)
    weights = jax.nn.softmax(e_top, axis=-1) * p_top
    expert_id = g_sel * MOE_PER_GROUP + e_sel
    return grouped_expert_ffn(h, expert_id, weights, w_gate_up, w_down)


def setup_inputs(seed: int = 0) -> dict:
    key = jax.random.key(seed)
    keys = iter(list(jax.random.split(key, 40)))
    f32 = jnp.float32
    nrm = lambda shape, scale: jax.random.normal(next(keys), shape, f32) * scale
    gain = lambda shape: 1.0 + nrm(shape, 0.02)
    D = D_MODEL
    x = nrm((BATCH, SEQ, D), 1.0)
    c = nrm((BATCH, D), 1.0)
    ctx = nrm((BATCH, CTX_LEN, D), 1.0)
    c_ctx = nrm((D,), 1.0)
    ada_w = nrm((DEPTH, D, 6 * D), 0.5 * D ** -0.5)
    ada_b = nrm((DEPTH, 6 * D), 0.02)
    norm_mix = gain((DEPTH, D))
    norm_ffn = gain((DEPTH, D))
    rg_w_in = nrm((N_RG_LAYERS, D, 2 * RG_WIDTH), D ** -0.5)
    rg_conv_w = nrm((N_RG_LAYERS, RG_CONV_W, RG_WIDTH), RG_CONV_W ** -0.5)
    rg_conv_b = nrm((N_RG_LAYERS, RG_WIDTH), 0.02)
    rg_gate_w = nrm((N_RG_LAYERS, 2, 2, RG_BLOCKS, RG_BLOCK_W, RG_BLOCK_W), RG_BLOCK_W ** -0.5)
    rg_gate_b = nrm((N_RG_LAYERS, 2, 2, RG_WIDTH), 0.02)
    a0 = jax.random.uniform(next(keys), (N_RG_LAYERS, 2, RG_WIDTH), f32, 0.9, 0.999)
    rg_lambda = jnp.log(a0) - jnp.log1p(-a0)
    rg_w_out = nrm((N_RG_LAYERS, RG_WIDTH, D), RG_WIDTH ** -0.5)
    mla_w_down = nrm((N_MLA_LAYERS, D, MLA_Q_RANK + MLA_KV_RANK + MLA_ROPE), D ** -0.5)
    mla_q_norm = gain((N_MLA_LAYERS, MLA_Q_RANK))
    mla_kv_norm = gain((N_MLA_LAYERS, MLA_KV_RANK))
    mla_w_uq = nrm((N_MLA_LAYERS, MLA_Q_RANK, MLA_HEADS * MLA_QK), MLA_Q_RANK ** -0.5)
    mla_w_ukv = nrm((N_MLA_LAYERS, MLA_KV_RANK, MLA_HEADS * (MLA_NOPE + MLA_V)), MLA_KV_RANK ** -0.5)
    mla_qk_norm = gain((N_MLA_LAYERS, 2, MLA_QK))
    mla_w_o = nrm((N_MLA_LAYERS, MLA_HEADS * MLA_V, D), (MLA_HEADS * MLA_V) ** -0.5)
    ml_w_in = nrm((N_MLSTM_LAYERS, D, ML_IN), D ** -0.5)
    ml_gate_b = nrm((N_MLSTM_LAYERS, 4, ML_HEADS), 0.1) + jnp.array([0.0, 3.0, 0.0, 3.0], f32)[None, :, None]
    ml_out_norm = gain((N_MLSTM_LAYERS, ML_HEADS * ML_DV))
    ml_w_out = nrm((N_MLSTM_LAYERS, ML_HEADS * ML_DV, D), (ML_HEADS * ML_DV) ** -0.5)
    moe_w_group = nrm((DEPTH, D, MOE_GROUPS), D ** -0.5)
    moe_b_group = nrm((DEPTH, MOE_GROUPS), 0.01)
    moe_w_expert = nrm((DEPTH, D, MOE_EXPERTS), D ** -0.5)
    moe_b_expert = nrm((DEPTH, MOE_EXPERTS), 0.01)
    moe_w_gate_up = nrm((DEPTH, MOE_EXPERTS, D, 2 * MOE_FF), D ** -0.5)
    moe_w_down = nrm((DEPTH, MOE_EXPERTS, MOE_FF, D), MOE_FF ** -0.5)
    return {"x": x, "c": c, "ctx": ctx, "c_ctx": c_ctx,
            "ada_w": ada_w, "ada_b": ada_b, "norm_mix": norm_mix, "norm_ffn": norm_ffn,
            "rg_w_in": rg_w_in, "rg_conv_w": rg_conv_w, "rg_conv_b": rg_conv_b, "rg_gate_w": rg_gate_w,
            "rg_gate_b": rg_gate_b, "rg_lambda": rg_lambda, "rg_w_out": rg_w_out,
            "mla_w_down": mla_w_down, "mla_q_norm": mla_q_norm, "mla_kv_norm": mla_kv_norm,
            "mla_w_uq": mla_w_uq, "mla_w_ukv": mla_w_ukv, "mla_qk_norm": mla_qk_norm, "mla_w_o": mla_w_o,
            "ml_w_in": ml_w_in, "ml_gate_b": ml_gate_b, "ml_out_norm": ml_out_norm, "ml_w_out": ml_w_out,
            "moe_w_group": moe_w_group, "moe_b_group": moe_b_group, "moe_w_expert": moe_w_expert,
            "moe_b_expert": moe_b_expert, "moe_w_gate_up": moe_w_gate_up, "moe_w_down": moe_w_down}


def reference(x, c, ctx, c_ctx, ada_w, ada_b, norm_mix, norm_ffn,
              rg_w_in, rg_conv_w, rg_conv_b, rg_gate_w, rg_gate_b, rg_lambda, rg_w_out,
              mla_w_down, mla_q_norm, mla_kv_norm, mla_w_uq, mla_w_ukv, mla_qk_norm, mla_w_o,
              ml_w_in, ml_gate_b, ml_out_norm, ml_w_out,
              moe_w_group, moe_b_group, moe_w_expert, moe_b_expert, moe_w_gate_up, moe_w_down):
    B, L, D = x.shape
    ROWS = L // GRID_W
    row = jnp.broadcast_to(jnp.arange(ROWS, dtype=jnp.float32)[:, None], (ROWS, GRID_W)).reshape(L)
    col = jnp.broadcast_to(jnp.arange(GRID_W, dtype=jnp.float32)[None, :], (ROWS, GRID_W)).reshape(L)
    inv_freq = ROPE_BASE ** (-jnp.arange(0, ROPE_AXIS_DIM, 2, dtype=jnp.float32) / ROPE_AXIS_DIM)
    angles = (row[:, None] * inv_freq, col[:, None] * inv_freq)

    for i in range(DEPTH):
        last = i == DEPTH - 1
        mod_lat = (jax.nn.silu(c) @ ada_w[i] + ada_b[i]).reshape(B, 6, 1, D)
        mod_ctx = (jax.nn.silu(c_ctx) @ ada_w[i] + ada_b[i]).reshape(6, D)
        h_lat = rms_norm(x, norm_mix[i]) * (1 + mod_lat[:, 1]) + mod_lat[:, 0]
        h_ctx = rms_norm(ctx, norm_mix[i]) * (1 + mod_ctx[1]) + mod_ctx[0]
        kind, j = i % N_MIXERS, i // N_MIXERS
        if kind == 0:
            y_ctx, y_lat = rglru_mixer(h_ctx, h_lat, rg_w_in[j], rg_conv_w[j], rg_conv_b[j], rg_gate_w[j],
                                       rg_gate_b[j], rg_lambda[j], rg_w_out[j])
        elif kind == 1:
            y_ctx, y_lat = mla_mixer(h_ctx, h_lat, mla_w_down[j], mla_q_norm[j], mla_kv_norm[j], mla_w_uq[j],
                                     mla_w_ukv[j], mla_qk_norm[j], mla_w_o[j], angles)
        else:
            y_ctx, y_lat = mlstm_mixer(h_ctx, h_lat, ml_w_in[j], ml_gate_b[j], ml_out_norm[j], ml_w_out[j])
        x = x + mod_lat[:, 2] * y_lat
        f_lat = rms_norm(x, norm_ffn[i]) * (1 + mod_lat[:, 4]) + mod_lat[:, 3]
        moe_params = (moe_w_group[i], moe_b_group[i], moe_w_expert[i], moe_b_expert[i],
                      moe_w_gate_up[i], moe_w_down[i])
        if last:
            x = x + mod_lat[:, 5] * hier_moe(f_lat.reshape(B * L, D), *moe_params).reshape(B, L, D)
        else:
            ctx = ctx + mod_ctx[2] * y_ctx
            f_ctx = rms_norm(ctx, norm_ffn[i]) * (1 + mod_ctx[4]) + mod_ctx[3]
            y = hier_moe(jnp.concatenate([f_lat.reshape(B * L, D), f_ctx.reshape(-1, D)], axis=0), *moe_params)
            x = x + mod_lat[:, 5] * y[:B * L].reshape(B, L, D)
            ctx = ctx + mod_ctx[5] * y[B * L:].reshape(B, -1, D)
    return x
```

```python
import functools
import math

import jax
import jax.numpy as jnp
from jax import lax
from jax.experimental import pallas as pl
from jax.experimental.pallas import tpu as pltpu

F32 = jnp.float32
BF16 = jnp.bfloat16

D_MODEL = 1024
RMS_EPS = 1e-6

TM = 256
VMEM_LIMIT = 48 * 1024 * 1024

RG_BLOCK_W = 64
RG_CHUNK = 256
RG_CONV_W = 4
RG_C = 8.0
RG_TT = 64

MLA_HEADS = 16
MLA_Q_RANK = 256
MLA_KV_RANK = 128
MLA_NOPE = 64
MLA_ROPE = 32
MLA_V = 64
MLA_QK = MLA_NOPE + MLA_ROPE
MLA_HP = 128
ROPE_AXIS_DIM = MLA_ROPE // 2
ROPE_BASE = 10000.0
GRID_W = 64
ATT_TQ = 512

ML_HEADS = 4
ML_DV = 256
ML_DQK = 128
ML_TC = 256
ML_M_INIT = -1e30
ML_NP = 3200

MOE_GROUPS = 8
MOE_PER_GROUP = 8
MOE_EXPERTS = 64
MOE_TOPK = 2
MOE_FF = 256
MOE_BM = 256


def _cparams(sem):
    return pltpu.CompilerParams(dimension_semantics=sem, vmem_limit_bytes=VMEM_LIMIT)


def _dot(a, b):
    return jnp.dot(a, b, preferred_element_type=F32)


def _dot_t(a, b):
    return lax.dot_general(a, b, (((1,), (1,)), ((), ())), preferred_element_type=F32)


def _dot3(a, b):
    ah = a.astype(BF16)
    al = (a - ah.astype(F32)).astype(BF16)
    bh = b.astype(BF16)
    bl = (b - bh.astype(F32)).astype(BF16)
    return _dot(ah, bh) + (_dot(al, bh) + _dot(ah, bl))


def _sigmoid(x):
    return 1.0 / (1.0 + jnp.exp(-x))


def _softplus(x):
    return jnp.maximum(x, 0.0) + jnp.log1p(jnp.exp(-jnp.abs(x)))


def _gelu_tanh(x):
    return 0.5 * x * (1.0 + jnp.tanh(0.7978845608028654 * (x + 0.044715 * (x * x * x))))


def _rms(x, n=None):
    n = x.shape[-1] if n is None else n
    ms = jnp.sum(x * x, axis=-1, keepdims=True) * (1.0 / n)
    return x * lax.rsqrt(ms + RMS_EPS)


class _Tiles:
    def __init__(self, B, L, C):
        assert L % TM == 0 and C % TM == 0
        self.B, self.L, self.C = B, L, C
        self.lt, self.ct = L // TM, C // TM
        self.nlat = B * self.lt
        self.ntiles = self.nlat + B * self.ct
        self.NT = self.ntiles * TM

    def batch(self, i):
        return jnp.where(i < self.nlat, i // self.lt, (i - self.nlat) // self.ct)

    def mod(self, i):
        return jnp.where(i < self.nlat, i // self.lt, self.B)

    def ttile(self, i):
        return jnp.where(i < self.nlat, self.ct + i % self.lt, (i - self.nlat) % self.ct)


def _mod_kernel(c_ref, w_ref, b_ref, o_ref):
    c = c_ref[...]
    o_ref[0] = _dot3(c * _sigmoid(c), w_ref[0]) + b_ref[0]


def _modulation(cc, ada_w, ada_b):
    depth, d, n = ada_w.shape
    tn = 1536
    return pl.pallas_call(
        _mod_kernel,
        grid=(depth, n // tn),
        in_specs=[pl.BlockSpec((16, d), lambda l, j: (0, 0)),
                  pl.BlockSpec((1, d, tn), lambda l, j: (l, 0, j)),
                  pl.BlockSpec((1, 1, tn), lambda l, j: (l, 0, j))],
        out_specs=pl.BlockSpec((1, 16, tn), lambda l, j: (l, 0, j)),
        out_shape=jax.ShapeDtypeStruct((depth, 16, n), F32),
        compiler_params=_cparams(("arbitrary", "arbitrary")),
        name="ada_modulation",
    )(cc, ada_w, ada_b.reshape(depth, 1, n))


def _norm_mod(x, gain, mod, shift_row, scale_row):
    return _rms(x) * gain * (1.0 + mod[scale_row:scale_row + 1, :]) + mod[shift_row:shift_row + 1, :]


def _norm_proj_kernel(x_ref, mod_ref, g_ref, w_ref, o_ref, hn_ref):
    @pl.when(pl.program_id(1) == 0)
    def _():
        hn_ref[...] = _norm_mod(x_ref[...], g_ref[...], mod_ref[0], 0, 1).astype(BF16)

    o_ref[...] = _dot(hn_ref[...], w_ref[...]).astype(o_ref.dtype)


def _norm_proj(tl, x, mod, gain, w, tn, out_dtype=F32, name="norm_proj"):
    n = w.shape[1]
    return pl.pallas_call(
        _norm_proj_kernel,
        grid=(tl.ntiles, n // tn),
        in_specs=[pl.BlockSpec((TM, D_MODEL), lambda i, j: (i, 0)),
                  pl.BlockSpec((1, 8, D_MODEL), lambda i, j: (tl.mod(i), 0, 0)),
                  pl.BlockSpec((1, D_MODEL), lambda i, j: (0, 0)),
                  pl.BlockSpec((D_MODEL, tn), lambda i, j: (0, j))],
        out_specs=pl.BlockSpec((TM, tn), lambda i, j: (i, j)),
        out_shape=jax.ShapeDtypeStruct((tl.NT, n), out_dtype),
        scratch_shapes=[pltpu.VMEM((TM, D_MODEL), BF16)],
        compiler_params=_cparams(("arbitrary", "arbitrary")),
        name=name,
    )(x, mod, gain, w)


def _rg_in_kernel(x_ref, mod_ref, g_ref, wg_ref, wu_ref, gate_ref, u_ref, hn_ref):
    @pl.when(pl.program_id(1) == 0)
    def _():
        hn_ref[...] = _norm_mod(x_ref[...], g_ref[...], mod_ref[0], 0, 1).astype(BF16)

    h = hn_ref[...]
    gate_ref[...] = _gelu_tanh(_dot(h, wg_ref[...]))
    u_ref[...] = _dot(h, wu_ref[...])


def _rg_in(tl, x, mod, gain, w_in):
    W = D_MODEL
    tn = 512
    nj = W // tn
    ltot = tl.L + tl.C
    return pl.pallas_call(
        _rg_in_kernel,
        grid=(tl.ntiles, nj),
        in_specs=[pl.BlockSpec((TM, D_MODEL), lambda i, j: (i, 0)),
                  pl.BlockSpec((1, 8, D_MODEL), lambda i, j: (tl.mod(i), 0, 0)),
                  pl.BlockSpec((1, D_MODEL), lambda i, j: (0, 0)),
                  pl.BlockSpec((D_MODEL, tn), lambda i, j: (0, j)),
                  pl.BlockSpec((D_MODEL, tn), lambda i, j: (0, nj + j))],
        out_specs=[pl.BlockSpec((TM, tn), lambda i, j: (i, j)),
                   pl.BlockSpec((TM, tn), lambda i, j: (tl.ttile(i), tl.batch(i) * nj + j))],
        out_shape=[jax.ShapeDtypeStruct((tl.NT, W), F32),
                   jax.ShapeDtypeStruct((ltot, tl.B * W), F32)],
        scratch_shapes=[pltpu.VMEM((TM, D_MODEL), BF16)],
        compiler_params=_cparams(("arbitrary", "arbitrary")),
        name="rg_in_proj",
    )(x, mod, gain, w_in, w_in)


def _rg_tile_order(d, k, ct, ntt):
    bwd = jnp.where(k < ct, ct - 1 - k, ntt - 1 - (k - ct))
    return jnp.where(d == 0, k, bwd)


def _rg_scan_kernel(um_ref, up_ref, un_ref, cw_ref, cb_ref, wg_ref, gb_ref, lam_ref, o_ref,
                    ext_ref, a_ref, b_ref, h_ref, *, B, ct, ntt):
    d = pl.program_id(0)
    k = pl.program_id(1)
    tile = _rg_tile_order(d, k, ct, ntt)
    R = RG_TT * B

    @pl.when(k == 0)
    def _():
        h_ref[...] = jnp.zeros_like(h_ref)

    seq_start = jnp.logical_or(tile == 0, tile == ct)
    seq_end = jnp.logical_or(tile == ct - 1, tile == ntt - 1)
    ext_ref[0:2 * B, :] = jnp.where(seq_start, 0.0, up_ref[...])
    ext_ref[2 * B:2 * B + R, :] = um_ref[...]
    ext_ref[2 * B + R:3 * B + R, :] = jnp.where(seq_end, 0.0, un_ref[...])
    cw = cw_ref[...]
    uc = cb_ref[...] + cw[0:1, :] * ext_ref[0:R, :]
    for j in range(1, RG_CONV_W):
        uc = uc + cw[j:j + 1, :] * ext_ref[j * B:j * B + R, :]
    ucb = uc.astype(BF16)
    c_lam = -RG_C * _softplus(-lam_ref[0])
    for c in range(D_MODEL // RG_CHUNK):
        sl = slice(c * RG_CHUNK, (c + 1) * RG_CHUNK)
        z = _dot(ucb[:, sl], wg_ref[0, c]) + gb_ref[0, c]
        r = _sigmoid(z[:, :RG_CHUNK])
        ig = _sigmoid(z[:, RG_CHUNK:])
        log_a = c_lam[:, sl] * r
        a = jnp.exp(log_a)
        one_minus_a2 = -jnp.tanh(log_a) * (a * a + 1.0)
        a_ref[:, sl] = a
        b_ref[:, sl] = jnp.sqrt(one_minus_a2) * (ig * uc[:, sl])

    def step(t, h):
        tt = jnp.where(d == 0, t, RG_TT - 1 - t)
        row = pl.multiple_of(tt * B, B)
        h = a_ref[pl.ds(row, B), :] * h + b_ref[pl.ds(row, B), :]
        o_ref[0, pl.ds(row, B), :] = h
        return h

    h_ref[...] = lax.fori_loop(0, RG_TT, step, h_ref[...], unroll=8)


def _rg_scan(tl, u_tm, conv_w, conv_b, wg, gb, lam):
    B, W = tl.B, D_MODEL
    assert B % 8 == 0 and tl.C % RG_TT == 0 and tl.L % RG_TT == 0
    ltot = tl.L + tl.C
    ntt, ct = ltot // RG_TT, tl.C // RG_TT
    R = RG_TT * B
    order = functools.partial(_rg_tile_order, ct=ct, ntt=ntt)
    nch = W // RG_CHUNK
    return pl.pallas_call(
        functools.partial(_rg_scan_kernel, B=B, ct=ct, ntt=ntt),
        grid=(2, ntt),
        in_specs=[pl.BlockSpec((R, W), lambda d, k: (order(d, k), 0)),
                  pl.BlockSpec((2 * B, W), lambda d, k: (jnp.maximum(order(d, k) * (RG_TT // 2) - 1, 0), 0)),
                  pl.BlockSpec((B, W), lambda d, k: (jnp.minimum((order(d, k) + 1) * RG_TT, ltot - 1), 0)),
                  pl.BlockSpec((RG_CONV_W, W), lambda d, k: (0, 0)),
                  pl.BlockSpec((1, W), lambda d, k: (0, 0)),
                  pl.BlockSpec((1, nch, RG_CHUNK, 2 * RG_CHUNK), lambda d, k: (d, 0, 0, 0)),
                  pl.BlockSpec((1, nch, 1, 2 * RG_CHUNK), lambda d, k: (d, 0, 0, 0)),
                  pl.BlockSpec((1, 1, W), lambda d, k: (d, 0, 0))],
        out_specs=pl.BlockSpec((1, R, W), lambda d, k: (d, order(d, k), 0)),
        out_shape=jax.ShapeDtypeStruct((2, ltot * B, W), F32),
        scratch_shapes=[pltpu.VMEM((3 * B + R, W), F32),
                        pltpu.VMEM((R, W), F32),
                        pltpu.VMEM((R, W), F32),
                        pltpu.VMEM((B, W), F32)],
        compiler_params=_cparams(("arbitrary", "arbitrary")),
        name="rg_scan",
    )(u_tm, u_tm, u_tm, conv_w, conv_b, wg, gb, lam)


def _rg_gate_weights(gate_w, gate_b):
    nb = gate_w.shape[2]
    per = RG_CHUNK // RG_BLOCK_W
    nch = nb // per
    gw = gate_w.reshape(2, 2, nch, per, RG_BLOCK_W, RG_BLOCK_W)
    eye = jnp.eye(per, dtype=gate_w.dtype)
    bd = jnp.einsum('dgcnij,nm->dgcnimj', gw, eye).reshape(2, 2, nch, RG_CHUNK, RG_CHUNK)
    wg = jnp.concatenate([bd[:, 0], bd[:, 1]], axis=-1).astype(BF16)
    gb = gate_b.reshape(2, 2, nch, 1, RG_CHUNK)
    gb = jnp.concatenate([gb[:, 0], gb[:, 1]], axis=-1)
    return wg, gb


def _mla_up_kernel(dn_ref, qn_ref, kvn_ref, wq_ref, wk_ref, we_ref, wv_ref, gq_ref, gk_ref, cos_ref, sin_ref,
                   q_ref, k_ref, v_ref):
    dn = dn_ref[...]
    cq = _rms(dn[:, :MLA_Q_RANK]) * qn_ref[...]
    ckv = _rms(dn[:, MLA_Q_RANK:MLA_Q_RANK + MLA_KV_RANK]) * kvn_ref[...]
    kr = dn[:, MLA_Q_RANK + MLA_KV_RANK:]
    kr_hi = kr.astype(BF16)
    kr_lo = (kr - kr_hi.astype(F32)).astype(BF16)
    ckvb = ckv.astype(BF16)
    q_pre = _dot(cq.astype(BF16), wq_ref[...])
    k_pre = _dot(ckvb, wk_ref[...]) + (_dot(kr_hi, we_ref[...]) + _dot(kr_lo, we_ref[...]))
    v_ref[...] = _dot(ckvb, wv_ref[...]).astype(v_ref.dtype)
    cos = cos_ref[...]
    sin = sin_ref[...]
    lane = lax.broadcasted_iota(jnp.int32, (TM, MLA_HP), 1)
    half = ROPE_AXIS_DIM // 2
    first_half = ((lane - MLA_NOPE) % ROPE_AXIS_DIM) < half
    scale = MLA_QK ** -0.5

    def head(x, gain):
        xn = _rms(x, MLA_QK) * gain
        partner = jnp.where(first_half, pltpu.roll(xn, MLA_HP - half, 1), pltpu.roll(xn, half, 1))
        return xn * cos + partner * sin

    for h in range(MLA_HEADS):
        sl = slice(h * MLA_HP, (h + 1) * MLA_HP)
        q_ref[:, sl] = (head(q_pre[:, sl], gq_ref[...]) * scale).astype(q_ref.dtype)
        k_ref[:, sl] = head(k_pre[:, sl], gk_ref[...]).astype(k_ref.dtype)


def _mla_up(tl, down, q_norm, kv_norm, wq, wk, we, wv, gq, gk, cos_t, sin_t):
    hw = MLA_HEADS * MLA_HP
    const = lambda i: (0, 0)
    rope_idx = lambda i: (jnp.where(i < tl.nlat, i % tl.lt, tl.lt), 0)
    return pl.pallas_call(
        _mla_up_kernel,
        grid=(tl.ntiles,),
        in_specs=[pl.BlockSpec((TM, 512), lambda i: (i, 0)),
                  pl.BlockSpec((1, MLA_Q_RANK), const),
                  pl.BlockSpec((1, MLA_KV_RANK), const),
                  pl.BlockSpec((MLA_Q_RANK, hw), const),
                  pl.BlockSpec((MLA_KV_RANK, hw), const),
                  pl.BlockSpec((128, hw), const),
                  pl.BlockSpec((MLA_KV_RANK, MLA_HEADS * MLA_V), const),
                  pl.BlockSpec((1, MLA_HP), const),
                  pl.BlockSpec((1, MLA_HP), const),
                  pl.BlockSpec((TM, MLA_HP), rope_idx),
                  pl.BlockSpec((TM, MLA_HP), rope_idx)],
        out_specs=[pl.BlockSpec((TM, hw), lambda i: (i, 0)),
                   pl.BlockSpec((TM, hw), lambda i: (i, 0)),
                   pl.BlockSpec((TM, MLA_HEADS * MLA_V), lambda i: (i, 0))],
        out_shape=[jax.ShapeDtypeStruct((tl.NT, hw), BF16),
                   jax.ShapeDtypeStruct((tl.NT, hw), BF16),
                   jax.ShapeDtypeStruct((tl.NT, MLA_HEADS * MLA_V), BF16)],
        compiler_params=_cparams(("arbitrary",)),
        name="mla_up_proj",
    )(down, q_norm, kv_norm, wq, wk, we, wv, gq, gk, cos_t, sin_t)


def _attn_kernel(*refs, n_kv):
    q_ref = refs[0]
    k_refs = refs[1:1 + n_kv]
    v_refs = refs[1 + n_kv:1 + 2 * n_kv]
    o_ref = refs[1 + 2 * n_kv]
    outs = []
    for hh in range(2):
        sl = slice(hh * MLA_HP, (hh + 1) * MLA_HP)
        q = q_ref[:, sl]
        s = [_dot_t(q, k_ref[:, sl]) for k_ref in k_refs]
        m = functools.reduce(jnp.maximum, [jnp.max(x, axis=-1, keepdims=True) for x in s])
        p = [jnp.exp(x - m) for x in s]
        l = functools.reduce(jnp.add, [jnp.sum(x, axis=-1, keepdims=True) for x in p])
        o = functools.reduce(jnp.add, [_dot(x.astype(BF16), v_ref[...]) for x, v_ref in zip(p, v_refs)])
        outs.append(o * (1.0 / l))
    lane = lax.broadcasted_iota(jnp.int32, outs[0].shape, 1)
    o_ref[...] = jnp.where(lane < MLA_V, outs[0], outs[1]).astype(o_ref.dtype)


def _attention(tl, q, k, v):
    B, L, C = tl.B, tl.L, tl.C
    hp2 = MLA_HEADS // 2
    tq = min(ATT_TQ, L)
    nq = L // tq
    ctx0 = (B * L) // C
    o_lat = pl.pallas_call(
        functools.partial(_attn_kernel, n_kv=2),
        grid=(B, hp2, nq),
        in_specs=[pl.BlockSpec((tq, 2 * MLA_HP), lambda b, h, i: (b * nq + i, h)),
                  pl.BlockSpec((C, 2 * MLA_HP), lambda b, h, i: (ctx0 + b, h)),
                  pl.BlockSpec((L, 2 * MLA_HP), lambda b, h, i: (b, h)),
                  pl.BlockSpec((C, 2 * MLA_V), lambda b, h, i: (ctx0 + b, h)),
                  pl.BlockSpec((L, 2 * MLA_V), lambda b, h, i: (b, h))],
        out_specs=pl.BlockSpec((tq, 2 * MLA_V), lambda b, h, i: (b * nq + i, h)),
        out_shape=jax.ShapeDtypeStruct((B * L, MLA_HEADS * MLA_V), BF16),
        compiler_params=_cparams(("arbitrary", "arbitrary", "arbitrary")),
        name="mla_attention_latent",
    )(q, k, k, v, v)
    o_ctx = pl.pallas_call(
        functools.partial(_attn_kernel, n_kv=1),
        grid=(B, hp2),
        in_specs=[pl.BlockSpec((C, 2 * MLA_HP), lambda b, h: (ctx0 + b, h)),
                  pl.BlockSpec((C, 2 * MLA_HP), lambda b, h: (ctx0 + b, h)),
                  pl.BlockSpec((C, 2 * MLA_V), lambda b, h: (ctx0 + b, h))],
        out_specs=pl.BlockSpec((C, 2 * MLA_V), lambda b, h: (b, h)),
        out_shape=jax.ShapeDtypeStruct((B * C, MLA_HEADS * MLA_V), BF16),
        compiler_params=_cparams(("arbitrary", "arbitrary")),
        name="mla_attention_context",
    )(q, k, v)
    return jnp.concatenate([o_lat, o_ctx], axis=0)


def _mla_weights(w_uq, w_ukv, qk_norm):
    H = MLA_HEADS
    wq = jnp.pad(w_uq.reshape(MLA_Q_RANK, H, MLA_QK), ((0, 0), (0, 0), (0, MLA_HP - MLA_QK)))
    wkv = w_ukv.reshape(MLA_KV_RANK, H, MLA_NOPE + MLA_V)
    wk = jnp.pad(wkv[:, :, :MLA_NOPE], ((0, 0), (0, 0), (0, MLA_HP - MLA_NOPE)))
    wv = wkv[:, :, MLA_NOPE:]
    r = jnp.arange(128)[:, None]
    col = jnp.arange(MLA_HP)[None, :]
    place = ((col == r + MLA_NOPE) & (r < MLA_ROPE)).astype(BF16)
    we = jnp.tile(place, (1, H))
    gq = jnp.pad(qk_norm[0], (0, MLA_HP - MLA_QK)).reshape(1, MLA_HP)
    gk = jnp.pad(qk_norm[1], (0, MLA_HP - MLA_QK)).reshape(1, MLA_HP)
    return (wq.reshape(MLA_Q_RANK, H * MLA_HP).astype(BF16), wk.reshape(MLA_KV_RANK, H * MLA_HP).astype(BF16),
            we, wv.reshape(MLA_KV_RANK, H * MLA_V).astype(BF16), gq, gk)


def _rope_tables(L):
    rows = L // GRID_W
    row = jnp.broadcast_to(jnp.arange(rows, dtype=F32)[:, None], (rows, GRID_W)).reshape(L)
    col = jnp.broadcast_to(jnp.arange(GRID_W, dtype=F32)[None, :], (rows, GRID_W)).reshape(L)
    inv_freq = ROPE_BASE ** (-jnp.arange(0, ROPE_AXIS_DIM, 2, dtype=F32) / ROPE_AXIS_DIM)
    ar = row[:, None] * inv_freq
    ac = col[:, None] * inv_freq
    ones = jnp.ones((L, MLA_NOPE), F32)
    zeros = jnp.zeros((L, MLA_NOPE), F32)
    pad1 = jnp.ones((L, MLA_HP - MLA_QK), F32)
    pad0 = jnp.zeros((L, MLA_HP - MLA_QK), F32)
    cos_t = jnp.concatenate([ones, jnp.cos(ar), jnp.cos(ar), jnp.cos(ac), jnp.cos(ac), pad1], axis=1)
    sin_t = jnp.concatenate([zeros, -jnp.sin(ar), jnp.sin(ar), -jnp.sin(ac), jnp.sin(ac), pad0], axis=1)
    cos_t = jnp.concatenate([cos_t, jnp.ones((TM, MLA_HP), F32)], axis=0)
    sin_t = jnp.concatenate([sin_t, jnp.zeros((TM, MLA_HP), F32)], axis=0)
    return cos_t, sin_t


def _log_sigmoid(x):
    return jnp.minimum(x, 0.0) - jnp.log1p(jnp.exp(-jnp.abs(x)))


def _mlstm_kernel(qf_ref, kf_ref, vf_ref, gf_ref, qb_ref, kb_ref, vb_ref, gb_ref, bias_ref,
                  of_ref, ob_ref, c_ref, n_ref, m_ref):
    T = ML_TC

    @pl.when(pl.program_id(1) == 0)
    def _():
        c_ref[...] = jnp.zeros_like(c_ref)
        n_ref[...] = jnp.zeros_like(n_ref)
        m_ref[...] = jnp.full(m_ref.shape, ML_M_INIT, F32)

    ti = lax.broadcasted_iota(jnp.int32, (T, T), 0)
    si = lax.broadcasted_iota(jnp.int32, (T, T), 1)
    dirs = ((qf_ref, kf_ref, vf_ref, gf_ref, of_ref), (qb_ref, kb_ref, vb_ref, gb_ref, ob_ref))
    for d, (q_ref, k_ref, v_ref, g_ref, o_ref) in enumerate(dirs):
        tri = (si <= ti) if d == 0 else (si >= ti)
        tri_t = (ti <= si) if d == 0 else (ti >= si)
        g = g_ref[...] + bias_ref[...]
        g_t = g.T
        for h in range(ML_HEADS):
            st = d * ML_HEADS + h
            li, lf_ = (2 * d) * ML_HEADS + h, (2 * d + 1) * ML_HEADS + h
            ig_col = g[:, li:li + 1]
            ig_row = g_t[li:li + 1, :]
            lf_col = _log_sigmoid(g[:, lf_:lf_ + 1])
            lf_row = _log_sigmoid(g_t[lf_:lf_ + 1, :])
            b_col = jnp.sum(jnp.where(tri, lf_row, 0.0), axis=1, keepdims=True)
            b_row = jnp.sum(jnp.where(tri_t, lf_col, 0.0), axis=0, keepdims=True)
            total = jnp.sum(lf_row, axis=1, keepdims=True)
            m_old = m_ref[st, 0:1, 0:1]
            d_log = jnp.where(tri, b_col - b_row + ig_row, -jnp.inf)
            inter_log = b_col + m_old
            m_t = jnp.maximum(inter_log, jnp.max(d_log, axis=1, keepdims=True))
            qh = q_ref[:, h * ML_DQK:(h + 1) * ML_DQK] * (ML_DQK ** -0.5)
            kh = k_ref[:, h * ML_DQK:(h + 1) * ML_DQK]
            vh = v_ref[:, h * ML_DV:(h + 1) * ML_DV].astype(BF16)
            qb16 = qh.astype(BF16)
            s_mat = _dot_t(qb16, kh.astype(BF16)) * jnp.exp(d_log - m_t)
            inter = jnp.exp(inter_log - m_t)
            c_old = c_ref[st]
            n_old = n_ref[st, 0:1, :]
            num = _dot(s_mat.astype(BF16), vh) + inter * _dot(qb16, c_old.astype(BF16))
            den = jnp.sum(s_mat, axis=1, keepdims=True) + inter * jnp.sum(qh * n_old, axis=1, keepdims=True)
            o_ref[0, :, h * ML_DV:(h + 1) * ML_DV] = num / jnp.maximum(jnp.abs(den), jnp.exp(-m_t))
            w_log = total - b_col + ig_col
            m_new = jnp.maximum(total + m_old, jnp.max(w_log, axis=0, keepdims=True))
            w = jnp.exp(w_log - m_new)
            decay = jnp.exp(total + m_old - m_new)
            kw = kh * w
            c_ref[st] = decay * c_old + _dot(kw.T.astype(BF16), vh)
            n_ref[st, 0:1, :] = decay * n_old + jnp.sum(kw, axis=0, keepdims=True)
            m_ref[st] = jnp.broadcast_to(m_new, m_ref.shape[1:])


def _mlstm(tl, proj, gate_bias):
    B, L, C = tl.B, tl.L, tl.C
    assert L % ML_TC == 0 and C % ML_TC == 0
    lc, cc = L // ML_TC, C // ML_TC
    nch = lc + cc
    ctx0 = (B * L) // ML_TC
    qw = ML_HEADS * ML_DQK
    vw = ML_HEADS * ML_DV
    gcol = (2 * qw + 2 * vw) // 128

    def rb(d, b, k):
        chunk = k if d == 0 else jnp.where(k < cc, cc - 1 - k, nch - 1 - (k - cc))
        return jnp.where(chunk < cc, ctx0 + b * cc + chunk, b * lc + (chunk - cc))

    def specs(d):
        return [pl.BlockSpec((ML_TC, qw), lambda b, k: (rb(d, b, k), 0)),
                pl.BlockSpec((ML_TC, qw), lambda b, k: (rb(d, b, k), 1)),
                pl.BlockSpec((ML_TC, vw), lambda b, k: (rb(d, b, k), (2 * qw) // vw)),
                pl.BlockSpec((ML_TC, 128), lambda b, k: (rb(d, b, k), gcol))]

    nst = 2 * ML_HEADS
    out = jax.ShapeDtypeStruct((1, tl.NT, vw), F32)
    hf, hb = pl.pallas_call(
        _mlstm_kernel,
        grid=(B, nch),
        in_specs=specs(0) + specs(1) + [pl.BlockSpec((1, 128), lambda b, k: (0, 0))],
        out_specs=[pl.BlockSpec((1, ML_TC, vw), lambda b, k: (0, rb(0, b, k), 0)),
                   pl.BlockSpec((1, ML_TC, vw), lambda b, k: (0, rb(1, b, k), 0))],
        out_shape=[out, out],
        scratch_shapes=[pltpu.VMEM((nst, ML_DQK, ML_DV), F32),
                        pltpu.VMEM((nst, 8, ML_DQK), F32),
                        pltpu.VMEM((nst, 8, 128), F32)],
        compiler_params=_cparams(("arbitrary", "arbitrary")),
        name="mlstm_chunks",
    )(proj, proj, proj, proj, proj, proj, proj, proj, gate_bias)
    return hf[0], hb[0]


def _route(lg):
    lane_i = lax.broadcasted_iota(jnp.int32, lg.shape, 1)
    lane = lane_i.astype(F32)
    neg = -jnp.inf
    gl = jnp.where(lane_i < MOE_GROUPS, lg, neg)
    gmax = jnp.max(gl, axis=-1, keepdims=True)
    gsum = jnp.sum(jnp.where(lane_i < MOE_GROUPS, jnp.exp(lg - gmax), 0.0), axis=-1, keepdims=True)
    p_top = 1.0 / gsum
    g_sel = jnp.min(jnp.where(gl == gmax, lane, 128.0), axis=-1, keepdims=True)
    group_of_lane = (lane_i >> 3).astype(F32) - 1.0
    el = jnp.where(group_of_lane == g_sel, lg, neg)
    e1 = jnp.max(el, axis=-1, keepdims=True)
    i1 = jnp.min(jnp.where(el == e1, lane, 128.0), axis=-1, keepdims=True)
    el2 = jnp.where(lane == i1, neg, el)
    e2 = jnp.max(el2, axis=-1, keepdims=True)
    i2 = jnp.min(jnp.where(el2 == e2, lane, 128.0), axis=-1, keepdims=True)
    t = jnp.exp(e2 - e1)
    w1 = p_top / (1.0 + t)
    w2 = w1 * t
    id1 = i1 - MOE_GROUPS
    id2 = i2 - MOE_GROUPS
    return jnp.where(lane_i == 0, id1, jnp.where(lane_i == 1, id2,
                                                 jnp.where(lane_i == 2, w1, jnp.where(lane_i == 3, w2, 0.0))))


def _lhs_rg(g_ref, hf_ref, hb_ref):
    return g_ref[...] * (hf_ref[0] + hb_ref[0])


def _lhs_mla(o_ref):
    return o_ref[...]


def _lhs_mlstm(hf_ref, hb_ref, og_ref, onorm_ref):
    hs = hf_ref[...] + hb_ref[...]
    og = _sigmoid(og_ref[...])
    parts = []
    for h in range(ML_HEADS):
        sl = slice(h * ML_DV, (h + 1) * ML_DV)
        parts.append(_rms(hs[:, sl]) * onorm_ref[:, sl] * og[:, sl])
    return jnp.concatenate(parts, axis=1)


def _mixer_out_kernel(*refs, n_lhs, lhs_fn):
    lhs_refs = refs[:n_lhs]
    w_ref, x_ref, mod_ref, gain_ref, wr_ref, br_ref, xo_ref, f_ref, r_ref = refs[n_lhs:]
    y = _dot(lhs_fn(*lhs_refs).astype(BF16), w_ref[...])
    mod = mod_ref[0]
    xn = x_ref[...] + mod[2:3, :] * y
    xo_ref[...] = xn
    f = _norm_mod(xn, gain_ref[...], mod, 3, 4)
    f_ref[...] = f.astype(f_ref.dtype)
    r_ref[...] = _route(_dot3(f, wr_ref[...]) + br_ref[...])


def _mixer_out(tl, ntiles, lhs_fn, lhs_args, lhs_specs, w_out, x, mod, gain, wr, br, name):
    n = ntiles * TM
    const = lambda i: (0, 0)
    return pl.pallas_call(
        functools.partial(_mixer_out_kernel, n_lhs=len(lhs_args), lhs_fn=lhs_fn),
        grid=(ntiles,),
        in_specs=list(lhs_specs) + [
            pl.BlockSpec((D_MODEL, D_MODEL), const),
            pl.BlockSpec((TM, D_MODEL), lambda i: (i, 0)),
            pl.BlockSpec((1, 8, D_MODEL), lambda i: (tl.mod(i), 0, 0)),
            pl.BlockSpec((1, D_MODEL), const),
            pl.BlockSpec((D_MODEL, 128), const),
            pl.BlockSpec((1, 128), const)],
        out_specs=[pl.BlockSpec((TM, D_MODEL), lambda i: (i, 0)),
                   pl.BlockSpec((TM, D_MODEL), lambda i: (i, 0)),
                   pl.BlockSpec((TM, 128), lambda i: (i, 0))],
        out_shape=[jax.ShapeDtypeStruct((n, D_MODEL), F32),
                   jax.ShapeDtypeStruct((n, D_MODEL), BF16),
                   jax.ShapeDtypeStruct((n, 128), F32)],
        compiler_params=_cparams(("arbitrary",)),
        name=name,
    )(*lhs_args, w_out, x, mod, gain, wr, br)


def _expert_kernel(be_ref, na_ref, x_ref, wgu_ref, wd_ref, ws_ref, o_ref):
    i = pl.program_id(0)

    @pl.when(i < na_ref[0])
    def _():
        gu = _dot(x_ref[...], wgu_ref[0].astype(BF16))
        gate = gu[:, :MOE_FF]
        act = gate * _sigmoid(gate) * gu[:, MOE_FF:]
        o_ref[...] = (_dot(act.astype(BF16), wd_ref[0].astype(BF16)) * ws_ref[...]).astype(o_ref.dtype)

    @pl.when(i >= na_ref[0])
    def _():
        o_ref[...] = jnp.zeros_like(o_ref)


def _expert_ffn(xs, w_slot, block_expert, n_active, w_gate_up, w_down):
    n_slots = xs.shape[0]
    nb = n_slots // MOE_BM
    grid_spec = pltpu.PrefetchScalarGridSpec(
        num_scalar_prefetch=2,
        grid=(nb,),
        in_specs=[pl.BlockSpec((MOE_BM, D_MODEL), lambda i, be, na: (i, 0)),
                  pl.BlockSpec((1, D_MODEL, 2 * MOE_FF), lambda i, be, na: (be[i], 0, 0)),
                  pl.BlockSpec((1, MOE_FF, D_MODEL), lambda i, be, na: (be[i], 0, 0)),
                  pl.BlockSpec((MOE_BM, 1), lambda i, be, na: (i, 0))],
        out_specs=pl.BlockSpec((MOE_BM, D_MODEL), lambda i, be, na: (i, 0)),
    )
    return pl.pallas_call(
        _expert_kernel,
        grid_spec=grid_spec,
        out_shape=jax.ShapeDtypeStruct((n_slots, D_MODEL), F32),
        compiler_params=_cparams(("arbitrary",)),
        name="moe_expert_ffn",
    )(block_expert, n_active, xs, w_gate_up, w_down, w_slot)


def _combine_kernel(x_ref, ya_ref, yb_ref, mod_ref, o_ref):
    o_ref[...] = x_ref[...] + mod_ref[0][5:6, :] * (ya_ref[...] + yb_ref[...])


def _combine(tl, ntiles, x, ya, yb, mod):
    spec = pl.BlockSpec((TM, D_MODEL), lambda i: (i, 0))
    return pl.pallas_call(
        _combine_kernel,
        grid=(ntiles,),
        in_specs=[spec, spec, spec, pl.BlockSpec((1, 8, D_MODEL), lambda i: (tl.mod(i), 0, 0))],
        out_specs=spec,
        out_shape=jax.ShapeDtypeStruct((ntiles * TM, D_MODEL), F32),
        compiler_params=_cparams(("arbitrary",)),
        name="moe_combine",
    )(x, ya, yb, mod)


def _moe(tl, ntiles, x, f, route, mod, w_gate_up, w_down):
    n = ntiles * TM
    nk = n * MOE_TOPK
    nb = -(-nk // MOE_BM) + MOE_EXPERTS
    e_flat = route[:, :MOE_TOPK].astype(jnp.int32).reshape(nk)
    w_flat = route[:, MOE_TOPK:2 * MOE_TOPK].reshape(nk)
    order = jnp.argsort(e_flat)
    e_sorted = e_flat[order]
    counts = jnp.zeros((MOE_EXPERTS,), jnp.int32).at[e_flat].add(1)
    starts = jnp.cumsum(counts) - counts
    padded = (counts + MOE_BM - 1) // MOE_BM * MOE_BM
    pad_end = jnp.cumsum(padded)
    pad_start = pad_end - padded
    dest = pad_start[e_sorted] + jnp.arange(nk, dtype=jnp.int32) - starts[e_sorted]
    slot_token = jnp.zeros((nb * MOE_BM,), jnp.int32).at[dest].set((order // MOE_TOPK).astype(jnp.int32))
    w_slot = jnp.zeros((nb * MOE_BM,), F32).at[dest].set(w_flat[order])
    slot_of = jnp.zeros((nk,), jnp.int32).at[order].set(dest)
    block_expert = jnp.minimum(
        jnp.searchsorted(pad_end, jnp.arange(nb, dtype=jnp.int32) * MOE_BM, side='right'),
        MOE_EXPERTS - 1).astype(jnp.int32)
    n_active = (pad_end[-1:] // MOE_BM).astype(jnp.int32)
    xs = jnp.take(f, slot_token, axis=0)
    y = _expert_ffn(xs, w_slot.reshape(-1, 1), block_expert, n_active, w_gate_up, w_down)
    slot_of = slot_of.reshape(n, MOE_TOPK)
    ya = jnp.take(y, slot_of[:, 0], axis=0)
    yb = jnp.take(y, slot_of[:, 1], axis=0)
    return _combine(tl, ntiles, x, ya, yb, mod)


def kernel(x, c, ctx, c_ctx, ada_w, ada_b, norm_mix, norm_ffn, rg_w_in, rg_conv_w, rg_conv_b, rg_gate_w, rg_gate_b, rg_lambda, rg_w_out, mla_w_down, mla_q_norm, mla_kv_norm, mla_w_uq, mla_w_ukv, mla_qk_norm, mla_w_o, ml_w_in, ml_gate_b, ml_out_norm, ml_w_out, moe_w_group, moe_b_group, moe_w_expert, moe_b_expert, moe_w_gate_up, moe_w_down):
    B, L, D = x.shape
    C = ctx.shape[1]
    depth = ada_w.shape[0]
    assert D == D_MODEL
    tl = _Tiles(B, L, C)
    ltot = L + C

    xs = jnp.concatenate([x.reshape(B * L, D), ctx.reshape(B * C, D)], axis=0)
    cc = jnp.zeros((16, D), F32).at[:B].set(c).at[B].set(c_ctx)
    mod_all = _modulation(cc, ada_w, ada_b)
    mod_all = jnp.pad(mod_all[:, :B + 1].reshape(depth, B + 1, 6, D), ((0, 0), (0, 0), (0, 2), (0, 0)))

    row = lambda a: a.reshape(1, -1)
    tile_spec = lambda w: pl.BlockSpec((TM, w), lambda i: (i, 0))

    for i in range(depth):
        last = i == depth - 1
        mod = mod_all[i]
        kind, j = i % 3, i // 3
        ntiles = tl.nlat if last else tl.ntiles
        wr = jnp.zeros((D, 128), F32).at[:, :MOE_GROUPS].set(moe_w_group[i]) \
            .at[:, MOE_GROUPS:MOE_GROUPS + MOE_EXPERTS].set(moe_w_expert[i])
        br = jnp.zeros((1, 128), F32).at[0, :MOE_GROUPS].set(moe_b_group[i]) \
            .at[0, MOE_GROUPS:MOE_GROUPS + MOE_EXPERTS].set(moe_b_expert[i])
        out_args = (xs, mod, row(norm_ffn[i]), wr, br)

        if kind == 0:
            gate, u = _rg_in(tl, xs, mod, row(norm_mix[i]), rg_w_in[j].astype(BF16))
            wg, gb = _rg_gate_weights(rg_gate_w[j], rg_gate_b[j])
            hs = _rg_scan(tl, u.reshape(ltot * B, D), rg_conv_w[j], row(rg_conv_b[j]), wg, gb,
                          rg_lambda[j].reshape(2, 1, D))
            hs = hs.reshape(2, ltot, B * D)
            h_spec = lambda d: pl.BlockSpec((1, TM, D), lambda t: (d, tl.ttile(t), tl.batch(t)))
            xs, f, route = _mixer_out(tl, ntiles, _lhs_rg, (gate, hs, hs), (tile_spec(D), h_spec(0), h_spec(1)),
                                      rg_w_out[j].astype(BF16), *out_args, name="rg_out")
        elif kind == 1:
            w_down = jnp.pad(mla_w_down[j], ((0, 0), (0, 512 - mla_w_down.shape[2]))).astype(BF16)
            down = _norm_proj(tl, xs, mod, row(norm_mix[i]), w_down, 512, name="mla_down_proj")
            wq, wk, we, wv, gq, gk = _mla_weights(mla_w_uq[j], mla_w_ukv[j], mla_qk_norm[j])
            cos_t, sin_t = _rope_tables(L)
            q, k, v = _mla_up(tl, down, row(mla_q_norm[j]), row(mla_kv_norm[j]), wq, wk, we, wv, gq, gk, cos_t, sin_t)
            o = _attention(tl, q, k, v)
            xs, f, route = _mixer_out(tl, ntiles, _lhs_mla, (o,), (tile_spec(D),),
                                      mla_w_o[j].astype(BF16), *out_args, name="mla_out")
        else:
            n_in = ml_w_in.shape[2]
            w_in = jnp.pad(ml_w_in[j], ((0, 0), (0, ML_NP - n_in))).astype(BF16)
            proj = _norm_proj(tl, xs, mod, row(norm_mix[i]), w_in, 640, name="mlstm_in_proj")
            gate_bias = jnp.pad(ml_gate_b[j].reshape(1, -1), ((0, 0), (0, 128 - 4 * ML_HEADS)))
            hf, hb = _mlstm(tl, proj, gate_bias)
            og_spec = pl.BlockSpec((TM, D), lambda t: (t, 2))
            xs, f, route = _mixer_out(tl, ntiles, _lhs_mlstm, (hf, hb, proj, row(ml_out_norm[j])),
                                      (tile_spec(D), tile_spec(D), og_spec, pl.BlockSpec((1, D), lambda t: (0, 0))),
                                      ml_w_out[j].astype(BF16), *out_args, name="mlstm_out")

        xs_new = _moe(tl, ntiles, xs, f, route, mod, moe_w_gate_up[i], moe_w_down[i])
        xs = xs_new

    return xs[:B * L].reshape(B, L, D)
```

```python
import functools
import math

import jax
import jax.numpy as jnp
from jax import lax
from jax.experimental import pallas as pl
from jax.experimental.pallas import tpu as pltpu

F32 = jnp.float32
BF16 = jnp.bfloat16

D_MODEL = 1024
RMS_EPS = 1e-6

TM = 256
VMEM_LIMIT = 48 * 1024 * 1024

RG_BLOCK_W = 64
RG_CHUNK = 256
RG_CONV_W = 4
RG_C = 8.0
RG_TT = 64

MLA_HEADS = 16
MLA_Q_RANK = 256
MLA_KV_RANK = 128
MLA_NOPE = 64
MLA_ROPE = 32
MLA_V = 64
MLA_QK = MLA_NOPE + MLA_ROPE
MLA_HP = 128
ROPE_AXIS_DIM = MLA_ROPE // 2
ROPE_BASE = 10000.0
GRID_W = 64
ATT_TQ = 512

ML_HEADS = 4
ML_DV = 256
ML_DQK = 128
ML_TC = 256
ML_M_INIT = -1e30
ML_NP = 3200

MOE_GROUPS = 8
MOE_PER_GROUP = 8
MOE_EXPERTS = 64
MOE_TOPK = 2
MOE_FF = 256
MOE_BM = 256


def _cparams(sem):
    return pltpu.CompilerParams(dimension_semantics=sem, vmem_limit_bytes=VMEM_LIMIT)


def _dot(a, b):
    return jnp.dot(a, b, preferred_element_type=F32)


def _dot_t(a, b):
    return lax.dot_general(a, b, (((1,), (1,)), ((), ())), preferred_element_type=F32)


def _dot3(a, b):
    ah = a.astype(BF16)
    al = (a - ah.astype(F32)).astype(BF16)
    bh = b.astype(BF16)
    bl = (b - bh.astype(F32)).astype(BF16)
    return _dot(ah, bh) + (_dot(al, bh) + _dot(ah, bl))


def _sigmoid(x):
    return 1.0 / (1.0 + jnp.exp(-x))


def _softplus(x):
    return jnp.maximum(x, 0.0) + jnp.log1p(jnp.exp(-jnp.abs(x)))


def _gelu_tanh(x):
    return 0.5 * x * (1.0 + jnp.tanh(0.7978845608028654 * (x + 0.044715 * (x * x * x))))


def _rms(x, n=None):
    n = x.shape[-1] if n is None else n
    ms = jnp.sum(x * x, axis=-1, keepdims=True) * (1.0 / n)
    return x * lax.rsqrt(ms + RMS_EPS)


class _Tiles:
    def __init__(self, B, L, C):
        assert L % TM == 0 and C % TM == 0
        self.B, self.L, self.C = B, L, C
        self.lt, self.ct = L // TM, C // TM
        self.nlat = B * self.lt
        self.ntiles = self.nlat + B * self.ct
        self.NT = self.ntiles * TM

    def batch(self, i):
        return jnp.where(i < self.nlat, i // self.lt, (i - self.nlat) // self.ct)

    def mod(self, i):
        return jnp.where(i < self.nlat, i // self.lt, self.B)

    def ttile(self, i):
        return jnp.where(i < self.nlat, self.ct + i % self.lt, (i - self.nlat) % self.ct)


def _mod_kernel(c_ref, w_ref, b_ref, o_ref):
    c = c_ref[...]
    o_ref[0] = _dot3(c * _sigmoid(c), w_ref[0]) + b_ref[0]


def _modulation(cc, ada_w, ada_b):
    depth, d, n = ada_w.shape
    tn = 1536
    return pl.pallas_call(
        _mod_kernel,
        grid=(depth, n // tn),
        in_specs=[pl.BlockSpec((16, d), lambda l, j: (0, 0)),
                  pl.BlockSpec((1, d, tn), lambda l, j: (l, 0, j)),
                  pl.BlockSpec((1, 1, tn), lambda l, j: (l, 0, j))],
        out_specs=pl.BlockSpec((1, 16, tn), lambda l, j: (l, 0, j)),
        out_shape=jax.ShapeDtypeStruct((depth, 16, n), F32),
        compiler_params=_cparams(("arbitrary", "arbitrary")),
        name="ada_modulation",
    )(cc, ada_w, ada_b.reshape(depth, 1, n))


def _norm_mod(x, gain, mod, shift_row, scale_row):
    return _rms(x) * gain * (1.0 + mod[scale_row:scale_row + 1, :]) + mod[shift_row:shift_row + 1, :]


def _norm_proj_kernel(x_ref, mod_ref, g_ref, w_ref, o_ref, hn_ref):
    @pl.when(pl.program_id(1) == 0)
    def _():
        hn_ref[...] = _norm_mod(x_ref[...], g_ref[...], mod_ref[0], 0, 1).astype(BF16)

    o_ref[...] = _dot(hn_ref[...], w_ref[...]).astype(o_ref.dtype)


def _norm_proj(tl, x, mod, gain, w, tn, out_dtype=F32, name="norm_proj"):
    n = w.shape[1]
    return pl.pallas_call(
        _norm_proj_kernel,
        grid=(tl.ntiles, n // tn),
        in_specs=[pl.BlockSpec((TM, D_MODEL), lambda i, j: (i, 0)),
                  pl.BlockSpec((1, 8, D_MODEL), lambda i, j: (tl.mod(i), 0, 0)),
                  pl.BlockSpec((1, D_MODEL), lambda i, j: (0, 0)),
                  pl.BlockSpec((D_MODEL, tn), lambda i, j: (0, j))],
        out_specs=pl.BlockSpec((TM, tn), lambda i, j: (i, j)),
        out_shape=jax.ShapeDtypeStruct((tl.NT, n), out_dtype),
        scratch_shapes=[pltpu.VMEM((TM, D_MODEL), BF16)],
        compiler_params=_cparams(("arbitrary", "arbitrary")),
        name=name,
    )(x, mod, gain, w)


def _rg_in_kernel(x_ref, mod_ref, g_ref, wg_ref, wu_ref, gate_ref, u_ref, hn_ref):
    @pl.when(pl.program_id(1) == 0)
    def _():
        hn_ref[...] = _norm_mod(x_ref[...], g_ref[...], mod_ref[0], 0, 1).astype(BF16)

    h = hn_ref[...]
    gate_ref[...] = _gelu_tanh(_dot(h, wg_ref[...]))
    u_ref[...] = _dot(h, wu_ref[...])


def _rg_in(tl, x, mod, gain, w_in):
    W = D_MODEL
    tn = 512
    nj = W // tn
    ltot = tl.L + tl.C
    return pl.pallas_call(
        _rg_in_kernel,
        grid=(tl.ntiles, nj),
        in_specs=[pl.BlockSpec((TM, D_MODEL), lambda i, j: (i, 0)),
                  pl.BlockSpec((1, 8, D_MODEL), lambda i, j: (tl.mod(i), 0, 0)),
                  pl.BlockSpec((1, D_MODEL), lambda i, j: (0, 0)),
                  pl.BlockSpec((D_MODEL, tn), lambda i, j: (0, j)),
                  pl.BlockSpec((D_MODEL, tn), lambda i, j: (0, nj + j))],
        out_specs=[pl.BlockSpec((TM, tn), lambda i, j: (i, j)),
                   pl.BlockSpec((TM, tn), lambda i, j: (tl.ttile(i), tl.batch(i) * nj + j))],
        out_shape=[jax.ShapeDtypeStruct((tl.NT, W), F32),
                   jax.ShapeDtypeStruct((ltot, tl.B * W), F32)],
        scratch_shapes=[pltpu.VMEM((TM, D_MODEL), BF16)],
        compiler_params=_cparams(("arbitrary", "arbitrary")),
        name="rg_in_proj",
    )(x, mod, gain, w_in, w_in)


def _rg_tile_order(d, k, ct, ntt):
    bwd = jnp.where(k < ct, ct - 1 - k, ntt - 1 - (k - ct))
    return jnp.where(d == 0, k, bwd)


def _rg_scan_kernel(um_ref, up_ref, un_ref, cw_ref, cb_ref, wg_ref, gb_ref, lam_ref, o_ref,
                    ext_ref, a_ref, b_ref, h_ref, *, B, ct, ntt):
    d = pl.program_id(0)
    k = pl.program_id(1)
    tile = _rg_tile_order(d, k, ct, ntt)
    R = RG_TT * B

    @pl.when(k == 0)
    def _():
        h_ref[...] = jnp.zeros_like(h_ref)

    seq_start = jnp.logical_or(tile == 0, tile == ct)
    seq_end = jnp.logical_or(tile == ct - 1, tile == ntt - 1)
    ext_ref[0:2 * B, :] = jnp.where(seq_start, 0.0, up_ref[...])
    ext_ref[2 * B:2 * B + R, :] = um_ref[...]
    ext_ref[2 * B + R:3 * B + R, :] = jnp.where(seq_end, 0.0, un_ref[...])
    cw = cw_ref[...]
    uc = cb_ref[...] + cw[0:1, :] * ext_ref[0:R, :]
    for j in range(1, RG_CONV_W):
        uc = uc + cw[j:j + 1, :] * ext_ref[j * B:j * B + R, :]
    ucb = uc.astype(BF16)
    c_lam = -RG_C * _softplus(-lam_ref[0])
    for c in range(D_MODEL // RG_CHUNK):
        sl = slice(c * RG_CHUNK, (c + 1) * RG_CHUNK)
        z = _dot(ucb[:, sl], wg_ref[0, c]) + gb_ref[0, c]
        r = _sigmoid(z[:, :RG_CHUNK])
        ig = _sigmoid(z[:, RG_CHUNK:])
        log_a = c_lam[:, sl] * r
        a = jnp.exp(log_a)
        one_minus_a2 = -jnp.tanh(log_a) * (a * a + 1.0)
        a_ref[:, sl] = a
        b_ref[:, sl] = jnp.sqrt(one_minus_a2) * (ig * uc[:, sl])

    def step(t, h):
        tt = jnp.where(d == 0, t, RG_TT - 1 - t)
        row = pl.multiple_of(tt * B, B)
        h = a_ref[pl.ds(row, B), :] * h + b_ref[pl.ds(row, B), :]
        o_ref[0, pl.ds(row, B), :] = h
        return h

    h_ref[...] = lax.fori_loop(0, RG_TT, step, h_ref[...], unroll=8)


def _rg_scan(tl, u_tm, conv_w, conv_b, wg, gb, lam):
    B, W = tl.B, D_MODEL
    assert B % 8 == 0 and tl.C % RG_TT == 0 and tl.L % RG_TT == 0
    ltot = tl.L + tl.C
    ntt, ct = ltot // RG_TT, tl.C // RG_TT
    R = RG_TT * B
    order = functools.partial(_rg_tile_order, ct=ct, ntt=ntt)
    nch = W // RG_CHUNK
    return pl.pallas_call(
        functools.partial(_rg_scan_kernel, B=B, ct=ct, ntt=ntt),
        grid=(2, ntt),
        in_specs=[pl.BlockSpec((R, W), lambda d, k: (order(d, k), 0)),
                  pl.BlockSpec((2 * B, W), lambda d, k: (jnp.maximum(order(d, k) * (RG_TT // 2) - 1, 0), 0)),
                  pl.BlockSpec((B, W), lambda d, k: (jnp.minimum((order(d, k) + 1) * RG_TT, ltot - 1), 0)),
                  pl.BlockSpec((RG_CONV_W, W), lambda d, k: (0, 0)),
                  pl.BlockSpec((1, W), lambda d, k: (0, 0)),
                  pl.BlockSpec((1, nch, RG_CHUNK, 2 * RG_CHUNK), lambda d, k: (d, 0, 0, 0)),
                  pl.BlockSpec((1, nch, 1, 2 * RG_CHUNK), lambda d, k: (d, 0, 0, 0)),
                  pl.BlockSpec((1, 1, W), lambda d, k: (d, 0, 0))],
        out_specs=pl.BlockSpec((1, R, W), lambda d, k: (d, order(d, k), 0)),
        out_shape=jax.ShapeDtypeStruct((2, ltot * B, W), F32),
        scratch_shapes=[pltpu.VMEM((3 * B + R, W), F32),
                        pltpu.VMEM((R, W), F32),
                        pltpu.VMEM((R, W), F32),
                        pltpu.VMEM((B, W), F32)],
        compiler_params=_cparams(("arbitrary", "arbitrary")),
        name="rg_scan",
    )(u_tm, u_tm, u_tm, conv_w, conv_b, wg, gb, lam)


def _rg_gate_weights(gate_w, gate_b):
    nb = gate_w.shape[2]
    per = RG_CHUNK // RG_BLOCK_W
    nch = nb // per
    gw = gate_w.reshape(2, 2, nch, per, RG_BLOCK_W, RG_BLOCK_W)
    eye = jnp.eye(per, dtype=gate_w.dtype)
    bd = jnp.einsum('dgcnij,nm->dgcnimj', gw, eye).reshape(2, 2, nch, RG_CHUNK, RG_CHUNK)
    wg = jnp.concatenate([bd[:, 0], bd[:, 1]], axis=-1).astype(BF16)
    gb = gate_b.reshape(2, 2, nch, 1, RG_CHUNK)
    gb = jnp.concatenate([gb[:, 0], gb[:, 1]], axis=-1)
    return wg, gb


def _mla_up_kernel(dn_ref, qn_ref, kvn_ref, wq_ref, wk_ref, we_ref, wv_ref, gq_ref, gk_ref, cos_ref, sin_ref,
                   q_ref, k_ref, v_ref):
    dn = dn_ref[...]
    cq = _rms(dn[:, :MLA_Q_RANK]) * qn_ref[...]
    ckv = _rms(dn[:, MLA_Q_RANK:MLA_Q_RANK + MLA_KV_RANK]) * kvn_ref[...]
    kr = dn[:, MLA_Q_RANK + MLA_KV_RANK:]
    kr_hi = kr.astype(BF16)
    kr_lo = (kr - kr_hi.astype(F32)).astype(BF16)
    ckvb = ckv.astype(BF16)
    q_pre = _dot(cq.astype(BF16), wq_ref[...])
    k_pre = _dot(ckvb, wk_ref[...]) + (_dot(kr_hi, we_ref[...]) + _dot(kr_lo, we_ref[...]))
    v_ref[...] = _dot(ckvb, wv_ref[...]).astype(v_ref.dtype)
    cos = cos_ref[...]
    sin = sin_ref[...]
    lane = lax.broadcasted_iota(jnp.int32, (TM, MLA_HP), 1)
    half = ROPE_AXIS_DIM // 2
    first_half = ((lane - MLA_NOPE) % ROPE_AXIS_DIM) < half
    scale = MLA_QK ** -0.5

    def head(x, gain):
        xn = _rms(x, MLA_QK) * gain
        partner = jnp.where(first_half, pltpu.roll(xn, MLA_HP - half, 1), pltpu.roll(xn, half, 1))
        return xn * cos + partner * sin

    for h in range(MLA_HEADS):
        sl = slice(h * MLA_HP, (h + 1) * MLA_HP)
        q_ref[:, sl] = (head(q_pre[:, sl], gq_ref[...]) * scale).astype(q_ref.dtype)
        k_ref[:, sl] = head(k_pre[:, sl], gk_ref[...]).astype(k_ref.dtype)


def _mla_up(tl, down, q_norm, kv_norm, wq, wk, we, wv, gq, gk, cos_t, sin_t):
    hw = MLA_HEADS * MLA_HP
    const = lambda i: (0, 0)
    rope_idx = lambda i: (jnp.where(i < tl.nlat, i % tl.lt, tl.lt), 0)
    return pl.pallas_call(
        _mla_up_kernel,
        grid=(tl.ntiles,),
        in_specs=[pl.BlockSpec((TM, 512), lambda i: (i, 0)),
                  pl.BlockSpec((1, MLA_Q_RANK), const),
                  pl.BlockSpec((1, MLA_KV_RANK), const),
                  pl.BlockSpec((MLA_Q_RANK, hw), const),
                  pl.BlockSpec((MLA_KV_RANK, hw), const),
                  pl.BlockSpec((128, hw), const),
                  pl.BlockSpec((MLA_KV_RANK, MLA_HEADS * MLA_V), const),
                  pl.BlockSpec((1, MLA_HP), const),
                  pl.BlockSpec((1, MLA_HP), const),
                  pl.BlockSpec((TM, MLA_HP), rope_idx),
                  pl.BlockSpec((TM, MLA_HP), rope_idx)],
        out_specs=[pl.BlockSpec((TM, hw), lambda i: (i, 0)),
                   pl.BlockSpec((TM, hw), lambda i: (i, 0)),
                   pl.BlockSpec((TM, MLA_HEADS * MLA_V), lambda i: (i, 0))],
        out_shape=[jax.ShapeDtypeStruct((tl.NT, hw), BF16),
                   jax.ShapeDtypeStruct((tl.NT, hw), BF16),
                   jax.ShapeDtypeStruct((tl.NT, MLA_HEADS * MLA_V), BF16)],
        compiler_params=_cparams(("arbitrary",)),
        name="mla_up_proj",
    )(down, q_norm, kv_norm, wq, wk, we, wv, gq, gk, cos_t, sin_t)


def _attn_kernel(*refs, n_kv):
    q_ref = refs[0]
    k_refs = refs[1:1 + n_kv]
    v_refs = refs[1 + n_kv:1 + 2 * n_kv]
    o_ref = refs[1 + 2 * n_kv]
    outs = []
    for hh in range(2):
        sl = slice(hh * MLA_HP, (hh + 1) * MLA_HP)
        q = q_ref[:, sl]
        s = [_dot_t(q, k_ref[:, sl]) for k_ref in k_refs]
        m = functools.reduce(jnp.maximum, [jnp.max(x, axis=-1, keepdims=True) for x in s])
        p = [jnp.exp(x - m) for x in s]
        l = functools.reduce(jnp.add, [jnp.sum(x, axis=-1, keepdims=True) for x in p])
        o = functools.reduce(jnp.add, [_dot(x.astype(BF16), v_ref[...]) for x, v_ref in zip(p, v_refs)])
        outs.append(o * (1.0 / l))
    lane = lax.broadcasted_iota(jnp.int32, outs[0].shape, 1)
    o_ref[...] = jnp.where(lane < MLA_V, outs[0], outs[1]).astype(o_ref.dtype)


def _attention(tl, q, k, v):
    B, L, C = tl.B, tl.L, tl.C
    hp2 = MLA_HEADS // 2
    tq = min(ATT_TQ, L)
    nq = L // tq
    ctx0 = (B * L) // C
    o_lat = pl.pallas_call(
        functools.partial(_attn_kernel, n_kv=2),
        grid=(B, hp2, nq),
        in_specs=[pl.BlockSpec((tq, 2 * MLA_HP), lambda b, h, i: (b * nq + i, h)),
                  pl.BlockSpec((C, 2 * MLA_HP), lambda b, h, i: (ctx0 + b, h)),
                  pl.BlockSpec((L, 2 * MLA_HP), lambda b, h, i: (b, h)),
                  pl.BlockSpec((C, 2 * MLA_V), lambda b, h, i: (ctx0 + b, h)),
                  pl.BlockSpec((L, 2 * MLA_V), lambda b, h, i: (b, h))],
        out_specs=pl.BlockSpec((tq, 2 * MLA_V), lambda b, h, i: (b * nq + i, h)),
        out_shape=jax.ShapeDtypeStruct((B * L, MLA_HEADS * MLA_V), BF16),
        compiler_params=_cparams(("arbitrary", "arbitrary", "arbitrary")),
        name="mla_attention_latent",
    )(q, k, k, v, v)
    o_ctx = pl.pallas_call(
        functools.partial(_attn_kernel, n_kv=1),
        grid=(B, hp2),
        in_specs=[pl.BlockSpec((C, 2 * MLA_HP), lambda b, h: (ctx0 + b, h)),
                  pl.BlockSpec((C, 2 * MLA_HP), lambda b, h: (ctx0 + b, h)),
                  pl.BlockSpec((C, 2 * MLA_V), lambda b, h: (ctx0 + b, h))],
        out_specs=pl.BlockSpec((C, 2 * MLA_V), lambda b, h: (b, h)),
        out_shape=jax.ShapeDtypeStruct((B * C, MLA_HEADS * MLA_V), BF16),
        compiler_params=_cparams(("arbitrary", "arbitrary")),
        name="mla_attention_context",
    )(q, k, v)
    return jnp.concatenate([o_lat, o_ctx], axis=0)


def _mla_weights(w_uq, w_ukv, qk_norm):
    H = MLA_HEADS
    wq = jnp.pad(w_uq.reshape(MLA_Q_RANK, H, MLA_QK), ((0, 0), (0, 0), (0, MLA_HP - MLA_QK)))
    wkv = w_ukv.reshape(MLA_KV_RANK, H, MLA_NOPE + MLA_V)
    wk = jnp.pad(wkv[:, :, :MLA_NOPE], ((0, 0), (0, 0), (0, MLA_HP - MLA_NOPE)))
    wv = wkv[:, :, MLA_NOPE:]
    r = jnp.arange(128)[:, None]
    col = jnp.arange(MLA_HP)[None, :]
    place = ((col == r + MLA_NOPE) & (r < MLA_ROPE)).astype(BF16)
    we = jnp.tile(place, (1, H))
    gq = jnp.pad(qk_norm[0], (0, MLA_HP - MLA_QK)).reshape(1, MLA_HP)
    gk = jnp.pad(qk_norm[1], (0, MLA_HP - MLA_QK)).reshape(1, MLA_HP)
    return (wq.reshape(MLA_Q_RANK, H * MLA_HP).astype(BF16), wk.reshape(MLA_KV_RANK, H * MLA_HP).astype(BF16),
            we, wv.reshape(MLA_KV_RANK, H * MLA_V).astype(BF16), gq, gk)


def _rope_tables(L):
    rows = L // GRID_W
    row = jnp.broadcast_to(jnp.arange(rows, dtype=F32)[:, None], (rows, GRID_W)).reshape(L)
    col = jnp.broadcast_to(jnp.arange(GRID_W, dtype=F32)[None, :], (rows, GRID_W)).reshape(L)
    inv_freq = ROPE_BASE ** (-jnp.arange(0, ROPE_AXIS_DIM, 2, dtype=F32) / ROPE_AXIS_DIM)
    ar = row[:, None] * inv_freq
    ac = col[:, None] * inv_freq
    ones = jnp.ones((L, MLA_NOPE), F32)
    zeros = jnp.zeros((L, MLA_NOPE), F32)
    pad1 = jnp.ones((L, MLA_HP - MLA_QK), F32)
    pad0 = jnp.zeros((L, MLA_HP - MLA_QK), F32)
    cos_t = jnp.concatenate([ones, jnp.cos(ar), jnp.cos(ar), jnp.cos(ac), jnp.cos(ac), pad1], axis=1)
    sin_t = jnp.concatenate([zeros, -jnp.sin(ar), jnp.sin(ar), -jnp.sin(ac), jnp.sin(ac), pad0], axis=1)
    cos_t = jnp.concatenate([cos_t, jnp.ones((TM, MLA_HP), F32)], axis=0)
    sin_t = jnp.concatenate([sin_t, jnp.zeros((TM, MLA_HP), F32)], axis=0)
    return cos_t, sin_t


def _log_sigmoid(x):
    return jnp.minimum(x, 0.0) - jnp.log1p(jnp.exp(-jnp.abs(x)))


def _mlstm_kernel(qf_ref, kf_ref, vf_ref, gf_ref, qb_ref, kb_ref, vb_ref, gb_ref, bias_ref,
                  of_ref, ob_ref, c_ref, n_ref, m_ref):
    T = ML_TC

    @pl.when(pl.program_id(1) == 0)
    def _():
        c_ref[...] = jnp.zeros_like(c_ref)
        n_ref[...] = jnp.zeros_like(n_ref)
        m_ref[...] = jnp.full(m_ref.shape, ML_M_INIT, F32)

    ti = lax.broadcasted_iota(jnp.int32, (T, T), 0)
    si = lax.broadcasted_iota(jnp.int32, (T, T), 1)
    dirs = ((qf_ref, kf_ref, vf_ref, gf_ref, of_ref), (qb_ref, kb_ref, vb_ref, gb_ref, ob_ref))
    for d, (q_ref, k_ref, v_ref, g_ref, o_ref) in enumerate(dirs):
        tri = (si <= ti) if d == 0 else (si >= ti)
        tri_t = (ti <= si) if d == 0 else (ti >= si)
        g = g_ref[...] + bias_ref[...]
        g_t = g.T
        for h in range(ML_HEADS):
            st = d * ML_HEADS + h
            li, lf_ = (2 * d) * ML_HEADS + h, (2 * d + 1) * ML_HEADS + h
            ig_col = g[:, li:li + 1]
            ig_row = g_t[li:li + 1, :]
            lf_col = _log_sigmoid(g[:, lf_:lf_ + 1])
            lf_row = _log_sigmoid(g_t[lf_:lf_ + 1, :])
            b_col = jnp.sum(jnp.where(tri, lf_row, 0.0), axis=1, keepdims=True)
            b_row = jnp.sum(jnp.where(tri_t, lf_col, 0.0), axis=0, keepdims=True)
            total = jnp.sum(lf_row, axis=1, keepdims=True)
            m_old = m_ref[st, 0:1, 0:1]
            d_log = jnp.where(tri, b_col - b_row + ig_row, -jnp.inf)
            inter_log = b_col + m_old
            m_t = jnp.maximum(inter_log, jnp.max(d_log, axis=1, keepdims=True))
            qh = q_ref[:, h * ML_DQK:(h + 1) * ML_DQK] * (ML_DQK ** -0.5)
            kh = k_ref[:, h * ML_DQK:(h + 1) * ML_DQK]
            vh = v_ref[:, h * ML_DV:(h + 1) * ML_DV].astype(BF16)
            qb16 = qh.astype(BF16)
            s_mat = _dot_t(qb16, kh.astype(BF16)) * jnp.exp(d_log - m_t)
            inter = jnp.exp(inter_log - m_t)
            c_old = c_ref[st]
            n_old = n_ref[st, 0:1, :]
            num = _dot(s_mat.astype(BF16), vh) + inter * _dot(qb16, c_old.astype(BF16))
            den = jnp.sum(s_mat, axis=1, keepdims=True) + inter * jnp.sum(qh * n_old, axis=1, keepdims=True)
            o_ref[0, :, h * ML_DV:(h + 1) * ML_DV] = num / jnp.maximum(jnp.abs(den), jnp.exp(-m_t))
            w_log = total - b_col + ig_col
            m_new = jnp.maximum(total + m_old, jnp.max(w_log, axis=0, keepdims=True))
            w = jnp.exp(w_log - m_new)
            decay = jnp.exp(total + m_old - m_new)
            kw = kh * w
            c_ref[st] = decay * c_old + _dot(kw.T.astype(BF16), vh)
            n_ref[st, 0:1, :] = decay * n_old + jnp.sum(kw, axis=0, keepdims=True)
            m_ref[st] = jnp.broadcast_to(m_new, m_ref.shape[1:])


def _mlstm(tl, proj, gate_bias):
    B, L, C = tl.B, tl.L, tl.C
    assert L % ML_TC == 0 and C % ML_TC == 0
    lc, cc = L // ML_TC, C // ML_TC
    nch = lc + cc
    ctx0 = (B * L) // ML_TC
    qw = ML_HEADS * ML_DQK
    vw = ML_HEADS * ML_DV
    gcol = (2 * qw + 2 * vw) // 128

    def rb(d, b, k):
        chunk = k if d == 0 else jnp.where(k < cc, cc - 1 - k, nch - 1 - (k - cc))
        return jnp.where(chunk < cc, ctx0 + b * cc + chunk, b * lc + (chunk - cc))

    def specs(d):
        return [pl.BlockSpec((ML_TC, qw), lambda b, k: (rb(d, b, k), 0)),
                pl.BlockSpec((ML_TC, qw), lambda b, k: (rb(d, b, k), 1)),
                pl.BlockSpec((ML_TC, vw), lambda b, k: (rb(d, b, k), (2 * qw) // vw)),
                pl.BlockSpec((ML_TC, 128), lambda b, k: (rb(d, b, k), gcol))]

    nst = 2 * ML_HEADS
    out = jax.ShapeDtypeStruct((1, tl.NT, vw), F32)
    hf, hb = pl.pallas_call(
        _mlstm_kernel,
        grid=(B, nch),
        in_specs=specs(0) + specs(1) + [pl.BlockSpec((1, 128), lambda b, k: (0, 0))],
        out_specs=[pl.BlockSpec((1, ML_TC, vw), lambda b, k: (0, rb(0, b, k), 0)),
                   pl.BlockSpec((1, ML_TC, vw), lambda b, k: (0, rb(1, b, k), 0))],
        out_shape=[out, out],
        scratch_shapes=[pltpu.VMEM((nst, ML_DQK, ML_DV), F32),
                        pltpu.VMEM((nst, 8, ML_DQK), F32),
                        pltpu.VMEM((nst, 8, 128), F32)],
        compiler_params=_cparams(("arbitrary", "arbitrary")),
        name="mlstm_chunks",
    )(proj, proj, proj, proj, proj, proj, proj, proj, gate_bias)
    return hf[0], hb[0]


def _route(lg):
    lane_i = lax.broadcasted_iota(jnp.int32, lg.shape, 1)
    lane = lane_i.astype(F32)
    neg = -jnp.inf
    gl = jnp.where(lane_i < MOE_GROUPS, lg, neg)
    gmax = jnp.max(gl, axis=-1, keepdims=True)
    gsum = jnp.sum(jnp.where(lane_i < MOE_GROUPS, jnp.exp(lg - gmax), 0.0), axis=-1, keepdims=True)
    p_top = 1.0 / gsum
    g_sel = jnp.min(jnp.where(gl == gmax, lane, 128.0), axis=-1, keepdims=True)
    group_of_lane = (lane_i >> 3).astype(F32) - 1.0
    el = jnp.where(group_of_lane == g_sel, lg, neg)
    e1 = jnp.max(el, axis=-1, keepdims=True)
    i1 = jnp.min(jnp.where(el == e1, lane, 128.0), axis=-1, keepdims=True)
    el2 = jnp.where(lane == i1, neg, el)
    e2 = jnp.max(el2, axis=-1, keepdims=True)
    i2 = jnp.min(jnp.where(el2 == e2, lane, 128.0), axis=-1, keepdims=True)
    t = jnp.exp(e2 - e1)
    w1 = p_top / (1.0 + t)
    w2 = w1 * t
    id1 = i1 - MOE_GROUPS
    id2 = i2 - MOE_GROUPS
    return jnp.where(lane_i == 0, id1, jnp.where(lane_i == 1, id2,
                                                 jnp.where(lane_i == 2, w1, jnp.where(lane_i == 3, w2, 0.0))))


def _lhs_rg(g_ref, hf_ref, hb_ref):
    return g_ref[...] * (hf_ref[0] + hb_ref[0])


def _lhs_mla(o_ref):
    return o_ref[...]


def _lhs_mlstm(hf_ref, hb_ref, og_ref, onorm_ref):
    hs = hf_ref[...] + hb_ref[...]
    og = _sigmoid(og_ref[...])
    parts = []
    for h in range(ML_HEADS):
        sl = slice(h * ML_DV, (h + 1) * ML_DV)
        parts.append(_rms(hs[:, sl]) * onorm_ref[:, sl] * og[:, sl])
    return jnp.concatenate(parts, axis=1)


def _mixer_out_kernel(*refs, n_lhs, lhs_fn):
    lhs_refs = refs[:n_lhs]
    w_ref, x_ref, mod_ref, gain_ref, wr_ref, br_ref, xo_ref, f_ref, r_ref = refs[n_lhs:]
    y = _dot(lhs_fn(*lhs_refs).astype(BF16), w_ref[...])
    mod = mod_ref[0]
    xn = x_ref[...] + mod[2:3, :] * y
    xo_ref[...] = xn
    f = _norm_mod(xn, gain_ref[...], mod, 3, 4)
    f_ref[...] = f.astype(f_ref.dtype)
    r_ref[...] = _route(_dot3(f, wr_ref[...]) + br_ref[...])


def _mixer_out(tl, ntiles, lhs_fn, lhs_args, lhs_specs, w_out, x, mod, gain, wr, br, name):
    n = ntiles * TM
    const = lambda i: (0, 0)
    return pl.pallas_call(
        functools.partial(_mixer_out_kernel, n_lhs=len(lhs_args), lhs_fn=lhs_fn),
        grid=(ntiles,),
        in_specs=list(lhs_specs) + [
            pl.BlockSpec((D_MODEL, D_MODEL), const),
            pl.BlockSpec((TM, D_MODEL), lambda i: (i, 0)),
            pl.BlockSpec((1, 8, D_MODEL), lambda i: (tl.mod(i), 0, 0)),
            pl.BlockSpec((1, D_MODEL), const),
            pl.BlockSpec((D_MODEL, 128), const),
            pl.BlockSpec((1, 128), const)],
        out_specs=[pl.BlockSpec((TM, D_MODEL), lambda i: (i, 0)),
                   pl.BlockSpec((TM, D_MODEL), lambda i: (i, 0)),
                   pl.BlockSpec((TM, 128), lambda i: (i, 0))],
        out_shape=[jax.ShapeDtypeStruct((n, D_MODEL), F32),
                   jax.ShapeDtypeStruct((n, D_MODEL), F32),
                   jax.ShapeDtypeStruct((n, 128), F32)],
        compiler_params=_cparams(("arbitrary",)),
        name=name,
    )(*lhs_args, w_out, x, mod, gain, wr, br)


def _expert_kernel(vb_ref, ve_ref, lo_ref, hi_ref, first_ref, last_ref, tab_ref, f_hbm, wgu_ref, wd_ref,
                   y_hbm, xbuf, acc, gsem, ssem):
    v = pl.program_id(0)

    def gather_row(r):
        tok = lax.div(tab_ref[0, 0, r], MOE_TOPK)
        return pltpu.make_async_copy(f_hbm.at[pl.ds(tok, 1), :], xbuf.at[pl.ds(r, 1), :], gsem)

    def scatter_row(r):
        return pltpu.make_async_copy(acc.at[pl.ds(r, 1), :], y_hbm.at[pl.ds(tab_ref[0, 0, r], 1), :], ssem)

    @pl.when(first_ref[v] == 1)
    def _():
        def issue(r, carry):
            gather_row(r).start()
            return carry
        lax.fori_loop(0, MOE_BM, issue, 0, unroll=8)
        acc[...] = jnp.zeros_like(acc)
        pltpu.make_async_copy(f_hbm.at[pl.ds(0, MOE_BM), :], xbuf, gsem).wait()

    @pl.when(hi_ref[v] > lo_ref[v])
    def _():
        x = xbuf[...].astype(BF16)
        gu = _dot(x, wgu_ref[0].astype(BF16))
        gate = gu[:, :MOE_FF]
        act = gate * _sigmoid(gate) * gu[:, MOE_FF:]
        y = _dot(act.astype(BF16), wd_ref[0].astype(BF16))
        r = lax.broadcasted_iota(jnp.int32, (MOE_BM, 1), 0)
        mine = jnp.logical_and(r >= lo_ref[v], r < hi_ref[v])
        acc[...] += jnp.where(mine, y, 0.0)

    @pl.when(last_ref[v] == 1)
    def _():
        def issue(r, carry):
            scatter_row(r).start()
            return carry
        lax.fori_loop(0, MOE_BM, issue, 0, unroll=8)
        pltpu.make_async_copy(acc, y_hbm.at[pl.ds(0, MOE_BM), :], ssem).wait()


def _expert_ffn(f, table, visits, w_gate_up, w_down):
    nk = table.shape[0] * MOE_BM
    nvis = visits[0].shape[0]
    idx = lambda which: (lambda v, vb, ve, lo, hi, fi, la: ((vb, ve)[which][v], 0, 0))
    grid_spec = pltpu.PrefetchScalarGridSpec(
        num_scalar_prefetch=6,
        grid=(nvis,),
        in_specs=[pl.BlockSpec((1, 1, MOE_BM), idx(0), memory_space=pltpu.SMEM),
                  pl.BlockSpec(memory_space=pl.ANY),
                  pl.BlockSpec((1, D_MODEL, 2 * MOE_FF), idx(1)),
                  pl.BlockSpec((1, MOE_FF, D_MODEL), idx(1))],
        out_specs=pl.BlockSpec(memory_space=pl.ANY),
        scratch_shapes=[pltpu.VMEM((MOE_BM, D_MODEL), F32),
                        pltpu.VMEM((MOE_BM, D_MODEL), F32),
                        pltpu.SemaphoreType.DMA(()),
                        pltpu.SemaphoreType.DMA(())],
    )
    return pl.pallas_call(
        _expert_kernel,
        grid_spec=grid_spec,
        out_shape=jax.ShapeDtypeStruct((nk, D_MODEL), F32),
        compiler_params=_cparams(("arbitrary",)),
        name="moe_expert_ffn",
    )(*visits, table, f, w_gate_up, w_down)


def _combine_kernel(x_ref, y_ref, r_ref, mod_ref, o_ref):
    w = r_ref[...]
    y = w[:, MOE_TOPK:MOE_TOPK + 1] * y_ref[:, :D_MODEL]
    for k in range(1, MOE_TOPK):
        y = y + w[:, MOE_TOPK + k:MOE_TOPK + k + 1] * y_ref[:, k * D_MODEL:(k + 1) * D_MODEL]
    o_ref[...] = x_ref[...] + mod_ref[0][5:6, :] * y


def _combine(tl, ntiles, x, y2, route, mod):
    spec = pl.BlockSpec((TM, D_MODEL), lambda i: (i, 0))
    return pl.pallas_call(
        _combine_kernel,
        grid=(ntiles,),
        in_specs=[spec, pl.BlockSpec((TM, MOE_TOPK * D_MODEL), lambda i: (i, 0)),
                  pl.BlockSpec((TM, 128), lambda i: (i, 0)),
                  pl.BlockSpec((1, 8, D_MODEL), lambda i: (tl.mod(i), 0, 0))],
        out_specs=spec,
        out_shape=jax.ShapeDtypeStruct((ntiles * TM, D_MODEL), F32),
        compiler_params=_cparams(("arbitrary",)),
        name="moe_combine",
    )(x, y2, route, mod)


def _visit_tables(e_sorted, nk):
    E = MOE_EXPERTS
    nblk = nk // MOE_BM
    nvis = nblk + E
    bounds = jnp.searchsorted(e_sorted, jnp.arange(E + 1, dtype=jnp.int32), side='left').astype(jnp.int32)
    starts, ends = bounds[:-1], bounds[1:]
    fb = starts // MOE_BM
    nv = jnp.where(ends > starts, (ends - 1) // MOE_BM - fb + 1, 0)
    cum = jnp.cumsum(nv)
    total = cum[-1]
    v = jnp.arange(nvis, dtype=jnp.int32)
    active = v < total
    vc = jnp.minimum(v, total - 1)
    ve = jnp.minimum(jnp.searchsorted(cum, vc, side='right'), E - 1).astype(jnp.int32)
    vb = fb[ve] + (vc - (cum - nv)[ve])
    lo = jnp.where(active, jnp.maximum(starts[ve], vb * MOE_BM) - vb * MOE_BM, 0)
    hi = jnp.where(active, jnp.minimum(ends[ve], (vb + 1) * MOE_BM) - vb * MOE_BM, 0)
    prev_b = jnp.concatenate([jnp.full((1,), -1, jnp.int32), vb[:-1]])
    next_b = jnp.concatenate([vb[1:], jnp.full((1,), -1, jnp.int32)])
    first = jnp.logical_and(active, vb != prev_b)
    last = jnp.logical_and(active, jnp.logical_or(vb != next_b, v == total - 1))
    i32 = lambda a: a.astype(jnp.int32)
    return i32(vb), i32(ve), i32(lo), i32(hi), i32(first), i32(last)


def _moe(tl, ntiles, x, f, route, mod, w_gate_up, w_down):
    n = ntiles * TM
    nk = n * MOE_TOPK
    e_flat = route[:, :MOE_TOPK].astype(jnp.int32).reshape(nk)
    e_sorted, order = lax.sort((e_flat, jnp.arange(nk, dtype=jnp.int32)), num_keys=1, is_stable=True)
    visits = _visit_tables(e_sorted, nk)
    y = _expert_ffn(f, order.reshape(nk // MOE_BM, 1, MOE_BM), visits, w_gate_up, w_down)
    return _combine(tl, ntiles, x, y.reshape(n, MOE_TOPK * D_MODEL), route, mod)


def kernel(x, c, ctx, c_ctx, ada_w, ada_b, norm_mix, norm_ffn, rg_w_in, rg_conv_w, rg_conv_b, rg_gate_w, rg_gate_b, rg_lambda, rg_w_out, mla_w_down, mla_q_norm, mla_kv_norm, mla_w_uq, mla_w_ukv, mla_qk_norm, mla_w_o, ml_w_in, ml_gate_b, ml_out_norm, ml_w_out, moe_w_group, moe_b_group, moe_w_expert, moe_b_expert, moe_w_gate_up, moe_w_down):
    B, L, D = x.shape
    C = ctx.shape[1]
    depth = ada_w.shape[0]
    assert D == D_MODEL
    tl = _Tiles(B, L, C)
    ltot = L + C

    xs = jnp.concatenate([x.reshape(B * L, D), ctx.reshape(B * C, D)], axis=0)
    cc = jnp.zeros((16, D), F32).at[:B].set(c).at[B].set(c_ctx)
    mod_all = _modulation(cc, ada_w, ada_b)
    mod_all = jnp.pad(mod_all[:, :B + 1].reshape(depth, B + 1, 6, D), ((0, 0), (0, 0), (0, 2), (0, 0)))

    row = lambda a: a.reshape(1, -1)
    tile_spec = lambda w: pl.BlockSpec((TM, w), lambda i: (i, 0))

    for i in range(depth):
        last = i == depth - 1
        mod = mod_all[i]
        kind, j = i % 3, i // 3
        ntiles = tl.nlat if last else tl.ntiles
        wr = jnp.zeros((D, 128), F32).at[:, :MOE_GROUPS].set(moe_w_group[i]) \
            .at[:, MOE_GROUPS:MOE_GROUPS + MOE_EXPERTS].set(moe_w_expert[i])
        br = jnp.zeros((1, 128), F32).at[0, :MOE_GROUPS].set(moe_b_group[i]) \
            .at[0, MOE_GROUPS:MOE_GROUPS + MOE_EXPERTS].set(moe_b_expert[i])
        out_args = (xs, mod, row(norm_ffn[i]), wr, br)

        if kind == 0:
            gate, u = _rg_in(tl, xs, mod, row(norm_mix[i]), rg_w_in[j].astype(BF16))
            wg, gb = _rg_gate_weights(rg_gate_w[j], rg_gate_b[j])
            hs = _rg_scan(tl, u.reshape(ltot * B, D), rg_conv_w[j], row(rg_conv_b[j]), wg, gb,
                          rg_lambda[j].reshape(2, 1, D))
            hs = hs.reshape(2, ltot, B * D)
            h_spec = lambda d: pl.BlockSpec((1, TM, D), lambda t: (d, tl.ttile(t), tl.batch(t)))
            xs, f, route = _mixer_out(tl, ntiles, _lhs_rg, (gate, hs, hs), (tile_spec(D), h_spec(0), h_spec(1)),
                                      rg_w_out[j].astype(BF16), *out_args, name="rg_out")
        elif kind == 1:
            w_down = jnp.pad(mla_w_down[j], ((0, 0), (0, 512 - mla_w_down.shape[2]))).astype(BF16)
            down = _norm_proj(tl, xs, mod, row(norm_mix[i]), w_down, 512, name="mla_down_proj")
            wq, wk, we, wv, gq, gk = _mla_weights(mla_w_uq[j], mla_w_ukv[j], mla_qk_norm[j])
            cos_t, sin_t = _rope_tables(L)
            q, k, v = _mla_up(tl, down, row(mla_q_norm[j]), row(mla_kv_norm[j]), wq, wk, we, wv, gq, gk, cos_t, sin_t)
            o = _attention(tl, q, k, v)
            xs, f, route = _mixer_out(tl, ntiles, _lhs_mla, (o,), (tile_spec(D),),
                                      mla_w_o[j].astype(BF16), *out_args, name="mla_out")
        else:
            n_in = ml_w_in.shape[2]
            w_in = jnp.pad(ml_w_in[j], ((0, 0), (0, ML_NP - n_in))).astype(BF16)
            proj = _norm_proj(tl, xs, mod, row(norm_mix[i]), w_in, 640, name="mlstm_in_proj")
            gate_bias = jnp.pad(ml_gate_b[j].reshape(1, -1), ((0, 0), (0, 128 - 4 * ML_HEADS)))
            hf, hb = _mlstm(tl, proj, gate_bias)
            og_spec = pl.BlockSpec((TM, D), lambda t: (t, 2))
            xs, f, route = _mixer_out(tl, ntiles, _lhs_mlstm, (hf, hb, proj, row(ml_out_norm[j])),
                                      (tile_spec(D), tile_spec(D), og_spec, pl.BlockSpec((1, D), lambda t: (0, 0))),
                                      ml_w_out[j].astype(BF16), *out_args, name="mlstm_out")

        xs_new = _moe(tl, ntiles, xs, f, route, mod, moe_w_gate_up[i], moe_w_down[i])
        xs = xs_new

    return xs[:B * L].reshape(B, L, D)
```

```python
import functools
import math

import jax
import jax.numpy as jnp
from jax import lax
from jax.experimental import pallas as pl
from jax.experimental.pallas import tpu as pltpu

F32 = jnp.float32
BF16 = jnp.bfloat16

D_MODEL = 1024
RMS_EPS = 1e-6

TM = 256
VMEM_LIMIT = 48 * 1024 * 1024

RG_BLOCK_W = 64
RG_CHUNK = 256
RG_CONV_W = 4
RG_C = 8.0
RG_TT = 64

MLA_HEADS = 16
MLA_Q_RANK = 256
MLA_KV_RANK = 128
MLA_NOPE = 64
MLA_ROPE = 32
MLA_V = 64
MLA_QK = MLA_NOPE + MLA_ROPE
MLA_HP = 128
ROPE_AXIS_DIM = MLA_ROPE // 2
ROPE_BASE = 10000.0
GRID_W = 64
ATT_TQ = 512

ML_HEADS = 4
ML_DV = 256
ML_DQK = 128
ML_TC = 256
ML_M_INIT = -1e30
ML_NP = 3200

MOE_GROUPS = 8
MOE_PER_GROUP = 8
MOE_EXPERTS = 64
MOE_TOPK = 2
MOE_FF = 256
MOE_BM = 256


def _cparams(sem):
    return pltpu.CompilerParams(dimension_semantics=sem, vmem_limit_bytes=VMEM_LIMIT)


def _dot(a, b):
    return jnp.dot(a, b, preferred_element_type=F32)


def _dot_t(a, b):
    return lax.dot_general(a, b, (((1,), (1,)), ((), ())), preferred_element_type=F32)


def _dot3(a, b):
    ah = a.astype(BF16)
    al = (a - ah.astype(F32)).astype(BF16)
    bh = b.astype(BF16)
    bl = (b - bh.astype(F32)).astype(BF16)
    return _dot(ah, bh) + (_dot(al, bh) + _dot(ah, bl))


def _sigmoid(x):
    return 1.0 / (1.0 + jnp.exp(-x))


def _softplus(x):
    return jnp.maximum(x, 0.0) + jnp.log1p(jnp.exp(-jnp.abs(x)))


def _gelu_tanh(x):
    return 0.5 * x * (1.0 + jnp.tanh(0.7978845608028654 * (x + 0.044715 * (x * x * x))))


def _rms(x, n=None):
    n = x.shape[-1] if n is None else n
    ms = jnp.sum(x * x, axis=-1, keepdims=True) * (1.0 / n)
    return x * lax.rsqrt(ms + RMS_EPS)


class _Tiles:
    def __init__(self, B, L, C):
        assert L % TM == 0 and C % TM == 0
        self.B, self.L, self.C = B, L, C
        self.lt, self.ct = L // TM, C // TM
        self.nlat = B * self.lt
        self.ntiles = self.nlat + B * self.ct
        self.NT = self.ntiles * TM

    def batch(self, i):
        return jnp.where(i < self.nlat, i // self.lt, (i - self.nlat) // self.ct)

    def mod(self, i):
        return jnp.where(i < self.nlat, i // self.lt, self.B)

    def ttile(self, i):
        return jnp.where(i < self.nlat, self.ct + i % self.lt, (i - self.nlat) % self.ct)


def _mod_kernel(c_ref, w_ref, b_ref, o_ref):
    c = c_ref[...]
    o_ref[0] = _dot3(c * _sigmoid(c), w_ref[0]) + b_ref[0]


def _modulation(cc, ada_w, ada_b):
    depth, d, n = ada_w.shape
    tn = 1536
    return pl.pallas_call(
        _mod_kernel,
        grid=(depth, n // tn),
        in_specs=[pl.BlockSpec((16, d), lambda l, j: (0, 0)),
                  pl.BlockSpec((1, d, tn), lambda l, j: (l, 0, j)),
                  pl.BlockSpec((1, 1, tn), lambda l, j: (l, 0, j))],
        out_specs=pl.BlockSpec((1, 16, tn), lambda l, j: (l, 0, j)),
        out_shape=jax.ShapeDtypeStruct((depth, 16, n), F32),
        compiler_params=_cparams(("arbitrary", "arbitrary")),
        name="ada_modulation",
    )(cc, ada_w, ada_b.reshape(depth, 1, n))


def _norm_mod(x, gain, mod, shift_row, scale_row):
    return _rms(x) * gain * (1.0 + mod[scale_row:scale_row + 1, :]) + mod[shift_row:shift_row + 1, :]


def _norm_proj_kernel(x_ref, mod_ref, g_ref, w_ref, o_ref, hn_ref):
    @pl.when(pl.program_id(1) == 0)
    def _():
        hn_ref[...] = _norm_mod(x_ref[...], g_ref[...], mod_ref[0], 0, 1).astype(BF16)

    o_ref[...] = _dot(hn_ref[...], w_ref[...]).astype(o_ref.dtype)


def _norm_proj(tl, x, mod, gain, w, tn, out_dtype=F32, name="norm_proj"):
    n = w.shape[1]
    return pl.pallas_call(
        _norm_proj_kernel,
        grid=(tl.ntiles, n // tn),
        in_specs=[pl.BlockSpec((TM, D_MODEL), lambda i, j: (i, 0)),
                  pl.BlockSpec((1, 8, D_MODEL), lambda i, j: (tl.mod(i), 0, 0)),
                  pl.BlockSpec((1, D_MODEL), lambda i, j: (0, 0)),
                  pl.BlockSpec((D_MODEL, tn), lambda i, j: (0, j))],
        out_specs=pl.BlockSpec((TM, tn), lambda i, j: (i, j)),
        out_shape=jax.ShapeDtypeStruct((tl.NT, n), out_dtype),
        scratch_shapes=[pltpu.VMEM((TM, D_MODEL), BF16)],
        compiler_params=_cparams(("arbitrary", "arbitrary")),
        name=name,
    )(x, mod, gain, w)


def _rg_in_kernel(x_ref, mod_ref, g_ref, wg_ref, wu_ref, gate_ref, u_ref, hn_ref):
    @pl.when(pl.program_id(1) == 0)
    def _():
        hn_ref[...] = _norm_mod(x_ref[...], g_ref[...], mod_ref[0], 0, 1).astype(BF16)

    h = hn_ref[...]
    gate_ref[...] = _gelu_tanh(_dot(h, wg_ref[...]))
    u_ref[...] = _dot(h, wu_ref[...])


def _rg_in(tl, x, mod, gain, w_in):
    W = D_MODEL
    tn = W
    nj = W // tn
    ltot = tl.L + tl.C
    return pl.pallas_call(
        _rg_in_kernel,
        grid=(tl.ntiles, nj),
        in_specs=[pl.BlockSpec((TM, D_MODEL), lambda i, j: (i, 0)),
                  pl.BlockSpec((1, 8, D_MODEL), lambda i, j: (tl.mod(i), 0, 0)),
                  pl.BlockSpec((1, D_MODEL), lambda i, j: (0, 0)),
                  pl.BlockSpec((D_MODEL, tn), lambda i, j: (0, j)),
                  pl.BlockSpec((D_MODEL, tn), lambda i, j: (0, nj + j))],
        out_specs=[pl.BlockSpec((TM, tn), lambda i, j: (i, j)),
                   pl.BlockSpec((TM, tn), lambda i, j: (tl.ttile(i), tl.batch(i) * nj + j))],
        out_shape=[jax.ShapeDtypeStruct((tl.NT, W), F32),
                   jax.ShapeDtypeStruct((ltot, tl.B * W), F32)],
        scratch_shapes=[pltpu.VMEM((TM, D_MODEL), BF16)],
        compiler_params=_cparams(("arbitrary", "arbitrary")),
        name="rg_in_proj",
    )(x, mod, gain, w_in, w_in)


def _rg_tile_order(d, k, ct, ntt):
    bwd = jnp.where(k < ct, ct - 1 - k, ntt - 1 - (k - ct))
    return jnp.where(d == 0, k, bwd)


def _rg_scan_kernel(um_ref, up_ref, un_ref, cw_ref, cb_ref, wg_ref, gb_ref, lam_ref, o_ref,
                    ext_ref, a_ref, b_ref, h_ref, *, B, ct, ntt):
    d = pl.program_id(0)
    k = pl.program_id(1)
    tile = _rg_tile_order(d, k, ct, ntt)
    R = RG_TT * B

    @pl.when(k == 0)
    def _():
        h_ref[...] = jnp.zeros_like(h_ref)

    seq_start = jnp.logical_or(tile == 0, tile == ct)
    seq_end = jnp.logical_or(tile == ct - 1, tile == ntt - 1)
    ext_ref[0:2 * B, :] = jnp.where(seq_start, 0.0, up_ref[...])
    ext_ref[2 * B:2 * B + R, :] = um_ref[...]
    ext_ref[2 * B + R:3 * B + R, :] = jnp.where(seq_end, 0.0, un_ref[...])
    cw = cw_ref[...]
    uc = cb_ref[...] + cw[0:1, :] * ext_ref[0:R, :]
    for j in range(1, RG_CONV_W):
        uc = uc + cw[j:j + 1, :] * ext_ref[j * B:j * B + R, :]
    ucb = uc.astype(BF16)
    c_lam = -RG_C * _softplus(-lam_ref[0])
    for c in range(D_MODEL // RG_CHUNK):
        sl = slice(c * RG_CHUNK, (c + 1) * RG_CHUNK)
        z = _dot(ucb[:, sl], wg_ref[0, c]) + gb_ref[0, c]
        r = _sigmoid(z[:, :RG_CHUNK])
        ig = _sigmoid(z[:, RG_CHUNK:])
        log_a = c_lam[:, sl] * r
        a = jnp.exp(log_a)
        one_minus_a2 = -jnp.tanh(log_a) * (a * a + 1.0)
        a_ref[:, sl] = a
        b_ref[:, sl] = jnp.sqrt(one_minus_a2) * (ig * uc[:, sl])

    def step(t, h):
        tt = jnp.where(d == 0, t, RG_TT - 1 - t)
        row = pl.multiple_of(tt * B, B)
        h = a_ref[pl.ds(row, B), :] * h + b_ref[pl.ds(row, B), :]
        o_ref[0, pl.ds(row, B), :] = h
        return h

    h_ref[...] = lax.fori_loop(0, RG_TT, step, h_ref[...], unroll=8)


def _rg_scan(tl, u_tm, conv_w, conv_b, wg, gb, lam):
    B, W = tl.B, D_MODEL
    assert B % 8 == 0 and tl.C % RG_TT == 0 and tl.L % RG_TT == 0
    ltot = tl.L + tl.C
    ntt, ct = ltot // RG_TT, tl.C // RG_TT
    R = RG_TT * B
    order = functools.partial(_rg_tile_order, ct=ct, ntt=ntt)
    nch = W // RG_CHUNK
    return pl.pallas_call(
        functools.partial(_rg_scan_kernel, B=B, ct=ct, ntt=ntt),
        grid=(2, ntt),
        in_specs=[pl.BlockSpec((R, W), lambda d, k: (order(d, k), 0)),
                  pl.BlockSpec((2 * B, W), lambda d, k: (jnp.maximum(order(d, k) * (RG_TT // 2) - 1, 0), 0)),
                  pl.BlockSpec((B, W), lambda d, k: (jnp.minimum((order(d, k) + 1) * RG_TT, ltot - 1), 0)),
                  pl.BlockSpec((RG_CONV_W, W), lambda d, k: (0, 0)),
                  pl.BlockSpec((1, W), lambda d, k: (0, 0)),
                  pl.BlockSpec((1, nch, RG_CHUNK, 2 * RG_CHUNK), lambda d, k: (d, 0, 0, 0)),
                  pl.BlockSpec((1, nch, 1, 2 * RG_CHUNK), lambda d, k: (d, 0, 0, 0)),
                  pl.BlockSpec((1, 1, W), lambda d, k: (d, 0, 0))],
        out_specs=pl.BlockSpec((1, R, W), lambda d, k: (d, order(d, k), 0)),
        out_shape=jax.ShapeDtypeStruct((2, ltot * B, W), F32),
        scratch_shapes=[pltpu.VMEM((3 * B + R, W), F32),
                        pltpu.VMEM((R, W), F32),
                        pltpu.VMEM((R, W), F32),
                        pltpu.VMEM((B, W), F32)],
        compiler_params=_cparams(("arbitrary", "arbitrary")),
        name="rg_scan",
    )(u_tm, u_tm, u_tm, conv_w, conv_b, wg, gb, lam)


def _rg_gate_weights(gate_w, gate_b):
    nb = gate_w.shape[2]
    per = RG_CHUNK // RG_BLOCK_W
    nch = nb // per
    gw = gate_w.reshape(2, 2, nch, per, RG_BLOCK_W, RG_BLOCK_W)
    eye = jnp.eye(per, dtype=gate_w.dtype)
    bd = jnp.einsum('dgcnij,nm->dgcnimj', gw, eye).reshape(2, 2, nch, RG_CHUNK, RG_CHUNK)
    wg = jnp.concatenate([bd[:, 0], bd[:, 1]], axis=-1).astype(BF16)
    gb = gate_b.reshape(2, 2, nch, 1, RG_CHUNK)
    gb = jnp.concatenate([gb[:, 0], gb[:, 1]], axis=-1)
    return wg, gb


def _mla_up_kernel(dn_ref, qn_ref, kvn_ref, wq_ref, wk_ref, we_ref, wv_ref, gq_ref, gk_ref, cos_ref, sin_ref,
                   q_ref, k_ref, v_ref):
    dn = dn_ref[...]
    cq = _rms(dn[:, :MLA_Q_RANK]) * qn_ref[...]
    ckv = _rms(dn[:, MLA_Q_RANK:MLA_Q_RANK + MLA_KV_RANK]) * kvn_ref[...]
    kr = dn[:, MLA_Q_RANK + MLA_KV_RANK:]
    kr_hi = kr.astype(BF16)
    kr_lo = (kr - kr_hi.astype(F32)).astype(BF16)
    ckvb = ckv.astype(BF16)
    q_pre = _dot(cq.astype(BF16), wq_ref[...])
    k_pre = _dot(ckvb, wk_ref[...]) + (_dot(kr_hi, we_ref[...]) + _dot(kr_lo, we_ref[...]))
    v_ref[...] = _dot(ckvb, wv_ref[...]).astype(v_ref.dtype)
    cos = cos_ref[...]
    sin = sin_ref[...]
    lane = lax.broadcasted_iota(jnp.int32, (TM, MLA_HP), 1)
    half = ROPE_AXIS_DIM // 2
    first_half = ((lane - MLA_NOPE) % ROPE_AXIS_DIM) < half
    scale = MLA_QK ** -0.5

    def head(x, gain):
        xn = _rms(x, MLA_QK) * gain
        partner = jnp.where(first_half, pltpu.roll(xn, MLA_HP - half, 1), pltpu.roll(xn, half, 1))
        return xn * cos + partner * sin

    for h in range(MLA_HEADS):
        sl = slice(h * MLA_HP, (h + 1) * MLA_HP)
        q_ref[:, sl] = (head(q_pre[:, sl], gq_ref[...]) * scale).astype(q_ref.dtype)
        k_ref[:, sl] = head(k_pre[:, sl], gk_ref[...]).astype(k_ref.dtype)


def _mla_up(tl, down, q_norm, kv_norm, wq, wk, we, wv, gq, gk, cos_t, sin_t):
    hw = MLA_HEADS * MLA_HP
    const = lambda i: (0, 0)
    rope_idx = lambda i: (jnp.where(i < tl.nlat, i % tl.lt, tl.lt), 0)
    return pl.pallas_call(
        _mla_up_kernel,
        grid=(tl.ntiles,),
        in_specs=[pl.BlockSpec((TM, 512), lambda i: (i, 0)),
                  pl.BlockSpec((1, MLA_Q_RANK), const),
                  pl.BlockSpec((1, MLA_KV_RANK), const),
                  pl.BlockSpec((MLA_Q_RANK, hw), const),
                  pl.BlockSpec((MLA_KV_RANK, hw), const),
                  pl.BlockSpec((128, hw), const),
                  pl.BlockSpec((MLA_KV_RANK, MLA_HEADS * MLA_V), const),
                  pl.BlockSpec((1, MLA_HP), const),
                  pl.BlockSpec((1, MLA_HP), const),
                  pl.BlockSpec((TM, MLA_HP), rope_idx),
                  pl.BlockSpec((TM, MLA_HP), rope_idx)],
        out_specs=[pl.BlockSpec((TM, hw), lambda i: (i, 0)),
                   pl.BlockSpec((TM, hw), lambda i: (i, 0)),
                   pl.BlockSpec((TM, MLA_HEADS * MLA_V), lambda i: (i, 0))],
        out_shape=[jax.ShapeDtypeStruct((tl.NT, hw), BF16),
                   jax.ShapeDtypeStruct((tl.NT, hw), BF16),
                   jax.ShapeDtypeStruct((tl.NT, MLA_HEADS * MLA_V), BF16)],
        compiler_params=_cparams(("arbitrary",)),
        name="mla_up_proj",
    )(down, q_norm, kv_norm, wq, wk, we, wv, gq, gk, cos_t, sin_t)


def _attn_kernel(*refs, n_kv):
    q_ref = refs[0]
    k_refs = refs[1:1 + n_kv]
    v_refs = refs[1 + n_kv:1 + 2 * n_kv]
    o_ref = refs[1 + 2 * n_kv]
    outs = []
    for hh in range(2):
        sl = slice(hh * MLA_HP, (hh + 1) * MLA_HP)
        q = q_ref[:, sl]
        s = [_dot_t(q, k_ref[:, sl]) for k_ref in k_refs]
        m = functools.reduce(jnp.maximum, [jnp.max(x, axis=-1, keepdims=True) for x in s])
        p = [jnp.exp(x - m) for x in s]
        l = functools.reduce(jnp.add, [jnp.sum(x, axis=-1, keepdims=True) for x in p])
        o = functools.reduce(jnp.add, [_dot(x.astype(BF16), v_ref[...]) for x, v_ref in zip(p, v_refs)])
        outs.append(o * (1.0 / l))
    lane = lax.broadcasted_iota(jnp.int32, outs[0].shape, 1)
    o_ref[...] = jnp.where(lane < MLA_V, outs[0], outs[1]).astype(o_ref.dtype)


def _attention(tl, q, k, v):
    B, L, C = tl.B, tl.L, tl.C
    hp2 = MLA_HEADS // 2
    tq = min(ATT_TQ, L)
    nq = L // tq
    ctx0 = (B * L) // C
    o_lat = pl.pallas_call(
        functools.partial(_attn_kernel, n_kv=2),
        grid=(B, hp2, nq),
        in_specs=[pl.BlockSpec((tq, 2 * MLA_HP), lambda b, h, i: (b * nq + i, h)),
                  pl.BlockSpec((C, 2 * MLA_HP), lambda b, h, i: (ctx0 + b, h)),
                  pl.BlockSpec((L, 2 * MLA_HP), lambda b, h, i: (b, h)),
                  pl.BlockSpec((C, 2 * MLA_V), lambda b, h, i: (ctx0 + b, h)),
                  pl.BlockSpec((L, 2 * MLA_V), lambda b, h, i: (b, h))],
        out_specs=pl.BlockSpec((tq, 2 * MLA_V), lambda b, h, i: (b * nq + i, h)),
        out_shape=jax.ShapeDtypeStruct((B * L, MLA_HEADS * MLA_V), BF16),
        compiler_params=_cparams(("arbitrary", "arbitrary", "arbitrary")),
        name="mla_attention_latent",
    )(q, k, k, v, v)
    o_ctx = pl.pallas_call(
        functools.partial(_attn_kernel, n_kv=1),
        grid=(B, hp2),
        in_specs=[pl.BlockSpec((C, 2 * MLA_HP), lambda b, h: (ctx0 + b, h)),
                  pl.BlockSpec((C, 2 * MLA_HP), lambda b, h: (ctx0 + b, h)),
                  pl.BlockSpec((C, 2 * MLA_V), lambda b, h: (ctx0 + b, h))],
        out_specs=pl.BlockSpec((C, 2 * MLA_V), lambda b, h: (b, h)),
        out_shape=jax.ShapeDtypeStruct((B * C, MLA_HEADS * MLA_V), BF16),
        compiler_params=_cparams(("arbitrary", "arbitrary")),
        name="mla_attention_context",
    )(q, k, v)
    return jnp.concatenate([o_lat, o_ctx], axis=0)


def _mla_weights(w_uq, w_ukv, qk_norm):
    H = MLA_HEADS
    wq = jnp.pad(w_uq.reshape(MLA_Q_RANK, H, MLA_QK), ((0, 0), (0, 0), (0, MLA_HP - MLA_QK)))
    wkv = w_ukv.reshape(MLA_KV_RANK, H, MLA_NOPE + MLA_V)
    wk = jnp.pad(wkv[:, :, :MLA_NOPE], ((0, 0), (0, 0), (0, MLA_HP - MLA_NOPE)))
    wv = wkv[:, :, MLA_NOPE:]
    r = jnp.arange(128)[:, None]
    col = jnp.arange(MLA_HP)[None, :]
    place = ((col == r + MLA_NOPE) & (r < MLA_ROPE)).astype(BF16)
    we = jnp.tile(place, (1, H))
    gq = jnp.pad(qk_norm[0], (0, MLA_HP - MLA_QK)).reshape(1, MLA_HP)
    gk = jnp.pad(qk_norm[1], (0, MLA_HP - MLA_QK)).reshape(1, MLA_HP)
    return (wq.reshape(MLA_Q_RANK, H * MLA_HP).astype(BF16), wk.reshape(MLA_KV_RANK, H * MLA_HP).astype(BF16),
            we, wv.reshape(MLA_KV_RANK, H * MLA_V).astype(BF16), gq, gk)


def _rope_tables(L):
    rows = L // GRID_W
    row = jnp.broadcast_to(jnp.arange(rows, dtype=F32)[:, None], (rows, GRID_W)).reshape(L)
    col = jnp.broadcast_to(jnp.arange(GRID_W, dtype=F32)[None, :], (rows, GRID_W)).reshape(L)
    inv_freq = ROPE_BASE ** (-jnp.arange(0, ROPE_AXIS_DIM, 2, dtype=F32) / ROPE_AXIS_DIM)
    ar = row[:, None] * inv_freq
    ac = col[:, None] * inv_freq
    ones = jnp.ones((L, MLA_NOPE), F32)
    zeros = jnp.zeros((L, MLA_NOPE), F32)
    pad1 = jnp.ones((L, MLA_HP - MLA_QK), F32)
    pad0 = jnp.zeros((L, MLA_HP - MLA_QK), F32)
    cos_t = jnp.concatenate([ones, jnp.cos(ar), jnp.cos(ar), jnp.cos(ac), jnp.cos(ac), pad1], axis=1)
    sin_t = jnp.concatenate([zeros, -jnp.sin(ar), jnp.sin(ar), -jnp.sin(ac), jnp.sin(ac), pad0], axis=1)
    cos_t = jnp.concatenate([cos_t, jnp.ones((TM, MLA_HP), F32)], axis=0)
    sin_t = jnp.concatenate([sin_t, jnp.zeros((TM, MLA_HP), F32)], axis=0)
    return cos_t, sin_t


def _log_sigmoid(x):
    return jnp.minimum(x, 0.0) - jnp.log1p(jnp.exp(-jnp.abs(x)))


def _mlstm_kernel(qf_ref, kf_ref, vf_ref, gf_ref, qb_ref, kb_ref, vb_ref, gb_ref, bias_ref,
                  of_ref, ob_ref, c_ref, n_ref, m_ref):
    T = ML_TC

    @pl.when(pl.program_id(1) == 0)
    def _():
        c_ref[...] = jnp.zeros_like(c_ref)
        n_ref[...] = jnp.zeros_like(n_ref)
        m_ref[...] = jnp.full(m_ref.shape, ML_M_INIT, F32)

    ti = lax.broadcasted_iota(jnp.int32, (T, T), 0)
    si = lax.broadcasted_iota(jnp.int32, (T, T), 1)
    dirs = ((qf_ref, kf_ref, vf_ref, gf_ref, of_ref), (qb_ref, kb_ref, vb_ref, gb_ref, ob_ref))
    for d, (q_ref, k_ref, v_ref, g_ref, o_ref) in enumerate(dirs):
        tri = (si <= ti) if d == 0 else (si >= ti)
        tri_t = (ti <= si) if d == 0 else (ti >= si)
        g = g_ref[...] + bias_ref[...]
        g_t = g.T
        for h in range(ML_HEADS):
            st = d * ML_HEADS + h
            li, lf_ = (2 * d) * ML_HEADS + h, (2 * d + 1) * ML_HEADS + h
            ig_col = g[:, li:li + 1]
            ig_row = g_t[li:li + 1, :]
            lf_col = _log_sigmoid(g[:, lf_:lf_ + 1])
            lf_row = _log_sigmoid(g_t[lf_:lf_ + 1, :])
            b_col = jnp.sum(jnp.where(tri, lf_row, 0.0), axis=1, keepdims=True)
            b_row = jnp.sum(jnp.where(tri_t, lf_col, 0.0), axis=0, keepdims=True)
            total = jnp.sum(lf_row, axis=1, keepdims=True)
            m_old = m_ref[st, 0:1, 0:1]
            d_log = jnp.where(tri, b_col - b_row + ig_row, -jnp.inf)
            inter_log = b_col + m_old
            m_t = jnp.maximum(inter_log, jnp.max(d_log, axis=1, keepdims=True))
            qh = q_ref[:, h * ML_DQK:(h + 1) * ML_DQK] * (ML_DQK ** -0.5)
            kh = k_ref[:, h * ML_DQK:(h + 1) * ML_DQK]
            vh = v_ref[:, h * ML_DV:(h + 1) * ML_DV].astype(BF16)
            qb16 = qh.astype(BF16)
            s_mat = _dot_t(qb16, kh.astype(BF16)) * jnp.exp(d_log - m_t)
            inter = jnp.exp(inter_log - m_t)
            c_old = c_ref[st]
            n_old = n_ref[st, 0:1, :]
            num = _dot(s_mat.astype(BF16), vh) + inter * _dot(qb16, c_old.astype(BF16))
            den = jnp.sum(s_mat, axis=1, keepdims=True) + inter * jnp.sum(qh * n_old, axis=1, keepdims=True)
            o_ref[0, :, h * ML_DV:(h + 1) * ML_DV] = num / jnp.maximum(jnp.abs(den), jnp.exp(-m_t))
            w_log = total - b_col + ig_col
            m_new = jnp.maximum(total + m_old, jnp.max(w_log, axis=0, keepdims=True))
            w = jnp.exp(w_log - m_new)
            decay = jnp.exp(total + m_old - m_new)
            kw = kh * w
            c_ref[st] = decay * c_old + _dot(kw.T.astype(BF16), vh)
            n_ref[st, 0:1, :] = decay * n_old + jnp.sum(kw, axis=0, keepdims=True)
            m_ref[st] = jnp.broadcast_to(m_new, m_ref.shape[1:])


def _mlstm(tl, proj, gate_bias):
    B, L, C = tl.B, tl.L, tl.C
    assert L % ML_TC == 0 and C % ML_TC == 0
    lc, cc = L // ML_TC, C // ML_TC
    nch = lc + cc
    ctx0 = (B * L) // ML_TC
    qw = ML_HEADS * ML_DQK
    vw = ML_HEADS * ML_DV
    gcol = (2 * qw + 2 * vw) // 128

    def rb(d, b, k):
        chunk = k if d == 0 else jnp.where(k < cc, cc - 1 - k, nch - 1 - (k - cc))
        return jnp.where(chunk < cc, ctx0 + b * cc + chunk, b * lc + (chunk - cc))

    def specs(d):
        return [pl.BlockSpec((ML_TC, qw), lambda b, k: (rb(d, b, k), 0)),
                pl.BlockSpec((ML_TC, qw), lambda b, k: (rb(d, b, k), 1)),
                pl.BlockSpec((ML_TC, vw), lambda b, k: (rb(d, b, k), (2 * qw) // vw)),
                pl.BlockSpec((ML_TC, 128), lambda b, k: (rb(d, b, k), gcol))]

    nst = 2 * ML_HEADS
    out = jax.ShapeDtypeStruct((1, tl.NT, vw), F32)
    hf, hb = pl.pallas_call(
        _mlstm_kernel,
        grid=(B, nch),
        in_specs=specs(0) + specs(1) + [pl.BlockSpec((1, 128), lambda b, k: (0, 0))],
        out_specs=[pl.BlockSpec((1, ML_TC, vw), lambda b, k: (0, rb(0, b, k), 0)),
                   pl.BlockSpec((1, ML_TC, vw), lambda b, k: (0, rb(1, b, k), 0))],
        out_shape=[out, out],
        scratch_shapes=[pltpu.VMEM((nst, ML_DQK, ML_DV), F32),
                        pltpu.VMEM((nst, 8, ML_DQK), F32),
                        pltpu.VMEM((nst, 8, 128), F32)],
        compiler_params=_cparams(("arbitrary", "arbitrary")),
        name="mlstm_chunks",
    )(proj, proj, proj, proj, proj, proj, proj, proj, gate_bias)
    return hf[0], hb[0]


def _route(lg):
    lane_i = lax.broadcasted_iota(jnp.int32, lg.shape, 1)
    lane = lane_i.astype(F32)
    neg = -jnp.inf
    gl = jnp.where(lane_i < MOE_GROUPS, lg, neg)
    gmax = jnp.max(gl, axis=-1, keepdims=True)
    gsum = jnp.sum(jnp.where(lane_i < MOE_GROUPS, jnp.exp(lg - gmax), 0.0), axis=-1, keepdims=True)
    p_top = 1.0 / gsum
    g_sel = jnp.min(jnp.where(gl == gmax, lane, 128.0), axis=-1, keepdims=True)
    group_of_lane = (lane_i >> 3).astype(F32) - 1.0
    el = jnp.where(group_of_lane == g_sel, lg, neg)
    e1 = jnp.max(el, axis=-1, keepdims=True)
    i1 = jnp.min(jnp.where(el == e1, lane, 128.0), axis=-1, keepdims=True)
    el2 = jnp.where(lane == i1, neg, el)
    e2 = jnp.max(el2, axis=-1, keepdims=True)
    i2 = jnp.min(jnp.where(el2 == e2, lane, 128.0), axis=-1, keepdims=True)
    t = jnp.exp(e2 - e1)
    w1 = p_top / (1.0 + t)
    w2 = w1 * t
    id1 = i1 - MOE_GROUPS
    id2 = i2 - MOE_GROUPS
    return jnp.where(lane_i == 0, id1, jnp.where(lane_i == 1, id2,
                                                 jnp.where(lane_i == 2, w1, jnp.where(lane_i == 3, w2, 0.0))))


def _lhs_rg(g_ref, hf_ref, hb_ref):
    return g_ref[...] * (hf_ref[0] + hb_ref[0])


def _lhs_mla(o_ref):
    return o_ref[...]


def _lhs_mlstm(hf_ref, hb_ref, og_ref, onorm_ref):
    hs = hf_ref[...] + hb_ref[...]
    og = _sigmoid(og_ref[...])
    parts = []
    for h in range(ML_HEADS):
        sl = slice(h * ML_DV, (h + 1) * ML_DV)
        parts.append(_rms(hs[:, sl]) * onorm_ref[:, sl] * og[:, sl])
    return jnp.concatenate(parts, axis=1)


def _mixer_out_kernel(*refs, n_lhs, lhs_fn):
    lhs_refs = refs[:n_lhs]
    w_ref, x_ref, mod_ref, gain_ref, wr_ref, br_ref, xo_ref, f_ref, r_ref = refs[n_lhs:]
    y = _dot(lhs_fn(*lhs_refs).astype(BF16), w_ref[...])
    mod = mod_ref[0]
    xn = x_ref[...] + mod[2:3, :] * y
    xo_ref[...] = xn
    f = _norm_mod(xn, gain_ref[...], mod, 3, 4)
    f_ref[...] = f.astype(f_ref.dtype)
    r_ref[...] = _route(_dot3(f, wr_ref[...]) + br_ref[...])


def _mixer_out(tl, ntiles, lhs_fn, lhs_args, lhs_specs, w_out, x, mod, gain, wr, br, name):
    n = ntiles * TM
    const = lambda i: (0, 0)
    return pl.pallas_call(
        functools.partial(_mixer_out_kernel, n_lhs=len(lhs_args), lhs_fn=lhs_fn),
        grid=(ntiles,),
        in_specs=list(lhs_specs) + [
            pl.BlockSpec((D_MODEL, D_MODEL), const),
            pl.BlockSpec((TM, D_MODEL), lambda i: (i, 0)),
            pl.BlockSpec((1, 8, D_MODEL), lambda i: (tl.mod(i), 0, 0)),
            pl.BlockSpec((1, D_MODEL), const),
            pl.BlockSpec((D_MODEL, 128), const),
            pl.BlockSpec((1, 128), const)],
        out_specs=[pl.BlockSpec((TM, D_MODEL), lambda i: (i, 0)),
                   pl.BlockSpec((TM, D_MODEL), lambda i: (i, 0)),
                   pl.BlockSpec((TM, 128), lambda i: (i, 0))],
        out_shape=[jax.ShapeDtypeStruct((n, D_MODEL), F32),
                   jax.ShapeDtypeStruct((n, D_MODEL), F32),
                   jax.ShapeDtypeStruct((n, 128), F32)],
        compiler_params=_cparams(("arbitrary",)),
        name=name,
    )(*lhs_args, w_out, x, mod, gain, wr, br)


def _expert_kernel(vb_ref, ve_ref, lo_ref, hi_ref, first_ref, last_ref, tab_ref, nxt_ref, f_hbm, wgu_ref, wd_ref,
                   y_hbm, xbuf, acc, gsem, ssem, *, nblk, n_tok):
    v = pl.program_id(0)
    blk = vb_ref[v]
    slot = blk % 2

    def start_gather(table, s):
        for r in range(MOE_BM):
            tok = lax.shift_right_logical(table[0, 0, r], 1)
            pltpu.make_async_copy(f_hbm.at[pl.ds(tok, 1), :], xbuf.at[s, pl.ds(r, 1), :], gsem.at[s]).start()

    def start_scatter(s):
        for r in range(MOE_BM):
            a = tab_ref[0, 0, r]
            row = (a & 1) * n_tok + lax.shift_right_logical(a, 1)
            pltpu.make_async_copy(acc.at[s, pl.ds(r, 1), :], y_hbm.at[pl.ds(row, 1), :], ssem.at[s]).start()

    def wait_gather(s):
        pltpu.make_async_copy(f_hbm.at[pl.ds(0, MOE_BM), :], xbuf.at[s], gsem.at[s]).wait()

    def wait_scatter(s):
        pltpu.make_async_copy(acc.at[s], y_hbm.at[pl.ds(0, MOE_BM), :], ssem.at[s]).wait()

    def per_slot(fn):
        for s in range(2):
            pl.when(slot == s)(functools.partial(fn, s))

    @pl.when(first_ref[v] == 1)
    def _():
        @pl.when(blk == 0)
        def _():
            start_gather(tab_ref, 0)

        def on_first(s):
            @pl.when(blk + 1 < nblk)
            def _():
                start_gather(nxt_ref, 1 - s)

            @pl.when(blk >= 2)
            def _():
                wait_scatter(s)
            acc[s] = jnp.zeros((MOE_BM, D_MODEL), F32)
            wait_gather(s)
        per_slot(on_first)

    @pl.when(hi_ref[v] > lo_ref[v])
    def _():
        def compute(s):
            x = xbuf[s].astype(BF16)
            gu = _dot(x, wgu_ref[0].astype(BF16))
            gate = gu[:, :MOE_FF]
            act = gate * _sigmoid(gate) * gu[:, MOE_FF:]
            y = _dot(act.astype(BF16), wd_ref[0].astype(BF16))
            r = lax.broadcasted_iota(jnp.int32, (MOE_BM, 1), 0)
            mine = jnp.logical_and(r >= lo_ref[v], r < hi_ref[v])
            acc[s] += jnp.where(mine, y, 0.0)
        per_slot(compute)

    @pl.when(last_ref[v] == 1)
    def _():
        def on_last(s):
            start_scatter(s)

            @pl.when(blk == nblk - 1)
            def _():
                if nblk >= 2:
                    wait_scatter(1 - s)
                wait_scatter(s)
        per_slot(on_last)


def _expert_ffn(f, table, visits, w_gate_up, w_down):
    n_tok = f.shape[0]
    nblk = table.shape[0]
    nvis = visits[0].shape[0]
    blk_idx = lambda v, vb, ve, lo, hi, fi, la: (vb[v], 0, 0)
    nxt_idx = lambda v, vb, ve, lo, hi, fi, la: (jnp.minimum(vb[v] + 1, nblk - 1), 0, 0)
    exp_idx = lambda v, vb, ve, lo, hi, fi, la: (ve[v], 0, 0)
    grid_spec = pltpu.PrefetchScalarGridSpec(
        num_scalar_prefetch=6,
        grid=(nvis,),
        in_specs=[pl.BlockSpec((1, 1, MOE_BM), blk_idx, memory_space=pltpu.SMEM),
                  pl.BlockSpec((1, 1, MOE_BM), nxt_idx, memory_space=pltpu.SMEM),
                  pl.BlockSpec(memory_space=pl.ANY),
                  pl.BlockSpec((1, D_MODEL, 2 * MOE_FF), exp_idx),
                  pl.BlockSpec((1, MOE_FF, D_MODEL), exp_idx)],
        out_specs=pl.BlockSpec(memory_space=pl.ANY),
        scratch_shapes=[pltpu.VMEM((2, MOE_BM, D_MODEL), F32),
                        pltpu.VMEM((2, MOE_BM, D_MODEL), F32),
                        pltpu.SemaphoreType.DMA((2,)),
                        pltpu.SemaphoreType.DMA((2,))],
    )
    return pl.pallas_call(
        functools.partial(_expert_kernel, nblk=nblk, n_tok=n_tok),
        grid_spec=grid_spec,
        out_shape=jax.ShapeDtypeStruct((MOE_TOPK * n_tok, D_MODEL), F32),
        compiler_params=_cparams(("arbitrary",)),
        name="moe_expert_ffn",
    )(*visits, table, table, f, w_gate_up, w_down)


def _combine_kernel(x_ref, *refs):
    y_refs = refs[:MOE_TOPK]
    r_ref, mod_ref, o_ref = refs[MOE_TOPK:]
    w = r_ref[...]
    y = w[:, MOE_TOPK:MOE_TOPK + 1] * y_refs[0][...]
    for k in range(1, MOE_TOPK):
        y = y + w[:, MOE_TOPK + k:MOE_TOPK + k + 1] * y_refs[k][...]
    o_ref[...] = x_ref[...] + mod_ref[0][5:6, :] * y


def _combine(tl, ntiles, x, y, route, mod):
    spec = pl.BlockSpec((TM, D_MODEL), lambda i: (i, 0))
    y_specs = [pl.BlockSpec((TM, D_MODEL), functools.partial(lambda i, k: (k * ntiles + i, 0), k=k))
               for k in range(MOE_TOPK)]
    return pl.pallas_call(
        _combine_kernel,
        grid=(ntiles,),
        in_specs=[spec] + y_specs + [pl.BlockSpec((TM, 128), lambda i: (i, 0)),
                                     pl.BlockSpec((1, 8, D_MODEL), lambda i: (tl.mod(i), 0, 0))],
        out_specs=spec,
        out_shape=jax.ShapeDtypeStruct((ntiles * TM, D_MODEL), F32),
        compiler_params=_cparams(("arbitrary",)),
        name="moe_combine",
    )(x, *([y] * MOE_TOPK), route, mod)


def _visit_tables(e_sorted, nk):
    E = MOE_EXPERTS
    nblk = nk // MOE_BM
    nvis = nblk + E
    bounds = jnp.searchsorted(e_sorted, jnp.arange(E + 1, dtype=jnp.int32), side='left').astype(jnp.int32)
    starts, ends = bounds[:-1], bounds[1:]
    fb = starts // MOE_BM
    nv = jnp.where(ends > starts, (ends - 1) // MOE_BM - fb + 1, 0)
    cum = jnp.cumsum(nv)
    total = cum[-1]
    v = jnp.arange(nvis, dtype=jnp.int32)
    active = v < total
    vc = jnp.minimum(v, total - 1)
    ve = jnp.minimum(jnp.searchsorted(cum, vc, side='right'), E - 1).astype(jnp.int32)
    vb = fb[ve] + (vc - (cum - nv)[ve])
    lo = jnp.where(active, jnp.maximum(starts[ve], vb * MOE_BM) - vb * MOE_BM, 0)
    hi = jnp.where(active, jnp.minimum(ends[ve], (vb + 1) * MOE_BM) - vb * MOE_BM, 0)
    prev_b = jnp.concatenate([jnp.full((1,), -1, jnp.int32), vb[:-1]])
    next_b = jnp.concatenate([vb[1:], jnp.full((1,), -1, jnp.int32)])
    first = jnp.logical_and(active, vb != prev_b)
    last = jnp.logical_and(active, jnp.logical_or(vb != next_b, v == total - 1))
    i32 = lambda a: a.astype(jnp.int32)
    return i32(vb), i32(ve), i32(lo), i32(hi), i32(first), i32(last)


def _moe(tl, ntiles, x, f, route, mod, w_gate_up, w_down):
    n = ntiles * TM
    nk = n * MOE_TOPK
    e_flat = route[:, :MOE_TOPK].astype(jnp.int32).reshape(nk)
    e_sorted, order = lax.sort((e_flat, jnp.arange(nk, dtype=jnp.int32)), num_keys=1, is_stable=True)
    visits = _visit_tables(e_sorted, nk)
    y = _expert_ffn(f, order.reshape(nk // MOE_BM, 1, MOE_BM), visits, w_gate_up, w_down)
    return _combine(tl, ntiles, x, y, route, mod)


def kernel(x, c, ctx, c_ctx, ada_w, ada_b, norm_mix, norm_ffn, rg_w_in, rg_conv_w, rg_conv_b, rg_gate_w, rg_gate_b, rg_lambda, rg_w_out, mla_w_down, mla_q_norm, mla_kv_norm, mla_w_uq, mla_w_ukv, mla_qk_norm, mla_w_o, ml_w_in, ml_gate_b, ml_out_norm, ml_w_out, moe_w_group, moe_b_group, moe_w_expert, moe_b_expert, moe_w_gate_up, moe_w_down):
    B, L, D = x.shape
    C = ctx.shape[1]
    depth = ada_w.shape[0]
    assert D == D_MODEL
    tl = _Tiles(B, L, C)
    ltot = L + C

    xs = jnp.concatenate([x.reshape(B * L, D), ctx.reshape(B * C, D)], axis=0)
    cc = jnp.zeros((16, D), F32).at[:B].set(c).at[B].set(c_ctx)
    mod_all = _modulation(cc, ada_w, ada_b)
    mod_all = jnp.pad(mod_all[:, :B + 1].reshape(depth, B + 1, 6, D), ((0, 0), (0, 0), (0, 2), (0, 0)))

    row = lambda a: a.reshape(1, -1)
    tile_spec = lambda w: pl.BlockSpec((TM, w), lambda i: (i, 0))

    for i in range(depth):
        last = i == depth - 1
        mod = mod_all[i]
        kind, j = i % 3, i // 3
        ntiles = tl.nlat if last else tl.ntiles
        wr = jnp.zeros((D, 128), F32).at[:, :MOE_GROUPS].set(moe_w_group[i]) \
            .at[:, MOE_GROUPS:MOE_GROUPS + MOE_EXPERTS].set(moe_w_expert[i])
        br = jnp.zeros((1, 128), F32).at[0, :MOE_GROUPS].set(moe_b_group[i]) \
            .at[0, MOE_GROUPS:MOE_GROUPS + MOE_EXPERTS].set(moe_b_expert[i])
        out_args = (xs, mod, row(norm_ffn[i]), wr, br)

        if kind == 0:
            gate, u = _rg_in(tl, xs, mod, row(norm_mix[i]), rg_w_in[j].astype(BF16))
            wg, gb = _rg_gate_weights(rg_gate_w[j], rg_gate_b[j])
            hs = _rg_scan(tl, u.reshape(ltot * B, D), rg_conv_w[j], row(rg_conv_b[j]), wg, gb,
                          rg_lambda[j].reshape(2, 1, D))
            hs = hs.reshape(2, ltot, B * D)
            h_spec = lambda d: pl.BlockSpec((1, TM, D), lambda t: (d, tl.ttile(t), tl.batch(t)))
            xs, f, route = _mixer_out(tl, ntiles, _lhs_rg, (gate, hs, hs), (tile_spec(D), h_spec(0), h_spec(1)),
                                      rg_w_out[j].astype(BF16), *out_args, name="rg_out")
        elif kind == 1:
            w_down = jnp.pad(mla_w_down[j], ((0, 0), (0, 512 - mla_w_down.shape[2]))).astype(BF16)
            down = _norm_proj(tl, xs, mod, row(norm_mix[i]), w_down, 512, name="mla_down_proj")
            wq, wk, we, wv, gq, gk = _mla_weights(mla_w_uq[j], mla_w_ukv[j], mla_qk_norm[j])
            cos_t, sin_t = _rope_tables(L)
            q, k, v = _mla_up(tl, down, row(mla_q_norm[j]), row(mla_kv_norm[j]), wq, wk, we, wv, gq, gk, cos_t, sin_t)
            o = _attention(tl, q, k, v)
            xs, f, route = _mixer_out(tl, ntiles, _lhs_mla, (o,), (tile_spec(D),),
                                      mla_w_o[j].astype(BF16), *out_args, name="mla_out")
        else:
            n_in = ml_w_in.shape[2]
            w_in = jnp.pad(ml_w_in[j], ((0, 0), (0, ML_NP - n_in))).astype(BF16)
            proj = _norm_proj(tl, xs, mod, row(norm_mix[i]), w_in, ML_NP, name="mlstm_in_proj")
            gate_bias = jnp.pad(ml_gate_b[j].reshape(1, -1), ((0, 0), (0, 128 - 4 * ML_HEADS)))
            hf, hb = _mlstm(tl, proj, gate_bias)
            og_spec = pl.BlockSpec((TM, D), lambda t: (t, 2))
            xs, f, route = _mixer_out(tl, ntiles, _lhs_mlstm, (hf, hb, proj, row(ml_out_norm[j])),
                                      (tile_spec(D), tile_spec(D), og_spec, pl.BlockSpec((1, D), lambda t: (0, 0))),
                                      ml_w_out[j].astype(BF16), *out_args, name="mlstm_out")

        xs_new = _moe(tl, ntiles, xs, f, route, mod, moe_w_gate_up[i], moe_w_down[i])
        xs = xs_new

    return xs[:B * L].reshape(B, L, D)
```

```python
import functools
import math

import jax
import jax.numpy as jnp
from jax import lax
from jax.experimental import pallas as pl
from jax.experimental.pallas import tpu as pltpu

F32 = jnp.float32
BF16 = jnp.bfloat16

D_MODEL = 1024
RMS_EPS = 1e-6

TM = 256
VMEM_LIMIT = 48 * 1024 * 1024

RG_BLOCK_W = 64
RG_CHUNK = 256
RG_CONV_W = 4
RG_C = 8.0
RG_TT = 64

MLA_HEADS = 16
MLA_Q_RANK = 256
MLA_KV_RANK = 128
MLA_NOPE = 64
MLA_ROPE = 32
MLA_V = 64
MLA_QK = MLA_NOPE + MLA_ROPE
MLA_HP = 128
ROPE_AXIS_DIM = MLA_ROPE // 2
ROPE_BASE = 10000.0
GRID_W = 64
ATT_HEADS = 4

ML_HEADS = 4
ML_DV = 256
ML_DQK = 128
ML_TC = 256
ML_M_INIT = -1e30
ML_NP = 3200

MOE_GROUPS = 8
MOE_PER_GROUP = 8
MOE_EXPERTS = 64
MOE_TOPK = 2
MOE_FF = 256
MOE_BM = 256


def _cparams(sem):
    return pltpu.CompilerParams(dimension_semantics=sem, vmem_limit_bytes=VMEM_LIMIT)


def _dot(a, b):
    return jnp.dot(a, b, preferred_element_type=F32)


def _dot_t(a, b):
    return lax.dot_general(a, b, (((1,), (1,)), ((), ())), preferred_element_type=F32)


def _dot3(a, b):
    ah = a.astype(BF16)
    al = (a - ah.astype(F32)).astype(BF16)
    bh = b.astype(BF16)
    bl = (b - bh.astype(F32)).astype(BF16)
    return _dot(ah, bh) + (_dot(al, bh) + _dot(ah, bl))


def _sigmoid(x):
    return 0.5 * jnp.tanh(0.5 * x) + 0.5


def _softplus(x):
    return jnp.maximum(x, 0.0) + jnp.log1p(jnp.exp(-jnp.abs(x)))


def _gelu_tanh(x):
    return 0.5 * x * (1.0 + jnp.tanh(0.7978845608028654 * (x + 0.044715 * (x * x * x))))


def _rms(x, n=None):
    n = x.shape[-1] if n is None else n
    ms = jnp.sum(x * x, axis=-1, keepdims=True) * (1.0 / n)
    return x * lax.rsqrt(ms + RMS_EPS)


class _Rows:
    def __init__(self, n, src, mod):
        self.n, self.src, self.mod = n, src, mod


class _Stream:
    def __init__(self, B, L, C):
        assert L % TM == 0 and C % TM == 0 and B % 8 == 0
        self.B, self.L, self.C = B, L, C
        self.ltot = L + C
        self.lt, self.ct = L // TM, C // TM
        self.tpb = self.lt + self.ct
        self.NT = B * self.ltot

    def all_rows(self):
        tpb, ct, B = self.tpb, self.ct, self.B
        return _Rows(B * tpb, lambda k: k, lambda k: jnp.where(k % tpb < ct, B, k // tpb))

    def latent_rows(self):
        tpb, ct, lt = self.tpb, self.ct, self.lt
        return _Rows(self.B * lt, lambda k: (k // lt) * tpb + ct + k % lt, lambda k: k // lt)

    def dense_latent_rows(self):
        lt = self.lt
        return _Rows(self.B * lt, lambda k: k, lambda k: k // lt)


def _mod_kernel(c_ref, w_ref, b_ref, o_ref):
    c = c_ref[...]
    o_ref[0] = _dot3(c * _sigmoid(c), w_ref[0]) + b_ref[0]


def _modulation(cc, ada_w, ada_b):
    depth, d, n = ada_w.shape
    tn = 1536
    return pl.pallas_call(
        _mod_kernel,
        grid=(depth, n // tn),
        in_specs=[pl.BlockSpec((16, d), lambda l, j: (0, 0)),
                  pl.BlockSpec((1, d, tn), lambda l, j: (l, 0, j)),
                  pl.BlockSpec((1, 1, tn), lambda l, j: (l, 0, j))],
        out_specs=pl.BlockSpec((1, 16, tn), lambda l, j: (l, 0, j)),
        out_shape=jax.ShapeDtypeStruct((depth, 16, n), F32),
        compiler_params=_cparams(("arbitrary", "arbitrary")),
        name="ada_modulation",
    )(cc, ada_w, ada_b.reshape(depth, 1, n))


def _norm_mod(x, gain, shift, scale):
    return _rms(x) * gain * (1.0 + scale) + shift


def _norm_proj_kernel(x_ref, mod_ref, g_ref, w_ref, o_ref):
    mod = mod_ref[0]
    h = _norm_mod(x_ref[...], g_ref[...], mod[0:1, :], mod[1:2, :]).astype(BF16)
    o_ref[...] = _dot(h, w_ref[...]).astype(o_ref.dtype)


def _norm_proj(rows, x, mod, gain, w, name):
    n = w.shape[1]
    const = lambda k: (0, 0)
    return pl.pallas_call(
        _norm_proj_kernel,
        grid=(rows.n,),
        in_specs=[pl.BlockSpec((TM, D_MODEL), lambda k: (rows.src(k), 0)),
                  pl.BlockSpec((1, 8, D_MODEL), lambda k: (rows.mod(k), 0, 0)),
                  pl.BlockSpec((1, D_MODEL), const),
                  pl.BlockSpec((D_MODEL, n), const)],
        out_specs=pl.BlockSpec((TM, n), lambda k: (k, 0)),
        out_shape=jax.ShapeDtypeStruct((rows.n * TM, n), F32),
        compiler_params=_cparams(("arbitrary",)),
        name=name,
    )(x, mod, gain, w)


RG_TR = 32


def _row_permutation(B):
    n = B * RG_TR
    r = jnp.arange(n)
    src = (r % B) * RG_TR + r // B
    return (src[:, None] == jnp.arange(n)[None, :]).astype(BF16)


def _time_tile_mod(mod_ref, is_ctx, B, row):
    return jnp.where(is_ctx, mod_ref[B:B + 1, row:row + 1, :], mod_ref[0:B, row:row + 1, :])


def _rg_in_kernel(x_ref, mod_ref, g_ref, perm_ref, wg_ref, wu_ref, gate_ref, u_ref, *, B, ctx_tiles):
    is_ctx = pl.program_id(0) < ctx_tiles
    h = _norm_mod(x_ref[...], g_ref[...], _time_tile_mod(mod_ref, is_ctx, B, 0), _time_tile_mod(mod_ref, is_ctx, B, 1))
    h = h.reshape(B * RG_TR, D_MODEL).astype(BF16)
    h = _dot(perm_ref[...], h).astype(BF16)
    gate_ref[...] = _gelu_tanh(_dot(h, wg_ref[...]))
    u_ref[...] = _dot(h, wu_ref[...])


def _rg_in(st, x, mod, gain, perm, w_in):
    B, W = st.B, D_MODEL
    R = B * RG_TR
    nt = st.ltot // RG_TR
    const = lambda t: (0, 0)
    out = jax.ShapeDtypeStruct((st.ltot * B, W), F32)
    return pl.pallas_call(
        functools.partial(_rg_in_kernel, B=B, ctx_tiles=st.C // RG_TR),
        grid=(nt,),
        in_specs=[pl.BlockSpec((B, RG_TR, D_MODEL), lambda t: (0, t, 0)),
                  pl.BlockSpec((B + 1, 8, D_MODEL), lambda t: (0, 0, 0)),
                  pl.BlockSpec((1, D_MODEL), const),
                  pl.BlockSpec((R, R), const),
                  pl.BlockSpec((D_MODEL, W), const),
                  pl.BlockSpec((D_MODEL, W), lambda t: (0, 1))],
        out_specs=[pl.BlockSpec((R, W), lambda t: (t, 0)),
                   pl.BlockSpec((R, W), lambda t: (t, 0))],
        out_shape=[out, out],
        compiler_params=_cparams(("arbitrary",)),
        name="rg_in_proj",
    )(x.reshape(B, st.ltot, D_MODEL), mod, gain, perm, w_in, w_in)


def _rg_tile_order(d, k, ct, ntt):
    bwd = jnp.where(k < ct, ct - 1 - k, ntt - 1 - (k - ct))
    return jnp.where(d == 0, k, bwd)


def _rg_scan_kernel(um_ref, up_ref, un_ref, cw_ref, cb_ref, wg_ref, gb_ref, lam_ref, o_ref,
                    ext_ref, a_ref, b_ref, h_ref, *, B, ct, ntt):
    d = pl.program_id(0)
    k = pl.program_id(1)
    tile = _rg_tile_order(d, k, ct, ntt)
    R = RG_TT * B

    @pl.when(k == 0)
    def _():
        h_ref[...] = jnp.zeros_like(h_ref)

    seq_start = jnp.logical_or(tile == 0, tile == ct)
    seq_end = jnp.logical_or(tile == ct - 1, tile == ntt - 1)
    ext_ref[0:2 * B, :] = jnp.where(seq_start, 0.0, up_ref[...])
    ext_ref[2 * B:2 * B + R, :] = um_ref[...]
    ext_ref[2 * B + R:3 * B + R, :] = jnp.where(seq_end, 0.0, un_ref[...])
    cw = cw_ref[...]
    uc = cb_ref[...] + cw[0:1, :] * ext_ref[0:R, :]
    for j in range(1, RG_CONV_W):
        uc = uc + cw[j:j + 1, :] * ext_ref[j * B:j * B + R, :]
    ucb = uc.astype(BF16)
    c_lam = -RG_C * _softplus(-lam_ref[0])
    for c in range(D_MODEL // RG_CHUNK):
        sl = slice(c * RG_CHUNK, (c + 1) * RG_CHUNK)
        z = _dot(ucb[:, sl], wg_ref[0, c]) + gb_ref[0, c]
        r = _sigmoid(z[:, :RG_CHUNK])
        ig = _sigmoid(z[:, RG_CHUNK:])
        log_a = c_lam[:, sl] * r
        a = jnp.exp(log_a)
        one_minus_a2 = -jnp.tanh(log_a) * (a * a + 1.0)
        a_ref[:, sl] = a
        b_ref[:, sl] = jnp.sqrt(one_minus_a2) * (ig * uc[:, sl])

    def scan(times):
        for c in range(D_MODEL // 128):
            cs = slice(c * 128, (c + 1) * 128)
            h = h_ref[:, cs]
            for t in times:
                rs = slice(t * B, (t + 1) * B)
                h = a_ref[rs, cs] * h + b_ref[rs, cs]
                o_ref[0, rs, cs] = h
            h_ref[:, cs] = h

    pl.when(d == 0)(lambda: scan(range(RG_TT)))
    pl.when(d == 1)(lambda: scan(range(RG_TT - 1, -1, -1)))


def _rg_scan(st, u_tm, conv_w, conv_b, wg, gb, lam):
    B, W = st.B, D_MODEL
    assert st.C % RG_TT == 0 and st.L % RG_TT == 0
    ltot = st.ltot
    ntt, ct = ltot // RG_TT, st.C // RG_TT
    R = RG_TT * B
    order = functools.partial(_rg_tile_order, ct=ct, ntt=ntt)
    nch = W // RG_CHUNK
    return pl.pallas_call(
        functools.partial(_rg_scan_kernel, B=B, ct=ct, ntt=ntt),
        grid=(2, ntt),
        in_specs=[pl.BlockSpec((R, W), lambda d, k: (order(d, k), 0)),
                  pl.BlockSpec((2 * B, W), lambda d, k: (jnp.maximum(order(d, k) * (RG_TT // 2) - 1, 0), 0)),
                  pl.BlockSpec((B, W), lambda d, k: (jnp.minimum((order(d, k) + 1) * RG_TT, ltot - 1), 0)),
                  pl.BlockSpec((RG_CONV_W, W), lambda d, k: (0, 0)),
                  pl.BlockSpec((1, W), lambda d, k: (0, 0)),
                  pl.BlockSpec((1, nch, RG_CHUNK, 2 * RG_CHUNK), lambda d, k: (d, 0, 0, 0)),
                  pl.BlockSpec((1, nch, 1, 2 * RG_CHUNK), lambda d, k: (d, 0, 0, 0)),
                  pl.BlockSpec((1, 1, W), lambda d, k: (d, 0, 0))],
        out_specs=pl.BlockSpec((1, R, W), lambda d, k: (d, order(d, k), 0)),
        out_shape=jax.ShapeDtypeStruct((2, ltot * B, W), F32),
        scratch_shapes=[pltpu.VMEM((3 * B + R, W), F32),
                        pltpu.VMEM((R, W), F32),
                        pltpu.VMEM((R, W), F32),
                        pltpu.VMEM((B, W), F32)],
        compiler_params=_cparams(("arbitrary", "arbitrary")),
        name="rg_scan",
    )(u_tm, u_tm, u_tm, conv_w, conv_b, wg, gb, lam)


def _rg_gate_weights(gate_w, gate_b):
    nb = gate_w.shape[2]
    per = RG_CHUNK // RG_BLOCK_W
    nch = nb // per
    gw = gate_w.reshape(2, 2, nch, per, RG_BLOCK_W, RG_BLOCK_W)
    eye = jnp.eye(per, dtype=gate_w.dtype)
    bd = jnp.einsum('dgcnij,nm->dgcnimj', gw, eye).reshape(2, 2, nch, RG_CHUNK, RG_CHUNK)
    wg = jnp.concatenate([bd[:, 0], bd[:, 1]], axis=-1).astype(BF16)
    gb = gate_b.reshape(2, 2, nch, 1, RG_CHUNK)
    gb = jnp.concatenate([gb[:, 0], gb[:, 1]], axis=-1)
    return wg, gb


def _rg_out_kernel(g_ref, hf_ref, hb_ref, perm_ref, w_ref, x_ref, mod_ref, gain_ref, wr_ref, br_ref,
                   xo_ref, f_ref, r_ref, *, B, ctx_tiles, t0):
    is_ctx = pl.program_id(0) + t0 < ctx_tiles
    lhs = (g_ref[...] * (hf_ref[0] + hb_ref[0])).astype(BF16)
    lhs = _dot(perm_ref[...], lhs).astype(BF16)
    y = _dot(lhs, w_ref[...]).reshape(B, RG_TR, D_MODEL)
    m = lambda row: _time_tile_mod(mod_ref, is_ctx, B, row)
    xn = x_ref[...] + m(2) * y
    xo_ref[...] = xn
    f = _norm_mod(xn, gain_ref[...], m(3), m(4))
    f_ref[...] = f
    lg = _dot3(f.reshape(B * RG_TR, D_MODEL), wr_ref[...]) + br_ref[...]
    r_ref[...] = _route(lg).reshape(B, RG_TR, 128)


def _rg_out(st, latent_only, gate, hs, perm, w_out, x, mod, gain, wr, br):
    B = st.B
    R = B * RG_TR
    t0 = st.C // RG_TR if latent_only else 0
    lo = st.L if latent_only else st.ltot
    nt = lo // RG_TR
    const = lambda t: (0, 0)
    blk = lambda w: pl.BlockSpec((B, RG_TR, w), lambda t: (0, t, 0))
    xo, f, route = pl.pallas_call(
        functools.partial(_rg_out_kernel, B=B, ctx_tiles=st.C // RG_TR, t0=t0),
        grid=(nt,),
        in_specs=[pl.BlockSpec((R, D_MODEL), lambda t: (t + t0, 0)),
                  pl.BlockSpec((1, R, D_MODEL), lambda t: (0, t + t0, 0)),
                  pl.BlockSpec((1, R, D_MODEL), lambda t: (1, t + t0, 0)),
                  pl.BlockSpec((R, R), const),
                  pl.BlockSpec((D_MODEL, D_MODEL), const),
                  pl.BlockSpec((B, RG_TR, D_MODEL), lambda t: (0, t + t0, 0)),
                  pl.BlockSpec((B + 1, 8, D_MODEL), lambda t: (0, 0, 0)),
                  pl.BlockSpec((1, D_MODEL), const),
                  pl.BlockSpec((D_MODEL, 128), const),
                  pl.BlockSpec((1, 128), const)],
        out_specs=[blk(D_MODEL), blk(D_MODEL), blk(128)],
        out_shape=[jax.ShapeDtypeStruct((B, lo, D_MODEL), F32),
                   jax.ShapeDtypeStruct((B, lo, D_MODEL), F32),
                   jax.ShapeDtypeStruct((B, lo, 128), F32)],
        compiler_params=_cparams(("arbitrary",)),
        name="rg_out",
    )(gate, hs, hs, perm, w_out, x.reshape(B, st.ltot, D_MODEL), mod, gain, wr, br)
    return xo.reshape(B * lo, D_MODEL), f.reshape(B * lo, D_MODEL), route.reshape(B * lo, 128)


def _mla_up_kernel(dn_ref, qn_ref, kvn_ref, wq_ref, wk_ref, we_ref, wv_ref, gq_ref, gk_ref, cos_ref, sin_ref,
                   q_ref, k_ref, v_ref):
    dn = dn_ref[...]
    cq = _rms(dn[:, :MLA_Q_RANK]) * qn_ref[...]
    ckv = _rms(dn[:, MLA_Q_RANK:MLA_Q_RANK + MLA_KV_RANK]) * kvn_ref[...]
    kr = dn[:, MLA_Q_RANK + MLA_KV_RANK:]
    kr_hi = kr.astype(BF16)
    kr_lo = (kr - kr_hi.astype(F32)).astype(BF16)
    ckvb = ckv.astype(BF16)
    q_pre = _dot(cq.astype(BF16), wq_ref[...])
    k_pre = _dot(ckvb, wk_ref[...]) + (_dot(kr_hi, we_ref[...]) + _dot(kr_lo, we_ref[...]))
    v_ref[...] = _dot(ckvb, wv_ref[...]).astype(v_ref.dtype)
    cos = cos_ref[...]
    sin = sin_ref[...]
    lane = lax.broadcasted_iota(jnp.int32, (TM, MLA_HP), 1)
    half = ROPE_AXIS_DIM // 2
    first_half = ((lane - MLA_NOPE) % ROPE_AXIS_DIM) < half
    scale = MLA_QK ** -0.5 * math.log2(math.e)

    def head(x, gain):
        xn = _rms(x, MLA_QK) * gain
        partner = jnp.where(first_half, pltpu.roll(xn, MLA_HP - half, 1), pltpu.roll(xn, half, 1))
        return xn * cos + partner * sin

    for h in range(MLA_HEADS):
        sl = slice(h * MLA_HP, (h + 1) * MLA_HP)
        q_ref[:, sl] = (head(q_pre[:, sl], gq_ref[...]) * scale).astype(q_ref.dtype)
        k_ref[:, sl] = head(k_pre[:, sl], gk_ref[...]).astype(k_ref.dtype)


def _mla_up(st, rows, down, q_norm, kv_norm, wq, wk, we, wv, gq, gk, cos_t, sin_t):
    hw = MLA_HEADS * MLA_HP
    const = lambda i: (0, 0)
    tpb, ct, lt = st.tpb, st.ct, st.lt
    rope_idx = lambda i: (jnp.where(i % tpb < ct, lt, i % tpb - ct), 0)
    return pl.pallas_call(
        _mla_up_kernel,
        grid=(rows.n,),
        in_specs=[pl.BlockSpec((TM, 512), lambda i: (i, 0)),
                  pl.BlockSpec((1, MLA_Q_RANK), const),
                  pl.BlockSpec((1, MLA_KV_RANK), const),
                  pl.BlockSpec((MLA_Q_RANK, hw), const),
                  pl.BlockSpec((MLA_KV_RANK, hw), const),
                  pl.BlockSpec((128, hw), const),
                  pl.BlockSpec((MLA_KV_RANK, MLA_HEADS * MLA_V), const),
                  pl.BlockSpec((1, MLA_HP), const),
                  pl.BlockSpec((1, MLA_HP), const),
                  pl.BlockSpec((TM, MLA_HP), rope_idx),
                  pl.BlockSpec((TM, MLA_HP), rope_idx)],
        out_specs=[pl.BlockSpec((TM, hw), lambda i: (i, 0)),
                   pl.BlockSpec((TM, hw), lambda i: (i, 0)),
                   pl.BlockSpec((TM, MLA_HEADS * MLA_V), lambda i: (i, 0))],
        out_shape=[jax.ShapeDtypeStruct((st.NT, hw), BF16),
                   jax.ShapeDtypeStruct((st.NT, hw), BF16),
                   jax.ShapeDtypeStruct((st.NT, MLA_HEADS * MLA_V), BF16)],
        compiler_params=_cparams(("arbitrary",)),
        name="mla_up_proj",
    )(down, q_norm, kv_norm, wq, wk, we, wv, gq, gk, cos_t, sin_t)


def _attn_kernel(q_ref, k_ref, v_ref, o_ref, vaug_ref, *, C, ct):
    qi = pl.program_id(2)

    @pl.when(qi == 0)
    def _():
        lane = lax.broadcasted_iota(jnp.int32, (k_ref.shape[0], 2 * MLA_V), 1)
        for hh in range(ATT_HEADS):
            pair = v_ref[:, (hh // 2) * 2 * MLA_V:(hh // 2 + 1) * 2 * MLA_V].astype(F32)
            if hh % 2 == 0:
                aug = jnp.where(lane < MLA_V, pair, jnp.where(lane == MLA_V, 1.0, 0.0))
            else:
                aug = jnp.where(lane >= MLA_V, pair, jnp.where(lane == 0, 1.0, 0.0))
            vaug_ref[hh] = aug.astype(BF16)

    def attend(nkeys):
        lane = lax.broadcasted_iota(jnp.int32, (TM, 2 * MLA_V), 1)
        for pair in range(ATT_HEADS // 2):
            outs = []
            for hh in (2 * pair, 2 * pair + 1):
                sl = slice(hh * MLA_HP, (hh + 1) * MLA_HP)
                s = _dot_t(q_ref[:, sl], k_ref[0:nkeys, sl])
                p = jnp.exp2(s - jnp.max(s, axis=-1, keepdims=True)).astype(BF16)
                o = _dot(p, vaug_ref[hh, 0:nkeys, :])
                rowsum = o[:, MLA_V:MLA_V + 1] if hh % 2 == 0 else o[:, 0:1]
                outs.append(o * (1.0 / rowsum))
            o_ref[:, pair * 2 * MLA_V:(pair + 1) * 2 * MLA_V] = jnp.where(lane < MLA_V, outs[0], outs[1]).astype(o_ref.dtype)

    pl.when(qi < ct)(lambda: attend(C))
    pl.when(qi >= ct)(lambda: attend(k_ref.shape[0]))


def _attention(st, q, k, v):
    B, ltot, tpb = st.B, st.ltot, st.tpb
    hg = MLA_HEADS // ATT_HEADS
    return pl.pallas_call(
        functools.partial(_attn_kernel, C=st.C, ct=st.ct),
        grid=(B, hg, tpb),
        in_specs=[pl.BlockSpec((TM, ATT_HEADS * MLA_HP), lambda b, h, i: (b * tpb + i, h)),
                  pl.BlockSpec((ltot, ATT_HEADS * MLA_HP), lambda b, h, i: (b, h)),
                  pl.BlockSpec((ltot, ATT_HEADS * MLA_V), lambda b, h, i: (b, h))],
        out_specs=pl.BlockSpec((TM, ATT_HEADS * MLA_V), lambda b, h, i: (b * tpb + i, h)),
        out_shape=jax.ShapeDtypeStruct((st.NT, MLA_HEADS * MLA_V), BF16),
        scratch_shapes=[pltpu.VMEM((ATT_HEADS, ltot, 2 * MLA_V), BF16)],
        compiler_params=_cparams(("arbitrary", "arbitrary", "arbitrary")),
        name="mla_attention",
    )(q, k, v)


def _mla_weights(w_uq, w_ukv, qk_norm):
    H = MLA_HEADS
    wq = jnp.pad(w_uq.reshape(MLA_Q_RANK, H, MLA_QK), ((0, 0), (0, 0), (0, MLA_HP - MLA_QK)))
    wkv = w_ukv.reshape(MLA_KV_RANK, H, MLA_NOPE + MLA_V)
    wk = jnp.pad(wkv[:, :, :MLA_NOPE], ((0, 0), (0, 0), (0, MLA_HP - MLA_NOPE)))
    wv = wkv[:, :, MLA_NOPE:]
    r = jnp.arange(128)[:, None]
    col = jnp.arange(MLA_HP)[None, :]
    place = ((col == r + MLA_NOPE) & (r < MLA_ROPE)).astype(BF16)
    we = jnp.tile(place, (1, H))
    gq = jnp.pad(qk_norm[0], (0, MLA_HP - MLA_QK)).reshape(1, MLA_HP)
    gk = jnp.pad(qk_norm[1], (0, MLA_HP - MLA_QK)).reshape(1, MLA_HP)
    return (wq.reshape(MLA_Q_RANK, H * MLA_HP).astype(BF16), wk.reshape(MLA_KV_RANK, H * MLA_HP).astype(BF16),
            we, wv.reshape(MLA_KV_RANK, H * MLA_V).astype(BF16), gq, gk)


def _rope_tables(L):
    rows = L // GRID_W
    row = jnp.broadcast_to(jnp.arange(rows, dtype=F32)[:, None], (rows, GRID_W)).reshape(L)
    col = jnp.broadcast_to(jnp.arange(GRID_W, dtype=F32)[None, :], (rows, GRID_W)).reshape(L)
    inv_freq = ROPE_BASE ** (-jnp.arange(0, ROPE_AXIS_DIM, 2, dtype=F32) / ROPE_AXIS_DIM)
    ar = row[:, None] * inv_freq
    ac = col[:, None] * inv_freq
    ones = jnp.ones((L, MLA_NOPE), F32)
    zeros = jnp.zeros((L, MLA_NOPE), F32)
    pad1 = jnp.ones((L, MLA_HP - MLA_QK), F32)
    pad0 = jnp.zeros((L, MLA_HP - MLA_QK), F32)
    cos_t = jnp.concatenate([ones, jnp.cos(ar), jnp.cos(ar), jnp.cos(ac), jnp.cos(ac), pad1], axis=1)
    sin_t = jnp.concatenate([zeros, -jnp.sin(ar), jnp.sin(ar), -jnp.sin(ac), jnp.sin(ac), pad0], axis=1)
    cos_t = jnp.concatenate([cos_t, jnp.ones((TM, MLA_HP), F32)], axis=0)
    sin_t = jnp.concatenate([sin_t, jnp.zeros((TM, MLA_HP), F32)], axis=0)
    return cos_t, sin_t


def _log_sigmoid(x):
    return jnp.minimum(x, 0.0) - jnp.log1p(jnp.exp(-jnp.abs(x)))


def _mlstm_kernel(qf_ref, kf_ref, vf_ref, gf_ref, qb_ref, kb_ref, vb_ref, gb_ref, bias_ref,
                  of_ref, ob_ref, c_ref, n_ref, m_ref):
    T = ML_TC

    @pl.when(pl.program_id(1) == 0)
    def _():
        c_ref[...] = jnp.zeros_like(c_ref)
        n_ref[...] = jnp.zeros_like(n_ref)
        m_ref[...] = jnp.full(m_ref.shape, ML_M_INIT, F32)

    ti = lax.broadcasted_iota(jnp.int32, (T, T), 0)
    si = lax.broadcasted_iota(jnp.int32, (T, T), 1)
    dirs = ((qf_ref, kf_ref, vf_ref, gf_ref, of_ref), (qb_ref, kb_ref, vb_ref, gb_ref, ob_ref))
    for d, (q_ref, k_ref, v_ref, g_ref, o_ref) in enumerate(dirs):
        tri = (si <= ti) if d == 0 else (si >= ti)
        tri_t = (ti <= si) if d == 0 else (ti >= si)
        g = g_ref[...] + bias_ref[...]
        g_t = g.T
        for h in range(ML_HEADS):
            st = d * ML_HEADS + h
            li, lf_ = (2 * d) * ML_HEADS + h, (2 * d + 1) * ML_HEADS + h
            ig_col = g[:, li:li + 1]
            ig_row = g_t[li:li + 1, :]
            lf_col = _log_sigmoid(g[:, lf_:lf_ + 1])
            lf_row = _log_sigmoid(g_t[lf_:lf_ + 1, :])
            b_col = jnp.sum(jnp.where(tri, lf_row, 0.0), axis=1, keepdims=True)
            b_row = jnp.sum(jnp.where(tri_t, lf_col, 0.0), axis=0, keepdims=True)
            total = jnp.sum(lf_row, axis=1, keepdims=True)
            m_old = m_ref[st, 0:1, 0:1]
            d_log = jnp.where(tri, b_col - b_row + ig_row, -jnp.inf)
            inter_log = b_col + m_old
            m_t = jnp.maximum(inter_log, jnp.max(d_log, axis=1, keepdims=True))
            qh = q_ref[:, h * ML_DQK:(h + 1) * ML_DQK] * (ML_DQK ** -0.5)
            kh = k_ref[:, h * ML_DQK:(h + 1) * ML_DQK]
            vh = v_ref[:, h * ML_DV:(h + 1) * ML_DV].astype(BF16)
            qb16 = qh.astype(BF16)
            s_mat = _dot_t(qb16, kh.astype(BF16)) * jnp.exp(d_log - m_t)
            inter = jnp.exp(inter_log - m_t)
            c_old = c_ref[st]
            n_old = n_ref[st, 0:1, :]
            num = _dot(s_mat.astype(BF16), vh) + inter * _dot(qb16, c_old.astype(BF16))
            den = jnp.sum(s_mat, axis=1, keepdims=True) + inter * jnp.sum(qh * n_old, axis=1, keepdims=True)
            o_ref[0, :, h * ML_DV:(h + 1) * ML_DV] = num / jnp.maximum(jnp.abs(den), jnp.exp(-m_t))
            w_log = total - b_col + ig_col
            m_new = jnp.maximum(total + m_old, jnp.max(w_log, axis=0, keepdims=True))
            w = jnp.exp(w_log - m_new)
            decay = jnp.exp(total + m_old - m_new)
            kw = kh * w
            c_ref[st] = decay * c_old + _dot(kw.T.astype(BF16), vh)
            n_ref[st, 0:1, :] = decay * n_old + jnp.sum(kw, axis=0, keepdims=True)
            m_ref[st] = jnp.broadcast_to(m_new, m_ref.shape[1:])


def _mlstm(st, proj, gate_bias):
    B = st.B
    assert st.L % ML_TC == 0 and st.C % ML_TC == 0
    cc = st.C // ML_TC
    nch = st.ltot // ML_TC
    qw = ML_HEADS * ML_DQK
    vw = ML_HEADS * ML_DV
    gcol = (2 * qw + 2 * vw) // 128

    def rb(d, b, k):
        chunk = k if d == 0 else jnp.where(k < cc, cc - 1 - k, nch - 1 - (k - cc))
        return b * nch + chunk

    def specs(d):
        return [pl.BlockSpec((ML_TC, qw), lambda b, k: (rb(d, b, k), 0)),
                pl.BlockSpec((ML_TC, qw), lambda b, k: (rb(d, b, k), 1)),
                pl.BlockSpec((ML_TC, vw), lambda b, k: (rb(d, b, k), (2 * qw) // vw)),
                pl.BlockSpec((ML_TC, 128), lambda b, k: (rb(d, b, k), gcol))]

    nst = 2 * ML_HEADS
    out = jax.ShapeDtypeStruct((1, st.NT, vw), F32)
    hf, hb = pl.pallas_call(
        _mlstm_kernel,
        grid=(B, nch),
        in_specs=specs(0) + specs(1) + [pl.BlockSpec((1, 128), lambda b, k: (0, 0))],
        out_specs=[pl.BlockSpec((1, ML_TC, vw), lambda b, k: (0, rb(0, b, k), 0)),
                   pl.BlockSpec((1, ML_TC, vw), lambda b, k: (0, rb(1, b, k), 0))],
        out_shape=[out, out],
        scratch_shapes=[pltpu.VMEM((nst, ML_DQK, ML_DV), F32),
                        pltpu.VMEM((nst, 8, ML_DQK), F32),
                        pltpu.VMEM((nst, 8, 128), F32)],
        compiler_params=_cparams(("arbitrary", "arbitrary")),
        name="mlstm_chunks",
    )(proj, proj, proj, proj, proj, proj, proj, proj, gate_bias)
    return hf[0], hb[0]


def _route(lg):
    lane_i = lax.broadcasted_iota(jnp.int32, lg.shape, 1)
    lane = lane_i.astype(F32)
    neg = -jnp.inf
    gl = jnp.where(lane_i < MOE_GROUPS, lg, neg)
    gmax = jnp.max(gl, axis=-1, keepdims=True)
    gsum = jnp.sum(jnp.where(lane_i < MOE_GROUPS, jnp.exp(lg - gmax), 0.0), axis=-1, keepdims=True)
    p_top = 1.0 / gsum
    g_sel = jnp.min(jnp.where(gl == gmax, lane, 128.0), axis=-1, keepdims=True)
    group_of_lane = (lane_i >> 3).astype(F32) - 1.0
    el = jnp.where(group_of_lane == g_sel, lg, neg)
    e1 = jnp.max(el, axis=-1, keepdims=True)
    i1 = jnp.min(jnp.where(el == e1, lane, 128.0), axis=-1, keepdims=True)
    el2 = jnp.where(lane == i1, neg, el)
    e2 = jnp.max(el2, axis=-1, keepdims=True)
    i2 = jnp.min(jnp.where(el2 == e2, lane, 128.0), axis=-1, keepdims=True)
    t = jnp.exp(e2 - e1)
    w1 = p_top / (1.0 + t)
    w2 = w1 * t
    id1 = i1 - MOE_GROUPS
    id2 = i2 - MOE_GROUPS
    return jnp.where(lane_i == 0, id1, jnp.where(lane_i == 1, id2,
                                                 jnp.where(lane_i == 2, w1, jnp.where(lane_i == 3, w2, 0.0))))


def _lhs_mla(o_ref):
    return o_ref[...]


def _lhs_mlstm(hf_ref, hb_ref, og_ref, onorm_ref):
    hs = hf_ref[...] + hb_ref[...]
    og = _sigmoid(og_ref[...])
    parts = []
    for h in range(ML_HEADS):
        sl = slice(h * ML_DV, (h + 1) * ML_DV)
        parts.append(_rms(hs[:, sl]) * onorm_ref[:, sl] * og[:, sl])
    return jnp.concatenate(parts, axis=1)


def _mixer_out_kernel(*refs, n_lhs, lhs_fn):
    lhs_refs = refs[:n_lhs]
    w_ref, x_ref, mod_ref, gain_ref, wr_ref, br_ref, xo_ref, f_ref, r_ref = refs[n_lhs:]
    y = _dot(lhs_fn(*lhs_refs).astype(BF16), w_ref[...])
    mod = mod_ref[0]
    xn = x_ref[...] + mod[2:3, :] * y
    xo_ref[...] = xn
    f = _norm_mod(xn, gain_ref[...], mod[3:4, :], mod[4:5, :])
    f_ref[...] = f
    r_ref[...] = _route(_dot3(f, wr_ref[...]) + br_ref[...])


def _mixer_out(rows, lhs_fn, lhs_args, lhs_specs, w_out, x, mod, gain, wr, br, name):
    n = rows.n * TM
    const = lambda k: (0, 0)
    out = lambda w: pl.BlockSpec((TM, w), lambda k: (k, 0))
    return pl.pallas_call(
        functools.partial(_mixer_out_kernel, n_lhs=len(lhs_args), lhs_fn=lhs_fn),
        grid=(rows.n,),
        in_specs=list(lhs_specs) + [
            pl.BlockSpec((D_MODEL, D_MODEL), const),
            pl.BlockSpec((TM, D_MODEL), lambda k: (rows.src(k), 0)),
            pl.BlockSpec((1, 8, D_MODEL), lambda k: (rows.mod(k), 0, 0)),
            pl.BlockSpec((1, D_MODEL), const),
            pl.BlockSpec((D_MODEL, 128), const),
            pl.BlockSpec((1, 128), const)],
        out_specs=[out(D_MODEL), out(D_MODEL), out(128)],
        out_shape=[jax.ShapeDtypeStruct((n, D_MODEL), F32),
                   jax.ShapeDtypeStruct((n, D_MODEL), F32),
                   jax.ShapeDtypeStruct((n, 128), F32)],
        compiler_params=_cparams(("arbitrary",)),
        name=name,
    )(*lhs_args, w_out, x, mod, gain, wr, br)


def _expert_kernel(vb_ref, ve_ref, lo_ref, hi_ref, first_ref, last_ref, tab_ref, nxt_ref, f_hbm, wgu_ref, wd_ref,
                   y_hbm, xbuf, acc, gsem, ssem, *, nblk, n_tok):
    v = pl.program_id(0)
    blk = vb_ref[v]
    slot = blk % 2

    def start_gather(table, s):
        for r in range(MOE_BM):
            tok = lax.shift_right_logical(table[0, 0, r], 1)
            pltpu.make_async_copy(f_hbm.at[pl.ds(tok, 1), :], xbuf.at[s, pl.ds(r, 1), :], gsem.at[s]).start()

    def start_scatter(s):
        for r in range(MOE_BM):
            a = tab_ref[0, 0, r]
            row = (a & 1) * n_tok + lax.shift_right_logical(a, 1)
            pltpu.make_async_copy(acc.at[s, pl.ds(r, 1), :], y_hbm.at[pl.ds(row, 1), :], ssem.at[s]).start()

    def wait_gather(s):
        pltpu.make_async_copy(f_hbm.at[pl.ds(0, MOE_BM), :], xbuf.at[s], gsem.at[s]).wait()

    def wait_scatter(s):
        pltpu.make_async_copy(acc.at[s], y_hbm.at[pl.ds(0, MOE_BM), :], ssem.at[s]).wait()

    def per_slot(fn):
        for s in range(2):
            pl.when(slot == s)(functools.partial(fn, s))

    @pl.when(first_ref[v] == 1)
    def _():
        @pl.when(blk == 0)
        def _():
            start_gather(tab_ref, 0)

        def on_first(s):
            @pl.when(blk + 1 < nblk)
            def _():
                start_gather(nxt_ref, 1 - s)

            @pl.when(blk >= 2)
            def _():
                wait_scatter(s)
            acc[s] = jnp.zeros((MOE_BM, D_MODEL), F32)
            wait_gather(s)
        per_slot(on_first)

    @pl.when(hi_ref[v] > lo_ref[v])
    def _():
        def compute(s):
            x = xbuf[s].astype(BF16)
            gu = _dot(x, wgu_ref[0].astype(BF16))
            gate = gu[:, :MOE_FF]
            act = gate * _sigmoid(gate) * gu[:, MOE_FF:]
            y = _dot(act.astype(BF16), wd_ref[0].astype(BF16))
            r = lax.broadcasted_iota(jnp.int32, (MOE_BM, 1), 0)
            mine = jnp.logical_and(r >= lo_ref[v], r < hi_ref[v])
            acc[s] += jnp.where(mine, y, 0.0)
        per_slot(compute)

    @pl.when(last_ref[v] == 1)
    def _():
        def on_last(s):
            start_scatter(s)

            @pl.when(blk == nblk - 1)
            def _():
                if nblk >= 2:
                    wait_scatter(1 - s)
                wait_scatter(s)
        per_slot(on_last)


def _expert_ffn(f, table, visits, w_gate_up, w_down):
    n_tok = f.shape[0]
    nblk = table.shape[0]
    nvis = visits[0].shape[0]
    blk_idx = lambda v, vb, ve, lo, hi, fi, la: (vb[v], 0, 0)
    nxt_idx = lambda v, vb, ve, lo, hi, fi, la: (jnp.minimum(vb[v] + 1, nblk - 1), 0, 0)
    exp_idx = lambda v, vb, ve, lo, hi, fi, la: (ve[v], 0, 0)
    grid_spec = pltpu.PrefetchScalarGridSpec(
        num_scalar_prefetch=6,
        grid=(nvis,),
        in_specs=[pl.BlockSpec((1, 1, MOE_BM), blk_idx, memory_space=pltpu.SMEM),
                  pl.BlockSpec((1, 1, MOE_BM), nxt_idx, memory_space=pltpu.SMEM),
                  pl.BlockSpec(memory_space=pl.ANY),
                  pl.BlockSpec((1, D_MODEL, 2 * MOE_FF), exp_idx),
                  pl.BlockSpec((1, MOE_FF, D_MODEL), exp_idx)],
        out_specs=pl.BlockSpec(memory_space=pl.ANY),
        scratch_shapes=[pltpu.VMEM((2, MOE_BM, D_MODEL), F32),
                        pltpu.VMEM((2, MOE_BM, D_MODEL), F32),
                        pltpu.SemaphoreType.DMA((2,)),
                        pltpu.SemaphoreType.DMA((2,))],
    )
    return pl.pallas_call(
        functools.partial(_expert_kernel, nblk=nblk, n_tok=n_tok),
        grid_spec=grid_spec,
        out_shape=jax.ShapeDtypeStruct((MOE_TOPK * n_tok, D_MODEL), F32),
        compiler_params=_cparams(("arbitrary",)),
        name="moe_expert_ffn",
    )(*visits, table, table, f, w_gate_up, w_down)


def _combine_kernel(x_ref, *refs):
    y_refs = refs[:MOE_TOPK]
    r_ref, mod_ref, o_ref = refs[MOE_TOPK:]
    w = r_ref[...]
    y = w[:, MOE_TOPK:MOE_TOPK + 1] * y_refs[0][...]
    for k in range(1, MOE_TOPK):
        y = y + w[:, MOE_TOPK + k:MOE_TOPK + k + 1] * y_refs[k][...]
    o_ref[...] = x_ref[...] + mod_ref[0][5:6, :] * y


def _combine(rows, x, y, route, mod):
    spec = pl.BlockSpec((TM, D_MODEL), lambda i: (i, 0))
    y_specs = [pl.BlockSpec((TM, D_MODEL), functools.partial(lambda i, k: (k * rows.n + i, 0), k=k))
               for k in range(MOE_TOPK)]
    return pl.pallas_call(
        _combine_kernel,
        grid=(rows.n,),
        in_specs=[spec] + y_specs + [pl.BlockSpec((TM, 128), lambda i: (i, 0)),
                                     pl.BlockSpec((1, 8, D_MODEL), lambda i: (rows.mod(i), 0, 0))],
        out_specs=spec,
        out_shape=jax.ShapeDtypeStruct((rows.n * TM, D_MODEL), F32),
        compiler_params=_cparams(("arbitrary",)),
        name="moe_combine",
    )(x, *([y] * MOE_TOPK), route, mod)


def _visit_tables(e_sorted, nk):
    E = MOE_EXPERTS
    nblk = nk // MOE_BM
    nvis = nblk + E
    bounds = jnp.searchsorted(e_sorted, jnp.arange(E + 1, dtype=jnp.int32), side='left').astype(jnp.int32)
    starts, ends = bounds[:-1], bounds[1:]
    fb = starts // MOE_BM
    nv = jnp.where(ends > starts, (ends - 1) // MOE_BM - fb + 1, 0)
    cum = jnp.cumsum(nv)
    total = cum[-1]
    v = jnp.arange(nvis, dtype=jnp.int32)
    active = v < total
    vc = jnp.minimum(v, total - 1)
    ve = jnp.minimum(jnp.searchsorted(cum, vc, side='right'), E - 1).astype(jnp.int32)
    vb = fb[ve] + (vc - (cum - nv)[ve])
    lo = jnp.where(active, jnp.maximum(starts[ve], vb * MOE_BM) - vb * MOE_BM, 0)
    hi = jnp.where(active, jnp.minimum(ends[ve], (vb + 1) * MOE_BM) - vb * MOE_BM, 0)
    prev_b = jnp.concatenate([jnp.full((1,), -1, jnp.int32), vb[:-1]])
    next_b = jnp.concatenate([vb[1:], jnp.full((1,), -1, jnp.int32)])
    first = jnp.logical_and(active, vb != prev_b)
    last = jnp.logical_and(active, jnp.logical_or(vb != next_b, v == total - 1))
    i32 = lambda a: a.astype(jnp.int32)
    return i32(vb), i32(ve), i32(lo), i32(hi), i32(first), i32(last)


def _moe(rows, x, f, route, mod, w_gate_up, w_down):
    n = rows.n * TM
    nk = n * MOE_TOPK
    e_flat = route[:, :MOE_TOPK].astype(jnp.int32).reshape(nk)
    e_sorted, order = lax.sort((e_flat, jnp.arange(nk, dtype=jnp.int32)), num_keys=1, is_stable=True)
    visits = _visit_tables(e_sorted, nk)
    y = _expert_ffn(f, order.reshape(nk // MOE_BM, 1, MOE_BM), visits, w_gate_up, w_down)
    return _combine(rows, x, y, route, mod)


def kernel(x, c, ctx, c_ctx, ada_w, ada_b, norm_mix, norm_ffn, rg_w_in, rg_conv_w, rg_conv_b, rg_gate_w, rg_gate_b, rg_lambda, rg_w_out, mla_w_down, mla_q_norm, mla_kv_norm, mla_w_uq, mla_w_ukv, mla_qk_norm, mla_w_o, ml_w_in, ml_gate_b, ml_out_norm, ml_w_out, moe_w_group, moe_b_group, moe_w_expert, moe_b_expert, moe_w_gate_up, moe_w_down):
    B, L, D = x.shape
    C = ctx.shape[1]
    depth = ada_w.shape[0]
    assert D == D_MODEL
    st = _Stream(B, L, C)

    xs = jnp.concatenate([ctx, x], axis=1).reshape(st.NT, D)
    cc = jnp.zeros((16, D), F32).at[:B].set(c).at[B].set(c_ctx)
    mod_all = _modulation(cc, ada_w, ada_b)
    mod_all = jnp.pad(mod_all[:, :B + 1].reshape(depth, B + 1, 6, D), ((0, 0), (0, 0), (0, 2), (0, 0)))
    perm = _row_permutation(B)

    row = lambda a: a.reshape(1, -1)

    for i in range(depth):
        last = i == depth - 1
        mod = mod_all[i]
        kind, j = i % 3, i // 3
        all_rows = st.all_rows()
        out_rows = st.latent_rows() if last else all_rows
        tile_spec = lambda w: pl.BlockSpec((TM, w), lambda k: (out_rows.src(k), 0))
        wr = jnp.zeros((D, 128), F32).at[:, :MOE_GROUPS].set(moe_w_group[i]) \
            .at[:, MOE_GROUPS:MOE_GROUPS + MOE_EXPERTS].set(moe_w_expert[i])
        br = jnp.zeros((1, 128), F32).at[0, :MOE_GROUPS].set(moe_b_group[i]) \
            .at[0, MOE_GROUPS:MOE_GROUPS + MOE_EXPERTS].set(moe_b_expert[i])
        out_args = (xs, mod, row(norm_ffn[i]), wr, br)

        if kind == 0:
            gate, u = _rg_in(st, xs, mod, row(norm_mix[i]), perm, rg_w_in[j].astype(BF16))
            wg, gb = _rg_gate_weights(rg_gate_w[j], rg_gate_b[j])
            hs = _rg_scan(st, u, rg_conv_w[j], row(rg_conv_b[j]), wg, gb, rg_lambda[j].reshape(2, 1, D))
            xs, f, route = _rg_out(st, last, gate, hs, perm.T, rg_w_out[j].astype(BF16), *out_args)
        elif kind == 1:
            w_down = jnp.pad(mla_w_down[j], ((0, 0), (0, 512 - mla_w_down.shape[2]))).astype(BF16)
            down = _norm_proj(all_rows, xs, mod, row(norm_mix[i]), w_down, name="mla_down_proj")
            wq, wk, we, wv, gq, gk = _mla_weights(mla_w_uq[j], mla_w_ukv[j], mla_qk_norm[j])
            cos_t, sin_t = _rope_tables(L)
            q, k, v = _mla_up(st, all_rows, down, row(mla_q_norm[j]), row(mla_kv_norm[j]), wq, wk, we, wv, gq, gk,
                              cos_t, sin_t)
            o = _attention(st, q, k, v)
            xs, f, route = _mixer_out(out_rows, _lhs_mla, (o,), (tile_spec(D),),
                                      mla_w_o[j].astype(BF16), *out_args, name="mla_out")
        else:
            n_in = ml_w_in.shape[2]
            w_in = jnp.pad(ml_w_in[j], ((0, 0), (0, ML_NP - n_in))).astype(BF16)
            proj = _norm_proj(all_rows, xs, mod, row(norm_mix[i]), w_in, name="mlstm_in_proj")
            gate_bias = jnp.pad(ml_gate_b[j].reshape(1, -1), ((0, 0), (0, 128 - 4 * ML_HEADS)))
            hf, hb = _mlstm(st, proj, gate_bias)
            og_spec = pl.BlockSpec((TM, D), lambda k: (out_rows.src(k), 2))
            xs, f, route = _mixer_out(out_rows, _lhs_mlstm, (hf, hb, proj, row(ml_out_norm[j])),
                                      (tile_spec(D), tile_spec(D), og_spec, pl.BlockSpec((1, D), lambda k: (0, 0))),
                                      ml_w_out[j].astype(BF16), *out_args, name="mlstm_out")

        moe_rows = st.dense_latent_rows() if last else all_rows
        xs = _moe(moe_rows, xs, f, route, mod, moe_w_gate_up[i], moe_w_down[i])

    return xs.reshape(B, L, D)
```

```python
import functools
import math

import jax
import jax.numpy as jnp
from jax import lax
from jax.experimental import pallas as pl
from jax.experimental.pallas import tpu as pltpu

F32 = jnp.float32
BF16 = jnp.bfloat16

D_MODEL = 1024
RMS_EPS = 1e-6

TM = 256
VMEM_LIMIT = 48 * 1024 * 1024

RG_BLOCK_W = 64
RG_CHUNK = 256
RG_CONV_W = 4
RG_C = 8.0
RG_TT = 64

MLA_HEADS = 16
MLA_Q_RANK = 256
MLA_KV_RANK = 128
MLA_NOPE = 64
MLA_ROPE = 32
MLA_V = 64
MLA_QK = MLA_NOPE + MLA_ROPE
MLA_HP = 128
ROPE_AXIS_DIM = MLA_ROPE // 2
ROPE_BASE = 10000.0
GRID_W = 64
ATT_HEADS = 4

ML_HEADS = 4
ML_DV = 256
ML_DQK = 128
ML_TC = 256
ML_M_INIT = -1e30
ML_NP = 3200

MOE_GROUPS = 8
MOE_PER_GROUP = 8
MOE_EXPERTS = 64
MOE_TOPK = 2
MOE_FF = 256
MOE_BM = 256


def _cparams(sem):
    return pltpu.CompilerParams(dimension_semantics=sem, vmem_limit_bytes=VMEM_LIMIT)


def _dot(a, b):
    return jnp.dot(a, b, preferred_element_type=F32)


def _dot_t(a, b):
    return lax.dot_general(a, b, (((1,), (1,)), ((), ())), preferred_element_type=F32)


def _dot3(a, b):
    ah = a.astype(BF16)
    al = (a - ah.astype(F32)).astype(BF16)
    bh = b.astype(BF16)
    bl = (b - bh.astype(F32)).astype(BF16)
    return _dot(ah, bh) + (_dot(al, bh) + _dot(ah, bl))


def _sigmoid(x):
    return 0.5 * jnp.tanh(0.5 * x) + 0.5


def _softplus(x):
    return jnp.maximum(x, 0.0) + jnp.log1p(jnp.exp(-jnp.abs(x)))


def _gelu_tanh(x):
    return 0.5 * x * (1.0 + jnp.tanh(0.7978845608028654 * (x + 0.044715 * (x * x * x))))


def _rms(x, n=None):
    n = x.shape[-1] if n is None else n
    ms = jnp.sum(x * x, axis=-1, keepdims=True) * (1.0 / n)
    return x * lax.rsqrt(ms + RMS_EPS)


class _Rows:
    def __init__(self, n, src, mod):
        self.n, self.src, self.mod = n, src, mod


class _Stream:
    def __init__(self, B, L, C):
        assert L % TM == 0 and C % TM == 0 and B % 8 == 0
        self.B, self.L, self.C = B, L, C
        self.ltot = L + C
        self.lt, self.ct = L // TM, C // TM
        self.tpb = self.lt + self.ct
        self.NT = B * self.ltot

    def all_rows(self):
        tpb, ct, B = self.tpb, self.ct, self.B
        return _Rows(B * tpb, lambda k: k, lambda k: jnp.where(k % tpb < ct, B, k // tpb))

    def latent_rows(self):
        tpb, ct, lt = self.tpb, self.ct, self.lt
        return _Rows(self.B * lt, lambda k: (k // lt) * tpb + ct + k % lt, lambda k: k // lt)

    def dense_latent_rows(self):
        lt = self.lt
        return _Rows(self.B * lt, lambda k: k, lambda k: k // lt)


def _mod_kernel(c_ref, w_ref, b_ref, o_ref):
    c = c_ref[...]
    o_ref[0] = _dot3(c * _sigmoid(c), w_ref[0]) + b_ref[0]


def _modulation(cc, ada_w, ada_b):
    depth, d, n = ada_w.shape
    tn = 1536
    return pl.pallas_call(
        _mod_kernel,
        grid=(depth, n // tn),
        in_specs=[pl.BlockSpec((16, d), lambda l, j: (0, 0)),
                  pl.BlockSpec((1, d, tn), lambda l, j: (l, 0, j)),
                  pl.BlockSpec((1, 1, tn), lambda l, j: (l, 0, j))],
        out_specs=pl.BlockSpec((1, 16, tn), lambda l, j: (l, 0, j)),
        out_shape=jax.ShapeDtypeStruct((depth, 16, n), F32),
        compiler_params=_cparams(("arbitrary", "arbitrary")),
        name="ada_modulation",
    )(cc, ada_w, ada_b.reshape(depth, 1, n))


def _norm_mod(x, gain, shift, scale):
    return _rms(x) * gain * (1.0 + scale) + shift


def _norm_proj_kernel(x_ref, mod_ref, g_ref, w_ref, o_ref):
    mod = mod_ref[0]
    h = _norm_mod(x_ref[...], g_ref[...], mod[0:1, :], mod[1:2, :]).astype(BF16)
    o_ref[...] = _dot(h, w_ref[...]).astype(o_ref.dtype)


def _norm_proj(rows, x, mod, gain, w, name):
    n = w.shape[1]
    const = lambda k: (0, 0)
    return pl.pallas_call(
        _norm_proj_kernel,
        grid=(rows.n,),
        in_specs=[pl.BlockSpec((TM, D_MODEL), lambda k: (rows.src(k), 0)),
                  pl.BlockSpec((1, 8, D_MODEL), lambda k: (rows.mod(k), 0, 0)),
                  pl.BlockSpec((1, D_MODEL), const),
                  pl.BlockSpec((D_MODEL, n), const)],
        out_specs=pl.BlockSpec((TM, n), lambda k: (k, 0)),
        out_shape=jax.ShapeDtypeStruct((rows.n * TM, n), F32),
        compiler_params=_cparams(("arbitrary",)),
        name=name,
    )(x, mod, gain, w)


RG_TR = 32


def _row_permutation(B):
    n = B * RG_TR
    r = jnp.arange(n)
    src = (r % B) * RG_TR + r // B
    return (src[:, None] == jnp.arange(n)[None, :]).astype(BF16)


def _time_tile_mod(mod_ref, is_ctx, B, row):
    return jnp.where(is_ctx, mod_ref[B:B + 1, row:row + 1, :], mod_ref[0:B, row:row + 1, :])


def _rg_in_kernel(x_ref, mod_ref, g_ref, perm_ref, wg_ref, wu_ref, gate_ref, u_ref, *, B, ctx_tiles):
    is_ctx = pl.program_id(0) < ctx_tiles
    h = _norm_mod(x_ref[...], g_ref[...], _time_tile_mod(mod_ref, is_ctx, B, 0), _time_tile_mod(mod_ref, is_ctx, B, 1))
    h = h.reshape(B * RG_TR, D_MODEL).astype(BF16)
    h = _dot(perm_ref[...], h).astype(BF16)
    gate_ref[...] = _gelu_tanh(_dot(h, wg_ref[...])).astype(gate_ref.dtype)
    u_ref[...] = _dot(h, wu_ref[...])


def _rg_in(st, x, mod, gain, perm, w_in):
    B, W = st.B, D_MODEL
    R = B * RG_TR
    nt = st.ltot // RG_TR
    const = lambda t: (0, 0)
    return pl.pallas_call(
        functools.partial(_rg_in_kernel, B=B, ctx_tiles=st.C // RG_TR),
        grid=(nt,),
        in_specs=[pl.BlockSpec((B, RG_TR, D_MODEL), lambda t: (0, t, 0)),
                  pl.BlockSpec((B + 1, 8, D_MODEL), lambda t: (0, 0, 0)),
                  pl.BlockSpec((1, D_MODEL), const),
                  pl.BlockSpec((R, R), const),
                  pl.BlockSpec((D_MODEL, W), const),
                  pl.BlockSpec((D_MODEL, W), lambda t: (0, 1))],
        out_specs=[pl.BlockSpec((R, W), lambda t: (t, 0)),
                   pl.BlockSpec((R, W), lambda t: (t, 0))],
        out_shape=[jax.ShapeDtypeStruct((st.ltot * B, W), BF16), jax.ShapeDtypeStruct((st.ltot * B, W), F32)],
        compiler_params=_cparams(("arbitrary",)),
        name="rg_in_proj",
    )(x.reshape(B, st.ltot, D_MODEL), mod, gain, perm, w_in, w_in)


def _rg_tile_order(d, k, ct, ntt):
    bwd = jnp.where(k < ct, ct - 1 - k, ntt - 1 - (k - ct))
    return jnp.where(d == 0, k, bwd)


def _rg_scan_kernel(um_ref, up_ref, un_ref, cw_ref, cb_ref, wg_ref, gb_ref, lam_ref, o_ref,
                    ext_ref, a_ref, b_ref, h_ref, *, B, ct, ntt):
    d = pl.program_id(0)
    k = pl.program_id(1)
    tile = _rg_tile_order(d, k, ct, ntt)
    R = RG_TT * B

    @pl.when(k == 0)
    def _():
        h_ref[...] = jnp.zeros_like(h_ref)

    seq_start = jnp.logical_or(tile == 0, tile == ct)
    seq_end = jnp.logical_or(tile == ct - 1, tile == ntt - 1)
    ext_ref[0:2 * B, :] = jnp.where(seq_start, 0.0, up_ref[...])
    ext_ref[2 * B:2 * B + R, :] = um_ref[...]
    ext_ref[2 * B + R:3 * B + R, :] = jnp.where(seq_end, 0.0, un_ref[...])
    cw = cw_ref[...]
    uc = cb_ref[...] + cw[0:1, :] * ext_ref[0:R, :]
    for j in range(1, RG_CONV_W):
        uc = uc + cw[j:j + 1, :] * ext_ref[j * B:j * B + R, :]
    ucb = uc.astype(BF16)
    c_lam = -RG_C * _softplus(-lam_ref[0])
    for c in range(D_MODEL // RG_CHUNK):
        sl = slice(c * RG_CHUNK, (c + 1) * RG_CHUNK)
        z = _dot(ucb[:, sl], wg_ref[0, c]) + gb_ref[0, c]
        r = _sigmoid(z[:, :RG_CHUNK])
        ig = _sigmoid(z[:, RG_CHUNK:])
        log_a = c_lam[:, sl] * r
        a = jnp.exp(log_a)
        one_minus_a2 = -jnp.tanh(log_a) * (a * a + 1.0)
        a_ref[:, sl] = a
        b_ref[:, sl] = jnp.sqrt(one_minus_a2) * (ig * uc[:, sl])

    def scan(times):
        for c in range(D_MODEL // 128):
            cs = slice(c * 128, (c + 1) * 128)
            h = h_ref[:, cs]
            for t in times:
                rs = slice(t * B, (t + 1) * B)
                h = a_ref[rs, cs] * h + b_ref[rs, cs]
                b_ref[rs, cs] = h
            h_ref[:, cs] = h

    pl.when(d == 0)(lambda: scan(range(RG_TT)))
    pl.when(d == 1)(lambda: scan(range(RG_TT - 1, -1, -1)))
    o_ref[0] = b_ref[...].astype(o_ref.dtype)


def _rg_scan(st, u_tm, conv_w, conv_b, wg, gb, lam):
    B, W = st.B, D_MODEL
    assert st.C % RG_TT == 0 and st.L % RG_TT == 0
    ltot = st.ltot
    ntt, ct = ltot // RG_TT, st.C // RG_TT
    R = RG_TT * B
    order = functools.partial(_rg_tile_order, ct=ct, ntt=ntt)
    nch = W // RG_CHUNK
    return pl.pallas_call(
        functools.partial(_rg_scan_kernel, B=B, ct=ct, ntt=ntt),
        grid=(2, ntt),
        in_specs=[pl.BlockSpec((R, W), lambda d, k: (order(d, k), 0)),
                  pl.BlockSpec((2 * B, W), lambda d, k: (jnp.maximum(order(d, k) * (RG_TT // 2) - 1, 0), 0)),
                  pl.BlockSpec((B, W), lambda d, k: (jnp.minimum((order(d, k) + 1) * RG_TT, ltot - 1), 0)),
                  pl.BlockSpec((RG_CONV_W, W), lambda d, k: (0, 0)),
                  pl.BlockSpec((1, W), lambda d, k: (0, 0)),
                  pl.BlockSpec((1, nch, RG_CHUNK, 2 * RG_CHUNK), lambda d, k: (d, 0, 0, 0)),
                  pl.BlockSpec((1, nch, 1, 2 * RG_CHUNK), lambda d, k: (d, 0, 0, 0)),
                  pl.BlockSpec((1, 1, W), lambda d, k: (d, 0, 0))],
        out_specs=pl.BlockSpec((1, R, W), lambda d, k: (d, order(d, k), 0)),
        out_shape=jax.ShapeDtypeStruct((2, ltot * B, W), BF16),
        scratch_shapes=[pltpu.VMEM((3 * B + R, W), F32),
                        pltpu.VMEM((R, W), F32),
                        pltpu.VMEM((R, W), F32),
                        pltpu.VMEM((B, W), F32)],
        compiler_params=_cparams(("arbitrary", "arbitrary")),
        name="rg_scan",
    )(u_tm, u_tm, u_tm, conv_w, conv_b, wg, gb, lam)


def _rg_gate_weights(gate_w, gate_b):
    nb = gate_w.shape[2]
    per = RG_CHUNK // RG_BLOCK_W
    nch = nb // per
    gw = gate_w.reshape(2, 2, nch, per, RG_BLOCK_W, RG_BLOCK_W)
    eye = jnp.eye(per, dtype=gate_w.dtype)
    bd = jnp.einsum('dgcnij,nm->dgcnimj', gw, eye).reshape(2, 2, nch, RG_CHUNK, RG_CHUNK)
    wg = jnp.concatenate([bd[:, 0], bd[:, 1]], axis=-1).astype(BF16)
    gb = gate_b.reshape(2, 2, nch, 1, RG_CHUNK)
    gb = jnp.concatenate([gb[:, 0], gb[:, 1]], axis=-1)
    return wg, gb


def _rg_out_kernel(g_ref, hf_ref, hb_ref, perm_ref, w_ref, x_ref, mod_ref, gain_ref, wr_ref, br_ref,
                   xo_ref, f_ref, r_ref, *, B, ctx_tiles, t0):
    is_ctx = pl.program_id(0) + t0 < ctx_tiles
    hsum = hf_ref[0].astype(F32) + hb_ref[0].astype(F32)
    lhs = (g_ref[...].astype(F32) * hsum).astype(BF16)
    lhs = _dot(perm_ref[...], lhs).astype(BF16)
    y = _dot(lhs, w_ref[...]).reshape(B, RG_TR, D_MODEL)
    m = lambda row: _time_tile_mod(mod_ref, is_ctx, B, row)
    xn = x_ref[...] + m(2) * y
    xo_ref[...] = xn
    f = _norm_mod(xn, gain_ref[...], m(3), m(4))
    f_ref[...] = f
    lg = _dot3(f.reshape(B * RG_TR, D_MODEL), wr_ref[...]) + br_ref[...]
    r_ref[...] = _route(lg).reshape(B, RG_TR, 128)


def _rg_out(st, latent_only, gate, hs, perm, w_out, x, mod, gain, wr, br):
    B = st.B
    R = B * RG_TR
    t0 = st.C // RG_TR if latent_only else 0
    lo = st.L if latent_only else st.ltot
    nt = lo // RG_TR
    const = lambda t: (0, 0)
    blk = lambda w: pl.BlockSpec((B, RG_TR, w), lambda t: (0, t, 0))
    xo, f, route = pl.pallas_call(
        functools.partial(_rg_out_kernel, B=B, ctx_tiles=st.C // RG_TR, t0=t0),
        grid=(nt,),
        in_specs=[pl.BlockSpec((R, D_MODEL), lambda t: (t + t0, 0)),
                  pl.BlockSpec((1, R, D_MODEL), lambda t: (0, t + t0, 0)),
                  pl.BlockSpec((1, R, D_MODEL), lambda t: (1, t + t0, 0)),
                  pl.BlockSpec((R, R), const),
                  pl.BlockSpec((D_MODEL, D_MODEL), const),
                  pl.BlockSpec((B, RG_TR, D_MODEL), lambda t: (0, t + t0, 0)),
                  pl.BlockSpec((B + 1, 8, D_MODEL), lambda t: (0, 0, 0)),
                  pl.BlockSpec((1, D_MODEL), const),
                  pl.BlockSpec((D_MODEL, 128), const),
                  pl.BlockSpec((1, 128), const)],
        out_specs=[blk(D_MODEL), blk(D_MODEL), blk(128)],
        out_shape=[jax.ShapeDtypeStruct((B, lo, D_MODEL), F32),
                   jax.ShapeDtypeStruct((B, lo, D_MODEL), F32),
                   jax.ShapeDtypeStruct((B, lo, 128), F32)],
        compiler_params=_cparams(("arbitrary",)),
        name="rg_out",
    )(gate, hs, hs, perm, w_out, x.reshape(B, st.ltot, D_MODEL), mod, gain, wr, br)
    return xo.reshape(B * lo, D_MODEL), f.reshape(B * lo, D_MODEL), route.reshape(B * lo, 128)


def _mla_up_kernel(dn_ref, qn_ref, kvn_ref, wq_ref, wk_ref, we_ref, wv_ref, gq_ref, gk_ref, cos_ref, sin_ref,
                   q_ref, k_ref, v_ref):
    dn = dn_ref[...]
    cq = _rms(dn[:, :MLA_Q_RANK]) * qn_ref[...]
    ckv = _rms(dn[:, MLA_Q_RANK:MLA_Q_RANK + MLA_KV_RANK]) * kvn_ref[...]
    kr = dn[:, MLA_Q_RANK + MLA_KV_RANK:]
    kr_hi = kr.astype(BF16)
    kr_lo = (kr - kr_hi.astype(F32)).astype(BF16)
    ckvb = ckv.astype(BF16)
    q_pre = _dot(cq.astype(BF16), wq_ref[...])
    k_pre = _dot(ckvb, wk_ref[...]) + (_dot(kr_hi, we_ref[...]) + _dot(kr_lo, we_ref[...]))
    v_ref[...] = _dot(ckvb, wv_ref[...]).astype(v_ref.dtype)
    cos = cos_ref[...]
    sin = sin_ref[...]
    lane = lax.broadcasted_iota(jnp.int32, (TM, MLA_HP), 1)
    half = ROPE_AXIS_DIM // 2
    first_half = ((lane - MLA_NOPE) % ROPE_AXIS_DIM) < half
    scale = MLA_QK ** -0.5 * math.log2(math.e)

    def head(x, gain):
        xn = _rms(x, MLA_QK) * gain
        partner = jnp.where(first_half, pltpu.roll(xn, MLA_HP - half, 1), pltpu.roll(xn, half, 1))
        return xn * cos + partner * sin

    for h in range(MLA_HEADS):
        sl = slice(h * MLA_HP, (h + 1) * MLA_HP)
        q_ref[:, sl] = (head(q_pre[:, sl], gq_ref[...]) * scale).astype(q_ref.dtype)
        k_ref[:, sl] = head(k_pre[:, sl], gk_ref[...]).astype(k_ref.dtype)


def _mla_up(st, rows, down, q_norm, kv_norm, wq, wk, we, wv, gq, gk, cos_t, sin_t):
    hw = MLA_HEADS * MLA_HP
    const = lambda i: (0, 0)
    tpb, ct, lt = st.tpb, st.ct, st.lt
    rope_idx = lambda i: (jnp.where(i % tpb < ct, lt, i % tpb - ct), 0)
    return pl.pallas_call(
        _mla_up_kernel,
        grid=(rows.n,),
        in_specs=[pl.BlockSpec((TM, 512), lambda i: (i, 0)),
                  pl.BlockSpec((1, MLA_Q_RANK), const),
                  pl.BlockSpec((1, MLA_KV_RANK), const),
                  pl.BlockSpec((MLA_Q_RANK, hw), const),
                  pl.BlockSpec((MLA_KV_RANK, hw), const),
                  pl.BlockSpec((128, hw), const),
                  pl.BlockSpec((MLA_KV_RANK, MLA_HEADS * MLA_V), const),
                  pl.BlockSpec((1, MLA_HP), const),
                  pl.BlockSpec((1, MLA_HP), const),
                  pl.BlockSpec((TM, MLA_HP), rope_idx),
                  pl.BlockSpec((TM, MLA_HP), rope_idx)],
        out_specs=[pl.BlockSpec((TM, hw), lambda i: (i, 0)),
                   pl.BlockSpec((TM, hw), lambda i: (i, 0)),
                   pl.BlockSpec((TM, MLA_HEADS * MLA_V), lambda i: (i, 0))],
        out_shape=[jax.ShapeDtypeStruct((st.NT, hw), BF16),
                   jax.ShapeDtypeStruct((st.NT, hw), BF16),
                   jax.ShapeDtypeStruct((st.NT, MLA_HEADS * MLA_V), BF16)],
        compiler_params=_cparams(("arbitrary",)),
        name="mla_up_proj",
    )(down, q_norm, kv_norm, wq, wk, we, wv, gq, gk, cos_t, sin_t)


def _attn_kernel(q_ref, k_ref, v_ref, o_ref, vaug_ref, *, C, ct):
    qi = pl.program_id(2)

    @pl.when(qi == 0)
    def _():
        lane = lax.broadcasted_iota(jnp.int32, (k_ref.shape[0], 2 * MLA_V), 1)
        for hh in range(ATT_HEADS):
            pair = v_ref[:, (hh // 2) * 2 * MLA_V:(hh // 2 + 1) * 2 * MLA_V].astype(F32)
            if hh % 2 == 0:
                aug = jnp.where(lane < MLA_V, pair, jnp.where(lane == MLA_V, 1.0, 0.0))
            else:
                aug = jnp.where(lane >= MLA_V, pair, jnp.where(lane == 0, 1.0, 0.0))
            vaug_ref[hh] = aug.astype(BF16)

    def attend(nkeys):
        lane = lax.broadcasted_iota(jnp.int32, (TM, 2 * MLA_V), 1)
        for pair in range(ATT_HEADS // 2):
            outs = []
            for hh in (2 * pair, 2 * pair + 1):
                sl = slice(hh * MLA_HP, (hh + 1) * MLA_HP)
                s = _dot_t(q_ref[:, sl], k_ref[0:nkeys, sl])
                p = jnp.exp2(s - jnp.max(s, axis=-1, keepdims=True)).astype(BF16)
                o = _dot(p, vaug_ref[hh, 0:nkeys, :])
                rowsum = o[:, MLA_V:MLA_V + 1] if hh % 2 == 0 else o[:, 0:1]
                outs.append(o * (1.0 / rowsum))
            o_ref[:, pair * 2 * MLA_V:(pair + 1) * 2 * MLA_V] = jnp.where(lane < MLA_V, outs[0], outs[1]).astype(o_ref.dtype)

    pl.when(qi < ct)(lambda: attend(C))
    pl.when(qi >= ct)(lambda: attend(k_ref.shape[0]))


def _attention(st, q, k, v):
    B, ltot, tpb = st.B, st.ltot, st.tpb
    hg = MLA_HEADS // ATT_HEADS
    return pl.pallas_call(
        functools.partial(_attn_kernel, C=st.C, ct=st.ct),
        grid=(B, hg, tpb),
        in_specs=[pl.BlockSpec((TM, ATT_HEADS * MLA_HP), lambda b, h, i: (b * tpb + i, h)),
                  pl.BlockSpec((ltot, ATT_HEADS * MLA_HP), lambda b, h, i: (b, h)),
                  pl.BlockSpec((ltot, ATT_HEADS * MLA_V), lambda b, h, i: (b, h))],
        out_specs=pl.BlockSpec((TM, ATT_HEADS * MLA_V), lambda b, h, i: (b * tpb + i, h)),
        out_shape=jax.ShapeDtypeStruct((st.NT, MLA_HEADS * MLA_V), BF16),
        scratch_shapes=[pltpu.VMEM((ATT_HEADS, ltot, 2 * MLA_V), BF16)],
        compiler_params=_cparams(("arbitrary", "arbitrary", "arbitrary")),
        name="mla_attention",
    )(q, k, v)


def _mla_weights(w_uq, w_ukv, qk_norm):
    H = MLA_HEADS
    wq = jnp.pad(w_uq.reshape(MLA_Q_RANK, H, MLA_QK), ((0, 0), (0, 0), (0, MLA_HP - MLA_QK)))
    wkv = w_ukv.reshape(MLA_KV_RANK, H, MLA_NOPE + MLA_V)
    wk = jnp.pad(wkv[:, :, :MLA_NOPE], ((0, 0), (0, 0), (0, MLA_HP - MLA_NOPE)))
    wv = wkv[:, :, MLA_NOPE:]
    r = jnp.arange(128)[:, None]
    col = jnp.arange(MLA_HP)[None, :]
    place = ((col == r + MLA_NOPE) & (r < MLA_ROPE)).astype(BF16)
    we = jnp.tile(place, (1, H))
    gq = jnp.pad(qk_norm[0], (0, MLA_HP - MLA_QK)).reshape(1, MLA_HP)
    gk = jnp.pad(qk_norm[1], (0, MLA_HP - MLA_QK)).reshape(1, MLA_HP)
    return (wq.reshape(MLA_Q_RANK, H * MLA_HP).astype(BF16), wk.reshape(MLA_KV_RANK, H * MLA_HP).astype(BF16),
            we, wv.reshape(MLA_KV_RANK, H * MLA_V).astype(BF16), gq, gk)


def _rope_tables(L):
    rows = L // GRID_W
    row = jnp.broadcast_to(jnp.arange(rows, dtype=F32)[:, None], (rows, GRID_W)).reshape(L)
    col = jnp.broadcast_to(jnp.arange(GRID_W, dtype=F32)[None, :], (rows, GRID_W)).reshape(L)
    inv_freq = ROPE_BASE ** (-jnp.arange(0, ROPE_AXIS_DIM, 2, dtype=F32) / ROPE_AXIS_DIM)
    ar = row[:, None] * inv_freq
    ac = col[:, None] * inv_freq
    ones = jnp.ones((L, MLA_NOPE), F32)
    zeros = jnp.zeros((L, MLA_NOPE), F32)
    pad1 = jnp.ones((L, MLA_HP - MLA_QK), F32)
    pad0 = jnp.zeros((L, MLA_HP - MLA_QK), F32)
    cos_t = jnp.concatenate([ones, jnp.cos(ar), jnp.cos(ar), jnp.cos(ac), jnp.cos(ac), pad1], axis=1)
    sin_t = jnp.concatenate([zeros, -jnp.sin(ar), jnp.sin(ar), -jnp.sin(ac), jnp.sin(ac), pad0], axis=1)
    cos_t = jnp.concatenate([cos_t, jnp.ones((TM, MLA_HP), F32)], axis=0)
    sin_t = jnp.concatenate([sin_t, jnp.zeros((TM, MLA_HP), F32)], axis=0)
    return cos_t, sin_t


def _log_sigmoid(x):
    return jnp.minimum(x, 0.0) - jnp.log1p(jnp.exp(-jnp.abs(x)))


def _mlstm_kernel(qf_ref, kf_ref, vf_ref, gf_ref, qb_ref, kb_ref, vb_ref, gb_ref, bias_ref,
                  of_ref, ob_ref, c_ref, n_ref, m_ref):
    T = ML_TC

    @pl.when(pl.program_id(1) == 0)
    def _():
        c_ref[...] = jnp.zeros_like(c_ref)
        n_ref[...] = jnp.zeros_like(n_ref)
        m_ref[...] = jnp.full(m_ref.shape, ML_M_INIT, F32)

    ti = lax.broadcasted_iota(jnp.int32, (T, T), 0)
    si = lax.broadcasted_iota(jnp.int32, (T, T), 1)
    dirs = ((qf_ref, kf_ref, vf_ref, gf_ref, of_ref), (qb_ref, kb_ref, vb_ref, gb_ref, ob_ref))
    for d, (q_ref, k_ref, v_ref, g_ref, o_ref) in enumerate(dirs):
        tri = (si <= ti) if d == 0 else (si >= ti)
        tri_t = (ti <= si) if d == 0 else (ti >= si)
        g = g_ref[...] + bias_ref[...]
        g_t = g.T
        for h in range(ML_HEADS):
            st = d * ML_HEADS + h
            li, lf_ = (2 * d) * ML_HEADS + h, (2 * d + 1) * ML_HEADS + h
            ig_col = g[:, li:li + 1]
            ig_row = g_t[li:li + 1, :]
            lf_col = _log_sigmoid(g[:, lf_:lf_ + 1])
            lf_row = _log_sigmoid(g_t[lf_:lf_ + 1, :])
            b_col = jnp.sum(jnp.where(tri, lf_row, 0.0), axis=1, keepdims=True)
            b_row = jnp.sum(jnp.where(tri_t, lf_col, 0.0), axis=0, keepdims=True)
            total = jnp.sum(lf_row, axis=1, keepdims=True)
            m_old = m_ref[st, 0:1, 0:1]
            d_log = jnp.where(tri, b_col - b_row + ig_row, -jnp.inf)
            inter_log = b_col + m_old
            m_t = jnp.maximum(inter_log, jnp.max(d_log, axis=1, keepdims=True))
            qh = q_ref[:, h * ML_DQK:(h + 1) * ML_DQK] * (ML_DQK ** -0.5)
            kh = k_ref[:, h * ML_DQK:(h + 1) * ML_DQK]
            vh = v_ref[:, h * ML_DV:(h + 1) * ML_DV].astype(BF16)
            qb16 = qh.astype(BF16)
            s_mat = _dot_t(qb16, kh.astype(BF16)) * jnp.exp(d_log - m_t)
            inter = jnp.exp(inter_log - m_t)
            c_old = c_ref[st]
            n_old = n_ref[st, 0:1, :]
            num = _dot(s_mat.astype(BF16), vh) + inter * _dot(qb16, c_old.astype(BF16))
            den = jnp.sum(s_mat, axis=1, keepdims=True) + inter * jnp.sum(qh * n_old, axis=1, keepdims=True)
            o_ref[:, h * ML_DV:(h + 1) * ML_DV] = num / jnp.maximum(jnp.abs(den), jnp.exp(-m_t))
            w_log = total - b_col + ig_col
            m_new = jnp.maximum(total + m_old, jnp.max(w_log, axis=0, keepdims=True))
            w = jnp.exp(w_log - m_new)
            decay = jnp.exp(total + m_old - m_new)
            kw = kh * w
            c_ref[st] = decay * c_old + _dot(kw.T.astype(BF16), vh)
            n_ref[st, 0:1, :] = decay * n_old + jnp.sum(kw, axis=0, keepdims=True)
            m_ref[st] = jnp.broadcast_to(m_new, m_ref.shape[1:])


def _mlstm(st, proj, gate_bias):
    B = st.B
    assert st.L % ML_TC == 0 and st.C % ML_TC == 0
    cc = st.C // ML_TC
    nch = st.ltot // ML_TC
    qw = ML_HEADS * ML_DQK
    vw = ML_HEADS * ML_DV
    gcol = (2 * qw + 2 * vw) // 128

    def rb(d, b, k):
        chunk = k if d == 0 else jnp.where(k < cc, cc - 1 - k, nch - 1 - (k - cc))
        return b * nch + chunk

    def specs(d):
        return [pl.BlockSpec((ML_TC, qw), lambda b, k: (rb(d, b, k), 0)),
                pl.BlockSpec((ML_TC, qw), lambda b, k: (rb(d, b, k), 1)),
                pl.BlockSpec((ML_TC, vw), lambda b, k: (rb(d, b, k), (2 * qw) // vw)),
                pl.BlockSpec((ML_TC, 128), lambda b, k: (rb(d, b, k), gcol))]

    nst = 2 * ML_HEADS
    out = jax.ShapeDtypeStruct((st.NT, vw), F32)
    return pl.pallas_call(
        _mlstm_kernel,
        grid=(B, nch),
        in_specs=specs(0) + specs(1) + [pl.BlockSpec((1, 128), lambda b, k: (0, 0))],
        out_specs=[pl.BlockSpec((ML_TC, vw), lambda b, k: (rb(0, b, k), 0)),
                   pl.BlockSpec((ML_TC, vw), lambda b, k: (rb(1, b, k), 0))],
        out_shape=[out, out],
        scratch_shapes=[pltpu.VMEM((nst, ML_DQK, ML_DV), F32),
                        pltpu.VMEM((nst, 8, ML_DQK), F32),
                        pltpu.VMEM((nst, 8, 128), F32)],
        compiler_params=_cparams(("arbitrary", "arbitrary")),
        name="mlstm_chunks",
    )(proj, proj, proj, proj, proj, proj, proj, proj, gate_bias)


def _route(lg):
    lane_i = lax.broadcasted_iota(jnp.int32, lg.shape, 1)
    lane = lane_i.astype(F32)
    neg = -jnp.inf
    gl = jnp.where(lane_i < MOE_GROUPS, lg, neg)
    gmax = jnp.max(gl, axis=-1, keepdims=True)
    gsum = jnp.sum(jnp.where(lane_i < MOE_GROUPS, jnp.exp(lg - gmax), 0.0), axis=-1, keepdims=True)
    p_top = 1.0 / gsum
    g_sel = jnp.min(jnp.where(gl == gmax, lane, 128.0), axis=-1, keepdims=True)
    group_of_lane = (lane_i >> 3).astype(F32) - 1.0
    el = jnp.where(group_of_lane == g_sel, lg, neg)
    e1 = jnp.max(el, axis=-1, keepdims=True)
    i1 = jnp.min(jnp.where(el == e1, lane, 128.0), axis=-1, keepdims=True)
    el2 = jnp.where(lane == i1, neg, el)
    e2 = jnp.max(el2, axis=-1, keepdims=True)
    i2 = jnp.min(jnp.where(el2 == e2, lane, 128.0), axis=-1, keepdims=True)
    t = jnp.exp(e2 - e1)
    w1 = p_top / (1.0 + t)
    w2 = w1 * t
    id1 = i1 - MOE_GROUPS
    id2 = i2 - MOE_GROUPS
    return jnp.where(lane_i == 0, id1, jnp.where(lane_i == 1, id2,
                                                 jnp.where(lane_i == 2, w1, jnp.where(lane_i == 3, w2, 0.0))))


def _lhs_mla(o_ref):
    return o_ref[...]


def _lhs_mlstm(hf_ref, hb_ref, og_ref, onorm_ref):
    hs = hf_ref[...] + hb_ref[...]
    og = _sigmoid(og_ref[...])
    parts = []
    for h in range(ML_HEADS):
        sl = slice(h * ML_DV, (h + 1) * ML_DV)
        parts.append(_rms(hs[:, sl]) * onorm_ref[:, sl] * og[:, sl])
    return jnp.concatenate(parts, axis=1)


def _mixer_out_kernel(*refs, n_lhs, lhs_fn):
    lhs_refs = refs[:n_lhs]
    w_ref, x_ref, mod_ref, gain_ref, wr_ref, br_ref, xo_ref, f_ref, r_ref = refs[n_lhs:]
    y = _dot(lhs_fn(*lhs_refs).astype(BF16), w_ref[...])
    mod = mod_ref[0]
    xn = x_ref[...] + mod[2:3, :] * y
    xo_ref[...] = xn
    f = _norm_mod(xn, gain_ref[...], mod[3:4, :], mod[4:5, :])
    f_ref[...] = f
    r_ref[...] = _route(_dot3(f, wr_ref[...]) + br_ref[...])


def _mixer_out(rows, lhs_fn, lhs_args, lhs_specs, w_out, x, mod, gain, wr, br, name):
    n = rows.n * TM
    const = lambda k: (0, 0)
    out = lambda w: pl.BlockSpec((TM, w), lambda k: (k, 0))
    return pl.pallas_call(
        functools.partial(_mixer_out_kernel, n_lhs=len(lhs_args), lhs_fn=lhs_fn),
        grid=(rows.n,),
        in_specs=list(lhs_specs) + [
            pl.BlockSpec((D_MODEL, D_MODEL), const),
            pl.BlockSpec((TM, D_MODEL), lambda k: (rows.src(k), 0)),
            pl.BlockSpec((1, 8, D_MODEL), lambda k: (rows.mod(k), 0, 0)),
            pl.BlockSpec((1, D_MODEL), const),
            pl.BlockSpec((D_MODEL, 128), const),
            pl.BlockSpec((1, 128), const)],
        out_specs=[out(D_MODEL), out(D_MODEL), out(128)],
        out_shape=[jax.ShapeDtypeStruct((n, D_MODEL), F32),
                   jax.ShapeDtypeStruct((n, D_MODEL), F32),
                   jax.ShapeDtypeStruct((n, 128), F32)],
        compiler_params=_cparams(("arbitrary",)),
        name=name,
    )(*lhs_args, w_out, x, mod, gain, wr, br)


def _expert_kernel(vb_ref, ve_ref, lo_ref, hi_ref, first_ref, last_ref, tab_ref, nxt_ref, f_hbm, wgu_ref, wd_ref,
                   y_hbm, xbuf, acc, gsem, ssem, *, nblk, n_tok):
    v = pl.program_id(0)
    blk = vb_ref[v]
    slot = blk % 2

    def start_gather(table, s):
        for r in range(MOE_BM):
            tok = lax.shift_right_logical(table[0, 0, r], 1)
            pltpu.make_async_copy(f_hbm.at[pl.ds(tok, 1), :], xbuf.at[s, pl.ds(r, 1), :], gsem.at[s]).start()

    def start_scatter(s):
        for r in range(MOE_BM):
            a = tab_ref[0, 0, r]
            row = (a & 1) * n_tok + lax.shift_right_logical(a, 1)
            pltpu.make_async_copy(acc.at[s, pl.ds(r, 1), :], y_hbm.at[pl.ds(row, 1), :], ssem.at[s]).start()

    def wait_gather(s):
        pltpu.make_async_copy(f_hbm.at[pl.ds(0, MOE_BM), :], xbuf.at[s], gsem.at[s]).wait()

    def wait_scatter(s):
        pltpu.make_async_copy(acc.at[s], y_hbm.at[pl.ds(0, MOE_BM), :], ssem.at[s]).wait()

    def per_slot(fn):
        for s in range(2):
            pl.when(slot == s)(functools.partial(fn, s))

    @pl.when(first_ref[v] == 1)
    def _():
        @pl.when(blk == 0)
        def _():
            start_gather(tab_ref, 0)

        def on_first(s):
            @pl.when(blk + 1 < nblk)
            def _():
                start_gather(nxt_ref, 1 - s)

            @pl.when(blk >= 2)
            def _():
                wait_scatter(s)
            acc[s] = jnp.zeros((MOE_BM, D_MODEL), F32)
            wait_gather(s)
        per_slot(on_first)

    @pl.when(hi_ref[v] > lo_ref[v])
    def _():
        def compute(s):
            x = xbuf[s].astype(BF16)
            gu = _dot(x, wgu_ref[0, 0].astype(BF16))
            gate = gu[:, :MOE_FF]
            act = gate * _sigmoid(gate) * gu[:, MOE_FF:]
            y = _dot(act.astype(BF16), wd_ref[0, 0].astype(BF16))
            r = lax.broadcasted_iota(jnp.int32, (MOE_BM, 1), 0)
            mine = jnp.logical_and(r >= lo_ref[v], r < hi_ref[v])
            acc[s] += jnp.where(mine, y, 0.0)
        per_slot(compute)

    @pl.when(last_ref[v] == 1)
    def _():
        def on_last(s):
            start_scatter(s)

            @pl.when(blk == nblk - 1)
            def _():
                if nblk >= 2:
                    wait_scatter(1 - s)
                wait_scatter(s)
        per_slot(on_last)


def _expert_ffn(f, table, visits, layer, w_gate_up, w_down):
    n_tok = f.shape[0]
    nblk = table.shape[0]
    nvis = visits[0].shape[0]
    blk_idx = lambda v, vb, ve, lo, hi, fi, la: (vb[v], 0, 0)
    nxt_idx = lambda v, vb, ve, lo, hi, fi, la: (jnp.minimum(vb[v] + 1, nblk - 1), 0, 0)
    exp_idx = lambda v, vb, ve, lo, hi, fi, la: (layer, ve[v], 0, 0)
    grid_spec = pltpu.PrefetchScalarGridSpec(
        num_scalar_prefetch=6,
        grid=(nvis,),
        in_specs=[pl.BlockSpec((1, 1, MOE_BM), blk_idx, memory_space=pltpu.SMEM),
                  pl.BlockSpec((1, 1, MOE_BM), nxt_idx, memory_space=pltpu.SMEM),
                  pl.BlockSpec(memory_space=pl.ANY),
                  pl.BlockSpec((1, 1, D_MODEL, 2 * MOE_FF), exp_idx),
                  pl.BlockSpec((1, 1, MOE_FF, D_MODEL), exp_idx)],
        out_specs=pl.BlockSpec(memory_space=pl.ANY),
        scratch_shapes=[pltpu.VMEM((2, MOE_BM, D_MODEL), F32),
                        pltpu.VMEM((2, MOE_BM, D_MODEL), F32),
                        pltpu.SemaphoreType.DMA((2,)),
                        pltpu.SemaphoreType.DMA((2,))],
    )
    return pl.pallas_call(
        functools.partial(_expert_kernel, nblk=nblk, n_tok=n_tok),
        grid_spec=grid_spec,
        out_shape=jax.ShapeDtypeStruct((MOE_TOPK * n_tok, D_MODEL), F32),
        compiler_params=_cparams(("arbitrary",)),
        name="moe_expert_ffn",
    )(*visits, table, table, f, w_gate_up, w_down)


def _combine_kernel(x_ref, *refs):
    y_refs = refs[:MOE_TOPK]
    r_ref, mod_ref, o_ref = refs[MOE_TOPK:]
    w = r_ref[...]
    y = w[:, MOE_TOPK:MOE_TOPK + 1] * y_refs[0][...]
    for k in range(1, MOE_TOPK):
        y = y + w[:, MOE_TOPK + k:MOE_TOPK + k + 1] * y_refs[k][...]
    o_ref[...] = x_ref[...] + mod_ref[0][5:6, :] * y


def _combine(rows, x, y, route, mod):
    spec = pl.BlockSpec((TM, D_MODEL), lambda i: (i, 0))
    y_specs = [pl.BlockSpec((TM, D_MODEL), functools.partial(lambda i, k: (k * rows.n + i, 0), k=k))
               for k in range(MOE_TOPK)]
    return pl.pallas_call(
        _combine_kernel,
        grid=(rows.n,),
        in_specs=[spec] + y_specs + [pl.BlockSpec((TM, 128), lambda i: (i, 0)),
                                     pl.BlockSpec((1, 8, D_MODEL), lambda i: (rows.mod(i), 0, 0))],
        out_specs=spec,
        out_shape=jax.ShapeDtypeStruct((rows.n * TM, D_MODEL), F32),
        compiler_params=_cparams(("arbitrary",)),
        name="moe_combine",
    )(x, *([y] * MOE_TOPK), route, mod)


def _visit_tables(e_sorted, nk):
    E = MOE_EXPERTS
    nblk = nk // MOE_BM
    nvis = nblk + E
    bounds = jnp.searchsorted(e_sorted, jnp.arange(E + 1, dtype=jnp.int32), side='left').astype(jnp.int32)
    starts, ends = bounds[:-1], bounds[1:]
    fb = starts // MOE_BM
    nv = jnp.where(ends > starts, (ends - 1) // MOE_BM - fb + 1, 0)
    cum = jnp.cumsum(nv)
    total = cum[-1]
    v = jnp.arange(nvis, dtype=jnp.int32)
    active = v < total
    vc = jnp.minimum(v, total - 1)
    ve = jnp.minimum(jnp.searchsorted(cum, vc, side='right'), E - 1).astype(jnp.int32)
    vb = fb[ve] + (vc - (cum - nv)[ve])
    lo = jnp.where(active, jnp.maximum(starts[ve], vb * MOE_BM) - vb * MOE_BM, 0)
    hi = jnp.where(active, jnp.minimum(ends[ve], (vb + 1) * MOE_BM) - vb * MOE_BM, 0)
    prev_b = jnp.concatenate([jnp.full((1,), -1, jnp.int32), vb[:-1]])
    next_b = jnp.concatenate([vb[1:], jnp.full((1,), -1, jnp.int32)])
    first = jnp.logical_and(active, vb != prev_b)
    last = jnp.logical_and(active, jnp.logical_or(vb != next_b, v == total - 1))
    i32 = lambda a: a.astype(jnp.int32)
    return i32(vb), i32(ve), i32(lo), i32(hi), i32(first), i32(last)


def _moe(rows, x, f, route, mod, layer, w_gate_up, w_down):
    n = rows.n * TM
    nk = n * MOE_TOPK
    e_flat = route[:, :MOE_TOPK].astype(jnp.int32).reshape(nk)
    e_sorted, order = lax.sort((e_flat, jnp.arange(nk, dtype=jnp.int32)), num_keys=1, is_stable=True)
    visits = _visit_tables(e_sorted, nk)
    y = _expert_ffn(f, order.reshape(nk // MOE_BM, 1, MOE_BM), visits, layer, w_gate_up, w_down)
    return _combine(rows, x, y, route, mod)


def kernel(x, c, ctx, c_ctx, ada_w, ada_b, norm_mix, norm_ffn, rg_w_in, rg_conv_w, rg_conv_b, rg_gate_w, rg_gate_b, rg_lambda, rg_w_out, mla_w_down, mla_q_norm, mla_kv_norm, mla_w_uq, mla_w_ukv, mla_qk_norm, mla_w_o, ml_w_in, ml_gate_b, ml_out_norm, ml_w_out, moe_w_group, moe_b_group, moe_w_expert, moe_b_expert, moe_w_gate_up, moe_w_down):
    B, L, D = x.shape
    C = ctx.shape[1]
    depth = ada_w.shape[0]
    assert D == D_MODEL
    st = _Stream(B, L, C)

    xs = jnp.concatenate([ctx, x], axis=1).reshape(st.NT, D)
    cc = jnp.zeros((16, D), F32).at[:B].set(c).at[B].set(c_ctx)
    mod_all = _modulation(cc, ada_w, ada_b)
    mod_all = jnp.pad(mod_all[:, :B + 1].reshape(depth, B + 1, 6, D), ((0, 0), (0, 0), (0, 2), (0, 0)))
    perm = _row_permutation(B)

    row = lambda a: a.reshape(1, -1)

    for i in range(depth):
        last = i == depth - 1
        mod = mod_all[i]
        kind, j = i % 3, i // 3
        all_rows = st.all_rows()
        out_rows = st.latent_rows() if last else all_rows
        tile_spec = lambda w: pl.BlockSpec((TM, w), lambda k: (out_rows.src(k), 0))
        wr = jnp.zeros((D, 128), F32).at[:, :MOE_GROUPS].set(moe_w_group[i]) \
            .at[:, MOE_GROUPS:MOE_GROUPS + MOE_EXPERTS].set(moe_w_expert[i])
        br = jnp.zeros((1, 128), F32).at[0, :MOE_GROUPS].set(moe_b_group[i]) \
            .at[0, MOE_GROUPS:MOE_GROUPS + MOE_EXPERTS].set(moe_b_expert[i])
        out_args = (xs, mod, row(norm_ffn[i]), wr, br)

        if kind == 0:
            gate, u = _rg_in(st, xs, mod, row(norm_mix[i]), perm, rg_w_in[j].astype(BF16))
            wg, gb = _rg_gate_weights(rg_gate_w[j], rg_gate_b[j])
            hs = _rg_scan(st, u, rg_conv_w[j], row(rg_conv_b[j]), wg, gb, rg_lambda[j].reshape(2, 1, D))
            xs, f, route = _rg_out(st, last, gate, hs, perm.T, rg_w_out[j].astype(BF16), *out_args)
        elif kind == 1:
            w_down = jnp.pad(mla_w_down[j], ((0, 0), (0, 512 - mla_w_down.shape[2]))).astype(BF16)
            down = _norm_proj(all_rows, xs, mod, row(norm_mix[i]), w_down, name="mla_down_proj")
            wq, wk, we, wv, gq, gk = _mla_weights(mla_w_uq[j], mla_w_ukv[j], mla_qk_norm[j])
            cos_t, sin_t = _rope_tables(L)
            q, k, v = _mla_up(st, all_rows, down, row(mla_q_norm[j]), row(mla_kv_norm[j]), wq, wk, we, wv, gq, gk,
                              cos_t, sin_t)
            o = _attention(st, q, k, v)
            xs, f, route = _mixer_out(out_rows, _lhs_mla, (o,), (tile_spec(D),),
                                      mla_w_o[j].astype(BF16), *out_args, name="mla_out")
        else:
            n_in = ml_w_in.shape[2]
            w_in = jnp.pad(ml_w_in[j], ((0, 0), (0, ML_NP - n_in))).astype(BF16)
            proj = _norm_proj(all_rows, xs, mod, row(norm_mix[i]), w_in, name="mlstm_in_proj")
            gate_bias = jnp.pad(ml_gate_b[j].reshape(1, -1), ((0, 0), (0, 128 - 4 * ML_HEADS)))
            hf, hb = _mlstm(st, proj, gate_bias)
            og_spec = pl.BlockSpec((TM, D), lambda k: (out_rows.src(k), 2))
            xs, f, route = _mixer_out(out_rows, _lhs_mlstm, (hf, hb, proj, row(ml_out_norm[j])),
                                      (tile_spec(D), tile_spec(D), og_spec, pl.BlockSpec((1, D), lambda k: (0, 0))),
                                      ml_w_out[j].astype(BF16), *out_args, name="mlstm_out")

        moe_rows = st.dense_latent_rows() if last else all_rows
        xs = _moe(moe_rows, xs, f, route, mod, i, moe_w_gate_up, moe_w_down)

    return xs.reshape(B, L, D)
```

```python
import functools
import math

import jax
import jax.numpy as jnp
from jax import lax
from jax.experimental import pallas as pl
from jax.experimental.pallas import tpu as pltpu

F32 = jnp.float32
BF16 = jnp.bfloat16

D_MODEL = 1024
RMS_EPS = 1e-6

TM = 256
VMEM_LIMIT = 48 * 1024 * 1024

RG_BLOCK_W = 64
RG_CHUNK = 256
RG_CONV_W = 4
RG_C = 8.0
RG_TT = 64

MLA_HEADS = 16
MLA_Q_RANK = 256
MLA_KV_RANK = 128
MLA_NOPE = 64
MLA_ROPE = 32
MLA_V = 64
MLA_QK = MLA_NOPE + MLA_ROPE
MLA_HP = 128
ROPE_AXIS_DIM = MLA_ROPE // 2
ROPE_BASE = 10000.0
GRID_W = 64
ATT_HEADS = 4

ML_HEADS = 4
ML_DV = 256
ML_DQK = 128
ML_TC = 256
ML_M_INIT = -1e30
ML_NP = 3200

MOE_GROUPS = 8
MOE_PER_GROUP = 8
MOE_EXPERTS = 64
MOE_TOPK = 2
MOE_FF = 256
MOE_BM = 256


def _cparams(sem):
    return pltpu.CompilerParams(dimension_semantics=sem, vmem_limit_bytes=VMEM_LIMIT)


def _dot(a, b):
    return jnp.dot(a, b, preferred_element_type=F32)


def _dot_t(a, b):
    return lax.dot_general(a, b, (((1,), (1,)), ((), ())), preferred_element_type=F32)


def _dot3(a, b):
    ah = a.astype(BF16)
    al = (a - ah.astype(F32)).astype(BF16)
    bh = b.astype(BF16)
    bl = (b - bh.astype(F32)).astype(BF16)
    return _dot(ah, bh) + (_dot(al, bh) + _dot(ah, bl))


def _sigmoid(x):
    return 0.5 * jnp.tanh(0.5 * x) + 0.5


def _softplus(x):
    return jnp.maximum(x, 0.0) + jnp.log1p(jnp.exp(-jnp.abs(x)))


def _gelu_tanh(x):
    return 0.5 * x * (1.0 + jnp.tanh(0.7978845608028654 * (x + 0.044715 * (x * x * x))))


def _rms(x, n=None):
    n = x.shape[-1] if n is None else n
    ms = jnp.sum(x * x, axis=-1, keepdims=True) * (1.0 / n)
    return x * lax.rsqrt(ms + RMS_EPS)


class _Rows:
    def __init__(self, n, src, mod):
        self.n, self.src, self.mod = n, src, mod


class _Stream:
    def __init__(self, B, L, C):
        assert L % TM == 0 and C % TM == 0 and B % 8 == 0
        self.B, self.L, self.C = B, L, C
        self.ltot = L + C
        self.lt, self.ct = L // TM, C // TM
        self.tpb = self.lt + self.ct
        self.NT = B * self.ltot

    def all_rows(self):
        tpb, ct, B = self.tpb, self.ct, self.B
        return _Rows(B * tpb, lambda k: k, lambda k: jnp.where(k % tpb < ct, B, k // tpb))

    def latent_rows(self):
        tpb, ct, lt = self.tpb, self.ct, self.lt
        return _Rows(self.B * lt, lambda k: (k // lt) * tpb + ct + k % lt, lambda k: k // lt)

    def dense_latent_rows(self):
        lt = self.lt
        return _Rows(self.B * lt, lambda k: k, lambda k: k // lt)


def _mod_kernel(c_ref, w_ref, b_ref, o_ref):
    c = c_ref[...]
    o_ref[0] = _dot3(c * _sigmoid(c), w_ref[0]) + b_ref[0]


def _modulation(cc, ada_w, ada_b):
    depth, d, n = ada_w.shape
    tn = 1536
    return pl.pallas_call(
        _mod_kernel,
        grid=(depth, n // tn),
        in_specs=[pl.BlockSpec((16, d), lambda l, j: (0, 0)),
                  pl.BlockSpec((1, d, tn), lambda l, j: (l, 0, j)),
                  pl.BlockSpec((1, 1, tn), lambda l, j: (l, 0, j))],
        out_specs=pl.BlockSpec((1, 16, tn), lambda l, j: (l, 0, j)),
        out_shape=jax.ShapeDtypeStruct((depth, 16, n), F32),
        compiler_params=_cparams(("arbitrary", "arbitrary")),
        name="ada_modulation",
    )(cc, ada_w, ada_b.reshape(depth, 1, n))


def _norm_mod(x, gain, shift, scale):
    return _rms(x) * gain * (1.0 + scale) + shift


def _norm_proj_kernel(x_ref, mod_ref, g_ref, w_ref, o_ref):
    mod = mod_ref[0]
    h = _norm_mod(x_ref[...], g_ref[...], mod[0:1, :], mod[1:2, :]).astype(BF16)
    o_ref[...] = _dot(h, w_ref[...]).astype(o_ref.dtype)


def _norm_proj(rows, x, mod, gain, w, name):
    n = w.shape[1]
    const = lambda k: (0, 0)
    return pl.pallas_call(
        _norm_proj_kernel,
        grid=(rows.n,),
        in_specs=[pl.BlockSpec((TM, D_MODEL), lambda k: (rows.src(k), 0)),
                  pl.BlockSpec((1, 8, D_MODEL), lambda k: (rows.mod(k), 0, 0)),
                  pl.BlockSpec((1, D_MODEL), const),
                  pl.BlockSpec((D_MODEL, n), const)],
        out_specs=pl.BlockSpec((TM, n), lambda k: (k, 0)),
        out_shape=jax.ShapeDtypeStruct((rows.n * TM, n), F32),
        compiler_params=_cparams(("arbitrary",)),
        name=name,
    )(x, mod, gain, w)


RG_TR = 32


def _row_permutation(B):
    n = B * RG_TR
    r = jnp.arange(n)
    src = (r % B) * RG_TR + r // B
    return (src[:, None] == jnp.arange(n)[None, :]).astype(BF16)


def _time_tile_mod(mod_ref, is_ctx, B, row):
    return jnp.where(is_ctx, mod_ref[B:B + 1, row:row + 1, :], mod_ref[0:B, row:row + 1, :])


def _rg_in_kernel(x_ref, mod_ref, g_ref, perm_ref, wg_ref, wu_ref, gate_ref, u_ref, *, B, ctx_tiles):
    is_ctx = pl.program_id(0) < ctx_tiles
    h = _norm_mod(x_ref[...], g_ref[...], _time_tile_mod(mod_ref, is_ctx, B, 0), _time_tile_mod(mod_ref, is_ctx, B, 1))
    h = h.reshape(B * RG_TR, D_MODEL).astype(BF16)
    h = _dot(perm_ref[...], h).astype(BF16)
    gate_ref[...] = _gelu_tanh(_dot(h, wg_ref[...])).astype(gate_ref.dtype)
    u_ref[...] = _dot(h, wu_ref[...])


def _rg_in(st, x, mod, gain, perm, w_in):
    B, W = st.B, D_MODEL
    R = B * RG_TR
    nt = st.ltot // RG_TR
    const = lambda t: (0, 0)
    return pl.pallas_call(
        functools.partial(_rg_in_kernel, B=B, ctx_tiles=st.C // RG_TR),
        grid=(nt,),
        in_specs=[pl.BlockSpec((B, RG_TR, D_MODEL), lambda t: (0, t, 0)),
                  pl.BlockSpec((B + 1, 8, D_MODEL), lambda t: (0, 0, 0)),
                  pl.BlockSpec((1, D_MODEL), const),
                  pl.BlockSpec((R, R), const),
                  pl.BlockSpec((D_MODEL, W), const),
                  pl.BlockSpec((D_MODEL, W), lambda t: (0, 1))],
        out_specs=[pl.BlockSpec((R, W), lambda t: (t, 0)),
                   pl.BlockSpec((R, W), lambda t: (t, 0))],
        out_shape=[jax.ShapeDtypeStruct((st.ltot * B, W), BF16), jax.ShapeDtypeStruct((st.ltot * B, W), F32)],
        compiler_params=_cparams(("arbitrary",)),
        name="rg_in_proj",
    )(x.reshape(B, st.ltot, D_MODEL), mod, gain, perm, w_in, w_in)


def _rg_tile_order(d, k, ct, ntt):
    bwd = jnp.where(k < ct, ct - 1 - k, ntt - 1 - (k - ct))
    return jnp.where(d == 0, k, bwd)


def _rg_scan_kernel(um_ref, up_ref, un_ref, cw_ref, cb_ref, wg_ref, gb_ref, lam_ref, o_ref,
                    ext_ref, a_ref, b_ref, h_ref, *, B, ct, ntt):
    d = pl.program_id(0)
    k = pl.program_id(1)
    tile = _rg_tile_order(d, k, ct, ntt)
    R = RG_TT * B

    @pl.when(k == 0)
    def _():
        h_ref[...] = jnp.zeros_like(h_ref)

    seq_start = jnp.logical_or(tile == 0, tile == ct)
    seq_end = jnp.logical_or(tile == ct - 1, tile == ntt - 1)
    ext_ref[0:2 * B, :] = jnp.where(seq_start, 0.0, up_ref[...])
    ext_ref[2 * B:2 * B + R, :] = um_ref[...]
    ext_ref[2 * B + R:3 * B + R, :] = jnp.where(seq_end, 0.0, un_ref[...])
    cw = cw_ref[...]
    uc = cb_ref[...] + cw[0:1, :] * ext_ref[0:R, :]
    for j in range(1, RG_CONV_W):
        uc = uc + cw[j:j + 1, :] * ext_ref[j * B:j * B + R, :]
    ucb = uc.astype(BF16)
    c_lam = -RG_C * _softplus(-lam_ref[0])
    for c in range(D_MODEL // RG_CHUNK):
        sl = slice(c * RG_CHUNK, (c + 1) * RG_CHUNK)
        z = _dot(ucb[:, sl], wg_ref[0, c]) + gb_ref[0, c]
        r = _sigmoid(z[:, :RG_CHUNK])
        ig = _sigmoid(z[:, RG_CHUNK:])
        log_a = c_lam[:, sl] * r
        a = jnp.exp(log_a)
        one_minus_a2 = -jnp.tanh(log_a) * (a * a + 1.0)
        a_ref[:, sl] = a
        b_ref[:, sl] = jnp.sqrt(one_minus_a2) * (ig * uc[:, sl])

    def scan(times):
        for c in range(D_MODEL // 128):
            cs = slice(c * 128, (c + 1) * 128)
            h = h_ref[:, cs]
            for t in times:
                rs = slice(t * B, (t + 1) * B)
                h = a_ref[rs, cs] * h + b_ref[rs, cs]
                b_ref[rs, cs] = h
            h_ref[:, cs] = h

    pl.when(d == 0)(lambda: scan(range(RG_TT)))
    pl.when(d == 1)(lambda: scan(range(RG_TT - 1, -1, -1)))
    o_ref[0] = b_ref[...].astype(o_ref.dtype)


def _rg_scan(st, u_tm, conv_w, conv_b, wg, gb, lam):
    B, W = st.B, D_MODEL
    assert st.C % RG_TT == 0 and st.L % RG_TT == 0
    ltot = st.ltot
    ntt, ct = ltot // RG_TT, st.C // RG_TT
    R = RG_TT * B
    order = functools.partial(_rg_tile_order, ct=ct, ntt=ntt)
    nch = W // RG_CHUNK
    return pl.pallas_call(
        functools.partial(_rg_scan_kernel, B=B, ct=ct, ntt=ntt),
        grid=(2, ntt),
        in_specs=[pl.BlockSpec((R, W), lambda d, k: (order(d, k), 0)),
                  pl.BlockSpec((2 * B, W), lambda d, k: (jnp.maximum(order(d, k) * (RG_TT // 2) - 1, 0), 0)),
                  pl.BlockSpec((B, W), lambda d, k: (jnp.minimum((order(d, k) + 1) * RG_TT, ltot - 1), 0)),
                  pl.BlockSpec((RG_CONV_W, W), lambda d, k: (0, 0)),
                  pl.BlockSpec((1, W), lambda d, k: (0, 0)),
                  pl.BlockSpec((1, nch, RG_CHUNK, 2 * RG_CHUNK), lambda d, k: (d, 0, 0, 0)),
                  pl.BlockSpec((1, nch, 1, 2 * RG_CHUNK), lambda d, k: (d, 0, 0, 0)),
                  pl.BlockSpec((1, 1, W), lambda d, k: (d, 0, 0))],
        out_specs=pl.BlockSpec((1, R, W), lambda d, k: (d, order(d, k), 0)),
        out_shape=jax.ShapeDtypeStruct((2, ltot * B, W), BF16),
        scratch_shapes=[pltpu.VMEM((3 * B + R, W), F32),
                        pltpu.VMEM((R, W), F32),
                        pltpu.VMEM((R, W), F32),
                        pltpu.VMEM((B, W), F32)],
        compiler_params=_cparams(("arbitrary", "arbitrary")),
        name="rg_scan",
    )(u_tm, u_tm, u_tm, conv_w, conv_b, wg, gb, lam)


def _rg_gate_weights(gate_w, gate_b):
    nb = gate_w.shape[2]
    per = RG_CHUNK // RG_BLOCK_W
    nch = nb // per
    gw = gate_w.reshape(2, 2, nch, per, RG_BLOCK_W, RG_BLOCK_W)
    eye = jnp.eye(per, dtype=gate_w.dtype)
    bd = jnp.einsum('dgcnij,nm->dgcnimj', gw, eye).reshape(2, 2, nch, RG_CHUNK, RG_CHUNK)
    wg = jnp.concatenate([bd[:, 0], bd[:, 1]], axis=-1).astype(BF16)
    gb = gate_b.reshape(2, 2, nch, 1, RG_CHUNK)
    gb = jnp.concatenate([gb[:, 0], gb[:, 1]], axis=-1)
    return wg, gb


def _rg_out_kernel(g_ref, hf_ref, hb_ref, perm_ref, w_ref, x_ref, mod_ref, gain_ref, wr_ref, br_ref,
                   xo_ref, f_ref, r_ref, *, B, ctx_tiles, t0):
    is_ctx = pl.program_id(0) + t0 < ctx_tiles
    hsum = hf_ref[0].astype(F32) + hb_ref[0].astype(F32)
    lhs = (g_ref[...].astype(F32) * hsum).astype(BF16)
    lhs = _dot(perm_ref[...], lhs).astype(BF16)
    y = _dot(lhs, w_ref[...]).reshape(B, RG_TR, D_MODEL)
    m = lambda row: _time_tile_mod(mod_ref, is_ctx, B, row)
    xn = x_ref[...] + m(2) * y
    xo_ref[...] = xn
    f = _norm_mod(xn, gain_ref[...], m(3), m(4))
    f_ref[...] = f
    lg = _dot3(f.reshape(B * RG_TR, D_MODEL), wr_ref[...]) + br_ref[...]
    r_ref[...] = _route(lg).reshape(B, RG_TR, 128)


def _rg_out(st, latent_only, gate, hs, perm, w_out, x, mod, gain, wr, br):
    B = st.B
    R = B * RG_TR
    t0 = st.C // RG_TR if latent_only else 0
    lo = st.L if latent_only else st.ltot
    nt = lo // RG_TR
    const = lambda t: (0, 0)
    blk = lambda w: pl.BlockSpec((B, RG_TR, w), lambda t: (0, t, 0))
    xo, f, route = pl.pallas_call(
        functools.partial(_rg_out_kernel, B=B, ctx_tiles=st.C // RG_TR, t0=t0),
        grid=(nt,),
        in_specs=[pl.BlockSpec((R, D_MODEL), lambda t: (t + t0, 0)),
                  pl.BlockSpec((1, R, D_MODEL), lambda t: (0, t + t0, 0)),
                  pl.BlockSpec((1, R, D_MODEL), lambda t: (1, t + t0, 0)),
                  pl.BlockSpec((R, R), const),
                  pl.BlockSpec((D_MODEL, D_MODEL), const),
                  pl.BlockSpec((B, RG_TR, D_MODEL), lambda t: (0, t + t0, 0)),
                  pl.BlockSpec((B + 1, 8, D_MODEL), lambda t: (0, 0, 0)),
                  pl.BlockSpec((1, D_MODEL), const),
                  pl.BlockSpec((D_MODEL, 128), const),
                  pl.BlockSpec((1, 128), const)],
        out_specs=[blk(D_MODEL), blk(D_MODEL), blk(128)],
        out_shape=[jax.ShapeDtypeStruct((B, lo, D_MODEL), F32),
                   jax.ShapeDtypeStruct((B, lo, D_MODEL), F32),
                   jax.ShapeDtypeStruct((B, lo, 128), F32)],
        compiler_params=_cparams(("arbitrary",)),
        name="rg_out",
    )(gate, hs, hs, perm, w_out, x.reshape(B, st.ltot, D_MODEL), mod, gain, wr, br)
    return xo.reshape(B * lo, D_MODEL), f.reshape(B * lo, D_MODEL), route.reshape(B * lo, 128)


_ROPE_HALF = ROPE_AXIS_DIM // 2
_MLA_SRC_DIM = (list(range(MLA_NOPE + ROPE_AXIS_DIM)) + list(range(MLA_NOPE, MLA_NOPE + _ROPE_HALF))
                + list(range(MLA_NOPE + ROPE_AXIS_DIM, MLA_QK))
                + list(range(MLA_NOPE + ROPE_AXIS_DIM, MLA_NOPE + ROPE_AXIS_DIM + _ROPE_HALF)))
_MLA_REAL_LANE = ([1.0] * (MLA_NOPE + ROPE_AXIS_DIM) + [0.0] * _ROPE_HALF + [1.0] * ROPE_AXIS_DIM
                  + [0.0] * _ROPE_HALF + [0.0] * (MLA_HP - len(_MLA_SRC_DIM)))


def _mla_up_kernel(dn_ref, qn_ref, kvn_ref, wq_ref, wk_ref, we_ref, wv_ref, real_ref, cq_ref, sq_ref, ck_ref, sk_ref,
                   q_ref, k_ref, v_ref):
    dn = dn_ref[...]
    cq = _rms(dn[:, :MLA_Q_RANK]) * qn_ref[...]
    ckv = _rms(dn[:, MLA_Q_RANK:MLA_Q_RANK + MLA_KV_RANK]) * kvn_ref[...]
    kr = dn[:, MLA_Q_RANK + MLA_KV_RANK:]
    kr_hi = kr.astype(BF16)
    kr_lo = (kr - kr_hi.astype(F32)).astype(BF16)
    ckvb = ckv.astype(BF16)
    q_pre = _dot(cq.astype(BF16), wq_ref[...])
    k_pre = _dot(ckvb, wk_ref[...]) + (_dot(kr_hi, we_ref[...]) + _dot(kr_lo, we_ref[...]))
    v_ref[...] = _dot(ckvb, wv_ref[...]).astype(v_ref.dtype)
    real = real_ref[...]

    def head(x, cos_g, sin_g):
        ms = jnp.sum(x * x * real, axis=-1, keepdims=True) * (1.0 / MLA_QK)
        xr = x * lax.rsqrt(ms + RMS_EPS)
        return xr * cos_g + pltpu.roll(xr, MLA_HP - _ROPE_HALF, 1) * sin_g

    cos_q, sin_q, cos_k, sin_k = cq_ref[...], sq_ref[...], ck_ref[...], sk_ref[...]
    for h in range(MLA_HEADS):
        sl = slice(h * MLA_HP, (h + 1) * MLA_HP)
        q_ref[:, sl] = head(q_pre[:, sl], cos_q, sin_q).astype(q_ref.dtype)
        k_ref[:, sl] = head(k_pre[:, sl], cos_k, sin_k).astype(k_ref.dtype)


def _mla_up(st, rows, down, q_norm, kv_norm, wq, wk, we, wv, tables):
    hw = MLA_HEADS * MLA_HP
    const = lambda i: (0, 0)
    tpb, ct, lt = st.tpb, st.ct, st.lt
    rope_idx = lambda i: (jnp.where(i % tpb < ct, lt, i % tpb - ct), 0)
    real = jnp.asarray(_MLA_REAL_LANE, F32).reshape(1, MLA_HP)
    return pl.pallas_call(
        _mla_up_kernel,
        grid=(rows.n,),
        in_specs=[pl.BlockSpec((TM, 512), lambda i: (i, 0)),
                  pl.BlockSpec((1, MLA_Q_RANK), const),
                  pl.BlockSpec((1, MLA_KV_RANK), const),
                  pl.BlockSpec((MLA_Q_RANK, hw), const),
                  pl.BlockSpec((MLA_KV_RANK, hw), const),
                  pl.BlockSpec((128, hw), const),
                  pl.BlockSpec((MLA_KV_RANK, MLA_HEADS * MLA_V), const),
                  pl.BlockSpec((1, MLA_HP), const)] + [pl.BlockSpec((TM, MLA_HP), rope_idx)] * 4,
        out_specs=[pl.BlockSpec((TM, hw), lambda i: (i, 0)),
                   pl.BlockSpec((TM, hw), lambda i: (i, 0)),
                   pl.BlockSpec((TM, MLA_HEADS * MLA_V), lambda i: (i, 0))],
        out_shape=[jax.ShapeDtypeStruct((st.NT, hw), BF16),
                   jax.ShapeDtypeStruct((st.NT, hw), BF16),
                   jax.ShapeDtypeStruct((st.NT, MLA_HEADS * MLA_V), BF16)],
        compiler_params=_cparams(("arbitrary",)),
        name="mla_up_proj",
    )(down, q_norm, kv_norm, wq, wk, we, wv, real, *tables)


def _attn_kernel(q_ref, k_ref, v_ref, o_ref, vaug_ref, *, C, ct):
    qi = pl.program_id(2)

    @pl.when(qi == 0)
    def _():
        lane = lax.broadcasted_iota(jnp.int32, (k_ref.shape[0], 2 * MLA_V), 1)
        for hh in range(ATT_HEADS):
            pair = v_ref[:, (hh // 2) * 2 * MLA_V:(hh // 2 + 1) * 2 * MLA_V].astype(F32)
            if hh % 2 == 0:
                aug = jnp.where(lane < MLA_V, pair, jnp.where(lane == MLA_V, 1.0, 0.0))
            else:
                aug = jnp.where(lane >= MLA_V, pair, jnp.where(lane == 0, 1.0, 0.0))
            vaug_ref[hh] = aug.astype(BF16)

    def attend(nkeys):
        lane = lax.broadcasted_iota(jnp.int32, (TM, 2 * MLA_V), 1)
        for pair in range(ATT_HEADS // 2):
            outs = []
            for hh in (2 * pair, 2 * pair + 1):
                sl = slice(hh * MLA_HP, (hh + 1) * MLA_HP)
                s = _dot_t(q_ref[:, sl], k_ref[0:nkeys, sl])
                p = jnp.exp2(s - jnp.max(s, axis=-1, keepdims=True)).astype(BF16)
                o = _dot(p, vaug_ref[hh, 0:nkeys, :])
                rowsum = o[:, MLA_V:MLA_V + 1] if hh % 2 == 0 else o[:, 0:1]
                outs.append(o * (1.0 / rowsum))
            o_ref[:, pair * 2 * MLA_V:(pair + 1) * 2 * MLA_V] = jnp.where(lane < MLA_V, outs[0], outs[1]).astype(o_ref.dtype)

    pl.when(qi < ct)(lambda: attend(C))
    pl.when(qi >= ct)(lambda: attend(k_ref.shape[0]))


def _attention(st, q, k, v):
    B, ltot, tpb = st.B, st.ltot, st.tpb
    hg = MLA_HEADS // ATT_HEADS
    return pl.pallas_call(
        functools.partial(_attn_kernel, C=st.C, ct=st.ct),
        grid=(B, hg, tpb),
        in_specs=[pl.BlockSpec((TM, ATT_HEADS * MLA_HP), lambda b, h, i: (b * tpb + i, h)),
                  pl.BlockSpec((ltot, ATT_HEADS * MLA_HP), lambda b, h, i: (b, h)),
                  pl.BlockSpec((ltot, ATT_HEADS * MLA_V), lambda b, h, i: (b, h))],
        out_specs=pl.BlockSpec((TM, ATT_HEADS * MLA_V), lambda b, h, i: (b * tpb + i, h)),
        out_shape=jax.ShapeDtypeStruct((st.NT, MLA_HEADS * MLA_V), BF16),
        scratch_shapes=[pltpu.VMEM((ATT_HEADS, ltot, 2 * MLA_V), BF16)],
        compiler_params=_cparams(("arbitrary", "arbitrary", "arbitrary")),
        name="mla_attention",
    )(q, k, v)


def _mla_weights(w_uq, w_ukv):
    H = MLA_HEADS
    src = jnp.asarray(_MLA_SRC_DIM, jnp.int32)
    pad = MLA_HP - len(_MLA_SRC_DIM)
    wq = jnp.pad(w_uq.reshape(MLA_Q_RANK, H, MLA_QK)[:, :, src], ((0, 0), (0, 0), (0, pad)))
    wkv = w_ukv.reshape(MLA_KV_RANK, H, MLA_NOPE + MLA_V)
    wk = jnp.pad(wkv[:, :, :MLA_NOPE], ((0, 0), (0, 0), (0, MLA_HP - MLA_NOPE)))
    wv = wkv[:, :, MLA_NOPE:]
    r = jnp.arange(128)[:, None]
    lane_dim = jnp.pad(src, (0, pad), constant_values=-1)[None, :]
    place = (lane_dim == r + MLA_NOPE).astype(BF16)
    we = jnp.tile(place, (1, H))
    return (wq.reshape(MLA_Q_RANK, H * MLA_HP).astype(BF16), wk.reshape(MLA_KV_RANK, H * MLA_HP).astype(BF16),
            we, wv.reshape(MLA_KV_RANK, H * MLA_V).astype(BF16))


def _rope_tables(L, qk_norm):
    rows = L // GRID_W
    row = jnp.broadcast_to(jnp.arange(rows, dtype=F32)[:, None], (rows, GRID_W)).reshape(L)
    col = jnp.broadcast_to(jnp.arange(GRID_W, dtype=F32)[None, :], (rows, GRID_W)).reshape(L)
    inv_freq = ROPE_BASE ** (-jnp.arange(0, ROPE_AXIS_DIM, 2, dtype=F32) / ROPE_AXIS_DIM)
    ar = row[:, None] * inv_freq
    ac = col[:, None] * inv_freq
    h8 = _ROPE_HALF
    one = jnp.ones((L, MLA_NOPE), F32)
    z8 = jnp.zeros((L, h8), F32)
    zpad = jnp.zeros((L, MLA_HP - len(_MLA_SRC_DIM)), F32)
    cos_t = jnp.concatenate([one, jnp.cos(ar), jnp.cos(ar), z8, jnp.cos(ac), jnp.cos(ac), z8, zpad], axis=1)
    sin_t = jnp.concatenate([0 * one, -jnp.sin(ar), jnp.sin(ar), z8, -jnp.sin(ac), jnp.sin(ac), z8, zpad], axis=1)
    ident = jnp.asarray(_MLA_REAL_LANE, F32)[None, :]
    cos_t = jnp.concatenate([cos_t, jnp.broadcast_to(ident, (TM, MLA_HP))], axis=0)
    sin_t = jnp.concatenate([sin_t, jnp.zeros((TM, MLA_HP), F32)], axis=0)
    src = jnp.asarray(_MLA_SRC_DIM, jnp.int32)
    pad = MLA_HP - len(_MLA_SRC_DIM)
    scale = MLA_QK ** -0.5 * math.log2(math.e)
    tables = []
    for g, s in ((qk_norm[0], scale), (qk_norm[1], 1.0)):
        g_lane = jnp.pad(g[src], (0, pad))
        g_partner = jnp.roll(g_lane, -h8)
        tables += [cos_t * (g_lane * s)[None, :], sin_t * (g_partner * s)[None, :]]
    return tables


def _log_sigmoid(x):
    return jnp.minimum(x, 0.0) - jnp.log1p(jnp.exp(-jnp.abs(x)))


def _mlstm_kernel(qf_ref, kf_ref, vf_ref, gf_ref, qb_ref, kb_ref, vb_ref, gb_ref, bias_ref, tril_ref, triu_ref,
                  of_ref, ob_ref, c_ref, n_ref, m_ref):
    T = ML_TC

    @pl.when(pl.program_id(1) == 0)
    def _():
        c_ref[...] = jnp.zeros_like(c_ref)
        n_ref[...] = jnp.zeros_like(n_ref)
        m_ref[...] = jnp.full(m_ref.shape, ML_M_INIT, F32)

    ti = lax.broadcasted_iota(jnp.int32, (T, T), 0)
    si = lax.broadcasted_iota(jnp.int32, (T, T), 1)
    dirs = ((qf_ref, kf_ref, vf_ref, gf_ref, of_ref, tril_ref), (qb_ref, kb_ref, vb_ref, gb_ref, ob_ref, triu_ref))
    for d, (q_ref, k_ref, v_ref, g_ref, o_ref, tri_ref) in enumerate(dirs):
        tri = (si <= ti) if d == 0 else (si >= ti)
        g = g_ref[...] + bias_ref[...]
        g_t = g.T
        lsg = _log_sigmoid(g)
        lsg_hi = lsg.astype(BF16)
        lsg_lo = (lsg - lsg_hi.astype(F32)).astype(BF16)
        cum = _dot(tri_ref[...], lsg_hi) + _dot(tri_ref[...], lsg_lo)
        cum_t = cum.T
        last = T - 1 if d == 0 else 0
        for h in range(ML_HEADS):
            st = d * ML_HEADS + h
            li, lf_ = (2 * d) * ML_HEADS + h, (2 * d + 1) * ML_HEADS + h
            ig_col = g[:, li:li + 1]
            ig_row = g_t[li:li + 1, :]
            b_col = cum[:, lf_:lf_ + 1]
            b_row = cum_t[lf_:lf_ + 1, :]
            total = cum[last:last + 1, lf_:lf_ + 1]
            m_old = m_ref[st, 0:1, 0:1]
            d_log = jnp.where(tri, b_col - b_row + ig_row, -jnp.inf)
            inter_log = b_col + m_old
            m_t = jnp.maximum(inter_log, jnp.max(d_log, axis=1, keepdims=True))
            qh = q_ref[:, h * ML_DQK:(h + 1) * ML_DQK] * (ML_DQK ** -0.5)
            kh = k_ref[:, h * ML_DQK:(h + 1) * ML_DQK]
            vh = v_ref[:, h * ML_DV:(h + 1) * ML_DV].astype(BF16)
            qb16 = qh.astype(BF16)
            s_mat = _dot_t(qb16, kh.astype(BF16)) * jnp.exp(d_log - m_t)
            inter = jnp.exp(inter_log - m_t)
            c_old = c_ref[st]
            n_old = n_ref[st, 0:1, :]
            num = _dot(s_mat.astype(BF16), vh) + inter * _dot(qb16, c_old.astype(BF16))
            den = jnp.sum(s_mat, axis=1, keepdims=True) + inter * jnp.sum(qh * n_old, axis=1, keepdims=True)
            o_ref[:, h * ML_DV:(h + 1) * ML_DV] = num / jnp.maximum(jnp.abs(den), jnp.exp(-m_t))
            w_log = total - b_col + ig_col
            m_new = jnp.maximum(total + m_old, jnp.max(w_log, axis=0, keepdims=True))
            w = jnp.exp(w_log - m_new)
            decay = jnp.exp(total + m_old - m_new)
            kw = kh * w
            c_ref[st] = decay * c_old + _dot(kw.T.astype(BF16), vh)
            n_ref[st, 0:1, :] = decay * n_old + jnp.sum(kw, axis=0, keepdims=True)
            m_ref[st] = jnp.broadcast_to(m_new, m_ref.shape[1:])


def _mlstm(st, proj, gate_bias):
    B = st.B
    assert st.L % ML_TC == 0 and st.C % ML_TC == 0
    cc = st.C // ML_TC
    nch = st.ltot // ML_TC
    qw = ML_HEADS * ML_DQK
    vw = ML_HEADS * ML_DV
    gcol = (2 * qw + 2 * vw) // 128

    def rb(d, b, k):
        chunk = k if d == 0 else jnp.where(k < cc, cc - 1 - k, nch - 1 - (k - cc))
        return b * nch + chunk

    def specs(d):
        return [pl.BlockSpec((ML_TC, qw), lambda b, k: (rb(d, b, k), 0)),
                pl.BlockSpec((ML_TC, qw), lambda b, k: (rb(d, b, k), 1)),
                pl.BlockSpec((ML_TC, vw), lambda b, k: (rb(d, b, k), (2 * qw) // vw)),
                pl.BlockSpec((ML_TC, 128), lambda b, k: (rb(d, b, k), gcol))]

    nst = 2 * ML_HEADS
    out = jax.ShapeDtypeStruct((st.NT, vw), F32)
    tril = jnp.tril(jnp.ones((ML_TC, ML_TC), BF16))
    return pl.pallas_call(
        _mlstm_kernel,
        grid=(B, nch),
        in_specs=specs(0) + specs(1) + [pl.BlockSpec((1, 128), lambda b, k: (0, 0)),
                                        pl.BlockSpec((ML_TC, ML_TC), lambda b, k: (0, 0)),
                                        pl.BlockSpec((ML_TC, ML_TC), lambda b, k: (0, 0))],
        out_specs=[pl.BlockSpec((ML_TC, vw), lambda b, k: (rb(0, b, k), 0)),
                   pl.BlockSpec((ML_TC, vw), lambda b, k: (rb(1, b, k), 0))],
        out_shape=[out, out],
        scratch_shapes=[pltpu.VMEM((nst, ML_DQK, ML_DV), F32),
                        pltpu.VMEM((nst, 8, ML_DQK), F32),
                        pltpu.VMEM((nst, 8, 128), F32)],
        compiler_params=_cparams(("arbitrary", "arbitrary")),
        name="mlstm_chunks",
    )(proj, proj, proj, proj, proj, proj, proj, proj, gate_bias, tril, tril.T)


def _route(lg):
    lane_i = lax.broadcasted_iota(jnp.int32, lg.shape, 1)
    lane = lane_i.astype(F32)
    neg = -jnp.inf
    gl = jnp.where(lane_i < MOE_GROUPS, lg, neg)
    gmax = jnp.max(gl, axis=-1, keepdims=True)
    gsum = jnp.sum(jnp.where(lane_i < MOE_GROUPS, jnp.exp(lg - gmax), 0.0), axis=-1, keepdims=True)
    p_top = 1.0 / gsum
    g_sel = jnp.min(jnp.where(gl == gmax, lane, 128.0), axis=-1, keepdims=True)
    group_of_lane = (lane_i >> 3).astype(F32) - 1.0
    el = jnp.where(group_of_lane == g_sel, lg, neg)
    e1 = jnp.max(el, axis=-1, keepdims=True)
    i1 = jnp.min(jnp.where(el == e1, lane, 128.0), axis=-1, keepdims=True)
    el2 = jnp.where(lane == i1, neg, el)
    e2 = jnp.max(el2, axis=-1, keepdims=True)
    i2 = jnp.min(jnp.where(el2 == e2, lane, 128.0), axis=-1, keepdims=True)
    t = jnp.exp(e2 - e1)
    w1 = p_top / (1.0 + t)
    w2 = w1 * t
    id1 = i1 - MOE_GROUPS
    id2 = i2 - MOE_GROUPS
    return jnp.where(lane_i == 0, id1, jnp.where(lane_i == 1, id2,
                                                 jnp.where(lane_i == 2, w1, jnp.where(lane_i == 3, w2, 0.0))))


def _lhs_mla(o_ref):
    return o_ref[...]


def _lhs_mlstm(hf_ref, hb_ref, og_ref, onorm_ref):
    hs = hf_ref[...] + hb_ref[...]
    og = _sigmoid(og_ref[...])
    parts = []
    for h in range(ML_HEADS):
        sl = slice(h * ML_DV, (h + 1) * ML_DV)
        parts.append(_rms(hs[:, sl]) * onorm_ref[:, sl] * og[:, sl])
    return jnp.concatenate(parts, axis=1)


def _mixer_out_kernel(*refs, n_lhs, lhs_fn):
    lhs_refs = refs[:n_lhs]
    w_ref, x_ref, mod_ref, gain_ref, wr_ref, br_ref, xo_ref, f_ref, r_ref = refs[n_lhs:]
    y = _dot(lhs_fn(*lhs_refs).astype(BF16), w_ref[...])
    mod = mod_ref[0]
    xn = x_ref[...] + mod[2:3, :] * y
    xo_ref[...] = xn
    f = _norm_mod(xn, gain_ref[...], mod[3:4, :], mod[4:5, :])
    f_ref[...] = f
    r_ref[...] = _route(_dot3(f, wr_ref[...]) + br_ref[...])


def _mixer_out(rows, lhs_fn, lhs_args, lhs_specs, w_out, x, mod, gain, wr, br, name):
    n = rows.n * TM
    const = lambda k: (0, 0)
    out = lambda w: pl.BlockSpec((TM, w), lambda k: (k, 0))
    return pl.pallas_call(
        functools.partial(_mixer_out_kernel, n_lhs=len(lhs_args), lhs_fn=lhs_fn),
        grid=(rows.n,),
        in_specs=list(lhs_specs) + [
            pl.BlockSpec((D_MODEL, D_MODEL), const),
            pl.BlockSpec((TM, D_MODEL), lambda k: (rows.src(k), 0)),
            pl.BlockSpec((1, 8, D_MODEL), lambda k: (rows.mod(k), 0, 0)),
            pl.BlockSpec((1, D_MODEL), const),
            pl.BlockSpec((D_MODEL, 128), const),
            pl.BlockSpec((1, 128), const)],
        out_specs=[out(D_MODEL), out(D_MODEL), out(128)],
        out_shape=[jax.ShapeDtypeStruct((n, D_MODEL), F32),
                   jax.ShapeDtypeStruct((n, D_MODEL), F32),
                   jax.ShapeDtypeStruct((n, 128), F32)],
        compiler_params=_cparams(("arbitrary",)),
        name=name,
    )(*lhs_args, w_out, x, mod, gain, wr, br)


def _expert_kernel(vb_ref, ve_ref, lo_ref, hi_ref, first_ref, last_ref, tok_ref, nxt_ref, dst_ref, f_hbm, wgu_ref,
                   wd_ref, y_hbm, xbuf, acc, gsem, ssem, *, nblk):
    v = pl.program_id(0)
    blk = vb_ref[v]
    slot = blk % 2

    def start_gather(table, s):
        for r in range(MOE_BM):
            pltpu.make_async_copy(f_hbm.at[pl.ds(table[0, 0, r], 1), :], xbuf.at[s, pl.ds(r, 1), :],
                                  gsem.at[s]).start()

    def start_scatter(s):
        for r in range(MOE_BM):
            pltpu.make_async_copy(acc.at[s, pl.ds(r, 1), :], y_hbm.at[pl.ds(dst_ref[0, 0, r], 1), :],
                                  ssem.at[s]).start()

    def wait_gather(s):
        pltpu.make_async_copy(f_hbm.at[pl.ds(0, MOE_BM), :], xbuf.at[s], gsem.at[s]).wait()

    def wait_scatter(s):
        pltpu.make_async_copy(acc.at[s], y_hbm.at[pl.ds(0, MOE_BM), :], ssem.at[s]).wait()

    def per_slot(fn):
        for s in range(2):
            pl.when(slot == s)(functools.partial(fn, s))

    @pl.when(first_ref[v] == 1)
    def _():
        @pl.when(blk == 0)
        def _():
            start_gather(tok_ref, 0)

        def on_first(s):
            @pl.when(blk + 1 < nblk)
            def _():
                start_gather(nxt_ref, 1 - s)

            @pl.when(blk >= 2)
            def _():
                wait_scatter(s)
            acc[s] = jnp.zeros((MOE_BM, D_MODEL), F32)
            wait_gather(s)
        per_slot(on_first)

    @pl.when(hi_ref[v] > lo_ref[v])
    def _():
        def compute(s):
            x = xbuf[s].astype(BF16)
            gu = _dot(x, wgu_ref[0, 0].astype(BF16))
            gate = gu[:, :MOE_FF]
            act = gate * _sigmoid(gate) * gu[:, MOE_FF:]
            y = _dot(act.astype(BF16), wd_ref[0, 0].astype(BF16))
            r = lax.broadcasted_iota(jnp.int32, (MOE_BM, 1), 0)
            mine = jnp.logical_and(r >= lo_ref[v], r < hi_ref[v])
            acc[s] += jnp.where(mine, y, 0.0)
        per_slot(compute)

    @pl.when(last_ref[v] == 1)
    def _():
        def on_last(s):
            start_scatter(s)

            @pl.when(blk == nblk - 1)
            def _():
                if nblk >= 2:
                    wait_scatter(1 - s)
                wait_scatter(s)
        per_slot(on_last)


def _expert_ffn(f, tok, dst, visits, layer, w_gate_up, w_down):
    n_tok = f.shape[0]
    nblk = tok.shape[0]
    nvis = visits[0].shape[0]
    blk_idx = lambda v, vb, ve, lo, hi, fi, la: (vb[v], 0, 0)
    nxt_idx = lambda v, vb, ve, lo, hi, fi, la: (jnp.minimum(vb[v] + 1, nblk - 1), 0, 0)
    exp_idx = lambda v, vb, ve, lo, hi, fi, la: (layer, ve[v], 0, 0)
    grid_spec = pltpu.PrefetchScalarGridSpec(
        num_scalar_prefetch=6,
        grid=(nvis,),
        in_specs=[pl.BlockSpec((1, 1, MOE_BM), blk_idx, memory_space=pltpu.SMEM),
                  pl.BlockSpec((1, 1, MOE_BM), nxt_idx, memory_space=pltpu.SMEM),
                  pl.BlockSpec((1, 1, MOE_BM), blk_idx, memory_space=pltpu.SMEM),
                  pl.BlockSpec(memory_space=pl.ANY),
                  pl.BlockSpec((1, 1, D_MODEL, 2 * MOE_FF), exp_idx),
                  pl.BlockSpec((1, 1, MOE_FF, D_MODEL), exp_idx)],
        out_specs=pl.BlockSpec(memory_space=pl.ANY),
        scratch_shapes=[pltpu.VMEM((2, MOE_BM, D_MODEL), F32),
                        pltpu.VMEM((2, MOE_BM, D_MODEL), F32),
                        pltpu.SemaphoreType.DMA((2,)),
                        pltpu.SemaphoreType.DMA((2,))],
    )
    return pl.pallas_call(
        functools.partial(_expert_kernel, nblk=nblk),
        grid_spec=grid_spec,
        out_shape=jax.ShapeDtypeStruct((MOE_TOPK * n_tok, D_MODEL), F32),
        compiler_params=_cparams(("arbitrary",)),
        name="moe_expert_ffn",
    )(*visits, tok, tok, dst, f, w_gate_up, w_down)


def _combine_kernel(x_ref, *refs):
    y_refs = refs[:MOE_TOPK]
    r_ref, mod_ref, o_ref = refs[MOE_TOPK:]
    w = r_ref[...]
    y = w[:, MOE_TOPK:MOE_TOPK + 1] * y_refs[0][...]
    for k in range(1, MOE_TOPK):
        y = y + w[:, MOE_TOPK + k:MOE_TOPK + k + 1] * y_refs[k][...]
    o_ref[...] = x_ref[...] + mod_ref[0][5:6, :] * y


def _combine(rows, x, y, route, mod):
    spec = pl.BlockSpec((TM, D_MODEL), lambda i: (i, 0))
    y_specs = [pl.BlockSpec((TM, D_MODEL), functools.partial(lambda i, k: (k * rows.n + i, 0), k=k))
               for k in range(MOE_TOPK)]
    return pl.pallas_call(
        _combine_kernel,
        grid=(rows.n,),
        in_specs=[spec] + y_specs + [pl.BlockSpec((TM, 128), lambda i: (i, 0)),
                                     pl.BlockSpec((1, 8, D_MODEL), lambda i: (rows.mod(i), 0, 0))],
        out_specs=spec,
        out_shape=jax.ShapeDtypeStruct((rows.n * TM, D_MODEL), F32),
        compiler_params=_cparams(("arbitrary",)),
        name="moe_combine",
    )(x, *([y] * MOE_TOPK), route, mod)


def _visit_tables(e_flat, nk):
    E = MOE_EXPERTS
    nblk = nk // MOE_BM
    nvis = nblk + E
    ids = jnp.arange(E + 1, dtype=jnp.int32)
    bounds = jnp.sum((e_flat[None, :] < ids[:, None]).astype(jnp.int32), axis=1)
    starts, ends = bounds[:-1], bounds[1:]
    fb = starts // MOE_BM
    nv = jnp.where(ends > starts, (ends - 1) // MOE_BM - fb + 1, 0)
    cum = jnp.cumsum(nv)
    total = cum[-1]
    v = jnp.arange(nvis, dtype=jnp.int32)
    active = v < total
    vc = jnp.minimum(v, total - 1)
    ve = jnp.minimum(jnp.sum((cum[None, :] <= vc[:, None]).astype(jnp.int32), axis=1), E - 1)
    vb = fb[ve] + (vc - (cum - nv)[ve])
    lo = jnp.where(active, jnp.maximum(starts[ve], vb * MOE_BM) - vb * MOE_BM, 0)
    hi = jnp.where(active, jnp.minimum(ends[ve], (vb + 1) * MOE_BM) - vb * MOE_BM, 0)
    prev_b = jnp.concatenate([jnp.full((1,), -1, jnp.int32), vb[:-1]])
    next_b = jnp.concatenate([vb[1:], jnp.full((1,), -1, jnp.int32)])
    first = jnp.logical_and(active, vb != prev_b)
    last = jnp.logical_and(active, jnp.logical_or(vb != next_b, v == total - 1))
    i32 = lambda a: a.astype(jnp.int32)
    return i32(vb), i32(ve), i32(lo), i32(hi), i32(first), i32(last)


def _moe(rows, x, f, route, mod, layer, w_gate_up, w_down):
    n = rows.n * TM
    nk = n * MOE_TOPK
    e_flat = route[:, :MOE_TOPK].astype(jnp.int32).reshape(nk)
    _, order = lax.sort((e_flat, jnp.arange(nk, dtype=jnp.int32)), num_keys=1, is_stable=True)
    visits = _visit_tables(e_flat, nk)
    tok = (order // MOE_TOPK).reshape(nk // MOE_BM, 1, MOE_BM)
    dst = ((order % MOE_TOPK) * n + order // MOE_TOPK).reshape(nk // MOE_BM, 1, MOE_BM)
    y = _expert_ffn(f, tok, dst, visits, layer, w_gate_up, w_down)
    return _combine(rows, x, y, route, mod)


def kernel(x, c, ctx, c_ctx, ada_w, ada_b, norm_mix, norm_ffn, rg_w_in, rg_conv_w, rg_conv_b, rg_gate_w, rg_gate_b, rg_lambda, rg_w_out, mla_w_down, mla_q_norm, mla_kv_norm, mla_w_uq, mla_w_ukv, mla_qk_norm, mla_w_o, ml_w_in, ml_gate_b, ml_out_norm, ml_w_out, moe_w_group, moe_b_group, moe_w_expert, moe_b_expert, moe_w_gate_up, moe_w_down):
    B, L, D = x.shape
    C = ctx.shape[1]
    depth = ada_w.shape[0]
    assert D == D_MODEL
    st = _Stream(B, L, C)

    xs = jnp.concatenate([ctx, x], axis=1).reshape(st.NT, D)
    cc = jnp.zeros((16, D), F32).at[:B].set(c).at[B].set(c_ctx)
    mod_all = _modulation(cc, ada_w, ada_b)
    mod_all = jnp.pad(mod_all[:, :B + 1].reshape(depth, B + 1, 6, D), ((0, 0), (0, 0), (0, 2), (0, 0)))
    perm = _row_permutation(B)

    row = lambda a: a.reshape(1, -1)

    for i in range(depth):
        last = i == depth - 1
        mod = mod_all[i]
        kind, j = i % 3, i // 3
        all_rows = st.all_rows()
        out_rows = st.latent_rows() if last else all_rows
        tile_spec = lambda w: pl.BlockSpec((TM, w), lambda k: (out_rows.src(k), 0))
        wr = jnp.zeros((D, 128), F32).at[:, :MOE_GROUPS].set(moe_w_group[i]) \
            .at[:, MOE_GROUPS:MOE_GROUPS + MOE_EXPERTS].set(moe_w_expert[i])
        br = jnp.zeros((1, 128), F32).at[0, :MOE_GROUPS].set(moe_b_group[i]) \
            .at[0, MOE_GROUPS:MOE_GROUPS + MOE_EXPERTS].set(moe_b_expert[i])
        out_args = (xs, mod, row(norm_ffn[i]), wr, br)

        if kind == 0:
            gate, u = _rg_in(st, xs, mod, row(norm_mix[i]), perm, rg_w_in[j].astype(BF16))
            wg, gb = _rg_gate_weights(rg_gate_w[j], rg_gate_b[j])
            hs = _rg_scan(st, u, rg_conv_w[j], row(rg_conv_b[j]), wg, gb, rg_lambda[j].reshape(2, 1, D))
            xs, f, route = _rg_out(st, last, gate, hs, perm.T, rg_w_out[j].astype(BF16), *out_args)
        elif kind == 1:
            w_down = jnp.pad(mla_w_down[j], ((0, 0), (0, 512 - mla_w_down.shape[2]))).astype(BF16)
            down = _norm_proj(all_rows, xs, mod, row(norm_mix[i]), w_down, name="mla_down_proj")
            wq, wk, we, wv = _mla_weights(mla_w_uq[j], mla_w_ukv[j])
            q, k, v = _mla_up(st, all_rows, down, row(mla_q_norm[j]), row(mla_kv_norm[j]), wq, wk, we, wv,
                              _rope_tables(L, mla_qk_norm[j]))
            o = _attention(st, q, k, v)
            xs, f, route = _mixer_out(out_rows, _lhs_mla, (o,), (tile_spec(D),),
                                      mla_w_o[j].astype(BF16), *out_args, name="mla_out")
        else:
            n_in = ml_w_in.shape[2]
            w_in = jnp.pad(ml_w_in[j], ((0, 0), (0, ML_NP - n_in))).astype(BF16)
            proj = _norm_proj(all_rows, xs, mod, row(norm_mix[i]), w_in, name="mlstm_in_proj")
            gate_bias = jnp.pad(ml_gate_b[j].reshape(1, -1), ((0, 0), (0, 128 - 4 * ML_HEADS)))
            hf, hb = _mlstm(st, proj, gate_bias)
            og_spec = pl.BlockSpec((TM, D), lambda k: (out_rows.src(k), 2))
            xs, f, route = _mixer_out(out_rows, _lhs_mlstm, (hf, hb, proj, row(ml_out_norm[j])),
                                      (tile_spec(D), tile_spec(D), og_spec, pl.BlockSpec((1, D), lambda k: (0, 0))),
                                      ml_w_out[j].astype(BF16), *out_args, name="mlstm_out")

        moe_rows = st.dense_latent_rows() if last else all_rows
        xs = _moe(moe_rows, xs, f, route, mod, i, moe_w_gate_up, moe_w_down)

    return xs.reshape(B, L, D)
```

```python
import functools
import math

import jax
import jax.numpy as jnp
from jax import lax
from jax.experimental import pallas as pl
from jax.experimental.pallas import tpu as pltpu

F32 = jnp.float32
BF16 = jnp.bfloat16

D_MODEL = 1024
RMS_EPS = 1e-6

TM = 256
VMEM_LIMIT = 48 * 1024 * 1024

RG_BLOCK_W = 64
RG_CHUNK = 256
RG_CONV_W = 4
RG_C = 8.0
RG_TT = 64

MLA_HEADS = 16
MLA_Q_RANK = 256
MLA_KV_RANK = 128
MLA_NOPE = 64
MLA_ROPE = 32
MLA_V = 64
MLA_QK = MLA_NOPE + MLA_ROPE
MLA_HP = 128
ROPE_AXIS_DIM = MLA_ROPE // 2
ROPE_BASE = 10000.0
GRID_W = 64
ATT_HEADS = 4

ML_HEADS = 4
ML_DV = 256
ML_DQK = 128
ML_TC = 256
ML_M_INIT = -1e30
ML_NP = 3200

MOE_GROUPS = 8
MOE_PER_GROUP = 8
MOE_EXPERTS = 64
MOE_TOPK = 2
MOE_FF = 256
MOE_BM = 256


def _cparams(sem):
    return pltpu.CompilerParams(dimension_semantics=sem, vmem_limit_bytes=VMEM_LIMIT)


def _dot(a, b):
    return jnp.dot(a, b, preferred_element_type=F32)


def _dot_t(a, b):
    return lax.dot_general(a, b, (((1,), (1,)), ((), ())), preferred_element_type=F32)


def _dot3(a, b):
    ah = a.astype(BF16)
    al = (a - ah.astype(F32)).astype(BF16)
    bh = b.astype(BF16)
    bl = (b - bh.astype(F32)).astype(BF16)
    return _dot(ah, bh) + (_dot(al, bh) + _dot(ah, bl))


def _sigmoid(x):
    return 0.5 * jnp.tanh(0.5 * x) + 0.5


def _softplus(x):
    return jnp.maximum(x, 0.0) + jnp.log1p(jnp.exp(-jnp.abs(x)))


def _gelu_tanh(x):
    return 0.5 * x * (1.0 + jnp.tanh(0.7978845608028654 * (x + 0.044715 * (x * x * x))))


def _rms(x, n=None):
    n = x.shape[-1] if n is None else n
    ms = jnp.sum(x * x, axis=-1, keepdims=True) * (1.0 / n)
    return x * lax.rsqrt(ms + RMS_EPS)


class _Rows:
    def __init__(self, n, src, mod):
        self.n, self.src, self.mod = n, src, mod


class _Stream:
    def __init__(self, B, L, C):
        assert L % TM == 0 and C % TM == 0 and B % 8 == 0
        self.B, self.L, self.C = B, L, C
        self.ltot = L + C
        self.lt, self.ct = L // TM, C // TM
        self.tpb = self.lt + self.ct
        self.NT = B * self.ltot

    def all_rows(self):
        tpb, ct, B = self.tpb, self.ct, self.B
        return _Rows(B * tpb, lambda k: k, lambda k: jnp.where(k % tpb < ct, B, k // tpb))

    def latent_rows(self):
        tpb, ct, lt = self.tpb, self.ct, self.lt
        return _Rows(self.B * lt, lambda k: (k // lt) * tpb + ct + k % lt, lambda k: k // lt)

    def dense_latent_rows(self):
        lt = self.lt
        return _Rows(self.B * lt, lambda k: k, lambda k: k // lt)


def _mod_kernel(c_ref, w_ref, b_ref, o_ref):
    c = c_ref[...]
    o_ref[0] = _dot3(c * _sigmoid(c), w_ref[0]) + b_ref[0]


def _modulation(cc, ada_w, ada_b):
    depth, d, n = ada_w.shape
    tn = 1536
    return pl.pallas_call(
        _mod_kernel,
        grid=(depth, n // tn),
        in_specs=[pl.BlockSpec((16, d), lambda l, j: (0, 0)),
                  pl.BlockSpec((1, d, tn), lambda l, j: (l, 0, j)),
                  pl.BlockSpec((1, 1, tn), lambda l, j: (l, 0, j))],
        out_specs=pl.BlockSpec((1, 16, tn), lambda l, j: (l, 0, j)),
        out_shape=jax.ShapeDtypeStruct((depth, 16, n), F32),
        compiler_params=_cparams(("arbitrary", "arbitrary")),
        name="ada_modulation",
    )(cc, ada_w, ada_b.reshape(depth, 1, n))


def _norm_mod(x, gain, shift, scale):
    return _rms(x) * gain * (1.0 + scale) + shift


def _norm_proj_kernel(x_ref, mod_ref, g_ref, w_ref, o_ref):
    mod = mod_ref[0]
    h = _norm_mod(x_ref[...], g_ref[...], mod[0:1, :], mod[1:2, :]).astype(BF16)
    o_ref[...] = _dot(h, w_ref[...]).astype(o_ref.dtype)


def _norm_proj(rows, x, mod, gain, w, name):
    n = w.shape[1]
    const = lambda k: (0, 0)
    return pl.pallas_call(
        _norm_proj_kernel,
        grid=(rows.n,),
        in_specs=[pl.BlockSpec((TM, D_MODEL), lambda k: (rows.src(k), 0)),
                  pl.BlockSpec((1, 8, D_MODEL), lambda k: (rows.mod(k), 0, 0)),
                  pl.BlockSpec((1, D_MODEL), const),
                  pl.BlockSpec((D_MODEL, n), const)],
        out_specs=pl.BlockSpec((TM, n), lambda k: (k, 0)),
        out_shape=jax.ShapeDtypeStruct((rows.n * TM, n), F32),
        compiler_params=_cparams(("arbitrary",)),
        name=name,
    )(x, mod, gain, w)


RG_TR = 32


def _row_permutation(B):
    n = B * RG_TR
    r = jnp.arange(n)
    src = (r % B) * RG_TR + r // B
    return (src[:, None] == jnp.arange(n)[None, :]).astype(BF16)


def _time_tile_mod(mod_ref, is_ctx, B, row):
    return jnp.where(is_ctx, mod_ref[B:B + 1, row:row + 1, :], mod_ref[0:B, row:row + 1, :])


def _rg_in_kernel(x_ref, mod_ref, g_ref, perm_ref, wg_ref, wu_ref, gate_ref, u_ref, *, B, ctx_tiles):
    is_ctx = pl.program_id(0) < ctx_tiles
    h = _norm_mod(x_ref[...], g_ref[...], _time_tile_mod(mod_ref, is_ctx, B, 0), _time_tile_mod(mod_ref, is_ctx, B, 1))
    h = h.reshape(B * RG_TR, D_MODEL).astype(BF16)
    h = _dot(perm_ref[...], h).astype(BF16)
    gate_ref[...] = _gelu_tanh(_dot(h, wg_ref[...])).astype(gate_ref.dtype)
    u_ref[...] = _dot(h, wu_ref[...])


def _rg_in(st, x, mod, gain, perm, w_in):
    B, W = st.B, D_MODEL
    R = B * RG_TR
    nt = st.ltot // RG_TR
    const = lambda t: (0, 0)
    return pl.pallas_call(
        functools.partial(_rg_in_kernel, B=B, ctx_tiles=st.C // RG_TR),
        grid=(nt,),
        in_specs=[pl.BlockSpec((B, RG_TR, D_MODEL), lambda t: (0, t, 0)),
                  pl.BlockSpec((B + 1, 8, D_MODEL), lambda t: (0, 0, 0)),
                  pl.BlockSpec((1, D_MODEL), const),
                  pl.BlockSpec((R, R), const),
                  pl.BlockSpec((D_MODEL, W), const),
                  pl.BlockSpec((D_MODEL, W), lambda t: (0, 1))],
        out_specs=[pl.BlockSpec((R, W), lambda t: (t, 0)),
                   pl.BlockSpec((R, W), lambda t: (t, 0))],
        out_shape=[jax.ShapeDtypeStruct((st.ltot * B, W), BF16), jax.ShapeDtypeStruct((st.ltot * B, W), F32)],
        compiler_params=_cparams(("arbitrary",)),
        name="rg_in_proj",
    )(x.reshape(B, st.ltot, D_MODEL), mod, gain, perm, w_in, w_in)


def _rg_tile_order(d, k, ct, ntt):
    bwd = jnp.where(k < ct, ct - 1 - k, ntt - 1 - (k - ct))
    return jnp.where(d == 0, k, bwd)


def _rg_scan_kernel(um_ref, up_ref, un_ref, cw_ref, cb_ref, wg_ref, gb_ref, lam_ref, o_ref,
                    ext_ref, a_ref, b_ref, h_ref, *, B, ct, ntt):
    d = pl.program_id(0)
    k = pl.program_id(1)
    tile = _rg_tile_order(d, k, ct, ntt)
    R = RG_TT * B

    @pl.when(k == 0)
    def _():
        h_ref[...] = jnp.zeros_like(h_ref)

    seq_start = jnp.logical_or(tile == 0, tile == ct)
    seq_end = jnp.logical_or(tile == ct - 1, tile == ntt - 1)
    ext_ref[0:2 * B, :] = jnp.where(seq_start, 0.0, up_ref[...])
    ext_ref[2 * B:2 * B + R, :] = um_ref[...]
    ext_ref[2 * B + R:3 * B + R, :] = jnp.where(seq_end, 0.0, un_ref[...])
    cw = cw_ref[...]
    uc = cb_ref[...] + cw[0:1, :] * ext_ref[0:R, :]
    for j in range(1, RG_CONV_W):
        uc = uc + cw[j:j + 1, :] * ext_ref[j * B:j * B + R, :]
    ucb = uc.astype(BF16)
    c_lam = -RG_C * _softplus(-lam_ref[0])
    for c in range(D_MODEL // RG_CHUNK):
        sl = slice(c * RG_CHUNK, (c + 1) * RG_CHUNK)
        z = _dot(ucb[:, sl], wg_ref[0, c]) + gb_ref[0, c]
        r = _sigmoid(z[:, :RG_CHUNK])
        ig = _sigmoid(z[:, RG_CHUNK:])
        log_a = c_lam[:, sl] * r
        a = jnp.exp(log_a)
        one_minus_a2 = -jnp.tanh(log_a) * (a * a + 1.0)
        a_ref[:, sl] = a
        b_ref[:, sl] = jnp.sqrt(one_minus_a2) * (ig * uc[:, sl])

    def scan(times):
        for c in range(D_MODEL // 128):
            cs = slice(c * 128, (c + 1) * 128)
            h = h_ref[:, cs]
            for t in times:
                rs = slice(t * B, (t + 1) * B)
                h = a_ref[rs, cs] * h + b_ref[rs, cs]
                b_ref[rs, cs] = h
            h_ref[:, cs] = h

    pl.when(d == 0)(lambda: scan(range(RG_TT)))
    pl.when(d == 1)(lambda: scan(range(RG_TT - 1, -1, -1)))
    o_ref[0] = b_ref[...].astype(o_ref.dtype)


def _rg_scan(st, u_tm, conv_w, conv_b, wg, gb, lam):
    B, W = st.B, D_MODEL
    assert st.C % RG_TT == 0 and st.L % RG_TT == 0
    ltot = st.ltot
    ntt, ct = ltot // RG_TT, st.C // RG_TT
    R = RG_TT * B
    order = functools.partial(_rg_tile_order, ct=ct, ntt=ntt)
    nch = W // RG_CHUNK
    return pl.pallas_call(
        functools.partial(_rg_scan_kernel, B=B, ct=ct, ntt=ntt),
        grid=(2, ntt),
        in_specs=[pl.BlockSpec((R, W), lambda d, k: (order(d, k), 0)),
                  pl.BlockSpec((2 * B, W), lambda d, k: (jnp.maximum(order(d, k) * (RG_TT // 2) - 1, 0), 0)),
                  pl.BlockSpec((B, W), lambda d, k: (jnp.minimum((order(d, k) + 1) * RG_TT, ltot - 1), 0)),
                  pl.BlockSpec((RG_CONV_W, W), lambda d, k: (0, 0)),
                  pl.BlockSpec((1, W), lambda d, k: (0, 0)),
                  pl.BlockSpec((1, nch, RG_CHUNK, 2 * RG_CHUNK), lambda d, k: (d, 0, 0, 0)),
                  pl.BlockSpec((1, nch, 1, 2 * RG_CHUNK), lambda d, k: (d, 0, 0, 0)),
                  pl.BlockSpec((1, 1, W), lambda d, k: (d, 0, 0))],
        out_specs=pl.BlockSpec((1, R, W), lambda d, k: (d, order(d, k), 0)),
        out_shape=jax.ShapeDtypeStruct((2, ltot * B, W), BF16),
        scratch_shapes=[pltpu.VMEM((3 * B + R, W), F32),
                        pltpu.VMEM((R, W), F32),
                        pltpu.VMEM((R, W), F32),
                        pltpu.VMEM((B, W), F32)],
        compiler_params=_cparams(("arbitrary", "arbitrary")),
        name="rg_scan",
    )(u_tm, u_tm, u_tm, conv_w, conv_b, wg, gb, lam)


def _rg_gate_weights(gate_w, gate_b):
    nb = gate_w.shape[2]
    per = RG_CHUNK // RG_BLOCK_W
    nch = nb // per
    gw = gate_w.reshape(2, 2, nch, per, RG_BLOCK_W, RG_BLOCK_W)
    eye = jnp.eye(per, dtype=gate_w.dtype)
    bd = jnp.einsum('dgcnij,nm->dgcnimj', gw, eye).reshape(2, 2, nch, RG_CHUNK, RG_CHUNK)
    wg = jnp.concatenate([bd[:, 0], bd[:, 1]], axis=-1).astype(BF16)
    gb = gate_b.reshape(2, 2, nch, 1, RG_CHUNK)
    gb = jnp.concatenate([gb[:, 0], gb[:, 1]], axis=-1)
    return wg, gb


def _rg_out_kernel(g_ref, hf_ref, hb_ref, perm_ref, w_ref, x_ref, mod_ref, gain_ref, wr_ref, br_ref,
                   xo_ref, f_ref, r_ref, cnt_ref, *, B, ctx_tiles, t0):
    is_ctx = pl.program_id(0) + t0 < ctx_tiles
    hsum = hf_ref[0].astype(F32) + hb_ref[0].astype(F32)
    lhs = (g_ref[...].astype(F32) * hsum).astype(BF16)
    lhs = _dot(perm_ref[...], lhs).astype(BF16)
    y = _dot(lhs, w_ref[...]).reshape(B, RG_TR, D_MODEL)
    m = lambda row: _time_tile_mod(mod_ref, is_ctx, B, row)
    xn = x_ref[...] + m(2) * y
    xo_ref[...] = xn
    f = _norm_mod(xn, gain_ref[...], m(3), m(4))
    f_ref[...] = f
    lg = _dot3(f.reshape(B * RG_TR, D_MODEL), wr_ref[...]) + br_ref[...]
    table, hist = _route(lg)
    r_ref[...] = table.reshape(B, RG_TR, 128)
    _accumulate_counts(cnt_ref, hist)


def _rg_out(st, latent_only, gate, hs, perm, w_out, x, mod, gain, wr, br):
    B = st.B
    R = B * RG_TR
    t0 = st.C // RG_TR if latent_only else 0
    lo = st.L if latent_only else st.ltot
    nt = lo // RG_TR
    const = lambda t: (0, 0)
    blk = lambda w: pl.BlockSpec((B, RG_TR, w), lambda t: (0, t, 0))
    xo, f, route, counts = pl.pallas_call(
        functools.partial(_rg_out_kernel, B=B, ctx_tiles=st.C // RG_TR, t0=t0),
        grid=(nt,),
        in_specs=[pl.BlockSpec((R, D_MODEL), lambda t: (t + t0, 0)),
                  pl.BlockSpec((1, R, D_MODEL), lambda t: (0, t + t0, 0)),
                  pl.BlockSpec((1, R, D_MODEL), lambda t: (1, t + t0, 0)),
                  pl.BlockSpec((R, R), const),
                  pl.BlockSpec((D_MODEL, D_MODEL), const),
                  pl.BlockSpec((B, RG_TR, D_MODEL), lambda t: (0, t + t0, 0)),
                  pl.BlockSpec((B + 1, 8, D_MODEL), lambda t: (0, 0, 0)),
                  pl.BlockSpec((1, D_MODEL), const),
                  pl.BlockSpec((D_MODEL, 128), const),
                  pl.BlockSpec((1, 128), const)],
        out_specs=[blk(D_MODEL), blk(D_MODEL), blk(128), pl.BlockSpec((1, 128), const)],
        out_shape=[jax.ShapeDtypeStruct((B, lo, D_MODEL), F32),
                   jax.ShapeDtypeStruct((B, lo, D_MODEL), F32),
                   jax.ShapeDtypeStruct((B, lo, 128), F32),
                   jax.ShapeDtypeStruct((1, 128), F32)],
        compiler_params=_cparams(("arbitrary",)),
        name="rg_out",
    )(gate, hs, hs, perm, w_out, x.reshape(B, st.ltot, D_MODEL), mod, gain, wr, br)
    return xo.reshape(B * lo, D_MODEL), f.reshape(B * lo, D_MODEL), route.reshape(B * lo, 128), counts


_ROPE_HALF = ROPE_AXIS_DIM // 2
_MLA_SRC_DIM = (list(range(MLA_NOPE + ROPE_AXIS_DIM)) + list(range(MLA_NOPE, MLA_NOPE + _ROPE_HALF))
                + list(range(MLA_NOPE + ROPE_AXIS_DIM, MLA_QK))
                + list(range(MLA_NOPE + ROPE_AXIS_DIM, MLA_NOPE + ROPE_AXIS_DIM + _ROPE_HALF)))
_MLA_REAL_LANE = ([1.0] * (MLA_NOPE + ROPE_AXIS_DIM) + [0.0] * _ROPE_HALF + [1.0] * ROPE_AXIS_DIM
                  + [0.0] * _ROPE_HALF + [0.0] * (MLA_HP - len(_MLA_SRC_DIM)))


def _mla_up_kernel(dn_ref, qn_ref, kvn_ref, wq_ref, wk_ref, we_ref, wv_ref, real_ref, cq_ref, sq_ref, ck_ref, sk_ref,
                   q_ref, k_ref, v_ref):
    dn = dn_ref[...]
    cq = _rms(dn[:, :MLA_Q_RANK]) * qn_ref[...]
    ckv = _rms(dn[:, MLA_Q_RANK:MLA_Q_RANK + MLA_KV_RANK]) * kvn_ref[...]
    kr = dn[:, MLA_Q_RANK + MLA_KV_RANK:]
    kr_hi = kr.astype(BF16)
    kr_lo = (kr - kr_hi.astype(F32)).astype(BF16)
    ckvb = ckv.astype(BF16)
    q_pre = _dot(cq.astype(BF16), wq_ref[...])
    k_pre = _dot(ckvb, wk_ref[...]) + (_dot(kr_hi, we_ref[...]) + _dot(kr_lo, we_ref[...]))
    v_ref[...] = _dot(ckvb, wv_ref[...]).astype(v_ref.dtype)
    real = real_ref[...]

    def head(x, cos_g, sin_g):
        ms = jnp.sum(x * x * real, axis=-1, keepdims=True) * (1.0 / MLA_QK)
        xr = x * lax.rsqrt(ms + RMS_EPS)
        return xr * cos_g + pltpu.roll(xr, MLA_HP - _ROPE_HALF, 1) * sin_g

    cos_q, sin_q, cos_k, sin_k = cq_ref[...], sq_ref[...], ck_ref[...], sk_ref[...]
    for h in range(MLA_HEADS):
        sl = slice(h * MLA_HP, (h + 1) * MLA_HP)
        q_ref[:, sl] = head(q_pre[:, sl], cos_q, sin_q).astype(q_ref.dtype)
        k_ref[:, sl] = head(k_pre[:, sl], cos_k, sin_k).astype(k_ref.dtype)


def _mla_up(st, rows, down, q_norm, kv_norm, wq, wk, we, wv, tables):
    hw = MLA_HEADS * MLA_HP
    const = lambda i: (0, 0)
    tpb, ct, lt = st.tpb, st.ct, st.lt
    rope_idx = lambda i: (jnp.where(i % tpb < ct, lt, i % tpb - ct), 0)
    real = jnp.asarray(_MLA_REAL_LANE, F32).reshape(1, MLA_HP)
    return pl.pallas_call(
        _mla_up_kernel,
        grid=(rows.n,),
        in_specs=[pl.BlockSpec((TM, 512), lambda i: (i, 0)),
                  pl.BlockSpec((1, MLA_Q_RANK), const),
                  pl.BlockSpec((1, MLA_KV_RANK), const),
                  pl.BlockSpec((MLA_Q_RANK, hw), const),
                  pl.BlockSpec((MLA_KV_RANK, hw), const),
                  pl.BlockSpec((128, hw), const),
                  pl.BlockSpec((MLA_KV_RANK, MLA_HEADS * MLA_V), const),
                  pl.BlockSpec((1, MLA_HP), const)] + [pl.BlockSpec((TM, MLA_HP), rope_idx)] * 4,
        out_specs=[pl.BlockSpec((TM, hw), lambda i: (i, 0)),
                   pl.BlockSpec((TM, hw), lambda i: (i, 0)),
                   pl.BlockSpec((TM, MLA_HEADS * MLA_V), lambda i: (i, 0))],
        out_shape=[jax.ShapeDtypeStruct((st.NT, hw), BF16),
                   jax.ShapeDtypeStruct((st.NT, hw), BF16),
                   jax.ShapeDtypeStruct((st.NT, MLA_HEADS * MLA_V), BF16)],
        compiler_params=_cparams(("arbitrary",)),
        name="mla_up_proj",
    )(down, q_norm, kv_norm, wq, wk, we, wv, real, *tables)


def _attn_kernel(q_ref, k_ref, v_ref, o_ref, vaug_ref, *, C, ct):
    qi = pl.program_id(2)

    @pl.when(qi == 0)
    def _():
        lane = lax.broadcasted_iota(jnp.int32, (k_ref.shape[0], 2 * MLA_V), 1)
        for hh in range(ATT_HEADS):
            pair = v_ref[:, (hh // 2) * 2 * MLA_V:(hh // 2 + 1) * 2 * MLA_V].astype(F32)
            if hh % 2 == 0:
                aug = jnp.where(lane < MLA_V, pair, jnp.where(lane == MLA_V, 1.0, 0.0))
            else:
                aug = jnp.where(lane >= MLA_V, pair, jnp.where(lane == 0, 1.0, 0.0))
            vaug_ref[hh] = aug.astype(BF16)

    def attend(nkeys):
        lane = lax.broadcasted_iota(jnp.int32, (TM, 2 * MLA_V), 1)
        for pair in range(ATT_HEADS // 2):
            outs = []
            for hh in (2 * pair, 2 * pair + 1):
                sl = slice(hh * MLA_HP, (hh + 1) * MLA_HP)
                s = _dot_t(q_ref[:, sl], k_ref[0:nkeys, sl])
                p = jnp.exp2(s - jnp.max(s, axis=-1, keepdims=True)).astype(BF16)
                o = _dot(p, vaug_ref[hh, 0:nkeys, :])
                rowsum = o[:, MLA_V:MLA_V + 1] if hh % 2 == 0 else o[:, 0:1]
                outs.append(o * (1.0 / rowsum))
            o_ref[:, pair * 2 * MLA_V:(pair + 1) * 2 * MLA_V] = jnp.where(lane < MLA_V, outs[0], outs[1]).astype(o_ref.dtype)

    pl.when(qi < ct)(lambda: attend(C))
    pl.when(qi >= ct)(lambda: attend(k_ref.shape[0]))


def _attention(st, q, k, v):
    B, ltot, tpb = st.B, st.ltot, st.tpb
    hg = MLA_HEADS // ATT_HEADS
    return pl.pallas_call(
        functools.partial(_attn_kernel, C=st.C, ct=st.ct),
        grid=(B, hg, tpb),
        in_specs=[pl.BlockSpec((TM, ATT_HEADS * MLA_HP), lambda b, h, i: (b * tpb + i, h)),
                  pl.BlockSpec((ltot, ATT_HEADS * MLA_HP), lambda b, h, i: (b, h)),
                  pl.BlockSpec((ltot, ATT_HEADS * MLA_V), lambda b, h, i: (b, h))],
        out_specs=pl.BlockSpec((TM, ATT_HEADS * MLA_V), lambda b, h, i: (b * tpb + i, h)),
        out_shape=jax.ShapeDtypeStruct((st.NT, MLA_HEADS * MLA_V), BF16),
        scratch_shapes=[pltpu.VMEM((ATT_HEADS, ltot, 2 * MLA_V), BF16)],
        compiler_params=_cparams(("arbitrary", "arbitrary", "arbitrary")),
        name="mla_attention",
    )(q, k, v)


def _mla_weights(w_uq, w_ukv):
    H = MLA_HEADS
    src = jnp.asarray(_MLA_SRC_DIM, jnp.int32)
    pad = MLA_HP - len(_MLA_SRC_DIM)
    wq = jnp.pad(w_uq.reshape(MLA_Q_RANK, H, MLA_QK)[:, :, src], ((0, 0), (0, 0), (0, pad)))
    wkv = w_ukv.reshape(MLA_KV_RANK, H, MLA_NOPE + MLA_V)
    wk = jnp.pad(wkv[:, :, :MLA_NOPE], ((0, 0), (0, 0), (0, MLA_HP - MLA_NOPE)))
    wv = wkv[:, :, MLA_NOPE:]
    r = jnp.arange(128)[:, None]
    lane_dim = jnp.pad(src, (0, pad), constant_values=-1)[None, :]
    place = (lane_dim == r + MLA_NOPE).astype(BF16)
    we = jnp.tile(place, (1, H))
    return (wq.reshape(MLA_Q_RANK, H * MLA_HP).astype(BF16), wk.reshape(MLA_KV_RANK, H * MLA_HP).astype(BF16),
            we, wv.reshape(MLA_KV_RANK, H * MLA_V).astype(BF16))


def _rope_tables(L, qk_norm):
    rows = L // GRID_W
    row = jnp.broadcast_to(jnp.arange(rows, dtype=F32)[:, None], (rows, GRID_W)).reshape(L)
    col = jnp.broadcast_to(jnp.arange(GRID_W, dtype=F32)[None, :], (rows, GRID_W)).reshape(L)
    inv_freq = ROPE_BASE ** (-jnp.arange(0, ROPE_AXIS_DIM, 2, dtype=F32) / ROPE_AXIS_DIM)
    ar = row[:, None] * inv_freq
    ac = col[:, None] * inv_freq
    h8 = _ROPE_HALF
    one = jnp.ones((L, MLA_NOPE), F32)
    z8 = jnp.zeros((L, h8), F32)
    zpad = jnp.zeros((L, MLA_HP - len(_MLA_SRC_DIM)), F32)
    cos_t = jnp.concatenate([one, jnp.cos(ar), jnp.cos(ar), z8, jnp.cos(ac), jnp.cos(ac), z8, zpad], axis=1)
    sin_t = jnp.concatenate([0 * one, -jnp.sin(ar), jnp.sin(ar), z8, -jnp.sin(ac), jnp.sin(ac), z8, zpad], axis=1)
    ident = jnp.asarray(_MLA_REAL_LANE, F32)[None, :]
    cos_t = jnp.concatenate([cos_t, jnp.broadcast_to(ident, (TM, MLA_HP))], axis=0)
    sin_t = jnp.concatenate([sin_t, jnp.zeros((TM, MLA_HP), F32)], axis=0)
    src = jnp.asarray(_MLA_SRC_DIM, jnp.int32)
    pad = MLA_HP - len(_MLA_SRC_DIM)
    scale = MLA_QK ** -0.5 * math.log2(math.e)
    tables = []
    for g, s in ((qk_norm[0], scale), (qk_norm[1], 1.0)):
        g_lane = jnp.pad(g[src], (0, pad))
        g_partner = jnp.roll(g_lane, -h8)
        tables += [cos_t * (g_lane * s)[None, :], sin_t * (g_partner * s)[None, :]]
    return tables


def _log_sigmoid(x):
    return jnp.minimum(x, 0.0) - jnp.log1p(jnp.exp(-jnp.abs(x)))


def _mlstm_kernel(qf_ref, kf_ref, vf_ref, gf_ref, qb_ref, kb_ref, vb_ref, gb_ref, bias_ref, tril_ref, triu_ref,
                  of_ref, ob_ref, c_ref, n_ref, m_ref):
    T = ML_TC

    @pl.when(pl.program_id(1) == 0)
    def _():
        c_ref[...] = jnp.zeros_like(c_ref)
        n_ref[...] = jnp.zeros_like(n_ref)
        m_ref[...] = jnp.full(m_ref.shape, ML_M_INIT, F32)

    ti = lax.broadcasted_iota(jnp.int32, (T, T), 0)
    si = lax.broadcasted_iota(jnp.int32, (T, T), 1)
    dirs = ((qf_ref, kf_ref, vf_ref, gf_ref, of_ref, tril_ref), (qb_ref, kb_ref, vb_ref, gb_ref, ob_ref, triu_ref))
    for d, (q_ref, k_ref, v_ref, g_ref, o_ref, tri_ref) in enumerate(dirs):
        tri = (si <= ti) if d == 0 else (si >= ti)
        g = g_ref[...] + bias_ref[...]
        g_t = g.T
        lsg = _log_sigmoid(g)
        lsg_hi = lsg.astype(BF16)
        lsg_lo = (lsg - lsg_hi.astype(F32)).astype(BF16)
        cum = _dot(tri_ref[...], lsg_hi) + _dot(tri_ref[...], lsg_lo)
        cum_t = cum.T
        last = T - 1 if d == 0 else 0
        for h in range(ML_HEADS):
            st = d * ML_HEADS + h
            li, lf_ = (2 * d) * ML_HEADS + h, (2 * d + 1) * ML_HEADS + h
            ig_col = g[:, li:li + 1]
            ig_row = g_t[li:li + 1, :]
            b_col = cum[:, lf_:lf_ + 1]
            b_row = cum_t[lf_:lf_ + 1, :]
            total = cum[last:last + 1, lf_:lf_ + 1]
            m_old = m_ref[st, 0:1, 0:1]
            d_log = jnp.where(tri, b_col - b_row + ig_row, -jnp.inf)
            inter_log = b_col + m_old
            m_t = jnp.maximum(inter_log, jnp.max(d_log, axis=1, keepdims=True))
            qh = q_ref[:, h * ML_DQK:(h + 1) * ML_DQK] * (ML_DQK ** -0.5)
            kh = k_ref[:, h * ML_DQK:(h + 1) * ML_DQK]
            vh = v_ref[:, h * ML_DV:(h + 1) * ML_DV].astype(BF16)
            qb16 = qh.astype(BF16)
            s_mat = _dot_t(qb16, kh.astype(BF16)) * jnp.exp(d_log - m_t)
            inter = jnp.exp(inter_log - m_t)
            c_old = c_ref[st]
            n_old = n_ref[st, 0:1, :]
            num = _dot(s_mat.astype(BF16), vh) + inter * _dot(qb16, c_old.astype(BF16))
            den = jnp.sum(s_mat, axis=1, keepdims=True) + inter * jnp.sum(qh * n_old, axis=1, keepdims=True)
            o_ref[:, h * ML_DV:(h + 1) * ML_DV] = num / jnp.maximum(jnp.abs(den), jnp.exp(-m_t))
            w_log = total - b_col + ig_col
            m_new = jnp.maximum(total + m_old, jnp.max(w_log, axis=0, keepdims=True))
            w = jnp.exp(w_log - m_new)
            decay = jnp.exp(total + m_old - m_new)
            kw = kh * w
            c_ref[st] = decay * c_old + _dot(kw.T.astype(BF16), vh)
            n_ref[st, 0:1, :] = decay * n_old + jnp.sum(kw, axis=0, keepdims=True)
            m_ref[st] = jnp.broadcast_to(m_new, m_ref.shape[1:])


def _mlstm(st, proj, gate_bias):
    B = st.B
    assert st.L % ML_TC == 0 and st.C % ML_TC == 0
    cc = st.C // ML_TC
    nch = st.ltot // ML_TC
    qw = ML_HEADS * ML_DQK
    vw = ML_HEADS * ML_DV
    gcol = (2 * qw + 2 * vw) // 128

    def rb(d, b, k):
        chunk = k if d == 0 else jnp.where(k < cc, cc - 1 - k, nch - 1 - (k - cc))
        return b * nch + chunk

    def specs(d):
        return [pl.BlockSpec((ML_TC, qw), lambda b, k: (rb(d, b, k), 0)),
                pl.BlockSpec((ML_TC, qw), lambda b, k: (rb(d, b, k), 1)),
                pl.BlockSpec((ML_TC, vw), lambda b, k: (rb(d, b, k), (2 * qw) // vw)),
                pl.BlockSpec((ML_TC, 128), lambda b, k: (rb(d, b, k), gcol))]

    nst = 2 * ML_HEADS
    out = jax.ShapeDtypeStruct((st.NT, vw), F32)
    tril = jnp.tril(jnp.ones((ML_TC, ML_TC), BF16))
    return pl.pallas_call(
        _mlstm_kernel,
        grid=(B, nch),
        in_specs=specs(0) + specs(1) + [pl.BlockSpec((1, 128), lambda b, k: (0, 0)),
                                        pl.BlockSpec((ML_TC, ML_TC), lambda b, k: (0, 0)),
                                        pl.BlockSpec((ML_TC, ML_TC), lambda b, k: (0, 0))],
        out_specs=[pl.BlockSpec((ML_TC, vw), lambda b, k: (rb(0, b, k), 0)),
                   pl.BlockSpec((ML_TC, vw), lambda b, k: (rb(1, b, k), 0))],
        out_shape=[out, out],
        scratch_shapes=[pltpu.VMEM((nst, ML_DQK, ML_DV), F32),
                        pltpu.VMEM((nst, 8, ML_DQK), F32),
                        pltpu.VMEM((nst, 8, 128), F32)],
        compiler_params=_cparams(("arbitrary", "arbitrary")),
        name="mlstm_chunks",
    )(proj, proj, proj, proj, proj, proj, proj, proj, gate_bias, tril, tril.T)


def _route(lg):
    lane_i = lax.broadcasted_iota(jnp.int32, lg.shape, 1)
    lane = lane_i.astype(F32)
    neg = -jnp.inf
    gl = jnp.where(lane_i < MOE_GROUPS, lg, neg)
    gmax = jnp.max(gl, axis=-1, keepdims=True)
    gsum = jnp.sum(jnp.where(lane_i < MOE_GROUPS, jnp.exp(lg - gmax), 0.0), axis=-1, keepdims=True)
    p_top = 1.0 / gsum
    g_sel = jnp.min(jnp.where(gl == gmax, lane, 128.0), axis=-1, keepdims=True)
    group_of_lane = (lane_i >> 3).astype(F32) - 1.0
    el = jnp.where(group_of_lane == g_sel, lg, neg)
    e1 = jnp.max(el, axis=-1, keepdims=True)
    i1 = jnp.min(jnp.where(el == e1, lane, 128.0), axis=-1, keepdims=True)
    el2 = jnp.where(lane == i1, neg, el)
    e2 = jnp.max(el2, axis=-1, keepdims=True)
    i2 = jnp.min(jnp.where(el2 == e2, lane, 128.0), axis=-1, keepdims=True)
    t = jnp.exp(e2 - e1)
    w1 = p_top / (1.0 + t)
    w2 = w1 * t
    id1 = i1 - MOE_GROUPS
    id2 = i2 - MOE_GROUPS
    table = jnp.where(lane_i == 0, id1, jnp.where(lane_i == 1, id2,
                                                  jnp.where(lane_i == 2, w1, jnp.where(lane_i == 3, w2, 0.0))))
    chosen = jnp.where(lane == i1, 1.0, 0.0) + jnp.where(lane == i2, 1.0, 0.0)
    return table, jnp.sum(chosen, axis=0, keepdims=True)


def _accumulate_counts(cnt_ref, hist):
    @pl.when(pl.program_id(0) == 0)
    def _():
        cnt_ref[...] = jnp.zeros_like(cnt_ref)
    cnt_ref[...] += hist


def _lhs_mla(o_ref):
    return o_ref[...]


def _lhs_mlstm(hf_ref, hb_ref, og_ref, onorm_ref):
    hs = hf_ref[...] + hb_ref[...]
    og = _sigmoid(og_ref[...])
    parts = []
    for h in range(ML_HEADS):
        sl = slice(h * ML_DV, (h + 1) * ML_DV)
        parts.append(_rms(hs[:, sl]) * onorm_ref[:, sl] * og[:, sl])
    return jnp.concatenate(parts, axis=1)


def _mixer_out_kernel(*refs, n_lhs, lhs_fn):
    lhs_refs = refs[:n_lhs]
    w_ref, x_ref, mod_ref, gain_ref, wr_ref, br_ref, xo_ref, f_ref, r_ref, cnt_ref = refs[n_lhs:]
    y = _dot(lhs_fn(*lhs_refs).astype(BF16), w_ref[...])
    mod = mod_ref[0]
    xn = x_ref[...] + mod[2:3, :] * y
    xo_ref[...] = xn
    f = _norm_mod(xn, gain_ref[...], mod[3:4, :], mod[4:5, :])
    f_ref[...] = f
    r_ref[...], hist = _route(_dot3(f, wr_ref[...]) + br_ref[...])
    _accumulate_counts(cnt_ref, hist)


def _mixer_out(rows, lhs_fn, lhs_args, lhs_specs, w_out, x, mod, gain, wr, br, name):
    n = rows.n * TM
    const = lambda k: (0, 0)
    out = lambda w: pl.BlockSpec((TM, w), lambda k: (k, 0))
    return pl.pallas_call(
        functools.partial(_mixer_out_kernel, n_lhs=len(lhs_args), lhs_fn=lhs_fn),
        grid=(rows.n,),
        in_specs=list(lhs_specs) + [
            pl.BlockSpec((D_MODEL, D_MODEL), const),
            pl.BlockSpec((TM, D_MODEL), lambda k: (rows.src(k), 0)),
            pl.BlockSpec((1, 8, D_MODEL), lambda k: (rows.mod(k), 0, 0)),
            pl.BlockSpec((1, D_MODEL), const),
            pl.BlockSpec((D_MODEL, 128), const),
            pl.BlockSpec((1, 128), const)],
        out_specs=[out(D_MODEL), out(D_MODEL), out(128), pl.BlockSpec((1, 128), const)],
        out_shape=[jax.ShapeDtypeStruct((n, D_MODEL), F32),
                   jax.ShapeDtypeStruct((n, D_MODEL), F32),
                   jax.ShapeDtypeStruct((n, 128), F32),
                   jax.ShapeDtypeStruct((1, 128), F32)],
        compiler_params=_cparams(("arbitrary",)),
        name=name,
    )(*lhs_args, w_out, x, mod, gain, wr, br)


def _rank_kernel(r_ref, start_ref, tril_ref, pos_ref, carry_ref):
    @pl.when(pl.program_id(0) == 0)
    def _():
        carry_ref[...] = jnp.zeros_like(carry_ref)

    r = r_ref[...]
    lane = lax.broadcasted_iota(jnp.int32, r.shape, 1).astype(F32)
    chosen = [jnp.where(lane == r[:, k:k + 1] + MOE_GROUPS, 1.0, 0.0) for k in range(MOE_TOPK)]
    both = functools.reduce(jnp.add, chosen)
    before = _dot(tril_ref[...], both.astype(BF16)) + (start_ref[...] + carry_ref[...])
    lane_i = lax.broadcasted_iota(jnp.int32, r.shape, 1)
    pos = jnp.zeros(r.shape, F32)
    for k in range(MOE_TOPK):
        pos = jnp.where(lane_i == k, jnp.sum(chosen[k] * before, axis=-1, keepdims=True), pos)
    pos_ref[0] = pos.T[0:8, :].astype(jnp.int32)
    carry_ref[...] += jnp.sum(both, axis=0, keepdims=True)


def _assignment_slots(ntiles, route, starts):
    tril = jnp.tril(jnp.ones((TM, TM), BF16), -1)
    return pl.pallas_call(
        _rank_kernel,
        grid=(ntiles,),
        in_specs=[pl.BlockSpec((TM, 128), lambda i: (i, 0)),
                  pl.BlockSpec((1, 128), lambda i: (0, 0)),
                  pl.BlockSpec((TM, TM), lambda i: (0, 0))],
        out_specs=pl.BlockSpec((1, 8, TM), lambda i: (i, 0, 0)),
        out_shape=jax.ShapeDtypeStruct((ntiles, 8, TM), jnp.int32),
        scratch_shapes=[pltpu.VMEM((1, 128), F32)],
        compiler_params=_cparams(("arbitrary",)),
        name="moe_rank",
    )(route, starts, tril)


def _row_wait(hbm, buf, sem):
    pltpu.make_async_copy(hbm.at[pl.ds(0, TM), :], buf, sem).wait()


def _dispatch_kernel(pos_ref, f_ref, xs_hbm, buf, sem, *, ntiles):
    i = pl.program_id(0)

    def step(s):
        @pl.when(i >= 2)
        def _():
            for _ in range(MOE_TOPK):
                _row_wait(xs_hbm, buf.at[s], sem.at[s])
        buf[s] = f_ref[...]
        for r in range(TM):
            for k in range(MOE_TOPK):
                pltpu.make_async_copy(buf.at[s, pl.ds(r, 1), :], xs_hbm.at[pl.ds(pos_ref[0, k, r], 1), :],
                                      sem.at[s]).start()

        @pl.when(i == ntiles - 1)
        def _():
            for slot in ((1 - s, s) if ntiles >= 2 else (s,)):
                for _ in range(MOE_TOPK):
                    _row_wait(xs_hbm, buf.at[slot], sem.at[slot])

    for s in range(2):
        pl.when(i % 2 == s)(functools.partial(step, s))


def _dispatch(ntiles, f, pos):
    n = ntiles * TM
    return pl.pallas_call(
        functools.partial(_dispatch_kernel, ntiles=ntiles),
        grid=(ntiles,),
        in_specs=[pl.BlockSpec((1, 8, TM), lambda i: (i, 0, 0), memory_space=pltpu.SMEM),
                  pl.BlockSpec((TM, D_MODEL), lambda i: (i, 0))],
        out_specs=pl.BlockSpec(memory_space=pl.ANY),
        out_shape=jax.ShapeDtypeStruct((MOE_TOPK * n, D_MODEL), F32),
        scratch_shapes=[pltpu.VMEM((2, TM, D_MODEL), F32), pltpu.SemaphoreType.DMA((2,))],
        compiler_params=_cparams(("arbitrary",)),
        name="moe_dispatch",
    )(pos, f)


def _expert_kernel(vb_ref, ve_ref, lo_ref, hi_ref, first_ref, x_ref, wgu_ref, wd_ref, y_ref):
    v = pl.program_id(0)

    @pl.when(first_ref[v] == 1)
    def _():
        y_ref[...] = jnp.zeros_like(y_ref)

    @pl.when(hi_ref[v] > lo_ref[v])
    def _():
        gu = _dot(x_ref[...].astype(BF16), wgu_ref[0, 0].astype(BF16))
        gate = gu[:, :MOE_FF]
        act = gate * _sigmoid(gate) * gu[:, MOE_FF:]
        y = _dot(act.astype(BF16), wd_ref[0, 0].astype(BF16))
        r = lax.broadcasted_iota(jnp.int32, (MOE_BM, 1), 0)
        mine = jnp.logical_and(r >= lo_ref[v], r < hi_ref[v])
        y_ref[...] += jnp.where(mine, y, 0.0)


def _expert_ffn(xs, visits, layer, w_gate_up, w_down):
    nvis = visits[0].shape[0]
    blk_idx = lambda v, vb, ve, lo, hi, fi: (vb[v], 0)
    exp_idx = lambda v, vb, ve, lo, hi, fi: (layer, ve[v], 0, 0)
    grid_spec = pltpu.PrefetchScalarGridSpec(
        num_scalar_prefetch=5,
        grid=(nvis,),
        in_specs=[pl.BlockSpec((MOE_BM, D_MODEL), blk_idx),
                  pl.BlockSpec((1, 1, D_MODEL, 2 * MOE_FF), exp_idx),
                  pl.BlockSpec((1, 1, MOE_FF, D_MODEL), exp_idx)],
        out_specs=pl.BlockSpec((MOE_BM, D_MODEL), blk_idx),
    )
    return pl.pallas_call(
        _expert_kernel,
        grid_spec=grid_spec,
        out_shape=jax.ShapeDtypeStruct(xs.shape, F32),
        compiler_params=_cparams(("arbitrary",)),
        name="moe_expert_ffn",
    )(*visits, xs, w_gate_up, w_down)


def _combine_kernel(pos_ref, nxt_ref, x_ref, r_ref, mod_ref, ys_hbm, o_ref, ybuf, sem, *, ntiles):
    i = pl.program_id(0)

    def gather(table, s):
        for r in range(TM):
            for k in range(MOE_TOPK):
                pltpu.make_async_copy(ys_hbm.at[pl.ds(table[0, k, r], 1), :], ybuf.at[s, k, pl.ds(r, 1), :],
                                      sem.at[s]).start()

    @pl.when(i == 0)
    def _():
        gather(pos_ref, 0)

    def step(s):
        @pl.when(i + 1 < ntiles)
        def _():
            gather(nxt_ref, 1 - s)
        for k in range(MOE_TOPK):
            _row_wait(ys_hbm, ybuf.at[s, k], sem.at[s])
        w = r_ref[...]
        y = w[:, MOE_TOPK:MOE_TOPK + 1] * ybuf[s, 0]
        for k in range(1, MOE_TOPK):
            y = y + w[:, MOE_TOPK + k:MOE_TOPK + k + 1] * ybuf[s, k]
        o_ref[...] = x_ref[...] + mod_ref[0][5:6, :] * y

    for s in range(2):
        pl.when(i % 2 == s)(functools.partial(step, s))


def _combine(rows, x, ys, pos, route, mod):
    n = rows.n
    spec = pl.BlockSpec((TM, D_MODEL), lambda i: (i, 0))
    return pl.pallas_call(
        functools.partial(_combine_kernel, ntiles=n),
        grid=(n,),
        in_specs=[pl.BlockSpec((1, 8, TM), lambda i: (i, 0, 0), memory_space=pltpu.SMEM),
                  pl.BlockSpec((1, 8, TM), lambda i: (jnp.minimum(i + 1, n - 1), 0, 0), memory_space=pltpu.SMEM),
                  spec,
                  pl.BlockSpec((TM, 128), lambda i: (i, 0)),
                  pl.BlockSpec((1, 8, D_MODEL), lambda i: (rows.mod(i), 0, 0)),
                  pl.BlockSpec(memory_space=pl.ANY)],
        out_specs=spec,
        out_shape=jax.ShapeDtypeStruct((n * TM, D_MODEL), F32),
        scratch_shapes=[pltpu.VMEM((2, MOE_TOPK, TM, D_MODEL), F32), pltpu.SemaphoreType.DMA((2,))],
        compiler_params=_cparams(("arbitrary",)),
        name="moe_combine",
    )(pos, pos, x, route, mod, ys)


def _visit_tables(bounds, nk):
    E = MOE_EXPERTS
    nblk = nk // MOE_BM
    nvis = nblk + E
    starts, ends = bounds[:-1], bounds[1:]
    fb = starts // MOE_BM
    nv = jnp.where(ends > starts, (ends - 1) // MOE_BM - fb + 1, 0)
    cum = jnp.cumsum(nv)
    total = cum[-1]
    v = jnp.arange(nvis, dtype=jnp.int32)
    active = v < total
    vc = jnp.minimum(v, total - 1)
    ve = jnp.minimum(jnp.sum((cum[None, :] <= vc[:, None]).astype(jnp.int32), axis=1), E - 1)
    vb = fb[ve] + (vc - (cum - nv)[ve])
    lo = jnp.where(active, jnp.maximum(starts[ve], vb * MOE_BM) - vb * MOE_BM, 0)
    hi = jnp.where(active, jnp.minimum(ends[ve], (vb + 1) * MOE_BM) - vb * MOE_BM, 0)
    prev_b = jnp.concatenate([jnp.full((1,), -1, jnp.int32), vb[:-1]])
    first = jnp.logical_and(active, vb != prev_b)
    i32 = lambda a: a.astype(jnp.int32)
    return i32(vb), i32(ve), i32(lo), i32(hi), i32(first)


def _moe(rows, x, f, route, counts, mod, layer, w_gate_up, w_down):
    n = rows.n * TM
    cum = jnp.cumsum(counts[0])
    starts = (cum - counts[0]).reshape(1, 128)
    bounds = jnp.concatenate([starts[0, MOE_GROUPS:MOE_GROUPS + MOE_EXPERTS], cum[-1:]]).astype(jnp.int32)
    pos = _assignment_slots(rows.n, route, starts)
    xs = _dispatch(rows.n, f, pos)
    ys = _expert_ffn(xs, _visit_tables(bounds, n * MOE_TOPK), layer, w_gate_up, w_down)
    return _combine(rows, x, ys, pos, route, mod)


def kernel(x, c, ctx, c_ctx, ada_w, ada_b, norm_mix, norm_ffn, rg_w_in, rg_conv_w, rg_conv_b, rg_gate_w, rg_gate_b, rg_lambda, rg_w_out, mla_w_down, mla_q_norm, mla_kv_norm, mla_w_uq, mla_w_ukv, mla_qk_norm, mla_w_o, ml_w_in, ml_gate_b, ml_out_norm, ml_w_out, moe_w_group, moe_b_group, moe_w_expert, moe_b_expert, moe_w_gate_up, moe_w_down):
    B, L, D = x.shape
    C = ctx.shape[1]
    depth = ada_w.shape[0]
    assert D == D_MODEL
    st = _Stream(B, L, C)

    xs = jnp.concatenate([ctx, x], axis=1).reshape(st.NT, D)
    cc = jnp.zeros((16, D), F32).at[:B].set(c).at[B].set(c_ctx)
    mod_all = _modulation(cc, ada_w, ada_b)
    mod_all = jnp.pad(mod_all[:, :B + 1].reshape(depth, B + 1, 6, D), ((0, 0), (0, 0), (0, 2), (0, 0)))
    perm = _row_permutation(B)

    row = lambda a: a.reshape(1, -1)

    for i in range(depth):
        last = i == depth - 1
        mod = mod_all[i]
        kind, j = i % 3, i // 3
        all_rows = st.all_rows()
        out_rows = st.latent_rows() if last else all_rows
        tile_spec = lambda w: pl.BlockSpec((TM, w), lambda k: (out_rows.src(k), 0))
        wr = jnp.zeros((D, 128), F32).at[:, :MOE_GROUPS].set(moe_w_group[i]) \
            .at[:, MOE_GROUPS:MOE_GROUPS + MOE_EXPERTS].set(moe_w_expert[i])
        br = jnp.zeros((1, 128), F32).at[0, :MOE_GROUPS].set(moe_b_group[i]) \
            .at[0, MOE_GROUPS:MOE_GROUPS + MOE_EXPERTS].set(moe_b_expert[i])
        out_args = (xs, mod, row(norm_ffn[i]), wr, br)

        if kind == 0:
            gate, u = _rg_in(st, xs, mod, row(norm_mix[i]), perm, rg_w_in[j].astype(BF16))
            wg, gb = _rg_gate_weights(rg_gate_w[j], rg_gate_b[j])
            hs = _rg_scan(st, u, rg_conv_w[j], row(rg_conv_b[j]), wg, gb, rg_lambda[j].reshape(2, 1, D))
            xs, f, route, counts = _rg_out(st, last, gate, hs, perm.T, rg_w_out[j].astype(BF16), *out_args)
        elif kind == 1:
            w_down = jnp.pad(mla_w_down[j], ((0, 0), (0, 512 - mla_w_down.shape[2]))).astype(BF16)
            down = _norm_proj(all_rows, xs, mod, row(norm_mix[i]), w_down, name="mla_down_proj")
            wq, wk, we, wv = _mla_weights(mla_w_uq[j], mla_w_ukv[j])
            q, k, v = _mla_up(st, all_rows, down, row(mla_q_norm[j]), row(mla_kv_norm[j]), wq, wk, we, wv,
                              _rope_tables(L, mla_qk_norm[j]))
            o = _attention(st, q, k, v)
            xs, f, route, counts = _mixer_out(out_rows, _lhs_mla, (o,), (tile_spec(D),),
                                      mla_w_o[j].astype(BF16), *out_args, name="mla_out")
        else:
            n_in = ml_w_in.shape[2]
            w_in = jnp.pad(ml_w_in[j], ((0, 0), (0, ML_NP - n_in))).astype(BF16)
            proj = _norm_proj(all_rows, xs, mod, row(norm_mix[i]), w_in, name="mlstm_in_proj")
            gate_bias = jnp.pad(ml_gate_b[j].reshape(1, -1), ((0, 0), (0, 128 - 4 * ML_HEADS)))
            hf, hb = _mlstm(st, proj, gate_bias)
            og_spec = pl.BlockSpec((TM, D), lambda k: (out_rows.src(k), 2))
            xs, f, route, counts = _mixer_out(out_rows, _lhs_mlstm, (hf, hb, proj, row(ml_out_norm[j])),
                                      (tile_spec(D), tile_spec(D), og_spec, pl.BlockSpec((1, D), lambda k: (0, 0))),
                                      ml_w_out[j].astype(BF16), *out_args, name="mlstm_out")

        moe_rows = st.dense_latent_rows() if last else all_rows
        xs = _moe(moe_rows, xs, f, route, counts, mod, i, moe_w_gate_up, moe_w_down)

    return xs.reshape(B, L, D)
```

```python
import functools
import math

import jax
import jax.numpy as jnp
from jax import lax
from jax.experimental import pallas as pl
from jax.experimental.pallas import tpu as pltpu

F32 = jnp.float32
BF16 = jnp.bfloat16

D_MODEL = 1024
RMS_EPS = 1e-6

TM = 256
VMEM_LIMIT = 48 * 1024 * 1024

RG_BLOCK_W = 64
RG_CHUNK = 256
RG_CONV_W = 4
RG_C = 8.0
RG_TT = 64

MLA_HEADS = 16
MLA_Q_RANK = 256
MLA_KV_RANK = 128
MLA_NOPE = 64
MLA_ROPE = 32
MLA_V = 64
MLA_QK = MLA_NOPE + MLA_ROPE
MLA_HP = 128
ROPE_AXIS_DIM = MLA_ROPE // 2
ROPE_BASE = 10000.0
GRID_W = 64
ATT_HEADS = 4

ML_HEADS = 4
ML_DV = 256
ML_DQK = 128
ML_TC = 256
ML_M_INIT = -1e30
ML_NP = 3200

MOE_GROUPS = 8
MOE_PER_GROUP = 8
MOE_EXPERTS = 64
MOE_TOPK = 2
MOE_FF = 256
MOE_BM = 256


def _cparams(sem):
    return pltpu.CompilerParams(dimension_semantics=sem, vmem_limit_bytes=VMEM_LIMIT)


def _dot(a, b):
    return jnp.dot(a, b, preferred_element_type=F32)


def _dot_t(a, b):
    return lax.dot_general(a, b, (((1,), (1,)), ((), ())), preferred_element_type=F32)


def _dot3(a, b):
    ah = a.astype(BF16)
    al = (a - ah.astype(F32)).astype(BF16)
    bh = b.astype(BF16)
    bl = (b - bh.astype(F32)).astype(BF16)
    return _dot(ah, bh) + (_dot(al, bh) + _dot(ah, bl))


def _sigmoid(x):
    return 0.5 * jnp.tanh(0.5 * x) + 0.5


def _softplus(x):
    return jnp.maximum(x, 0.0) + jnp.log1p(jnp.exp(-jnp.abs(x)))


def _gelu_tanh(x):
    return 0.5 * x * (1.0 + jnp.tanh(0.7978845608028654 * (x + 0.044715 * (x * x * x))))


def _rms(x, n=None):
    n = x.shape[-1] if n is None else n
    ms = jnp.sum(x * x, axis=-1, keepdims=True) * (1.0 / n)
    return x * lax.rsqrt(ms + RMS_EPS)


class _Rows:
    def __init__(self, n, src, mod):
        self.n, self.src, self.mod = n, src, mod


class _Stream:
    def __init__(self, B, L, C):
        assert L % TM == 0 and C % TM == 0 and B % 8 == 0
        self.B, self.L, self.C = B, L, C
        self.ltot = L + C
        self.lt, self.ct = L // TM, C // TM
        self.tpb = self.lt + self.ct
        self.NT = B * self.ltot

    def all_rows(self):
        tpb, ct, B = self.tpb, self.ct, self.B
        return _Rows(B * tpb, lambda k: k, lambda k: jnp.where(k % tpb < ct, B, k // tpb))

    def latent_rows(self):
        tpb, ct, lt = self.tpb, self.ct, self.lt
        return _Rows(self.B * lt, lambda k: (k // lt) * tpb + ct + k % lt, lambda k: k // lt)

    def dense_latent_rows(self):
        lt = self.lt
        return _Rows(self.B * lt, lambda k: k, lambda k: k // lt)


def _mod_kernel(c_ref, w_ref, b_ref, o_ref):
    c = c_ref[...]
    o_ref[0] = _dot3(c * _sigmoid(c), w_ref[0]) + b_ref[0]


def _modulation(cc, ada_w, ada_b):
    depth, d, n = ada_w.shape
    tn = 1536
    return pl.pallas_call(
        _mod_kernel,
        grid=(depth, n // tn),
        in_specs=[pl.BlockSpec((16, d), lambda l, j: (0, 0)),
                  pl.BlockSpec((1, d, tn), lambda l, j: (l, 0, j)),
                  pl.BlockSpec((1, 1, tn), lambda l, j: (l, 0, j))],
        out_specs=pl.BlockSpec((1, 16, tn), lambda l, j: (l, 0, j)),
        out_shape=jax.ShapeDtypeStruct((depth, 16, n), F32),
        compiler_params=_cparams(("arbitrary", "arbitrary")),
        name="ada_modulation",
    )(cc, ada_w, ada_b.reshape(depth, 1, n))


def _norm_mod(x, gain, shift, scale):
    return _rms(x) * gain * (1.0 + scale) + shift


def _norm_proj_kernel(x_ref, mod_ref, g_ref, w_ref, o_ref):
    mod = mod_ref[0]
    h = _norm_mod(x_ref[...], g_ref[...], mod[0:1, :], mod[1:2, :]).astype(BF16)
    o_ref[...] = _dot(h, w_ref[...]).astype(o_ref.dtype)


def _norm_proj(rows, x, mod, gain, w, name):
    n = w.shape[1]
    const = lambda k: (0, 0)
    return pl.pallas_call(
        _norm_proj_kernel,
        grid=(rows.n,),
        in_specs=[pl.BlockSpec((TM, D_MODEL), lambda k: (rows.src(k), 0)),
                  pl.BlockSpec((1, 8, D_MODEL), lambda k: (rows.mod(k), 0, 0)),
                  pl.BlockSpec((1, D_MODEL), const),
                  pl.BlockSpec((D_MODEL, n), const)],
        out_specs=pl.BlockSpec((TM, n), lambda k: (k, 0)),
        out_shape=jax.ShapeDtypeStruct((rows.n * TM, n), F32),
        compiler_params=_cparams(("arbitrary",)),
        name=name,
    )(x, mod, gain, w)


RG_TR = 32


def _row_permutation(B):
    n = B * RG_TR
    r = jnp.arange(n)
    src = (r % B) * RG_TR + r // B
    return (src[:, None] == jnp.arange(n)[None, :]).astype(BF16)


def _time_tile_mod(mod_ref, is_ctx, B, row):
    return jnp.where(is_ctx, mod_ref[B:B + 1, row:row + 1, :], mod_ref[0:B, row:row + 1, :])


def _rg_in_kernel(x_ref, mod_ref, g_ref, perm_ref, wg_ref, wu_ref, gate_ref, u_ref, *, B, ctx_tiles):
    is_ctx = pl.program_id(0) < ctx_tiles
    h = _norm_mod(x_ref[...], g_ref[...], _time_tile_mod(mod_ref, is_ctx, B, 0), _time_tile_mod(mod_ref, is_ctx, B, 1))
    h = h.reshape(B * RG_TR, D_MODEL).astype(BF16)
    h = _dot(perm_ref[...], h).astype(BF16)
    gate_ref[...] = _gelu_tanh(_dot(h, wg_ref[...])).astype(gate_ref.dtype)
    u_ref[...] = _dot(h, wu_ref[...])


def _rg_in(st, x, mod, gain, perm, w_in):
    B, W = st.B, D_MODEL
    R = B * RG_TR
    nt = st.ltot // RG_TR
    const = lambda t: (0, 0)
    return pl.pallas_call(
        functools.partial(_rg_in_kernel, B=B, ctx_tiles=st.C // RG_TR),
        grid=(nt,),
        in_specs=[pl.BlockSpec((B, RG_TR, D_MODEL), lambda t: (0, t, 0)),
                  pl.BlockSpec((B + 1, 8, D_MODEL), lambda t: (0, 0, 0)),
                  pl.BlockSpec((1, D_MODEL), const),
                  pl.BlockSpec((R, R), const),
                  pl.BlockSpec((D_MODEL, W), const),
                  pl.BlockSpec((D_MODEL, W), lambda t: (0, 1))],
        out_specs=[pl.BlockSpec((R, W), lambda t: (t, 0)),
                   pl.BlockSpec((R, W), lambda t: (t, 0))],
        out_shape=[jax.ShapeDtypeStruct((st.ltot * B, W), BF16), jax.ShapeDtypeStruct((st.ltot * B, W), F32)],
        compiler_params=_cparams(("arbitrary",)),
        name="rg_in_proj",
    )(x.reshape(B, st.ltot, D_MODEL), mod, gain, perm, w_in, w_in)


def _rg_tile_order(d, k, ct, ntt):
    bwd = jnp.where(k < ct, ct - 1 - k, ntt - 1 - (k - ct))
    return jnp.where(d == 0, k, bwd)


def _rg_scan_kernel(um_ref, up_ref, un_ref, cw_ref, cb_ref, wg_ref, gb_ref, lam_ref, o_ref,
                    ext_ref, a_ref, b_ref, h_ref, *, B, ct, ntt):
    d = pl.program_id(0)
    k = pl.program_id(1)
    tile = _rg_tile_order(d, k, ct, ntt)
    R = RG_TT * B

    @pl.when(k == 0)
    def _():
        h_ref[...] = jnp.zeros_like(h_ref)

    seq_start = jnp.logical_or(tile == 0, tile == ct)
    seq_end = jnp.logical_or(tile == ct - 1, tile == ntt - 1)
    ext_ref[0:2 * B, :] = jnp.where(seq_start, 0.0, up_ref[...])
    ext_ref[2 * B:2 * B + R, :] = um_ref[...]
    ext_ref[2 * B + R:3 * B + R, :] = jnp.where(seq_end, 0.0, un_ref[...])
    cw = cw_ref[...]
    uc = cb_ref[...] + cw[0:1, :] * ext_ref[0:R, :]
    for j in range(1, RG_CONV_W):
        uc = uc + cw[j:j + 1, :] * ext_ref[j * B:j * B + R, :]
    ucb = uc.astype(BF16)
    c_lam = -RG_C * _softplus(-lam_ref[0])
    for c in range(D_MODEL // RG_CHUNK):
        sl = slice(c * RG_CHUNK, (c + 1) * RG_CHUNK)
        z = _dot(ucb[:, sl], wg_ref[0, c]) + gb_ref[0, c]
        r = _sigmoid(z[:, :RG_CHUNK])
        ig = _sigmoid(z[:, RG_CHUNK:])
        log_a = c_lam[:, sl] * r
        a = jnp.exp(log_a)
        one_minus_a2 = -jnp.tanh(log_a) * (a * a + 1.0)
        a_ref[:, sl] = a
        b_ref[:, sl] = jnp.sqrt(one_minus_a2) * (ig * uc[:, sl])

    def scan(times):
        for c in range(D_MODEL // 128):
            cs = slice(c * 128, (c + 1) * 128)
            h = h_ref[:, cs]
            for t in times:
                rs = slice(t * B, (t + 1) * B)
                h = a_ref[rs, cs] * h + b_ref[rs, cs]
                b_ref[rs, cs] = h
            h_ref[:, cs] = h

    pl.when(d == 0)(lambda: scan(range(RG_TT)))
    pl.when(d == 1)(lambda: scan(range(RG_TT - 1, -1, -1)))
    o_ref[0] = b_ref[...].astype(o_ref.dtype)


def _rg_scan(st, u_tm, conv_w, conv_b, wg, gb, lam):
    B, W = st.B, D_MODEL
    assert st.C % RG_TT == 0 and st.L % RG_TT == 0
    ltot = st.ltot
    ntt, ct = ltot // RG_TT, st.C // RG_TT
    R = RG_TT * B
    order = functools.partial(_rg_tile_order, ct=ct, ntt=ntt)
    nch = W // RG_CHUNK
    return pl.pallas_call(
        functools.partial(_rg_scan_kernel, B=B, ct=ct, ntt=ntt),
        grid=(2, ntt),
        in_specs=[pl.BlockSpec((R, W), lambda d, k: (order(d, k), 0)),
                  pl.BlockSpec((2 * B, W), lambda d, k: (jnp.maximum(order(d, k) * (RG_TT // 2) - 1, 0), 0)),
                  pl.BlockSpec((B, W), lambda d, k: (jnp.minimum((order(d, k) + 1) * RG_TT, ltot - 1), 0)),
                  pl.BlockSpec((RG_CONV_W, W), lambda d, k: (0, 0)),
                  pl.BlockSpec((1, W), lambda d, k: (0, 0)),
                  pl.BlockSpec((1, nch, RG_CHUNK, 2 * RG_CHUNK), lambda d, k: (d, 0, 0, 0)),
                  pl.BlockSpec((1, nch, 1, 2 * RG_CHUNK), lambda d, k: (d, 0, 0, 0)),
                  pl.BlockSpec((1, 1, W), lambda d, k: (d, 0, 0))],
        out_specs=pl.BlockSpec((1, R, W), lambda d, k: (d, order(d, k), 0)),
        out_shape=jax.ShapeDtypeStruct((2, ltot * B, W), BF16),
        scratch_shapes=[pltpu.VMEM((3 * B + R, W), F32),
                        pltpu.VMEM((R, W), F32),
                        pltpu.VMEM((R, W), F32),
                        pltpu.VMEM((B, W), F32)],
        compiler_params=_cparams(("arbitrary", "arbitrary")),
        name="rg_scan",
    )(u_tm, u_tm, u_tm, conv_w, conv_b, wg, gb, lam)


def _rg_gate_weights(gate_w, gate_b):
    nb = gate_w.shape[2]
    per = RG_CHUNK // RG_BLOCK_W
    nch = nb // per
    gw = gate_w.reshape(2, 2, nch, per, RG_BLOCK_W, RG_BLOCK_W)
    eye = jnp.eye(per, dtype=gate_w.dtype)
    bd = jnp.einsum('dgcnij,nm->dgcnimj', gw, eye).reshape(2, 2, nch, RG_CHUNK, RG_CHUNK)
    wg = jnp.concatenate([bd[:, 0], bd[:, 1]], axis=-1).astype(BF16)
    gb = gate_b.reshape(2, 2, nch, 1, RG_CHUNK)
    gb = jnp.concatenate([gb[:, 0], gb[:, 1]], axis=-1)
    return wg, gb


def _rg_out_kernel(g_ref, hf_ref, hb_ref, perm_ref, w_ref, x_ref, mod_ref, gain_ref, wr_ref, br_ref,
                   xo_ref, f_ref, r_ref, cnt_ref, *, B, ctx_tiles, t0):
    is_ctx = pl.program_id(0) + t0 < ctx_tiles
    hsum = hf_ref[0].astype(F32) + hb_ref[0].astype(F32)
    lhs = (g_ref[...].astype(F32) * hsum).astype(BF16)
    lhs = _dot(perm_ref[...], lhs).astype(BF16)
    y = _dot(lhs, w_ref[...]).reshape(B, RG_TR, D_MODEL)
    m = lambda row: _time_tile_mod(mod_ref, is_ctx, B, row)
    xn = x_ref[...] + m(2) * y
    xo_ref[...] = xn
    f = _norm_mod(xn, gain_ref[...], m(3), m(4))
    f_ref[...] = f
    lg = _dot3(f.reshape(B * RG_TR, D_MODEL), wr_ref[...]) + br_ref[...]
    table, hist = _route(lg)
    r_ref[...] = table.reshape(B, RG_TR, 128)
    _accumulate_counts(cnt_ref, hist)


def _rg_out(st, latent_only, gate, hs, perm, w_out, x, mod, gain, wr, br):
    B = st.B
    R = B * RG_TR
    t0 = st.C // RG_TR if latent_only else 0
    lo = st.L if latent_only else st.ltot
    nt = lo // RG_TR
    const = lambda t: (0, 0)
    blk = lambda w: pl.BlockSpec((B, RG_TR, w), lambda t: (0, t, 0))
    xo, f, route, counts = pl.pallas_call(
        functools.partial(_rg_out_kernel, B=B, ctx_tiles=st.C // RG_TR, t0=t0),
        grid=(nt,),
        in_specs=[pl.BlockSpec((R, D_MODEL), lambda t: (t + t0, 0)),
                  pl.BlockSpec((1, R, D_MODEL), lambda t: (0, t + t0, 0)),
                  pl.BlockSpec((1, R, D_MODEL), lambda t: (1, t + t0, 0)),
                  pl.BlockSpec((R, R), const),
                  pl.BlockSpec((D_MODEL, D_MODEL), const),
                  pl.BlockSpec((B, RG_TR, D_MODEL), lambda t: (0, t + t0, 0)),
                  pl.BlockSpec((B + 1, 8, D_MODEL), lambda t: (0, 0, 0)),
                  pl.BlockSpec((1, D_MODEL), const),
                  pl.BlockSpec((D_MODEL, 128), const),
                  pl.BlockSpec((1, 128), const)],
        out_specs=[blk(D_MODEL), blk(D_MODEL), blk(128), pl.BlockSpec((1, 128), const)],
        out_shape=[jax.ShapeDtypeStruct((B, lo, D_MODEL), F32),
                   jax.ShapeDtypeStruct((B, lo, D_MODEL), F32),
                   jax.ShapeDtypeStruct((B, lo, 128), F32),
                   jax.ShapeDtypeStruct((1, 128), F32)],
        compiler_params=_cparams(("arbitrary",)),
        name="rg_out",
    )(gate, hs, hs, perm, w_out, x.reshape(B, st.ltot, D_MODEL), mod, gain, wr, br)
    return xo.reshape(B * lo, D_MODEL), f.reshape(B * lo, D_MODEL), route.reshape(B * lo, 128), counts


_ROPE_HALF = ROPE_AXIS_DIM // 2
_MLA_SRC_DIM = (list(range(MLA_NOPE + ROPE_AXIS_DIM)) + list(range(MLA_NOPE, MLA_NOPE + _ROPE_HALF))
                + list(range(MLA_NOPE + ROPE_AXIS_DIM, MLA_QK))
                + list(range(MLA_NOPE + ROPE_AXIS_DIM, MLA_NOPE + ROPE_AXIS_DIM + _ROPE_HALF)))
_MLA_REAL_LANE = ([1.0] * (MLA_NOPE + ROPE_AXIS_DIM) + [0.0] * _ROPE_HALF + [1.0] * ROPE_AXIS_DIM
                  + [0.0] * _ROPE_HALF + [0.0] * (MLA_HP - len(_MLA_SRC_DIM)))


def _mla_up_kernel(dn_ref, qn_ref, kvn_ref, wq_ref, wk_ref, we_ref, wv_ref, real_ref, cq_ref, sq_ref, ck_ref, sk_ref,
                   q_ref, k_ref, v_ref):
    dn = dn_ref[...]
    cq = _rms(dn[:, :MLA_Q_RANK]) * qn_ref[...]
    ckv = _rms(dn[:, MLA_Q_RANK:MLA_Q_RANK + MLA_KV_RANK]) * kvn_ref[...]
    kr = dn[:, MLA_Q_RANK + MLA_KV_RANK:]
    kr_hi = kr.astype(BF16)
    kr_lo = (kr - kr_hi.astype(F32)).astype(BF16)
    ckvb = ckv.astype(BF16)
    q_pre = _dot(cq.astype(BF16), wq_ref[...])
    k_pre = _dot(ckvb, wk_ref[...]) + (_dot(kr_hi, we_ref[...]) + _dot(kr_lo, we_ref[...]))
    v_ref[...] = _dot(ckvb, wv_ref[...]).astype(v_ref.dtype)
    real = real_ref[...]

    def head(x, cos_g, sin_g):
        ms = jnp.sum(x * x * real, axis=-1, keepdims=True) * (1.0 / MLA_QK)
        xr = x * lax.rsqrt(ms + RMS_EPS)
        return xr * cos_g + pltpu.roll(xr, MLA_HP - _ROPE_HALF, 1) * sin_g

    cos_q, sin_q, cos_k, sin_k = cq_ref[...], sq_ref[...], ck_ref[...], sk_ref[...]
    for h in range(MLA_HEADS):
        sl = slice(h * MLA_HP, (h + 1) * MLA_HP)
        q_ref[:, sl] = head(q_pre[:, sl], cos_q, sin_q).astype(q_ref.dtype)
        k_ref[:, sl] = head(k_pre[:, sl], cos_k, sin_k).astype(k_ref.dtype)


def _mla_up(st, rows, down, q_norm, kv_norm, wq, wk, we, wv, tables):
    hw = MLA_HEADS * MLA_HP
    const = lambda i: (0, 0)
    tpb, ct, lt = st.tpb, st.ct, st.lt
    rope_idx = lambda i: (jnp.where(i % tpb < ct, lt, i % tpb - ct), 0)
    real = jnp.asarray(_MLA_REAL_LANE, F32).reshape(1, MLA_HP)
    return pl.pallas_call(
        _mla_up_kernel,
        grid=(rows.n,),
        in_specs=[pl.BlockSpec((TM, 512), lambda i: (i, 0)),
                  pl.BlockSpec((1, MLA_Q_RANK), const),
                  pl.BlockSpec((1, MLA_KV_RANK), const),
                  pl.BlockSpec((MLA_Q_RANK, hw), const),
                  pl.BlockSpec((MLA_KV_RANK, hw), const),
                  pl.BlockSpec((128, hw), const),
                  pl.BlockSpec((MLA_KV_RANK, MLA_HEADS * MLA_V), const),
                  pl.BlockSpec((1, MLA_HP), const)] + [pl.BlockSpec((TM, MLA_HP), rope_idx)] * 4,
        out_specs=[pl.BlockSpec((TM, hw), lambda i: (i, 0)),
                   pl.BlockSpec((TM, hw), lambda i: (i, 0)),
                   pl.BlockSpec((TM, MLA_HEADS * MLA_V), lambda i: (i, 0))],
        out_shape=[jax.ShapeDtypeStruct((st.NT, hw), BF16),
                   jax.ShapeDtypeStruct((st.NT, hw), BF16),
                   jax.ShapeDtypeStruct((st.NT, MLA_HEADS * MLA_V), BF16)],
        compiler_params=_cparams(("arbitrary",)),
        name="mla_up_proj",
    )(down, q_norm, kv_norm, wq, wk, we, wv, real, *tables)


def _attn_kernel(q_ref, k_ref, v_ref, o_ref, vaug_ref, *, C, ct):
    qi = pl.program_id(2)

    @pl.when(qi == 0)
    def _():
        lane = lax.broadcasted_iota(jnp.int32, (k_ref.shape[0], 2 * MLA_V), 1)
        for hh in range(ATT_HEADS):
            pair = v_ref[:, (hh // 2) * 2 * MLA_V:(hh // 2 + 1) * 2 * MLA_V].astype(F32)
            if hh % 2 == 0:
                aug = jnp.where(lane < MLA_V, pair, jnp.where(lane == MLA_V, 1.0, 0.0))
            else:
                aug = jnp.where(lane >= MLA_V, pair, jnp.where(lane == 0, 1.0, 0.0))
            vaug_ref[hh] = aug.astype(BF16)

    def attend(nkeys):
        lane = lax.broadcasted_iota(jnp.int32, (TM, 2 * MLA_V), 1)
        for pair in range(ATT_HEADS // 2):
            outs = []
            for hh in (2 * pair, 2 * pair + 1):
                sl = slice(hh * MLA_HP, (hh + 1) * MLA_HP)
                s = _dot_t(q_ref[:, sl], k_ref[0:nkeys, sl])
                p = jnp.exp2(s - jnp.max(s, axis=-1, keepdims=True)).astype(BF16)
                o = _dot(p, vaug_ref[hh, 0:nkeys, :])
                rowsum = o[:, MLA_V:MLA_V + 1] if hh % 2 == 0 else o[:, 0:1]
                outs.append(o * (1.0 / rowsum))
            o_ref[:, pair * 2 * MLA_V:(pair + 1) * 2 * MLA_V] = jnp.where(lane < MLA_V, outs[0], outs[1]).astype(o_ref.dtype)

    pl.when(qi < ct)(lambda: attend(C))
    pl.when(qi >= ct)(lambda: attend(k_ref.shape[0]))


def _attention(st, q, k, v):
    B, ltot, tpb = st.B, st.ltot, st.tpb
    hg = MLA_HEADS // ATT_HEADS
    return pl.pallas_call(
        functools.partial(_attn_kernel, C=st.C, ct=st.ct),
        grid=(B, hg, tpb),
        in_specs=[pl.BlockSpec((TM, ATT_HEADS * MLA_HP), lambda b, h, i: (b * tpb + i, h)),
                  pl.BlockSpec((ltot, ATT_HEADS * MLA_HP), lambda b, h, i: (b, h)),
                  pl.BlockSpec((ltot, ATT_HEADS * MLA_V), lambda b, h, i: (b, h))],
        out_specs=pl.BlockSpec((TM, ATT_HEADS * MLA_V), lambda b, h, i: (b * tpb + i, h)),
        out_shape=jax.ShapeDtypeStruct((st.NT, MLA_HEADS * MLA_V), BF16),
        scratch_shapes=[pltpu.VMEM((ATT_HEADS, ltot, 2 * MLA_V), BF16)],
        compiler_params=_cparams(("arbitrary", "arbitrary", "arbitrary")),
        name="mla_attention",
    )(q, k, v)


def _mla_weights(w_uq, w_ukv):
    H = MLA_HEADS
    src = jnp.asarray(_MLA_SRC_DIM, jnp.int32)
    pad = MLA_HP - len(_MLA_SRC_DIM)
    wq = jnp.pad(w_uq.reshape(MLA_Q_RANK, H, MLA_QK)[:, :, src], ((0, 0), (0, 0), (0, pad)))
    wkv = w_ukv.reshape(MLA_KV_RANK, H, MLA_NOPE + MLA_V)
    wk = jnp.pad(wkv[:, :, :MLA_NOPE], ((0, 0), (0, 0), (0, MLA_HP - MLA_NOPE)))
    wv = wkv[:, :, MLA_NOPE:]
    r = jnp.arange(128)[:, None]
    lane_dim = jnp.pad(src, (0, pad), constant_values=-1)[None, :]
    place = (lane_dim == r + MLA_NOPE).astype(BF16)
    we = jnp.tile(place, (1, H))
    return (wq.reshape(MLA_Q_RANK, H * MLA_HP).astype(BF16), wk.reshape(MLA_KV_RANK, H * MLA_HP).astype(BF16),
            we, wv.reshape(MLA_KV_RANK, H * MLA_V).astype(BF16))


def _rope_tables(L, qk_norm):
    rows = L // GRID_W
    row = jnp.broadcast_to(jnp.arange(rows, dtype=F32)[:, None], (rows, GRID_W)).reshape(L)
    col = jnp.broadcast_to(jnp.arange(GRID_W, dtype=F32)[None, :], (rows, GRID_W)).reshape(L)
    inv_freq = ROPE_BASE ** (-jnp.arange(0, ROPE_AXIS_DIM, 2, dtype=F32) / ROPE_AXIS_DIM)
    ar = row[:, None] * inv_freq
    ac = col[:, None] * inv_freq
    h8 = _ROPE_HALF
    one = jnp.ones((L, MLA_NOPE), F32)
    z8 = jnp.zeros((L, h8), F32)
    zpad = jnp.zeros((L, MLA_HP - len(_MLA_SRC_DIM)), F32)
    cos_t = jnp.concatenate([one, jnp.cos(ar), jnp.cos(ar), z8, jnp.cos(ac), jnp.cos(ac), z8, zpad], axis=1)
    sin_t = jnp.concatenate([0 * one, -jnp.sin(ar), jnp.sin(ar), z8, -jnp.sin(ac), jnp.sin(ac), z8, zpad], axis=1)
    ident = jnp.asarray(_MLA_REAL_LANE, F32)[None, :]
    cos_t = jnp.concatenate([cos_t, jnp.broadcast_to(ident, (TM, MLA_HP))], axis=0)
    sin_t = jnp.concatenate([sin_t, jnp.zeros((TM, MLA_HP), F32)], axis=0)
    src = jnp.asarray(_MLA_SRC_DIM, jnp.int32)
    pad = MLA_HP - len(_MLA_SRC_DIM)
    scale = MLA_QK ** -0.5 * math.log2(math.e)
    tables = []
    for g, s in ((qk_norm[0], scale), (qk_norm[1], 1.0)):
        g_lane = jnp.pad(g[src], (0, pad))
        g_partner = jnp.roll(g_lane, -h8)
        tables += [cos_t * (g_lane * s)[None, :], sin_t * (g_partner * s)[None, :]]
    return tables


def _log_sigmoid(x):
    return jnp.minimum(x, 0.0) - jnp.log1p(jnp.exp(-jnp.abs(x)))


def _mlstm_kernel(qf_ref, kf_ref, vf_ref, gf_ref, qb_ref, kb_ref, vb_ref, gb_ref, bias_ref, tril_ref, triu_ref,
                  of_ref, ob_ref, c_ref, n_ref, m_ref):
    T = ML_TC

    @pl.when(pl.program_id(1) == 0)
    def _():
        c_ref[...] = jnp.zeros_like(c_ref)
        n_ref[...] = jnp.zeros_like(n_ref)
        m_ref[...] = jnp.full(m_ref.shape, ML_M_INIT, F32)

    ti = lax.broadcasted_iota(jnp.int32, (T, T), 0)
    si = lax.broadcasted_iota(jnp.int32, (T, T), 1)
    dirs = ((qf_ref, kf_ref, vf_ref, gf_ref, of_ref, tril_ref), (qb_ref, kb_ref, vb_ref, gb_ref, ob_ref, triu_ref))
    for d, (q_ref, k_ref, v_ref, g_ref, o_ref, tri_ref) in enumerate(dirs):
        tri = (si <= ti) if d == 0 else (si >= ti)
        g = g_ref[...] + bias_ref[...]
        g_t = g.T
        lsg = _log_sigmoid(g)
        lsg_hi = lsg.astype(BF16)
        lsg_lo = (lsg - lsg_hi.astype(F32)).astype(BF16)
        cum = _dot(tri_ref[...], lsg_hi) + _dot(tri_ref[...], lsg_lo)
        cum_t = cum.T
        last = T - 1 if d == 0 else 0
        for h in range(ML_HEADS):
            st = d * ML_HEADS + h
            li, lf_ = (2 * d) * ML_HEADS + h, (2 * d + 1) * ML_HEADS + h
            ig_col = g[:, li:li + 1]
            ig_row = g_t[li:li + 1, :]
            b_col = cum[:, lf_:lf_ + 1]
            b_row = cum_t[lf_:lf_ + 1, :]
            total = cum[last:last + 1, lf_:lf_ + 1]
            m_old = m_ref[st, 0:1, 0:1]
            d_log = jnp.where(tri, b_col - b_row + ig_row, -jnp.inf)
            inter_log = b_col + m_old
            m_t = jnp.maximum(inter_log, jnp.max(d_log, axis=1, keepdims=True))
            qh = q_ref[:, h * ML_DQK:(h + 1) * ML_DQK] * (ML_DQK ** -0.5)
            kh = k_ref[:, h * ML_DQK:(h + 1) * ML_DQK]
            vh = v_ref[:, h * ML_DV:(h + 1) * ML_DV].astype(BF16)
            qb16 = qh.astype(BF16)
            s_mat = _dot_t(qb16, kh.astype(BF16)) * jnp.exp(d_log - m_t)
            inter = jnp.exp(inter_log - m_t)
            c_old = c_ref[st]
            n_old = n_ref[st, 0:1, :]
            num = _dot(s_mat.astype(BF16), vh) + inter * _dot(qb16, c_old.astype(BF16))
            den = jnp.sum(s_mat, axis=1, keepdims=True) + inter * jnp.sum(qh * n_old, axis=1, keepdims=True)
            o_ref[:, h * ML_DV:(h + 1) * ML_DV] = num / jnp.maximum(jnp.abs(den), jnp.exp(-m_t))
            w_log = total - b_col + ig_col
            m_new = jnp.maximum(total + m_old, jnp.max(w_log, axis=0, keepdims=True))
            w = jnp.exp(w_log - m_new)
            decay = jnp.exp(total + m_old - m_new)
            kw = kh * w
            c_ref[st] = decay * c_old + _dot(kw.T.astype(BF16), vh)
            n_ref[st, 0:1, :] = decay * n_old + jnp.sum(kw, axis=0, keepdims=True)
            m_ref[st] = jnp.broadcast_to(m_new, m_ref.shape[1:])


def _mlstm(st, proj, gate_bias):
    B = st.B
    assert st.L % ML_TC == 0 and st.C % ML_TC == 0
    cc = st.C // ML_TC
    nch = st.ltot // ML_TC
    qw = ML_HEADS * ML_DQK
    vw = ML_HEADS * ML_DV
    gcol = (2 * qw + 2 * vw) // 128

    def rb(d, b, k):
        chunk = k if d == 0 else jnp.where(k < cc, cc - 1 - k, nch - 1 - (k - cc))
        return b * nch + chunk

    def specs(d):
        return [pl.BlockSpec((ML_TC, qw), lambda b, k: (rb(d, b, k), 0)),
                pl.BlockSpec((ML_TC, qw), lambda b, k: (rb(d, b, k), 1)),
                pl.BlockSpec((ML_TC, vw), lambda b, k: (rb(d, b, k), (2 * qw) // vw)),
                pl.BlockSpec((ML_TC, 128), lambda b, k: (rb(d, b, k), gcol))]

    nst = 2 * ML_HEADS
    out = jax.ShapeDtypeStruct((st.NT, vw), F32)
    tril = jnp.tril(jnp.ones((ML_TC, ML_TC), BF16))
    return pl.pallas_call(
        _mlstm_kernel,
        grid=(B, nch),
        in_specs=specs(0) + specs(1) + [pl.BlockSpec((1, 128), lambda b, k: (0, 0)),
                                        pl.BlockSpec((ML_TC, ML_TC), lambda b, k: (0, 0)),
                                        pl.BlockSpec((ML_TC, ML_TC), lambda b, k: (0, 0))],
        out_specs=[pl.BlockSpec((ML_TC, vw), lambda b, k: (rb(0, b, k), 0)),
                   pl.BlockSpec((ML_TC, vw), lambda b, k: (rb(1, b, k), 0))],
        out_shape=[out, out],
        scratch_shapes=[pltpu.VMEM((nst, ML_DQK, ML_DV), F32),
                        pltpu.VMEM((nst, 8, ML_DQK), F32),
                        pltpu.VMEM((nst, 8, 128), F32)],
        compiler_params=_cparams(("arbitrary", "arbitrary")),
        name="mlstm_chunks",
    )(proj, proj, proj, proj, proj, proj, proj, proj, gate_bias, tril, tril.T)


def _route(lg):
    lane_i = lax.broadcasted_iota(jnp.int32, lg.shape, 1)
    lane = lane_i.astype(F32)
    neg = -jnp.inf
    gl = jnp.where(lane_i < MOE_GROUPS, lg, neg)
    gmax = jnp.max(gl, axis=-1, keepdims=True)
    gsum = jnp.sum(jnp.where(lane_i < MOE_GROUPS, jnp.exp(lg - gmax), 0.0), axis=-1, keepdims=True)
    p_top = 1.0 / gsum
    g_sel = jnp.min(jnp.where(gl == gmax, lane, 128.0), axis=-1, keepdims=True)
    group_of_lane = (lane_i >> 3).astype(F32) - 1.0
    el = jnp.where(group_of_lane == g_sel, lg, neg)
    e1 = jnp.max(el, axis=-1, keepdims=True)
    i1 = jnp.min(jnp.where(el == e1, lane, 128.0), axis=-1, keepdims=True)
    el2 = jnp.where(lane == i1, neg, el)
    e2 = jnp.max(el2, axis=-1, keepdims=True)
    i2 = jnp.min(jnp.where(el2 == e2, lane, 128.0), axis=-1, keepdims=True)
    t = jnp.exp(e2 - e1)
    w1 = p_top / (1.0 + t)
    w2 = w1 * t
    id1 = i1 - MOE_GROUPS
    id2 = i2 - MOE_GROUPS
    table = jnp.where(lane_i == 0, id1, jnp.where(lane_i == 1, id2,
                                                  jnp.where(lane_i == 2, w1, jnp.where(lane_i == 3, w2, 0.0))))
    chosen = jnp.where(lane == i1, 1.0, 0.0) + jnp.where(lane == i2, 1.0, 0.0)
    return table, jnp.sum(chosen, axis=0, keepdims=True)


def _accumulate_counts(cnt_ref, hist):
    @pl.when(pl.program_id(0) == 0)
    def _():
        cnt_ref[...] = jnp.zeros_like(cnt_ref)
    cnt_ref[...] += hist


def _lhs_mla(o_ref):
    return o_ref[...]


def _lhs_mlstm(hf_ref, hb_ref, og_ref, onorm_ref):
    hs = hf_ref[...] + hb_ref[...]
    og = _sigmoid(og_ref[...])
    parts = []
    for h in range(ML_HEADS):
        sl = slice(h * ML_DV, (h + 1) * ML_DV)
        parts.append(_rms(hs[:, sl]) * onorm_ref[:, sl] * og[:, sl])
    return jnp.concatenate(parts, axis=1)


def _mixer_out_kernel(*refs, n_lhs, lhs_fn):
    lhs_refs = refs[:n_lhs]
    w_ref, x_ref, mod_ref, gain_ref, wr_ref, br_ref, xo_ref, f_ref, r_ref, cnt_ref = refs[n_lhs:]
    y = _dot(lhs_fn(*lhs_refs).astype(BF16), w_ref[...])
    mod = mod_ref[0]
    xn = x_ref[...] + mod[2:3, :] * y
    xo_ref[...] = xn
    f = _norm_mod(xn, gain_ref[...], mod[3:4, :], mod[4:5, :])
    f_ref[...] = f
    r_ref[...], hist = _route(_dot3(f, wr_ref[...]) + br_ref[...])
    _accumulate_counts(cnt_ref, hist)


def _mixer_out(rows, lhs_fn, lhs_args, lhs_specs, w_out, x, mod, gain, wr, br, name):
    n = rows.n * TM
    const = lambda k: (0, 0)
    out = lambda w: pl.BlockSpec((TM, w), lambda k: (k, 0))
    return pl.pallas_call(
        functools.partial(_mixer_out_kernel, n_lhs=len(lhs_args), lhs_fn=lhs_fn),
        grid=(rows.n,),
        in_specs=list(lhs_specs) + [
            pl.BlockSpec((D_MODEL, D_MODEL), const),
            pl.BlockSpec((TM, D_MODEL), lambda k: (rows.src(k), 0)),
            pl.BlockSpec((1, 8, D_MODEL), lambda k: (rows.mod(k), 0, 0)),
            pl.BlockSpec((1, D_MODEL), const),
            pl.BlockSpec((D_MODEL, 128), const),
            pl.BlockSpec((1, 128), const)],
        out_specs=[out(D_MODEL), out(D_MODEL), out(128), pl.BlockSpec((1, 128), const)],
        out_shape=[jax.ShapeDtypeStruct((n, D_MODEL), F32),
                   jax.ShapeDtypeStruct((n, D_MODEL), F32),
                   jax.ShapeDtypeStruct((n, 128), F32),
                   jax.ShapeDtypeStruct((1, 128), F32)],
        compiler_params=_cparams(("arbitrary",)),
        name=name,
    )(*lhs_args, w_out, x, mod, gain, wr, br)


RANK_TILES = 4


def _rank_kernel(r_ref, start_ref, tril_ref, pos_ref, carry_ref):
    @pl.when(pl.program_id(0) == 0)
    def _():
        carry_ref[...] = jnp.zeros_like(carry_ref)

    r = r_ref[...]
    lane = lax.broadcasted_iota(jnp.int32, r.shape, 1).astype(F32)
    chosen = [jnp.where(lane == r[:, k:k + 1] + MOE_GROUPS, 1.0, 0.0) for k in range(MOE_TOPK)]
    both = functools.reduce(jnp.add, chosen)
    before = _dot(tril_ref[...], both.astype(BF16)) + (start_ref[...] + carry_ref[...])
    lane_i = lax.broadcasted_iota(jnp.int32, r.shape, 1)
    pos = jnp.zeros(r.shape, F32)
    for k in range(MOE_TOPK):
        pos = jnp.where(lane_i == k, jnp.sum(chosen[k] * before, axis=-1, keepdims=True), pos)
    pos_t = pos.T[0:8, :].astype(jnp.int32)
    for j in range(RANK_TILES):
        pos_ref[j] = pos_t[:, j * TM:(j + 1) * TM]
    carry_ref[...] += jnp.sum(both, axis=0, keepdims=True)


def _assignment_slots(ntiles, route, starts):
    assert ntiles % RANK_TILES == 0
    rows = RANK_TILES * TM
    tril = jnp.tril(jnp.ones((rows, rows), BF16), -1)
    return pl.pallas_call(
        _rank_kernel,
        grid=(ntiles // RANK_TILES,),
        in_specs=[pl.BlockSpec((rows, 128), lambda i: (i, 0)),
                  pl.BlockSpec((1, 128), lambda i: (0, 0)),
                  pl.BlockSpec((rows, rows), lambda i: (0, 0))],
        out_specs=pl.BlockSpec((RANK_TILES, 8, TM), lambda i: (i, 0, 0)),
        out_shape=jax.ShapeDtypeStruct((ntiles, 8, TM), jnp.int32),
        scratch_shapes=[pltpu.VMEM((1, 128), F32)],
        compiler_params=_cparams(("arbitrary",)),
        name="moe_rank",
    )(route, starts, tril)


def _row_wait(hbm, buf, sem):
    pltpu.make_async_copy(hbm.at[pl.ds(0, TM), :], buf, sem).wait()


def _dispatch_kernel(pos_ref, f_ref, xs_hbm, buf, sem, *, ntiles):
    i = pl.program_id(0)

    def step(s):
        @pl.when(i >= 2)
        def _():
            for _ in range(MOE_TOPK):
                _row_wait(xs_hbm, buf.at[s], sem.at[s])
        buf[s] = f_ref[...]
        for r in range(TM):
            for k in range(MOE_TOPK):
                pltpu.make_async_copy(buf.at[s, pl.ds(r, 1), :], xs_hbm.at[pl.ds(pos_ref[0, k, r], 1), :],
                                      sem.at[s]).start(priority=k % 2)

        @pl.when(i == ntiles - 1)
        def _():
            for slot in ((1 - s, s) if ntiles >= 2 else (s,)):
                for _ in range(MOE_TOPK):
                    _row_wait(xs_hbm, buf.at[slot], sem.at[slot])

    for s in range(2):
        pl.when(i % 2 == s)(functools.partial(step, s))


def _dispatch(ntiles, f, pos):
    n = ntiles * TM
    return pl.pallas_call(
        functools.partial(_dispatch_kernel, ntiles=ntiles),
        grid=(ntiles,),
        in_specs=[pl.BlockSpec((1, 8, TM), lambda i: (i, 0, 0), memory_space=pltpu.SMEM),
                  pl.BlockSpec((TM, D_MODEL), lambda i: (i, 0))],
        out_specs=pl.BlockSpec(memory_space=pl.ANY),
        out_shape=jax.ShapeDtypeStruct((MOE_TOPK * n, D_MODEL), F32),
        scratch_shapes=[pltpu.VMEM((2, TM, D_MODEL), F32), pltpu.SemaphoreType.DMA((2,))],
        compiler_params=_cparams(("arbitrary",)),
        name="moe_dispatch",
    )(pos, f)


def _expert_kernel(vb_ref, ve_ref, lo_ref, hi_ref, first_ref, newexp_ref, x_ref, wgu_ref, wd_ref, y_ref,
                   wgu_b, wd_b):
    v = pl.program_id(0)

    @pl.when(newexp_ref[v] == 1)
    def _():
        wgu_b[...] = wgu_ref[0, 0].astype(BF16)
        wd_b[...] = wd_ref[0, 0].astype(BF16)

    @pl.when(hi_ref[v] > lo_ref[v])
    def _():
        gu = _dot(x_ref[...].astype(BF16), wgu_b[...])
        gate = gu[:, :MOE_FF]
        act = gate * _sigmoid(gate) * gu[:, MOE_FF:]
        y = _dot(act.astype(BF16), wd_b[...])
        r = lax.broadcasted_iota(jnp.int32, (MOE_BM, 1), 0)
        mine = jnp.logical_and(r >= lo_ref[v], r < hi_ref[v])
        y = jnp.where(mine, y, 0.0)

        @pl.when(first_ref[v] == 1)
        def _():
            y_ref[...] = y

        @pl.when(first_ref[v] == 0)
        def _():
            y_ref[...] += y


def _expert_ffn(xs, visits, layer, w_gate_up, w_down):
    nvis = visits[0].shape[0]
    blk_idx = lambda v, vb, ve, lo, hi, fi, ne: (vb[v], 0)
    exp_idx = lambda v, vb, ve, lo, hi, fi, ne: (layer, ve[v], 0, 0)
    grid_spec = pltpu.PrefetchScalarGridSpec(
        num_scalar_prefetch=6,
        grid=(nvis,),
        in_specs=[pl.BlockSpec((MOE_BM, D_MODEL), blk_idx),
                  pl.BlockSpec((1, 1, D_MODEL, 2 * MOE_FF), exp_idx),
                  pl.BlockSpec((1, 1, MOE_FF, D_MODEL), exp_idx)],
        out_specs=pl.BlockSpec((MOE_BM, D_MODEL), blk_idx),
        scratch_shapes=[pltpu.VMEM((D_MODEL, 2 * MOE_FF), BF16), pltpu.VMEM((MOE_FF, D_MODEL), BF16)],
    )
    return pl.pallas_call(
        _expert_kernel,
        grid_spec=grid_spec,
        out_shape=jax.ShapeDtypeStruct(xs.shape, F32),
        compiler_params=_cparams(("arbitrary",)),
        name="moe_expert_ffn",
    )(*visits, xs, w_gate_up, w_down)


def _combine_kernel(pos_ref, nxt_ref, x_ref, r_ref, mod_ref, ys_hbm, o_ref, ybuf, sem, *, ntiles):
    i = pl.program_id(0)

    def gather(table, s):
        for r in range(TM):
            for k in range(MOE_TOPK):
                pltpu.make_async_copy(ys_hbm.at[pl.ds(table[0, k, r], 1), :], ybuf.at[s, k, pl.ds(r, 1), :],
                                      sem.at[s]).start(priority=k % 2)

    @pl.when(i == 0)
    def _():
        gather(pos_ref, 0)

    def step(s):
        @pl.when(i + 1 < ntiles)
        def _():
            gather(nxt_ref, 1 - s)
        for k in range(MOE_TOPK):
            _row_wait(ys_hbm, ybuf.at[s, k], sem.at[s])
        w = r_ref[...]
        y = w[:, MOE_TOPK:MOE_TOPK + 1] * ybuf[s, 0]
        for k in range(1, MOE_TOPK):
            y = y + w[:, MOE_TOPK + k:MOE_TOPK + k + 1] * ybuf[s, k]
        o_ref[...] = x_ref[...] + mod_ref[0][5:6, :] * y

    for s in range(2):
        pl.when(i % 2 == s)(functools.partial(step, s))


def _combine(rows, x, ys, pos, route, mod):
    n = rows.n
    spec = pl.BlockSpec((TM, D_MODEL), lambda i: (i, 0))
    return pl.pallas_call(
        functools.partial(_combine_kernel, ntiles=n),
        grid=(n,),
        in_specs=[pl.BlockSpec((1, 8, TM), lambda i: (i, 0, 0), memory_space=pltpu.SMEM),
                  pl.BlockSpec((1, 8, TM), lambda i: (jnp.minimum(i + 1, n - 1), 0, 0), memory_space=pltpu.SMEM),
                  spec,
                  pl.BlockSpec((TM, 128), lambda i: (i, 0)),
                  pl.BlockSpec((1, 8, D_MODEL), lambda i: (rows.mod(i), 0, 0)),
                  pl.BlockSpec(memory_space=pl.ANY)],
        out_specs=spec,
        out_shape=jax.ShapeDtypeStruct((n * TM, D_MODEL), F32),
        scratch_shapes=[pltpu.VMEM((2, MOE_TOPK, TM, D_MODEL), F32), pltpu.SemaphoreType.DMA((2,))],
        compiler_params=_cparams(("arbitrary",)),
        name="moe_combine",
    )(pos, pos, x, route, mod, ys)


def _visit_tables(bounds, nk):
    E = MOE_EXPERTS
    nblk = nk // MOE_BM
    nvis = nblk + E
    starts, ends = bounds[:-1], bounds[1:]
    fb = starts // MOE_BM
    nv = jnp.where(ends > starts, (ends - 1) // MOE_BM - fb + 1, 0)
    cum = jnp.cumsum(nv)
    total = cum[-1]
    v = jnp.arange(nvis, dtype=jnp.int32)
    active = v < total
    vc = jnp.minimum(v, total - 1)
    ve = jnp.minimum(jnp.sum((cum[None, :] <= vc[:, None]).astype(jnp.int32), axis=1), E - 1)
    vb = fb[ve] + (vc - (cum - nv)[ve])
    lo = jnp.where(active, jnp.maximum(starts[ve], vb * MOE_BM) - vb * MOE_BM, 0)
    hi = jnp.where(active, jnp.minimum(ends[ve], (vb + 1) * MOE_BM) - vb * MOE_BM, 0)
    prev_b = jnp.concatenate([jnp.full((1,), -1, jnp.int32), vb[:-1]])
    first = jnp.logical_and(active, vb != prev_b)
    prev_e = jnp.concatenate([jnp.full((1,), -1, jnp.int32), ve[:-1]])
    new_expert = jnp.logical_and(active, ve != prev_e)
    i32 = lambda a: a.astype(jnp.int32)
    return i32(vb), i32(ve), i32(lo), i32(hi), i32(first), i32(new_expert)


def _moe(rows, x, f, route, counts, mod, layer, w_gate_up, w_down):
    n = rows.n * TM
    cum = jnp.cumsum(counts[0])
    starts = (cum - counts[0]).reshape(1, 128)
    bounds = jnp.concatenate([starts[0, MOE_GROUPS:MOE_GROUPS + MOE_EXPERTS], cum[-1:]]).astype(jnp.int32)
    pos = _assignment_slots(rows.n, route, starts)
    xs = _dispatch(rows.n, f, pos)
    ys = _expert_ffn(xs, _visit_tables(bounds, n * MOE_TOPK), layer, w_gate_up, w_down)
    return _combine(rows, x, ys, pos, route, mod)


def kernel(x, c, ctx, c_ctx, ada_w, ada_b, norm_mix, norm_ffn, rg_w_in, rg_conv_w, rg_conv_b, rg_gate_w, rg_gate_b, rg_lambda, rg_w_out, mla_w_down, mla_q_norm, mla_kv_norm, mla_w_uq, mla_w_ukv, mla_qk_norm, mla_w_o, ml_w_in, ml_gate_b, ml_out_norm, ml_w_out, moe_w_group, moe_b_group, moe_w_expert, moe_b_expert, moe_w_gate_up, moe_w_down):
    B, L, D = x.shape
    C = ctx.shape[1]
    depth = ada_w.shape[0]
    assert D == D_MODEL
    st = _Stream(B, L, C)

    xs = jnp.concatenate([ctx, x], axis=1).reshape(st.NT, D)
    cc = jnp.zeros((16, D), F32).at[:B].set(c).at[B].set(c_ctx)
    mod_all = _modulation(cc, ada_w, ada_b)
    mod_all = jnp.pad(mod_all[:, :B + 1].reshape(depth, B + 1, 6, D), ((0, 0), (0, 0), (0, 2), (0, 0)))
    perm = _row_permutation(B)

    row = lambda a: a.reshape(1, -1)

    for i in range(depth):
        last = i == depth - 1
        mod = mod_all[i]
        kind, j = i % 3, i // 3
        all_rows = st.all_rows()
        out_rows = st.latent_rows() if last else all_rows
        tile_spec = lambda w: pl.BlockSpec((TM, w), lambda k: (out_rows.src(k), 0))
        wr = jnp.zeros((D, 128), F32).at[:, :MOE_GROUPS].set(moe_w_group[i]) \
            .at[:, MOE_GROUPS:MOE_GROUPS + MOE_EXPERTS].set(moe_w_expert[i])
        br = jnp.zeros((1, 128), F32).at[0, :MOE_GROUPS].set(moe_b_group[i]) \
            .at[0, MOE_GROUPS:MOE_GROUPS + MOE_EXPERTS].set(moe_b_expert[i])
        out_args = (xs, mod, row(norm_ffn[i]), wr, br)

        if kind == 0:
            gate, u = _rg_in(st, xs, mod, row(norm_mix[i]), perm, rg_w_in[j].astype(BF16))
            wg, gb = _rg_gate_weights(rg_gate_w[j], rg_gate_b[j])
            hs = _rg_scan(st, u, rg_conv_w[j], row(rg_conv_b[j]), wg, gb, rg_lambda[j].reshape(2, 1, D))
            xs, f, route, counts = _rg_out(st, last, gate, hs, perm.T, rg_w_out[j].astype(BF16), *out_args)
        elif kind == 1:
            w_down = jnp.pad(mla_w_down[j], ((0, 0), (0, 512 - mla_w_down.shape[2]))).astype(BF16)
            down = _norm_proj(all_rows, xs, mod, row(norm_mix[i]), w_down, name="mla_down_proj")
            wq, wk, we, wv = _mla_weights(mla_w_uq[j], mla_w_ukv[j])
            q, k, v = _mla_up(st, all_rows, down, row(mla_q_norm[j]), row(mla_kv_norm[j]), wq, wk, we, wv,
                              _rope_tables(L, mla_qk_norm[j]))
            o = _attention(st, q, k, v)
            xs, f, route, counts = _mixer_out(out_rows, _lhs_mla, (o,), (tile_spec(D),),
                                      mla_w_o[j].astype(BF16), *out_args, name="mla_out")
        else:
            n_in = ml_w_in.shape[2]
            w_in = jnp.pad(ml_w_in[j], ((0, 0), (0, ML_NP - n_in))).astype(BF16)
            proj = _norm_proj(all_rows, xs, mod, row(norm_mix[i]), w_in, name="mlstm_in_proj")
            gate_bias = jnp.pad(ml_gate_b[j].reshape(1, -1), ((0, 0), (0, 128 - 4 * ML_HEADS)))
            hf, hb = _mlstm(st, proj, gate_bias)
            og_spec = pl.BlockSpec((TM, D), lambda k: (out_rows.src(k), 2))
            xs, f, route, counts = _mixer_out(out_rows, _lhs_mlstm, (hf, hb, proj, row(ml_out_norm[j])),
                                      (tile_spec(D), tile_spec(D), og_spec, pl.BlockSpec((1, D), lambda k: (0, 0))),
                                      ml_w_out[j].astype(BF16), *out_args, name="mlstm_out")

        moe_rows = st.dense_latent_rows() if last else all_rows
        xs = _moe(moe_rows, xs, f, route, counts, mod, i, moe_w_gate_up, moe_w_down)

    return xs.reshape(B, L, D)
```

```python
import functools
import math

import jax
import jax.numpy as jnp
from jax import lax
from jax.experimental import pallas as pl
from jax.experimental.pallas import tpu as pltpu

F32 = jnp.float32
BF16 = jnp.bfloat16

D_MODEL = 1024
RMS_EPS = 1e-6

TM = 256
VMEM_LIMIT = 48 * 1024 * 1024

RG_BLOCK_W = 64
RG_CHUNK = 256
RG_CONV_W = 4
RG_C = 8.0
RG_TT = 64

MLA_HEADS = 16
MLA_Q_RANK = 256
MLA_KV_RANK = 128
MLA_NOPE = 64
MLA_ROPE = 32
MLA_V = 64
MLA_QK = MLA_NOPE + MLA_ROPE
MLA_HP = 128
ROPE_AXIS_DIM = MLA_ROPE // 2
ROPE_BASE = 10000.0
GRID_W = 64
ATT_HEADS = 4

ML_HEADS = 4
ML_DV = 256
ML_DQK = 128
ML_TC = 256
ML_M_INIT = -1e30
ML_NP = 3200

MOE_GROUPS = 8
MOE_PER_GROUP = 8
MOE_EXPERTS = 64
MOE_TOPK = 2
MOE_FF = 256
MOE_BM = 256


def _cparams(sem):
    return pltpu.CompilerParams(dimension_semantics=sem, vmem_limit_bytes=VMEM_LIMIT)


def _dot(a, b):
    return jnp.dot(a, b, preferred_element_type=F32)


def _dot_t(a, b):
    return lax.dot_general(a, b, (((1,), (1,)), ((), ())), preferred_element_type=F32)


def _dot3(a, b):
    ah = a.astype(BF16)
    al = (a - ah.astype(F32)).astype(BF16)
    bh = b.astype(BF16)
    bl = (b - bh.astype(F32)).astype(BF16)
    return _dot(ah, bh) + (_dot(al, bh) + _dot(ah, bl))


def _sigmoid(x):
    return 0.5 * jnp.tanh(0.5 * x) + 0.5


def _softplus(x):
    return jnp.maximum(x, 0.0) + jnp.log1p(jnp.exp(-jnp.abs(x)))


def _gelu_tanh(x):
    return 0.5 * x * (1.0 + jnp.tanh(0.7978845608028654 * (x + 0.044715 * (x * x * x))))


def _rms(x, n=None):
    n = x.shape[-1] if n is None else n
    ms = jnp.sum(x * x, axis=-1, keepdims=True) * (1.0 / n)
    return x * lax.rsqrt(ms + RMS_EPS)


class _Rows:
    def __init__(self, n, src, mod):
        self.n, self.src, self.mod = n, src, mod


class _Stream:
    def __init__(self, B, L, C):
        assert L % TM == 0 and C % TM == 0 and B % 8 == 0
        self.B, self.L, self.C = B, L, C
        self.ltot = L + C
        self.lt, self.ct = L // TM, C // TM
        self.tpb = self.lt + self.ct
        self.NT = B * self.ltot

    def all_rows(self):
        tpb, ct, B = self.tpb, self.ct, self.B
        return _Rows(B * tpb, lambda k: k, lambda k: jnp.where(k % tpb < ct, B, k // tpb))

    def latent_rows(self):
        tpb, ct, lt = self.tpb, self.ct, self.lt
        return _Rows(self.B * lt, lambda k: (k // lt) * tpb + ct + k % lt, lambda k: k // lt)

    def dense_latent_rows(self):
        lt = self.lt
        return _Rows(self.B * lt, lambda k: k, lambda k: k // lt)


def _mod_kernel(c_ref, w_ref, b_ref, o_ref):
    c = c_ref[...]
    o_ref[0] = _dot3(c * _sigmoid(c), w_ref[0]) + b_ref[0]


def _modulation(cc, ada_w, ada_b):
    depth, d, n = ada_w.shape
    tn = 1536
    return pl.pallas_call(
        _mod_kernel,
        grid=(depth, n // tn),
        in_specs=[pl.BlockSpec((16, d), lambda l, j: (0, 0)),
                  pl.BlockSpec((1, d, tn), lambda l, j: (l, 0, j)),
                  pl.BlockSpec((1, 1, tn), lambda l, j: (l, 0, j))],
        out_specs=pl.BlockSpec((1, 16, tn), lambda l, j: (l, 0, j)),
        out_shape=jax.ShapeDtypeStruct((depth, 16, n), F32),
        compiler_params=_cparams(("arbitrary", "arbitrary")),
        name="ada_modulation",
    )(cc, ada_w, ada_b.reshape(depth, 1, n))


def _norm_mod(x, gain, shift, scale):
    return _rms(x) * gain * (1.0 + scale) + shift


def _norm_proj_kernel(x_ref, mod_ref, g_ref, w_ref, o_ref):
    mod = mod_ref[0]
    h = _norm_mod(x_ref[...], g_ref[...], mod[0:1, :], mod[1:2, :]).astype(BF16)
    o_ref[...] = _dot(h, w_ref[...]).astype(o_ref.dtype)


def _norm_proj(rows, x, mod, gain, w, name):
    n = w.shape[1]
    const = lambda k: (0, 0)
    return pl.pallas_call(
        _norm_proj_kernel,
        grid=(rows.n,),
        in_specs=[pl.BlockSpec((TM, D_MODEL), lambda k: (rows.src(k), 0)),
                  pl.BlockSpec((1, 8, D_MODEL), lambda k: (rows.mod(k), 0, 0)),
                  pl.BlockSpec((1, D_MODEL), const),
                  pl.BlockSpec((D_MODEL, n), const)],
        out_specs=pl.BlockSpec((TM, n), lambda k: (k, 0)),
        out_shape=jax.ShapeDtypeStruct((rows.n * TM, n), F32),
        compiler_params=_cparams(("arbitrary",)),
        name=name,
    )(x, mod, gain, w)


RG_TR = 32


def _row_permutation(B):
    n = B * RG_TR
    r = jnp.arange(n)
    src = (r % B) * RG_TR + r // B
    return (src[:, None] == jnp.arange(n)[None, :]).astype(BF16)


def _time_tile_mod(mod_ref, is_ctx, B, row):
    return jnp.where(is_ctx, mod_ref[B:B + 1, row:row + 1, :], mod_ref[0:B, row:row + 1, :])


def _rg_in_kernel(x_ref, mod_ref, g_ref, perm_ref, wg_ref, wu_ref, gate_ref, u_ref, *, B, ctx_tiles):
    is_ctx = pl.program_id(0) < ctx_tiles
    h = _norm_mod(x_ref[...], g_ref[...], _time_tile_mod(mod_ref, is_ctx, B, 0), _time_tile_mod(mod_ref, is_ctx, B, 1))
    h = h.reshape(B * RG_TR, D_MODEL).astype(BF16)
    h = _dot(perm_ref[...], h).astype(BF16)
    gate_ref[...] = _gelu_tanh(_dot(h, wg_ref[...])).astype(gate_ref.dtype)
    u_ref[...] = _dot(h, wu_ref[...])


def _rg_in(st, x, mod, gain, perm, w_in):
    B, W = st.B, D_MODEL
    R = B * RG_TR
    nt = st.ltot // RG_TR
    const = lambda t: (0, 0)
    return pl.pallas_call(
        functools.partial(_rg_in_kernel, B=B, ctx_tiles=st.C // RG_TR),
        grid=(nt,),
        in_specs=[pl.BlockSpec((B, RG_TR, D_MODEL), lambda t: (0, t, 0)),
                  pl.BlockSpec((B + 1, 8, D_MODEL), lambda t: (0, 0, 0)),
                  pl.BlockSpec((1, D_MODEL), const),
                  pl.BlockSpec((R, R), const),
                  pl.BlockSpec((D_MODEL, W), const),
                  pl.BlockSpec((D_MODEL, W), lambda t: (0, 1))],
        out_specs=[pl.BlockSpec((R, W), lambda t: (t, 0)),
                   pl.BlockSpec((R, W), lambda t: (t, 0))],
        out_shape=[jax.ShapeDtypeStruct((st.ltot * B, W), BF16), jax.ShapeDtypeStruct((st.ltot * B, W), F32)],
        compiler_params=_cparams(("arbitrary",)),
        name="rg_in_proj",
    )(x.reshape(B, st.ltot, D_MODEL), mod, gain, perm, w_in, w_in)


def _rg_tile_order(d, k, ct, ntt):
    bwd = jnp.where(k < ct, ct - 1 - k, ntt - 1 - (k - ct))
    return jnp.where(d == 0, k, bwd)


def _rg_scan_kernel(um_ref, up_ref, un_ref, cw_ref, cb_ref, wg_ref, gb_ref, lam_ref, o_ref,
                    ext_ref, a_ref, b_ref, h_ref, *, B, ct, ntt):
    d = pl.program_id(0)
    k = pl.program_id(1)
    tile = _rg_tile_order(d, k, ct, ntt)
    R = RG_TT * B

    @pl.when(k == 0)
    def _():
        h_ref[...] = jnp.zeros_like(h_ref)

    seq_start = jnp.logical_or(tile == 0, tile == ct)
    seq_end = jnp.logical_or(tile == ct - 1, tile == ntt - 1)
    ext_ref[0:2 * B, :] = jnp.where(seq_start, 0.0, up_ref[...])
    ext_ref[2 * B:2 * B + R, :] = um_ref[...]
    ext_ref[2 * B + R:3 * B + R, :] = jnp.where(seq_end, 0.0, un_ref[...])
    cw = cw_ref[...]
    uc = cb_ref[...] + cw[0:1, :] * ext_ref[0:R, :]
    for j in range(1, RG_CONV_W):
        uc = uc + cw[j:j + 1, :] * ext_ref[j * B:j * B + R, :]
    ucb = uc.astype(BF16)
    c_lam = -RG_C * _softplus(-lam_ref[0])
    for c in range(D_MODEL // RG_CHUNK):
        sl = slice(c * RG_CHUNK, (c + 1) * RG_CHUNK)
        z = _dot(ucb[:, sl], wg_ref[0, c]) + gb_ref[0, c]
        r = _sigmoid(z[:, :RG_CHUNK])
        ig = _sigmoid(z[:, RG_CHUNK:])
        log_a = c_lam[:, sl] * r
        a = jnp.exp(log_a)
        one_minus_a2 = -jnp.tanh(log_a) * (a * a + 1.0)
        a_ref[:, sl] = a
        b_ref[:, sl] = jnp.sqrt(one_minus_a2) * (ig * uc[:, sl])

    def scan(times):
        for c in range(D_MODEL // 128):
            cs = slice(c * 128, (c + 1) * 128)
            h = h_ref[:, cs]
            for t in times:
                rs = slice(t * B, (t + 1) * B)
                h = a_ref[rs, cs] * h + b_ref[rs, cs]
                b_ref[rs, cs] = h
            h_ref[:, cs] = h

    pl.when(d == 0)(lambda: scan(range(RG_TT)))
    pl.when(d == 1)(lambda: scan(range(RG_TT - 1, -1, -1)))
    o_ref[0] = b_ref[...].astype(o_ref.dtype)


def _rg_scan(st, u_tm, conv_w, conv_b, wg, gb, lam):
    B, W = st.B, D_MODEL
    assert st.C % RG_TT == 0 and st.L % RG_TT == 0
    ltot = st.ltot
    ntt, ct = ltot // RG_TT, st.C // RG_TT
    R = RG_TT * B
    order = functools.partial(_rg_tile_order, ct=ct, ntt=ntt)
    nch = W // RG_CHUNK
    return pl.pallas_call(
        functools.partial(_rg_scan_kernel, B=B, ct=ct, ntt=ntt),
        grid=(2, ntt),
        in_specs=[pl.BlockSpec((R, W), lambda d, k: (order(d, k), 0)),
                  pl.BlockSpec((2 * B, W), lambda d, k: (jnp.maximum(order(d, k) * (RG_TT // 2) - 1, 0), 0)),
                  pl.BlockSpec((B, W), lambda d, k: (jnp.minimum((order(d, k) + 1) * RG_TT, ltot - 1), 0)),
                  pl.BlockSpec((RG_CONV_W, W), lambda d, k: (0, 0)),
                  pl.BlockSpec((1, W), lambda d, k: (0, 0)),
                  pl.BlockSpec((1, nch, RG_CHUNK, 2 * RG_CHUNK), lambda d, k: (d, 0, 0, 0)),
                  pl.BlockSpec((1, nch, 1, 2 * RG_CHUNK), lambda d, k: (d, 0, 0, 0)),
                  pl.BlockSpec((1, 1, W), lambda d, k: (d, 0, 0))],
        out_specs=pl.BlockSpec((1, R, W), lambda d, k: (d, order(d, k), 0)),
        out_shape=jax.ShapeDtypeStruct((2, ltot * B, W), BF16),
        scratch_shapes=[pltpu.VMEM((3 * B + R, W), F32),
                        pltpu.VMEM((R, W), F32),
                        pltpu.VMEM((R, W), F32),
                        pltpu.VMEM((B, W), F32)],
        compiler_params=_cparams(("arbitrary", "arbitrary")),
        name="rg_scan",
    )(u_tm, u_tm, u_tm, conv_w, conv_b, wg, gb, lam)


def _rg_gate_weights(gate_w, gate_b):
    nb = gate_w.shape[2]
    per = RG_CHUNK // RG_BLOCK_W
    nch = nb // per
    gw = gate_w.reshape(2, 2, nch, per, RG_BLOCK_W, RG_BLOCK_W)
    eye = jnp.eye(per, dtype=gate_w.dtype)
    bd = jnp.einsum('dgcnij,nm->dgcnimj', gw, eye).reshape(2, 2, nch, RG_CHUNK, RG_CHUNK)
    wg = jnp.concatenate([bd[:, 0], bd[:, 1]], axis=-1).astype(BF16)
    gb = gate_b.reshape(2, 2, nch, 1, RG_CHUNK)
    gb = jnp.concatenate([gb[:, 0], gb[:, 1]], axis=-1)
    return wg, gb


def _rg_out_kernel(g_ref, hf_ref, hb_ref, perm_ref, w_ref, x_ref, mod_ref, gain_ref, wr_ref, br_ref,
                   xo_ref, f_ref, r_ref, cnt_ref, *, B, ctx_tiles, t0):
    is_ctx = pl.program_id(0) + t0 < ctx_tiles
    hsum = hf_ref[0].astype(F32) + hb_ref[0].astype(F32)
    lhs = (g_ref[...].astype(F32) * hsum).astype(BF16)
    lhs = _dot(perm_ref[...], lhs).astype(BF16)
    y = _dot(lhs, w_ref[...]).reshape(B, RG_TR, D_MODEL)
    m = lambda row: _time_tile_mod(mod_ref, is_ctx, B, row)
    xn = x_ref[...] + m(2) * y
    xo_ref[...] = xn
    f = _norm_mod(xn, gain_ref[...], m(3), m(4))
    f_ref[...] = f
    lg = _dot3(f.reshape(B * RG_TR, D_MODEL), wr_ref[...]) + br_ref[...]
    table, hist = _route(lg)
    r_ref[...] = table.reshape(B, RG_TR, 128)
    _accumulate_counts(cnt_ref, hist)


def _rg_out(st, latent_only, gate, hs, perm, w_out, x, mod, gain, wr, br):
    B = st.B
    R = B * RG_TR
    t0 = st.C // RG_TR if latent_only else 0
    lo = st.L if latent_only else st.ltot
    nt = lo // RG_TR
    const = lambda t: (0, 0)
    blk = lambda w: pl.BlockSpec((B, RG_TR, w), lambda t: (0, t, 0))
    xo, f, route, counts = pl.pallas_call(
        functools.partial(_rg_out_kernel, B=B, ctx_tiles=st.C // RG_TR, t0=t0),
        grid=(nt,),
        in_specs=[pl.BlockSpec((R, D_MODEL), lambda t: (t + t0, 0)),
                  pl.BlockSpec((1, R, D_MODEL), lambda t: (0, t + t0, 0)),
                  pl.BlockSpec((1, R, D_MODEL), lambda t: (1, t + t0, 0)),
                  pl.BlockSpec((R, R), const),
                  pl.BlockSpec((D_MODEL, D_MODEL), const),
                  pl.BlockSpec((B, RG_TR, D_MODEL), lambda t: (0, t + t0, 0)),
                  pl.BlockSpec((B + 1, 8, D_MODEL), lambda t: (0, 0, 0)),
                  pl.BlockSpec((1, D_MODEL), const),
                  pl.BlockSpec((D_MODEL, 128), const),
                  pl.BlockSpec((1, 128), const)],
        out_specs=[blk(D_MODEL), blk(D_MODEL), blk(128), pl.BlockSpec((1, 128), const)],
        out_shape=[jax.ShapeDtypeStruct((B, lo, D_MODEL), F32),
                   jax.ShapeDtypeStruct((B, lo, D_MODEL), F32),
                   jax.ShapeDtypeStruct((B, lo, 128), F32),
                   jax.ShapeDtypeStruct((1, 128), F32)],
        compiler_params=_cparams(("arbitrary",)),
        name="rg_out",
    )(gate, hs, hs, perm, w_out, x.reshape(B, st.ltot, D_MODEL), mod, gain, wr, br)
    return xo.reshape(B * lo, D_MODEL), f.reshape(B * lo, D_MODEL), route.reshape(B * lo, 128), counts


_ROPE_HALF = ROPE_AXIS_DIM // 2
_MLA_SRC_DIM = (list(range(MLA_NOPE + ROPE_AXIS_DIM)) + list(range(MLA_NOPE, MLA_NOPE + _ROPE_HALF))
                + list(range(MLA_NOPE + ROPE_AXIS_DIM, MLA_QK))
                + list(range(MLA_NOPE + ROPE_AXIS_DIM, MLA_NOPE + ROPE_AXIS_DIM + _ROPE_HALF)))
_MLA_REAL_LANE = ([1.0] * (MLA_NOPE + ROPE_AXIS_DIM) + [0.0] * _ROPE_HALF + [1.0] * ROPE_AXIS_DIM
                  + [0.0] * _ROPE_HALF + [0.0] * (MLA_HP - len(_MLA_SRC_DIM)))


def _mla_up_kernel(dn_ref, qn_ref, kvn_ref, wq_ref, wk_ref, we_ref, wv_ref, real_ref, cq_ref, sq_ref, ck_ref, sk_ref,
                   q_ref, k_ref, v_ref):
    dn = dn_ref[...]
    cq = _rms(dn[:, :MLA_Q_RANK]) * qn_ref[...]
    ckv = _rms(dn[:, MLA_Q_RANK:MLA_Q_RANK + MLA_KV_RANK]) * kvn_ref[...]
    kr = dn[:, MLA_Q_RANK + MLA_KV_RANK:]
    kr_hi = kr.astype(BF16)
    kr_lo = (kr - kr_hi.astype(F32)).astype(BF16)
    ckvb = ckv.astype(BF16)
    q_pre = _dot(cq.astype(BF16), wq_ref[...])
    k_pre = _dot(ckvb, wk_ref[...]) + (_dot(kr_hi, we_ref[...]) + _dot(kr_lo, we_ref[...]))
    v_ref[...] = _dot(ckvb, wv_ref[...]).astype(v_ref.dtype)
    real = real_ref[...]

    def head(x, cos_g, sin_g):
        ms = jnp.sum(x * x * real, axis=-1, keepdims=True) * (1.0 / MLA_QK)
        xr = x * lax.rsqrt(ms + RMS_EPS)
        return xr * cos_g + pltpu.roll(xr, MLA_HP - _ROPE_HALF, 1) * sin_g

    cos_q, sin_q, cos_k, sin_k = cq_ref[...], sq_ref[...], ck_ref[...], sk_ref[...]
    for h in range(MLA_HEADS):
        sl = slice(h * MLA_HP, (h + 1) * MLA_HP)
        q_ref[:, sl] = head(q_pre[:, sl], cos_q, sin_q).astype(q_ref.dtype)
        k_ref[:, sl] = head(k_pre[:, sl], cos_k, sin_k).astype(k_ref.dtype)


def _mla_up(st, rows, down, q_norm, kv_norm, wq, wk, we, wv, tables):
    hw = MLA_HEADS * MLA_HP
    const = lambda i: (0, 0)
    tpb, ct, lt = st.tpb, st.ct, st.lt
    rope_idx = lambda i: (jnp.where(i % tpb < ct, lt, i % tpb - ct), 0)
    real = jnp.asarray(_MLA_REAL_LANE, F32).reshape(1, MLA_HP)
    return pl.pallas_call(
        _mla_up_kernel,
        grid=(rows.n,),
        in_specs=[pl.BlockSpec((TM, 512), lambda i: (i, 0)),
                  pl.BlockSpec((1, MLA_Q_RANK), const),
                  pl.BlockSpec((1, MLA_KV_RANK), const),
                  pl.BlockSpec((MLA_Q_RANK, hw), const),
                  pl.BlockSpec((MLA_KV_RANK, hw), const),
                  pl.BlockSpec((128, hw), const),
                  pl.BlockSpec((MLA_KV_RANK, MLA_HEADS * MLA_V), const),
                  pl.BlockSpec((1, MLA_HP), const)] + [pl.BlockSpec((TM, MLA_HP), rope_idx)] * 4,
        out_specs=[pl.BlockSpec((TM, hw), lambda i: (i, 0)),
                   pl.BlockSpec((TM, hw), lambda i: (i, 0)),
                   pl.BlockSpec((TM, MLA_HEADS * MLA_V), lambda i: (i, 0))],
        out_shape=[jax.ShapeDtypeStruct((st.NT, hw), BF16),
                   jax.ShapeDtypeStruct((st.NT, hw), BF16),
                   jax.ShapeDtypeStruct((st.NT, MLA_HEADS * MLA_V), BF16)],
        compiler_params=_cparams(("arbitrary",)),
        name="mla_up_proj",
    )(down, q_norm, kv_norm, wq, wk, we, wv, real, *tables)


def _attn_kernel(q_ref, k_ref, v_ref, o_ref, vaug_ref, *, C, ct):
    qi = pl.program_id(2)

    @pl.when(qi == 0)
    def _():
        lane = lax.broadcasted_iota(jnp.int32, (k_ref.shape[0], 2 * MLA_V), 1)
        for hh in range(ATT_HEADS):
            pair = v_ref[:, (hh // 2) * 2 * MLA_V:(hh // 2 + 1) * 2 * MLA_V].astype(F32)
            if hh % 2 == 0:
                aug = jnp.where(lane < MLA_V, pair, jnp.where(lane == MLA_V, 1.0, 0.0))
            else:
                aug = jnp.where(lane >= MLA_V, pair, jnp.where(lane == 0, 1.0, 0.0))
            vaug_ref[hh] = aug.astype(BF16)

    def attend(nkeys):
        lane = lax.broadcasted_iota(jnp.int32, (TM, 2 * MLA_V), 1)
        for pair in range(ATT_HEADS // 2):
            outs = []
            for hh in (2 * pair, 2 * pair + 1):
                sl = slice(hh * MLA_HP, (hh + 1) * MLA_HP)
                s = _dot_t(q_ref[:, sl], k_ref[0:nkeys, sl])
                p = jnp.exp2(s - jnp.max(s, axis=-1, keepdims=True)).astype(BF16)
                o = _dot(p, vaug_ref[hh, 0:nkeys, :])
                rowsum = o[:, MLA_V:MLA_V + 1] if hh % 2 == 0 else o[:, 0:1]
                outs.append(o * (1.0 / rowsum))
            o_ref[:, pair * 2 * MLA_V:(pair + 1) * 2 * MLA_V] = jnp.where(lane < MLA_V, outs[0], outs[1]).astype(o_ref.dtype)

    pl.when(qi < ct)(lambda: attend(C))
    pl.when(qi >= ct)(lambda: attend(k_ref.shape[0]))


def _attention(st, q, k, v):
    B, ltot, tpb = st.B, st.ltot, st.tpb
    hg = MLA_HEADS // ATT_HEADS
    return pl.pallas_call(
        functools.partial(_attn_kernel, C=st.C, ct=st.ct),
        grid=(B, hg, tpb),
        in_specs=[pl.BlockSpec((TM, ATT_HEADS * MLA_HP), lambda b, h, i: (b * tpb + i, h)),
                  pl.BlockSpec((ltot, ATT_HEADS * MLA_HP), lambda b, h, i: (b, h)),
                  pl.BlockSpec((ltot, ATT_HEADS * MLA_V), lambda b, h, i: (b, h))],
        out_specs=pl.BlockSpec((TM, ATT_HEADS * MLA_V), lambda b, h, i: (b * tpb + i, h)),
        out_shape=jax.ShapeDtypeStruct((st.NT, MLA_HEADS * MLA_V), BF16),
        scratch_shapes=[pltpu.VMEM((ATT_HEADS, ltot, 2 * MLA_V), BF16)],
        compiler_params=_cparams(("arbitrary", "arbitrary", "arbitrary")),
        name="mla_attention",
    )(q, k, v)


def _mla_weights(w_uq, w_ukv):
    H = MLA_HEADS
    src = jnp.asarray(_MLA_SRC_DIM, jnp.int32)
    pad = MLA_HP - len(_MLA_SRC_DIM)
    wq = jnp.pad(w_uq.reshape(MLA_Q_RANK, H, MLA_QK)[:, :, src], ((0, 0), (0, 0), (0, pad)))
    wkv = w_ukv.reshape(MLA_KV_RANK, H, MLA_NOPE + MLA_V)
    wk = jnp.pad(wkv[:, :, :MLA_NOPE], ((0, 0), (0, 0), (0, MLA_HP - MLA_NOPE)))
    wv = wkv[:, :, MLA_NOPE:]
    r = jnp.arange(128)[:, None]
    lane_dim = jnp.pad(src, (0, pad), constant_values=-1)[None, :]
    place = (lane_dim == r + MLA_NOPE).astype(BF16)
    we = jnp.tile(place, (1, H))
    return (wq.reshape(MLA_Q_RANK, H * MLA_HP).astype(BF16), wk.reshape(MLA_KV_RANK, H * MLA_HP).astype(BF16),
            we, wv.reshape(MLA_KV_RANK, H * MLA_V).astype(BF16))


def _rope_tables(L, qk_norm):
    rows = L // GRID_W
    row = jnp.broadcast_to(jnp.arange(rows, dtype=F32)[:, None], (rows, GRID_W)).reshape(L)
    col = jnp.broadcast_to(jnp.arange(GRID_W, dtype=F32)[None, :], (rows, GRID_W)).reshape(L)
    inv_freq = ROPE_BASE ** (-jnp.arange(0, ROPE_AXIS_DIM, 2, dtype=F32) / ROPE_AXIS_DIM)
    ar = row[:, None] * inv_freq
    ac = col[:, None] * inv_freq
    h8 = _ROPE_HALF
    one = jnp.ones((L, MLA_NOPE), F32)
    z8 = jnp.zeros((L, h8), F32)
    zpad = jnp.zeros((L, MLA_HP - len(_MLA_SRC_DIM)), F32)
    cos_t = jnp.concatenate([one, jnp.cos(ar), jnp.cos(ar), z8, jnp.cos(ac), jnp.cos(ac), z8, zpad], axis=1)
    sin_t = jnp.concatenate([0 * one, -jnp.sin(ar), jnp.sin(ar), z8, -jnp.sin(ac), jnp.sin(ac), z8, zpad], axis=1)
    ident = jnp.asarray(_MLA_REAL_LANE, F32)[None, :]
    cos_t = jnp.concatenate([cos_t, jnp.broadcast_to(ident, (TM, MLA_HP))], axis=0)
    sin_t = jnp.concatenate([sin_t, jnp.zeros((TM, MLA_HP), F32)], axis=0)
    src = jnp.asarray(_MLA_SRC_DIM, jnp.int32)
    pad = MLA_HP - len(_MLA_SRC_DIM)
    scale = MLA_QK ** -0.5 * math.log2(math.e)
    tables = []
    for g, s in ((qk_norm[0], scale), (qk_norm[1], 1.0)):
        g_lane = jnp.pad(g[src], (0, pad))
        g_partner = jnp.roll(g_lane, -h8)
        tables += [cos_t * (g_lane * s)[None, :], sin_t * (g_partner * s)[None, :]]
    return tables


def _log_sigmoid(x):
    return jnp.minimum(x, 0.0) - jnp.log1p(jnp.exp(-jnp.abs(x)))


def _mlstm_kernel(qf_ref, kf_ref, vf_ref, gf_ref, qb_ref, kb_ref, vb_ref, gb_ref, bias_ref, tril_ref, triu_ref,
                  of_ref, ob_ref, c_ref, n_ref, m_ref):
    T = ML_TC

    @pl.when(pl.program_id(1) == 0)
    def _():
        c_ref[...] = jnp.zeros_like(c_ref)
        n_ref[...] = jnp.zeros_like(n_ref)
        m_ref[...] = jnp.full(m_ref.shape, ML_M_INIT, F32)

    ti = lax.broadcasted_iota(jnp.int32, (T, T), 0)
    si = lax.broadcasted_iota(jnp.int32, (T, T), 1)
    dirs = ((qf_ref, kf_ref, vf_ref, gf_ref, of_ref, tril_ref), (qb_ref, kb_ref, vb_ref, gb_ref, ob_ref, triu_ref))
    for d, (q_ref, k_ref, v_ref, g_ref, o_ref, tri_ref) in enumerate(dirs):
        tri = (si <= ti) if d == 0 else (si >= ti)
        g = g_ref[...] + bias_ref[...]
        g_t = g.T
        lsg = _log_sigmoid(g)
        lsg_hi = lsg.astype(BF16)
        lsg_lo = (lsg - lsg_hi.astype(F32)).astype(BF16)
        cum = _dot(tri_ref[...], lsg_hi) + _dot(tri_ref[...], lsg_lo)
        cum_t = cum.T
        last = T - 1 if d == 0 else 0
        for h in range(ML_HEADS):
            st = d * ML_HEADS + h
            li, lf_ = (2 * d) * ML_HEADS + h, (2 * d + 1) * ML_HEADS + h
            ig_col = g[:, li:li + 1]
            ig_row = g_t[li:li + 1, :]
            b_col = cum[:, lf_:lf_ + 1]
            b_row = cum_t[lf_:lf_ + 1, :]
            total = cum[last:last + 1, lf_:lf_ + 1]
            m_old = m_ref[st, 0:1, 0:1]
            d_log = jnp.where(tri, b_col - b_row + ig_row, -jnp.inf)
            inter_log = b_col + m_old
            m_t = jnp.maximum(inter_log, jnp.max(d_log, axis=1, keepdims=True))
            qh = q_ref[:, h * ML_DQK:(h + 1) * ML_DQK] * (ML_DQK ** -0.5)
            kh = k_ref[:, h * ML_DQK:(h + 1) * ML_DQK]
            vh = v_ref[:, h * ML_DV:(h + 1) * ML_DV].astype(BF16)
            qb16 = qh.astype(BF16)
            s_mat = _dot_t(qb16, kh.astype(BF16)) * jnp.exp(d_log - m_t)
            inter = jnp.exp(inter_log - m_t)
            c_old = c_ref[st]
            n_old = n_ref[st, 0:1, :]
            num = _dot(s_mat.astype(BF16), vh) + inter * _dot(qb16, c_old.astype(BF16))
            den = jnp.sum(s_mat, axis=1, keepdims=True) + inter * jnp.sum(qh * n_old, axis=1, keepdims=True)
            o_ref[:, h * ML_DV:(h + 1) * ML_DV] = num / jnp.maximum(jnp.abs(den), jnp.exp(-m_t))
            w_log = total - b_col + ig_col
            m_new = jnp.maximum(total + m_old, jnp.max(w_log, axis=0, keepdims=True))
            w = jnp.exp(w_log - m_new)
            decay = jnp.exp(total + m_old - m_new)
            kw = kh * w
            c_ref[st] = decay * c_old + _dot(kw.T.astype(BF16), vh)
            n_ref[st, 0:1, :] = decay * n_old + jnp.sum(kw, axis=0, keepdims=True)
            m_ref[st] = jnp.broadcast_to(m_new, m_ref.shape[1:])


def _mlstm(st, proj, gate_bias):
    B = st.B
    assert st.L % ML_TC == 0 and st.C % ML_TC == 0
    cc = st.C // ML_TC
    nch = st.ltot // ML_TC
    qw = ML_HEADS * ML_DQK
    vw = ML_HEADS * ML_DV
    gcol = (2 * qw + 2 * vw) // 128

    def rb(d, b, k):
        chunk = k if d == 0 else jnp.where(k < cc, cc - 1 - k, nch - 1 - (k - cc))
        return b * nch + chunk

    def specs(d):
        return [pl.BlockSpec((ML_TC, qw), lambda b, k: (rb(d, b, k), 0)),
                pl.BlockSpec((ML_TC, qw), lambda b, k: (rb(d, b, k), 1)),
                pl.BlockSpec((ML_TC, vw), lambda b, k: (rb(d, b, k), (2 * qw) // vw)),
                pl.BlockSpec((ML_TC, 128), lambda b, k: (rb(d, b, k), gcol))]

    nst = 2 * ML_HEADS
    out = jax.ShapeDtypeStruct((st.NT, vw), F32)
    tril = jnp.tril(jnp.ones((ML_TC, ML_TC), BF16))
    return pl.pallas_call(
        _mlstm_kernel,
        grid=(B, nch),
        in_specs=specs(0) + specs(1) + [pl.BlockSpec((1, 128), lambda b, k: (0, 0)),
                                        pl.BlockSpec((ML_TC, ML_TC), lambda b, k: (0, 0)),
                                        pl.BlockSpec((ML_TC, ML_TC), lambda b, k: (0, 0))],
        out_specs=[pl.BlockSpec((ML_TC, vw), lambda b, k: (rb(0, b, k), 0)),
                   pl.BlockSpec((ML_TC, vw), lambda b, k: (rb(1, b, k), 0))],
        out_shape=[out, out],
        scratch_shapes=[pltpu.VMEM((nst, ML_DQK, ML_DV), F32),
                        pltpu.VMEM((nst, 8, ML_DQK), F32),
                        pltpu.VMEM((nst, 8, 128), F32)],
        compiler_params=_cparams(("arbitrary", "arbitrary")),
        name="mlstm_chunks",
    )(proj, proj, proj, proj, proj, proj, proj, proj, gate_bias, tril, tril.T)


def _route(lg):
    lane_i = lax.broadcasted_iota(jnp.int32, lg.shape, 1)
    lane = lane_i.astype(F32)
    neg = -jnp.inf
    gl = jnp.where(lane_i < MOE_GROUPS, lg, neg)
    gmax = jnp.max(gl, axis=-1, keepdims=True)
    gsum = jnp.sum(jnp.where(lane_i < MOE_GROUPS, jnp.exp(lg - gmax), 0.0), axis=-1, keepdims=True)
    p_top = 1.0 / gsum
    g_sel = jnp.min(jnp.where(gl == gmax, lane, 128.0), axis=-1, keepdims=True)
    group_of_lane = (lane_i >> 3).astype(F32) - 1.0
    el = jnp.where(group_of_lane == g_sel, lg, neg)
    e1 = jnp.max(el, axis=-1, keepdims=True)
    i1 = jnp.min(jnp.where(el == e1, lane, 128.0), axis=-1, keepdims=True)
    el2 = jnp.where(lane == i1, neg, el)
    e2 = jnp.max(el2, axis=-1, keepdims=True)
    i2 = jnp.min(jnp.where(el2 == e2, lane, 128.0), axis=-1, keepdims=True)
    t = jnp.exp(e2 - e1)
    w1 = p_top / (1.0 + t)
    w2 = w1 * t
    id1 = i1 - MOE_GROUPS
    id2 = i2 - MOE_GROUPS
    table = jnp.where(lane_i == 0, id1, jnp.where(lane_i == 1, id2,
                                                  jnp.where(lane_i == 2, w1, jnp.where(lane_i == 3, w2, 0.0))))
    chosen = jnp.where(lane == i1, 1.0, 0.0) + jnp.where(lane == i2, 1.0, 0.0)
    return table, jnp.sum(chosen, axis=0, keepdims=True)


def _accumulate_counts(cnt_ref, hist):
    @pl.when(pl.program_id(0) == 0)
    def _():
        cnt_ref[...] = jnp.zeros_like(cnt_ref)
    cnt_ref[...] += hist


def _lhs_mla(o_ref):
    return o_ref[...]


def _lhs_mlstm(hf_ref, hb_ref, og_ref, onorm_ref):
    hs = hf_ref[...] + hb_ref[...]
    og = _sigmoid(og_ref[...])
    parts = []
    for h in range(ML_HEADS):
        sl = slice(h * ML_DV, (h + 1) * ML_DV)
        parts.append(_rms(hs[:, sl]) * onorm_ref[:, sl] * og[:, sl])
    return jnp.concatenate(parts, axis=1)


def _mixer_out_kernel(*refs, n_lhs, lhs_fn):
    lhs_refs = refs[:n_lhs]
    w_ref, x_ref, mod_ref, gain_ref, wr_ref, br_ref, xo_ref, f_ref, r_ref, cnt_ref = refs[n_lhs:]
    y = _dot(lhs_fn(*lhs_refs).astype(BF16), w_ref[...])
    mod = mod_ref[0]
    xn = x_ref[...] + mod[2:3, :] * y
    xo_ref[...] = xn
    f = _norm_mod(xn, gain_ref[...], mod[3:4, :], mod[4:5, :])
    f_ref[...] = f
    r_ref[...], hist = _route(_dot3(f, wr_ref[...]) + br_ref[...])
    _accumulate_counts(cnt_ref, hist)


def _mixer_out(rows, lhs_fn, lhs_args, lhs_specs, w_out, x, mod, gain, wr, br, name):
    n = rows.n * TM
    const = lambda k: (0, 0)
    out = lambda w: pl.BlockSpec((TM, w), lambda k: (k, 0))
    return pl.pallas_call(
        functools.partial(_mixer_out_kernel, n_lhs=len(lhs_args), lhs_fn=lhs_fn),
        grid=(rows.n,),
        in_specs=list(lhs_specs) + [
            pl.BlockSpec((D_MODEL, D_MODEL), const),
            pl.BlockSpec((TM, D_MODEL), lambda k: (rows.src(k), 0)),
            pl.BlockSpec((1, 8, D_MODEL), lambda k: (rows.mod(k), 0, 0)),
            pl.BlockSpec((1, D_MODEL), const),
            pl.BlockSpec((D_MODEL, 128), const),
            pl.BlockSpec((1, 128), const)],
        out_specs=[out(D_MODEL), out(D_MODEL), out(128), pl.BlockSpec((1, 128), const)],
        out_shape=[jax.ShapeDtypeStruct((n, D_MODEL), F32),
                   jax.ShapeDtypeStruct((n, D_MODEL), F32),
                   jax.ShapeDtypeStruct((n, 128), F32),
                   jax.ShapeDtypeStruct((1, 128), F32)],
        compiler_params=_cparams(("arbitrary",)),
        name=name,
    )(*lhs_args, w_out, x, mod, gain, wr, br)


RANK_TILES = 4


def _rank_kernel(r_ref, start_ref, tril_ref, pos_ref, carry_ref):
    @pl.when(pl.program_id(0) == 0)
    def _():
        carry_ref[...] = jnp.zeros_like(carry_ref)

    r = r_ref[...]
    lane = lax.broadcasted_iota(jnp.int32, r.shape, 1).astype(F32)
    chosen = [jnp.where(lane == r[:, k:k + 1] + MOE_GROUPS, 1.0, 0.0) for k in range(MOE_TOPK)]
    both = functools.reduce(jnp.add, chosen)
    before = _dot(tril_ref[...], both.astype(BF16)) + (start_ref[...] + carry_ref[...])
    lane_i = lax.broadcasted_iota(jnp.int32, r.shape, 1)
    pos = jnp.zeros(r.shape, F32)
    for k in range(MOE_TOPK):
        pos = jnp.where(lane_i == k, jnp.sum(chosen[k] * before, axis=-1, keepdims=True), pos)
    pos_t = pos.T[0:8, :].astype(jnp.int32)
    for j in range(RANK_TILES):
        pos_ref[j] = pos_t[:, j * TM:(j + 1) * TM]
    carry_ref[...] += jnp.sum(both, axis=0, keepdims=True)


def _assignment_slots(ntiles, route, starts):
    assert ntiles % RANK_TILES == 0
    rows = RANK_TILES * TM
    tril = jnp.tril(jnp.ones((rows, rows), BF16), -1)
    return pl.pallas_call(
        _rank_kernel,
        grid=(ntiles // RANK_TILES,),
        in_specs=[pl.BlockSpec((rows, 128), lambda i: (i, 0)),
                  pl.BlockSpec((1, 128), lambda i: (0, 0)),
                  pl.BlockSpec((rows, rows), lambda i: (0, 0))],
        out_specs=pl.BlockSpec((RANK_TILES, 8, TM), lambda i: (i, 0, 0)),
        out_shape=jax.ShapeDtypeStruct((ntiles, 8, TM), jnp.int32),
        scratch_shapes=[pltpu.VMEM((1, 128), F32)],
        compiler_params=_cparams(("arbitrary",)),
        name="moe_rank",
    )(route, starts, tril)


def _row_wait(hbm, buf, sem):
    pltpu.make_async_copy(hbm.at[pl.ds(0, TM), :], buf, sem).wait()


def _dispatch_kernel(pos_ref, f_ref, xs_hbm, buf, sem, *, ntiles):
    i = pl.program_id(0)

    def step(s):
        @pl.when(i >= 2)
        def _():
            for _ in range(MOE_TOPK):
                _row_wait(xs_hbm, buf.at[s], sem.at[s])
        buf[s] = f_ref[...]
        for r in range(TM):
            for k in range(MOE_TOPK):
                pltpu.make_async_copy(buf.at[s, pl.ds(r, 1), :], xs_hbm.at[pl.ds(pos_ref[0, k, r], 1), :],
                                      sem.at[s]).start(priority=k % 2)

        @pl.when(i == ntiles - 1)
        def _():
            for slot in ((1 - s, s) if ntiles >= 2 else (s,)):
                for _ in range(MOE_TOPK):
                    _row_wait(xs_hbm, buf.at[slot], sem.at[slot])

    for s in range(2):
        pl.when(i % 2 == s)(functools.partial(step, s))


def _dispatch(ntiles, f, pos):
    n = ntiles * TM
    return pl.pallas_call(
        functools.partial(_dispatch_kernel, ntiles=ntiles),
        grid=(ntiles,),
        in_specs=[pl.BlockSpec((1, 8, TM), lambda i: (i, 0, 0), memory_space=pltpu.SMEM),
                  pl.BlockSpec((TM, D_MODEL), lambda i: (i, 0))],
        out_specs=pl.BlockSpec(memory_space=pl.ANY),
        out_shape=jax.ShapeDtypeStruct((MOE_TOPK * n, D_MODEL), F32),
        scratch_shapes=[pltpu.VMEM((2, TM, D_MODEL), F32), pltpu.SemaphoreType.DMA((2,))],
        compiler_params=_cparams(("arbitrary",)),
        name="moe_dispatch",
    )(pos, f)


def _expert_kernel(vb_ref, ve_ref, lo_ref, hi_ref, first_ref, newexp_ref, eslot_ref, enext_ref, x_ref, wgu_hbm,
                   wd_hbm, y_ref, wgu_f, wd_f, wgu_b, wd_b, sem, *, layer):
    v = pl.program_id(0)

    def fetch(e, s):
        return (pltpu.make_async_copy(wgu_hbm.at[layer, e], wgu_f.at[s], sem.at[0, s]),
                pltpu.make_async_copy(wd_hbm.at[layer, e], wd_f.at[s], sem.at[1, s]))

    @pl.when(newexp_ref[v] == 1)
    def _():
        for s in range(2):
            @pl.when(eslot_ref[v] == s)
            def _():
                @pl.when(v == 0)
                def _():
                    for c in fetch(ve_ref[v], s):
                        c.start()

                @pl.when(enext_ref[v] >= 0)
                def _():
                    for c in fetch(enext_ref[v], 1 - s):
                        c.start()
                for c in fetch(ve_ref[v], s):
                    c.wait()
                wgu_b[...] = wgu_f[s].astype(BF16)
                wd_b[...] = wd_f[s].astype(BF16)

    @pl.when(hi_ref[v] > lo_ref[v])
    def _():
        gu = _dot(x_ref[...].astype(BF16), wgu_b[...])
        gate = gu[:, :MOE_FF]
        act = gate * _sigmoid(gate) * gu[:, MOE_FF:]
        y = _dot(act.astype(BF16), wd_b[...])
        r = lax.broadcasted_iota(jnp.int32, (MOE_BM, 1), 0)
        mine = jnp.logical_and(r >= lo_ref[v], r < hi_ref[v])
        y = jnp.where(mine, y, 0.0)

        @pl.when(first_ref[v] == 1)
        def _():
            y_ref[...] = y

        @pl.when(first_ref[v] == 0)
        def _():
            y_ref[...] += y


def _expert_ffn(xs, visits, layer, w_gate_up, w_down):
    nvis = visits[0].shape[0]
    blk_idx = lambda v, vb, *_: (vb[v], 0)
    grid_spec = pltpu.PrefetchScalarGridSpec(
        num_scalar_prefetch=8,
        grid=(nvis,),
        in_specs=[pl.BlockSpec((MOE_BM, D_MODEL), blk_idx),
                  pl.BlockSpec(memory_space=pl.ANY),
                  pl.BlockSpec(memory_space=pl.ANY)],
        out_specs=pl.BlockSpec((MOE_BM, D_MODEL), blk_idx),
        scratch_shapes=[pltpu.VMEM((2, D_MODEL, 2 * MOE_FF), F32), pltpu.VMEM((2, MOE_FF, D_MODEL), F32),
                        pltpu.VMEM((D_MODEL, 2 * MOE_FF), BF16), pltpu.VMEM((MOE_FF, D_MODEL), BF16),
                        pltpu.SemaphoreType.DMA((2, 2))],
    )
    return pl.pallas_call(
        functools.partial(_expert_kernel, layer=layer),
        grid_spec=grid_spec,
        out_shape=jax.ShapeDtypeStruct(xs.shape, F32),
        compiler_params=_cparams(("arbitrary",)),
        name="moe_expert_ffn",
    )(*visits, xs, w_gate_up, w_down)


def _combine_kernel(pos_ref, nxt_ref, x_ref, r_ref, mod_ref, ys_hbm, o_ref, ybuf, sem, *, ntiles):
    i = pl.program_id(0)

    def gather(table, s):
        for r in range(TM):
            for k in range(MOE_TOPK):
                pltpu.make_async_copy(ys_hbm.at[pl.ds(table[0, k, r], 1), :], ybuf.at[s, k, pl.ds(r, 1), :],
                                      sem.at[s]).start(priority=k % 2)

    @pl.when(i == 0)
    def _():
        gather(pos_ref, 0)

    def step(s):
        @pl.when(i + 1 < ntiles)
        def _():
            gather(nxt_ref, 1 - s)
        for k in range(MOE_TOPK):
            _row_wait(ys_hbm, ybuf.at[s, k], sem.at[s])
        w = r_ref[...]
        y = w[:, MOE_TOPK:MOE_TOPK + 1] * ybuf[s, 0]
        for k in range(1, MOE_TOPK):
            y = y + w[:, MOE_TOPK + k:MOE_TOPK + k + 1] * ybuf[s, k]
        o_ref[...] = x_ref[...] + mod_ref[0][5:6, :] * y

    for s in range(2):
        pl.when(i % 2 == s)(functools.partial(step, s))


def _combine(rows, x, ys, pos, route, mod):
    n = rows.n
    spec = pl.BlockSpec((TM, D_MODEL), lambda i: (i, 0))
    return pl.pallas_call(
        functools.partial(_combine_kernel, ntiles=n),
        grid=(n,),
        in_specs=[pl.BlockSpec((1, 8, TM), lambda i: (i, 0, 0), memory_space=pltpu.SMEM),
                  pl.BlockSpec((1, 8, TM), lambda i: (jnp.minimum(i + 1, n - 1), 0, 0), memory_space=pltpu.SMEM),
                  spec,
                  pl.BlockSpec((TM, 128), lambda i: (i, 0)),
                  pl.BlockSpec((1, 8, D_MODEL), lambda i: (rows.mod(i), 0, 0)),
                  pl.BlockSpec(memory_space=pl.ANY)],
        out_specs=spec,
        out_shape=jax.ShapeDtypeStruct((n * TM, D_MODEL), F32),
        scratch_shapes=[pltpu.VMEM((2, MOE_TOPK, TM, D_MODEL), F32), pltpu.SemaphoreType.DMA((2,))],
        compiler_params=_cparams(("arbitrary",)),
        name="moe_combine",
    )(pos, pos, x, route, mod, ys)


def _visit_tables(bounds, nk):
    E = MOE_EXPERTS
    nblk = nk // MOE_BM
    nvis = nblk + E
    starts, ends = bounds[:-1], bounds[1:]
    fb = starts // MOE_BM
    nv = jnp.where(ends > starts, (ends - 1) // MOE_BM - fb + 1, 0)
    cum = jnp.cumsum(nv)
    total = cum[-1]
    v = jnp.arange(nvis, dtype=jnp.int32)
    active = v < total
    vc = jnp.minimum(v, total - 1)
    ve = jnp.minimum(jnp.sum((cum[None, :] <= vc[:, None]).astype(jnp.int32), axis=1), E - 1)
    vb = fb[ve] + (vc - (cum - nv)[ve])
    lo = jnp.where(active, jnp.maximum(starts[ve], vb * MOE_BM) - vb * MOE_BM, 0)
    hi = jnp.where(active, jnp.minimum(ends[ve], (vb + 1) * MOE_BM) - vb * MOE_BM, 0)
    prev_b = jnp.concatenate([jnp.full((1,), -1, jnp.int32), vb[:-1]])
    first = jnp.logical_and(active, vb != prev_b)
    prev_e = jnp.concatenate([jnp.full((1,), -1, jnp.int32), ve[:-1]])
    new_expert = jnp.logical_and(active, ve != prev_e)
    eslot = (jnp.cumsum(new_expert.astype(jnp.int32)) - 1) % 2
    at = jnp.where(new_expert, v, nvis)
    nxt = jnp.concatenate([lax.cummin(at[::-1])[::-1][1:], jnp.full((1,), nvis, jnp.int32)])
    enext = jnp.where(nxt < nvis, ve[jnp.minimum(nxt, nvis - 1)], -1)
    i32 = lambda a: a.astype(jnp.int32)
    return i32(vb), i32(ve), i32(lo), i32(hi), i32(first), i32(new_expert), i32(eslot), i32(enext)


def _moe(rows, x, f, route, counts, mod, layer, w_gate_up, w_down):
    n = rows.n * TM
    cum = jnp.cumsum(counts[0])
    starts = (cum - counts[0]).reshape(1, 128)
    bounds = jnp.concatenate([starts[0, MOE_GROUPS:MOE_GROUPS + MOE_EXPERTS], cum[-1:]]).astype(jnp.int32)
    pos = _assignment_slots(rows.n, route, starts)
    xs = _dispatch(rows.n, f, pos)
    ys = _expert_ffn(xs, _visit_tables(bounds, n * MOE_TOPK), layer, w_gate_up, w_down)
    return _combine(rows, x, ys, pos, route, mod)


def kernel(x, c, ctx, c_ctx, ada_w, ada_b, norm_mix, norm_ffn, rg_w_in, rg_conv_w, rg_conv_b, rg_gate_w, rg_gate_b, rg_lambda, rg_w_out, mla_w_down, mla_q_norm, mla_kv_norm, mla_w_uq, mla_w_ukv, mla_qk_norm, mla_w_o, ml_w_in, ml_gate_b, ml_out_norm, ml_w_out, moe_w_group, moe_b_group, moe_w_expert, moe_b_expert, moe_w_gate_up, moe_w_down):
    B, L, D = x.shape
    C = ctx.shape[1]
    depth = ada_w.shape[0]
    assert D == D_MODEL
    st = _Stream(B, L, C)

    xs = jnp.concatenate([ctx, x], axis=1).reshape(st.NT, D)
    cc = jnp.zeros((16, D), F32).at[:B].set(c).at[B].set(c_ctx)
    mod_all = _modulation(cc, ada_w, ada_b)
    mod_all = jnp.pad(mod_all[:, :B + 1].reshape(depth, B + 1, 6, D), ((0, 0), (0, 0), (0, 2), (0, 0)))
    perm = _row_permutation(B)

    row = lambda a: a.reshape(1, -1)

    for i in range(depth):
        last = i == depth - 1
        mod = mod_all[i]
        kind, j = i % 3, i // 3
        all_rows = st.all_rows()
        out_rows = st.latent_rows() if last else all_rows
        tile_spec = lambda w: pl.BlockSpec((TM, w), lambda k: (out_rows.src(k), 0))
        wr = jnp.zeros((D, 128), F32).at[:, :MOE_GROUPS].set(moe_w_group[i]) \
            .at[:, MOE_GROUPS:MOE_GROUPS + MOE_EXPERTS].set(moe_w_expert[i])
        br = jnp.zeros((1, 128), F32).at[0, :MOE_GROUPS].set(moe_b_group[i]) \
            .at[0, MOE_GROUPS:MOE_GROUPS + MOE_EXPERTS].set(moe_b_expert[i])
        out_args = (xs, mod, row(norm_ffn[i]), wr, br)

        if kind == 0:
            gate, u = _rg_in(st, xs, mod, row(norm_mix[i]), perm, rg_w_in[j].astype(BF16))
            wg, gb = _rg_gate_weights(rg_gate_w[j], rg_gate_b[j])
            hs = _rg_scan(st, u, rg_conv_w[j], row(rg_conv_b[j]), wg, gb, rg_lambda[j].reshape(2, 1, D))
            xs, f, route, counts = _rg_out(st, last, gate, hs, perm.T, rg_w_out[j].astype(BF16), *out_args)
        elif kind == 1:
            w_down = jnp.pad(mla_w_down[j], ((0, 0), (0, 512 - mla_w_down.shape[2]))).astype(BF16)
            down = _norm_proj(all_rows, xs, mod, row(norm_mix[i]), w_down, name="mla_down_proj")
            wq, wk, we, wv = _mla_weights(mla_w_uq[j], mla_w_ukv[j])
            q, k, v = _mla_up(st, all_rows, down, row(mla_q_norm[j]), row(mla_kv_norm[j]), wq, wk, we, wv,
                              _rope_tables(L, mla_qk_norm[j]))
            o = _attention(st, q, k, v)
            xs, f, route, counts = _mixer_out(out_rows, _lhs_mla, (o,), (tile_spec(D),),
                                      mla_w_o[j].astype(BF16), *out_args, name="mla_out")
        else:
            n_in = ml_w_in.shape[2]
            w_in = jnp.pad(ml_w_in[j], ((0, 0), (0, ML_NP - n_in))).astype(BF16)
            proj = _norm_proj(all_rows, xs, mod, row(norm_mix[i]), w_in, name="mlstm_in_proj")
            gate_bias = jnp.pad(ml_gate_b[j].reshape(1, -1), ((0, 0), (0, 128 - 4 * ML_HEADS)))
            hf, hb = _mlstm(st, proj, gate_bias)
            og_spec = pl.BlockSpec((TM, D), lambda k: (out_rows.src(k), 2))
            xs, f, route, counts = _mixer_out(out_rows, _lhs_mlstm, (hf, hb, proj, row(ml_out_norm[j])),
                                      (tile_spec(D), tile_spec(D), og_spec, pl.BlockSpec((1, D), lambda k: (0, 0))),
                                      ml_w_out[j].astype(BF16), *out_args, name="mlstm_out")

        moe_rows = st.dense_latent_rows() if last else all_rows
        xs = _moe(moe_rows, xs, f, route, counts, mod, i, moe_w_gate_up, moe_w_down)

    return xs.reshape(B, L, D)
```

```python
import functools
import math

import jax
import jax.numpy as jnp
from jax import lax
from jax.experimental import pallas as pl
from jax.experimental.pallas import tpu as pltpu

F32 = jnp.float32
BF16 = jnp.bfloat16

D_MODEL = 1024
RMS_EPS = 1e-6

TM = 256
VMEM_LIMIT = 48 * 1024 * 1024

RG_BLOCK_W = 64
RG_CHUNK = 256
RG_CONV_W = 4
RG_C = 8.0
RG_TT = 64

MLA_HEADS = 16
MLA_Q_RANK = 256
MLA_KV_RANK = 128
MLA_NOPE = 64
MLA_ROPE = 32
MLA_V = 64
MLA_QK = MLA_NOPE + MLA_ROPE
MLA_HP = 128
ROPE_AXIS_DIM = MLA_ROPE // 2
ROPE_BASE = 10000.0
GRID_W = 64
ATT_HEADS = 4

ML_HEADS = 4
ML_DV = 256
ML_DQK = 128
ML_TC = 256
ML_M_INIT = -1e30
ML_NP = 3200

MOE_GROUPS = 8
MOE_PER_GROUP = 8
MOE_EXPERTS = 64
MOE_TOPK = 2
MOE_FF = 256
MOE_BM = 512


def _cparams(sem):
    return pltpu.CompilerParams(dimension_semantics=sem, vmem_limit_bytes=VMEM_LIMIT)


def _dot(a, b):
    return jnp.dot(a, b, preferred_element_type=F32)


def _dot_t(a, b):
    return lax.dot_general(a, b, (((1,), (1,)), ((), ())), preferred_element_type=F32)


def _dot3(a, b):
    ah = a.astype(BF16)
    al = (a - ah.astype(F32)).astype(BF16)
    bh = b.astype(BF16)
    bl = (b - bh.astype(F32)).astype(BF16)
    return _dot(ah, bh) + (_dot(al, bh) + _dot(ah, bl))


def _sigmoid(x):
    return 0.5 * jnp.tanh(0.5 * x) + 0.5


def _softplus(x):
    return jnp.maximum(x, 0.0) + jnp.log1p(jnp.exp(-jnp.abs(x)))


def _gelu_tanh(x):
    return 0.5 * x * (1.0 + jnp.tanh(0.7978845608028654 * (x + 0.044715 * (x * x * x))))


def _rms(x, n=None):
    n = x.shape[-1] if n is None else n
    ms = jnp.sum(x * x, axis=-1, keepdims=True) * (1.0 / n)
    return x * lax.rsqrt(ms + RMS_EPS)


class _Rows:
    def __init__(self, n, src, mod):
        self.n, self.src, self.mod = n, src, mod


class _Stream:
    def __init__(self, B, L, C):
        assert L % TM == 0 and C % TM == 0 and B % 8 == 0
        self.B, self.L, self.C = B, L, C
        self.ltot = L + C
        self.lt, self.ct = L // TM, C // TM
        self.tpb = self.lt + self.ct
        self.NT = B * self.ltot

    def all_rows(self):
        tpb, ct, B = self.tpb, self.ct, self.B
        return _Rows(B * tpb, lambda k: k, lambda k: jnp.where(k % tpb < ct, B, k // tpb))

    def latent_rows(self):
        tpb, ct, lt = self.tpb, self.ct, self.lt
        return _Rows(self.B * lt, lambda k: (k // lt) * tpb + ct + k % lt, lambda k: k // lt)

    def dense_latent_rows(self):
        lt = self.lt
        return _Rows(self.B * lt, lambda k: k, lambda k: k // lt)


def _mod_kernel(c_ref, w_ref, b_ref, o_ref):
    c = c_ref[...]
    o_ref[0] = _dot3(c * _sigmoid(c), w_ref[0]) + b_ref[0]


def _modulation(cc, ada_w, ada_b):
    depth, d, n = ada_w.shape
    tn = 1536
    return pl.pallas_call(
        _mod_kernel,
        grid=(depth, n // tn),
        in_specs=[pl.BlockSpec((16, d), lambda l, j: (0, 0)),
                  pl.BlockSpec((1, d, tn), lambda l, j: (l, 0, j)),
                  pl.BlockSpec((1, 1, tn), lambda l, j: (l, 0, j))],
        out_specs=pl.BlockSpec((1, 16, tn), lambda l, j: (l, 0, j)),
        out_shape=jax.ShapeDtypeStruct((depth, 16, n), F32),
        compiler_params=_cparams(("arbitrary", "arbitrary")),
        name="ada_modulation",
    )(cc, ada_w, ada_b.reshape(depth, 1, n))


def _norm_mod(x, gain, shift, scale):
    return _rms(x) * gain * (1.0 + scale) + shift


def _norm_proj_kernel(x_ref, mod_ref, g_ref, w_ref, o_ref):
    mod = mod_ref[0]
    h = _norm_mod(x_ref[...], g_ref[...], mod[0:1, :], mod[1:2, :]).astype(BF16)
    o_ref[...] = _dot(h, w_ref[...]).astype(o_ref.dtype)


def _norm_proj(rows, x, mod, gain, w, name):
    n = w.shape[1]
    const = lambda k: (0, 0)
    return pl.pallas_call(
        _norm_proj_kernel,
        grid=(rows.n,),
        in_specs=[pl.BlockSpec((TM, D_MODEL), lambda k: (rows.src(k), 0)),
                  pl.BlockSpec((1, 8, D_MODEL), lambda k: (rows.mod(k), 0, 0)),
                  pl.BlockSpec((1, D_MODEL), const),
                  pl.BlockSpec((D_MODEL, n), const)],
        out_specs=pl.BlockSpec((TM, n), lambda k: (k, 0)),
        out_shape=jax.ShapeDtypeStruct((rows.n * TM, n), F32),
        compiler_params=_cparams(("arbitrary",)),
        name=name,
    )(x, mod, gain, w)


RG_TR = 32


def _row_permutation(B):
    n = B * RG_TR
    r = jnp.arange(n)
    src = (r % B) * RG_TR + r // B
    return (src[:, None] == jnp.arange(n)[None, :]).astype(BF16)


def _time_tile_mod(mod_ref, is_ctx, B, row):
    return jnp.where(is_ctx, mod_ref[B:B + 1, row:row + 1, :], mod_ref[0:B, row:row + 1, :])


def _rg_in_kernel(x_ref, mod_ref, g_ref, perm_ref, wg_ref, wu_ref, gate_ref, u_ref, *, B, ctx_tiles):
    is_ctx = pl.program_id(0) < ctx_tiles
    h = _norm_mod(x_ref[...], g_ref[...], _time_tile_mod(mod_ref, is_ctx, B, 0), _time_tile_mod(mod_ref, is_ctx, B, 1))
    h = h.reshape(B * RG_TR, D_MODEL).astype(BF16)
    h = _dot(perm_ref[...], h).astype(BF16)
    gate_ref[...] = _gelu_tanh(_dot(h, wg_ref[...])).astype(gate_ref.dtype)
    u_ref[...] = _dot(h, wu_ref[...])


def _rg_in(st, x, mod, gain, perm, w_in):
    B, W = st.B, D_MODEL
    R = B * RG_TR
    nt = st.ltot // RG_TR
    const = lambda t: (0, 0)
    return pl.pallas_call(
        functools.partial(_rg_in_kernel, B=B, ctx_tiles=st.C // RG_TR),
        grid=(nt,),
        in_specs=[pl.BlockSpec((B, RG_TR, D_MODEL), lambda t: (0, t, 0)),
                  pl.BlockSpec((B + 1, 8, D_MODEL), lambda t: (0, 0, 0)),
                  pl.BlockSpec((1, D_MODEL), const),
                  pl.BlockSpec((R, R), const),
                  pl.BlockSpec((D_MODEL, W), const),
                  pl.BlockSpec((D_MODEL, W), lambda t: (0, 1))],
        out_specs=[pl.BlockSpec((R, W), lambda t: (t, 0)),
                   pl.BlockSpec((R, W), lambda t: (t, 0))],
        out_shape=[jax.ShapeDtypeStruct((st.ltot * B, W), BF16), jax.ShapeDtypeStruct((st.ltot * B, W), F32)],
        compiler_params=_cparams(("arbitrary",)),
        name="rg_in_proj",
    )(x.reshape(B, st.ltot, D_MODEL), mod, gain, perm, w_in, w_in)


def _rg_tile_order(d, k, ct, ntt):
    bwd = jnp.where(k < ct, ct - 1 - k, ntt - 1 - (k - ct))
    return jnp.where(d == 0, k, bwd)


def _rg_scan_kernel(um_ref, up_ref, un_ref, cw_ref, cb_ref, wg_ref, gb_ref, lam_ref, o_ref,
                    ext_ref, a_ref, b_ref, h_ref, *, B, ct, ntt):
    d = pl.program_id(0)
    k = pl.program_id(1)
    tile = _rg_tile_order(d, k, ct, ntt)
    R = RG_TT * B

    @pl.when(k == 0)
    def _():
        h_ref[...] = jnp.zeros_like(h_ref)

    seq_start = jnp.logical_or(tile == 0, tile == ct)
    seq_end = jnp.logical_or(tile == ct - 1, tile == ntt - 1)
    ext_ref[0:2 * B, :] = jnp.where(seq_start, 0.0, up_ref[...])
    ext_ref[2 * B:2 * B + R, :] = um_ref[...]
    ext_ref[2 * B + R:3 * B + R, :] = jnp.where(seq_end, 0.0, un_ref[...])
    cw = cw_ref[...]
    uc = cb_ref[...] + cw[0:1, :] * ext_ref[0:R, :]
    for j in range(1, RG_CONV_W):
        uc = uc + cw[j:j + 1, :] * ext_ref[j * B:j * B + R, :]
    ucb = uc.astype(BF16)
    c_lam = -RG_C * _softplus(-lam_ref[0])
    for c in range(D_MODEL // RG_CHUNK):
        sl = slice(c * RG_CHUNK, (c + 1) * RG_CHUNK)
        z = _dot(ucb[:, sl], wg_ref[0, c]) + gb_ref[0, c]
        r = _sigmoid(z[:, :RG_CHUNK])
        ig = _sigmoid(z[:, RG_CHUNK:])
        log_a = c_lam[:, sl] * r
        a = jnp.exp(log_a)
        one_minus_a2 = -jnp.tanh(log_a) * (a * a + 1.0)
        a_ref[:, sl] = a
        b_ref[:, sl] = jnp.sqrt(one_minus_a2) * (ig * uc[:, sl])

    def scan(times):
        for c in range(D_MODEL // 128):
            cs = slice(c * 128, (c + 1) * 128)
            h = h_ref[:, cs]
            for t in times:
                rs = slice(t * B, (t + 1) * B)
                h = a_ref[rs, cs] * h + b_ref[rs, cs]
                b_ref[rs, cs] = h
            h_ref[:, cs] = h

    pl.when(d == 0)(lambda: scan(range(RG_TT)))
    pl.when(d == 1)(lambda: scan(range(RG_TT - 1, -1, -1)))
    o_ref[0] = b_ref[...].astype(o_ref.dtype)


def _rg_scan(st, u_tm, conv_w, conv_b, wg, gb, lam):
    B, W = st.B, D_MODEL
    assert st.C % RG_TT == 0 and st.L % RG_TT == 0
    ltot = st.ltot
    ntt, ct = ltot // RG_TT, st.C // RG_TT
    R = RG_TT * B
    order = functools.partial(_rg_tile_order, ct=ct, ntt=ntt)
    nch = W // RG_CHUNK
    return pl.pallas_call(
        functools.partial(_rg_scan_kernel, B=B, ct=ct, ntt=ntt),
        grid=(2, ntt),
        in_specs=[pl.BlockSpec((R, W), lambda d, k: (order(d, k), 0)),
                  pl.BlockSpec((2 * B, W), lambda d, k: (jnp.maximum(order(d, k) * (RG_TT // 2) - 1, 0), 0)),
                  pl.BlockSpec((B, W), lambda d, k: (jnp.minimum((order(d, k) + 1) * RG_TT, ltot - 1), 0)),
                  pl.BlockSpec((RG_CONV_W, W), lambda d, k: (0, 0)),
                  pl.BlockSpec((1, W), lambda d, k: (0, 0)),
                  pl.BlockSpec((1, nch, RG_CHUNK, 2 * RG_CHUNK), lambda d, k: (d, 0, 0, 0)),
                  pl.BlockSpec((1, nch, 1, 2 * RG_CHUNK), lambda d, k: (d, 0, 0, 0)),
                  pl.BlockSpec((1, 1, W), lambda d, k: (d, 0, 0))],
        out_specs=pl.BlockSpec((1, R, W), lambda d, k: (d, order(d, k), 0)),
        out_shape=jax.ShapeDtypeStruct((2, ltot * B, W), BF16),
        scratch_shapes=[pltpu.VMEM((3 * B + R, W), F32),
                        pltpu.VMEM((R, W), F32),
                        pltpu.VMEM((R, W), F32),
                        pltpu.VMEM((B, W), F32)],
        compiler_params=_cparams(("arbitrary", "arbitrary")),
        name="rg_scan",
    )(u_tm, u_tm, u_tm, conv_w, conv_b, wg, gb, lam)


def _rg_gate_weights(gate_w, gate_b):
    nb = gate_w.shape[2]
    per = RG_CHUNK // RG_BLOCK_W
    nch = nb // per
    gw = gate_w.reshape(2, 2, nch, per, RG_BLOCK_W, RG_BLOCK_W)
    eye = jnp.eye(per, dtype=gate_w.dtype)
    bd = jnp.einsum('dgcnij,nm->dgcnimj', gw, eye).reshape(2, 2, nch, RG_CHUNK, RG_CHUNK)
    wg = jnp.concatenate([bd[:, 0], bd[:, 1]], axis=-1).astype(BF16)
    gb = gate_b.reshape(2, 2, nch, 1, RG_CHUNK)
    gb = jnp.concatenate([gb[:, 0], gb[:, 1]], axis=-1)
    return wg, gb


def _rg_out_kernel(g_ref, hf_ref, hb_ref, perm_ref, w_ref, x_ref, mod_ref, gain_ref, wr_ref, br_ref,
                   xo_ref, f_ref, r_ref, cnt_ref, *, B, ctx_tiles, t0):
    is_ctx = pl.program_id(0) + t0 < ctx_tiles
    hsum = hf_ref[0].astype(F32) + hb_ref[0].astype(F32)
    lhs = (g_ref[...].astype(F32) * hsum).astype(BF16)
    lhs = _dot(perm_ref[...], lhs).astype(BF16)
    y = _dot(lhs, w_ref[...]).reshape(B, RG_TR, D_MODEL)
    m = lambda row: _time_tile_mod(mod_ref, is_ctx, B, row)
    xn = x_ref[...] + m(2) * y
    xo_ref[...] = xn
    f = _norm_mod(xn, gain_ref[...], m(3), m(4))
    f_ref[...] = f
    lg = _dot3(f.reshape(B * RG_TR, D_MODEL), wr_ref[...]) + br_ref[...]
    table, hist = _route(lg)
    r_ref[...] = table.reshape(B, RG_TR, 128)
    _accumulate_counts(cnt_ref, hist)


def _rg_out(st, latent_only, gate, hs, perm, w_out, x, mod, gain, wr, br):
    B = st.B
    R = B * RG_TR
    t0 = st.C // RG_TR if latent_only else 0
    lo = st.L if latent_only else st.ltot
    nt = lo // RG_TR
    const = lambda t: (0, 0)
    blk = lambda w: pl.BlockSpec((B, RG_TR, w), lambda t: (0, t, 0))
    xo, f, route, counts = pl.pallas_call(
        functools.partial(_rg_out_kernel, B=B, ctx_tiles=st.C // RG_TR, t0=t0),
        grid=(nt,),
        in_specs=[pl.BlockSpec((R, D_MODEL), lambda t: (t + t0, 0)),
                  pl.BlockSpec((1, R, D_MODEL), lambda t: (0, t + t0, 0)),
                  pl.BlockSpec((1, R, D_MODEL), lambda t: (1, t + t0, 0)),
                  pl.BlockSpec((R, R), const),
                  pl.BlockSpec((D_MODEL, D_MODEL), const),
                  pl.BlockSpec((B, RG_TR, D_MODEL), lambda t: (0, t + t0, 0)),
                  pl.BlockSpec((B + 1, 8, D_MODEL), lambda t: (0, 0, 0)),
                  pl.BlockSpec((1, D_MODEL), const),
                  pl.BlockSpec((D_MODEL, 128), const),
                  pl.BlockSpec((1, 128), const)],
        out_specs=[blk(D_MODEL), blk(D_MODEL), blk(128), pl.BlockSpec((1, 128), const)],
        out_shape=[jax.ShapeDtypeStruct((B, lo, D_MODEL), F32),
                   jax.ShapeDtypeStruct((B, lo, D_MODEL), F32),
                   jax.ShapeDtypeStruct((B, lo, 128), F32),
                   jax.ShapeDtypeStruct((1, 128), F32)],
        compiler_params=_cparams(("arbitrary",)),
        name="rg_out",
    )(gate, hs, hs, perm, w_out, x.reshape(B, st.ltot, D_MODEL), mod, gain, wr, br)
    return xo.reshape(B * lo, D_MODEL), f.reshape(B * lo, D_MODEL), route.reshape(B * lo, 128), counts


_ROPE_HALF = ROPE_AXIS_DIM // 2
_MLA_SRC_DIM = (list(range(MLA_NOPE + ROPE_AXIS_DIM)) + list(range(MLA_NOPE, MLA_NOPE + _ROPE_HALF))
                + list(range(MLA_NOPE + ROPE_AXIS_DIM, MLA_QK))
                + list(range(MLA_NOPE + ROPE_AXIS_DIM, MLA_NOPE + ROPE_AXIS_DIM + _ROPE_HALF)))
_MLA_REAL_LANE = ([1.0] * (MLA_NOPE + ROPE_AXIS_DIM) + [0.0] * _ROPE_HALF + [1.0] * ROPE_AXIS_DIM
                  + [0.0] * _ROPE_HALF + [0.0] * (MLA_HP - len(_MLA_SRC_DIM)))


def _mla_up_kernel(dn_ref, qn_ref, kvn_ref, wq_ref, wk_ref, we_ref, wv_ref, real_ref, cq_ref, sq_ref, ck_ref, sk_ref,
                   q_ref, k_ref, v_ref):
    dn = dn_ref[...]
    cq = _rms(dn[:, :MLA_Q_RANK]) * qn_ref[...]
    ckv = _rms(dn[:, MLA_Q_RANK:MLA_Q_RANK + MLA_KV_RANK]) * kvn_ref[...]
    kr = dn[:, MLA_Q_RANK + MLA_KV_RANK:]
    kr_hi = kr.astype(BF16)
    kr_lo = (kr - kr_hi.astype(F32)).astype(BF16)
    ckvb = ckv.astype(BF16)
    q_pre = _dot(cq.astype(BF16), wq_ref[...])
    k_pre = _dot(ckvb, wk_ref[...]) + (_dot(kr_hi, we_ref[...]) + _dot(kr_lo, we_ref[...]))
    v_ref[...] = _dot(ckvb, wv_ref[...]).astype(v_ref.dtype)
    real = real_ref[...]

    def head(x, cos_g, sin_g):
        ms = jnp.sum(x * x * real, axis=-1, keepdims=True) * (1.0 / MLA_QK)
        xr = x * lax.rsqrt(ms + RMS_EPS)
        return xr * cos_g + pltpu.roll(xr, MLA_HP - _ROPE_HALF, 1) * sin_g

    cos_q, sin_q, cos_k, sin_k = cq_ref[...], sq_ref[...], ck_ref[...], sk_ref[...]
    for h in range(MLA_HEADS):
        sl = slice(h * MLA_HP, (h + 1) * MLA_HP)
        q_ref[:, sl] = head(q_pre[:, sl], cos_q, sin_q).astype(q_ref.dtype)
        k_ref[:, sl] = head(k_pre[:, sl], cos_k, sin_k).astype(k_ref.dtype)


def _mla_up(st, rows, down, q_norm, kv_norm, wq, wk, we, wv, tables):
    hw = MLA_HEADS * MLA_HP
    const = lambda i: (0, 0)
    tpb, ct, lt = st.tpb, st.ct, st.lt
    rope_idx = lambda i: (jnp.where(i % tpb < ct, lt, i % tpb - ct), 0)
    real = jnp.asarray(_MLA_REAL_LANE, F32).reshape(1, MLA_HP)
    return pl.pallas_call(
        _mla_up_kernel,
        grid=(rows.n,),
        in_specs=[pl.BlockSpec((TM, 512), lambda i: (i, 0)),
                  pl.BlockSpec((1, MLA_Q_RANK), const),
                  pl.BlockSpec((1, MLA_KV_RANK), const),
                  pl.BlockSpec((MLA_Q_RANK, hw), const),
                  pl.BlockSpec((MLA_KV_RANK, hw), const),
                  pl.BlockSpec((128, hw), const),
                  pl.BlockSpec((MLA_KV_RANK, MLA_HEADS * MLA_V), const),
                  pl.BlockSpec((1, MLA_HP), const)] + [pl.BlockSpec((TM, MLA_HP), rope_idx)] * 4,
        out_specs=[pl.BlockSpec((TM, hw), lambda i: (i, 0)),
                   pl.BlockSpec((TM, hw), lambda i: (i, 0)),
                   pl.BlockSpec((TM, MLA_HEADS * MLA_V), lambda i: (i, 0))],
        out_shape=[jax.ShapeDtypeStruct((st.NT, hw), BF16),
                   jax.ShapeDtypeStruct((st.NT, hw), BF16),
                   jax.ShapeDtypeStruct((st.NT, MLA_HEADS * MLA_V), BF16)],
        compiler_params=_cparams(("arbitrary",)),
        name="mla_up_proj",
    )(down, q_norm, kv_norm, wq, wk, we, wv, real, *tables)


def _attn_kernel(q_ref, k_ref, v_ref, o_ref, vaug_ref, *, C, ct):
    qi = pl.program_id(2)

    @pl.when(qi == 0)
    def _():
        lane = lax.broadcasted_iota(jnp.int32, (k_ref.shape[0], 2 * MLA_V), 1)
        for hh in range(ATT_HEADS):
            pair = v_ref[:, (hh // 2) * 2 * MLA_V:(hh // 2 + 1) * 2 * MLA_V].astype(F32)
            if hh % 2 == 0:
                aug = jnp.where(lane < MLA_V, pair, jnp.where(lane == MLA_V, 1.0, 0.0))
            else:
                aug = jnp.where(lane >= MLA_V, pair, jnp.where(lane == 0, 1.0, 0.0))
            vaug_ref[hh] = aug.astype(BF16)

    def attend(nkeys):
        lane = lax.broadcasted_iota(jnp.int32, (TM, 2 * MLA_V), 1)
        for pair in range(ATT_HEADS // 2):
            outs = []
            for hh in (2 * pair, 2 * pair + 1):
                sl = slice(hh * MLA_HP, (hh + 1) * MLA_HP)
                s = _dot_t(q_ref[:, sl], k_ref[0:nkeys, sl])
                p = jnp.exp2(s - jnp.max(s, axis=-1, keepdims=True)).astype(BF16)
                o = _dot(p, vaug_ref[hh, 0:nkeys, :])
                rowsum = o[:, MLA_V:MLA_V + 1] if hh % 2 == 0 else o[:, 0:1]
                outs.append(o * (1.0 / rowsum))
            o_ref[:, pair * 2 * MLA_V:(pair + 1) * 2 * MLA_V] = jnp.where(lane < MLA_V, outs[0], outs[1]).astype(o_ref.dtype)

    pl.when(qi < ct)(lambda: attend(C))
    pl.when(qi >= ct)(lambda: attend(k_ref.shape[0]))


def _attention(st, q, k, v):
    B, ltot, tpb = st.B, st.ltot, st.tpb
    hg = MLA_HEADS // ATT_HEADS
    return pl.pallas_call(
        functools.partial(_attn_kernel, C=st.C, ct=st.ct),
        grid=(B, hg, tpb),
        in_specs=[pl.BlockSpec((TM, ATT_HEADS * MLA_HP), lambda b, h, i: (b * tpb + i, h)),
                  pl.BlockSpec((ltot, ATT_HEADS * MLA_HP), lambda b, h, i: (b, h)),
                  pl.BlockSpec((ltot, ATT_HEADS * MLA_V), lambda b, h, i: (b, h))],
        out_specs=pl.BlockSpec((TM, ATT_HEADS * MLA_V), lambda b, h, i: (b * tpb + i, h)),
        out_shape=jax.ShapeDtypeStruct((st.NT, MLA_HEADS * MLA_V), BF16),
        scratch_shapes=[pltpu.VMEM((ATT_HEADS, ltot, 2 * MLA_V), BF16)],
        compiler_params=_cparams(("arbitrary", "arbitrary", "arbitrary")),
        name="mla_attention",
    )(q, k, v)


def _mla_weights(w_uq, w_ukv):
    H = MLA_HEADS
    src = jnp.asarray(_MLA_SRC_DIM, jnp.int32)
    pad = MLA_HP - len(_MLA_SRC_DIM)
    wq = jnp.pad(w_uq.reshape(MLA_Q_RANK, H, MLA_QK)[:, :, src], ((0, 0), (0, 0), (0, pad)))
    wkv = w_ukv.reshape(MLA_KV_RANK, H, MLA_NOPE + MLA_V)
    wk = jnp.pad(wkv[:, :, :MLA_NOPE], ((0, 0), (0, 0), (0, MLA_HP - MLA_NOPE)))
    wv = wkv[:, :, MLA_NOPE:]
    r = jnp.arange(128)[:, None]
    lane_dim = jnp.pad(src, (0, pad), constant_values=-1)[None, :]
    place = (lane_dim == r + MLA_NOPE).astype(BF16)
    we = jnp.tile(place, (1, H))
    return (wq.reshape(MLA_Q_RANK, H * MLA_HP).astype(BF16), wk.reshape(MLA_KV_RANK, H * MLA_HP).astype(BF16),
            we, wv.reshape(MLA_KV_RANK, H * MLA_V).astype(BF16))


def _rope_tables(L, qk_norm):
    rows = L // GRID_W
    row = jnp.broadcast_to(jnp.arange(rows, dtype=F32)[:, None], (rows, GRID_W)).reshape(L)
    col = jnp.broadcast_to(jnp.arange(GRID_W, dtype=F32)[None, :], (rows, GRID_W)).reshape(L)
    inv_freq = ROPE_BASE ** (-jnp.arange(0, ROPE_AXIS_DIM, 2, dtype=F32) / ROPE_AXIS_DIM)
    ar = row[:, None] * inv_freq
    ac = col[:, None] * inv_freq
    h8 = _ROPE_HALF
    one = jnp.ones((L, MLA_NOPE), F32)
    z8 = jnp.zeros((L, h8), F32)
    zpad = jnp.zeros((L, MLA_HP - len(_MLA_SRC_DIM)), F32)
    cos_t = jnp.concatenate([one, jnp.cos(ar), jnp.cos(ar), z8, jnp.cos(ac), jnp.cos(ac), z8, zpad], axis=1)
    sin_t = jnp.concatenate([0 * one, -jnp.sin(ar), jnp.sin(ar), z8, -jnp.sin(ac), jnp.sin(ac), z8, zpad], axis=1)
    ident = jnp.asarray(_MLA_REAL_LANE, F32)[None, :]
    cos_t = jnp.concatenate([cos_t, jnp.broadcast_to(ident, (TM, MLA_HP))], axis=0)
    sin_t = jnp.concatenate([sin_t, jnp.zeros((TM, MLA_HP), F32)], axis=0)
    src = jnp.asarray(_MLA_SRC_DIM, jnp.int32)
    pad = MLA_HP - len(_MLA_SRC_DIM)
    scale = MLA_QK ** -0.5 * math.log2(math.e)
    tables = []
    for g, s in ((qk_norm[0], scale), (qk_norm[1], 1.0)):
        g_lane = jnp.pad(g[src], (0, pad))
        g_partner = jnp.roll(g_lane, -h8)
        tables += [cos_t * (g_lane * s)[None, :], sin_t * (g_partner * s)[None, :]]
    return tables


def _log_sigmoid(x):
    return jnp.minimum(x, 0.0) - jnp.log1p(jnp.exp(-jnp.abs(x)))


def _mlstm_kernel(qf_ref, kf_ref, vf_ref, gf_ref, qb_ref, kb_ref, vb_ref, gb_ref, bias_ref, tril_ref, triu_ref,
                  of_ref, ob_ref, c_ref, n_ref, m_ref):
    T = ML_TC

    @pl.when(pl.program_id(1) == 0)
    def _():
        c_ref[...] = jnp.zeros_like(c_ref)
        n_ref[...] = jnp.zeros_like(n_ref)
        m_ref[...] = jnp.full(m_ref.shape, ML_M_INIT, F32)

    ti = lax.broadcasted_iota(jnp.int32, (T, T), 0)
    si = lax.broadcasted_iota(jnp.int32, (T, T), 1)
    dirs = ((qf_ref, kf_ref, vf_ref, gf_ref, of_ref, tril_ref), (qb_ref, kb_ref, vb_ref, gb_ref, ob_ref, triu_ref))
    for d, (q_ref, k_ref, v_ref, g_ref, o_ref, tri_ref) in enumerate(dirs):
        tri = (si <= ti) if d == 0 else (si >= ti)
        g = g_ref[...] + bias_ref[...]
        g_t = g.T
        lsg = _log_sigmoid(g)
        lsg_hi = lsg.astype(BF16)
        lsg_lo = (lsg - lsg_hi.astype(F32)).astype(BF16)
        cum = _dot(tri_ref[...], lsg_hi) + _dot(tri_ref[...], lsg_lo)
        cum_t = cum.T
        last = T - 1 if d == 0 else 0
        for h in range(ML_HEADS):
            st = d * ML_HEADS + h
            li, lf_ = (2 * d) * ML_HEADS + h, (2 * d + 1) * ML_HEADS + h
            ig_col = g[:, li:li + 1]
            ig_row = g_t[li:li + 1, :]
            b_col = cum[:, lf_:lf_ + 1]
            b_row = cum_t[lf_:lf_ + 1, :]
            total = cum[last:last + 1, lf_:lf_ + 1]
            m_old = m_ref[st, 0:1, 0:1]
            d_log = jnp.where(tri, b_col - b_row + ig_row, -jnp.inf)
            inter_log = b_col + m_old
            m_t = jnp.maximum(inter_log, jnp.max(d_log, axis=1, keepdims=True))
            qh = q_ref[:, h * ML_DQK:(h + 1) * ML_DQK] * (ML_DQK ** -0.5)
            kh = k_ref[:, h * ML_DQK:(h + 1) * ML_DQK]
            vh = v_ref[:, h * ML_DV:(h + 1) * ML_DV].astype(BF16)
            qb16 = qh.astype(BF16)
            s_mat = _dot_t(qb16, kh.astype(BF16)) * jnp.exp(d_log - m_t)
            inter = jnp.exp(inter_log - m_t)
            c_old = c_ref[st]
            n_old = n_ref[st, 0:1, :]
            num = _dot(s_mat.astype(BF16), vh) + inter * _dot(qb16, c_old.astype(BF16))
            den = jnp.sum(s_mat, axis=1, keepdims=True) + inter * jnp.sum(qh * n_old, axis=1, keepdims=True)
            o_ref[:, h * ML_DV:(h + 1) * ML_DV] = num / jnp.maximum(jnp.abs(den), jnp.exp(-m_t))
            w_log = total - b_col + ig_col
            m_new = jnp.maximum(total + m_old, jnp.max(w_log, axis=0, keepdims=True))
            w = jnp.exp(w_log - m_new)
            decay = jnp.exp(total + m_old - m_new)
            kw = kh * w
            c_ref[st] = decay * c_old + _dot(kw.T.astype(BF16), vh)
            n_ref[st, 0:1, :] = decay * n_old + jnp.sum(kw, axis=0, keepdims=True)
            m_ref[st] = jnp.broadcast_to(m_new, m_ref.shape[1:])


def _mlstm(st, proj, gate_bias):
    B = st.B
    assert st.L % ML_TC == 0 and st.C % ML_TC == 0
    cc = st.C // ML_TC
    nch = st.ltot // ML_TC
    qw = ML_HEADS * ML_DQK
    vw = ML_HEADS * ML_DV
    gcol = (2 * qw + 2 * vw) // 128

    def rb(d, b, k):
        chunk = k if d == 0 else jnp.where(k < cc, cc - 1 - k, nch - 1 - (k - cc))
        return b * nch + chunk

    def specs(d):
        return [pl.BlockSpec((ML_TC, qw), lambda b, k: (rb(d, b, k), 0)),
                pl.BlockSpec((ML_TC, qw), lambda b, k: (rb(d, b, k), 1)),
                pl.BlockSpec((ML_TC, vw), lambda b, k: (rb(d, b, k), (2 * qw) // vw)),
                pl.BlockSpec((ML_TC, 128), lambda b, k: (rb(d, b, k), gcol))]

    nst = 2 * ML_HEADS
    out = jax.ShapeDtypeStruct((st.NT, vw), F32)
    tril = jnp.tril(jnp.ones((ML_TC, ML_TC), BF16))
    return pl.pallas_call(
        _mlstm_kernel,
        grid=(B, nch),
        in_specs=specs(0) + specs(1) + [pl.BlockSpec((1, 128), lambda b, k: (0, 0)),
                                        pl.BlockSpec((ML_TC, ML_TC), lambda b, k: (0, 0)),
                                        pl.BlockSpec((ML_TC, ML_TC), lambda b, k: (0, 0))],
        out_specs=[pl.BlockSpec((ML_TC, vw), lambda b, k: (rb(0, b, k), 0)),
                   pl.BlockSpec((ML_TC, vw), lambda b, k: (rb(1, b, k), 0))],
        out_shape=[out, out],
        scratch_shapes=[pltpu.VMEM((nst, ML_DQK, ML_DV), F32),
                        pltpu.VMEM((nst, 8, ML_DQK), F32),
                        pltpu.VMEM((nst, 8, 128), F32)],
        compiler_params=_cparams(("arbitrary", "arbitrary")),
        name="mlstm_chunks",
    )(proj, proj, proj, proj, proj, proj, proj, proj, gate_bias, tril, tril.T)


def _route(lg):
    lane_i = lax.broadcasted_iota(jnp.int32, lg.shape, 1)
    lane = lane_i.astype(F32)
    neg = -jnp.inf
    gl = jnp.where(lane_i < MOE_GROUPS, lg, neg)
    gmax = jnp.max(gl, axis=-1, keepdims=True)
    gsum = jnp.sum(jnp.where(lane_i < MOE_GROUPS, jnp.exp(lg - gmax), 0.0), axis=-1, keepdims=True)
    p_top = 1.0 / gsum
    g_sel = jnp.min(jnp.where(gl == gmax, lane, 128.0), axis=-1, keepdims=True)
    group_of_lane = (lane_i >> 3).astype(F32) - 1.0
    el = jnp.where(group_of_lane == g_sel, lg, neg)
    e1 = jnp.max(el, axis=-1, keepdims=True)
    i1 = jnp.min(jnp.where(el == e1, lane, 128.0), axis=-1, keepdims=True)
    el2 = jnp.where(lane == i1, neg, el)
    e2 = jnp.max(el2, axis=-1, keepdims=True)
    i2 = jnp.min(jnp.where(el2 == e2, lane, 128.0), axis=-1, keepdims=True)
    t = jnp.exp(e2 - e1)
    w1 = p_top / (1.0 + t)
    w2 = w1 * t
    id1 = i1 - MOE_GROUPS
    id2 = i2 - MOE_GROUPS
    table = jnp.where(lane_i == 0, id1, jnp.where(lane_i == 1, id2,
                                                  jnp.where(lane_i == 2, w1, jnp.where(lane_i == 3, w2, 0.0))))
    chosen = jnp.where(lane == i1, 1.0, 0.0) + jnp.where(lane == i2, 1.0, 0.0)
    return table, jnp.sum(chosen, axis=0, keepdims=True)


def _accumulate_counts(cnt_ref, hist):
    @pl.when(pl.program_id(0) == 0)
    def _():
        cnt_ref[...] = jnp.zeros_like(cnt_ref)
    cnt_ref[...] += hist


def _lhs_mla(o_ref):
    return o_ref[...]


def _lhs_mlstm(hf_ref, hb_ref, og_ref, onorm_ref):
    hs = hf_ref[...] + hb_ref[...]
    og = _sigmoid(og_ref[...])
    parts = []
    for h in range(ML_HEADS):
        sl = slice(h * ML_DV, (h + 1) * ML_DV)
        parts.append(_rms(hs[:, sl]) * onorm_ref[:, sl] * og[:, sl])
    return jnp.concatenate(parts, axis=1)


def _mixer_out_kernel(*refs, n_lhs, lhs_fn):
    lhs_refs = refs[:n_lhs]
    w_ref, x_ref, mod_ref, gain_ref, wr_ref, br_ref, xo_ref, f_ref, r_ref, cnt_ref = refs[n_lhs:]
    y = _dot(lhs_fn(*lhs_refs).astype(BF16), w_ref[...])
    mod = mod_ref[0]
    xn = x_ref[...] + mod[2:3, :] * y
    xo_ref[...] = xn
    f = _norm_mod(xn, gain_ref[...], mod[3:4, :], mod[4:5, :])
    f_ref[...] = f
    r_ref[...], hist = _route(_dot3(f, wr_ref[...]) + br_ref[...])
    _accumulate_counts(cnt_ref, hist)


def _mixer_out(rows, lhs_fn, lhs_args, lhs_specs, w_out, x, mod, gain, wr, br, name):
    n = rows.n * TM
    const = lambda k: (0, 0)
    out = lambda w: pl.BlockSpec((TM, w), lambda k: (k, 0))
    return pl.pallas_call(
        functools.partial(_mixer_out_kernel, n_lhs=len(lhs_args), lhs_fn=lhs_fn),
        grid=(rows.n,),
        in_specs=list(lhs_specs) + [
            pl.BlockSpec((D_MODEL, D_MODEL), const),
            pl.BlockSpec((TM, D_MODEL), lambda k: (rows.src(k), 0)),
            pl.BlockSpec((1, 8, D_MODEL), lambda k: (rows.mod(k), 0, 0)),
            pl.BlockSpec((1, D_MODEL), const),
            pl.BlockSpec((D_MODEL, 128), const),
            pl.BlockSpec((1, 128), const)],
        out_specs=[out(D_MODEL), out(D_MODEL), out(128), pl.BlockSpec((1, 128), const)],
        out_shape=[jax.ShapeDtypeStruct((n, D_MODEL), F32),
                   jax.ShapeDtypeStruct((n, D_MODEL), F32),
                   jax.ShapeDtypeStruct((n, 128), F32),
                   jax.ShapeDtypeStruct((1, 128), F32)],
        compiler_params=_cparams(("arbitrary",)),
        name=name,
    )(*lhs_args, w_out, x, mod, gain, wr, br)


RANK_TILES = 4


def _rank_kernel(r_ref, start_ref, tril_ref, pos_ref, carry_ref):
    @pl.when(pl.program_id(0) == 0)
    def _():
        carry_ref[...] = jnp.zeros_like(carry_ref)

    r = r_ref[...]
    lane = lax.broadcasted_iota(jnp.int32, r.shape, 1).astype(F32)
    chosen = [jnp.where(lane == r[:, k:k + 1] + MOE_GROUPS, 1.0, 0.0) for k in range(MOE_TOPK)]
    both = functools.reduce(jnp.add, chosen)
    before = _dot(tril_ref[...], both.astype(BF16)) + (start_ref[...] + carry_ref[...])
    lane_i = lax.broadcasted_iota(jnp.int32, r.shape, 1)
    pos = jnp.zeros(r.shape, F32)
    for k in range(MOE_TOPK):
        pos = jnp.where(lane_i == k, jnp.sum(chosen[k] * before, axis=-1, keepdims=True), pos)
    pos_t = pos.T[0:8, :].astype(jnp.int32)
    for j in range(RANK_TILES):
        pos_ref[j] = pos_t[:, j * TM:(j + 1) * TM]
    carry_ref[...] += jnp.sum(both, axis=0, keepdims=True)


def _assignment_slots(ntiles, route, starts):
    assert ntiles % RANK_TILES == 0
    rows = RANK_TILES * TM
    tril = jnp.tril(jnp.ones((rows, rows), BF16), -1)
    return pl.pallas_call(
        _rank_kernel,
        grid=(ntiles // RANK_TILES,),
        in_specs=[pl.BlockSpec((rows, 128), lambda i: (i, 0)),
                  pl.BlockSpec((1, 128), lambda i: (0, 0)),
                  pl.BlockSpec((rows, rows), lambda i: (0, 0))],
        out_specs=pl.BlockSpec((RANK_TILES, 8, TM), lambda i: (i, 0, 0)),
        out_shape=jax.ShapeDtypeStruct((ntiles, 8, TM), jnp.int32),
        scratch_shapes=[pltpu.VMEM((1, 128), F32)],
        compiler_params=_cparams(("arbitrary",)),
        name="moe_rank",
    )(route, starts, tril)


def _row_wait(hbm, buf, sem):
    pltpu.make_async_copy(hbm.at[pl.ds(0, TM), :], buf, sem).wait()


def _dispatch_kernel(pos_ref, f_ref, xs_hbm, buf, sem, *, ntiles):
    i = pl.program_id(0)

    def step(s):
        @pl.when(i >= 2)
        def _():
            for _ in range(MOE_TOPK):
                _row_wait(xs_hbm, buf.at[s], sem.at[s])
        buf[s] = f_ref[...]
        for r in range(TM):
            for k in range(MOE_TOPK):
                pltpu.make_async_copy(buf.at[s, pl.ds(r, 1), :], xs_hbm.at[pl.ds(pos_ref[0, k, r], 1), :],
                                      sem.at[s]).start(priority=k % 2)

        @pl.when(i == ntiles - 1)
        def _():
            for slot in ((1 - s, s) if ntiles >= 2 else (s,)):
                for _ in range(MOE_TOPK):
                    _row_wait(xs_hbm, buf.at[slot], sem.at[slot])

    for s in range(2):
        pl.when(i % 2 == s)(functools.partial(step, s))


def _dispatch(ntiles, f, pos):
    n = ntiles * TM
    return pl.pallas_call(
        functools.partial(_dispatch_kernel, ntiles=ntiles),
        grid=(ntiles,),
        in_specs=[pl.BlockSpec((1, 8, TM), lambda i: (i, 0, 0), memory_space=pltpu.SMEM),
                  pl.BlockSpec((TM, D_MODEL), lambda i: (i, 0))],
        out_specs=pl.BlockSpec(memory_space=pl.ANY),
        out_shape=jax.ShapeDtypeStruct((MOE_TOPK * n, D_MODEL), F32),
        scratch_shapes=[pltpu.VMEM((2, TM, D_MODEL), F32), pltpu.SemaphoreType.DMA((2,))],
        compiler_params=_cparams(("arbitrary",)),
        name="moe_dispatch",
    )(pos, f)


def _expert_kernel(vb_ref, ve_ref, lo_ref, hi_ref, first_ref, newexp_ref, eslot_ref, enext_ref, x_ref, wgu_hbm,
                   wd_hbm, y_ref, wgu_f, wd_f, wgu_b, wd_b, sem, *, layer):
    v = pl.program_id(0)

    def fetch(e, s):
        return (pltpu.make_async_copy(wgu_hbm.at[layer, e], wgu_f.at[s], sem.at[0, s]),
                pltpu.make_async_copy(wd_hbm.at[layer, e], wd_f.at[s], sem.at[1, s]))

    @pl.when(newexp_ref[v] == 1)
    def _():
        for s in range(2):
            @pl.when(eslot_ref[v] == s)
            def _():
                @pl.when(v == 0)
                def _():
                    for c in fetch(ve_ref[v], s):
                        c.start()

                @pl.when(enext_ref[v] >= 0)
                def _():
                    for c in fetch(enext_ref[v], 1 - s):
                        c.start()
                for c in fetch(ve_ref[v], s):
                    c.wait()
                wgu_b[...] = wgu_f[s].astype(BF16)
                wd_b[...] = wd_f[s].astype(BF16)

    @pl.when(hi_ref[v] > lo_ref[v])
    def _():
        gu = _dot(x_ref[...].astype(BF16), wgu_b[...])
        gate = gu[:, :MOE_FF]
        act = gate * _sigmoid(gate) * gu[:, MOE_FF:]
        y = _dot(act.astype(BF16), wd_b[...])
        r = lax.broadcasted_iota(jnp.int32, (MOE_BM, 1), 0)
        mine = jnp.logical_and(r >= lo_ref[v], r < hi_ref[v])
        y = jnp.where(mine, y, 0.0)

        @pl.when(first_ref[v] == 1)
        def _():
            y_ref[...] = y

        @pl.when(first_ref[v] == 0)
        def _():
            y_ref[...] += y


def _expert_ffn(xs, visits, layer, w_gate_up, w_down):
    nvis = visits[0].shape[0]
    blk_idx = lambda v, vb, *_: (vb[v], 0)
    grid_spec = pltpu.PrefetchScalarGridSpec(
        num_scalar_prefetch=8,
        grid=(nvis,),
        in_specs=[pl.BlockSpec((MOE_BM, D_MODEL), blk_idx),
                  pl.BlockSpec(memory_space=pl.ANY),
                  pl.BlockSpec(memory_space=pl.ANY)],
        out_specs=pl.BlockSpec((MOE_BM, D_MODEL), blk_idx),
        scratch_shapes=[pltpu.VMEM((2, D_MODEL, 2 * MOE_FF), F32), pltpu.VMEM((2, MOE_FF, D_MODEL), F32),
                        pltpu.VMEM((D_MODEL, 2 * MOE_FF), BF16), pltpu.VMEM((MOE_FF, D_MODEL), BF16),
                        pltpu.SemaphoreType.DMA((2, 2))],
    )
    return pl.pallas_call(
        functools.partial(_expert_kernel, layer=layer),
        grid_spec=grid_spec,
        out_shape=jax.ShapeDtypeStruct(xs.shape, F32),
        compiler_params=_cparams(("arbitrary",)),
        name="moe_expert_ffn",
    )(*visits, xs, w_gate_up, w_down)


def _combine_kernel(pos_ref, nxt_ref, x_ref, r_ref, mod_ref, ys_hbm, o_ref, ybuf, sem, *, ntiles):
    i = pl.program_id(0)

    def gather(table, s):
        for r in range(TM):
            for k in range(MOE_TOPK):
                pltpu.make_async_copy(ys_hbm.at[pl.ds(table[0, k, r], 1), :], ybuf.at[s, k, pl.ds(r, 1), :],
                                      sem.at[s]).start(priority=k % 2)

    @pl.when(i == 0)
    def _():
        gather(pos_ref, 0)

    def step(s):
        @pl.when(i + 1 < ntiles)
        def _():
            gather(nxt_ref, 1 - s)
        for k in range(MOE_TOPK):
            _row_wait(ys_hbm, ybuf.at[s, k], sem.at[s])
        w = r_ref[...]
        y = w[:, MOE_TOPK:MOE_TOPK + 1] * ybuf[s, 0]
        for k in range(1, MOE_TOPK):
            y = y + w[:, MOE_TOPK + k:MOE_TOPK + k + 1] * ybuf[s, k]
        o_ref[...] = x_ref[...] + mod_ref[0][5:6, :] * y

    for s in range(2):
        pl.when(i % 2 == s)(functools.partial(step, s))


def _combine(rows, x, ys, pos, route, mod):
    n = rows.n
    spec = pl.BlockSpec((TM, D_MODEL), lambda i: (i, 0))
    return pl.pallas_call(
        functools.partial(_combine_kernel, ntiles=n),
        grid=(n,),
        in_specs=[pl.BlockSpec((1, 8, TM), lambda i: (i, 0, 0), memory_space=pltpu.SMEM),
                  pl.BlockSpec((1, 8, TM), lambda i: (jnp.minimum(i + 1, n - 1), 0, 0), memory_space=pltpu.SMEM),
                  spec,
                  pl.BlockSpec((TM, 128), lambda i: (i, 0)),
                  pl.BlockSpec((1, 8, D_MODEL), lambda i: (rows.mod(i), 0, 0)),
                  pl.BlockSpec(memory_space=pl.ANY)],
        out_specs=spec,
        out_shape=jax.ShapeDtypeStruct((n * TM, D_MODEL), F32),
        scratch_shapes=[pltpu.VMEM((2, MOE_TOPK, TM, D_MODEL), F32), pltpu.SemaphoreType.DMA((2,))],
        compiler_params=_cparams(("arbitrary",)),
        name="moe_combine",
    )(pos, pos, x, route, mod, ys)


def _visit_tables(bounds, nk):
    E = MOE_EXPERTS
    nblk = nk // MOE_BM
    nvis = nblk + E
    starts, ends = bounds[:-1], bounds[1:]
    fb = starts // MOE_BM
    nv = jnp.where(ends > starts, (ends - 1) // MOE_BM - fb + 1, 0)
    cum = jnp.cumsum(nv)
    total = cum[-1]
    v = jnp.arange(nvis, dtype=jnp.int32)
    active = v < total
    vc = jnp.minimum(v, total - 1)
    ve = jnp.minimum(jnp.sum((cum[None, :] <= vc[:, None]).astype(jnp.int32), axis=1), E - 1)
    vb = fb[ve] + (vc - (cum - nv)[ve])
    lo = jnp.where(active, jnp.maximum(starts[ve], vb * MOE_BM) - vb * MOE_BM, 0)
    hi = jnp.where(active, jnp.minimum(ends[ve], (vb + 1) * MOE_BM) - vb * MOE_BM, 0)
    prev_b = jnp.concatenate([jnp.full((1,), -1, jnp.int32), vb[:-1]])
    first = jnp.logical_and(active, vb != prev_b)
    prev_e = jnp.concatenate([jnp.full((1,), -1, jnp.int32), ve[:-1]])
    new_expert = jnp.logical_and(active, ve != prev_e)
    eslot = (jnp.cumsum(new_expert.astype(jnp.int32)) - 1) % 2
    at = jnp.where(new_expert, v, nvis)
    nxt = jnp.concatenate([lax.cummin(at[::-1])[::-1][1:], jnp.full((1,), nvis, jnp.int32)])
    enext = jnp.where(nxt < nvis, ve[jnp.minimum(nxt, nvis - 1)], -1)
    i32 = lambda a: a.astype(jnp.int32)
    return i32(vb), i32(ve), i32(lo), i32(hi), i32(first), i32(new_expert), i32(eslot), i32(enext)


def _moe(rows, x, f, route, counts, mod, layer, w_gate_up, w_down):
    n = rows.n * TM
    cum = jnp.cumsum(counts[0])
    starts = (cum - counts[0]).reshape(1, 128)
    bounds = jnp.concatenate([starts[0, MOE_GROUPS:MOE_GROUPS + MOE_EXPERTS], cum[-1:]]).astype(jnp.int32)
    pos = _assignment_slots(rows.n, route, starts)
    xs = _dispatch(rows.n, f, pos)
    ys = _expert_ffn(xs, _visit_tables(bounds, n * MOE_TOPK), layer, w_gate_up, w_down)
    return _combine(rows, x, ys, pos, route, mod)


def kernel(x, c, ctx, c_ctx, ada_w, ada_b, norm_mix, norm_ffn, rg_w_in, rg_conv_w, rg_conv_b, rg_gate_w, rg_gate_b, rg_lambda, rg_w_out, mla_w_down, mla_q_norm, mla_kv_norm, mla_w_uq, mla_w_ukv, mla_qk_norm, mla_w_o, ml_w_in, ml_gate_b, ml_out_norm, ml_w_out, moe_w_group, moe_b_group, moe_w_expert, moe_b_expert, moe_w_gate_up, moe_w_down):
    B, L, D = x.shape
    C = ctx.shape[1]
    depth = ada_w.shape[0]
    assert D == D_MODEL
    st = _Stream(B, L, C)

    xs = jnp.concatenate([ctx, x], axis=1).reshape(st.NT, D)
    cc = jnp.zeros((16, D), F32).at[:B].set(c).at[B].set(c_ctx)
    mod_all = _modulation(cc, ada_w, ada_b)
    mod_all = jnp.pad(mod_all[:, :B + 1].reshape(depth, B + 1, 6, D), ((0, 0), (0, 0), (0, 2), (0, 0)))
    perm = _row_permutation(B)

    row = lambda a: a.reshape(1, -1)

    for i in range(depth):
        last = i == depth - 1
        mod = mod_all[i]
        kind, j = i % 3, i // 3
        all_rows = st.all_rows()
        out_rows = st.latent_rows() if last else all_rows
        tile_spec = lambda w: pl.BlockSpec((TM, w), lambda k: (out_rows.src(k), 0))
        wr = jnp.zeros((D, 128), F32).at[:, :MOE_GROUPS].set(moe_w_group[i]) \
            .at[:, MOE_GROUPS:MOE_GROUPS + MOE_EXPERTS].set(moe_w_expert[i])
        br = jnp.zeros((1, 128), F32).at[0, :MOE_GROUPS].set(moe_b_group[i]) \
            .at[0, MOE_GROUPS:MOE_GROUPS + MOE_EXPERTS].set(moe_b_expert[i])
        out_args = (xs, mod, row(norm_ffn[i]), wr, br)

        if kind == 0:
            gate, u = _rg_in(st, xs, mod, row(norm_mix[i]), perm, rg_w_in[j].astype(BF16))
            wg, gb = _rg_gate_weights(rg_gate_w[j], rg_gate_b[j])
            hs = _rg_scan(st, u, rg_conv_w[j], row(rg_conv_b[j]), wg, gb, rg_lambda[j].reshape(2, 1, D))
            xs, f, route, counts = _rg_out(st, last, gate, hs, perm.T, rg_w_out[j].astype(BF16), *out_args)
        elif kind == 1:
            w_down = jnp.pad(mla_w_down[j], ((0, 0), (0, 512 - mla_w_down.shape[2]))).astype(BF16)
            down = _norm_proj(all_rows, xs, mod, row(norm_mix[i]), w_down, name="mla_down_proj")
            wq, wk, we, wv = _mla_weights(mla_w_uq[j], mla_w_ukv[j])
            q, k, v = _mla_up(st, all_rows, down, row(mla_q_norm[j]), row(mla_kv_norm[j]), wq, wk, we, wv,
                              _rope_tables(L, mla_qk_norm[j]))
            o = _attention(st, q, k, v)
            xs, f, route, counts = _mixer_out(out_rows, _lhs_mla, (o,), (tile_spec(D),),
                                      mla_w_o[j].astype(BF16), *out_args, name="mla_out")
        else:
            n_in = ml_w_in.shape[2]
            w_in = jnp.pad(ml_w_in[j], ((0, 0), (0, ML_NP - n_in))).astype(BF16)
            proj = _norm_proj(all_rows, xs, mod, row(norm_mix[i]), w_in, name="mlstm_in_proj")
            gate_bias = jnp.pad(ml_gate_b[j].reshape(1, -1), ((0, 0), (0, 128 - 4 * ML_HEADS)))
            hf, hb = _mlstm(st, proj, gate_bias)
            og_spec = pl.BlockSpec((TM, D), lambda k: (out_rows.src(k), 2))
            xs, f, route, counts = _mixer_out(out_rows, _lhs_mlstm, (hf, hb, proj, row(ml_out_norm[j])),
                                      (tile_spec(D), tile_spec(D), og_spec, pl.BlockSpec((1, D), lambda k: (0, 0))),
                                      ml_w_out[j].astype(BF16), *out_args, name="mlstm_out")

        moe_rows = st.dense_latent_rows() if last else all_rows
        xs = _moe(moe_rows, xs, f, route, counts, mod, i, moe_w_gate_up, moe_w_down)

    return xs.reshape(B, L, D)
```

```python
import functools
import math

import jax
import jax.numpy as jnp
from jax import lax
from jax.experimental import pallas as pl
from jax.experimental.pallas import tpu as pltpu

F32 = jnp.float32
BF16 = jnp.bfloat16

D_MODEL = 1024
RMS_EPS = 1e-6

TM = 256
VMEM_LIMIT = 48 * 1024 * 1024

RG_BLOCK_W = 64
RG_CHUNK = 256
RG_CONV_W = 4
RG_C = 8.0
RG_TT = 64

MLA_HEADS = 16
MLA_Q_RANK = 256
MLA_KV_RANK = 128
MLA_NOPE = 64
MLA_ROPE = 32
MLA_V = 64
MLA_QK = MLA_NOPE + MLA_ROPE
MLA_HP = 128
ROPE_AXIS_DIM = MLA_ROPE // 2
ROPE_BASE = 10000.0
GRID_W = 64
ATT_HEADS = 8

ML_HEADS = 4
ML_DV = 256
ML_DQK = 128
ML_TC = 256
ML_M_INIT = -1e30
ML_NP = 3200

MOE_GROUPS = 8
MOE_PER_GROUP = 8
MOE_EXPERTS = 64
MOE_TOPK = 2
MOE_FF = 256
MOE_BM = 512


def _cparams(sem):
    return pltpu.CompilerParams(dimension_semantics=sem, vmem_limit_bytes=VMEM_LIMIT)


def _dot(a, b):
    return jnp.dot(a, b, preferred_element_type=F32)


def _dot_t(a, b):
    return lax.dot_general(a, b, (((1,), (1,)), ((), ())), preferred_element_type=F32)


def _dot3(a, b):
    ah = a.astype(BF16)
    al = (a - ah.astype(F32)).astype(BF16)
    bh = b.astype(BF16)
    bl = (b - bh.astype(F32)).astype(BF16)
    return _dot(ah, bh) + (_dot(al, bh) + _dot(ah, bl))


def _sigmoid(x):
    return 0.5 * jnp.tanh(0.5 * x) + 0.5


def _softplus(x):
    return jnp.maximum(x, 0.0) + jnp.log1p(jnp.exp(-jnp.abs(x)))


def _gelu_tanh(x):
    return 0.5 * x * (1.0 + jnp.tanh(0.7978845608028654 * (x + 0.044715 * (x * x * x))))


def _rms(x, n=None):
    n = x.shape[-1] if n is None else n
    ms = jnp.sum(x * x, axis=-1, keepdims=True) * (1.0 / n)
    return x * lax.rsqrt(ms + RMS_EPS)


class _Rows:
    def __init__(self, n, src, mod):
        self.n, self.src, self.mod = n, src, mod


class _Stream:
    def __init__(self, B, L, C):
        assert L % TM == 0 and C % TM == 0 and B % 8 == 0
        self.B, self.L, self.C = B, L, C
        self.ltot = L + C
        self.lt, self.ct = L // TM, C // TM
        self.tpb = self.lt + self.ct
        self.NT = B * self.ltot

    def all_rows(self):
        tpb, ct, B = self.tpb, self.ct, self.B
        return _Rows(B * tpb, lambda k: k, lambda k: jnp.where(k % tpb < ct, B, k // tpb))

    def latent_rows(self):
        tpb, ct, lt = self.tpb, self.ct, self.lt
        return _Rows(self.B * lt, lambda k: (k // lt) * tpb + ct + k % lt, lambda k: k // lt)

    def dense_latent_rows(self):
        lt = self.lt
        return _Rows(self.B * lt, lambda k: k, lambda k: k // lt)


def _mod_kernel(c_ref, w_ref, b_ref, o_ref):
    c = c_ref[...]
    o_ref[0] = _dot3(c * _sigmoid(c), w_ref[0]) + b_ref[0]


def _modulation(cc, ada_w, ada_b):
    depth, d, n = ada_w.shape
    tn = 1536
    return pl.pallas_call(
        _mod_kernel,
        grid=(depth, n // tn),
        in_specs=[pl.BlockSpec((16, d), lambda l, j: (0, 0)),
                  pl.BlockSpec((1, d, tn), lambda l, j: (l, 0, j)),
                  pl.BlockSpec((1, 1, tn), lambda l, j: (l, 0, j))],
        out_specs=pl.BlockSpec((1, 16, tn), lambda l, j: (l, 0, j)),
        out_shape=jax.ShapeDtypeStruct((depth, 16, n), F32),
        compiler_params=_cparams(("arbitrary", "arbitrary")),
        name="ada_modulation",
    )(cc, ada_w, ada_b.reshape(depth, 1, n))


def _norm_mod(x, gain, shift, scale):
    return _rms(x) * gain * (1.0 + scale) + shift


def _norm_proj_kernel(x_ref, mod_ref, g_ref, w_ref, o_ref):
    mod = mod_ref[0]
    h = _norm_mod(x_ref[...], g_ref[...], mod[0:1, :], mod[1:2, :]).astype(BF16)
    o_ref[...] = _dot(h, w_ref[...]).astype(o_ref.dtype)


def _norm_proj(rows, x, mod, gain, w, name):
    n = w.shape[1]
    const = lambda k: (0, 0)
    return pl.pallas_call(
        _norm_proj_kernel,
        grid=(rows.n,),
        in_specs=[pl.BlockSpec((TM, D_MODEL), lambda k: (rows.src(k), 0)),
                  pl.BlockSpec((1, 8, D_MODEL), lambda k: (rows.mod(k), 0, 0)),
                  pl.BlockSpec((1, D_MODEL), const),
                  pl.BlockSpec((D_MODEL, n), const)],
        out_specs=pl.BlockSpec((TM, n), lambda k: (k, 0)),
        out_shape=jax.ShapeDtypeStruct((rows.n * TM, n), F32),
        compiler_params=_cparams(("arbitrary",)),
        name=name,
    )(x, mod, gain, w)


RG_TR = 32


def _row_permutation(B):
    n = B * RG_TR
    r = jnp.arange(n)
    src = (r % B) * RG_TR + r // B
    return (src[:, None] == jnp.arange(n)[None, :]).astype(BF16)


def _time_tile_mod(mod_ref, is_ctx, B, row):
    return jnp.where(is_ctx, mod_ref[B:B + 1, row:row + 1, :], mod_ref[0:B, row:row + 1, :])


def _rg_in_kernel(x_ref, mod_ref, g_ref, perm_ref, wg_ref, wu_ref, gate_ref, u_ref, *, B, ctx_tiles):
    is_ctx = pl.program_id(0) < ctx_tiles
    h = _norm_mod(x_ref[...], g_ref[...], _time_tile_mod(mod_ref, is_ctx, B, 0), _time_tile_mod(mod_ref, is_ctx, B, 1))
    h = h.reshape(B * RG_TR, D_MODEL).astype(BF16)
    h = _dot(perm_ref[...], h).astype(BF16)
    gate_ref[...] = _gelu_tanh(_dot(h, wg_ref[...])).astype(gate_ref.dtype)
    u_ref[...] = _dot(h, wu_ref[...])


def _rg_in(st, x, mod, gain, perm, w_in):
    B, W = st.B, D_MODEL
    R = B * RG_TR
    nt = st.ltot // RG_TR
    const = lambda t: (0, 0)
    return pl.pallas_call(
        functools.partial(_rg_in_kernel, B=B, ctx_tiles=st.C // RG_TR),
        grid=(nt,),
        in_specs=[pl.BlockSpec((B, RG_TR, D_MODEL), lambda t: (0, t, 0)),
                  pl.BlockSpec((B + 1, 8, D_MODEL), lambda t: (0, 0, 0)),
                  pl.BlockSpec((1, D_MODEL), const),
                  pl.BlockSpec((R, R), const),
                  pl.BlockSpec((D_MODEL, W), const),
                  pl.BlockSpec((D_MODEL, W), lambda t: (0, 1))],
        out_specs=[pl.BlockSpec((R, W), lambda t: (t, 0)),
                   pl.BlockSpec((R, W), lambda t: (t, 0))],
        out_shape=[jax.ShapeDtypeStruct((st.ltot * B, W), BF16), jax.ShapeDtypeStruct((st.ltot * B, W), F32)],
        compiler_params=_cparams(("arbitrary",)),
        name="rg_in_proj",
    )(x.reshape(B, st.ltot, D_MODEL), mod, gain, perm, w_in, w_in)


def _rg_tile_order(d, k, ct, ntt):
    bwd = jnp.where(k < ct, ct - 1 - k, ntt - 1 - (k - ct))
    return jnp.where(d == 0, k, bwd)


def _rg_scan_kernel(um_ref, up_ref, un_ref, cw_ref, cb_ref, wg_ref, gb_ref, lam_ref, o_ref,
                    ext_ref, a_ref, b_ref, h_ref, *, B, ct, ntt):
    d = pl.program_id(0)
    k = pl.program_id(1)
    tile = _rg_tile_order(d, k, ct, ntt)
    R = RG_TT * B

    @pl.when(k == 0)
    def _():
        h_ref[...] = jnp.zeros_like(h_ref)

    seq_start = jnp.logical_or(tile == 0, tile == ct)
    seq_end = jnp.logical_or(tile == ct - 1, tile == ntt - 1)
    ext_ref[0:2 * B, :] = jnp.where(seq_start, 0.0, up_ref[...])
    ext_ref[2 * B:2 * B + R, :] = um_ref[...]
    ext_ref[2 * B + R:3 * B + R, :] = jnp.where(seq_end, 0.0, un_ref[...])
    cw = cw_ref[...]
    uc = cb_ref[...] + cw[0:1, :] * ext_ref[0:R, :]
    for j in range(1, RG_CONV_W):
        uc = uc + cw[j:j + 1, :] * ext_ref[j * B:j * B + R, :]
    ucb = uc.astype(BF16)
    c_lam = -RG_C * _softplus(-lam_ref[0])
    for c in range(D_MODEL // RG_CHUNK):
        sl = slice(c * RG_CHUNK, (c + 1) * RG_CHUNK)
        z = _dot(ucb[:, sl], wg_ref[0, c]) + gb_ref[0, c]
        r = _sigmoid(z[:, :RG_CHUNK])
        ig = _sigmoid(z[:, RG_CHUNK:])
        log_a = c_lam[:, sl] * r
        a = jnp.exp(log_a)
        one_minus_a2 = -jnp.tanh(log_a) * (a * a + 1.0)
        a_ref[:, sl] = a
        b_ref[:, sl] = jnp.sqrt(one_minus_a2) * (ig * uc[:, sl])

    def scan(times):
        for c in range(D_MODEL // 128):
            cs = slice(c * 128, (c + 1) * 128)
            h = h_ref[:, cs]
            for t in times:
                rs = slice(t * B, (t + 1) * B)
                h = a_ref[rs, cs] * h + b_ref[rs, cs]
                b_ref[rs, cs] = h
            h_ref[:, cs] = h

    pl.when(d == 0)(lambda: scan(range(RG_TT)))
    pl.when(d == 1)(lambda: scan(range(RG_TT - 1, -1, -1)))
    o_ref[0] = b_ref[...].astype(o_ref.dtype)


def _rg_scan(st, u_tm, conv_w, conv_b, wg, gb, lam):
    B, W = st.B, D_MODEL
    assert st.C % RG_TT == 0 and st.L % RG_TT == 0
    ltot = st.ltot
    ntt, ct = ltot // RG_TT, st.C // RG_TT
    R = RG_TT * B
    order = functools.partial(_rg_tile_order, ct=ct, ntt=ntt)
    nch = W // RG_CHUNK
    return pl.pallas_call(
        functools.partial(_rg_scan_kernel, B=B, ct=ct, ntt=ntt),
        grid=(2, ntt),
        in_specs=[pl.BlockSpec((R, W), lambda d, k: (order(d, k), 0)),
                  pl.BlockSpec((2 * B, W), lambda d, k: (jnp.maximum(order(d, k) * (RG_TT // 2) - 1, 0), 0)),
                  pl.BlockSpec((B, W), lambda d, k: (jnp.minimum((order(d, k) + 1) * RG_TT, ltot - 1), 0)),
                  pl.BlockSpec((RG_CONV_W, W), lambda d, k: (0, 0)),
                  pl.BlockSpec((1, W), lambda d, k: (0, 0)),
                  pl.BlockSpec((1, nch, RG_CHUNK, 2 * RG_CHUNK), lambda d, k: (d, 0, 0, 0)),
                  pl.BlockSpec((1, nch, 1, 2 * RG_CHUNK), lambda d, k: (d, 0, 0, 0)),
                  pl.BlockSpec((1, 1, W), lambda d, k: (d, 0, 0))],
        out_specs=pl.BlockSpec((1, R, W), lambda d, k: (d, order(d, k), 0)),
        out_shape=jax.ShapeDtypeStruct((2, ltot * B, W), BF16),
        scratch_shapes=[pltpu.VMEM((3 * B + R, W), F32),
                        pltpu.VMEM((R, W), F32),
                        pltpu.VMEM((R, W), F32),
                        pltpu.VMEM((B, W), F32)],
        compiler_params=_cparams(("arbitrary", "arbitrary")),
        name="rg_scan",
    )(u_tm, u_tm, u_tm, conv_w, conv_b, wg, gb, lam)


def _rg_gate_weights(gate_w, gate_b):
    nb = gate_w.shape[2]
    per = RG_CHUNK // RG_BLOCK_W
    nch = nb // per
    gw = gate_w.reshape(2, 2, nch, per, RG_BLOCK_W, RG_BLOCK_W)
    eye = jnp.eye(per, dtype=gate_w.dtype)
    bd = jnp.einsum('dgcnij,nm->dgcnimj', gw, eye).reshape(2, 2, nch, RG_CHUNK, RG_CHUNK)
    wg = jnp.concatenate([bd[:, 0], bd[:, 1]], axis=-1).astype(BF16)
    gb = gate_b.reshape(2, 2, nch, 1, RG_CHUNK)
    gb = jnp.concatenate([gb[:, 0], gb[:, 1]], axis=-1)
    return wg, gb


def _rg_out_kernel(g_ref, hf_ref, hb_ref, perm_ref, w_ref, x_ref, mod_ref, gain_ref, wr_ref, br_ref,
                   xo_ref, f_ref, r_ref, cnt_ref, *, B, ctx_tiles, t0):
    is_ctx = pl.program_id(0) + t0 < ctx_tiles
    hsum = hf_ref[0].astype(F32) + hb_ref[0].astype(F32)
    lhs = (g_ref[...].astype(F32) * hsum).astype(BF16)
    lhs = _dot(perm_ref[...], lhs).astype(BF16)
    y = _dot(lhs, w_ref[...]).reshape(B, RG_TR, D_MODEL)
    m = lambda row: _time_tile_mod(mod_ref, is_ctx, B, row)
    xn = x_ref[...] + m(2) * y
    xo_ref[...] = xn
    f = _norm_mod(xn, gain_ref[...], m(3), m(4))
    f_ref[...] = f
    lg = _dot3(f.reshape(B * RG_TR, D_MODEL), wr_ref[...]) + br_ref[...]
    table, hist = _route(lg)
    r_ref[...] = table.reshape(B, RG_TR, 128)
    _accumulate_counts(cnt_ref, hist)


def _rg_out(st, latent_only, gate, hs, perm, w_out, x, mod, gain, wr, br):
    B = st.B
    R = B * RG_TR
    t0 = st.C // RG_TR if latent_only else 0
    lo = st.L if latent_only else st.ltot
    nt = lo // RG_TR
    const = lambda t: (0, 0)
    blk = lambda w: pl.BlockSpec((B, RG_TR, w), lambda t: (0, t, 0))
    xo, f, route, counts = pl.pallas_call(
        functools.partial(_rg_out_kernel, B=B, ctx_tiles=st.C // RG_TR, t0=t0),
        grid=(nt,),
        in_specs=[pl.BlockSpec((R, D_MODEL), lambda t: (t + t0, 0)),
                  pl.BlockSpec((1, R, D_MODEL), lambda t: (0, t + t0, 0)),
                  pl.BlockSpec((1, R, D_MODEL), lambda t: (1, t + t0, 0)),
                  pl.BlockSpec((R, R), const),
                  pl.BlockSpec((D_MODEL, D_MODEL), const),
                  pl.BlockSpec((B, RG_TR, D_MODEL), lambda t: (0, t + t0, 0)),
                  pl.BlockSpec((B + 1, 8, D_MODEL), lambda t: (0, 0, 0)),
                  pl.BlockSpec((1, D_MODEL), const),
                  pl.BlockSpec((D_MODEL, 128), const),
                  pl.BlockSpec((1, 128), const)],
        out_specs=[blk(D_MODEL), blk(D_MODEL), blk(128), pl.BlockSpec((1, 128), const)],
        out_shape=[jax.ShapeDtypeStruct((B, lo, D_MODEL), F32),
                   jax.ShapeDtypeStruct((B, lo, D_MODEL), F32),
                   jax.ShapeDtypeStruct((B, lo, 128), F32),
                   jax.ShapeDtypeStruct((1, 128), F32)],
        compiler_params=_cparams(("arbitrary",)),
        name="rg_out",
    )(gate, hs, hs, perm, w_out, x.reshape(B, st.ltot, D_MODEL), mod, gain, wr, br)
    return xo.reshape(B * lo, D_MODEL), f.reshape(B * lo, D_MODEL), route.reshape(B * lo, 128), counts


_ROPE_HALF = ROPE_AXIS_DIM // 2
_MLA_SRC_DIM = (list(range(MLA_NOPE + ROPE_AXIS_DIM)) + list(range(MLA_NOPE, MLA_NOPE + _ROPE_HALF))
                + list(range(MLA_NOPE + ROPE_AXIS_DIM, MLA_QK))
                + list(range(MLA_NOPE + ROPE_AXIS_DIM, MLA_NOPE + ROPE_AXIS_DIM + _ROPE_HALF)))
_MLA_REAL_LANE = ([1.0] * (MLA_NOPE + ROPE_AXIS_DIM) + [0.0] * _ROPE_HALF + [1.0] * ROPE_AXIS_DIM
                  + [0.0] * _ROPE_HALF + [0.0] * (MLA_HP - len(_MLA_SRC_DIM)))


def _mla_up_kernel(dn_ref, qn_ref, kvn_ref, wq_ref, wk_ref, we_ref, wv_ref, real_ref, cq_ref, sq_ref, ck_ref, sk_ref,
                   q_ref, k_ref, v_ref):
    dn = dn_ref[...]
    cq = _rms(dn[:, :MLA_Q_RANK]) * qn_ref[...]
    ckv = _rms(dn[:, MLA_Q_RANK:MLA_Q_RANK + MLA_KV_RANK]) * kvn_ref[...]
    kr = dn[:, MLA_Q_RANK + MLA_KV_RANK:]
    kr_hi = kr.astype(BF16)
    kr_lo = (kr - kr_hi.astype(F32)).astype(BF16)
    ckvb = ckv.astype(BF16)
    q_pre = _dot(cq.astype(BF16), wq_ref[...])
    k_pre = _dot(ckvb, wk_ref[...]) + (_dot(kr_hi, we_ref[...]) + _dot(kr_lo, we_ref[...]))
    v_ref[...] = _dot(ckvb, wv_ref[...]).astype(v_ref.dtype)
    real = real_ref[...]

    def head(x, cos_g, sin_g):
        ms = jnp.sum(x * x * real, axis=-1, keepdims=True) * (1.0 / MLA_QK)
        xr = x * lax.rsqrt(ms + RMS_EPS)
        return xr * cos_g + pltpu.roll(xr, MLA_HP - _ROPE_HALF, 1) * sin_g

    cos_q, sin_q, cos_k, sin_k = cq_ref[...], sq_ref[...], ck_ref[...], sk_ref[...]
    for h in range(MLA_HEADS):
        sl = slice(h * MLA_HP, (h + 1) * MLA_HP)
        q_ref[:, sl] = head(q_pre[:, sl], cos_q, sin_q).astype(q_ref.dtype)
        k_ref[:, sl] = head(k_pre[:, sl], cos_k, sin_k).astype(k_ref.dtype)


def _mla_up(st, rows, down, q_norm, kv_norm, wq, wk, we, wv, tables):
    hw = MLA_HEADS * MLA_HP
    const = lambda i: (0, 0)
    tpb, ct, lt = st.tpb, st.ct, st.lt
    rope_idx = lambda i: (jnp.where(i % tpb < ct, lt, i % tpb - ct), 0)
    real = jnp.asarray(_MLA_REAL_LANE, F32).reshape(1, MLA_HP)
    return pl.pallas_call(
        _mla_up_kernel,
        grid=(rows.n,),
        in_specs=[pl.BlockSpec((TM, 512), lambda i: (i, 0)),
                  pl.BlockSpec((1, MLA_Q_RANK), const),
                  pl.BlockSpec((1, MLA_KV_RANK), const),
                  pl.BlockSpec((MLA_Q_RANK, hw), const),
                  pl.BlockSpec((MLA_KV_RANK, hw), const),
                  pl.BlockSpec((128, hw), const),
                  pl.BlockSpec((MLA_KV_RANK, MLA_HEADS * MLA_V), const),
                  pl.BlockSpec((1, MLA_HP), const)] + [pl.BlockSpec((TM, MLA_HP), rope_idx)] * 4,
        out_specs=[pl.BlockSpec((TM, hw), lambda i: (i, 0)),
                   pl.BlockSpec((TM, hw), lambda i: (i, 0)),
                   pl.BlockSpec((TM, MLA_HEADS * MLA_V), lambda i: (i, 0))],
        out_shape=[jax.ShapeDtypeStruct((st.NT, hw), BF16),
                   jax.ShapeDtypeStruct((st.NT, hw), BF16),
                   jax.ShapeDtypeStruct((st.NT, MLA_HEADS * MLA_V), BF16)],
        compiler_params=_cparams(("arbitrary",)),
        name="mla_up_proj",
    )(down, q_norm, kv_norm, wq, wk, we, wv, real, *tables)


def _attn_kernel(q_ref, k_ref, v_ref, o_ref, vaug_ref, *, C, ct):
    qi = pl.program_id(2)

    @pl.when(qi == 0)
    def _():
        lane = lax.broadcasted_iota(jnp.int32, (k_ref.shape[0], 2 * MLA_V), 1)
        for hh in range(ATT_HEADS):
            pair = v_ref[:, (hh // 2) * 2 * MLA_V:(hh // 2 + 1) * 2 * MLA_V].astype(F32)
            if hh % 2 == 0:
                aug = jnp.where(lane < MLA_V, pair, jnp.where(lane == MLA_V, 1.0, 0.0))
            else:
                aug = jnp.where(lane >= MLA_V, pair, jnp.where(lane == 0, 1.0, 0.0))
            vaug_ref[hh] = aug.astype(BF16)

    def attend(nkeys):
        lane = lax.broadcasted_iota(jnp.int32, (TM, 2 * MLA_V), 1)
        for pair in range(ATT_HEADS // 2):
            outs = []
            for hh in (2 * pair, 2 * pair + 1):
                sl = slice(hh * MLA_HP, (hh + 1) * MLA_HP)
                s = _dot_t(q_ref[:, sl], k_ref[0:nkeys, sl])
                p = jnp.exp2(s - jnp.max(s, axis=-1, keepdims=True)).astype(BF16)
                o = _dot(p, vaug_ref[hh, 0:nkeys, :])
                rowsum = o[:, MLA_V:MLA_V + 1] if hh % 2 == 0 else o[:, 0:1]
                outs.append(o * (1.0 / rowsum))
            o_ref[:, pair * 2 * MLA_V:(pair + 1) * 2 * MLA_V] = jnp.where(lane < MLA_V, outs[0], outs[1]).astype(o_ref.dtype)

    pl.when(qi < ct)(lambda: attend(C))
    pl.when(qi >= ct)(lambda: attend(k_ref.shape[0]))


def _attention(st, q, k, v):
    B, ltot, tpb = st.B, st.ltot, st.tpb
    hg = MLA_HEADS // ATT_HEADS
    return pl.pallas_call(
        functools.partial(_attn_kernel, C=st.C, ct=st.ct),
        grid=(B, hg, tpb),
        in_specs=[pl.BlockSpec((TM, ATT_HEADS * MLA_HP), lambda b, h, i: (b * tpb + i, h)),
                  pl.BlockSpec((ltot, ATT_HEADS * MLA_HP), lambda b, h, i: (b, h)),
                  pl.BlockSpec((ltot, ATT_HEADS * MLA_V), lambda b, h, i: (b, h))],
        out_specs=pl.BlockSpec((TM, ATT_HEADS * MLA_V), lambda b, h, i: (b * tpb + i, h)),
        out_shape=jax.ShapeDtypeStruct((st.NT, MLA_HEADS * MLA_V), BF16),
        scratch_shapes=[pltpu.VMEM((ATT_HEADS, ltot, 2 * MLA_V), BF16)],
        compiler_params=_cparams(("arbitrary", "arbitrary", "arbitrary")),
        name="mla_attention",
    )(q, k, v)


def _mla_weights(w_uq, w_ukv):
    H = MLA_HEADS
    src = jnp.asarray(_MLA_SRC_DIM, jnp.int32)
    pad = MLA_HP - len(_MLA_SRC_DIM)
    wq = jnp.pad(w_uq.reshape(MLA_Q_RANK, H, MLA_QK)[:, :, src], ((0, 0), (0, 0), (0, pad)))
    wkv = w_ukv.reshape(MLA_KV_RANK, H, MLA_NOPE + MLA_V)
    wk = jnp.pad(wkv[:, :, :MLA_NOPE], ((0, 0), (0, 0), (0, MLA_HP - MLA_NOPE)))
    wv = wkv[:, :, MLA_NOPE:]
    r = jnp.arange(128)[:, None]
    lane_dim = jnp.pad(src, (0, pad), constant_values=-1)[None, :]
    place = (lane_dim == r + MLA_NOPE).astype(BF16)
    we = jnp.tile(place, (1, H))
    return (wq.reshape(MLA_Q_RANK, H * MLA_HP).astype(BF16), wk.reshape(MLA_KV_RANK, H * MLA_HP).astype(BF16),
            we, wv.reshape(MLA_KV_RANK, H * MLA_V).astype(BF16))


def _rope_tables(L, qk_norm):
    rows = L // GRID_W
    row = jnp.broadcast_to(jnp.arange(rows, dtype=F32)[:, None], (rows, GRID_W)).reshape(L)
    col = jnp.broadcast_to(jnp.arange(GRID_W, dtype=F32)[None, :], (rows, GRID_W)).reshape(L)
    inv_freq = ROPE_BASE ** (-jnp.arange(0, ROPE_AXIS_DIM, 2, dtype=F32) / ROPE_AXIS_DIM)
    ar = row[:, None] * inv_freq
    ac = col[:, None] * inv_freq
    h8 = _ROPE_HALF
    one = jnp.ones((L, MLA_NOPE), F32)
    z8 = jnp.zeros((L, h8), F32)
    zpad = jnp.zeros((L, MLA_HP - len(_MLA_SRC_DIM)), F32)
    cos_t = jnp.concatenate([one, jnp.cos(ar), jnp.cos(ar), z8, jnp.cos(ac), jnp.cos(ac), z8, zpad], axis=1)
    sin_t = jnp.concatenate([0 * one, -jnp.sin(ar), jnp.sin(ar), z8, -jnp.sin(ac), jnp.sin(ac), z8, zpad], axis=1)
    ident = jnp.asarray(_MLA_REAL_LANE, F32)[None, :]
    cos_t = jnp.concatenate([cos_t, jnp.broadcast_to(ident, (TM, MLA_HP))], axis=0)
    sin_t = jnp.concatenate([sin_t, jnp.zeros((TM, MLA_HP), F32)], axis=0)
    src = jnp.asarray(_MLA_SRC_DIM, jnp.int32)
    pad = MLA_HP - len(_MLA_SRC_DIM)
    scale = MLA_QK ** -0.5 * math.log2(math.e)
    tables = []
    for g, s in ((qk_norm[0], scale), (qk_norm[1], 1.0)):
        g_lane = jnp.pad(g[src], (0, pad))
        g_partner = jnp.roll(g_lane, -h8)
        tables += [cos_t * (g_lane * s)[None, :], sin_t * (g_partner * s)[None, :]]
    return tables


def _log_sigmoid(x):
    return jnp.minimum(x, 0.0) - jnp.log1p(jnp.exp(-jnp.abs(x)))


def _mlstm_kernel(qf_ref, kf_ref, vf_ref, gf_ref, qb_ref, kb_ref, vb_ref, gb_ref, bias_ref, tril_ref, triu_ref,
                  of_ref, ob_ref, c_ref, n_ref, m_ref):
    T = ML_TC

    @pl.when(pl.program_id(1) == 0)
    def _():
        c_ref[...] = jnp.zeros_like(c_ref)
        n_ref[...] = jnp.zeros_like(n_ref)
        m_ref[...] = jnp.full(m_ref.shape, ML_M_INIT, F32)

    ti = lax.broadcasted_iota(jnp.int32, (T, T), 0)
    si = lax.broadcasted_iota(jnp.int32, (T, T), 1)
    dirs = ((qf_ref, kf_ref, vf_ref, gf_ref, of_ref, tril_ref), (qb_ref, kb_ref, vb_ref, gb_ref, ob_ref, triu_ref))
    for d, (q_ref, k_ref, v_ref, g_ref, o_ref, tri_ref) in enumerate(dirs):
        tri = (si <= ti) if d == 0 else (si >= ti)
        g = g_ref[...] + bias_ref[...]
        g_t = g.T
        lsg = _log_sigmoid(g)
        lsg_hi = lsg.astype(BF16)
        lsg_lo = (lsg - lsg_hi.astype(F32)).astype(BF16)
        cum = _dot(tri_ref[...], lsg_hi) + _dot(tri_ref[...], lsg_lo)
        cum_t = cum.T
        last = T - 1 if d == 0 else 0
        for h in range(ML_HEADS):
            st = d * ML_HEADS + h
            li, lf_ = (2 * d) * ML_HEADS + h, (2 * d + 1) * ML_HEADS + h
            ig_col = g[:, li:li + 1]
            ig_row = g_t[li:li + 1, :]
            b_col = cum[:, lf_:lf_ + 1]
            b_row = cum_t[lf_:lf_ + 1, :]
            total = cum[last:last + 1, lf_:lf_ + 1]
            m_old = m_ref[st, 0:1, 0:1]
            d_log = jnp.where(tri, b_col - b_row + ig_row, -jnp.inf)
            inter_log = b_col + m_old
            m_t = jnp.maximum(inter_log, jnp.max(d_log, axis=1, keepdims=True))
            qh = q_ref[:, h * ML_DQK:(h + 1) * ML_DQK] * (ML_DQK ** -0.5)
            kh = k_ref[:, h * ML_DQK:(h + 1) * ML_DQK]
            vh = v_ref[:, h * ML_DV:(h + 1) * ML_DV].astype(BF16)
            qb16 = qh.astype(BF16)
            s_mat = _dot_t(qb16, kh.astype(BF16)) * jnp.exp(d_log - m_t)
            inter = jnp.exp(inter_log - m_t)
            c_old = c_ref[st]
            n_old = n_ref[st, 0:1, :]
            num = _dot(s_mat.astype(BF16), vh) + inter * _dot(qb16, c_old.astype(BF16))
            den = jnp.sum(s_mat, axis=1, keepdims=True) + inter * jnp.sum(qh * n_old, axis=1, keepdims=True)
            o_ref[:, h * ML_DV:(h + 1) * ML_DV] = num / jnp.maximum(jnp.abs(den), jnp.exp(-m_t))
            w_log = total - b_col + ig_col
            m_new = jnp.maximum(total + m_old, jnp.max(w_log, axis=0, keepdims=True))
            w = jnp.exp(w_log - m_new)
            decay = jnp.exp(total + m_old - m_new)
            kw = kh * w
            c_ref[st] = decay * c_old + _dot(kw.T.astype(BF16), vh)
            n_ref[st, 0:1, :] = decay * n_old + jnp.sum(kw, axis=0, keepdims=True)
            m_ref[st] = jnp.broadcast_to(m_new, m_ref.shape[1:])


def _mlstm(st, proj, gate_bias):
    B = st.B
    assert st.L % ML_TC == 0 and st.C % ML_TC == 0
    cc = st.C // ML_TC
    nch = st.ltot // ML_TC
    qw = ML_HEADS * ML_DQK
    vw = ML_HEADS * ML_DV
    gcol = (2 * qw + 2 * vw) // 128

    def rb(d, b, k):
        chunk = k if d == 0 else jnp.where(k < cc, cc - 1 - k, nch - 1 - (k - cc))
        return b * nch + chunk

    def specs(d):
        return [pl.BlockSpec((ML_TC, qw), lambda b, k: (rb(d, b, k), 0)),
                pl.BlockSpec((ML_TC, qw), lambda b, k: (rb(d, b, k), 1)),
                pl.BlockSpec((ML_TC, vw), lambda b, k: (rb(d, b, k), (2 * qw) // vw)),
                pl.BlockSpec((ML_TC, 128), lambda b, k: (rb(d, b, k), gcol))]

    nst = 2 * ML_HEADS
    out = jax.ShapeDtypeStruct((st.NT, vw), F32)
    tril = jnp.tril(jnp.ones((ML_TC, ML_TC), BF16))
    return pl.pallas_call(
        _mlstm_kernel,
        grid=(B, nch),
        in_specs=specs(0) + specs(1) + [pl.BlockSpec((1, 128), lambda b, k: (0, 0)),
                                        pl.BlockSpec((ML_TC, ML_TC), lambda b, k: (0, 0)),
                                        pl.BlockSpec((ML_TC, ML_TC), lambda b, k: (0, 0))],
        out_specs=[pl.BlockSpec((ML_TC, vw), lambda b, k: (rb(0, b, k), 0)),
                   pl.BlockSpec((ML_TC, vw), lambda b, k: (rb(1, b, k), 0))],
        out_shape=[out, out],
        scratch_shapes=[pltpu.VMEM((nst, ML_DQK, ML_DV), F32),
                        pltpu.VMEM((nst, 8, ML_DQK), F32),
                        pltpu.VMEM((nst, 8, 128), F32)],
        compiler_params=_cparams(("arbitrary", "arbitrary")),
        name="mlstm_chunks",
    )(proj, proj, proj, proj, proj, proj, proj, proj, gate_bias, tril, tril.T)


def _route(lg):
    lane_i = lax.broadcasted_iota(jnp.int32, lg.shape, 1)
    lane = lane_i.astype(F32)
    neg = -jnp.inf
    gl = jnp.where(lane_i < MOE_GROUPS, lg, neg)
    gmax = jnp.max(gl, axis=-1, keepdims=True)
    gsum = jnp.sum(jnp.where(lane_i < MOE_GROUPS, jnp.exp(lg - gmax), 0.0), axis=-1, keepdims=True)
    p_top = 1.0 / gsum
    g_sel = jnp.min(jnp.where(gl == gmax, lane, 128.0), axis=-1, keepdims=True)
    group_of_lane = (lane_i >> 3).astype(F32) - 1.0
    el = jnp.where(group_of_lane == g_sel, lg, neg)
    e1 = jnp.max(el, axis=-1, keepdims=True)
    i1 = jnp.min(jnp.where(el == e1, lane, 128.0), axis=-1, keepdims=True)
    el2 = jnp.where(lane == i1, neg, el)
    e2 = jnp.max(el2, axis=-1, keepdims=True)
    i2 = jnp.min(jnp.where(el2 == e2, lane, 128.0), axis=-1, keepdims=True)
    t = jnp.exp(e2 - e1)
    w1 = p_top / (1.0 + t)
    w2 = w1 * t
    id1 = i1 - MOE_GROUPS
    id2 = i2 - MOE_GROUPS
    table = jnp.where(lane_i == 0, id1, jnp.where(lane_i == 1, id2,
                                                  jnp.where(lane_i == 2, w1, jnp.where(lane_i == 3, w2, 0.0))))
    chosen = jnp.where(lane == i1, 1.0, 0.0) + jnp.where(lane == i2, 1.0, 0.0)
    return table, jnp.sum(chosen, axis=0, keepdims=True)


def _accumulate_counts(cnt_ref, hist):
    @pl.when(pl.program_id(0) == 0)
    def _():
        cnt_ref[...] = jnp.zeros_like(cnt_ref)
    cnt_ref[...] += hist


def _lhs_mla(o_ref):
    return o_ref[...]


def _lhs_mlstm(hf_ref, hb_ref, og_ref, onorm_ref):
    hs = hf_ref[...] + hb_ref[...]
    og = _sigmoid(og_ref[...])
    parts = []
    for h in range(ML_HEADS):
        sl = slice(h * ML_DV, (h + 1) * ML_DV)
        parts.append(_rms(hs[:, sl]) * onorm_ref[:, sl] * og[:, sl])
    return jnp.concatenate(parts, axis=1)


def _mixer_out_kernel(*refs, n_lhs, lhs_fn):
    lhs_refs = refs[:n_lhs]
    w_ref, x_ref, mod_ref, gain_ref, wr_ref, br_ref, xo_ref, f_ref, r_ref, cnt_ref = refs[n_lhs:]
    y = _dot(lhs_fn(*lhs_refs).astype(BF16), w_ref[...])
    mod = mod_ref[0]
    xn = x_ref[...] + mod[2:3, :] * y
    xo_ref[...] = xn
    f = _norm_mod(xn, gain_ref[...], mod[3:4, :], mod[4:5, :])
    f_ref[...] = f
    r_ref[...], hist = _route(_dot3(f, wr_ref[...]) + br_ref[...])
    _accumulate_counts(cnt_ref, hist)


def _mixer_out(rows, lhs_fn, lhs_args, lhs_specs, w_out, x, mod, gain, wr, br, name):
    n = rows.n * TM
    const = lambda k: (0, 0)
    out = lambda w: pl.BlockSpec((TM, w), lambda k: (k, 0))
    return pl.pallas_call(
        functools.partial(_mixer_out_kernel, n_lhs=len(lhs_args), lhs_fn=lhs_fn),
        grid=(rows.n,),
        in_specs=list(lhs_specs) + [
            pl.BlockSpec((D_MODEL, D_MODEL), const),
            pl.BlockSpec((TM, D_MODEL), lambda k: (rows.src(k), 0)),
            pl.BlockSpec((1, 8, D_MODEL), lambda k: (rows.mod(k), 0, 0)),
            pl.BlockSpec((1, D_MODEL), const),
            pl.BlockSpec((D_MODEL, 128), const),
            pl.BlockSpec((1, 128), const)],
        out_specs=[out(D_MODEL), out(D_MODEL), out(128), pl.BlockSpec((1, 128), const)],
        out_shape=[jax.ShapeDtypeStruct((n, D_MODEL), F32),
                   jax.ShapeDtypeStruct((n, D_MODEL), F32),
                   jax.ShapeDtypeStruct((n, 128), F32),
                   jax.ShapeDtypeStruct((1, 128), F32)],
        compiler_params=_cparams(("arbitrary",)),
        name=name,
    )(*lhs_args, w_out, x, mod, gain, wr, br)


RANK_TILES = 4


def _rank_kernel(r_ref, start_ref, tril_ref, pos_ref, carry_ref):
    @pl.when(pl.program_id(0) == 0)
    def _():
        carry_ref[...] = jnp.zeros_like(carry_ref)

    r = r_ref[...]
    lane = lax.broadcasted_iota(jnp.int32, r.shape, 1).astype(F32)
    chosen = [jnp.where(lane == r[:, k:k + 1] + MOE_GROUPS, 1.0, 0.0) for k in range(MOE_TOPK)]
    both = functools.reduce(jnp.add, chosen)
    before = _dot(tril_ref[...], both.astype(BF16)) + (start_ref[...] + carry_ref[...])
    lane_i = lax.broadcasted_iota(jnp.int32, r.shape, 1)
    pos = jnp.zeros(r.shape, F32)
    for k in range(MOE_TOPK):
        pos = jnp.where(lane_i == k, jnp.sum(chosen[k] * before, axis=-1, keepdims=True), pos)
    pos_t = pos.T[0:8, :].astype(jnp.int32)
    for j in range(RANK_TILES):
        pos_ref[j] = pos_t[:, j * TM:(j + 1) * TM]
    carry_ref[...] += jnp.sum(both, axis=0, keepdims=True)


def _assignment_slots(ntiles, route, starts):
    assert ntiles % RANK_TILES == 0
    rows = RANK_TILES * TM
    tril = jnp.tril(jnp.ones((rows, rows), BF16), -1)
    return pl.pallas_call(
        _rank_kernel,
        grid=(ntiles // RANK_TILES,),
        in_specs=[pl.BlockSpec((rows, 128), lambda i: (i, 0)),
                  pl.BlockSpec((1, 128), lambda i: (0, 0)),
                  pl.BlockSpec((rows, rows), lambda i: (0, 0))],
        out_specs=pl.BlockSpec((RANK_TILES, 8, TM), lambda i: (i, 0, 0)),
        out_shape=jax.ShapeDtypeStruct((ntiles, 8, TM), jnp.int32),
        scratch_shapes=[pltpu.VMEM((1, 128), F32)],
        compiler_params=_cparams(("arbitrary",)),
        name="moe_rank",
    )(route, starts, tril)


def _row_wait(hbm, buf, sem):
    pltpu.make_async_copy(hbm.at[pl.ds(0, TM), :], buf, sem).wait()


def _dispatch_kernel(pos_ref, f_ref, xs_hbm, buf, sem, *, ntiles):
    i = pl.program_id(0)

    def step(s):
        @pl.when(i >= 2)
        def _():
            for _ in range(MOE_TOPK):
                _row_wait(xs_hbm, buf.at[s], sem.at[s])
        buf[s] = f_ref[...]
        for r in range(TM):
            for k in range(MOE_TOPK):
                pltpu.make_async_copy(buf.at[s, pl.ds(r, 1), :], xs_hbm.at[pl.ds(pos_ref[0, k, r], 1), :],
                                      sem.at[s]).start(priority=k % 2)

        @pl.when(i == ntiles - 1)
        def _():
            for slot in ((1 - s, s) if ntiles >= 2 else (s,)):
                for _ in range(MOE_TOPK):
                    _row_wait(xs_hbm, buf.at[slot], sem.at[slot])

    for s in range(2):
        pl.when(i % 2 == s)(functools.partial(step, s))


def _dispatch(ntiles, f, pos):
    n = ntiles * TM
    return pl.pallas_call(
        functools.partial(_dispatch_kernel, ntiles=ntiles),
        grid=(ntiles,),
        in_specs=[pl.BlockSpec((1, 8, TM), lambda i: (i, 0, 0), memory_space=pltpu.SMEM),
                  pl.BlockSpec((TM, D_MODEL), lambda i: (i, 0))],
        out_specs=pl.BlockSpec(memory_space=pl.ANY),
        out_shape=jax.ShapeDtypeStruct((MOE_TOPK * n, D_MODEL), F32),
        scratch_shapes=[pltpu.VMEM((2, TM, D_MODEL), F32), pltpu.SemaphoreType.DMA((2,))],
        compiler_params=_cparams(("arbitrary",)),
        name="moe_dispatch",
    )(pos, f)


def _expert_kernel(vb_ref, ve_ref, lo_ref, hi_ref, first_ref, newexp_ref, eslot_ref, enext_ref, x_ref, wgu_hbm,
                   wd_hbm, y_ref, wgu_f, wd_f, wgu_b, wd_b, sem, *, layer):
    v = pl.program_id(0)

    def fetch(e, s):
        return (pltpu.make_async_copy(wgu_hbm.at[layer, e], wgu_f.at[s], sem.at[0, s]),
                pltpu.make_async_copy(wd_hbm.at[layer, e], wd_f.at[s], sem.at[1, s]))

    @pl.when(newexp_ref[v] == 1)
    def _():
        for s in range(2):
            @pl.when(eslot_ref[v] == s)
            def _():
                @pl.when(v == 0)
                def _():
                    for c in fetch(ve_ref[v], s):
                        c.start()

                @pl.when(enext_ref[v] >= 0)
                def _():
                    for c in fetch(enext_ref[v], 1 - s):
                        c.start()
                for c in fetch(ve_ref[v], s):
                    c.wait()
                wgu_b[...] = wgu_f[s].astype(BF16)
                wd_b[...] = wd_f[s].astype(BF16)

    @pl.when(hi_ref[v] > lo_ref[v])
    def _():
        gu = _dot(x_ref[...].astype(BF16), wgu_b[...])
        gate = gu[:, :MOE_FF]
        act = gate * _sigmoid(gate) * gu[:, MOE_FF:]
        y = _dot(act.astype(BF16), wd_b[...])
        r = lax.broadcasted_iota(jnp.int32, (MOE_BM, 1), 0)
        mine = jnp.logical_and(r >= lo_ref[v], r < hi_ref[v])
        y = jnp.where(mine, y, 0.0)

        @pl.when(first_ref[v] == 1)
        def _():
            y_ref[...] = y

        @pl.when(first_ref[v] == 0)
        def _():
            y_ref[...] += y


def _expert_ffn(xs, visits, layer, w_gate_up, w_down):
    nvis = visits[0].shape[0]
    blk_idx = lambda v, vb, *_: (vb[v], 0)
    grid_spec = pltpu.PrefetchScalarGridSpec(
        num_scalar_prefetch=8,
        grid=(nvis,),
        in_specs=[pl.BlockSpec((MOE_BM, D_MODEL), blk_idx),
                  pl.BlockSpec(memory_space=pl.ANY),
                  pl.BlockSpec(memory_space=pl.ANY)],
        out_specs=pl.BlockSpec((MOE_BM, D_MODEL), blk_idx),
        scratch_shapes=[pltpu.VMEM((2, D_MODEL, 2 * MOE_FF), F32), pltpu.VMEM((2, MOE_FF, D_MODEL), F32),
                        pltpu.VMEM((D_MODEL, 2 * MOE_FF), BF16), pltpu.VMEM((MOE_FF, D_MODEL), BF16),
                        pltpu.SemaphoreType.DMA((2, 2))],
    )
    return pl.pallas_call(
        functools.partial(_expert_kernel, layer=layer),
        grid_spec=grid_spec,
        out_shape=jax.ShapeDtypeStruct(xs.shape, F32),
        compiler_params=_cparams(("arbitrary",)),
        name="moe_expert_ffn",
    )(*visits, xs, w_gate_up, w_down)


def _combine_kernel(pos_ref, nxt_ref, x_ref, r_ref, mod_ref, ys_hbm, o_ref, ybuf, sem, *, ntiles):
    i = pl.program_id(0)

    def gather(table, s):
        for r in range(TM):
            for k in range(MOE_TOPK):
                pltpu.make_async_copy(ys_hbm.at[pl.ds(table[0, k, r], 1), :], ybuf.at[s, k, pl.ds(r, 1), :],
                                      sem.at[s]).start(priority=k % 2)

    @pl.when(i == 0)
    def _():
        gather(pos_ref, 0)

    def step(s):
        @pl.when(i + 1 < ntiles)
        def _():
            gather(nxt_ref, 1 - s)
        for k in range(MOE_TOPK):
            _row_wait(ys_hbm, ybuf.at[s, k], sem.at[s])
        w = r_ref[...]
        y = w[:, MOE_TOPK:MOE_TOPK + 1] * ybuf[s, 0]
        for k in range(1, MOE_TOPK):
            y = y + w[:, MOE_TOPK + k:MOE_TOPK + k + 1] * ybuf[s, k]
        o_ref[...] = x_ref[...] + mod_ref[0][5:6, :] * y

    for s in range(2):
        pl.when(i % 2 == s)(functools.partial(step, s))


def _combine(rows, x, ys, pos, route, mod):
    n = rows.n
    spec = pl.BlockSpec((TM, D_MODEL), lambda i: (i, 0))
    return pl.pallas_call(
        functools.partial(_combine_kernel, ntiles=n),
        grid=(n,),
        in_specs=[pl.BlockSpec((1, 8, TM), lambda i: (i, 0, 0), memory_space=pltpu.SMEM),
                  pl.BlockSpec((1, 8, TM), lambda i: (jnp.minimum(i + 1, n - 1), 0, 0), memory_space=pltpu.SMEM),
                  spec,
                  pl.BlockSpec((TM, 128), lambda i: (i, 0)),
                  pl.BlockSpec((1, 8, D_MODEL), lambda i: (rows.mod(i), 0, 0)),
                  pl.BlockSpec(memory_space=pl.ANY)],
        out_specs=spec,
        out_shape=jax.ShapeDtypeStruct((n * TM, D_MODEL), F32),
        scratch_shapes=[pltpu.VMEM((2, MOE_TOPK, TM, D_MODEL), F32), pltpu.SemaphoreType.DMA((2,))],
        compiler_params=_cparams(("arbitrary",)),
        name="moe_combine",
    )(pos, pos, x, route, mod, ys)


def _visit_tables(bounds, nk):
    E = MOE_EXPERTS
    nblk = nk // MOE_BM
    nvis = nblk + E
    starts, ends = bounds[:-1], bounds[1:]
    fb = starts // MOE_BM
    nv = jnp.where(ends > starts, (ends - 1) // MOE_BM - fb + 1, 0)
    cum = jnp.cumsum(nv)
    total = cum[-1]
    v = jnp.arange(nvis, dtype=jnp.int32)
    active = v < total
    vc = jnp.minimum(v, total - 1)
    ve = jnp.minimum(jnp.sum((cum[None, :] <= vc[:, None]).astype(jnp.int32), axis=1), E - 1)
    vb = fb[ve] + (vc - (cum - nv)[ve])
    lo = jnp.where(active, jnp.maximum(starts[ve], vb * MOE_BM) - vb * MOE_BM, 0)
    hi = jnp.where(active, jnp.minimum(ends[ve], (vb + 1) * MOE_BM) - vb * MOE_BM, 0)
    prev_b = jnp.concatenate([jnp.full((1,), -1, jnp.int32), vb[:-1]])
    first = jnp.logical_and(active, vb != prev_b)
    prev_e = jnp.concatenate([jnp.full((1,), -1, jnp.int32), ve[:-1]])
    new_expert = jnp.logical_and(active, ve != prev_e)
    eslot = (jnp.cumsum(new_expert.astype(jnp.int32)) - 1) % 2
    at = jnp.where(new_expert, v, nvis)
    nxt = jnp.concatenate([lax.cummin(at[::-1])[::-1][1:], jnp.full((1,), nvis, jnp.int32)])
    enext = jnp.where(nxt < nvis, ve[jnp.minimum(nxt, nvis - 1)], -1)
    i32 = lambda a: a.astype(jnp.int32)
    return i32(vb), i32(ve), i32(lo), i32(hi), i32(first), i32(new_expert), i32(eslot), i32(enext)


def _moe(rows, x, f, route, counts, mod, layer, w_gate_up, w_down):
    n = rows.n * TM
    cum = jnp.cumsum(counts[0])
    starts = (cum - counts[0]).reshape(1, 128)
    bounds = jnp.concatenate([starts[0, MOE_GROUPS:MOE_GROUPS + MOE_EXPERTS], cum[-1:]]).astype(jnp.int32)
    pos = _assignment_slots(rows.n, route, starts)
    xs = _dispatch(rows.n, f, pos)
    ys = _expert_ffn(xs, _visit_tables(bounds, n * MOE_TOPK), layer, w_gate_up, w_down)
    return _combine(rows, x, ys, pos, route, mod)


def kernel(x, c, ctx, c_ctx, ada_w, ada_b, norm_mix, norm_ffn, rg_w_in, rg_conv_w, rg_conv_b, rg_gate_w, rg_gate_b, rg_lambda, rg_w_out, mla_w_down, mla_q_norm, mla_kv_norm, mla_w_uq, mla_w_ukv, mla_qk_norm, mla_w_o, ml_w_in, ml_gate_b, ml_out_norm, ml_w_out, moe_w_group, moe_b_group, moe_w_expert, moe_b_expert, moe_w_gate_up, moe_w_down):
    B, L, D = x.shape
    C = ctx.shape[1]
    depth = ada_w.shape[0]
    assert D == D_MODEL
    st = _Stream(B, L, C)

    xs = jnp.concatenate([ctx, x], axis=1).reshape(st.NT, D)
    cc = jnp.zeros((16, D), F32).at[:B].set(c).at[B].set(c_ctx)
    mod_all = _modulation(cc, ada_w, ada_b)
    mod_all = jnp.pad(mod_all[:, :B + 1].reshape(depth, B + 1, 6, D), ((0, 0), (0, 0), (0, 2), (0, 0)))
    perm = _row_permutation(B)

    row = lambda a: a.reshape(1, -1)

    for i in range(depth):
        last = i == depth - 1
        mod = mod_all[i]
        kind, j = i % 3, i // 3
        all_rows = st.all_rows()
        out_rows = st.latent_rows() if last else all_rows
        tile_spec = lambda w: pl.BlockSpec((TM, w), lambda k: (out_rows.src(k), 0))
        wr = jnp.zeros((D, 128), F32).at[:, :MOE_GROUPS].set(moe_w_group[i]) \
            .at[:, MOE_GROUPS:MOE_GROUPS + MOE_EXPERTS].set(moe_w_expert[i])
        br = jnp.zeros((1, 128), F32).at[0, :MOE_GROUPS].set(moe_b_group[i]) \
            .at[0, MOE_GROUPS:MOE_GROUPS + MOE_EXPERTS].set(moe_b_expert[i])
        out_args = (xs, mod, row(norm_ffn[i]), wr, br)

        if kind == 0:
            gate, u = _rg_in(st, xs, mod, row(norm_mix[i]), perm, rg_w_in[j].astype(BF16))
            wg, gb = _rg_gate_weights(rg_gate_w[j], rg_gate_b[j])
            hs = _rg_scan(st, u, rg_conv_w[j], row(rg_conv_b[j]), wg, gb, rg_lambda[j].reshape(2, 1, D))
            xs, f, route, counts = _rg_out(st, last, gate, hs, perm.T, rg_w_out[j].astype(BF16), *out_args)
        elif kind == 1:
            w_down = jnp.pad(mla_w_down[j], ((0, 0), (0, 512 - mla_w_down.shape[2]))).astype(BF16)
            down = _norm_proj(all_rows, xs, mod, row(norm_mix[i]), w_down, name="mla_down_proj")
            wq, wk, we, wv = _mla_weights(mla_w_uq[j], mla_w_ukv[j])
            q, k, v = _mla_up(st, all_rows, down, row(mla_q_norm[j]), row(mla_kv_norm[j]), wq, wk, we, wv,
                              _rope_tables(L, mla_qk_norm[j]))
            o = _attention(st, q, k, v)
            xs, f, route, counts = _mixer_out(out_rows, _lhs_mla, (o,), (tile_spec(D),),
                                      mla_w_o[j].astype(BF16), *out_args, name="mla_out")
        else:
            n_in = ml_w_in.shape[2]
            w_in = jnp.pad(ml_w_in[j], ((0, 0), (0, ML_NP - n_in))).astype(BF16)
            proj = _norm_proj(all_rows, xs, mod, row(norm_mix[i]), w_in, name="mlstm_in_proj")
            gate_bias = jnp.pad(ml_gate_b[j].reshape(1, -1), ((0, 0), (0, 128 - 4 * ML_HEADS)))
            hf, hb = _mlstm(st, proj, gate_bias)
            og_spec = pl.BlockSpec((TM, D), lambda k: (out_rows.src(k), 2))
            xs, f, route, counts = _mixer_out(out_rows, _lhs_mlstm, (hf, hb, proj, row(ml_out_norm[j])),
                                      (tile_spec(D), tile_spec(D), og_spec, pl.BlockSpec((1, D), lambda k: (0, 0))),
                                      ml_w_out[j].astype(BF16), *out_args, name="mlstm_out")

        moe_rows = st.dense_latent_rows() if last else all_rows
        xs = _moe(moe_rows, xs, f, route, counts, mod, i, moe_w_gate_up, moe_w_down)

    return xs.reshape(B, L, D)
```

```python
import functools
import math

import jax
import jax.numpy as jnp
from jax import lax
from jax.experimental import pallas as pl
from jax.experimental.pallas import tpu as pltpu

F32 = jnp.float32
BF16 = jnp.bfloat16

D_MODEL = 1024
RMS_EPS = 1e-6

TM = 256
VMEM_LIMIT = 48 * 1024 * 1024

RG_BLOCK_W = 64
RG_CHUNK = 256
RG_CONV_W = 4
RG_C = 8.0
RG_TT = 64

MLA_HEADS = 16
MLA_Q_RANK = 256
MLA_KV_RANK = 128
MLA_NOPE = 64
MLA_ROPE = 32
MLA_V = 64
MLA_QK = MLA_NOPE + MLA_ROPE
MLA_HP = 128
ROPE_AXIS_DIM = MLA_ROPE // 2
ROPE_BASE = 10000.0
GRID_W = 64
ATT_HEADS = 8

ML_HEADS = 4
ML_DV = 256
ML_DQK = 128
ML_TC = 256
ML_M_INIT = -1e30
ML_NP = 3200

MOE_GROUPS = 8
MOE_PER_GROUP = 8
MOE_EXPERTS = 64
MOE_TOPK = 2
MOE_FF = 256
MOE_BM = 1024


def _cparams(sem):
    return pltpu.CompilerParams(dimension_semantics=sem, vmem_limit_bytes=VMEM_LIMIT)


def _dot(a, b):
    return jnp.dot(a, b, preferred_element_type=F32)


def _dot_t(a, b):
    return lax.dot_general(a, b, (((1,), (1,)), ((), ())), preferred_element_type=F32)


def _dot3(a, b):
    ah = a.astype(BF16)
    al = (a - ah.astype(F32)).astype(BF16)
    bh = b.astype(BF16)
    bl = (b - bh.astype(F32)).astype(BF16)
    return _dot(ah, bh) + (_dot(al, bh) + _dot(ah, bl))


def _sigmoid(x):
    return 0.5 * jnp.tanh(0.5 * x) + 0.5


def _softplus(x):
    return jnp.maximum(x, 0.0) + jnp.log1p(jnp.exp(-jnp.abs(x)))


def _gelu_tanh(x):
    return 0.5 * x * (1.0 + jnp.tanh(0.7978845608028654 * (x + 0.044715 * (x * x * x))))


def _rms(x, n=None):
    n = x.shape[-1] if n is None else n
    ms = jnp.sum(x * x, axis=-1, keepdims=True) * (1.0 / n)
    return x * lax.rsqrt(ms + RMS_EPS)


class _Rows:
    def __init__(self, n, src, mod):
        self.n, self.src, self.mod = n, src, mod


class _Stream:
    def __init__(self, B, L, C):
        assert L % TM == 0 and C % TM == 0 and B % 8 == 0
        self.B, self.L, self.C = B, L, C
        self.ltot = L + C
        self.lt, self.ct = L // TM, C // TM
        self.tpb = self.lt + self.ct
        self.NT = B * self.ltot

    def all_rows(self):
        tpb, ct, B = self.tpb, self.ct, self.B
        return _Rows(B * tpb, lambda k: k, lambda k: jnp.where(k % tpb < ct, B, k // tpb))

    def latent_rows(self):
        tpb, ct, lt = self.tpb, self.ct, self.lt
        return _Rows(self.B * lt, lambda k: (k // lt) * tpb + ct + k % lt, lambda k: k // lt)

    def dense_latent_rows(self):
        lt = self.lt
        return _Rows(self.B * lt, lambda k: k, lambda k: k // lt)


def _mod_kernel(c_ref, w_ref, b_ref, o_ref):
    c = c_ref[...]
    o_ref[0] = _dot3(c * _sigmoid(c), w_ref[0]) + b_ref[0]


def _modulation(cc, ada_w, ada_b):
    depth, d, n = ada_w.shape
    tn = 1536
    return pl.pallas_call(
        _mod_kernel,
        grid=(depth, n // tn),
        in_specs=[pl.BlockSpec((16, d), lambda l, j: (0, 0)),
                  pl.BlockSpec((1, d, tn), lambda l, j: (l, 0, j)),
                  pl.BlockSpec((1, 1, tn), lambda l, j: (l, 0, j))],
        out_specs=pl.BlockSpec((1, 16, tn), lambda l, j: (l, 0, j)),
        out_shape=jax.ShapeDtypeStruct((depth, 16, n), F32),
        compiler_params=_cparams(("arbitrary", "arbitrary")),
        name="ada_modulation",
    )(cc, ada_w, ada_b.reshape(depth, 1, n))


def _norm_mod(x, gain, shift, scale):
    return _rms(x) * gain * (1.0 + scale) + shift


def _norm_proj_kernel(x_ref, mod_ref, g_ref, w_ref, o_ref):
    mod = mod_ref[0]
    h = _norm_mod(x_ref[...], g_ref[...], mod[0:1, :], mod[1:2, :]).astype(BF16)
    o_ref[...] = _dot(h, w_ref[...]).astype(o_ref.dtype)


def _norm_proj(rows, x, mod, gain, w, name):
    n = w.shape[1]
    const = lambda k: (0, 0)
    return pl.pallas_call(
        _norm_proj_kernel,
        grid=(rows.n,),
        in_specs=[pl.BlockSpec((TM, D_MODEL), lambda k: (rows.src(k), 0)),
                  pl.BlockSpec((1, 8, D_MODEL), lambda k: (rows.mod(k), 0, 0)),
                  pl.BlockSpec((1, D_MODEL), const),
                  pl.BlockSpec((D_MODEL, n), const)],
        out_specs=pl.BlockSpec((TM, n), lambda k: (k, 0)),
        out_shape=jax.ShapeDtypeStruct((rows.n * TM, n), F32),
        compiler_params=_cparams(("arbitrary",)),
        name=name,
    )(x, mod, gain, w)


RG_TR = 32


def _row_permutation(B):
    n = B * RG_TR
    r = jnp.arange(n)
    src = (r % B) * RG_TR + r // B
    return (src[:, None] == jnp.arange(n)[None, :]).astype(BF16)


def _time_tile_mod(mod_ref, is_ctx, B, row):
    return jnp.where(is_ctx, mod_ref[B:B + 1, row:row + 1, :], mod_ref[0:B, row:row + 1, :])


def _rg_in_kernel(x_ref, mod_ref, g_ref, perm_ref, wg_ref, wu_ref, gate_ref, u_ref, *, B, ctx_tiles):
    is_ctx = pl.program_id(0) < ctx_tiles
    h = _norm_mod(x_ref[...], g_ref[...], _time_tile_mod(mod_ref, is_ctx, B, 0), _time_tile_mod(mod_ref, is_ctx, B, 1))
    h = h.reshape(B * RG_TR, D_MODEL).astype(BF16)
    h = _dot(perm_ref[...], h).astype(BF16)
    gate_ref[...] = _gelu_tanh(_dot(h, wg_ref[...])).astype(gate_ref.dtype)
    u_ref[...] = _dot(h, wu_ref[...])


def _rg_in(st, x, mod, gain, perm, w_in):
    B, W = st.B, D_MODEL
    R = B * RG_TR
    nt = st.ltot // RG_TR
    const = lambda t: (0, 0)
    return pl.pallas_call(
        functools.partial(_rg_in_kernel, B=B, ctx_tiles=st.C // RG_TR),
        grid=(nt,),
        in_specs=[pl.BlockSpec((B, RG_TR, D_MODEL), lambda t: (0, t, 0)),
                  pl.BlockSpec((B + 1, 8, D_MODEL), lambda t: (0, 0, 0)),
                  pl.BlockSpec((1, D_MODEL), const),
                  pl.BlockSpec((R, R), const),
                  pl.BlockSpec((D_MODEL, W), const),
                  pl.BlockSpec((D_MODEL, W), lambda t: (0, 1))],
        out_specs=[pl.BlockSpec((R, W), lambda t: (t, 0)),
                   pl.BlockSpec((R, W), lambda t: (t, 0))],
        out_shape=[jax.ShapeDtypeStruct((st.ltot * B, W), BF16), jax.ShapeDtypeStruct((st.ltot * B, W), F32)],
        compiler_params=_cparams(("arbitrary",)),
        name="rg_in_proj",
    )(x.reshape(B, st.ltot, D_MODEL), mod, gain, perm, w_in, w_in)


def _rg_tile_order(d, k, ct, ntt):
    bwd = jnp.where(k < ct, ct - 1 - k, ntt - 1 - (k - ct))
    return jnp.where(d == 0, k, bwd)


def _rg_scan_kernel(um_ref, up_ref, un_ref, cw_ref, cb_ref, wg_ref, gb_ref, lam_ref, o_ref,
                    ext_ref, a_ref, b_ref, h_ref, *, B, ct, ntt):
    d = pl.program_id(0)
    k = pl.program_id(1)
    tile = _rg_tile_order(d, k, ct, ntt)
    R = RG_TT * B

    @pl.when(k == 0)
    def _():
        h_ref[...] = jnp.zeros_like(h_ref)

    seq_start = jnp.logical_or(tile == 0, tile == ct)
    seq_end = jnp.logical_or(tile == ct - 1, tile == ntt - 1)
    ext_ref[0:2 * B, :] = jnp.where(seq_start, 0.0, up_ref[...])
    ext_ref[2 * B:2 * B + R, :] = um_ref[...]
    ext_ref[2 * B + R:3 * B + R, :] = jnp.where(seq_end, 0.0, un_ref[...])
    cw = cw_ref[...]
    uc = cb_ref[...] + cw[0:1, :] * ext_ref[0:R, :]
    for j in range(1, RG_CONV_W):
        uc = uc + cw[j:j + 1, :] * ext_ref[j * B:j * B + R, :]
    ucb = uc.astype(BF16)
    c_lam = -RG_C * _softplus(-lam_ref[0])
    for c in range(D_MODEL // RG_CHUNK):
        sl = slice(c * RG_CHUNK, (c + 1) * RG_CHUNK)
        z = _dot(ucb[:, sl], wg_ref[0, c]) + gb_ref[0, c]
        r = _sigmoid(z[:, :RG_CHUNK])
        ig = _sigmoid(z[:, RG_CHUNK:])
        log_a = c_lam[:, sl] * r
        a = jnp.exp(log_a)
        one_minus_a2 = -jnp.tanh(log_a) * (a * a + 1.0)
        a_ref[:, sl] = a
        b_ref[:, sl] = jnp.sqrt(one_minus_a2) * (ig * uc[:, sl])

    def scan(times):
        for c in range(D_MODEL // 128):
            cs = slice(c * 128, (c + 1) * 128)
            h = h_ref[:, cs]
            for t in times:
                rs = slice(t * B, (t + 1) * B)
                h = a_ref[rs, cs] * h + b_ref[rs, cs]
                b_ref[rs, cs] = h
            h_ref[:, cs] = h

    pl.when(d == 0)(lambda: scan(range(RG_TT)))
    pl.when(d == 1)(lambda: scan(range(RG_TT - 1, -1, -1)))
    o_ref[0] = b_ref[...].astype(o_ref.dtype)


def _rg_scan(st, u_tm, conv_w, conv_b, wg, gb, lam):
    B, W = st.B, D_MODEL
    assert st.C % RG_TT == 0 and st.L % RG_TT == 0
    ltot = st.ltot
    ntt, ct = ltot // RG_TT, st.C // RG_TT
    R = RG_TT * B
    order = functools.partial(_rg_tile_order, ct=ct, ntt=ntt)
    nch = W // RG_CHUNK
    return pl.pallas_call(
        functools.partial(_rg_scan_kernel, B=B, ct=ct, ntt=ntt),
        grid=(2, ntt),
        in_specs=[pl.BlockSpec((R, W), lambda d, k: (order(d, k), 0)),
                  pl.BlockSpec((2 * B, W), lambda d, k: (jnp.maximum(order(d, k) * (RG_TT // 2) - 1, 0), 0)),
                  pl.BlockSpec((B, W), lambda d, k: (jnp.minimum((order(d, k) + 1) * RG_TT, ltot - 1), 0)),
                  pl.BlockSpec((RG_CONV_W, W), lambda d, k: (0, 0)),
                  pl.BlockSpec((1, W), lambda d, k: (0, 0)),
                  pl.BlockSpec((1, nch, RG_CHUNK, 2 * RG_CHUNK), lambda d, k: (d, 0, 0, 0)),
                  pl.BlockSpec((1, nch, 1, 2 * RG_CHUNK), lambda d, k: (d, 0, 0, 0)),
                  pl.BlockSpec((1, 1, W), lambda d, k: (d, 0, 0))],
        out_specs=pl.BlockSpec((1, R, W), lambda d, k: (d, order(d, k), 0)),
        out_shape=jax.ShapeDtypeStruct((2, ltot * B, W), BF16),
        scratch_shapes=[pltpu.VMEM((3 * B + R, W), F32),
                        pltpu.VMEM((R, W), F32),
                        pltpu.VMEM((R, W), F32),
                        pltpu.VMEM((B, W), F32)],
        compiler_params=_cparams(("arbitrary", "arbitrary")),
        name="rg_scan",
    )(u_tm, u_tm, u_tm, conv_w, conv_b, wg, gb, lam)


def _rg_gate_weights(gate_w, gate_b):
    nb = gate_w.shape[2]
    per = RG_CHUNK // RG_BLOCK_W
    nch = nb // per
    gw = gate_w.reshape(2, 2, nch, per, RG_BLOCK_W, RG_BLOCK_W)
    eye = jnp.eye(per, dtype=gate_w.dtype)
    bd = jnp.einsum('dgcnij,nm->dgcnimj', gw, eye).reshape(2, 2, nch, RG_CHUNK, RG_CHUNK)
    wg = jnp.concatenate([bd[:, 0], bd[:, 1]], axis=-1).astype(BF16)
    gb = gate_b.reshape(2, 2, nch, 1, RG_CHUNK)
    gb = jnp.concatenate([gb[:, 0], gb[:, 1]], axis=-1)
    return wg, gb


def _rg_out_kernel(g_ref, hf_ref, hb_ref, perm_ref, w_ref, x_ref, mod_ref, gain_ref, wr_ref, br_ref,
                   xo_ref, f_ref, r_ref, cnt_ref, *, B, ctx_tiles, t0):
    is_ctx = pl.program_id(0) + t0 < ctx_tiles
    hsum = hf_ref[0].astype(F32) + hb_ref[0].astype(F32)
    lhs = (g_ref[...].astype(F32) * hsum).astype(BF16)
    lhs = _dot(perm_ref[...], lhs).astype(BF16)
    y = _dot(lhs, w_ref[...]).reshape(B, RG_TR, D_MODEL)
    m = lambda row: _time_tile_mod(mod_ref, is_ctx, B, row)
    xn = x_ref[...] + m(2) * y
    xo_ref[...] = xn
    f = _norm_mod(xn, gain_ref[...], m(3), m(4))
    f_ref[...] = f
    lg = _dot3(f.reshape(B * RG_TR, D_MODEL), wr_ref[...]) + br_ref[...]
    table, hist = _route(lg)
    r_ref[...] = table.reshape(B, RG_TR, 128)
    _accumulate_counts(cnt_ref, hist)


def _rg_out(st, latent_only, gate, hs, perm, w_out, x, mod, gain, wr, br):
    B = st.B
    R = B * RG_TR
    t0 = st.C // RG_TR if latent_only else 0
    lo = st.L if latent_only else st.ltot
    nt = lo // RG_TR
    const = lambda t: (0, 0)
    blk = lambda w: pl.BlockSpec((B, RG_TR, w), lambda t: (0, t, 0))
    xo, f, route, counts = pl.pallas_call(
        functools.partial(_rg_out_kernel, B=B, ctx_tiles=st.C // RG_TR, t0=t0),
        grid=(nt,),
        in_specs=[pl.BlockSpec((R, D_MODEL), lambda t: (t + t0, 0)),
                  pl.BlockSpec((1, R, D_MODEL), lambda t: (0, t + t0, 0)),
                  pl.BlockSpec((1, R, D_MODEL), lambda t: (1, t + t0, 0)),
                  pl.BlockSpec((R, R), const),
                  pl.BlockSpec((D_MODEL, D_MODEL), const),
                  pl.BlockSpec((B, RG_TR, D_MODEL), lambda t: (0, t + t0, 0)),
                  pl.BlockSpec((B + 1, 8, D_MODEL), lambda t: (0, 0, 0)),
                  pl.BlockSpec((1, D_MODEL), const),
                  pl.BlockSpec((D_MODEL, 128), const),
                  pl.BlockSpec((1, 128), const)],
        out_specs=[blk(D_MODEL), blk(D_MODEL), blk(128), pl.BlockSpec((1, 128), const)],
        out_shape=[jax.ShapeDtypeStruct((B, lo, D_MODEL), F32),
                   jax.ShapeDtypeStruct((B, lo, D_MODEL), F32),
                   jax.ShapeDtypeStruct((B, lo, 128), F32),
                   jax.ShapeDtypeStruct((1, 128), F32)],
        compiler_params=_cparams(("arbitrary",)),
        name="rg_out",
    )(gate, hs, hs, perm, w_out, x.reshape(B, st.ltot, D_MODEL), mod, gain, wr, br)
    return xo.reshape(B * lo, D_MODEL), f.reshape(B * lo, D_MODEL), route.reshape(B * lo, 128), counts


_ROPE_HALF = ROPE_AXIS_DIM // 2
_MLA_SRC_DIM = (list(range(MLA_NOPE + ROPE_AXIS_DIM)) + list(range(MLA_NOPE, MLA_NOPE + _ROPE_HALF))
                + list(range(MLA_NOPE + ROPE_AXIS_DIM, MLA_QK))
                + list(range(MLA_NOPE + ROPE_AXIS_DIM, MLA_NOPE + ROPE_AXIS_DIM + _ROPE_HALF)))
_MLA_REAL_LANE = ([1.0] * (MLA_NOPE + ROPE_AXIS_DIM) + [0.0] * _ROPE_HALF + [1.0] * ROPE_AXIS_DIM
                  + [0.0] * _ROPE_HALF + [0.0] * (MLA_HP - len(_MLA_SRC_DIM)))


def _mla_up_kernel(dn_ref, qn_ref, kvn_ref, wq_ref, wk_ref, we_ref, wv_ref, real_ref, cq_ref, sq_ref, ck_ref, sk_ref,
                   q_ref, k_ref, v_ref):
    dn = dn_ref[...]
    cq = _rms(dn[:, :MLA_Q_RANK]) * qn_ref[...]
    ckv = _rms(dn[:, MLA_Q_RANK:MLA_Q_RANK + MLA_KV_RANK]) * kvn_ref[...]
    kr = dn[:, MLA_Q_RANK + MLA_KV_RANK:]
    kr_hi = kr.astype(BF16)
    kr_lo = (kr - kr_hi.astype(F32)).astype(BF16)
    ckvb = ckv.astype(BF16)
    q_pre = _dot(cq.astype(BF16), wq_ref[...])
    k_pre = _dot(ckvb, wk_ref[...]) + (_dot(kr_hi, we_ref[...]) + _dot(kr_lo, we_ref[...]))
    v_ref[...] = _dot(ckvb, wv_ref[...]).astype(v_ref.dtype)
    real = real_ref[...]

    def head(x, cos_g, sin_g):
        ms = jnp.sum(x * x * real, axis=-1, keepdims=True) * (1.0 / MLA_QK)
        xr = x * lax.rsqrt(ms + RMS_EPS)
        return xr * cos_g + pltpu.roll(xr, MLA_HP - _ROPE_HALF, 1) * sin_g

    cos_q, sin_q, cos_k, sin_k = cq_ref[...], sq_ref[...], ck_ref[...], sk_ref[...]
    for h in range(MLA_HEADS):
        sl = slice(h * MLA_HP, (h + 1) * MLA_HP)
        q_ref[:, sl] = head(q_pre[:, sl], cos_q, sin_q).astype(q_ref.dtype)
        k_ref[:, sl] = head(k_pre[:, sl], cos_k, sin_k).astype(k_ref.dtype)


def _mla_up(st, rows, down, q_norm, kv_norm, wq, wk, we, wv, tables):
    hw = MLA_HEADS * MLA_HP
    const = lambda i: (0, 0)
    tpb, ct, lt = st.tpb, st.ct, st.lt
    rope_idx = lambda i: (jnp.where(i % tpb < ct, lt, i % tpb - ct), 0)
    real = jnp.asarray(_MLA_REAL_LANE, F32).reshape(1, MLA_HP)
    return pl.pallas_call(
        _mla_up_kernel,
        grid=(rows.n,),
        in_specs=[pl.BlockSpec((TM, 512), lambda i: (i, 0)),
                  pl.BlockSpec((1, MLA_Q_RANK), const),
                  pl.BlockSpec((1, MLA_KV_RANK), const),
                  pl.BlockSpec((MLA_Q_RANK, hw), const),
                  pl.BlockSpec((MLA_KV_RANK, hw), const),
                  pl.BlockSpec((128, hw), const),
                  pl.BlockSpec((MLA_KV_RANK, MLA_HEADS * MLA_V), const),
                  pl.BlockSpec((1, MLA_HP), const)] + [pl.BlockSpec((TM, MLA_HP), rope_idx)] * 4,
        out_specs=[pl.BlockSpec((TM, hw), lambda i: (i, 0)),
                   pl.BlockSpec((TM, hw), lambda i: (i, 0)),
                   pl.BlockSpec((TM, MLA_HEADS * MLA_V), lambda i: (i, 0))],
        out_shape=[jax.ShapeDtypeStruct((st.NT, hw), BF16),
                   jax.ShapeDtypeStruct((st.NT, hw), BF16),
                   jax.ShapeDtypeStruct((st.NT, MLA_HEADS * MLA_V), BF16)],
        compiler_params=_cparams(("arbitrary",)),
        name="mla_up_proj",
    )(down, q_norm, kv_norm, wq, wk, we, wv, real, *tables)


def _attn_kernel(q_ref, k_ref, v_ref, o_ref, vaug_ref, *, C, ct):
    qi = pl.program_id(2)

    @pl.when(qi == 0)
    def _():
        lane = lax.broadcasted_iota(jnp.int32, (k_ref.shape[0], 2 * MLA_V), 1)
        for hh in range(ATT_HEADS):
            pair = v_ref[:, (hh // 2) * 2 * MLA_V:(hh // 2 + 1) * 2 * MLA_V].astype(F32)
            if hh % 2 == 0:
                aug = jnp.where(lane < MLA_V, pair, jnp.where(lane == MLA_V, 1.0, 0.0))
            else:
                aug = jnp.where(lane >= MLA_V, pair, jnp.where(lane == 0, 1.0, 0.0))
            vaug_ref[hh] = aug.astype(BF16)

    def attend(nkeys):
        lane = lax.broadcasted_iota(jnp.int32, (TM, 2 * MLA_V), 1)
        for pair in range(ATT_HEADS // 2):
            outs = []
            for hh in (2 * pair, 2 * pair + 1):
                sl = slice(hh * MLA_HP, (hh + 1) * MLA_HP)
                s = _dot_t(q_ref[:, sl], k_ref[0:nkeys, sl])
                p = jnp.exp2(s - jnp.max(s, axis=-1, keepdims=True)).astype(BF16)
                o = _dot(p, vaug_ref[hh, 0:nkeys, :])
                rowsum = o[:, MLA_V:MLA_V + 1] if hh % 2 == 0 else o[:, 0:1]
                outs.append(o * (1.0 / rowsum))
            o_ref[:, pair * 2 * MLA_V:(pair + 1) * 2 * MLA_V] = jnp.where(lane < MLA_V, outs[0], outs[1]).astype(o_ref.dtype)

    pl.when(qi < ct)(lambda: attend(C))
    pl.when(qi >= ct)(lambda: attend(k_ref.shape[0]))


def _attention(st, q, k, v):
    B, ltot, tpb = st.B, st.ltot, st.tpb
    hg = MLA_HEADS // ATT_HEADS
    return pl.pallas_call(
        functools.partial(_attn_kernel, C=st.C, ct=st.ct),
        grid=(B, hg, tpb),
        in_specs=[pl.BlockSpec((TM, ATT_HEADS * MLA_HP), lambda b, h, i: (b * tpb + i, h)),
                  pl.BlockSpec((ltot, ATT_HEADS * MLA_HP), lambda b, h, i: (b, h)),
                  pl.BlockSpec((ltot, ATT_HEADS * MLA_V), lambda b, h, i: (b, h))],
        out_specs=pl.BlockSpec((TM, ATT_HEADS * MLA_V), lambda b, h, i: (b * tpb + i, h)),
        out_shape=jax.ShapeDtypeStruct((st.NT, MLA_HEADS * MLA_V), BF16),
        scratch_shapes=[pltpu.VMEM((ATT_HEADS, ltot, 2 * MLA_V), BF16)],
        compiler_params=_cparams(("arbitrary", "arbitrary", "arbitrary")),
        name="mla_attention",
    )(q, k, v)


def _mla_weights(w_uq, w_ukv):
    H = MLA_HEADS
    src = jnp.asarray(_MLA_SRC_DIM, jnp.int32)
    pad = MLA_HP - len(_MLA_SRC_DIM)
    wq = jnp.pad(w_uq.reshape(MLA_Q_RANK, H, MLA_QK)[:, :, src], ((0, 0), (0, 0), (0, pad)))
    wkv = w_ukv.reshape(MLA_KV_RANK, H, MLA_NOPE + MLA_V)
    wk = jnp.pad(wkv[:, :, :MLA_NOPE], ((0, 0), (0, 0), (0, MLA_HP - MLA_NOPE)))
    wv = wkv[:, :, MLA_NOPE:]
    r = jnp.arange(128)[:, None]
    lane_dim = jnp.pad(src, (0, pad), constant_values=-1)[None, :]
    place = (lane_dim == r + MLA_NOPE).astype(BF16)
    we = jnp.tile(place, (1, H))
    return (wq.reshape(MLA_Q_RANK, H * MLA_HP).astype(BF16), wk.reshape(MLA_KV_RANK, H * MLA_HP).astype(BF16),
            we, wv.reshape(MLA_KV_RANK, H * MLA_V).astype(BF16))


def _rope_tables(L, qk_norm):
    rows = L // GRID_W
    row = jnp.broadcast_to(jnp.arange(rows, dtype=F32)[:, None], (rows, GRID_W)).reshape(L)
    col = jnp.broadcast_to(jnp.arange(GRID_W, dtype=F32)[None, :], (rows, GRID_W)).reshape(L)
    inv_freq = ROPE_BASE ** (-jnp.arange(0, ROPE_AXIS_DIM, 2, dtype=F32) / ROPE_AXIS_DIM)
    ar = row[:, None] * inv_freq
    ac = col[:, None] * inv_freq
    h8 = _ROPE_HALF
    one = jnp.ones((L, MLA_NOPE), F32)
    z8 = jnp.zeros((L, h8), F32)
    zpad = jnp.zeros((L, MLA_HP - len(_MLA_SRC_DIM)), F32)
    cos_t = jnp.concatenate([one, jnp.cos(ar), jnp.cos(ar), z8, jnp.cos(ac), jnp.cos(ac), z8, zpad], axis=1)
    sin_t = jnp.concatenate([0 * one, -jnp.sin(ar), jnp.sin(ar), z8, -jnp.sin(ac), jnp.sin(ac), z8, zpad], axis=1)
    ident = jnp.asarray(_MLA_REAL_LANE, F32)[None, :]
    cos_t = jnp.concatenate([cos_t, jnp.broadcast_to(ident, (TM, MLA_HP))], axis=0)
    sin_t = jnp.concatenate([sin_t, jnp.zeros((TM, MLA_HP), F32)], axis=0)
    src = jnp.asarray(_MLA_SRC_DIM, jnp.int32)
    pad = MLA_HP - len(_MLA_SRC_DIM)
    scale = MLA_QK ** -0.5 * math.log2(math.e)
    tables = []
    for g, s in ((qk_norm[0], scale), (qk_norm[1], 1.0)):
        g_lane = jnp.pad(g[src], (0, pad))
        g_partner = jnp.roll(g_lane, -h8)
        tables += [cos_t * (g_lane * s)[None, :], sin_t * (g_partner * s)[None, :]]
    return tables


def _log_sigmoid(x):
    return jnp.minimum(x, 0.0) - jnp.log1p(jnp.exp(-jnp.abs(x)))


def _mlstm_kernel(qf_ref, kf_ref, vf_ref, gf_ref, qb_ref, kb_ref, vb_ref, gb_ref, bias_ref, tril_ref, triu_ref,
                  of_ref, ob_ref, c_ref, n_ref, m_ref):
    T = ML_TC

    @pl.when(pl.program_id(1) == 0)
    def _():
        c_ref[...] = jnp.zeros_like(c_ref)
        n_ref[...] = jnp.zeros_like(n_ref)
        m_ref[...] = jnp.full(m_ref.shape, ML_M_INIT, F32)

    ti = lax.broadcasted_iota(jnp.int32, (T, T), 0)
    si = lax.broadcasted_iota(jnp.int32, (T, T), 1)
    dirs = ((qf_ref, kf_ref, vf_ref, gf_ref, of_ref, tril_ref), (qb_ref, kb_ref, vb_ref, gb_ref, ob_ref, triu_ref))
    for d, (q_ref, k_ref, v_ref, g_ref, o_ref, tri_ref) in enumerate(dirs):
        tri = (si <= ti) if d == 0 else (si >= ti)
        g = g_ref[...] + bias_ref[...]
        g_t = g.T
        lsg = _log_sigmoid(g)
        lsg_hi = lsg.astype(BF16)
        lsg_lo = (lsg - lsg_hi.astype(F32)).astype(BF16)
        cum = _dot(tri_ref[...], lsg_hi) + _dot(tri_ref[...], lsg_lo)
        cum_t = cum.T
        last = T - 1 if d == 0 else 0
        for h in range(ML_HEADS):
            st = d * ML_HEADS + h
            li, lf_ = (2 * d) * ML_HEADS + h, (2 * d + 1) * ML_HEADS + h
            ig_col = g[:, li:li + 1]
            ig_row = g_t[li:li + 1, :]
            b_col = cum[:, lf_:lf_ + 1]
            b_row = cum_t[lf_:lf_ + 1, :]
            total = cum[last:last + 1, lf_:lf_ + 1]
            m_old = m_ref[st, 0:1, 0:1]
            d_log = jnp.where(tri, b_col - b_row + ig_row, -jnp.inf)
            inter_log = b_col + m_old
            m_t = jnp.maximum(inter_log, jnp.max(d_log, axis=1, keepdims=True))
            qh = q_ref[:, h * ML_DQK:(h + 1) * ML_DQK] * (ML_DQK ** -0.5)
            kh = k_ref[:, h * ML_DQK:(h + 1) * ML_DQK]
            vh = v_ref[:, h * ML_DV:(h + 1) * ML_DV].astype(BF16)
            qb16 = qh.astype(BF16)
            s_mat = _dot_t(qb16, kh.astype(BF16)) * jnp.exp(d_log - m_t)
            inter = jnp.exp(inter_log - m_t)
            c_old = c_ref[st]
            n_old = n_ref[st, 0:1, :]
            num = _dot(s_mat.astype(BF16), vh) + inter * _dot(qb16, c_old.astype(BF16))
            den = jnp.sum(s_mat, axis=1, keepdims=True) + inter * jnp.sum(qh * n_old, axis=1, keepdims=True)
            o_ref[:, h * ML_DV:(h + 1) * ML_DV] = num / jnp.maximum(jnp.abs(den), jnp.exp(-m_t))
            w_log = total - b_col + ig_col
            m_new = jnp.maximum(total + m_old, jnp.max(w_log, axis=0, keepdims=True))
            w = jnp.exp(w_log - m_new)
            decay = jnp.exp(total + m_old - m_new)
            kw = kh * w
            c_ref[st] = decay * c_old + _dot(kw.T.astype(BF16), vh)
            n_ref[st, 0:1, :] = decay * n_old + jnp.sum(kw, axis=0, keepdims=True)
            m_ref[st] = jnp.broadcast_to(m_new, m_ref.shape[1:])


def _mlstm(st, proj, gate_bias):
    B = st.B
    assert st.L % ML_TC == 0 and st.C % ML_TC == 0
    cc = st.C // ML_TC
    nch = st.ltot // ML_TC
    qw = ML_HEADS * ML_DQK
    vw = ML_HEADS * ML_DV
    gcol = (2 * qw + 2 * vw) // 128

    def rb(d, b, k):
        chunk = k if d == 0 else jnp.where(k < cc, cc - 1 - k, nch - 1 - (k - cc))
        return b * nch + chunk

    def specs(d):
        return [pl.BlockSpec((ML_TC, qw), lambda b, k: (rb(d, b, k), 0)),
                pl.BlockSpec((ML_TC, qw), lambda b, k: (rb(d, b, k), 1)),
                pl.BlockSpec((ML_TC, vw), lambda b, k: (rb(d, b, k), (2 * qw) // vw)),
                pl.BlockSpec((ML_TC, 128), lambda b, k: (rb(d, b, k), gcol))]

    nst = 2 * ML_HEADS
    out = jax.ShapeDtypeStruct((st.NT, vw), F32)
    tril = jnp.tril(jnp.ones((ML_TC, ML_TC), BF16))
    return pl.pallas_call(
        _mlstm_kernel,
        grid=(B, nch),
        in_specs=specs(0) + specs(1) + [pl.BlockSpec((1, 128), lambda b, k: (0, 0)),
                                        pl.BlockSpec((ML_TC, ML_TC), lambda b, k: (0, 0)),
                                        pl.BlockSpec((ML_TC, ML_TC), lambda b, k: (0, 0))],
        out_specs=[pl.BlockSpec((ML_TC, vw), lambda b, k: (rb(0, b, k), 0)),
                   pl.BlockSpec((ML_TC, vw), lambda b, k: (rb(1, b, k), 0))],
        out_shape=[out, out],
        scratch_shapes=[pltpu.VMEM((nst, ML_DQK, ML_DV), F32),
                        pltpu.VMEM((nst, 8, ML_DQK), F32),
                        pltpu.VMEM((nst, 8, 128), F32)],
        compiler_params=_cparams(("arbitrary", "arbitrary")),
        name="mlstm_chunks",
    )(proj, proj, proj, proj, proj, proj, proj, proj, gate_bias, tril, tril.T)


def _route(lg):
    lane_i = lax.broadcasted_iota(jnp.int32, lg.shape, 1)
    lane = lane_i.astype(F32)
    neg = -jnp.inf
    gl = jnp.where(lane_i < MOE_GROUPS, lg, neg)
    gmax = jnp.max(gl, axis=-1, keepdims=True)
    gsum = jnp.sum(jnp.where(lane_i < MOE_GROUPS, jnp.exp(lg - gmax), 0.0), axis=-1, keepdims=True)
    p_top = 1.0 / gsum
    g_sel = jnp.min(jnp.where(gl == gmax, lane, 128.0), axis=-1, keepdims=True)
    group_of_lane = (lane_i >> 3).astype(F32) - 1.0
    el = jnp.where(group_of_lane == g_sel, lg, neg)
    e1 = jnp.max(el, axis=-1, keepdims=True)
    i1 = jnp.min(jnp.where(el == e1, lane, 128.0), axis=-1, keepdims=True)
    el2 = jnp.where(lane == i1, neg, el)
    e2 = jnp.max(el2, axis=-1, keepdims=True)
    i2 = jnp.min(jnp.where(el2 == e2, lane, 128.0), axis=-1, keepdims=True)
    t = jnp.exp(e2 - e1)
    w1 = p_top / (1.0 + t)
    w2 = w1 * t
    id1 = i1 - MOE_GROUPS
    id2 = i2 - MOE_GROUPS
    table = jnp.where(lane_i == 0, id1, jnp.where(lane_i == 1, id2,
                                                  jnp.where(lane_i == 2, w1, jnp.where(lane_i == 3, w2, 0.0))))
    chosen = jnp.where(lane == i1, 1.0, 0.0) + jnp.where(lane == i2, 1.0, 0.0)
    return table, jnp.sum(chosen, axis=0, keepdims=True)


def _accumulate_counts(cnt_ref, hist):
    @pl.when(pl.program_id(0) == 0)
    def _():
        cnt_ref[...] = jnp.zeros_like(cnt_ref)
    cnt_ref[...] += hist


def _lhs_mla(o_ref):
    return o_ref[...]


def _lhs_mlstm(hf_ref, hb_ref, og_ref, onorm_ref):
    hs = hf_ref[...] + hb_ref[...]
    og = _sigmoid(og_ref[...])
    parts = []
    for h in range(ML_HEADS):
        sl = slice(h * ML_DV, (h + 1) * ML_DV)
        parts.append(_rms(hs[:, sl]) * onorm_ref[:, sl] * og[:, sl])
    return jnp.concatenate(parts, axis=1)


def _mixer_out_kernel(*refs, n_lhs, lhs_fn):
    lhs_refs = refs[:n_lhs]
    w_ref, x_ref, mod_ref, gain_ref, wr_ref, br_ref, xo_ref, f_ref, r_ref, cnt_ref = refs[n_lhs:]
    y = _dot(lhs_fn(*lhs_refs).astype(BF16), w_ref[...])
    mod = mod_ref[0]
    xn = x_ref[...] + mod[2:3, :] * y
    xo_ref[...] = xn
    f = _norm_mod(xn, gain_ref[...], mod[3:4, :], mod[4:5, :])
    f_ref[...] = f
    r_ref[...], hist = _route(_dot3(f, wr_ref[...]) + br_ref[...])
    _accumulate_counts(cnt_ref, hist)


def _mixer_out(rows, lhs_fn, lhs_args, lhs_specs, w_out, x, mod, gain, wr, br, name):
    n = rows.n * TM
    const = lambda k: (0, 0)
    out = lambda w: pl.BlockSpec((TM, w), lambda k: (k, 0))
    return pl.pallas_call(
        functools.partial(_mixer_out_kernel, n_lhs=len(lhs_args), lhs_fn=lhs_fn),
        grid=(rows.n,),
        in_specs=list(lhs_specs) + [
            pl.BlockSpec((D_MODEL, D_MODEL), const),
            pl.BlockSpec((TM, D_MODEL), lambda k: (rows.src(k), 0)),
            pl.BlockSpec((1, 8, D_MODEL), lambda k: (rows.mod(k), 0, 0)),
            pl.BlockSpec((1, D_MODEL), const),
            pl.BlockSpec((D_MODEL, 128), const),
            pl.BlockSpec((1, 128), const)],
        out_specs=[out(D_MODEL), out(D_MODEL), out(128), pl.BlockSpec((1, 128), const)],
        out_shape=[jax.ShapeDtypeStruct((n, D_MODEL), F32),
                   jax.ShapeDtypeStruct((n, D_MODEL), F32),
                   jax.ShapeDtypeStruct((n, 128), F32),
                   jax.ShapeDtypeStruct((1, 128), F32)],
        compiler_params=_cparams(("arbitrary",)),
        name=name,
    )(*lhs_args, w_out, x, mod, gain, wr, br)


RANK_TILES = 4


def _rank_kernel(r_ref, start_ref, tril_ref, pos_ref, carry_ref):
    @pl.when(pl.program_id(0) == 0)
    def _():
        carry_ref[...] = jnp.zeros_like(carry_ref)

    r = r_ref[...]
    lane = lax.broadcasted_iota(jnp.int32, r.shape, 1).astype(F32)
    chosen = [jnp.where(lane == r[:, k:k + 1] + MOE_GROUPS, 1.0, 0.0) for k in range(MOE_TOPK)]
    both = functools.reduce(jnp.add, chosen)
    before = _dot(tril_ref[...], both.astype(BF16)) + (start_ref[...] + carry_ref[...])
    lane_i = lax.broadcasted_iota(jnp.int32, r.shape, 1)
    pos = jnp.zeros(r.shape, F32)
    for k in range(MOE_TOPK):
        pos = jnp.where(lane_i == k, jnp.sum(chosen[k] * before, axis=-1, keepdims=True), pos)
    pos_t = pos.T[0:8, :].astype(jnp.int32)
    for j in range(RANK_TILES):
        pos_ref[j] = pos_t[:, j * TM:(j + 1) * TM]
    carry_ref[...] += jnp.sum(both, axis=0, keepdims=True)


def _assignment_slots(ntiles, route, starts):
    assert ntiles % RANK_TILES == 0
    rows = RANK_TILES * TM
    tril = jnp.tril(jnp.ones((rows, rows), BF16), -1)
    return pl.pallas_call(
        _rank_kernel,
        grid=(ntiles // RANK_TILES,),
        in_specs=[pl.BlockSpec((rows, 128), lambda i: (i, 0)),
                  pl.BlockSpec((1, 128), lambda i: (0, 0)),
                  pl.BlockSpec((rows, rows), lambda i: (0, 0))],
        out_specs=pl.BlockSpec((RANK_TILES, 8, TM), lambda i: (i, 0, 0)),
        out_shape=jax.ShapeDtypeStruct((ntiles, 8, TM), jnp.int32),
        scratch_shapes=[pltpu.VMEM((1, 128), F32)],
        compiler_params=_cparams(("arbitrary",)),
        name="moe_rank",
    )(route, starts, tril)


def _row_wait(hbm, buf, sem):
    pltpu.make_async_copy(hbm.at[pl.ds(0, TM), :], buf, sem).wait()


def _dispatch_kernel(pos_ref, f_ref, xs_hbm, buf, sem, *, ntiles):
    i = pl.program_id(0)

    def step(s):
        @pl.when(i >= 2)
        def _():
            for _ in range(MOE_TOPK):
                _row_wait(xs_hbm, buf.at[s], sem.at[s])
        buf[s] = f_ref[...]
        for r in range(TM):
            for k in range(MOE_TOPK):
                pltpu.make_async_copy(buf.at[s, pl.ds(r, 1), :], xs_hbm.at[pl.ds(pos_ref[0, k, r], 1), :],
                                      sem.at[s]).start(priority=k % 2)

        @pl.when(i == ntiles - 1)
        def _():
            for slot in ((1 - s, s) if ntiles >= 2 else (s,)):
                for _ in range(MOE_TOPK):
                    _row_wait(xs_hbm, buf.at[slot], sem.at[slot])

    for s in range(2):
        pl.when(i % 2 == s)(functools.partial(step, s))


def _dispatch(ntiles, f, pos):
    n = ntiles * TM
    return pl.pallas_call(
        functools.partial(_dispatch_kernel, ntiles=ntiles),
        grid=(ntiles,),
        in_specs=[pl.BlockSpec((1, 8, TM), lambda i: (i, 0, 0), memory_space=pltpu.SMEM),
                  pl.BlockSpec((TM, D_MODEL), lambda i: (i, 0))],
        out_specs=pl.BlockSpec(memory_space=pl.ANY),
        out_shape=jax.ShapeDtypeStruct((MOE_TOPK * n, D_MODEL), F32),
        scratch_shapes=[pltpu.VMEM((2, TM, D_MODEL), F32), pltpu.SemaphoreType.DMA((2,))],
        compiler_params=_cparams(("arbitrary",)),
        name="moe_dispatch",
    )(pos, f)


def _expert_kernel(vb_ref, ve_ref, lo_ref, hi_ref, first_ref, newexp_ref, eslot_ref, enext_ref, x_ref, wgu_hbm,
                   wd_hbm, y_ref, wgu_f, wd_f, wgu_b, wd_b, sem, *, layer):
    v = pl.program_id(0)

    def fetch(e, s):
        return (pltpu.make_async_copy(wgu_hbm.at[layer, e], wgu_f.at[s], sem.at[0, s]),
                pltpu.make_async_copy(wd_hbm.at[layer, e], wd_f.at[s], sem.at[1, s]))

    @pl.when(newexp_ref[v] == 1)
    def _():
        for s in range(2):
            @pl.when(eslot_ref[v] == s)
            def _():
                @pl.when(v == 0)
                def _():
                    for c in fetch(ve_ref[v], s):
                        c.start()

                @pl.when(enext_ref[v] >= 0)
                def _():
                    for c in fetch(enext_ref[v], 1 - s):
                        c.start()
                for c in fetch(ve_ref[v], s):
                    c.wait()
                wgu_b[...] = wgu_f[s].astype(BF16)
                wd_b[...] = wd_f[s].astype(BF16)

    @pl.when(hi_ref[v] > lo_ref[v])
    def _():
        gu = _dot(x_ref[...].astype(BF16), wgu_b[...])
        gate = gu[:, :MOE_FF]
        act = gate * _sigmoid(gate) * gu[:, MOE_FF:]
        y = _dot(act.astype(BF16), wd_b[...])
        r = lax.broadcasted_iota(jnp.int32, (MOE_BM, 1), 0)
        mine = jnp.logical_and(r >= lo_ref[v], r < hi_ref[v])
        y = jnp.where(mine, y, 0.0)

        @pl.when(first_ref[v] == 1)
        def _():
            y_ref[...] = y

        @pl.when(first_ref[v] == 0)
        def _():
            y_ref[...] += y


def _expert_ffn(xs, visits, layer, w_gate_up, w_down):
    nvis = visits[0].shape[0]
    blk_idx = lambda v, vb, *_: (vb[v], 0)
    grid_spec = pltpu.PrefetchScalarGridSpec(
        num_scalar_prefetch=8,
        grid=(nvis,),
        in_specs=[pl.BlockSpec((MOE_BM, D_MODEL), blk_idx),
                  pl.BlockSpec(memory_space=pl.ANY),
                  pl.BlockSpec(memory_space=pl.ANY)],
        out_specs=pl.BlockSpec((MOE_BM, D_MODEL), blk_idx),
        scratch_shapes=[pltpu.VMEM((2, D_MODEL, 2 * MOE_FF), F32), pltpu.VMEM((2, MOE_FF, D_MODEL), F32),
                        pltpu.VMEM((D_MODEL, 2 * MOE_FF), BF16), pltpu.VMEM((MOE_FF, D_MODEL), BF16),
                        pltpu.SemaphoreType.DMA((2, 2))],
    )
    return pl.pallas_call(
        functools.partial(_expert_kernel, layer=layer),
        grid_spec=grid_spec,
        out_shape=jax.ShapeDtypeStruct(xs.shape, F32),
        compiler_params=_cparams(("arbitrary",)),
        name="moe_expert_ffn",
    )(*visits, xs, w_gate_up, w_down)


def _combine_kernel(pos_ref, nxt_ref, x_ref, r_ref, mod_ref, ys_hbm, o_ref, ybuf, sem, *, ntiles):
    i = pl.program_id(0)

    def gather(table, s):
        for r in range(TM):
            for k in range(MOE_TOPK):
                pltpu.make_async_copy(ys_hbm.at[pl.ds(table[0, k, r], 1), :], ybuf.at[s, k, pl.ds(r, 1), :],
                                      sem.at[s]).start(priority=k % 2)

    @pl.when(i == 0)
    def _():
        gather(pos_ref, 0)

    def step(s):
        @pl.when(i + 1 < ntiles)
        def _():
            gather(nxt_ref, 1 - s)
        for k in range(MOE_TOPK):
            _row_wait(ys_hbm, ybuf.at[s, k], sem.at[s])
        w = r_ref[...]
        y = w[:, MOE_TOPK:MOE_TOPK + 1] * ybuf[s, 0]
        for k in range(1, MOE_TOPK):
            y = y + w[:, MOE_TOPK + k:MOE_TOPK + k + 1] * ybuf[s, k]
        o_ref[...] = x_ref[...] + mod_ref[0][5:6, :] * y

    for s in range(2):
        pl.when(i % 2 == s)(functools.partial(step, s))


def _combine(rows, x, ys, pos, route, mod):
    n = rows.n
    spec = pl.BlockSpec((TM, D_MODEL), lambda i: (i, 0))
    return pl.pallas_call(
        functools.partial(_combine_kernel, ntiles=n),
        grid=(n,),
        in_specs=[pl.BlockSpec((1, 8, TM), lambda i: (i, 0, 0), memory_space=pltpu.SMEM),
                  pl.BlockSpec((1, 8, TM), lambda i: (jnp.minimum(i + 1, n - 1), 0, 0), memory_space=pltpu.SMEM),
                  spec,
                  pl.BlockSpec((TM, 128), lambda i: (i, 0)),
                  pl.BlockSpec((1, 8, D_MODEL), lambda i: (rows.mod(i), 0, 0)),
                  pl.BlockSpec(memory_space=pl.ANY)],
        out_specs=spec,
        out_shape=jax.ShapeDtypeStruct((n * TM, D_MODEL), F32),
        scratch_shapes=[pltpu.VMEM((2, MOE_TOPK, TM, D_MODEL), F32), pltpu.SemaphoreType.DMA((2,))],
        compiler_params=_cparams(("arbitrary",)),
        name="moe_combine",
    )(pos, pos, x, route, mod, ys)


def _visit_tables(bounds, nk):
    E = MOE_EXPERTS
    nblk = nk // MOE_BM
    nvis = nblk + E
    starts, ends = bounds[:-1], bounds[1:]
    fb = starts // MOE_BM
    nv = jnp.where(ends > starts, (ends - 1) // MOE_BM - fb + 1, 0)
    cum = jnp.cumsum(nv)
    total = cum[-1]
    v = jnp.arange(nvis, dtype=jnp.int32)
    active = v < total
    vc = jnp.minimum(v, total - 1)
    ve = jnp.minimum(jnp.sum((cum[None, :] <= vc[:, None]).astype(jnp.int32), axis=1), E - 1)
    vb = fb[ve] + (vc - (cum - nv)[ve])
    lo = jnp.where(active, jnp.maximum(starts[ve], vb * MOE_BM) - vb * MOE_BM, 0)
    hi = jnp.where(active, jnp.minimum(ends[ve], (vb + 1) * MOE_BM) - vb * MOE_BM, 0)
    prev_b = jnp.concatenate([jnp.full((1,), -1, jnp.int32), vb[:-1]])
    first = jnp.logical_and(active, vb != prev_b)
    prev_e = jnp.concatenate([jnp.full((1,), -1, jnp.int32), ve[:-1]])
    new_expert = jnp.logical_and(active, ve != prev_e)
    eslot = (jnp.cumsum(new_expert.astype(jnp.int32)) - 1) % 2
    at = jnp.where(new_expert, v, nvis)
    nxt = jnp.concatenate([lax.cummin(at[::-1])[::-1][1:], jnp.full((1,), nvis, jnp.int32)])
    enext = jnp.where(nxt < nvis, ve[jnp.minimum(nxt, nvis - 1)], -1)
    i32 = lambda a: a.astype(jnp.int32)
    return i32(vb), i32(ve), i32(lo), i32(hi), i32(first), i32(new_expert), i32(eslot), i32(enext)


def _moe(rows, x, f, route, counts, mod, layer, w_gate_up, w_down):
    n = rows.n * TM
    cum = jnp.cumsum(counts[0])
    starts = (cum - counts[0]).reshape(1, 128)
    bounds = jnp.concatenate([starts[0, MOE_GROUPS:MOE_GROUPS + MOE_EXPERTS], cum[-1:]]).astype(jnp.int32)
    pos = _assignment_slots(rows.n, route, starts)
    xs = _dispatch(rows.n, f, pos)
    ys = _expert_ffn(xs, _visit_tables(bounds, n * MOE_TOPK), layer, w_gate_up, w_down)
    return _combine(rows, x, ys, pos, route, mod)


def kernel(x, c, ctx, c_ctx, ada_w, ada_b, norm_mix, norm_ffn, rg_w_in, rg_conv_w, rg_conv_b, rg_gate_w, rg_gate_b, rg_lambda, rg_w_out, mla_w_down, mla_q_norm, mla_kv_norm, mla_w_uq, mla_w_ukv, mla_qk_norm, mla_w_o, ml_w_in, ml_gate_b, ml_out_norm, ml_w_out, moe_w_group, moe_b_group, moe_w_expert, moe_b_expert, moe_w_gate_up, moe_w_down):
    B, L, D = x.shape
    C = ctx.shape[1]
    depth = ada_w.shape[0]
    assert D == D_MODEL
    st = _Stream(B, L, C)

    xs = jnp.concatenate([ctx, x], axis=1).reshape(st.NT, D)
    cc = jnp.zeros((16, D), F32).at[:B].set(c).at[B].set(c_ctx)
    mod_all = _modulation(cc, ada_w, ada_b)
    mod_all = jnp.pad(mod_all[:, :B + 1].reshape(depth, B + 1, 6, D), ((0, 0), (0, 0), (0, 2), (0, 0)))
    perm = _row_permutation(B)

    row = lambda a: a.reshape(1, -1)

    for i in range(depth):
        last = i == depth - 1
        mod = mod_all[i]
        kind, j = i % 3, i // 3
        all_rows = st.all_rows()
        out_rows = st.latent_rows() if last else all_rows
        tile_spec = lambda w: pl.BlockSpec((TM, w), lambda k: (out_rows.src(k), 0))
        wr = jnp.zeros((D, 128), F32).at[:, :MOE_GROUPS].set(moe_w_group[i]) \
            .at[:, MOE_GROUPS:MOE_GROUPS + MOE_EXPERTS].set(moe_w_expert[i])
        br = jnp.zeros((1, 128), F32).at[0, :MOE_GROUPS].set(moe_b_group[i]) \
            .at[0, MOE_GROUPS:MOE_GROUPS + MOE_EXPERTS].set(moe_b_expert[i])
        out_args = (xs, mod, row(norm_ffn[i]), wr, br)

        if kind == 0:
            gate, u = _rg_in(st, xs, mod, row(norm_mix[i]), perm, rg_w_in[j].astype(BF16))
            wg, gb = _rg_gate_weights(rg_gate_w[j], rg_gate_b[j])
            hs = _rg_scan(st, u, rg_conv_w[j], row(rg_conv_b[j]), wg, gb, rg_lambda[j].reshape(2, 1, D))
            xs, f, route, counts = _rg_out(st, last, gate, hs, perm.T, rg_w_out[j].astype(BF16), *out_args)
        elif kind == 1:
            w_down = jnp.pad(mla_w_down[j], ((0, 0), (0, 512 - mla_w_down.shape[2]))).astype(BF16)
            down = _norm_proj(all_rows, xs, mod, row(norm_mix[i]), w_down, name="mla_down_proj")
            wq, wk, we, wv = _mla_weights(mla_w_uq[j], mla_w_ukv[j])
            q, k, v = _mla_up(st, all_rows, down, row(mla_q_norm[j]), row(mla_kv_norm[j]), wq, wk, we, wv,
                              _rope_tables(L, mla_qk_norm[j]))
            o = _attention(st, q, k, v)
            xs, f, route, counts = _mixer_out(out_rows, _lhs_mla, (o,), (tile_spec(D),),
                                      mla_w_o[j].astype(BF16), *out_args, name="mla_out")
        else:
            n_in = ml_w_in.shape[2]
            w_in = jnp.pad(ml_w_in[j], ((0, 0), (0, ML_NP - n_in))).astype(BF16)
            proj = _norm_proj(all_rows, xs, mod, row(norm_mix[i]), w_in, name="mlstm_in_proj")
            gate_bias = jnp.pad(ml_gate_b[j].reshape(1, -1), ((0, 0), (0, 128 - 4 * ML_HEADS)))
            hf, hb = _mlstm(st, proj, gate_bias)
            og_spec = pl.BlockSpec((TM, D), lambda k: (out_rows.src(k), 2))
            xs, f, route, counts = _mixer_out(out_rows, _lhs_mlstm, (hf, hb, proj, row(ml_out_norm[j])),
                                      (tile_spec(D), tile_spec(D), og_spec, pl.BlockSpec((1, D), lambda k: (0, 0))),
                                      ml_w_out[j].astype(BF16), *out_args, name="mlstm_out")

        moe_rows = st.dense_latent_rows() if last else all_rows
        xs = _moe(moe_rows, xs, f, route, counts, mod, i, moe_w_gate_up, moe_w_down)

    return xs.reshape(B, L, D)
```

```python
import functools
import math

import jax
import jax.numpy as jnp
from jax import lax
from jax.experimental import pallas as pl
from jax.experimental.pallas import tpu as pltpu

F32 = jnp.float32
BF16 = jnp.bfloat16

D_MODEL = 1024
RMS_EPS = 1e-6

TM = 256
VMEM_LIMIT = 48 * 1024 * 1024

RG_BLOCK_W = 64
RG_CHUNK = 256
RG_CONV_W = 4
RG_C = 8.0
RG_TT = 64

MLA_HEADS = 16
MLA_Q_RANK = 256
MLA_KV_RANK = 128
MLA_NOPE = 64
MLA_ROPE = 32
MLA_V = 64
MLA_QK = MLA_NOPE + MLA_ROPE
MLA_HP = 128
ROPE_AXIS_DIM = MLA_ROPE // 2
ROPE_BASE = 10000.0
GRID_W = 64
ATT_HEADS = 8

ML_HEADS = 4
ML_DV = 256
ML_DQK = 128
ML_TC = 256
ML_M_INIT = -1e30
ML_NP = 3200

MOE_GROUPS = 8
MOE_PER_GROUP = 8
MOE_EXPERTS = 64
MOE_TOPK = 2
MOE_FF = 256
MOE_BM = 512


def _cparams(sem):
    return pltpu.CompilerParams(dimension_semantics=sem, vmem_limit_bytes=VMEM_LIMIT)


def _dot(a, b):
    return jnp.dot(a, b, preferred_element_type=F32)


def _dot_t(a, b):
    return lax.dot_general(a, b, (((1,), (1,)), ((), ())), preferred_element_type=F32)


def _dot3(a, b):
    ah = a.astype(BF16)
    al = (a - ah.astype(F32)).astype(BF16)
    bh = b.astype(BF16)
    bl = (b - bh.astype(F32)).astype(BF16)
    return _dot(ah, bh) + (_dot(al, bh) + _dot(ah, bl))


def _dot3_split(a, w2):
    n = w2.shape[1] // 2
    ah = a.astype(BF16)
    al = (a - ah.astype(F32)).astype(BF16)
    both = _dot(ah, w2)
    return both[:, :n] + (_dot(al, w2[:, :n]) + both[:, n:])


def _sigmoid(x):
    return 0.5 * jnp.tanh(0.5 * x) + 0.5


def _softplus(x):
    return jnp.maximum(x, 0.0) + jnp.log1p(jnp.exp(-jnp.abs(x)))


def _gelu_tanh(x):
    return 0.5 * x * (1.0 + jnp.tanh(0.7978845608028654 * (x + 0.044715 * (x * x * x))))


def _rms(x, n=None):
    n = x.shape[-1] if n is None else n
    ms = jnp.sum(x * x, axis=-1, keepdims=True) * (1.0 / n)
    return x * lax.rsqrt(ms + RMS_EPS)


class _Rows:
    def __init__(self, n, src, mod):
        self.n, self.src, self.mod = n, src, mod


class _Stream:
    def __init__(self, B, L, C):
        assert L % TM == 0 and C % TM == 0 and B % 8 == 0
        self.B, self.L, self.C = B, L, C
        self.ltot = L + C
        self.lt, self.ct = L // TM, C // TM
        self.tpb = self.lt + self.ct
        self.NT = B * self.ltot

    def all_rows(self):
        tpb, ct, B = self.tpb, self.ct, self.B
        return _Rows(B * tpb, lambda k: k, lambda k: jnp.where(k % tpb < ct, B, k // tpb))

    def latent_rows(self):
        tpb, ct, lt = self.tpb, self.ct, self.lt
        return _Rows(self.B * lt, lambda k: (k // lt) * tpb + ct + k % lt, lambda k: k // lt)

    def dense_latent_rows(self):
        lt = self.lt
        return _Rows(self.B * lt, lambda k: k, lambda k: k // lt)


def _mod_kernel(c_ref, w_ref, b_ref, o_ref):
    c = c_ref[...]
    o_ref[0] = _dot3(c * _sigmoid(c), w_ref[0]) + b_ref[0]


def _modulation(cc, ada_w, ada_b):
    depth, d, n = ada_w.shape
    tn = 1536
    return pl.pallas_call(
        _mod_kernel,
        grid=(depth, n // tn),
        in_specs=[pl.BlockSpec((16, d), lambda l, j: (0, 0)),
                  pl.BlockSpec((1, d, tn), lambda l, j: (l, 0, j)),
                  pl.BlockSpec((1, 1, tn), lambda l, j: (l, 0, j))],
        out_specs=pl.BlockSpec((1, 16, tn), lambda l, j: (l, 0, j)),
        out_shape=jax.ShapeDtypeStruct((depth, 16, n), F32),
        compiler_params=_cparams(("arbitrary", "arbitrary")),
        name="ada_modulation",
    )(cc, ada_w, ada_b.reshape(depth, 1, n))


def _norm_mod(x, gain, shift, scale):
    return _rms(x) * gain * (1.0 + scale) + shift


def _norm_proj_kernel(x_ref, mod_ref, g_ref, w_ref, o_ref):
    mod = mod_ref[0]
    h = _norm_mod(x_ref[...], g_ref[...], mod[0:1, :], mod[1:2, :]).astype(BF16)
    o_ref[...] = _dot(h, w_ref[...]).astype(o_ref.dtype)


def _norm_proj(rows, x, mod, gain, w, name):
    n = w.shape[1]
    const = lambda k: (0, 0)
    return pl.pallas_call(
        _norm_proj_kernel,
        grid=(rows.n,),
        in_specs=[pl.BlockSpec((TM, D_MODEL), lambda k: (rows.src(k), 0)),
                  pl.BlockSpec((1, 8, D_MODEL), lambda k: (rows.mod(k), 0, 0)),
                  pl.BlockSpec((1, D_MODEL), const),
                  pl.BlockSpec((D_MODEL, n), const)],
        out_specs=pl.BlockSpec((TM, n), lambda k: (k, 0)),
        out_shape=jax.ShapeDtypeStruct((rows.n * TM, n), F32),
        compiler_params=_cparams(("arbitrary",)),
        name=name,
    )(x, mod, gain, w)


RG_TR = 32


def _row_permutation(B):
    n = B * RG_TR
    r = jnp.arange(n)
    src = (r % B) * RG_TR + r // B
    return (src[:, None] == jnp.arange(n)[None, :]).astype(BF16)


def _time_tile_mod(mod_ref, is_ctx, B, row):
    return jnp.where(is_ctx, mod_ref[B:B + 1, row:row + 1, :], mod_ref[0:B, row:row + 1, :])


def _rg_in_kernel(x_ref, mod_ref, g_ref, perm_ref, wg_ref, wu_ref, gate_ref, u_ref, *, B, ctx_tiles):
    is_ctx = pl.program_id(0) < ctx_tiles
    h = _norm_mod(x_ref[...], g_ref[...], _time_tile_mod(mod_ref, is_ctx, B, 0), _time_tile_mod(mod_ref, is_ctx, B, 1))
    h = h.reshape(B * RG_TR, D_MODEL).astype(BF16)
    h = _dot(perm_ref[...], h).astype(BF16)
    gate_ref[...] = _gelu_tanh(_dot(h, wg_ref[...])).astype(gate_ref.dtype)
    u_ref[...] = _dot(h, wu_ref[...])


def _rg_in(st, x, mod, gain, perm, w_in):
    B, W = st.B, D_MODEL
    R = B * RG_TR
    nt = st.ltot // RG_TR
    const = lambda t: (0, 0)
    return pl.pallas_call(
        functools.partial(_rg_in_kernel, B=B, ctx_tiles=st.C // RG_TR),
        grid=(nt,),
        in_specs=[pl.BlockSpec((B, RG_TR, D_MODEL), lambda t: (0, t, 0)),
                  pl.BlockSpec((B + 1, 8, D_MODEL), lambda t: (0, 0, 0)),
                  pl.BlockSpec((1, D_MODEL), const),
                  pl.BlockSpec((R, R), const),
                  pl.BlockSpec((D_MODEL, W), const),
                  pl.BlockSpec((D_MODEL, W), lambda t: (0, 1))],
        out_specs=[pl.BlockSpec((R, W), lambda t: (t, 0)),
                   pl.BlockSpec((R, W), lambda t: (t, 0))],
        out_shape=[jax.ShapeDtypeStruct((st.ltot * B, W), BF16), jax.ShapeDtypeStruct((st.ltot * B, W), F32)],
        compiler_params=_cparams(("arbitrary",)),
        name="rg_in_proj",
    )(x.reshape(B, st.ltot, D_MODEL), mod, gain, perm, w_in, w_in)


def _rg_tile_order(d, k, ct, ntt):
    bwd = jnp.where(k < ct, ct - 1 - k, ntt - 1 - (k - ct))
    return jnp.where(d == 0, k, bwd)


def _rg_scan_kernel(um_ref, up_ref, un_ref, cw_ref, cb_ref, wg_ref, gb_ref, lam_ref, o_ref,
                    ext_ref, a_ref, b_ref, h_ref, *, B, ct, ntt):
    d = pl.program_id(0)
    k = pl.program_id(1)
    tile = _rg_tile_order(d, k, ct, ntt)
    R = RG_TT * B

    @pl.when(k == 0)
    def _():
        h_ref[...] = jnp.zeros_like(h_ref)

    seq_start = jnp.logical_or(tile == 0, tile == ct)
    seq_end = jnp.logical_or(tile == ct - 1, tile == ntt - 1)
    ext_ref[0:2 * B, :] = jnp.where(seq_start, 0.0, up_ref[...])
    ext_ref[2 * B:2 * B + R, :] = um_ref[...]
    ext_ref[2 * B + R:3 * B + R, :] = jnp.where(seq_end, 0.0, un_ref[...])
    cw = cw_ref[...]
    uc = cb_ref[...] + cw[0:1, :] * ext_ref[0:R, :]
    for j in range(1, RG_CONV_W):
        uc = uc + cw[j:j + 1, :] * ext_ref[j * B:j * B + R, :]
    ucb = uc.astype(BF16)
    c_half = (-0.5 * RG_C) * _softplus(-lam_ref[0])
    for c in range(D_MODEL // RG_CHUNK):
        sl = slice(c * RG_CHUNK, (c + 1) * RG_CHUNK)
        z = _dot(ucb[:, sl], wg_ref[0, c]) + gb_ref[0, c]
        log_a = c_half[:, sl] * jnp.tanh(z[:, :RG_CHUNK]) + c_half[:, sl]
        u_half = 0.5 * uc[:, sl]
        gated_u = jnp.tanh(z[:, RG_CHUNK:]) * u_half + u_half
        a = jnp.exp(log_a)
        one_minus_a2 = -jnp.tanh(log_a) * (a * a + 1.0)
        a_ref[:, sl] = a
        b_ref[:, sl] = jnp.sqrt(one_minus_a2) * gated_u

    def scan(times):
        for c in range(D_MODEL // 128):
            cs = slice(c * 128, (c + 1) * 128)
            h = h_ref[:, cs]
            for t in times:
                rs = slice(t * B, (t + 1) * B)
                h = a_ref[rs, cs] * h + b_ref[rs, cs]
                b_ref[rs, cs] = h
            h_ref[:, cs] = h

    pl.when(d == 0)(lambda: scan(range(RG_TT)))
    pl.when(d == 1)(lambda: scan(range(RG_TT - 1, -1, -1)))
    o_ref[0] = b_ref[...].astype(o_ref.dtype)


def _rg_scan(st, u_tm, conv_w, conv_b, wg, gb, lam):
    B, W = st.B, D_MODEL
    assert st.C % RG_TT == 0 and st.L % RG_TT == 0
    ltot = st.ltot
    ntt, ct = ltot // RG_TT, st.C // RG_TT
    R = RG_TT * B
    order = functools.partial(_rg_tile_order, ct=ct, ntt=ntt)
    nch = W // RG_CHUNK
    return pl.pallas_call(
        functools.partial(_rg_scan_kernel, B=B, ct=ct, ntt=ntt),
        grid=(2, ntt),
        in_specs=[pl.BlockSpec((R, W), lambda d, k: (order(d, k), 0)),
                  pl.BlockSpec((2 * B, W), lambda d, k: (jnp.maximum(order(d, k) * (RG_TT // 2) - 1, 0), 0)),
                  pl.BlockSpec((B, W), lambda d, k: (jnp.minimum((order(d, k) + 1) * RG_TT, ltot - 1), 0)),
                  pl.BlockSpec((RG_CONV_W, W), lambda d, k: (0, 0)),
                  pl.BlockSpec((1, W), lambda d, k: (0, 0)),
                  pl.BlockSpec((1, nch, RG_CHUNK, 2 * RG_CHUNK), lambda d, k: (d, 0, 0, 0)),
                  pl.BlockSpec((1, nch, 1, 2 * RG_CHUNK), lambda d, k: (d, 0, 0, 0)),
                  pl.BlockSpec((1, 1, W), lambda d, k: (d, 0, 0))],
        out_specs=pl.BlockSpec((1, R, W), lambda d, k: (d, order(d, k), 0)),
        out_shape=jax.ShapeDtypeStruct((2, ltot * B, W), BF16),
        scratch_shapes=[pltpu.VMEM((3 * B + R, W), F32),
                        pltpu.VMEM((R, W), F32),
                        pltpu.VMEM((R, W), F32),
                        pltpu.VMEM((B, W), F32)],
        compiler_params=_cparams(("arbitrary", "arbitrary")),
        name="rg_scan",
    )(u_tm, u_tm, u_tm, conv_w, conv_b, wg, gb, lam)


def _rg_gate_weights(gate_w, gate_b):
    nb = gate_w.shape[2]
    per = RG_CHUNK // RG_BLOCK_W
    nch = nb // per
    gw = gate_w.reshape(2, 2, nch, per, RG_BLOCK_W, RG_BLOCK_W)
    eye = jnp.eye(per, dtype=gate_w.dtype)
    bd = jnp.einsum('dgcnij,nm->dgcnimj', gw, eye).reshape(2, 2, nch, RG_CHUNK, RG_CHUNK)
    wg = (0.5 * jnp.concatenate([bd[:, 0], bd[:, 1]], axis=-1)).astype(BF16)
    gb = gate_b.reshape(2, 2, nch, 1, RG_CHUNK)
    gb = 0.5 * jnp.concatenate([gb[:, 0], gb[:, 1]], axis=-1)
    return wg, gb


def _rg_out_kernel(g_ref, hf_ref, hb_ref, perm_ref, w_ref, x_ref, mod_ref, gain_ref, wr_ref, br_ref,
                   xo_ref, f_ref, r_ref, cnt_ref, *, B, ctx_tiles, t0):
    is_ctx = pl.program_id(0) + t0 < ctx_tiles
    hsum = hf_ref[0].astype(F32) + hb_ref[0].astype(F32)
    lhs = (g_ref[...].astype(F32) * hsum).astype(BF16)
    lhs = _dot(perm_ref[...], lhs).astype(BF16)
    y = _dot(lhs, w_ref[...]).reshape(B, RG_TR, D_MODEL)
    m = lambda row: _time_tile_mod(mod_ref, is_ctx, B, row)
    xn = x_ref[...] + m(2) * y
    xo_ref[...] = xn
    f = _norm_mod(xn, gain_ref[...], m(3), m(4))
    f_ref[...] = f
    lg = _dot3_split(f.reshape(B * RG_TR, D_MODEL), wr_ref[...]) + br_ref[...]
    table, hist = _route(lg)
    r_ref[...] = table.reshape(B, RG_TR, 128)
    _accumulate_counts(cnt_ref, hist)


def _rg_out(st, latent_only, gate, hs, perm, w_out, x, mod, gain, wr, br):
    B = st.B
    R = B * RG_TR
    t0 = st.C // RG_TR if latent_only else 0
    lo = st.L if latent_only else st.ltot
    nt = lo // RG_TR
    const = lambda t: (0, 0)
    blk = lambda w: pl.BlockSpec((B, RG_TR, w), lambda t: (0, t, 0))
    xo, f, route, counts = pl.pallas_call(
        functools.partial(_rg_out_kernel, B=B, ctx_tiles=st.C // RG_TR, t0=t0),
        grid=(nt,),
        in_specs=[pl.BlockSpec((R, D_MODEL), lambda t: (t + t0, 0)),
                  pl.BlockSpec((1, R, D_MODEL), lambda t: (0, t + t0, 0)),
                  pl.BlockSpec((1, R, D_MODEL), lambda t: (1, t + t0, 0)),
                  pl.BlockSpec((R, R), const),
                  pl.BlockSpec((D_MODEL, D_MODEL), const),
                  pl.BlockSpec((B, RG_TR, D_MODEL), lambda t: (0, t + t0, 0)),
                  pl.BlockSpec((B + 1, 8, D_MODEL), lambda t: (0, 0, 0)),
                  pl.BlockSpec((1, D_MODEL), const),
                  pl.BlockSpec((D_MODEL, 256), const),
                  pl.BlockSpec((1, 128), const)],
        out_specs=[blk(D_MODEL), blk(D_MODEL), blk(128), pl.BlockSpec((1, 128), const)],
        out_shape=[jax.ShapeDtypeStruct((B, lo, D_MODEL), F32),
                   jax.ShapeDtypeStruct((B, lo, D_MODEL), F32),
                   jax.ShapeDtypeStruct((B, lo, 128), F32),
                   jax.ShapeDtypeStruct((1, 128), F32)],
        compiler_params=_cparams(("arbitrary",)),
        name="rg_out",
    )(gate, hs, hs, perm, w_out, x.reshape(B, st.ltot, D_MODEL), mod, gain, wr, br)
    return xo.reshape(B * lo, D_MODEL), f.reshape(B * lo, D_MODEL), route.reshape(B * lo, 128), counts


_ROPE_HALF = ROPE_AXIS_DIM // 2
_MLA_SRC_DIM = (list(range(MLA_NOPE + ROPE_AXIS_DIM)) + list(range(MLA_NOPE, MLA_NOPE + _ROPE_HALF))
                + list(range(MLA_NOPE + ROPE_AXIS_DIM, MLA_QK))
                + list(range(MLA_NOPE + ROPE_AXIS_DIM, MLA_NOPE + ROPE_AXIS_DIM + _ROPE_HALF)))
_MLA_REAL_LANE = ([1.0] * (MLA_NOPE + ROPE_AXIS_DIM) + [0.0] * _ROPE_HALF + [1.0] * ROPE_AXIS_DIM
                  + [0.0] * _ROPE_HALF + [0.0] * (MLA_HP - len(_MLA_SRC_DIM)))


def _mla_up_kernel(dn_ref, qn_ref, kvn_ref, wq_ref, wk_ref, we_ref, wv_ref, real_ref, cq_ref, sq_ref, ck_ref, sk_ref,
                   q_ref, k_ref, v_ref):
    dn = dn_ref[...]
    cq = _rms(dn[:, :MLA_Q_RANK]) * qn_ref[...]
    ckv = _rms(dn[:, MLA_Q_RANK:MLA_Q_RANK + MLA_KV_RANK]) * kvn_ref[...]
    kr = dn[:, MLA_Q_RANK + MLA_KV_RANK:]
    kr_hi = kr.astype(BF16)
    kr_lo = (kr - kr_hi.astype(F32)).astype(BF16)
    ckvb = ckv.astype(BF16)
    q_pre = _dot(cq.astype(BF16), wq_ref[...])
    k_pre = _dot(ckvb, wk_ref[...]) + (_dot(kr_hi, we_ref[...]) + _dot(kr_lo, we_ref[...]))
    v_ref[...] = _dot(ckvb, wv_ref[...]).astype(v_ref.dtype)
    real = real_ref[...]

    def head(x, cos_g, sin_g):
        ms = jnp.sum(x * x * real, axis=-1, keepdims=True) * (1.0 / MLA_QK)
        xr = x * lax.rsqrt(ms + RMS_EPS)
        return xr * cos_g + pltpu.roll(xr, MLA_HP - _ROPE_HALF, 1) * sin_g

    cos_q, sin_q, cos_k, sin_k = cq_ref[...], sq_ref[...], ck_ref[...], sk_ref[...]
    for h in range(MLA_HEADS):
        sl = slice(h * MLA_HP, (h + 1) * MLA_HP)
        q_ref[:, sl] = head(q_pre[:, sl], cos_q, sin_q).astype(q_ref.dtype)
        k_ref[:, sl] = head(k_pre[:, sl], cos_k, sin_k).astype(k_ref.dtype)


def _mla_up(st, rows, down, q_norm, kv_norm, wq, wk, we, wv, tables):
    hw = MLA_HEADS * MLA_HP
    const = lambda i: (0, 0)
    tpb, ct, lt = st.tpb, st.ct, st.lt
    rope_idx = lambda i: (jnp.where(i % tpb < ct, lt, i % tpb - ct), 0)
    real = jnp.asarray(_MLA_REAL_LANE, F32).reshape(1, MLA_HP)
    return pl.pallas_call(
        _mla_up_kernel,
        grid=(rows.n,),
        in_specs=[pl.BlockSpec((TM, 512), lambda i: (i, 0)),
                  pl.BlockSpec((1, MLA_Q_RANK), const),
                  pl.BlockSpec((1, MLA_KV_RANK), const),
                  pl.BlockSpec((MLA_Q_RANK, hw), const),
                  pl.BlockSpec((MLA_KV_RANK, hw), const),
                  pl.BlockSpec((128, hw), const),
                  pl.BlockSpec((MLA_KV_RANK, MLA_HEADS * MLA_V), const),
                  pl.BlockSpec((1, MLA_HP), const)] + [pl.BlockSpec((TM, MLA_HP), rope_idx)] * 4,
        out_specs=[pl.BlockSpec((TM, hw), lambda i: (i, 0)),
                   pl.BlockSpec((TM, hw), lambda i: (i, 0)),
                   pl.BlockSpec((TM, MLA_HEADS * MLA_V), lambda i: (i, 0))],
        out_shape=[jax.ShapeDtypeStruct((st.NT, hw), BF16),
                   jax.ShapeDtypeStruct((st.NT, hw), BF16),
                   jax.ShapeDtypeStruct((st.NT, MLA_HEADS * MLA_V), BF16)],
        compiler_params=_cparams(("arbitrary",)),
        name="mla_up_proj",
    )(down, q_norm, kv_norm, wq, wk, we, wv, real, *tables)


def _attn_kernel(q_ref, k_ref, v_ref, o_ref, vaug_ref, *, C, ct):
    qi = pl.program_id(2)

    @pl.when(qi == 0)
    def _():
        lane = lax.broadcasted_iota(jnp.int32, (k_ref.shape[0], 2 * MLA_V), 1)
        for hh in range(ATT_HEADS):
            pair = v_ref[:, (hh // 2) * 2 * MLA_V:(hh // 2 + 1) * 2 * MLA_V].astype(F32)
            if hh % 2 == 0:
                aug = jnp.where(lane < MLA_V, pair, jnp.where(lane == MLA_V, 1.0, 0.0))
            else:
                aug = jnp.where(lane >= MLA_V, pair, jnp.where(lane == 0, 1.0, 0.0))
            vaug_ref[hh] = aug.astype(BF16)

    def attend(nkeys):
        lane = lax.broadcasted_iota(jnp.int32, (TM, 2 * MLA_V), 1)

        def scores(hh):
            sl = slice(hh * MLA_HP, (hh + 1) * MLA_HP)
            return _dot_t(q_ref[:, sl], k_ref[0:nkeys, sl])

        s_next = scores(0)
        outs = []
        for hh in range(ATT_HEADS):
            s = s_next
            if hh + 1 < ATT_HEADS:
                s_next = scores(hh + 1)
            p = jnp.exp2(s - jnp.max(s, axis=-1, keepdims=True)).astype(BF16)
            o = _dot(p, vaug_ref[hh, 0:nkeys, :])
            rowsum = o[:, MLA_V:MLA_V + 1] if hh % 2 == 0 else o[:, 0:1]
            outs.append(o * (1.0 / rowsum))
            if hh % 2 == 1:
                pair = hh // 2
                o_ref[:, pair * 2 * MLA_V:(pair + 1) * 2 * MLA_V] = jnp.where(
                    lane < MLA_V, outs[hh - 1], outs[hh]).astype(o_ref.dtype)

    pl.when(qi < ct)(lambda: attend(C))
    pl.when(qi >= ct)(lambda: attend(k_ref.shape[0]))


def _attention(st, q, k, v):
    B, ltot, tpb = st.B, st.ltot, st.tpb
    hg = MLA_HEADS // ATT_HEADS
    return pl.pallas_call(
        functools.partial(_attn_kernel, C=st.C, ct=st.ct),
        grid=(B, hg, tpb),
        in_specs=[pl.BlockSpec((TM, ATT_HEADS * MLA_HP), lambda b, h, i: (b * tpb + i, h)),
                  pl.BlockSpec((ltot, ATT_HEADS * MLA_HP), lambda b, h, i: (b, h)),
                  pl.BlockSpec((ltot, ATT_HEADS * MLA_V), lambda b, h, i: (b, h))],
        out_specs=pl.BlockSpec((TM, ATT_HEADS * MLA_V), lambda b, h, i: (b * tpb + i, h)),
        out_shape=jax.ShapeDtypeStruct((st.NT, MLA_HEADS * MLA_V), BF16),
        scratch_shapes=[pltpu.VMEM((ATT_HEADS, ltot, 2 * MLA_V), BF16)],
        compiler_params=_cparams(("arbitrary", "arbitrary", "arbitrary")),
        name="mla_attention",
    )(q, k, v)


def _mla_weights(w_uq, w_ukv):
    H = MLA_HEADS
    src = jnp.asarray(_MLA_SRC_DIM, jnp.int32)
    pad = MLA_HP - len(_MLA_SRC_DIM)
    wq = jnp.pad(w_uq.reshape(MLA_Q_RANK, H, MLA_QK)[:, :, src], ((0, 0), (0, 0), (0, pad)))
    wkv = w_ukv.reshape(MLA_KV_RANK, H, MLA_NOPE + MLA_V)
    wk = jnp.pad(wkv[:, :, :MLA_NOPE], ((0, 0), (0, 0), (0, MLA_HP - MLA_NOPE)))
    wv = wkv[:, :, MLA_NOPE:]
    r = jnp.arange(128)[:, None]
    lane_dim = jnp.pad(src, (0, pad), constant_values=-1)[None, :]
    place = (lane_dim == r + MLA_NOPE).astype(BF16)
    we = jnp.tile(place, (1, H))
    return (wq.reshape(MLA_Q_RANK, H * MLA_HP).astype(BF16), wk.reshape(MLA_KV_RANK, H * MLA_HP).astype(BF16),
            we, wv.reshape(MLA_KV_RANK, H * MLA_V).astype(BF16))


def _rope_tables(L, qk_norm):
    rows = L // GRID_W
    row = jnp.broadcast_to(jnp.arange(rows, dtype=F32)[:, None], (rows, GRID_W)).reshape(L)
    col = jnp.broadcast_to(jnp.arange(GRID_W, dtype=F32)[None, :], (rows, GRID_W)).reshape(L)
    inv_freq = ROPE_BASE ** (-jnp.arange(0, ROPE_AXIS_DIM, 2, dtype=F32) / ROPE_AXIS_DIM)
    ar = row[:, None] * inv_freq
    ac = col[:, None] * inv_freq
    h8 = _ROPE_HALF
    one = jnp.ones((L, MLA_NOPE), F32)
    z8 = jnp.zeros((L, h8), F32)
    zpad = jnp.zeros((L, MLA_HP - len(_MLA_SRC_DIM)), F32)
    cos_t = jnp.concatenate([one, jnp.cos(ar), jnp.cos(ar), z8, jnp.cos(ac), jnp.cos(ac), z8, zpad], axis=1)
    sin_t = jnp.concatenate([0 * one, -jnp.sin(ar), jnp.sin(ar), z8, -jnp.sin(ac), jnp.sin(ac), z8, zpad], axis=1)
    ident = jnp.asarray(_MLA_REAL_LANE, F32)[None, :]
    cos_t = jnp.concatenate([cos_t, jnp.broadcast_to(ident, (TM, MLA_HP))], axis=0)
    sin_t = jnp.concatenate([sin_t, jnp.zeros((TM, MLA_HP), F32)], axis=0)
    src = jnp.asarray(_MLA_SRC_DIM, jnp.int32)
    pad = MLA_HP - len(_MLA_SRC_DIM)
    scale = MLA_QK ** -0.5 * math.log2(math.e)
    tables = []
    for g, s in ((qk_norm[0], scale), (qk_norm[1], 1.0)):
        g_lane = jnp.pad(g[src], (0, pad))
        g_partner = jnp.roll(g_lane, -h8)
        tables += [cos_t * (g_lane * s)[None, :], sin_t * (g_partner * s)[None, :]]
    return tables


def _log_sigmoid(x):
    return jnp.minimum(x, 0.0) - jnp.log1p(jnp.exp(-jnp.abs(x)))


def _mlstm_kernel(qf_ref, kf_ref, vf_ref, gf_ref, qb_ref, kb_ref, vb_ref, gb_ref, bias_ref, tril_ref, triu_ref,
                  of_ref, ob_ref, c_ref, n_ref, m_ref):
    T = ML_TC

    @pl.when(pl.program_id(1) == 0)
    def _():
        c_ref[...] = jnp.zeros_like(c_ref)
        n_ref[...] = jnp.zeros_like(n_ref)
        m_ref[...] = jnp.full(m_ref.shape, ML_M_INIT, F32)

    ti = lax.broadcasted_iota(jnp.int32, (T, T), 0)
    si = lax.broadcasted_iota(jnp.int32, (T, T), 1)
    dirs = ((qf_ref, kf_ref, vf_ref, gf_ref, of_ref, tril_ref), (qb_ref, kb_ref, vb_ref, gb_ref, ob_ref, triu_ref))
    for d, (q_ref, k_ref, v_ref, g_ref, o_ref, tri_ref) in enumerate(dirs):
        tri = (si <= ti) if d == 0 else (si >= ti)
        g = g_ref[...] + bias_ref[...]
        g_t = g.T
        lsg = _log_sigmoid(g)
        lsg_hi = lsg.astype(BF16)
        lsg_lo = (lsg - lsg_hi.astype(F32)).astype(BF16)
        cum = _dot(tri_ref[...], lsg_hi) + _dot(tri_ref[...], lsg_lo)
        cum_t = cum.T
        last = T - 1 if d == 0 else 0
        for h in range(ML_HEADS):
            st = d * ML_HEADS + h
            li, lf_ = (2 * d) * ML_HEADS + h, (2 * d + 1) * ML_HEADS + h
            ig_col = g[:, li:li + 1]
            ig_row = g_t[li:li + 1, :]
            b_col = cum[:, lf_:lf_ + 1]
            b_row = cum_t[lf_:lf_ + 1, :]
            total = cum[last:last + 1, lf_:lf_ + 1]
            m_old = m_ref[st, 0:1, 0:1]
            d_log = jnp.where(tri, b_col - b_row + ig_row, -jnp.inf)
            inter_log = b_col + m_old
            m_t = jnp.maximum(inter_log, jnp.max(d_log, axis=1, keepdims=True))
            qh = q_ref[:, h * ML_DQK:(h + 1) * ML_DQK] * (ML_DQK ** -0.5)
            kh = k_ref[:, h * ML_DQK:(h + 1) * ML_DQK]
            vh = v_ref[:, h * ML_DV:(h + 1) * ML_DV].astype(BF16)
            qb16 = qh.astype(BF16)
            s_mat = _dot_t(qb16, kh.astype(BF16)) * jnp.exp(d_log - m_t)
            inter = jnp.exp(inter_log - m_t)
            c_old = c_ref[st]
            n_old = n_ref[st, 0:1, :]
            num = _dot(s_mat.astype(BF16), vh) + inter * _dot(qb16, c_old.astype(BF16))
            den = jnp.sum(s_mat, axis=1, keepdims=True) + inter * jnp.sum(qh * n_old, axis=1, keepdims=True)
            o_ref[:, h * ML_DV:(h + 1) * ML_DV] = num / jnp.maximum(jnp.abs(den), jnp.exp(-m_t))
            w_log = total - b_col + ig_col
            m_new = jnp.maximum(total + m_old, jnp.max(w_log, axis=0, keepdims=True))
            w = jnp.exp(w_log - m_new)
            decay = jnp.exp(total + m_old - m_new)
            kw = kh * w
            c_ref[st] = decay * c_old + _dot(kw.T.astype(BF16), vh)
            n_ref[st, 0:1, :] = decay * n_old + jnp.sum(kw, axis=0, keepdims=True)
            m_ref[st] = jnp.broadcast_to(m_new, m_ref.shape[1:])


def _mlstm(st, proj, gate_bias):
    B = st.B
    assert st.L % ML_TC == 0 and st.C % ML_TC == 0
    cc = st.C // ML_TC
    nch = st.ltot // ML_TC
    qw = ML_HEADS * ML_DQK
    vw = ML_HEADS * ML_DV
    gcol = (2 * qw + 2 * vw) // 128

    def rb(d, b, k):
        chunk = k if d == 0 else jnp.where(k < cc, cc - 1 - k, nch - 1 - (k - cc))
        return b * nch + chunk

    def specs(d):
        return [pl.BlockSpec((ML_TC, qw), lambda b, k: (rb(d, b, k), 0)),
                pl.BlockSpec((ML_TC, qw), lambda b, k: (rb(d, b, k), 1)),
                pl.BlockSpec((ML_TC, vw), lambda b, k: (rb(d, b, k), (2 * qw) // vw)),
                pl.BlockSpec((ML_TC, 128), lambda b, k: (rb(d, b, k), gcol))]

    nst = 2 * ML_HEADS
    out = jax.ShapeDtypeStruct((st.NT, vw), F32)
    tril = jnp.tril(jnp.ones((ML_TC, ML_TC), BF16))
    return pl.pallas_call(
        _mlstm_kernel,
        grid=(B, nch),
        in_specs=specs(0) + specs(1) + [pl.BlockSpec((1, 128), lambda b, k: (0, 0)),
                                        pl.BlockSpec((ML_TC, ML_TC), lambda b, k: (0, 0)),
                                        pl.BlockSpec((ML_TC, ML_TC), lambda b, k: (0, 0))],
        out_specs=[pl.BlockSpec((ML_TC, vw), lambda b, k: (rb(0, b, k), 0)),
                   pl.BlockSpec((ML_TC, vw), lambda b, k: (rb(1, b, k), 0))],
        out_shape=[out, out],
        scratch_shapes=[pltpu.VMEM((nst, ML_DQK, ML_DV), F32),
                        pltpu.VMEM((nst, 8, ML_DQK), F32),
                        pltpu.VMEM((nst, 8, 128), F32)],
        compiler_params=_cparams(("arbitrary", "arbitrary")),
        name="mlstm_chunks",
    )(proj, proj, proj, proj, proj, proj, proj, proj, gate_bias, tril, tril.T)


def _route(lg):
    lane_i = lax.broadcasted_iota(jnp.int32, lg.shape, 1)
    lane = lane_i.astype(F32)
    neg = -jnp.inf
    gl = jnp.where(lane_i < MOE_GROUPS, lg, neg)
    gmax = jnp.max(gl, axis=-1, keepdims=True)
    gsum = jnp.sum(jnp.where(lane_i < MOE_GROUPS, jnp.exp(lg - gmax), 0.0), axis=-1, keepdims=True)
    p_top = 1.0 / gsum
    g_sel = jnp.min(jnp.where(gl == gmax, lane, 128.0), axis=-1, keepdims=True)
    group_of_lane = (lane_i >> 3).astype(F32) - 1.0
    el = jnp.where(group_of_lane == g_sel, lg, neg)
    e1 = jnp.max(el, axis=-1, keepdims=True)
    i1 = jnp.min(jnp.where(el == e1, lane, 128.0), axis=-1, keepdims=True)
    el2 = jnp.where(lane == i1, neg, el)
    e2 = jnp.max(el2, axis=-1, keepdims=True)
    i2 = jnp.min(jnp.where(el2 == e2, lane, 128.0), axis=-1, keepdims=True)
    t = jnp.exp(e2 - e1)
    w1 = p_top / (1.0 + t)
    w2 = w1 * t
    id1 = i1 - MOE_GROUPS
    id2 = i2 - MOE_GROUPS
    table = jnp.where(lane_i == 0, id1, jnp.where(lane_i == 1, id2,
                                                  jnp.where(lane_i == 2, w1, jnp.where(lane_i == 3, w2, 0.0))))
    chosen = jnp.where(lane == i1, 1.0, 0.0) + jnp.where(lane == i2, 1.0, 0.0)
    return table, jnp.sum(chosen, axis=0, keepdims=True)


def _accumulate_counts(cnt_ref, hist):
    @pl.when(pl.program_id(0) == 0)
    def _():
        cnt_ref[...] = jnp.zeros_like(cnt_ref)
    cnt_ref[...] += hist


def _lhs_mla(o_ref):
    return o_ref[...]


def _lhs_mlstm(hf_ref, hb_ref, og_ref, onorm_ref):
    hs = hf_ref[...] + hb_ref[...]
    og = _sigmoid(og_ref[...])
    parts = []
    for h in range(ML_HEADS):
        sl = slice(h * ML_DV, (h + 1) * ML_DV)
        parts.append(_rms(hs[:, sl]) * onorm_ref[:, sl] * og[:, sl])
    return jnp.concatenate(parts, axis=1)


def _mixer_out_kernel(*refs, n_lhs, lhs_fn):
    lhs_refs = refs[:n_lhs]
    w_ref, x_ref, mod_ref, gain_ref, wr_ref, br_ref, xo_ref, f_ref, r_ref, cnt_ref = refs[n_lhs:]
    y = _dot(lhs_fn(*lhs_refs).astype(BF16), w_ref[...])
    mod = mod_ref[0]
    xn = x_ref[...] + mod[2:3, :] * y
    xo_ref[...] = xn
    f = _norm_mod(xn, gain_ref[...], mod[3:4, :], mod[4:5, :])
    f_ref[...] = f
    r_ref[...], hist = _route(_dot3_split(f, wr_ref[...]) + br_ref[...])
    _accumulate_counts(cnt_ref, hist)


def _mixer_out(rows, lhs_fn, lhs_args, lhs_specs, w_out, x, mod, gain, wr, br, name):
    n = rows.n * TM
    const = lambda k: (0, 0)
    out = lambda w: pl.BlockSpec((TM, w), lambda k: (k, 0))
    return pl.pallas_call(
        functools.partial(_mixer_out_kernel, n_lhs=len(lhs_args), lhs_fn=lhs_fn),
        grid=(rows.n,),
        in_specs=list(lhs_specs) + [
            pl.BlockSpec((D_MODEL, D_MODEL), const),
            pl.BlockSpec((TM, D_MODEL), lambda k: (rows.src(k), 0)),
            pl.BlockSpec((1, 8, D_MODEL), lambda k: (rows.mod(k), 0, 0)),
            pl.BlockSpec((1, D_MODEL), const),
            pl.BlockSpec((D_MODEL, 256), const),
            pl.BlockSpec((1, 128), const)],
        out_specs=[out(D_MODEL), out(D_MODEL), out(128), pl.BlockSpec((1, 128), const)],
        out_shape=[jax.ShapeDtypeStruct((n, D_MODEL), F32),
                   jax.ShapeDtypeStruct((n, D_MODEL), F32),
                   jax.ShapeDtypeStruct((n, 128), F32),
                   jax.ShapeDtypeStruct((1, 128), F32)],
        compiler_params=_cparams(("arbitrary",)),
        name=name,
    )(*lhs_args, w_out, x, mod, gain, wr, br)


RANK_TILES = 4


def _rank_kernel(r_ref, start_ref, tril_ref, pos_ref, carry_ref):
    @pl.when(pl.program_id(0) == 0)
    def _():
        carry_ref[...] = jnp.zeros_like(carry_ref)

    r = r_ref[...]
    lane = lax.broadcasted_iota(jnp.int32, r.shape, 1).astype(F32)
    chosen = [jnp.where(lane == r[:, k:k + 1] + MOE_GROUPS, 1.0, 0.0) for k in range(MOE_TOPK)]
    both = functools.reduce(jnp.add, chosen)
    before = _dot(tril_ref[...], both.astype(BF16)) + (start_ref[...] + carry_ref[...])
    lane_i = lax.broadcasted_iota(jnp.int32, r.shape, 1)
    pos = jnp.zeros(r.shape, F32)
    for k in range(MOE_TOPK):
        pos = jnp.where(lane_i == k, jnp.sum(chosen[k] * before, axis=-1, keepdims=True), pos)
    pos_t = pos.T[0:8, :].astype(jnp.int32)
    for j in range(RANK_TILES):
        pos_ref[j] = pos_t[:, j * TM:(j + 1) * TM]
    carry_ref[...] += jnp.sum(both, axis=0, keepdims=True)


def _assignment_slots(ntiles, route, starts):
    assert ntiles % RANK_TILES == 0
    rows = RANK_TILES * TM
    tril = jnp.tril(jnp.ones((rows, rows), BF16), -1)
    return pl.pallas_call(
        _rank_kernel,
        grid=(ntiles // RANK_TILES,),
        in_specs=[pl.BlockSpec((rows, 128), lambda i: (i, 0)),
                  pl.BlockSpec((1, 128), lambda i: (0, 0)),
                  pl.BlockSpec((rows, rows), lambda i: (0, 0))],
        out_specs=pl.BlockSpec((RANK_TILES, 8, TM), lambda i: (i, 0, 0)),
        out_shape=jax.ShapeDtypeStruct((ntiles, 8, TM), jnp.int32),
        scratch_shapes=[pltpu.VMEM((1, 128), F32)],
        compiler_params=_cparams(("arbitrary",)),
        name="moe_rank",
    )(route, starts, tril)


def _row_wait(hbm, buf, sem):
    pltpu.make_async_copy(hbm.at[pl.ds(0, TM), :], buf, sem).wait()


def _dispatch_kernel(pos_ref, f_ref, xs_hbm, buf, sem, *, ntiles):
    i = pl.program_id(0)

    def step(s):
        @pl.when(i >= 2)
        def _():
            for _ in range(MOE_TOPK):
                _row_wait(xs_hbm, buf.at[s], sem.at[s])
        buf[s] = f_ref[...]
        for r in range(TM):
            for k in range(MOE_TOPK):
                pltpu.make_async_copy(buf.at[s, pl.ds(r, 1), :], xs_hbm.at[pl.ds(pos_ref[0, k, r], 1), :],
                                      sem.at[s]).start(priority=k % 2)

        @pl.when(i == ntiles - 1)
        def _():
            for slot in ((1 - s, s) if ntiles >= 2 else (s,)):
                for _ in range(MOE_TOPK):
                    _row_wait(xs_hbm, buf.at[slot], sem.at[slot])

    for s in range(2):
        pl.when(i % 2 == s)(functools.partial(step, s))


def _dispatch(ntiles, f, pos):
    n = ntiles * TM
    return pl.pallas_call(
        functools.partial(_dispatch_kernel, ntiles=ntiles),
        grid=(ntiles,),
        in_specs=[pl.BlockSpec((1, 8, TM), lambda i: (i, 0, 0), memory_space=pltpu.SMEM),
                  pl.BlockSpec((TM, D_MODEL), lambda i: (i, 0))],
        out_specs=pl.BlockSpec(memory_space=pl.ANY),
        out_shape=jax.ShapeDtypeStruct((MOE_TOPK * n, D_MODEL), F32),
        scratch_shapes=[pltpu.VMEM((2, TM, D_MODEL), F32), pltpu.SemaphoreType.DMA((2,))],
        compiler_params=_cparams(("arbitrary",)),
        name="moe_dispatch",
    )(pos, f)


def _expert_kernel(vb_ref, ve_ref, lo_ref, hi_ref, first_ref, newexp_ref, eslot_ref, enext_ref, x_ref, wgu_hbm,
                   wd_hbm, y_ref, wgu_f, wd_f, wgu_b, wd_b, sem, *, layer):
    v = pl.program_id(0)

    def fetch(e, s):
        return (pltpu.make_async_copy(wgu_hbm.at[layer, e], wgu_f.at[s], sem.at[0, s]),
                pltpu.make_async_copy(wd_hbm.at[layer, e], wd_f.at[s], sem.at[1, s]))

    @pl.when(newexp_ref[v] == 1)
    def _():
        for s in range(2):
            @pl.when(eslot_ref[v] == s)
            def _():
                @pl.when(v == 0)
                def _():
                    for c in fetch(ve_ref[v], s):
                        c.start()

                @pl.when(enext_ref[v] >= 0)
                def _():
                    for c in fetch(enext_ref[v], 1 - s):
                        c.start()
                for c in fetch(ve_ref[v], s):
                    c.wait()
                wgu_b[...] = wgu_f[s].astype(BF16)
                wd_b[...] = wd_f[s].astype(BF16)

    @pl.when(hi_ref[v] > lo_ref[v])
    def _():
        gu = _dot(x_ref[...].astype(BF16), wgu_b[...])
        gate = gu[:, :MOE_FF]
        act = gate * _sigmoid(gate) * gu[:, MOE_FF:]
        y = _dot(act.astype(BF16), wd_b[...])
        r = lax.broadcasted_iota(jnp.int32, (MOE_BM, 1), 0)
        mine = jnp.logical_and(r >= lo_ref[v], r < hi_ref[v])
        y = jnp.where(mine, y, 0.0)

        @pl.when(first_ref[v] == 1)
        def _():
            y_ref[...] = y

        @pl.when(first_ref[v] == 0)
        def _():
            y_ref[...] += y


def _expert_ffn(xs, visits, layer, w_gate_up, w_down):
    nvis = visits[0].shape[0]
    blk_idx = lambda v, vb, *_: (vb[v], 0)
    grid_spec = pltpu.PrefetchScalarGridSpec(
        num_scalar_prefetch=8,
        grid=(nvis,),
        in_specs=[pl.BlockSpec((MOE_BM, D_MODEL), blk_idx),
                  pl.BlockSpec(memory_space=pl.ANY),
                  pl.BlockSpec(memory_space=pl.ANY)],
        out_specs=pl.BlockSpec((MOE_BM, D_MODEL), blk_idx),
        scratch_shapes=[pltpu.VMEM((2, D_MODEL, 2 * MOE_FF), F32), pltpu.VMEM((2, MOE_FF, D_MODEL), F32),
                        pltpu.VMEM((D_MODEL, 2 * MOE_FF), BF16), pltpu.VMEM((MOE_FF, D_MODEL), BF16),
                        pltpu.SemaphoreType.DMA((2, 2))],
    )
    return pl.pallas_call(
        functools.partial(_expert_kernel, layer=layer),
        grid_spec=grid_spec,
        out_shape=jax.ShapeDtypeStruct(xs.shape, F32),
        compiler_params=_cparams(("arbitrary",)),
        name="moe_expert_ffn",
    )(*visits, xs, w_gate_up, w_down)


def _combine_kernel(pos_ref, nxt_ref, x_ref, r_ref, mod_ref, ys_hbm, o_ref, ybuf, sem, *, ntiles):
    i = pl.program_id(0)

    def gather(table, s):
        for r in range(TM):
            for k in range(MOE_TOPK):
                pltpu.make_async_copy(ys_hbm.at[pl.ds(table[0, k, r], 1), :], ybuf.at[s, k, pl.ds(r, 1), :],
                                      sem.at[s]).start(priority=k % 2)

    @pl.when(i == 0)
    def _():
        gather(pos_ref, 0)

    def step(s):
        @pl.when(i + 1 < ntiles)
        def _():
            gather(nxt_ref, 1 - s)
        for k in range(MOE_TOPK):
            _row_wait(ys_hbm, ybuf.at[s, k], sem.at[s])
        w = r_ref[...]
        y = w[:, MOE_TOPK:MOE_TOPK + 1] * ybuf[s, 0]
        for k in range(1, MOE_TOPK):
            y = y + w[:, MOE_TOPK + k:MOE_TOPK + k + 1] * ybuf[s, k]
        o_ref[...] = x_ref[...] + mod_ref[0][5:6, :] * y

    for s in range(2):
        pl.when(i % 2 == s)(functools.partial(step, s))


def _combine(rows, x, ys, pos, route, mod):
    n = rows.n
    spec = pl.BlockSpec((TM, D_MODEL), lambda i: (i, 0))
    return pl.pallas_call(
        functools.partial(_combine_kernel, ntiles=n),
        grid=(n,),
        in_specs=[pl.BlockSpec((1, 8, TM), lambda i: (i, 0, 0), memory_space=pltpu.SMEM),
                  pl.BlockSpec((1, 8, TM), lambda i: (jnp.minimum(i + 1, n - 1), 0, 0), memory_space=pltpu.SMEM),
                  spec,
                  pl.BlockSpec((TM, 128), lambda i: (i, 0)),
                  pl.BlockSpec((1, 8, D_MODEL), lambda i: (rows.mod(i), 0, 0)),
                  pl.BlockSpec(memory_space=pl.ANY)],
        out_specs=spec,
        out_shape=jax.ShapeDtypeStruct((n * TM, D_MODEL), F32),
        scratch_shapes=[pltpu.VMEM((2, MOE_TOPK, TM, D_MODEL), F32), pltpu.SemaphoreType.DMA((2,))],
        compiler_params=_cparams(("arbitrary",)),
        name="moe_combine",
    )(pos, pos, x, route, mod, ys)


def _visit_tables(bounds, nk):
    E = MOE_EXPERTS
    nblk = nk // MOE_BM
    nvis = nblk + E
    starts, ends = bounds[:-1], bounds[1:]
    fb = starts // MOE_BM
    nv = jnp.where(ends > starts, (ends - 1) // MOE_BM - fb + 1, 0)
    cum = jnp.cumsum(nv)
    total = cum[-1]
    v = jnp.arange(nvis, dtype=jnp.int32)
    active = v < total
    vc = jnp.minimum(v, total - 1)
    ve = jnp.minimum(jnp.sum((cum[None, :] <= vc[:, None]).astype(jnp.int32), axis=1), E - 1)
    vb = fb[ve] + (vc - (cum - nv)[ve])
    lo = jnp.where(active, jnp.maximum(starts[ve], vb * MOE_BM) - vb * MOE_BM, 0)
    hi = jnp.where(active, jnp.minimum(ends[ve], (vb + 1) * MOE_BM) - vb * MOE_BM, 0)
    prev_b = jnp.concatenate([jnp.full((1,), -1, jnp.int32), vb[:-1]])
    first = jnp.logical_and(active, vb != prev_b)
    prev_e = jnp.concatenate([jnp.full((1,), -1, jnp.int32), ve[:-1]])
    new_expert = jnp.logical_and(active, ve != prev_e)
    eslot = (jnp.cumsum(new_expert.astype(jnp.int32)) - 1) % 2
    at = jnp.where(new_expert, v, nvis)
    nxt = jnp.concatenate([lax.cummin(at[::-1])[::-1][1:], jnp.full((1,), nvis, jnp.int32)])
    enext = jnp.where(nxt < nvis, ve[jnp.minimum(nxt, nvis - 1)], -1)
    i32 = lambda a: a.astype(jnp.int32)
    return i32(vb), i32(ve), i32(lo), i32(hi), i32(first), i32(new_expert), i32(eslot), i32(enext)


def _moe(rows, x, f, route, counts, mod, layer, w_gate_up, w_down):
    n = rows.n * TM
    cum = jnp.cumsum(counts[0])
    starts = (cum - counts[0]).reshape(1, 128)
    bounds = jnp.concatenate([starts[0, MOE_GROUPS:MOE_GROUPS + MOE_EXPERTS], cum[-1:]]).astype(jnp.int32)
    pos = _assignment_slots(rows.n, route, starts)
    xs = _dispatch(rows.n, f, pos)
    ys = _expert_ffn(xs, _visit_tables(bounds, n * MOE_TOPK), layer, w_gate_up, w_down)
    return _combine(rows, x, ys, pos, route, mod)


def kernel(x, c, ctx, c_ctx, ada_w, ada_b, norm_mix, norm_ffn, rg_w_in, rg_conv_w, rg_conv_b, rg_gate_w, rg_gate_b, rg_lambda, rg_w_out, mla_w_down, mla_q_norm, mla_kv_norm, mla_w_uq, mla_w_ukv, mla_qk_norm, mla_w_o, ml_w_in, ml_gate_b, ml_out_norm, ml_w_out, moe_w_group, moe_b_group, moe_w_expert, moe_b_expert, moe_w_gate_up, moe_w_down):
    B, L, D = x.shape
    C = ctx.shape[1]
    depth = ada_w.shape[0]
    assert D == D_MODEL
    st = _Stream(B, L, C)

    xs = jnp.concatenate([ctx, x], axis=1).reshape(st.NT, D)
    cc = jnp.zeros((16, D), F32).at[:B].set(c).at[B].set(c_ctx)
    mod_all = _modulation(cc, ada_w, ada_b)
    mod_all = jnp.pad(mod_all[:, :B + 1].reshape(depth, B + 1, 6, D), ((0, 0), (0, 0), (0, 2), (0, 0)))
    perm = _row_permutation(B)

    row = lambda a: a.reshape(1, -1)

    for i in range(depth):
        last = i == depth - 1
        mod = mod_all[i]
        kind, j = i % 3, i // 3
        all_rows = st.all_rows()
        out_rows = st.latent_rows() if last else all_rows
        tile_spec = lambda w: pl.BlockSpec((TM, w), lambda k: (out_rows.src(k), 0))
        wr = jnp.zeros((D, 128), F32).at[:, :MOE_GROUPS].set(moe_w_group[i]) \
            .at[:, MOE_GROUPS:MOE_GROUPS + MOE_EXPERTS].set(moe_w_expert[i])
        wr_hi = wr.astype(BF16)
        wr = jnp.concatenate([wr_hi, (wr - wr_hi.astype(F32)).astype(BF16)], axis=1)
        br = jnp.zeros((1, 128), F32).at[0, :MOE_GROUPS].set(moe_b_group[i]) \
            .at[0, MOE_GROUPS:MOE_GROUPS + MOE_EXPERTS].set(moe_b_expert[i])
        out_args = (xs, mod, row(norm_ffn[i]), wr, br)

        if kind == 0:
            gate, u = _rg_in(st, xs, mod, row(norm_mix[i]), perm, rg_w_in[j].astype(BF16))
            wg, gb = _rg_gate_weights(rg_gate_w[j], rg_gate_b[j])
            hs = _rg_scan(st, u, rg_conv_w[j], row(rg_conv_b[j]), wg, gb, rg_lambda[j].reshape(2, 1, D))
            xs, f, route, counts = _rg_out(st, last, gate, hs, perm.T, rg_w_out[j].astype(BF16), *out_args)
        elif kind == 1:
            w_down = jnp.pad(mla_w_down[j], ((0, 0), (0, 512 - mla_w_down.shape[2]))).astype(BF16)
            down = _norm_proj(all_rows, xs, mod, row(norm_mix[i]), w_down, name="mla_down_proj")
            wq, wk, we, wv = _mla_weights(mla_w_uq[j], mla_w_ukv[j])
            q, k, v = _mla_up(st, all_rows, down, row(mla_q_norm[j]), row(mla_kv_norm[j]), wq, wk, we, wv,
                              _rope_tables(L, mla_qk_norm[j]))
            o = _attention(st, q, k, v)
            xs, f, route, counts = _mixer_out(out_rows, _lhs_mla, (o,), (tile_spec(D),),
                                      mla_w_o[j].astype(BF16), *out_args, name="mla_out")
        else:
            n_in = ml_w_in.shape[2]
            w_in = jnp.pad(ml_w_in[j], ((0, 0), (0, ML_NP - n_in))).astype(BF16)
            proj = _norm_proj(all_rows, xs, mod, row(norm_mix[i]), w_in, name="mlstm_in_proj")
            gate_bias = jnp.pad(ml_gate_b[j].reshape(1, -1), ((0, 0), (0, 128 - 4 * ML_HEADS)))
            hf, hb = _mlstm(st, proj, gate_bias)
            og_spec = pl.BlockSpec((TM, D), lambda k: (out_rows.src(k), 2))
            xs, f, route, counts = _mixer_out(out_rows, _lhs_mlstm, (hf, hb, proj, row(ml_out_norm[j])),
                                      (tile_spec(D), tile_spec(D), og_spec, pl.BlockSpec((1, D), lambda k: (0, 0))),
                                      ml_w_out[j].astype(BF16), *out_args, name="mlstm_out")

        moe_rows = st.dense_latent_rows() if last else all_rows
        xs = _moe(moe_rows, xs, f, route, counts, mod, i, moe_w_gate_up, moe_w_down)

    return xs.reshape(B, L, D)
```

```python
import functools
import math

import jax
import jax.numpy as jnp
from jax import lax
from jax.experimental import pallas as pl
from jax.experimental.pallas import tpu as pltpu

F32 = jnp.float32
BF16 = jnp.bfloat16

D_MODEL = 1024
RMS_EPS = 1e-6

TM = 256
VMEM_LIMIT = 48 * 1024 * 1024

RG_BLOCK_W = 64
RG_CHUNK = 256
RG_CONV_W = 4
RG_C = 8.0
RG_TT = 64

MLA_HEADS = 16
MLA_Q_RANK = 256
MLA_KV_RANK = 128
MLA_NOPE = 64
MLA_ROPE = 32
MLA_V = 64
MLA_QK = MLA_NOPE + MLA_ROPE
MLA_HP = 128
ROPE_AXIS_DIM = MLA_ROPE // 2
ROPE_BASE = 10000.0
GRID_W = 64
ATT_HEADS = 8

ML_HEADS = 4
ML_DV = 256
ML_DQK = 128
ML_TC = 256
ML_M_INIT = -1e30
ML_NP = 3200

MOE_GROUPS = 8
MOE_PER_GROUP = 8
MOE_EXPERTS = 64
MOE_TOPK = 2
MOE_FF = 256
MOE_BM = 512


def _cparams(sem):
    return pltpu.CompilerParams(dimension_semantics=sem, vmem_limit_bytes=VMEM_LIMIT)


def _dot(a, b):
    return jnp.dot(a, b, preferred_element_type=F32)


def _dot_t(a, b):
    return lax.dot_general(a, b, (((1,), (1,)), ((), ())), preferred_element_type=F32)


def _dot3(a, b):
    ah = a.astype(BF16)
    al = (a - ah.astype(F32)).astype(BF16)
    bh = b.astype(BF16)
    bl = (b - bh.astype(F32)).astype(BF16)
    return _dot(ah, bh) + (_dot(al, bh) + _dot(ah, bl))


def _dot3_split(a, w2):
    n = w2.shape[1] // 2
    ah = a.astype(BF16)
    al = (a - ah.astype(F32)).astype(BF16)
    both = _dot(ah, w2)
    return both[:, :n] + (_dot(al, w2[:, :n]) + both[:, n:])


def _sigmoid(x):
    return 0.5 * jnp.tanh(0.5 * x) + 0.5


def _softplus(x):
    return jnp.maximum(x, 0.0) + jnp.log1p(jnp.exp(-jnp.abs(x)))


def _gelu_tanh(x):
    return 0.5 * x * (1.0 + jnp.tanh(0.7978845608028654 * (x + 0.044715 * (x * x * x))))


def _rms(x, n=None):
    n = x.shape[-1] if n is None else n
    ms = jnp.sum(x * x, axis=-1, keepdims=True) * (1.0 / n)
    return x * lax.rsqrt(ms + RMS_EPS)


class _Rows:
    def __init__(self, n, src, mod):
        self.n, self.src, self.mod = n, src, mod


class _Stream:
    def __init__(self, B, L, C):
        assert L % TM == 0 and C % TM == 0 and B % 8 == 0
        self.B, self.L, self.C = B, L, C
        self.ltot = L + C
        self.lt, self.ct = L // TM, C // TM
        self.tpb = self.lt + self.ct
        self.NT = B * self.ltot

    def all_rows(self):
        tpb, ct, B = self.tpb, self.ct, self.B
        return _Rows(B * tpb, lambda k: k, lambda k: jnp.where(k % tpb < ct, B, k // tpb))

    def latent_rows(self):
        tpb, ct, lt = self.tpb, self.ct, self.lt
        return _Rows(self.B * lt, lambda k: (k // lt) * tpb + ct + k % lt, lambda k: k // lt)

    def dense_latent_rows(self):
        lt = self.lt
        return _Rows(self.B * lt, lambda k: k, lambda k: k // lt)


def _mod_kernel(c_ref, w_ref, b_ref, o_ref):
    c = c_ref[...]
    o_ref[0] = _dot3(c * _sigmoid(c), w_ref[0]) + b_ref[0]


def _modulation(cc, ada_w, ada_b):
    depth, d, n = ada_w.shape
    tn = 1536
    return pl.pallas_call(
        _mod_kernel,
        grid=(depth, n // tn),
        in_specs=[pl.BlockSpec((16, d), lambda l, j: (0, 0)),
                  pl.BlockSpec((1, d, tn), lambda l, j: (l, 0, j)),
                  pl.BlockSpec((1, 1, tn), lambda l, j: (l, 0, j))],
        out_specs=pl.BlockSpec((1, 16, tn), lambda l, j: (l, 0, j)),
        out_shape=jax.ShapeDtypeStruct((depth, 16, n), F32),
        compiler_params=_cparams(("arbitrary", "arbitrary")),
        name="ada_modulation",
    )(cc, ada_w, ada_b.reshape(depth, 1, n))


def _norm_mod(x, gain, shift, scale):
    return _rms(x) * gain * (1.0 + scale) + shift


def _norm_proj_kernel(x_ref, mod_ref, g_ref, w_ref, o_ref):
    mod = mod_ref[0]
    h = _norm_mod(x_ref[...], g_ref[...], mod[0:1, :], mod[1:2, :]).astype(BF16)
    o_ref[...] = _dot(h, w_ref[...]).astype(o_ref.dtype)


def _norm_proj(rows, x, mod, gain, w, name):
    n = w.shape[1]
    const = lambda k: (0, 0)
    return pl.pallas_call(
        _norm_proj_kernel,
        grid=(rows.n,),
        in_specs=[pl.BlockSpec((TM, D_MODEL), lambda k: (rows.src(k), 0)),
                  pl.BlockSpec((1, 8, D_MODEL), lambda k: (rows.mod(k), 0, 0)),
                  pl.BlockSpec((1, D_MODEL), const),
                  pl.BlockSpec((D_MODEL, n), const)],
        out_specs=pl.BlockSpec((TM, n), lambda k: (k, 0)),
        out_shape=jax.ShapeDtypeStruct((rows.n * TM, n), F32),
        compiler_params=_cparams(("arbitrary",)),
        name=name,
    )(x, mod, gain, w)


RG_TR = 32


def _row_permutation(B):
    n = B * RG_TR
    r = jnp.arange(n)
    src = (r % B) * RG_TR + r // B
    return (src[:, None] == jnp.arange(n)[None, :]).astype(BF16)


def _time_tile_mod(mod_ref, is_ctx, B, row):
    return jnp.where(is_ctx, mod_ref[B:B + 1, row:row + 1, :], mod_ref[0:B, row:row + 1, :])


def _rg_in_kernel(x_ref, mod_ref, g_ref, perm_ref, wg_ref, wu_ref, gate_ref, u_ref, *, B, ctx_tiles):
    is_ctx = pl.program_id(0) < ctx_tiles
    h = _norm_mod(x_ref[...], g_ref[...], _time_tile_mod(mod_ref, is_ctx, B, 0), _time_tile_mod(mod_ref, is_ctx, B, 1))
    h = h.reshape(B * RG_TR, D_MODEL).astype(BF16)
    h = _dot(perm_ref[...], h).astype(BF16)
    gate_ref[...] = _gelu_tanh(_dot(h, wg_ref[...])).astype(gate_ref.dtype)
    u_ref[...] = _dot(h, wu_ref[...])


def _rg_in(st, x, mod, gain, perm, w_in):
    B, W = st.B, D_MODEL
    R = B * RG_TR
    nt = st.ltot // RG_TR
    const = lambda t: (0, 0)
    return pl.pallas_call(
        functools.partial(_rg_in_kernel, B=B, ctx_tiles=st.C // RG_TR),
        grid=(nt,),
        in_specs=[pl.BlockSpec((B, RG_TR, D_MODEL), lambda t: (0, t, 0)),
                  pl.BlockSpec((B + 1, 8, D_MODEL), lambda t: (0, 0, 0)),
                  pl.BlockSpec((1, D_MODEL), const),
                  pl.BlockSpec((R, R), const),
                  pl.BlockSpec((D_MODEL, W), const),
                  pl.BlockSpec((D_MODEL, W), lambda t: (0, 1))],
        out_specs=[pl.BlockSpec((R, W), lambda t: (t, 0)),
                   pl.BlockSpec((R, W), lambda t: (t, 0))],
        out_shape=[jax.ShapeDtypeStruct((st.ltot * B, W), BF16), jax.ShapeDtypeStruct((st.ltot * B, W), F32)],
        compiler_params=_cparams(("arbitrary",)),
        name="rg_in_proj",
    )(x.reshape(B, st.ltot, D_MODEL), mod, gain, perm, w_in, w_in)


def _rg_tile_order(d, k, ct, ntt):
    bwd = jnp.where(k < ct, ct - 1 - k, ntt - 1 - (k - ct))
    return jnp.where(d == 0, k, bwd)


def _rg_scan_kernel(um_ref, up_ref, un_ref, cw_ref, cb_ref, wg_ref, gb_ref, lam_ref, o_ref,
                    ext_ref, a_ref, b_ref, h_ref, *, B, ct, ntt):
    d = pl.program_id(0)
    k = pl.program_id(1)
    tile = _rg_tile_order(d, k, ct, ntt)
    R = RG_TT * B

    @pl.when(k == 0)
    def _():
        h_ref[...] = jnp.zeros_like(h_ref)

    seq_start = jnp.logical_or(tile == 0, tile == ct)
    seq_end = jnp.logical_or(tile == ct - 1, tile == ntt - 1)
    ext_ref[0:2 * B, :] = jnp.where(seq_start, 0.0, up_ref[...])
    ext_ref[2 * B:2 * B + R, :] = um_ref[...]
    ext_ref[2 * B + R:3 * B + R, :] = jnp.where(seq_end, 0.0, un_ref[...])
    cw = cw_ref[...]
    uc = cb_ref[...] + cw[0:1, :] * ext_ref[0:R, :]
    for j in range(1, RG_CONV_W):
        uc = uc + cw[j:j + 1, :] * ext_ref[j * B:j * B + R, :]
    ucb = uc.astype(BF16)
    c_half = (-0.5 * RG_C) * _softplus(-lam_ref[0])
    for c in range(D_MODEL // RG_CHUNK):
        sl = slice(c * RG_CHUNK, (c + 1) * RG_CHUNK)
        z = _dot(ucb[:, sl], wg_ref[0, c]) + gb_ref[0, c]
        log_a = c_half[:, sl] * jnp.tanh(z[:, :RG_CHUNK]) + c_half[:, sl]
        u_half = 0.5 * uc[:, sl]
        gated_u = jnp.tanh(z[:, RG_CHUNK:]) * u_half + u_half
        a = jnp.exp(log_a)
        one_minus_a2 = -jnp.tanh(log_a) * (a * a + 1.0)
        a_ref[:, sl] = a
        b_ref[:, sl] = jnp.sqrt(one_minus_a2) * gated_u

    def scan(times):
        for c in range(D_MODEL // 128):
            cs = slice(c * 128, (c + 1) * 128)
            h = h_ref[:, cs]
            for t in times:
                rs = slice(t * B, (t + 1) * B)
                h = a_ref[rs, cs] * h + b_ref[rs, cs]
                b_ref[rs, cs] = h
            h_ref[:, cs] = h

    pl.when(d == 0)(lambda: scan(range(RG_TT)))
    pl.when(d == 1)(lambda: scan(range(RG_TT - 1, -1, -1)))
    o_ref[0] = b_ref[...].astype(o_ref.dtype)


def _rg_scan(st, u_tm, conv_w, conv_b, wg, gb, lam):
    B, W = st.B, D_MODEL
    assert st.C % RG_TT == 0 and st.L % RG_TT == 0
    ltot = st.ltot
    ntt, ct = ltot // RG_TT, st.C // RG_TT
    R = RG_TT * B
    order = functools.partial(_rg_tile_order, ct=ct, ntt=ntt)
    nch = W // RG_CHUNK
    return pl.pallas_call(
        functools.partial(_rg_scan_kernel, B=B, ct=ct, ntt=ntt),
        grid=(2, ntt),
        in_specs=[pl.BlockSpec((R, W), lambda d, k: (order(d, k), 0)),
                  pl.BlockSpec((2 * B, W), lambda d, k: (jnp.maximum(order(d, k) * (RG_TT // 2) - 1, 0), 0)),
                  pl.BlockSpec((B, W), lambda d, k: (jnp.minimum((order(d, k) + 1) * RG_TT, ltot - 1), 0)),
                  pl.BlockSpec((RG_CONV_W, W), lambda d, k: (0, 0)),
                  pl.BlockSpec((1, W), lambda d, k: (0, 0)),
                  pl.BlockSpec((1, nch, RG_CHUNK, 2 * RG_CHUNK), lambda d, k: (d, 0, 0, 0)),
                  pl.BlockSpec((1, nch, 1, 2 * RG_CHUNK), lambda d, k: (d, 0, 0, 0)),
                  pl.BlockSpec((1, 1, W), lambda d, k: (d, 0, 0))],
        out_specs=pl.BlockSpec((1, R, W), lambda d, k: (d, order(d, k), 0)),
        out_shape=jax.ShapeDtypeStruct((2, ltot * B, W), BF16),
        scratch_shapes=[pltpu.VMEM((3 * B + R, W), F32),
                        pltpu.VMEM((R, W), F32),
                        pltpu.VMEM((R, W), F32),
                        pltpu.VMEM((B, W), F32)],
        compiler_params=_cparams(("arbitrary", "arbitrary")),
        name="rg_scan",
    )(u_tm, u_tm, u_tm, conv_w, conv_b, wg, gb, lam)


def _rg_gate_weights(gate_w, gate_b):
    nb = gate_w.shape[2]
    per = RG_CHUNK // RG_BLOCK_W
    nch = nb // per
    gw = gate_w.reshape(2, 2, nch, per, RG_BLOCK_W, RG_BLOCK_W)
    eye = jnp.eye(per, dtype=gate_w.dtype)
    bd = jnp.einsum('dgcnij,nm->dgcnimj', gw, eye).reshape(2, 2, nch, RG_CHUNK, RG_CHUNK)
    wg = (0.5 * jnp.concatenate([bd[:, 0], bd[:, 1]], axis=-1)).astype(BF16)
    gb = gate_b.reshape(2, 2, nch, 1, RG_CHUNK)
    gb = 0.5 * jnp.concatenate([gb[:, 0], gb[:, 1]], axis=-1)
    return wg, gb


def _rg_out_kernel(g_ref, hf_ref, hb_ref, perm_ref, w_ref, x_ref, mod_ref, gain_ref, wr_ref, br_ref,
                   xo_ref, f_ref, r_ref, cnt_ref, *, B, ctx_tiles, t0):
    is_ctx = pl.program_id(0) + t0 < ctx_tiles
    hsum = hf_ref[0].astype(F32) + hb_ref[0].astype(F32)
    lhs = (g_ref[...].astype(F32) * hsum).astype(BF16)
    lhs = _dot(perm_ref[...], lhs).astype(BF16)
    y = _dot(lhs, w_ref[...]).reshape(B, RG_TR, D_MODEL)
    m = lambda row: _time_tile_mod(mod_ref, is_ctx, B, row)
    xn = x_ref[...] + m(2) * y
    xo_ref[...] = xn
    f = _norm_mod(xn, gain_ref[...], m(3), m(4))
    f_ref[...] = f
    lg = _dot3_split(f.reshape(B * RG_TR, D_MODEL), wr_ref[...]) + br_ref[...]
    table, hist = _route(lg)
    r_ref[...] = table.reshape(B, RG_TR, 128)
    _accumulate_counts(cnt_ref, hist)


def _rg_out(st, latent_only, gate, hs, perm, w_out, x, mod, gain, wr, br):
    B = st.B
    R = B * RG_TR
    t0 = st.C // RG_TR if latent_only else 0
    lo = st.L if latent_only else st.ltot
    nt = lo // RG_TR
    const = lambda t: (0, 0)
    blk = lambda w: pl.BlockSpec((B, RG_TR, w), lambda t: (0, t, 0))
    xo, f, route, counts = pl.pallas_call(
        functools.partial(_rg_out_kernel, B=B, ctx_tiles=st.C // RG_TR, t0=t0),
        grid=(nt,),
        in_specs=[pl.BlockSpec((R, D_MODEL), lambda t: (t + t0, 0)),
                  pl.BlockSpec((1, R, D_MODEL), lambda t: (0, t + t0, 0)),
                  pl.BlockSpec((1, R, D_MODEL), lambda t: (1, t + t0, 0)),
                  pl.BlockSpec((R, R), const),
                  pl.BlockSpec((D_MODEL, D_MODEL), const),
                  pl.BlockSpec((B, RG_TR, D_MODEL), lambda t: (0, t + t0, 0)),
                  pl.BlockSpec((B + 1, 8, D_MODEL), lambda t: (0, 0, 0)),
                  pl.BlockSpec((1, D_MODEL), const),
                  pl.BlockSpec((D_MODEL, 256), const),
                  pl.BlockSpec((1, 128), const)],
        out_specs=[blk(D_MODEL), blk(D_MODEL), blk(128), pl.BlockSpec((1, 128), const)],
        out_shape=[jax.ShapeDtypeStruct((B, lo, D_MODEL), F32),
                   jax.ShapeDtypeStruct((B, lo, D_MODEL), F32),
                   jax.ShapeDtypeStruct((B, lo, 128), F32),
                   jax.ShapeDtypeStruct((1, 128), F32)],
        compiler_params=_cparams(("arbitrary",)),
        name="rg_out",
    )(gate, hs, hs, perm, w_out, x.reshape(B, st.ltot, D_MODEL), mod, gain, wr, br)
    return xo.reshape(B * lo, D_MODEL), f.reshape(B * lo, D_MODEL), route.reshape(B * lo, 128), counts


_ROPE_HALF = ROPE_AXIS_DIM // 2
_MLA_SRC_DIM = (list(range(MLA_NOPE + ROPE_AXIS_DIM)) + list(range(MLA_NOPE, MLA_NOPE + _ROPE_HALF))
                + list(range(MLA_NOPE + ROPE_AXIS_DIM, MLA_QK))
                + list(range(MLA_NOPE + ROPE_AXIS_DIM, MLA_NOPE + ROPE_AXIS_DIM + _ROPE_HALF)))
_MLA_REAL_LANE = ([1.0] * (MLA_NOPE + ROPE_AXIS_DIM) + [0.0] * _ROPE_HALF + [1.0] * ROPE_AXIS_DIM
                  + [0.0] * _ROPE_HALF + [0.0] * (MLA_HP - len(_MLA_SRC_DIM)))


def _mla_up_kernel(dn_ref, qn_ref, kvn_ref, wq_ref, wk_ref, we_ref, wv_ref, real_ref, cq_ref, sq_ref, ck_ref, sk_ref,
                   q_ref, k_ref, v_ref):
    dn = dn_ref[...]
    cq = _rms(dn[:, :MLA_Q_RANK]) * qn_ref[...]
    ckv = _rms(dn[:, MLA_Q_RANK:MLA_Q_RANK + MLA_KV_RANK]) * kvn_ref[...]
    kr = dn[:, MLA_Q_RANK + MLA_KV_RANK:]
    kr_hi = kr.astype(BF16)
    kr_lo = (kr - kr_hi.astype(F32)).astype(BF16)
    ckvb = ckv.astype(BF16)
    q_pre = _dot(cq.astype(BF16), wq_ref[...])
    k_pre = _dot(ckvb, wk_ref[...]) + (_dot(kr_hi, we_ref[...]) + _dot(kr_lo, we_ref[...]))
    v_ref[...] = _dot(ckvb, wv_ref[...]).astype(v_ref.dtype)
    real = real_ref[...]

    def head(x, cos_g, sin_g):
        ms = jnp.sum(x * x * real, axis=-1, keepdims=True) * (1.0 / MLA_QK)
        xr = x * lax.rsqrt(ms + RMS_EPS)
        return xr * cos_g + pltpu.roll(xr, MLA_HP - _ROPE_HALF, 1) * sin_g

    cos_q, sin_q, cos_k, sin_k = cq_ref[...], sq_ref[...], ck_ref[...], sk_ref[...]
    for h in range(MLA_HEADS):
        sl = slice(h * MLA_HP, (h + 1) * MLA_HP)
        q_ref[:, sl] = head(q_pre[:, sl], cos_q, sin_q).astype(q_ref.dtype)
    for h in range(MLA_HEADS):
        sl = slice(h * MLA_HP, (h + 1) * MLA_HP)
        k_ref[:, sl] = head(k_pre[:, sl], cos_k, sin_k).astype(k_ref.dtype)


def _mla_up(st, rows, down, q_norm, kv_norm, wq, wk, we, wv, tables):
    hw = MLA_HEADS * MLA_HP
    const = lambda i: (0, 0)
    tpb, ct, lt = st.tpb, st.ct, st.lt
    rope_idx = lambda i: (jnp.where(i % tpb < ct, lt, i % tpb - ct), 0)
    real = jnp.asarray(_MLA_REAL_LANE, F32).reshape(1, MLA_HP)
    return pl.pallas_call(
        _mla_up_kernel,
        grid=(rows.n,),
        in_specs=[pl.BlockSpec((TM, 512), lambda i: (i, 0)),
                  pl.BlockSpec((1, MLA_Q_RANK), const),
                  pl.BlockSpec((1, MLA_KV_RANK), const),
                  pl.BlockSpec((MLA_Q_RANK, hw), const),
                  pl.BlockSpec((MLA_KV_RANK, hw), const),
                  pl.BlockSpec((128, hw), const),
                  pl.BlockSpec((MLA_KV_RANK, MLA_HEADS * MLA_V), const),
                  pl.BlockSpec((1, MLA_HP), const)] + [pl.BlockSpec((TM, MLA_HP), rope_idx)] * 4,
        out_specs=[pl.BlockSpec((TM, hw), lambda i: (i, 0)),
                   pl.BlockSpec((TM, hw), lambda i: (i, 0)),
                   pl.BlockSpec((TM, MLA_HEADS * MLA_V), lambda i: (i, 0))],
        out_shape=[jax.ShapeDtypeStruct((st.NT, hw), BF16),
                   jax.ShapeDtypeStruct((st.NT, hw), BF16),
                   jax.ShapeDtypeStruct((st.NT, MLA_HEADS * MLA_V), BF16)],
        compiler_params=_cparams(("arbitrary",)),
        name="mla_up_proj",
    )(down, q_norm, kv_norm, wq, wk, we, wv, real, *tables)


def _attn_kernel(q_ref, k_ref, v_ref, o_ref, vaug_ref, *, C, ct):
    qi = pl.program_id(2)

    @pl.when(qi == 0)
    def _():
        lane = lax.broadcasted_iota(jnp.int32, (k_ref.shape[0], 2 * MLA_V), 1)
        for hh in range(ATT_HEADS):
            pair = v_ref[:, (hh // 2) * 2 * MLA_V:(hh // 2 + 1) * 2 * MLA_V].astype(F32)
            if hh % 2 == 0:
                aug = jnp.where(lane < MLA_V, pair, jnp.where(lane == MLA_V, 1.0, 0.0))
            else:
                aug = jnp.where(lane >= MLA_V, pair, jnp.where(lane == 0, 1.0, 0.0))
            vaug_ref[hh] = aug.astype(BF16)

    def attend(nkeys):
        lane = lax.broadcasted_iota(jnp.int32, (TM, 2 * MLA_V), 1)

        def scores(hh):
            sl = slice(hh * MLA_HP, (hh + 1) * MLA_HP)
            return _dot_t(q_ref[:, sl], k_ref[0:nkeys, sl])

        ahead = 2
        pending = [scores(hh) for hh in range(min(ahead, ATT_HEADS))]
        outs = []
        for hh in range(ATT_HEADS):
            s = pending.pop(0)
            if hh + ahead < ATT_HEADS:
                pending.append(scores(hh + ahead))
            p = jnp.exp2(s - jnp.max(s, axis=-1, keepdims=True)).astype(BF16)
            o = _dot(p, vaug_ref[hh, 0:nkeys, :])
            rowsum = o[:, MLA_V:MLA_V + 1] if hh % 2 == 0 else o[:, 0:1]
            outs.append(o * (1.0 / rowsum))
            if hh % 2 == 1:
                pair = hh // 2
                o_ref[:, pair * 2 * MLA_V:(pair + 1) * 2 * MLA_V] = jnp.where(
                    lane < MLA_V, outs[hh - 1], outs[hh]).astype(o_ref.dtype)

    pl.when(qi < ct)(lambda: attend(C))
    pl.when(qi >= ct)(lambda: attend(k_ref.shape[0]))


def _attention(st, q, k, v):
    B, ltot, tpb = st.B, st.ltot, st.tpb
    hg = MLA_HEADS // ATT_HEADS
    return pl.pallas_call(
        functools.partial(_attn_kernel, C=st.C, ct=st.ct),
        grid=(B, hg, tpb),
        in_specs=[pl.BlockSpec((TM, ATT_HEADS * MLA_HP), lambda b, h, i: (b * tpb + i, h)),
                  pl.BlockSpec((ltot, ATT_HEADS * MLA_HP), lambda b, h, i: (b, h)),
                  pl.BlockSpec((ltot, ATT_HEADS * MLA_V), lambda b, h, i: (b, h))],
        out_specs=pl.BlockSpec((TM, ATT_HEADS * MLA_V), lambda b, h, i: (b * tpb + i, h)),
        out_shape=jax.ShapeDtypeStruct((st.NT, MLA_HEADS * MLA_V), BF16),
        scratch_shapes=[pltpu.VMEM((ATT_HEADS, ltot, 2 * MLA_V), BF16)],
        compiler_params=_cparams(("arbitrary", "arbitrary", "arbitrary")),
        name="mla_attention",
    )(q, k, v)


def _mla_weights(w_uq, w_ukv):
    H = MLA_HEADS
    src = jnp.asarray(_MLA_SRC_DIM, jnp.int32)
    pad = MLA_HP - len(_MLA_SRC_DIM)
    wq = jnp.pad(w_uq.reshape(MLA_Q_RANK, H, MLA_QK)[:, :, src], ((0, 0), (0, 0), (0, pad)))
    wkv = w_ukv.reshape(MLA_KV_RANK, H, MLA_NOPE + MLA_V)
    wk = jnp.pad(wkv[:, :, :MLA_NOPE], ((0, 0), (0, 0), (0, MLA_HP - MLA_NOPE)))
    wv = wkv[:, :, MLA_NOPE:]
    r = jnp.arange(128)[:, None]
    lane_dim = jnp.pad(src, (0, pad), constant_values=-1)[None, :]
    place = (lane_dim == r + MLA_NOPE).astype(BF16)
    we = jnp.tile(place, (1, H))
    return (wq.reshape(MLA_Q_RANK, H * MLA_HP).astype(BF16), wk.reshape(MLA_KV_RANK, H * MLA_HP).astype(BF16),
            we, wv.reshape(MLA_KV_RANK, H * MLA_V).astype(BF16))


def _rope_tables(L, qk_norm):
    rows = L // GRID_W
    row = jnp.broadcast_to(jnp.arange(rows, dtype=F32)[:, None], (rows, GRID_W)).reshape(L)
    col = jnp.broadcast_to(jnp.arange(GRID_W, dtype=F32)[None, :], (rows, GRID_W)).reshape(L)
    inv_freq = ROPE_BASE ** (-jnp.arange(0, ROPE_AXIS_DIM, 2, dtype=F32) / ROPE_AXIS_DIM)
    ar = row[:, None] * inv_freq
    ac = col[:, None] * inv_freq
    h8 = _ROPE_HALF
    one = jnp.ones((L, MLA_NOPE), F32)
    z8 = jnp.zeros((L, h8), F32)
    zpad = jnp.zeros((L, MLA_HP - len(_MLA_SRC_DIM)), F32)
    cos_t = jnp.concatenate([one, jnp.cos(ar), jnp.cos(ar), z8, jnp.cos(ac), jnp.cos(ac), z8, zpad], axis=1)
    sin_t = jnp.concatenate([0 * one, -jnp.sin(ar), jnp.sin(ar), z8, -jnp.sin(ac), jnp.sin(ac), z8, zpad], axis=1)
    ident = jnp.asarray(_MLA_REAL_LANE, F32)[None, :]
    cos_t = jnp.concatenate([cos_t, jnp.broadcast_to(ident, (TM, MLA_HP))], axis=0)
    sin_t = jnp.concatenate([sin_t, jnp.zeros((TM, MLA_HP), F32)], axis=0)
    src = jnp.asarray(_MLA_SRC_DIM, jnp.int32)
    pad = MLA_HP - len(_MLA_SRC_DIM)
    scale = MLA_QK ** -0.5 * math.log2(math.e)
    tables = []
    for g, s in ((qk_norm[0], scale), (qk_norm[1], 1.0)):
        g_lane = jnp.pad(g[src], (0, pad))
        g_partner = jnp.roll(g_lane, -h8)
        tables += [cos_t * (g_lane * s)[None, :], sin_t * (g_partner * s)[None, :]]
    return tables


def _log_sigmoid(x):
    return jnp.minimum(x, 0.0) - jnp.log1p(jnp.exp(-jnp.abs(x)))


def _mlstm_kernel(qf_ref, kf_ref, vf_ref, gf_ref, qb_ref, kb_ref, vb_ref, gb_ref, bias_ref, tril_ref, triu_ref,
                  of_ref, ob_ref, c_ref, n_ref, m_ref):
    T = ML_TC

    @pl.when(pl.program_id(1) == 0)
    def _():
        c_ref[...] = jnp.zeros_like(c_ref)
        n_ref[...] = jnp.zeros_like(n_ref)
        m_ref[...] = jnp.full(m_ref.shape, ML_M_INIT, F32)

    ti = lax.broadcasted_iota(jnp.int32, (T, T), 0)
    si = lax.broadcasted_iota(jnp.int32, (T, T), 1)
    dirs = ((qf_ref, kf_ref, vf_ref, gf_ref, of_ref, tril_ref), (qb_ref, kb_ref, vb_ref, gb_ref, ob_ref, triu_ref))
    early = {}
    for d, (q_ref, k_ref, v_ref, g_ref, o_ref, tri_ref) in enumerate(dirs):
        for h in range(ML_HEADS):
            qh = q_ref[:, h * ML_DQK:(h + 1) * ML_DQK] * (ML_DQK ** -0.5)
            kh = k_ref[:, h * ML_DQK:(h + 1) * ML_DQK]
            qb16 = qh.astype(BF16)
            c_old = c_ref[d * ML_HEADS + h]
            early[d, h] = (qh, kh, c_old, _dot_t(qb16, kh.astype(BF16)), _dot(qb16, c_old.astype(BF16)))
    for d, (q_ref, k_ref, v_ref, g_ref, o_ref, tri_ref) in enumerate(dirs):
        tri = (si <= ti) if d == 0 else (si >= ti)
        g = g_ref[...] + bias_ref[...]
        g_t = g.T
        lsg = _log_sigmoid(g)
        lsg_hi = lsg.astype(BF16)
        lsg_lo = (lsg - lsg_hi.astype(F32)).astype(BF16)
        cum = _dot(tri_ref[...], lsg_hi) + _dot(tri_ref[...], lsg_lo)
        cum_t = cum.T
        last = T - 1 if d == 0 else 0
        for h in range(ML_HEADS):
            st = d * ML_HEADS + h
            li, lf_ = (2 * d) * ML_HEADS + h, (2 * d + 1) * ML_HEADS + h
            ig_col = g[:, li:li + 1]
            ig_row = g_t[li:li + 1, :]
            b_col = cum[:, lf_:lf_ + 1]
            b_row = cum_t[lf_:lf_ + 1, :]
            total = cum[last:last + 1, lf_:lf_ + 1]
            m_old = m_ref[st, 0:1, 0:1]
            d_log = jnp.where(tri, b_col - b_row + ig_row, -jnp.inf)
            inter_log = b_col + m_old
            m_t = jnp.maximum(inter_log, jnp.max(d_log, axis=1, keepdims=True))
            qh, kh, c_old, qk, qc = early[d, h]
            vh = v_ref[:, h * ML_DV:(h + 1) * ML_DV].astype(BF16)
            s_mat = qk * jnp.exp(d_log - m_t)
            inter = jnp.exp(inter_log - m_t)
            n_old = n_ref[st, 0:1, :]
            num = _dot(s_mat.astype(BF16), vh) + inter * qc
            den = jnp.sum(s_mat, axis=1, keepdims=True) + inter * jnp.sum(qh * n_old, axis=1, keepdims=True)
            o_ref[:, h * ML_DV:(h + 1) * ML_DV] = num / jnp.maximum(jnp.abs(den), jnp.exp(-m_t))
            w_log = total - b_col + ig_col
            m_new = jnp.maximum(total + m_old, jnp.max(w_log, axis=0, keepdims=True))
            w = jnp.exp(w_log - m_new)
            decay = jnp.exp(total + m_old - m_new)
            kw = kh * w
            c_ref[st] = decay * c_old + _dot(kw.T.astype(BF16), vh)
            n_ref[st, 0:1, :] = decay * n_old + jnp.sum(kw, axis=0, keepdims=True)
            m_ref[st] = jnp.broadcast_to(m_new, m_ref.shape[1:])


def _mlstm(st, proj, gate_bias):
    B = st.B
    assert st.L % ML_TC == 0 and st.C % ML_TC == 0
    cc = st.C // ML_TC
    nch = st.ltot // ML_TC
    qw = ML_HEADS * ML_DQK
    vw = ML_HEADS * ML_DV
    gcol = (2 * qw + 2 * vw) // 128

    def rb(d, b, k):
        chunk = k if d == 0 else jnp.where(k < cc, cc - 1 - k, nch - 1 - (k - cc))
        return b * nch + chunk

    def specs(d):
        return [pl.BlockSpec((ML_TC, qw), lambda b, k: (rb(d, b, k), 0)),
                pl.BlockSpec((ML_TC, qw), lambda b, k: (rb(d, b, k), 1)),
                pl.BlockSpec((ML_TC, vw), lambda b, k: (rb(d, b, k), (2 * qw) // vw)),
                pl.BlockSpec((ML_TC, 128), lambda b, k: (rb(d, b, k), gcol))]

    nst = 2 * ML_HEADS
    out = jax.ShapeDtypeStruct((st.NT, vw), F32)
    tril = jnp.tril(jnp.ones((ML_TC, ML_TC), BF16))
    return pl.pallas_call(
        _mlstm_kernel,
        grid=(B, nch),
        in_specs=specs(0) + specs(1) + [pl.BlockSpec((1, 128), lambda b, k: (0, 0)),
                                        pl.BlockSpec((ML_TC, ML_TC), lambda b, k: (0, 0)),
                                        pl.BlockSpec((ML_TC, ML_TC), lambda b, k: (0, 0))],
        out_specs=[pl.BlockSpec((ML_TC, vw), lambda b, k: (rb(0, b, k), 0)),
                   pl.BlockSpec((ML_TC, vw), lambda b, k: (rb(1, b, k), 0))],
        out_shape=[out, out],
        scratch_shapes=[pltpu.VMEM((nst, ML_DQK, ML_DV), F32),
                        pltpu.VMEM((nst, 8, ML_DQK), F32),
                        pltpu.VMEM((nst, 8, 128), F32)],
        compiler_params=_cparams(("arbitrary", "arbitrary")),
        name="mlstm_chunks",
    )(proj, proj, proj, proj, proj, proj, proj, proj, gate_bias, tril, tril.T)


def _route(lg):
    lane_i = lax.broadcasted_iota(jnp.int32, lg.shape, 1)
    lane = lane_i.astype(F32)
    neg = -jnp.inf
    gl = jnp.where(lane_i < MOE_GROUPS, lg, neg)
    gmax = jnp.max(gl, axis=-1, keepdims=True)
    gsum = jnp.sum(jnp.where(lane_i < MOE_GROUPS, jnp.exp(lg - gmax), 0.0), axis=-1, keepdims=True)
    p_top = 1.0 / gsum
    g_sel = jnp.min(jnp.where(gl == gmax, lane, 128.0), axis=-1, keepdims=True)
    group_of_lane = (lane_i >> 3).astype(F32) - 1.0
    el = jnp.where(group_of_lane == g_sel, lg, neg)
    e1 = jnp.max(el, axis=-1, keepdims=True)
    i1 = jnp.min(jnp.where(el == e1, lane, 128.0), axis=-1, keepdims=True)
    el2 = jnp.where(lane == i1, neg, el)
    e2 = jnp.max(el2, axis=-1, keepdims=True)
    i2 = jnp.min(jnp.where(el2 == e2, lane, 128.0), axis=-1, keepdims=True)
    t = jnp.exp(e2 - e1)
    w1 = p_top / (1.0 + t)
    w2 = w1 * t
    id1 = i1 - MOE_GROUPS
    id2 = i2 - MOE_GROUPS
    table = jnp.where(lane_i == 0, id1, jnp.where(lane_i == 1, id2,
                                                  jnp.where(lane_i == 2, w1, jnp.where(lane_i == 3, w2, 0.0))))
    chosen = jnp.where(lane == i1, 1.0, 0.0) + jnp.where(lane == i2, 1.0, 0.0)
    return table, jnp.sum(chosen, axis=0, keepdims=True)


def _accumulate_counts(cnt_ref, hist):
    @pl.when(pl.program_id(0) == 0)
    def _():
        cnt_ref[...] = jnp.zeros_like(cnt_ref)
    cnt_ref[...] += hist


def _lhs_mla(o_ref):
    return o_ref[...]


def _lhs_mlstm(hf_ref, hb_ref, og_ref, onorm_ref):
    hs = hf_ref[...] + hb_ref[...]
    og = _sigmoid(og_ref[...])
    parts = []
    for h in range(ML_HEADS):
        sl = slice(h * ML_DV, (h + 1) * ML_DV)
        parts.append(_rms(hs[:, sl]) * onorm_ref[:, sl] * og[:, sl])
    return jnp.concatenate(parts, axis=1)


def _mixer_out_kernel(*refs, n_lhs, lhs_fn):
    lhs_refs = refs[:n_lhs]
    w_ref, x_ref, mod_ref, gain_ref, wr_ref, br_ref, xo_ref, f_ref, r_ref, cnt_ref = refs[n_lhs:]
    y = _dot(lhs_fn(*lhs_refs).astype(BF16), w_ref[...])
    mod = mod_ref[0]
    xn = x_ref[...] + mod[2:3, :] * y
    xo_ref[...] = xn
    f = _norm_mod(xn, gain_ref[...], mod[3:4, :], mod[4:5, :])
    f_ref[...] = f
    r_ref[...], hist = _route(_dot3_split(f, wr_ref[...]) + br_ref[...])
    _accumulate_counts(cnt_ref, hist)


def _mixer_out(rows, lhs_fn, lhs_args, lhs_specs, w_out, x, mod, gain, wr, br, name):
    n = rows.n * TM
    const = lambda k: (0, 0)
    out = lambda w: pl.BlockSpec((TM, w), lambda k: (k, 0))
    return pl.pallas_call(
        functools.partial(_mixer_out_kernel, n_lhs=len(lhs_args), lhs_fn=lhs_fn),
        grid=(rows.n,),
        in_specs=list(lhs_specs) + [
            pl.BlockSpec((D_MODEL, D_MODEL), const),
            pl.BlockSpec((TM, D_MODEL), lambda k: (rows.src(k), 0)),
            pl.BlockSpec((1, 8, D_MODEL), lambda k: (rows.mod(k), 0, 0)),
            pl.BlockSpec((1, D_MODEL), const),
            pl.BlockSpec((D_MODEL, 256), const),
            pl.BlockSpec((1, 128), const)],
        out_specs=[out(D_MODEL), out(D_MODEL), out(128), pl.BlockSpec((1, 128), const)],
        out_shape=[jax.ShapeDtypeStruct((n, D_MODEL), F32),
                   jax.ShapeDtypeStruct((n, D_MODEL), F32),
                   jax.ShapeDtypeStruct((n, 128), F32),
                   jax.ShapeDtypeStruct((1, 128), F32)],
        compiler_params=_cparams(("arbitrary",)),
        name=name,
    )(*lhs_args, w_out, x, mod, gain, wr, br)


RANK_TILES = 4


def _rank_kernel(r_ref, start_ref, tril_ref, pos_ref, carry_ref):
    @pl.when(pl.program_id(0) == 0)
    def _():
        carry_ref[...] = jnp.zeros_like(carry_ref)

    r = r_ref[...]
    lane = lax.broadcasted_iota(jnp.int32, r.shape, 1).astype(F32)
    chosen = [jnp.where(lane == r[:, k:k + 1] + MOE_GROUPS, 1.0, 0.0) for k in range(MOE_TOPK)]
    both = functools.reduce(jnp.add, chosen)
    before = _dot(tril_ref[...], both.astype(BF16)) + (start_ref[...] + carry_ref[...])
    lane_i = lax.broadcasted_iota(jnp.int32, r.shape, 1)
    pos = jnp.zeros(r.shape, F32)
    for k in range(MOE_TOPK):
        pos = jnp.where(lane_i == k, jnp.sum(chosen[k] * before, axis=-1, keepdims=True), pos)
    pos_t = pos.T[0:8, :].astype(jnp.int32)
    for j in range(RANK_TILES):
        pos_ref[j] = pos_t[:, j * TM:(j + 1) * TM]
    carry_ref[...] += jnp.sum(both, axis=0, keepdims=True)


def _assignment_slots(ntiles, route, starts):
    assert ntiles % RANK_TILES == 0
    rows = RANK_TILES * TM
    tril = jnp.tril(jnp.ones((rows, rows), BF16), -1)
    return pl.pallas_call(
        _rank_kernel,
        grid=(ntiles // RANK_TILES,),
        in_specs=[pl.BlockSpec((rows, 128), lambda i: (i, 0)),
                  pl.BlockSpec((1, 128), lambda i: (0, 0)),
                  pl.BlockSpec((rows, rows), lambda i: (0, 0))],
        out_specs=pl.BlockSpec((RANK_TILES, 8, TM), lambda i: (i, 0, 0)),
        out_shape=jax.ShapeDtypeStruct((ntiles, 8, TM), jnp.int32),
        scratch_shapes=[pltpu.VMEM((1, 128), F32)],
        compiler_params=_cparams(("arbitrary",)),
        name="moe_rank",
    )(route, starts, tril)


def _row_wait(hbm, buf, sem):
    pltpu.make_async_copy(hbm.at[pl.ds(0, TM), :], buf, sem).wait()


def _dispatch_kernel(pos_ref, f_ref, xs_hbm, buf, sem, *, ntiles):
    i = pl.program_id(0)

    def step(s):
        @pl.when(i >= 2)
        def _():
            for _ in range(MOE_TOPK):
                _row_wait(xs_hbm, buf.at[s], sem.at[s])
        buf[s] = f_ref[...]
        for r in range(TM):
            for k in range(MOE_TOPK):
                pltpu.make_async_copy(buf.at[s, pl.ds(r, 1), :], xs_hbm.at[pl.ds(pos_ref[0, k, r], 1), :],
                                      sem.at[s]).start(priority=k % 2)

        @pl.when(i == ntiles - 1)
        def _():
            for slot in ((1 - s, s) if ntiles >= 2 else (s,)):
                for _ in range(MOE_TOPK):
                    _row_wait(xs_hbm, buf.at[slot], sem.at[slot])

    for s in range(2):
        pl.when(i % 2 == s)(functools.partial(step, s))


def _dispatch(ntiles, f, pos):
    n = ntiles * TM
    return pl.pallas_call(
        functools.partial(_dispatch_kernel, ntiles=ntiles),
        grid=(ntiles,),
        in_specs=[pl.BlockSpec((1, 8, TM), lambda i: (i, 0, 0), memory_space=pltpu.SMEM),
                  pl.BlockSpec((TM, D_MODEL), lambda i: (i, 0))],
        out_specs=pl.BlockSpec(memory_space=pl.ANY),
        out_shape=jax.ShapeDtypeStruct((MOE_TOPK * n, D_MODEL), F32),
        scratch_shapes=[pltpu.VMEM((2, TM, D_MODEL), F32), pltpu.SemaphoreType.DMA((2,))],
        compiler_params=_cparams(("arbitrary",)),
        name="moe_dispatch",
    )(pos, f)


def _expert_kernel(vb_ref, ve_ref, lo_ref, hi_ref, first_ref, newexp_ref, eslot_ref, enext_ref, x_ref, wgu_hbm,
                   wd_hbm, y_ref, wgu_f, wd_f, wgu_b, wd_b, sem, *, layer):
    v = pl.program_id(0)

    def fetch(e, s):
        return (pltpu.make_async_copy(wgu_hbm.at[layer, e], wgu_f.at[s], sem.at[0, s]),
                pltpu.make_async_copy(wd_hbm.at[layer, e], wd_f.at[s], sem.at[1, s]))

    @pl.when(newexp_ref[v] == 1)
    def _():
        for s in range(2):
            @pl.when(eslot_ref[v] == s)
            def _():
                @pl.when(v == 0)
                def _():
                    for c in fetch(ve_ref[v], s):
                        c.start()

                @pl.when(enext_ref[v] >= 0)
                def _():
                    for c in fetch(enext_ref[v], 1 - s):
                        c.start()
                for c in fetch(ve_ref[v], s):
                    c.wait()
                wgu_b[...] = wgu_f[s].astype(BF16)
                wd_b[...] = wd_f[s].astype(BF16)

    @pl.when(hi_ref[v] > lo_ref[v])
    def _():
        gu = _dot(x_ref[...].astype(BF16), wgu_b[...])
        gate = gu[:, :MOE_FF]
        act = gate * _sigmoid(gate) * gu[:, MOE_FF:]
        y = _dot(act.astype(BF16), wd_b[...])
        r = lax.broadcasted_iota(jnp.int32, (MOE_BM, 1), 0)
        mine = jnp.logical_and(r >= lo_ref[v], r < hi_ref[v])
        y = jnp.where(mine, y, 0.0)

        @pl.when(first_ref[v] == 1)
        def _():
            y_ref[...] = y

        @pl.when(first_ref[v] == 0)
        def _():
            y_ref[...] += y


def _expert_ffn(xs, visits, layer, w_gate_up, w_down):
    nvis = visits[0].shape[0]
    blk_idx = lambda v, vb, *_: (vb[v], 0)
    grid_spec = pltpu.PrefetchScalarGridSpec(
        num_scalar_prefetch=8,
        grid=(nvis,),
        in_specs=[pl.BlockSpec((MOE_BM, D_MODEL), blk_idx),
                  pl.BlockSpec(memory_space=pl.ANY),
                  pl.BlockSpec(memory_space=pl.ANY)],
        out_specs=pl.BlockSpec((MOE_BM, D_MODEL), blk_idx),
        scratch_shapes=[pltpu.VMEM((2, D_MODEL, 2 * MOE_FF), F32), pltpu.VMEM((2, MOE_FF, D_MODEL), F32),
                        pltpu.VMEM((D_MODEL, 2 * MOE_FF), BF16), pltpu.VMEM((MOE_FF, D_MODEL), BF16),
                        pltpu.SemaphoreType.DMA((2, 2))],
    )
    return pl.pallas_call(
        functools.partial(_expert_kernel, layer=layer),
        grid_spec=grid_spec,
        out_shape=jax.ShapeDtypeStruct(xs.shape, F32),
        compiler_params=_cparams(("arbitrary",)),
        name="moe_expert_ffn",
    )(*visits, xs, w_gate_up, w_down)


def _combine_kernel(pos_ref, nxt_ref, x_ref, r_ref, mod_ref, ys_hbm, o_ref, ybuf, sem, *, ntiles):
    i = pl.program_id(0)

    def gather(table, s):
        for r in range(TM):
            for k in range(MOE_TOPK):
                pltpu.make_async_copy(ys_hbm.at[pl.ds(table[0, k, r], 1), :], ybuf.at[s, k, pl.ds(r, 1), :],
                                      sem.at[s]).start(priority=k % 2)

    @pl.when(i == 0)
    def _():
        gather(pos_ref, 0)

    def step(s):
        @pl.when(i + 1 < ntiles)
        def _():
            gather(nxt_ref, 1 - s)
        for k in range(MOE_TOPK):
            _row_wait(ys_hbm, ybuf.at[s, k], sem.at[s])
        w = r_ref[...]
        y = w[:, MOE_TOPK:MOE_TOPK + 1] * ybuf[s, 0]
        for k in range(1, MOE_TOPK):
            y = y + w[:, MOE_TOPK + k:MOE_TOPK + k + 1] * ybuf[s, k]
        o_ref[...] = x_ref[...] + mod_ref[0][5:6, :] * y

    for s in range(2):
        pl.when(i % 2 == s)(functools.partial(step, s))


def _combine(rows, x, ys, pos, route, mod):
    n = rows.n
    spec = pl.BlockSpec((TM, D_MODEL), lambda i: (i, 0))
    return pl.pallas_call(
        functools.partial(_combine_kernel, ntiles=n),
        grid=(n,),
        in_specs=[pl.BlockSpec((1, 8, TM), lambda i: (i, 0, 0), memory_space=pltpu.SMEM),
                  pl.BlockSpec((1, 8, TM), lambda i: (jnp.minimum(i + 1, n - 1), 0, 0), memory_space=pltpu.SMEM),
                  spec,
                  pl.BlockSpec((TM, 128), lambda i: (i, 0)),
                  pl.BlockSpec((1, 8, D_MODEL), lambda i: (rows.mod(i), 0, 0)),
                  pl.BlockSpec(memory_space=pl.ANY)],
        out_specs=spec,
        out_shape=jax.ShapeDtypeStruct((n * TM, D_MODEL), F32),
        scratch_shapes=[pltpu.VMEM((2, MOE_TOPK, TM, D_MODEL), F32), pltpu.SemaphoreType.DMA((2,))],
        compiler_params=_cparams(("arbitrary",)),
        name="moe_combine",
    )(pos, pos, x, route, mod, ys)


def _visit_tables(bounds, nk):
    E = MOE_EXPERTS
    nblk = nk // MOE_BM
    nvis = nblk + E
    starts, ends = bounds[:-1], bounds[1:]
    fb = starts // MOE_BM
    nv = jnp.where(ends > starts, (ends - 1) // MOE_BM - fb + 1, 0)
    cum = jnp.cumsum(nv)
    total = cum[-1]
    v = jnp.arange(nvis, dtype=jnp.int32)
    active = v < total
    vc = jnp.minimum(v, total - 1)
    ve = jnp.minimum(jnp.sum((cum[None, :] <= vc[:, None]).astype(jnp.int32), axis=1), E - 1)
    vb = fb[ve] + (vc - (cum - nv)[ve])
    lo = jnp.where(active, jnp.maximum(starts[ve], vb * MOE_BM) - vb * MOE_BM, 0)
    hi = jnp.where(active, jnp.minimum(ends[ve], (vb + 1) * MOE_BM) - vb * MOE_BM, 0)
    prev_b = jnp.concatenate([jnp.full((1,), -1, jnp.int32), vb[:-1]])
    first = jnp.logical_and(active, vb != prev_b)
    prev_e = jnp.concatenate([jnp.full((1,), -1, jnp.int32), ve[:-1]])
    new_expert = jnp.logical_and(active, ve != prev_e)
    eslot = (jnp.cumsum(new_expert.astype(jnp.int32)) - 1) % 2
    at = jnp.where(new_expert, v, nvis)
    nxt = jnp.concatenate([lax.cummin(at[::-1])[::-1][1:], jnp.full((1,), nvis, jnp.int32)])
    enext = jnp.where(nxt < nvis, ve[jnp.minimum(nxt, nvis - 1)], -1)
    i32 = lambda a: a.astype(jnp.int32)
    return i32(vb), i32(ve), i32(lo), i32(hi), i32(first), i32(new_expert), i32(eslot), i32(enext)


def _moe(rows, x, f, route, counts, mod, layer, w_gate_up, w_down):
    n = rows.n * TM
    cum = jnp.cumsum(counts[0])
    starts = (cum - counts[0]).reshape(1, 128)
    bounds = jnp.concatenate([starts[0, MOE_GROUPS:MOE_GROUPS + MOE_EXPERTS], cum[-1:]]).astype(jnp.int32)
    pos = _assignment_slots(rows.n, route, starts)
    xs = _dispatch(rows.n, f, pos)
    ys = _expert_ffn(xs, _visit_tables(bounds, n * MOE_TOPK), layer, w_gate_up, w_down)
    return _combine(rows, x, ys, pos, route, mod)


def kernel(x, c, ctx, c_ctx, ada_w, ada_b, norm_mix, norm_ffn, rg_w_in, rg_conv_w, rg_conv_b, rg_gate_w, rg_gate_b, rg_lambda, rg_w_out, mla_w_down, mla_q_norm, mla_kv_norm, mla_w_uq, mla_w_ukv, mla_qk_norm, mla_w_o, ml_w_in, ml_gate_b, ml_out_norm, ml_w_out, moe_w_group, moe_b_group, moe_w_expert, moe_b_expert, moe_w_gate_up, moe_w_down):
    B, L, D = x.shape
    C = ctx.shape[1]
    depth = ada_w.shape[0]
    assert D == D_MODEL
    st = _Stream(B, L, C)

    xs = jnp.concatenate([ctx, x], axis=1).reshape(st.NT, D)
    cc = jnp.zeros((16, D), F32).at[:B].set(c).at[B].set(c_ctx)
    mod_all = _modulation(cc, ada_w, ada_b)
    mod_all = jnp.pad(mod_all[:, :B + 1].reshape(depth, B + 1, 6, D), ((0, 0), (0, 0), (0, 2), (0, 0)))
    perm = _row_permutation(B)

    row = lambda a: a.reshape(1, -1)

    for i in range(depth):
        last = i == depth - 1
        mod = mod_all[i]
        kind, j = i % 3, i // 3
        all_rows = st.all_rows()
        out_rows = st.latent_rows() if last else all_rows
        tile_spec = lambda w: pl.BlockSpec((TM, w), lambda k: (out_rows.src(k), 0))
        wr = jnp.zeros((D, 128), F32).at[:, :MOE_GROUPS].set(moe_w_group[i]) \
            .at[:, MOE_GROUPS:MOE_GROUPS + MOE_EXPERTS].set(moe_w_expert[i])
        wr_hi = wr.astype(BF16)
        wr = jnp.concatenate([wr_hi, (wr - wr_hi.astype(F32)).astype(BF16)], axis=1)
        br = jnp.zeros((1, 128), F32).at[0, :MOE_GROUPS].set(moe_b_group[i]) \
            .at[0, MOE_GROUPS:MOE_GROUPS + MOE_EXPERTS].set(moe_b_expert[i])
        out_args = (xs, mod, row(norm_ffn[i]), wr, br)

        if kind == 0:
            gate, u = _rg_in(st, xs, mod, row(norm_mix[i]), perm, rg_w_in[j].astype(BF16))
            wg, gb = _rg_gate_weights(rg_gate_w[j], rg_gate_b[j])
            hs = _rg_scan(st, u, rg_conv_w[j], row(rg_conv_b[j]), wg, gb, rg_lambda[j].reshape(2, 1, D))
            xs, f, route, counts = _rg_out(st, last, gate, hs, perm.T, rg_w_out[j].astype(BF16), *out_args)
        elif kind == 1:
            w_down = jnp.pad(mla_w_down[j], ((0, 0), (0, 512 - mla_w_down.shape[2]))).astype(BF16)
            down = _norm_proj(all_rows, xs, mod, row(norm_mix[i]), w_down, name="mla_down_proj")
            wq, wk, we, wv = _mla_weights(mla_w_uq[j], mla_w_ukv[j])
            q, k, v = _mla_up(st, all_rows, down, row(mla_q_norm[j]), row(mla_kv_norm[j]), wq, wk, we, wv,
                              _rope_tables(L, mla_qk_norm[j]))
            o = _attention(st, q, k, v)
            xs, f, route, counts = _mixer_out(out_rows, _lhs_mla, (o,), (tile_spec(D),),
                                      mla_w_o[j].astype(BF16), *out_args, name="mla_out")
        else:
            n_in = ml_w_in.shape[2]
            w_in = jnp.pad(ml_w_in[j], ((0, 0), (0, ML_NP - n_in))).astype(BF16)
            proj = _norm_proj(all_rows, xs, mod, row(norm_mix[i]), w_in, name="mlstm_in_proj")
            gate_bias = jnp.pad(ml_gate_b[j].reshape(1, -1), ((0, 0), (0, 128 - 4 * ML_HEADS)))
            hf, hb = _mlstm(st, proj, gate_bias)
            og_spec = pl.BlockSpec((TM, D), lambda k: (out_rows.src(k), 2))
            xs, f, route, counts = _mixer_out(out_rows, _lhs_mlstm, (hf, hb, proj, row(ml_out_norm[j])),
                                      (tile_spec(D), tile_spec(D), og_spec, pl.BlockSpec((1, D), lambda k: (0, 0))),
                                      ml_w_out[j].astype(BF16), *out_args, name="mlstm_out")

        moe_rows = st.dense_latent_rows() if last else all_rows
        xs = _moe(moe_rows, xs, f, route, counts, mod, i, moe_w_gate_up, moe_w_down)

    return xs.reshape(B, L, D)
```

```python
import functools
import math

import jax
import jax.numpy as jnp
from jax import lax
from jax.experimental import pallas as pl
from jax.experimental.pallas import tpu as pltpu

F32 = jnp.float32
BF16 = jnp.bfloat16

D_MODEL = 1024
RMS_EPS = 1e-6

TM = 256
VMEM_LIMIT = 48 * 1024 * 1024

RG_BLOCK_W = 64
RG_CHUNK = 256
RG_CONV_W = 4
RG_C = 8.0
RG_TT = 64

MLA_HEADS = 16
MLA_Q_RANK = 256
MLA_KV_RANK = 128
MLA_NOPE = 64
MLA_ROPE = 32
MLA_V = 64
MLA_QK = MLA_NOPE + MLA_ROPE
MLA_HP = 128
ROPE_AXIS_DIM = MLA_ROPE // 2
ROPE_BASE = 10000.0
GRID_W = 64
ATT_HEADS = 8

ML_HEADS = 4
ML_DV = 256
ML_DQK = 128
ML_TC = 128
ML_M_INIT = -1e30
ML_NP = 3200

MOE_GROUPS = 8
MOE_PER_GROUP = 8
MOE_EXPERTS = 64
MOE_TOPK = 2
MOE_FF = 256
MOE_BM = 512


def _cparams(sem):
    return pltpu.CompilerParams(dimension_semantics=sem, vmem_limit_bytes=VMEM_LIMIT)


def _dot(a, b):
    return jnp.dot(a, b, preferred_element_type=F32)


def _dot_t(a, b):
    return lax.dot_general(a, b, (((1,), (1,)), ((), ())), preferred_element_type=F32)


def _dot3(a, b):
    ah = a.astype(BF16)
    al = (a - ah.astype(F32)).astype(BF16)
    bh = b.astype(BF16)
    bl = (b - bh.astype(F32)).astype(BF16)
    return _dot(ah, bh) + (_dot(al, bh) + _dot(ah, bl))


def _dot3_split(a, w2):
    n = w2.shape[1] // 2
    ah = a.astype(BF16)
    al = (a - ah.astype(F32)).astype(BF16)
    both = _dot(ah, w2)
    return both[:, :n] + (_dot(al, w2[:, :n]) + both[:, n:])


def _sigmoid(x):
    return 0.5 * jnp.tanh(0.5 * x) + 0.5


def _softplus(x):
    return jnp.maximum(x, 0.0) + jnp.log1p(jnp.exp(-jnp.abs(x)))


def _gelu_tanh(x):
    return 0.5 * x * (1.0 + jnp.tanh(0.7978845608028654 * (x + 0.044715 * (x * x * x))))


def _rms(x, n=None):
    n = x.shape[-1] if n is None else n
    ms = jnp.sum(x * x, axis=-1, keepdims=True) * (1.0 / n)
    return x * lax.rsqrt(ms + RMS_EPS)


class _Rows:
    def __init__(self, n, src, mod):
        self.n, self.src, self.mod = n, src, mod


class _Stream:
    def __init__(self, B, L, C):
        assert L % TM == 0 and C % TM == 0 and B % 8 == 0
        self.B, self.L, self.C = B, L, C
        self.ltot = L + C
        self.lt, self.ct = L // TM, C // TM
        self.tpb = self.lt + self.ct
        self.NT = B * self.ltot

    def all_rows(self):
        tpb, ct, B = self.tpb, self.ct, self.B
        return _Rows(B * tpb, lambda k: k, lambda k: jnp.where(k % tpb < ct, B, k // tpb))

    def latent_rows(self):
        tpb, ct, lt = self.tpb, self.ct, self.lt
        return _Rows(self.B * lt, lambda k: (k // lt) * tpb + ct + k % lt, lambda k: k // lt)

    def dense_latent_rows(self):
        lt = self.lt
        return _Rows(self.B * lt, lambda k: k, lambda k: k // lt)


def _mod_kernel(c_ref, w_ref, b_ref, o_ref):
    c = c_ref[...]
    o_ref[0] = _dot3(c * _sigmoid(c), w_ref[0]) + b_ref[0]


def _modulation(cc, ada_w, ada_b):
    depth, d, n = ada_w.shape
    tn = 1536
    return pl.pallas_call(
        _mod_kernel,
        grid=(depth, n // tn),
        in_specs=[pl.BlockSpec((16, d), lambda l, j: (0, 0)),
                  pl.BlockSpec((1, d, tn), lambda l, j: (l, 0, j)),
                  pl.BlockSpec((1, 1, tn), lambda l, j: (l, 0, j))],
        out_specs=pl.BlockSpec((1, 16, tn), lambda l, j: (l, 0, j)),
        out_shape=jax.ShapeDtypeStruct((depth, 16, n), F32),
        compiler_params=_cparams(("arbitrary", "arbitrary")),
        name="ada_modulation",
    )(cc, ada_w, ada_b.reshape(depth, 1, n))


def _norm_mod(x, gain, shift, scale):
    return _rms(x) * gain * (1.0 + scale) + shift


def _norm_proj_kernel(x_ref, mod_ref, g_ref, w_ref, o_ref):
    mod = mod_ref[0]
    h = _norm_mod(x_ref[...], g_ref[...], mod[0:1, :], mod[1:2, :]).astype(BF16)
    o_ref[...] = _dot(h, w_ref[...]).astype(o_ref.dtype)


def _norm_proj(rows, x, mod, gain, w, name):
    n = w.shape[1]
    const = lambda k: (0, 0)
    return pl.pallas_call(
        _norm_proj_kernel,
        grid=(rows.n,),
        in_specs=[pl.BlockSpec((TM, D_MODEL), lambda k: (rows.src(k), 0)),
                  pl.BlockSpec((1, 8, D_MODEL), lambda k: (rows.mod(k), 0, 0)),
                  pl.BlockSpec((1, D_MODEL), const),
                  pl.BlockSpec((D_MODEL, n), const)],
        out_specs=pl.BlockSpec((TM, n), lambda k: (k, 0)),
        out_shape=jax.ShapeDtypeStruct((rows.n * TM, n), F32),
        compiler_params=_cparams(("arbitrary",)),
        name=name,
    )(x, mod, gain, w)


RG_TR = 32


def _row_permutation(B):
    n = B * RG_TR
    r = jnp.arange(n)
    src = (r % B) * RG_TR + r // B
    return (src[:, None] == jnp.arange(n)[None, :]).astype(BF16)


def _time_tile_mod(mod_ref, is_ctx, B, row):
    return jnp.where(is_ctx, mod_ref[B:B + 1, row:row + 1, :], mod_ref[0:B, row:row + 1, :])


def _rg_in_kernel(x_ref, mod_ref, g_ref, perm_ref, wg_ref, wu_ref, gate_ref, u_ref, *, B, ctx_tiles):
    is_ctx = pl.program_id(0) < ctx_tiles
    h = _norm_mod(x_ref[...], g_ref[...], _time_tile_mod(mod_ref, is_ctx, B, 0), _time_tile_mod(mod_ref, is_ctx, B, 1))
    h = h.reshape(B * RG_TR, D_MODEL).astype(BF16)
    h = _dot(perm_ref[...], h).astype(BF16)
    gate_ref[...] = _gelu_tanh(_dot(h, wg_ref[...])).astype(gate_ref.dtype)
    u_ref[...] = _dot(h, wu_ref[...])


def _rg_in(st, x, mod, gain, perm, w_in):
    B, W = st.B, D_MODEL
    R = B * RG_TR
    nt = st.ltot // RG_TR
    const = lambda t: (0, 0)
    return pl.pallas_call(
        functools.partial(_rg_in_kernel, B=B, ctx_tiles=st.C // RG_TR),
        grid=(nt,),
        in_specs=[pl.BlockSpec((B, RG_TR, D_MODEL), lambda t: (0, t, 0)),
                  pl.BlockSpec((B + 1, 8, D_MODEL), lambda t: (0, 0, 0)),
                  pl.BlockSpec((1, D_MODEL), const),
                  pl.BlockSpec((R, R), const),
                  pl.BlockSpec((D_MODEL, W), const),
                  pl.BlockSpec((D_MODEL, W), lambda t: (0, 1))],
        out_specs=[pl.BlockSpec((R, W), lambda t: (t, 0)),
                   pl.BlockSpec((R, W), lambda t: (t, 0))],
        out_shape=[jax.ShapeDtypeStruct((st.ltot * B, W), BF16), jax.ShapeDtypeStruct((st.ltot * B, W), F32)],
        compiler_params=_cparams(("arbitrary",)),
        name="rg_in_proj",
    )(x.reshape(B, st.ltot, D_MODEL), mod, gain, perm, w_in, w_in)


def _rg_tile_order(d, k, ct, ntt):
    bwd = jnp.where(k < ct, ct - 1 - k, ntt - 1 - (k - ct))
    return jnp.where(d == 0, k, bwd)


def _rg_scan_kernel(um_ref, up_ref, un_ref, cw_ref, cb_ref, wg_ref, gb_ref, lam_ref, o_ref,
                    ext_ref, a_ref, b_ref, h_ref, *, B, ct, ntt):
    d = pl.program_id(0)
    k = pl.program_id(1)
    tile = _rg_tile_order(d, k, ct, ntt)
    R = RG_TT * B

    @pl.when(k == 0)
    def _():
        h_ref[...] = jnp.zeros_like(h_ref)

    seq_start = jnp.logical_or(tile == 0, tile == ct)
    seq_end = jnp.logical_or(tile == ct - 1, tile == ntt - 1)
    ext_ref[0:2 * B, :] = jnp.where(seq_start, 0.0, up_ref[...])
    ext_ref[2 * B:2 * B + R, :] = um_ref[...]
    ext_ref[2 * B + R:3 * B + R, :] = jnp.where(seq_end, 0.0, un_ref[...])
    cw = cw_ref[...]
    uc = cb_ref[...] + cw[0:1, :] * ext_ref[0:R, :]
    for j in range(1, RG_CONV_W):
        uc = uc + cw[j:j + 1, :] * ext_ref[j * B:j * B + R, :]
    ucb = uc.astype(BF16)
    c_half = (-0.5 * RG_C) * _softplus(-lam_ref[0])
    for c in range(D_MODEL // RG_CHUNK):
        sl = slice(c * RG_CHUNK, (c + 1) * RG_CHUNK)
        z = _dot(ucb[:, sl], wg_ref[0, c]) + gb_ref[0, c]
        log_a = c_half[:, sl] * jnp.tanh(z[:, :RG_CHUNK]) + c_half[:, sl]
        u_half = 0.5 * uc[:, sl]
        gated_u = jnp.tanh(z[:, RG_CHUNK:]) * u_half + u_half
        a = jnp.exp(log_a)
        one_minus_a2 = -jnp.tanh(log_a) * (a * a + 1.0)
        a_ref[:, sl] = a
        b_ref[:, sl] = jnp.sqrt(one_minus_a2) * gated_u

    def scan(times):
        for c in range(D_MODEL // 128):
            cs = slice(c * 128, (c + 1) * 128)
            h = h_ref[:, cs]
            for t in times:
                rs = slice(t * B, (t + 1) * B)
                h = a_ref[rs, cs] * h + b_ref[rs, cs]
                b_ref[rs, cs] = h
            h_ref[:, cs] = h

    pl.when(d == 0)(lambda: scan(range(RG_TT)))
    pl.when(d == 1)(lambda: scan(range(RG_TT - 1, -1, -1)))
    o_ref[0] = b_ref[...].astype(o_ref.dtype)


def _rg_scan(st, u_tm, conv_w, conv_b, wg, gb, lam):
    B, W = st.B, D_MODEL
    assert st.C % RG_TT == 0 and st.L % RG_TT == 0
    ltot = st.ltot
    ntt, ct = ltot // RG_TT, st.C // RG_TT
    R = RG_TT * B
    order = functools.partial(_rg_tile_order, ct=ct, ntt=ntt)
    nch = W // RG_CHUNK
    return pl.pallas_call(
        functools.partial(_rg_scan_kernel, B=B, ct=ct, ntt=ntt),
        grid=(2, ntt),
        in_specs=[pl.BlockSpec((R, W), lambda d, k: (order(d, k), 0)),
                  pl.BlockSpec((2 * B, W), lambda d, k: (jnp.maximum(order(d, k) * (RG_TT // 2) - 1, 0), 0)),
                  pl.BlockSpec((B, W), lambda d, k: (jnp.minimum((order(d, k) + 1) * RG_TT, ltot - 1), 0)),
                  pl.BlockSpec((RG_CONV_W, W), lambda d, k: (0, 0)),
                  pl.BlockSpec((1, W), lambda d, k: (0, 0)),
                  pl.BlockSpec((1, nch, RG_CHUNK, 2 * RG_CHUNK), lambda d, k: (d, 0, 0, 0)),
                  pl.BlockSpec((1, nch, 1, 2 * RG_CHUNK), lambda d, k: (d, 0, 0, 0)),
                  pl.BlockSpec((1, 1, W), lambda d, k: (d, 0, 0))],
        out_specs=pl.BlockSpec((1, R, W), lambda d, k: (d, order(d, k), 0)),
        out_shape=jax.ShapeDtypeStruct((2, ltot * B, W), BF16),
        scratch_shapes=[pltpu.VMEM((3 * B + R, W), F32),
                        pltpu.VMEM((R, W), F32),
                        pltpu.VMEM((R, W), F32),
                        pltpu.VMEM((B, W), F32)],
        compiler_params=_cparams(("arbitrary", "arbitrary")),
        name="rg_scan",
    )(u_tm, u_tm, u_tm, conv_w, conv_b, wg, gb, lam)


def _rg_gate_weights(gate_w, gate_b):
    nb = gate_w.shape[2]
    per = RG_CHUNK // RG_BLOCK_W
    nch = nb // per
    gw = gate_w.reshape(2, 2, nch, per, RG_BLOCK_W, RG_BLOCK_W)
    eye = jnp.eye(per, dtype=gate_w.dtype)
    bd = jnp.einsum('dgcnij,nm->dgcnimj', gw, eye).reshape(2, 2, nch, RG_CHUNK, RG_CHUNK)
    wg = (0.5 * jnp.concatenate([bd[:, 0], bd[:, 1]], axis=-1)).astype(BF16)
    gb = gate_b.reshape(2, 2, nch, 1, RG_CHUNK)
    gb = 0.5 * jnp.concatenate([gb[:, 0], gb[:, 1]], axis=-1)
    return wg, gb


def _rg_out_kernel(g_ref, hf_ref, hb_ref, perm_ref, w_ref, x_ref, mod_ref, gain_ref, wr_ref, br_ref,
                   xo_ref, f_ref, r_ref, cnt_ref, *, B, ctx_tiles, t0):
    is_ctx = pl.program_id(0) + t0 < ctx_tiles
    hsum = hf_ref[0].astype(F32) + hb_ref[0].astype(F32)
    lhs = (g_ref[...].astype(F32) * hsum).astype(BF16)
    lhs = _dot(perm_ref[...], lhs).astype(BF16)
    y = _dot(lhs, w_ref[...]).reshape(B, RG_TR, D_MODEL)
    m = lambda row: _time_tile_mod(mod_ref, is_ctx, B, row)
    xn = x_ref[...] + m(2) * y
    xo_ref[...] = xn
    f = _norm_mod(xn, gain_ref[...], m(3), m(4))
    f_ref[...] = f
    lg = _dot3_split(f.reshape(B * RG_TR, D_MODEL), wr_ref[...]) + br_ref[...]
    table, hist = _route(lg)
    r_ref[...] = table.reshape(B, RG_TR, 128)
    _accumulate_counts(cnt_ref, hist)


def _rg_out(st, latent_only, gate, hs, perm, w_out, x, mod, gain, wr, br):
    B = st.B
    R = B * RG_TR
    t0 = st.C // RG_TR if latent_only else 0
    lo = st.L if latent_only else st.ltot
    nt = lo // RG_TR
    const = lambda t: (0, 0)
    blk = lambda w: pl.BlockSpec((B, RG_TR, w), lambda t: (0, t, 0))
    xo, f, route, counts = pl.pallas_call(
        functools.partial(_rg_out_kernel, B=B, ctx_tiles=st.C // RG_TR, t0=t0),
        grid=(nt,),
        in_specs=[pl.BlockSpec((R, D_MODEL), lambda t: (t + t0, 0)),
                  pl.BlockSpec((1, R, D_MODEL), lambda t: (0, t + t0, 0)),
                  pl.BlockSpec((1, R, D_MODEL), lambda t: (1, t + t0, 0)),
                  pl.BlockSpec((R, R), const),
                  pl.BlockSpec((D_MODEL, D_MODEL), const),
                  pl.BlockSpec((B, RG_TR, D_MODEL), lambda t: (0, t + t0, 0)),
                  pl.BlockSpec((B + 1, 8, D_MODEL), lambda t: (0, 0, 0)),
                  pl.BlockSpec((1, D_MODEL), const),
                  pl.BlockSpec((D_MODEL, 256), const),
                  pl.BlockSpec((1, 128), const)],
        out_specs=[blk(D_MODEL), blk(D_MODEL), blk(128), pl.BlockSpec((1, 128), const)],
        out_shape=[jax.ShapeDtypeStruct((B, lo, D_MODEL), F32),
                   jax.ShapeDtypeStruct((B, lo, D_MODEL), F32),
                   jax.ShapeDtypeStruct((B, lo, 128), F32),
                   jax.ShapeDtypeStruct((1, 128), F32)],
        compiler_params=_cparams(("arbitrary",)),
        name="rg_out",
    )(gate, hs, hs, perm, w_out, x.reshape(B, st.ltot, D_MODEL), mod, gain, wr, br)
    return xo.reshape(B * lo, D_MODEL), f.reshape(B * lo, D_MODEL), route.reshape(B * lo, 128), counts


_ROPE_HALF = ROPE_AXIS_DIM // 2
_MLA_SRC_DIM = (list(range(MLA_NOPE + ROPE_AXIS_DIM)) + list(range(MLA_NOPE, MLA_NOPE + _ROPE_HALF))
                + list(range(MLA_NOPE + ROPE_AXIS_DIM, MLA_QK))
                + list(range(MLA_NOPE + ROPE_AXIS_DIM, MLA_NOPE + ROPE_AXIS_DIM + _ROPE_HALF)))
_MLA_REAL_LANE = ([1.0] * (MLA_NOPE + ROPE_AXIS_DIM) + [0.0] * _ROPE_HALF + [1.0] * ROPE_AXIS_DIM
                  + [0.0] * _ROPE_HALF + [0.0] * (MLA_HP - len(_MLA_SRC_DIM)))


def _mla_up_kernel(dn_ref, qn_ref, kvn_ref, wq_ref, wk_ref, we_ref, wv_ref, real_ref, cq_ref, sq_ref, ck_ref, sk_ref,
                   q_ref, k_ref, v_ref):
    dn = dn_ref[...]
    cq = _rms(dn[:, :MLA_Q_RANK]) * qn_ref[...]
    ckv = _rms(dn[:, MLA_Q_RANK:MLA_Q_RANK + MLA_KV_RANK]) * kvn_ref[...]
    kr = dn[:, MLA_Q_RANK + MLA_KV_RANK:]
    kr_hi = kr.astype(BF16)
    kr_lo = (kr - kr_hi.astype(F32)).astype(BF16)
    ckvb = ckv.astype(BF16)
    q_pre = _dot(cq.astype(BF16), wq_ref[...])
    k_pre = _dot(ckvb, wk_ref[...]) + (_dot(kr_hi, we_ref[...]) + _dot(kr_lo, we_ref[...]))
    v_ref[...] = _dot(ckvb, wv_ref[...]).astype(v_ref.dtype)
    real = real_ref[...]

    def head(x, cos_g, sin_g):
        ms = jnp.sum(x * x * real, axis=-1, keepdims=True) * (1.0 / MLA_QK)
        xr = x * lax.rsqrt(ms + RMS_EPS)
        return xr * cos_g + pltpu.roll(xr, MLA_HP - _ROPE_HALF, 1) * sin_g

    cos_q, sin_q, cos_k, sin_k = cq_ref[...], sq_ref[...], ck_ref[...], sk_ref[...]
    for h in range(MLA_HEADS):
        sl = slice(h * MLA_HP, (h + 1) * MLA_HP)
        q_ref[:, sl] = head(q_pre[:, sl], cos_q, sin_q).astype(q_ref.dtype)
        k_ref[:, sl] = head(k_pre[:, sl], cos_k, sin_k).astype(k_ref.dtype)


def _mla_up(st, rows, down, q_norm, kv_norm, wq, wk, we, wv, tables):
    hw = MLA_HEADS * MLA_HP
    const = lambda i: (0, 0)
    tpb, ct, lt = st.tpb, st.ct, st.lt
    rope_idx = lambda i: (jnp.where(i % tpb < ct, lt, i % tpb - ct), 0)
    real = jnp.asarray(_MLA_REAL_LANE, F32).reshape(1, MLA_HP)
    return pl.pallas_call(
        _mla_up_kernel,
        grid=(rows.n,),
        in_specs=[pl.BlockSpec((TM, 512), lambda i: (i, 0)),
                  pl.BlockSpec((1, MLA_Q_RANK), const),
                  pl.BlockSpec((1, MLA_KV_RANK), const),
                  pl.BlockSpec((MLA_Q_RANK, hw), const),
                  pl.BlockSpec((MLA_KV_RANK, hw), const),
                  pl.BlockSpec((128, hw), const),
                  pl.BlockSpec((MLA_KV_RANK, MLA_HEADS * MLA_V), const),
                  pl.BlockSpec((1, MLA_HP), const)] + [pl.BlockSpec((TM, MLA_HP), rope_idx)] * 4,
        out_specs=[pl.BlockSpec((TM, hw), lambda i: (i, 0)),
                   pl.BlockSpec((TM, hw), lambda i: (i, 0)),
                   pl.BlockSpec((TM, MLA_HEADS * MLA_V), lambda i: (i, 0))],
        out_shape=[jax.ShapeDtypeStruct((st.NT, hw), BF16),
                   jax.ShapeDtypeStruct((st.NT, hw), BF16),
                   jax.ShapeDtypeStruct((st.NT, MLA_HEADS * MLA_V), BF16)],
        compiler_params=_cparams(("arbitrary",)),
        name="mla_up_proj",
    )(down, q_norm, kv_norm, wq, wk, we, wv, real, *tables)


def _attn_kernel(q_ref, k_ref, v_ref, o_ref, vaug_ref, *, C, ct):
    qi = pl.program_id(2)

    @pl.when(qi == 0)
    def _():
        lane = lax.broadcasted_iota(jnp.int32, (k_ref.shape[0], 2 * MLA_V), 1)
        for hh in range(ATT_HEADS):
            pair = v_ref[:, (hh // 2) * 2 * MLA_V:(hh // 2 + 1) * 2 * MLA_V].astype(F32)
            if hh % 2 == 0:
                aug = jnp.where(lane < MLA_V, pair, jnp.where(lane == MLA_V, 1.0, 0.0))
            else:
                aug = jnp.where(lane >= MLA_V, pair, jnp.where(lane == 0, 1.0, 0.0))
            vaug_ref[hh] = aug.astype(BF16)

    def attend(nkeys):
        lane = lax.broadcasted_iota(jnp.int32, (TM, 2 * MLA_V), 1)

        def scores(hh):
            sl = slice(hh * MLA_HP, (hh + 1) * MLA_HP)
            return _dot_t(q_ref[:, sl], k_ref[0:nkeys, sl])

        s_next = scores(0)
        outs = []
        for hh in range(ATT_HEADS):
            s = s_next
            if hh + 1 < ATT_HEADS:
                s_next = scores(hh + 1)
            p = jnp.exp2(s - jnp.max(s, axis=-1, keepdims=True)).astype(BF16)
            o = _dot(p, vaug_ref[hh, 0:nkeys, :])
            rowsum = o[:, MLA_V:MLA_V + 1] if hh % 2 == 0 else o[:, 0:1]
            outs.append(o * (1.0 / rowsum))
            if hh % 2 == 1:
                pair = hh // 2
                o_ref[:, pair * 2 * MLA_V:(pair + 1) * 2 * MLA_V] = jnp.where(
                    lane < MLA_V, outs[hh - 1], outs[hh]).astype(o_ref.dtype)

    pl.when(qi < ct)(lambda: attend(C))
    pl.when(qi >= ct)(lambda: attend(k_ref.shape[0]))


def _attention(st, q, k, v):
    B, ltot, tpb = st.B, st.ltot, st.tpb
    hg = MLA_HEADS // ATT_HEADS
    return pl.pallas_call(
        functools.partial(_attn_kernel, C=st.C, ct=st.ct),
        grid=(B, hg, tpb),
        in_specs=[pl.BlockSpec((TM, ATT_HEADS * MLA_HP), lambda b, h, i: (b * tpb + i, h)),
                  pl.BlockSpec((ltot, ATT_HEADS * MLA_HP), lambda b, h, i: (b, h)),
                  pl.BlockSpec((ltot, ATT_HEADS * MLA_V), lambda b, h, i: (b, h))],
        out_specs=pl.BlockSpec((TM, ATT_HEADS * MLA_V), lambda b, h, i: (b * tpb + i, h)),
        out_shape=jax.ShapeDtypeStruct((st.NT, MLA_HEADS * MLA_V), BF16),
        scratch_shapes=[pltpu.VMEM((ATT_HEADS, ltot, 2 * MLA_V), BF16)],
        compiler_params=_cparams(("arbitrary", "arbitrary", "arbitrary")),
        name="mla_attention",
    )(q, k, v)


def _mla_weights(w_uq, w_ukv):
    H = MLA_HEADS
    src = jnp.asarray(_MLA_SRC_DIM, jnp.int32)
    pad = MLA_HP - len(_MLA_SRC_DIM)
    wq = jnp.pad(w_uq.reshape(MLA_Q_RANK, H, MLA_QK)[:, :, src], ((0, 0), (0, 0), (0, pad)))
    wkv = w_ukv.reshape(MLA_KV_RANK, H, MLA_NOPE + MLA_V)
    wk = jnp.pad(wkv[:, :, :MLA_NOPE], ((0, 0), (0, 0), (0, MLA_HP - MLA_NOPE)))
    wv = wkv[:, :, MLA_NOPE:]
    r = jnp.arange(128)[:, None]
    lane_dim = jnp.pad(src, (0, pad), constant_values=-1)[None, :]
    place = (lane_dim == r + MLA_NOPE).astype(BF16)
    we = jnp.tile(place, (1, H))
    return (wq.reshape(MLA_Q_RANK, H * MLA_HP).astype(BF16), wk.reshape(MLA_KV_RANK, H * MLA_HP).astype(BF16),
            we, wv.reshape(MLA_KV_RANK, H * MLA_V).astype(BF16))


def _rope_tables(L, qk_norm):
    rows = L // GRID_W
    row = jnp.broadcast_to(jnp.arange(rows, dtype=F32)[:, None], (rows, GRID_W)).reshape(L)
    col = jnp.broadcast_to(jnp.arange(GRID_W, dtype=F32)[None, :], (rows, GRID_W)).reshape(L)
    inv_freq = ROPE_BASE ** (-jnp.arange(0, ROPE_AXIS_DIM, 2, dtype=F32) / ROPE_AXIS_DIM)
    ar = row[:, None] * inv_freq
    ac = col[:, None] * inv_freq
    h8 = _ROPE_HALF
    one = jnp.ones((L, MLA_NOPE), F32)
    z8 = jnp.zeros((L, h8), F32)
    zpad = jnp.zeros((L, MLA_HP - len(_MLA_SRC_DIM)), F32)
    cos_t = jnp.concatenate([one, jnp.cos(ar), jnp.cos(ar), z8, jnp.cos(ac), jnp.cos(ac), z8, zpad], axis=1)
    sin_t = jnp.concatenate([0 * one, -jnp.sin(ar), jnp.sin(ar), z8, -jnp.sin(ac), jnp.sin(ac), z8, zpad], axis=1)
    ident = jnp.asarray(_MLA_REAL_LANE, F32)[None, :]
    cos_t = jnp.concatenate([cos_t, jnp.broadcast_to(ident, (TM, MLA_HP))], axis=0)
    sin_t = jnp.concatenate([sin_t, jnp.zeros((TM, MLA_HP), F32)], axis=0)
    src = jnp.asarray(_MLA_SRC_DIM, jnp.int32)
    pad = MLA_HP - len(_MLA_SRC_DIM)
    scale = MLA_QK ** -0.5 * math.log2(math.e)
    tables = []
    for g, s in ((qk_norm[0], scale), (qk_norm[1], 1.0)):
        g_lane = jnp.pad(g[src], (0, pad))
        g_partner = jnp.roll(g_lane, -h8)
        tables += [cos_t * (g_lane * s)[None, :], sin_t * (g_partner * s)[None, :]]
    return tables


def _log_sigmoid(x):
    return jnp.minimum(x, 0.0) - jnp.log1p(jnp.exp(-jnp.abs(x)))


def _mlstm_kernel(qf_ref, kf_ref, vf_ref, gf_ref, qb_ref, kb_ref, vb_ref, gb_ref, bias_ref, tril_ref, triu_ref,
                  of_ref, ob_ref, c_ref, n_ref, m_ref):
    T = ML_TC

    @pl.when(pl.program_id(1) == 0)
    def _():
        c_ref[...] = jnp.zeros_like(c_ref)
        n_ref[...] = jnp.zeros_like(n_ref)
        m_ref[...] = jnp.full(m_ref.shape, ML_M_INIT, F32)

    ti = lax.broadcasted_iota(jnp.int32, (T, T), 0)
    si = lax.broadcasted_iota(jnp.int32, (T, T), 1)
    dirs = ((qf_ref, kf_ref, vf_ref, gf_ref, of_ref, tril_ref), (qb_ref, kb_ref, vb_ref, gb_ref, ob_ref, triu_ref))
    for d, (q_ref, k_ref, v_ref, g_ref, o_ref, tri_ref) in enumerate(dirs):
        tri = (si <= ti) if d == 0 else (si >= ti)
        g = g_ref[...] + bias_ref[...]
        g_t = g.T
        lsg = _log_sigmoid(g)
        lsg_hi = lsg.astype(BF16)
        lsg_lo = (lsg - lsg_hi.astype(F32)).astype(BF16)
        cum = _dot(tri_ref[...], lsg_hi) + _dot(tri_ref[...], lsg_lo)
        cum_t = cum.T
        last = T - 1 if d == 0 else 0
        for h in range(ML_HEADS):
            st = d * ML_HEADS + h
            li, lf_ = (2 * d) * ML_HEADS + h, (2 * d + 1) * ML_HEADS + h
            ig_col = g[:, li:li + 1]
            ig_row = g_t[li:li + 1, :]
            b_col = cum[:, lf_:lf_ + 1]
            b_row = cum_t[lf_:lf_ + 1, :]
            total = cum[last:last + 1, lf_:lf_ + 1]
            m_old = m_ref[st, 0:1, 0:1]
            d_log = jnp.where(tri, b_col - b_row + ig_row, -jnp.inf)
            inter_log = b_col + m_old
            m_t = jnp.maximum(inter_log, jnp.max(d_log, axis=1, keepdims=True))
            qh = q_ref[:, h * ML_DQK:(h + 1) * ML_DQK] * (ML_DQK ** -0.5)
            kh = k_ref[:, h * ML_DQK:(h + 1) * ML_DQK]
            vh = v_ref[:, h * ML_DV:(h + 1) * ML_DV].astype(BF16)
            qb16 = qh.astype(BF16)
            s_mat = _dot_t(qb16, kh.astype(BF16)) * jnp.exp(d_log - m_t)
            inter = jnp.exp(inter_log - m_t)
            c_old = c_ref[st]
            n_old = n_ref[st, 0:1, :]
            num = _dot(s_mat.astype(BF16), vh) + inter * _dot(qb16, c_old.astype(BF16))
            den = jnp.sum(s_mat, axis=1, keepdims=True) + inter * jnp.sum(qh * n_old, axis=1, keepdims=True)
            o_ref[:, h * ML_DV:(h + 1) * ML_DV] = num / jnp.maximum(jnp.abs(den), jnp.exp(-m_t))
            w_log = total - b_col + ig_col
            m_new = jnp.maximum(total + m_old, jnp.max(w_log, axis=0, keepdims=True))
            w = jnp.exp(w_log - m_new)
            decay = jnp.exp(total + m_old - m_new)
            kw = kh * w
            c_ref[st] = decay * c_old + _dot(kw.T.astype(BF16), vh)
            n_ref[st, 0:1, :] = decay * n_old + jnp.sum(kw, axis=0, keepdims=True)
            m_ref[st] = jnp.broadcast_to(m_new, m_ref.shape[1:])


def _mlstm(st, proj, gate_bias):
    B = st.B
    assert st.L % ML_TC == 0 and st.C % ML_TC == 0
    cc = st.C // ML_TC
    nch = st.ltot // ML_TC
    qw = ML_HEADS * ML_DQK
    vw = ML_HEADS * ML_DV
    gcol = (2 * qw + 2 * vw) // 128

    def rb(d, b, k):
        chunk = k if d == 0 else jnp.where(k < cc, cc - 1 - k, nch - 1 - (k - cc))
        return b * nch + chunk

    def specs(d):
        return [pl.BlockSpec((ML_TC, qw), lambda b, k: (rb(d, b, k), 0)),
                pl.BlockSpec((ML_TC, qw), lambda b, k: (rb(d, b, k), 1)),
                pl.BlockSpec((ML_TC, vw), lambda b, k: (rb(d, b, k), (2 * qw) // vw)),
                pl.BlockSpec((ML_TC, 128), lambda b, k: (rb(d, b, k), gcol))]

    nst = 2 * ML_HEADS
    out = jax.ShapeDtypeStruct((st.NT, vw), F32)
    tril = jnp.tril(jnp.ones((ML_TC, ML_TC), BF16))
    return pl.pallas_call(
        _mlstm_kernel,
        grid=(B, nch),
        in_specs=specs(0) + specs(1) + [pl.BlockSpec((1, 128), lambda b, k: (0, 0)),
                                        pl.BlockSpec((ML_TC, ML_TC), lambda b, k: (0, 0)),
                                        pl.BlockSpec((ML_TC, ML_TC), lambda b, k: (0, 0))],
        out_specs=[pl.BlockSpec((ML_TC, vw), lambda b, k: (rb(0, b, k), 0)),
                   pl.BlockSpec((ML_TC, vw), lambda b, k: (rb(1, b, k), 0))],
        out_shape=[out, out],
        scratch_shapes=[pltpu.VMEM((nst, ML_DQK, ML_DV), F32),
                        pltpu.VMEM((nst, 8, ML_DQK), F32),
                        pltpu.VMEM((nst, 8, 128), F32)],
        compiler_params=_cparams(("arbitrary", "arbitrary")),
        name="mlstm_chunks",
    )(proj, proj, proj, proj, proj, proj, proj, proj, gate_bias, tril, tril.T)


def _route(lg):
    lane_i = lax.broadcasted_iota(jnp.int32, lg.shape, 1)
    lane = lane_i.astype(F32)
    neg = -jnp.inf
    gl = jnp.where(lane_i < MOE_GROUPS, lg, neg)
    gmax = jnp.max(gl, axis=-1, keepdims=True)
    gsum = jnp.sum(jnp.where(lane_i < MOE_GROUPS, jnp.exp(lg - gmax), 0.0), axis=-1, keepdims=True)
    p_top = 1.0 / gsum
    g_sel = jnp.min(jnp.where(gl == gmax, lane, 128.0), axis=-1, keepdims=True)
    group_of_lane = (lane_i >> 3).astype(F32) - 1.0
    el = jnp.where(group_of_lane == g_sel, lg, neg)
    e1 = jnp.max(el, axis=-1, keepdims=True)
    i1 = jnp.min(jnp.where(el == e1, lane, 128.0), axis=-1, keepdims=True)
    el2 = jnp.where(lane == i1, neg, el)
    e2 = jnp.max(el2, axis=-1, keepdims=True)
    i2 = jnp.min(jnp.where(el2 == e2, lane, 128.0), axis=-1, keepdims=True)
    t = jnp.exp(e2 - e1)
    w1 = p_top / (1.0 + t)
    w2 = w1 * t
    id1 = i1 - MOE_GROUPS
    id2 = i2 - MOE_GROUPS
    table = jnp.where(lane_i == 0, id1, jnp.where(lane_i == 1, id2,
                                                  jnp.where(lane_i == 2, w1, jnp.where(lane_i == 3, w2, 0.0))))
    chosen = jnp.where(lane == i1, 1.0, 0.0) + jnp.where(lane == i2, 1.0, 0.0)
    return table, jnp.sum(chosen, axis=0, keepdims=True)


def _accumulate_counts(cnt_ref, hist):
    @pl.when(pl.program_id(0) == 0)
    def _():
        cnt_ref[...] = jnp.zeros_like(cnt_ref)
    cnt_ref[...] += hist


def _lhs_mla(o_ref):
    return o_ref[...]


def _lhs_mlstm(hf_ref, hb_ref, og_ref, onorm_ref):
    hs = hf_ref[...] + hb_ref[...]
    og = _sigmoid(og_ref[...])
    parts = []
    for h in range(ML_HEADS):
        sl = slice(h * ML_DV, (h + 1) * ML_DV)
        parts.append(_rms(hs[:, sl]) * onorm_ref[:, sl] * og[:, sl])
    return jnp.concatenate(parts, axis=1)


def _mixer_out_kernel(*refs, n_lhs, lhs_fn):
    lhs_refs = refs[:n_lhs]
    w_ref, x_ref, mod_ref, gain_ref, wr_ref, br_ref, xo_ref, f_ref, r_ref, cnt_ref = refs[n_lhs:]
    y = _dot(lhs_fn(*lhs_refs).astype(BF16), w_ref[...])
    mod = mod_ref[0]
    xn = x_ref[...] + mod[2:3, :] * y
    xo_ref[...] = xn
    f = _norm_mod(xn, gain_ref[...], mod[3:4, :], mod[4:5, :])
    f_ref[...] = f
    r_ref[...], hist = _route(_dot3_split(f, wr_ref[...]) + br_ref[...])
    _accumulate_counts(cnt_ref, hist)


def _mixer_out(rows, lhs_fn, lhs_args, lhs_specs, w_out, x, mod, gain, wr, br, name):
    n = rows.n * TM
    const = lambda k: (0, 0)
    out = lambda w: pl.BlockSpec((TM, w), lambda k: (k, 0))
    return pl.pallas_call(
        functools.partial(_mixer_out_kernel, n_lhs=len(lhs_args), lhs_fn=lhs_fn),
        grid=(rows.n,),
        in_specs=list(lhs_specs) + [
            pl.BlockSpec((D_MODEL, D_MODEL), const),
            pl.BlockSpec((TM, D_MODEL), lambda k: (rows.src(k), 0)),
            pl.BlockSpec((1, 8, D_MODEL), lambda k: (rows.mod(k), 0, 0)),
            pl.BlockSpec((1, D_MODEL), const),
            pl.BlockSpec((D_MODEL, 256), const),
            pl.BlockSpec((1, 128), const)],
        out_specs=[out(D_MODEL), out(D_MODEL), out(128), pl.BlockSpec((1, 128), const)],
        out_shape=[jax.ShapeDtypeStruct((n, D_MODEL), F32),
                   jax.ShapeDtypeStruct((n, D_MODEL), F32),
                   jax.ShapeDtypeStruct((n, 128), F32),
                   jax.ShapeDtypeStruct((1, 128), F32)],
        compiler_params=_cparams(("arbitrary",)),
        name=name,
    )(*lhs_args, w_out, x, mod, gain, wr, br)


RANK_TILES = 4


def _rank_kernel(r_ref, start_ref, tril_ref, pos_ref, carry_ref):
    @pl.when(pl.program_id(0) == 0)
    def _():
        carry_ref[...] = jnp.zeros_like(carry_ref)

    r = r_ref[...]
    lane = lax.broadcasted_iota(jnp.int32, r.shape, 1).astype(F32)
    chosen = [jnp.where(lane == r[:, k:k + 1] + MOE_GROUPS, 1.0, 0.0) for k in range(MOE_TOPK)]
    both = functools.reduce(jnp.add, chosen)
    before = _dot(tril_ref[...], both.astype(BF16)) + (start_ref[...] + carry_ref[...])
    lane_i = lax.broadcasted_iota(jnp.int32, r.shape, 1)
    pos = jnp.zeros(r.shape, F32)
    for k in range(MOE_TOPK):
        pos = jnp.where(lane_i == k, jnp.sum(chosen[k] * before, axis=-1, keepdims=True), pos)
    pos_t = pos.T[0:8, :].astype(jnp.int32)
    for j in range(RANK_TILES):
        pos_ref[j] = pos_t[:, j * TM:(j + 1) * TM]
    carry_ref[...] += jnp.sum(both, axis=0, keepdims=True)


def _assignment_slots(ntiles, route, starts):
    assert ntiles % RANK_TILES == 0
    rows = RANK_TILES * TM
    tril = jnp.tril(jnp.ones((rows, rows), BF16), -1)
    return pl.pallas_call(
        _rank_kernel,
        grid=(ntiles // RANK_TILES,),
        in_specs=[pl.BlockSpec((rows, 128), lambda i: (i, 0)),
                  pl.BlockSpec((1, 128), lambda i: (0, 0)),
                  pl.BlockSpec((rows, rows), lambda i: (0, 0))],
        out_specs=pl.BlockSpec((RANK_TILES, 8, TM), lambda i: (i, 0, 0)),
        out_shape=jax.ShapeDtypeStruct((ntiles, 8, TM), jnp.int32),
        scratch_shapes=[pltpu.VMEM((1, 128), F32)],
        compiler_params=_cparams(("arbitrary",)),
        name="moe_rank",
    )(route, starts, tril)


def _row_wait(hbm, buf, sem):
    pltpu.make_async_copy(hbm.at[pl.ds(0, TM), :], buf, sem).wait()


def _dispatch_kernel(pos_ref, f_ref, xs_hbm, buf, sem, *, ntiles):
    i = pl.program_id(0)

    def step(s):
        @pl.when(i >= 2)
        def _():
            for _ in range(MOE_TOPK):
                _row_wait(xs_hbm, buf.at[s], sem.at[s])
        buf[s] = f_ref[...]
        for r in range(TM):
            for k in range(MOE_TOPK):
                pltpu.make_async_copy(buf.at[s, pl.ds(r, 1), :], xs_hbm.at[pl.ds(pos_ref[0, k, r], 1), :],
                                      sem.at[s]).start(priority=k % 2)

        @pl.when(i == ntiles - 1)
        def _():
            for slot in ((1 - s, s) if ntiles >= 2 else (s,)):
                for _ in range(MOE_TOPK):
                    _row_wait(xs_hbm, buf.at[slot], sem.at[slot])

    for s in range(2):
        pl.when(i % 2 == s)(functools.partial(step, s))


def _dispatch(ntiles, f, pos):
    n = ntiles * TM
    return pl.pallas_call(
        functools.partial(_dispatch_kernel, ntiles=ntiles),
        grid=(ntiles,),
        in_specs=[pl.BlockSpec((1, 8, TM), lambda i: (i, 0, 0), memory_space=pltpu.SMEM),
                  pl.BlockSpec((TM, D_MODEL), lambda i: (i, 0))],
        out_specs=pl.BlockSpec(memory_space=pl.ANY),
        out_shape=jax.ShapeDtypeStruct((MOE_TOPK * n, D_MODEL), F32),
        scratch_shapes=[pltpu.VMEM((2, TM, D_MODEL), F32), pltpu.SemaphoreType.DMA((2,))],
        compiler_params=_cparams(("arbitrary",)),
        name="moe_dispatch",
    )(pos, f)


def _expert_kernel(vb_ref, ve_ref, lo_ref, hi_ref, first_ref, newexp_ref, eslot_ref, enext_ref, x_ref, wgu_hbm,
                   wd_hbm, y_ref, wgu_f, wd_f, wgu_b, wd_b, sem, *, layer):
    v = pl.program_id(0)

    def fetch(e, s):
        return (pltpu.make_async_copy(wgu_hbm.at[layer, e], wgu_f.at[s], sem.at[0, s]),
                pltpu.make_async_copy(wd_hbm.at[layer, e], wd_f.at[s], sem.at[1, s]))

    @pl.when(newexp_ref[v] == 1)
    def _():
        for s in range(2):
            @pl.when(eslot_ref[v] == s)
            def _():
                @pl.when(v == 0)
                def _():
                    for c in fetch(ve_ref[v], s):
                        c.start()

                @pl.when(enext_ref[v] >= 0)
                def _():
                    for c in fetch(enext_ref[v], 1 - s):
                        c.start()
                for c in fetch(ve_ref[v], s):
                    c.wait()
                wgu_b[...] = wgu_f[s].astype(BF16)
                wd_b[...] = wd_f[s].astype(BF16)

    @pl.when(hi_ref[v] > lo_ref[v])
    def _():
        gu = _dot(x_ref[...].astype(BF16), wgu_b[...])
        gate = gu[:, :MOE_FF]
        act = gate * _sigmoid(gate) * gu[:, MOE_FF:]
        y = _dot(act.astype(BF16), wd_b[...])
        r = lax.broadcasted_iota(jnp.int32, (MOE_BM, 1), 0)
        mine = jnp.logical_and(r >= lo_ref[v], r < hi_ref[v])
        y = jnp.where(mine, y, 0.0)

        @pl.when(first_ref[v] == 1)
        def _():
            y_ref[...] = y

        @pl.when(first_ref[v] == 0)
        def _():
            y_ref[...] += y


def _expert_ffn(xs, visits, layer, w_gate_up, w_down):
    nvis = visits[0].shape[0]
    blk_idx = lambda v, vb, *_: (vb[v], 0)
    grid_spec = pltpu.PrefetchScalarGridSpec(
        num_scalar_prefetch=8,
        grid=(nvis,),
        in_specs=[pl.BlockSpec((MOE_BM, D_MODEL), blk_idx),
                  pl.BlockSpec(memory_space=pl.ANY),
                  pl.BlockSpec(memory_space=pl.ANY)],
        out_specs=pl.BlockSpec((MOE_BM, D_MODEL), blk_idx),
        scratch_shapes=[pltpu.VMEM((2, D_MODEL, 2 * MOE_FF), F32), pltpu.VMEM((2, MOE_FF, D_MODEL), F32),
                        pltpu.VMEM((D_MODEL, 2 * MOE_FF), BF16), pltpu.VMEM((MOE_FF, D_MODEL), BF16),
                        pltpu.SemaphoreType.DMA((2, 2))],
    )
    return pl.pallas_call(
        functools.partial(_expert_kernel, layer=layer),
        grid_spec=grid_spec,
        out_shape=jax.ShapeDtypeStruct(xs.shape, F32),
        compiler_params=_cparams(("arbitrary",)),
        name="moe_expert_ffn",
    )(*visits, xs, w_gate_up, w_down)


def _combine_kernel(pos_ref, nxt_ref, x_ref, r_ref, mod_ref, ys_hbm, o_ref, ybuf, sem, *, ntiles):
    i = pl.program_id(0)

    def gather(table, s):
        for r in range(TM):
            for k in range(MOE_TOPK):
                pltpu.make_async_copy(ys_hbm.at[pl.ds(table[0, k, r], 1), :], ybuf.at[s, k, pl.ds(r, 1), :],
                                      sem.at[s]).start(priority=k % 2)

    @pl.when(i == 0)
    def _():
        gather(pos_ref, 0)

    def step(s):
        @pl.when(i + 1 < ntiles)
        def _():
            gather(nxt_ref, 1 - s)
        for k in range(MOE_TOPK):
            _row_wait(ys_hbm, ybuf.at[s, k], sem.at[s])
        w = r_ref[...]
        y = w[:, MOE_TOPK:MOE_TOPK + 1] * ybuf[s, 0]
        for k in range(1, MOE_TOPK):
            y = y + w[:, MOE_TOPK + k:MOE_TOPK + k + 1] * ybuf[s, k]
        o_ref[...] = x_ref[...] + mod_ref[0][5:6, :] * y

    for s in range(2):
        pl.when(i % 2 == s)(functools.partial(step, s))


def _combine(rows, x, ys, pos, route, mod):
    n = rows.n
    spec = pl.BlockSpec((TM, D_MODEL), lambda i: (i, 0))
    return pl.pallas_call(
        functools.partial(_combine_kernel, ntiles=n),
        grid=(n,),
        in_specs=[pl.BlockSpec((1, 8, TM), lambda i: (i, 0, 0), memory_space=pltpu.SMEM),
                  pl.BlockSpec((1, 8, TM), lambda i: (jnp.minimum(i + 1, n - 1), 0, 0), memory_space=pltpu.SMEM),
                  spec,
                  pl.BlockSpec((TM, 128), lambda i: (i, 0)),
                  pl.BlockSpec((1, 8, D_MODEL), lambda i: (rows.mod(i), 0, 0)),
                  pl.BlockSpec(memory_space=pl.ANY)],
        out_specs=spec,
        out_shape=jax.ShapeDtypeStruct((n * TM, D_MODEL), F32),
        scratch_shapes=[pltpu.VMEM((2, MOE_TOPK, TM, D_MODEL), F32), pltpu.SemaphoreType.DMA((2,))],
        compiler_params=_cparams(("arbitrary",)),
        name="moe_combine",
    )(pos, pos, x, route, mod, ys)


def _visit_tables(bounds, nk):
    E = MOE_EXPERTS
    nblk = nk // MOE_BM
    nvis = nblk + E
    starts, ends = bounds[:-1], bounds[1:]
    fb = starts // MOE_BM
    nv = jnp.where(ends > starts, (ends - 1) // MOE_BM - fb + 1, 0)
    cum = jnp.cumsum(nv)
    total = cum[-1]
    v = jnp.arange(nvis, dtype=jnp.int32)
    active = v < total
    vc = jnp.minimum(v, total - 1)
    ve = jnp.minimum(jnp.sum((cum[None, :] <= vc[:, None]).astype(jnp.int32), axis=1), E - 1)
    vb = fb[ve] + (vc - (cum - nv)[ve])
    lo = jnp.where(active, jnp.maximum(starts[ve], vb * MOE_BM) - vb * MOE_BM, 0)
    hi = jnp.where(active, jnp.minimum(ends[ve], (vb + 1) * MOE_BM) - vb * MOE_BM, 0)
    prev_b = jnp.concatenate([jnp.full((1,), -1, jnp.int32), vb[:-1]])
    first = jnp.logical_and(active, vb != prev_b)
    prev_e = jnp.concatenate([jnp.full((1,), -1, jnp.int32), ve[:-1]])
    new_expert = jnp.logical_and(active, ve != prev_e)
    eslot = (jnp.cumsum(new_expert.astype(jnp.int32)) - 1) % 2
    at = jnp.where(new_expert, v, nvis)
    nxt = jnp.concatenate([lax.cummin(at[::-1])[::-1][1:], jnp.full((1,), nvis, jnp.int32)])
    enext = jnp.where(nxt < nvis, ve[jnp.minimum(nxt, nvis - 1)], -1)
    i32 = lambda a: a.astype(jnp.int32)
    return i32(vb), i32(ve), i32(lo), i32(hi), i32(first), i32(new_expert), i32(eslot), i32(enext)


def _moe(rows, x, f, route, counts, mod, layer, w_gate_up, w_down):
    n = rows.n * TM
    cum = jnp.cumsum(counts[0])
    starts = (cum - counts[0]).reshape(1, 128)
    bounds = jnp.concatenate([starts[0, MOE_GROUPS:MOE_GROUPS + MOE_EXPERTS], cum[-1:]]).astype(jnp.int32)
    pos = _assignment_slots(rows.n, route, starts)
    xs = _dispatch(rows.n, f, pos)
    ys = _expert_ffn(xs, _visit_tables(bounds, n * MOE_TOPK), layer, w_gate_up, w_down)
    return _combine(rows, x, ys, pos, route, mod)


def kernel(x, c, ctx, c_ctx, ada_w, ada_b, norm_mix, norm_ffn, rg_w_in, rg_conv_w, rg_conv_b, rg_gate_w, rg_gate_b, rg_lambda, rg_w_out, mla_w_down, mla_q_norm, mla_kv_norm, mla_w_uq, mla_w_ukv, mla_qk_norm, mla_w_o, ml_w_in, ml_gate_b, ml_out_norm, ml_w_out, moe_w_group, moe_b_group, moe_w_expert, moe_b_expert, moe_w_gate_up, moe_w_down):
    B, L, D = x.shape
    C = ctx.shape[1]
    depth = ada_w.shape[0]
    assert D == D_MODEL
    st = _Stream(B, L, C)

    xs = jnp.concatenate([ctx, x], axis=1).reshape(st.NT, D)
    cc = jnp.zeros((16, D), F32).at[:B].set(c).at[B].set(c_ctx)
    mod_all = _modulation(cc, ada_w, ada_b)
    mod_all = jnp.pad(mod_all[:, :B + 1].reshape(depth, B + 1, 6, D), ((0, 0), (0, 0), (0, 2), (0, 0)))
    perm = _row_permutation(B)

    row = lambda a: a.reshape(1, -1)

    for i in range(depth):
        last = i == depth - 1
        mod = mod_all[i]
        kind, j = i % 3, i // 3
        all_rows = st.all_rows()
        out_rows = st.latent_rows() if last else all_rows
        tile_spec = lambda w: pl.BlockSpec((TM, w), lambda k: (out_rows.src(k), 0))
        wr = jnp.zeros((D, 128), F32).at[:, :MOE_GROUPS].set(moe_w_group[i]) \
            .at[:, MOE_GROUPS:MOE_GROUPS + MOE_EXPERTS].set(moe_w_expert[i])
        wr_hi = wr.astype(BF16)
        wr = jnp.concatenate([wr_hi, (wr - wr_hi.astype(F32)).astype(BF16)], axis=1)
        br = jnp.zeros((1, 128), F32).at[0, :MOE_GROUPS].set(moe_b_group[i]) \
            .at[0, MOE_GROUPS:MOE_GROUPS + MOE_EXPERTS].set(moe_b_expert[i])
        out_args = (xs, mod, row(norm_ffn[i]), wr, br)

        if kind == 0:
            gate, u = _rg_in(st, xs, mod, row(norm_mix[i]), perm, rg_w_in[j].astype(BF16))
            wg, gb = _rg_gate_weights(rg_gate_w[j], rg_gate_b[j])
            hs = _rg_scan(st, u, rg_conv_w[j], row(rg_conv_b[j]), wg, gb, rg_lambda[j].reshape(2, 1, D))
            xs, f, route, counts = _rg_out(st, last, gate, hs, perm.T, rg_w_out[j].astype(BF16), *out_args)
        elif kind == 1:
            w_down = jnp.pad(mla_w_down[j], ((0, 0), (0, 512 - mla_w_down.shape[2]))).astype(BF16)
            down = _norm_proj(all_rows, xs, mod, row(norm_mix[i]), w_down, name="mla_down_proj")
            wq, wk, we, wv = _mla_weights(mla_w_uq[j], mla_w_ukv[j])
            q, k, v = _mla_up(st, all_rows, down, row(mla_q_norm[j]), row(mla_kv_norm[j]), wq, wk, we, wv,
                              _rope_tables(L, mla_qk_norm[j]))
            o = _attention(st, q, k, v)
            xs, f, route, counts = _mixer_out(out_rows, _lhs_mla, (o,), (tile_spec(D),),
                                      mla_w_o[j].astype(BF16), *out_args, name="mla_out")
        else:
            n_in = ml_w_in.shape[2]
            w_in = jnp.pad(ml_w_in[j], ((0, 0), (0, ML_NP - n_in))).astype(BF16)
            proj = _norm_proj(all_rows, xs, mod, row(norm_mix[i]), w_in, name="mlstm_in_proj")
            gate_bias = jnp.pad(ml_gate_b[j].reshape(1, -1), ((0, 0), (0, 128 - 4 * ML_HEADS)))
            hf, hb = _mlstm(st, proj, gate_bias)
            og_spec = pl.BlockSpec((TM, D), lambda k: (out_rows.src(k), 2))
            xs, f, route, counts = _mixer_out(out_rows, _lhs_mlstm, (hf, hb, proj, row(ml_out_norm[j])),
                                      (tile_spec(D), tile_spec(D), og_spec, pl.BlockSpec((1, D), lambda k: (0, 0))),
                                      ml_w_out[j].astype(BF16), *out_args, name="mlstm_out")

        moe_rows = st.dense_latent_rows() if last else all_rows
        xs = _moe(moe_rows, xs, f, route, counts, mod, i, moe_w_gate_up, moe_w_down)

    return xs.reshape(B, L, D)
```

```python
import functools
import math

import jax
import jax.numpy as jnp
from jax import lax
from jax.experimental import pallas as pl
from jax.experimental.pallas import tpu as pltpu

F32 = jnp.float32
BF16 = jnp.bfloat16

D_MODEL = 1024
RMS_EPS = 1e-6

TM = 256
VMEM_LIMIT = 48 * 1024 * 1024

RG_BLOCK_W = 64
RG_CHUNK = 256
RG_CONV_W = 4
RG_C = 8.0
RG_TT = 64

MLA_HEADS = 16
MLA_Q_RANK = 256
MLA_KV_RANK = 128
MLA_NOPE = 64
MLA_ROPE = 32
MLA_V = 64
MLA_QK = MLA_NOPE + MLA_ROPE
MLA_HP = 128
ROPE_AXIS_DIM = MLA_ROPE // 2
ROPE_BASE = 10000.0
GRID_W = 64
ATT_HEADS = 8

ML_HEADS = 4
ML_DV = 256
ML_DQK = 128
ML_TC = 256
ML_M_INIT = -1e30
ML_NP = 3200

MOE_GROUPS = 8
MOE_PER_GROUP = 8
MOE_EXPERTS = 64
MOE_TOPK = 2
MOE_FF = 256
MOE_BM = 512


def _cparams(sem):
    return pltpu.CompilerParams(dimension_semantics=sem, vmem_limit_bytes=VMEM_LIMIT)


def _dot(a, b):
    return jnp.dot(a, b, preferred_element_type=F32)


def _dot_t(a, b):
    return lax.dot_general(a, b, (((1,), (1,)), ((), ())), preferred_element_type=F32)


def _dot3(a, b):
    ah = a.astype(BF16)
    al = (a - ah.astype(F32)).astype(BF16)
    bh = b.astype(BF16)
    bl = (b - bh.astype(F32)).astype(BF16)
    return _dot(ah, bh) + (_dot(al, bh) + _dot(ah, bl))


def _dot3_split(a, w2):
    n = w2.shape[1] // 2
    ah = a.astype(BF16)
    al = (a - ah.astype(F32)).astype(BF16)
    both = _dot(ah, w2)
    return both[:, :n] + (_dot(al, w2[:, :n]) + both[:, n:])


def _sigmoid(x):
    return 0.5 * jnp.tanh(0.5 * x) + 0.5


def _softplus(x):
    return jnp.maximum(x, 0.0) + jnp.log1p(jnp.exp(-jnp.abs(x)))


def _gelu_tanh(x):
    return 0.5 * x * (1.0 + jnp.tanh(0.7978845608028654 * (x + 0.044715 * (x * x * x))))


def _rms(x, n=None):
    n = x.shape[-1] if n is None else n
    ms = jnp.sum(x * x, axis=-1, keepdims=True) * (1.0 / n)
    return x * lax.rsqrt(ms + RMS_EPS)


class _Rows:
    def __init__(self, n, src, mod):
        self.n, self.src, self.mod = n, src, mod


class _Stream:
    def __init__(self, B, L, C):
        assert L % TM == 0 and C % TM == 0 and B % 8 == 0
        self.B, self.L, self.C = B, L, C
        self.ltot = L + C
        self.lt, self.ct = L // TM, C // TM
        self.tpb = self.lt + self.ct
        self.NT = B * self.ltot

    def all_rows(self):
        tpb, ct, B = self.tpb, self.ct, self.B
        return _Rows(B * tpb, lambda k: k, lambda k: jnp.where(k % tpb < ct, B, k // tpb))

    def latent_rows(self):
        tpb, ct, lt = self.tpb, self.ct, self.lt
        return _Rows(self.B * lt, lambda k: (k // lt) * tpb + ct + k % lt, lambda k: k // lt)

    def dense_latent_rows(self):
        lt = self.lt
        return _Rows(self.B * lt, lambda k: k, lambda k: k // lt)


def _mod_kernel(c_ref, w_ref, b_ref, o_ref):
    c = c_ref[...]
    o_ref[0] = _dot3(c * _sigmoid(c), w_ref[0]) + b_ref[0]


def _modulation(cc, ada_w, ada_b):
    depth, d, n = ada_w.shape
    tn = 1536
    return pl.pallas_call(
        _mod_kernel,
        grid=(depth, n // tn),
        in_specs=[pl.BlockSpec((16, d), lambda l, j: (0, 0)),
                  pl.BlockSpec((1, d, tn), lambda l, j: (l, 0, j)),
                  pl.BlockSpec((1, 1, tn), lambda l, j: (l, 0, j))],
        out_specs=pl.BlockSpec((1, 16, tn), lambda l, j: (l, 0, j)),
        out_shape=jax.ShapeDtypeStruct((depth, 16, n), F32),
        compiler_params=_cparams(("arbitrary", "arbitrary")),
        name="ada_modulation",
    )(cc, ada_w, ada_b.reshape(depth, 1, n))


def _norm_mod(x, gain, shift, scale):
    return _rms(x) * gain * (1.0 + scale) + shift


def _norm_proj_kernel(x_ref, mod_ref, g_ref, w_ref, o_ref):
    mod = mod_ref[0]
    h = _norm_mod(x_ref[...], g_ref[...], mod[0:1, :], mod[1:2, :]).astype(BF16)
    o_ref[...] = _dot(h, w_ref[...]).astype(o_ref.dtype)


def _norm_proj(rows, x, mod, gain, w, name):
    n = w.shape[1]
    const = lambda k: (0, 0)
    return pl.pallas_call(
        _norm_proj_kernel,
        grid=(rows.n,),
        in_specs=[pl.BlockSpec((TM, D_MODEL), lambda k: (rows.src(k), 0)),
                  pl.BlockSpec((1, 8, D_MODEL), lambda k: (rows.mod(k), 0, 0)),
                  pl.BlockSpec((1, D_MODEL), const),
                  pl.BlockSpec((D_MODEL, n), const)],
        out_specs=pl.BlockSpec((TM, n), lambda k: (k, 0)),
        out_shape=jax.ShapeDtypeStruct((rows.n * TM, n), F32),
        compiler_params=_cparams(("arbitrary",)),
        name=name,
    )(x, mod, gain, w)


RG_TR = 32


def _row_permutation(B):
    n = B * RG_TR
    r = jnp.arange(n)
    src = (r % B) * RG_TR + r // B
    return (src[:, None] == jnp.arange(n)[None, :]).astype(BF16)


def _time_tile_mod(mod_ref, is_ctx, B, row):
    return jnp.where(is_ctx, mod_ref[B:B + 1, row:row + 1, :], mod_ref[0:B, row:row + 1, :])


def _rg_in_kernel(x_ref, mod_ref, g_ref, perm_ref, wg_ref, wu_ref, gate_ref, u_ref, *, B, ctx_tiles):
    is_ctx = pl.program_id(0) < ctx_tiles
    h = _norm_mod(x_ref[...], g_ref[...], _time_tile_mod(mod_ref, is_ctx, B, 0), _time_tile_mod(mod_ref, is_ctx, B, 1))
    h = h.reshape(B * RG_TR, D_MODEL).astype(BF16)
    h = _dot(perm_ref[...], h).astype(BF16)
    gate_ref[...] = _gelu_tanh(_dot(h, wg_ref[...])).astype(gate_ref.dtype)
    u_ref[...] = _dot(h, wu_ref[...])


def _rg_in(st, x, mod, gain, perm, w_in):
    B, W = st.B, D_MODEL
    R = B * RG_TR
    nt = st.ltot // RG_TR
    const = lambda t: (0, 0)
    return pl.pallas_call(
        functools.partial(_rg_in_kernel, B=B, ctx_tiles=st.C // RG_TR),
        grid=(nt,),
        in_specs=[pl.BlockSpec((B, RG_TR, D_MODEL), lambda t: (0, t, 0)),
                  pl.BlockSpec((B + 1, 8, D_MODEL), lambda t: (0, 0, 0)),
                  pl.BlockSpec((1, D_MODEL), const),
                  pl.BlockSpec((R, R), const),
                  pl.BlockSpec((D_MODEL, W), const),
                  pl.BlockSpec((D_MODEL, W), lambda t: (0, 1))],
        out_specs=[pl.BlockSpec((R, W), lambda t: (t, 0)),
                   pl.BlockSpec((R, W), lambda t: (t, 0))],
        out_shape=[jax.ShapeDtypeStruct((st.ltot * B, W), BF16), jax.ShapeDtypeStruct((st.ltot * B, W), F32)],
        compiler_params=_cparams(("arbitrary",)),
        name="rg_in_proj",
    )(x.reshape(B, st.ltot, D_MODEL), mod, gain, perm, w_in, w_in)


def _rg_tile_order(d, k, ct, ntt):
    bwd = jnp.where(k < ct, ct - 1 - k, ntt - 1 - (k - ct))
    return jnp.where(d == 0, k, bwd)


def _rg_scan_kernel(um_ref, up_ref, un_ref, cw_ref, cb_ref, wg_ref, gb_ref, lam_ref, o_ref,
                    ext_ref, a_ref, b_ref, h_ref, *, B, ct, ntt):
    d = pl.program_id(0)
    k = pl.program_id(1)
    tile = _rg_tile_order(d, k, ct, ntt)
    R = RG_TT * B

    @pl.when(k == 0)
    def _():
        h_ref[...] = jnp.zeros_like(h_ref)

    seq_start = jnp.logical_or(tile == 0, tile == ct)
    seq_end = jnp.logical_or(tile == ct - 1, tile == ntt - 1)
    ext_ref[0:2 * B, :] = jnp.where(seq_start, 0.0, up_ref[...])
    ext_ref[2 * B:2 * B + R, :] = um_ref[...]
    ext_ref[2 * B + R:3 * B + R, :] = jnp.where(seq_end, 0.0, un_ref[...])
    cw = cw_ref[...]
    uc = cb_ref[...] + cw[0:1, :] * ext_ref[0:R, :]
    for j in range(1, RG_CONV_W):
        uc = uc + cw[j:j + 1, :] * ext_ref[j * B:j * B + R, :]
    ucb = uc.astype(BF16)
    c_half = (-0.5 * RG_C) * _softplus(-lam_ref[0])
    for c in range(D_MODEL // RG_CHUNK):
        sl = slice(c * RG_CHUNK, (c + 1) * RG_CHUNK)
        z = _dot(ucb[:, sl], wg_ref[0, c]) + gb_ref[0, c]
        log_a = c_half[:, sl] * jnp.tanh(z[:, :RG_CHUNK]) + c_half[:, sl]
        u_half = 0.5 * uc[:, sl]
        gated_u = jnp.tanh(z[:, RG_CHUNK:]) * u_half + u_half
        a = jnp.exp(log_a)
        one_minus_a2 = -jnp.tanh(log_a) * (a * a + 1.0)
        a_ref[:, sl] = a
        b_ref[:, sl] = jnp.sqrt(one_minus_a2) * gated_u

    def scan(times):
        for c in range(D_MODEL // 128):
            cs = slice(c * 128, (c + 1) * 128)
            h = h_ref[:, cs]
            for t in times:
                rs = slice(t * B, (t + 1) * B)
                h = a_ref[rs, cs] * h + b_ref[rs, cs]
                b_ref[rs, cs] = h
            h_ref[:, cs] = h

    pl.when(d == 0)(lambda: scan(range(RG_TT)))
    pl.when(d == 1)(lambda: scan(range(RG_TT - 1, -1, -1)))
    o_ref[0] = b_ref[...].astype(o_ref.dtype)


def _rg_scan(st, u_tm, conv_w, conv_b, wg, gb, lam):
    B, W = st.B, D_MODEL
    assert st.C % RG_TT == 0 and st.L % RG_TT == 0
    ltot = st.ltot
    ntt, ct = ltot // RG_TT, st.C // RG_TT
    R = RG_TT * B
    order = functools.partial(_rg_tile_order, ct=ct, ntt=ntt)
    nch = W // RG_CHUNK
    return pl.pallas_call(
        functools.partial(_rg_scan_kernel, B=B, ct=ct, ntt=ntt),
        grid=(2, ntt),
        in_specs=[pl.BlockSpec((R, W), lambda d, k: (order(d, k), 0)),
                  pl.BlockSpec((2 * B, W), lambda d, k: (jnp.maximum(order(d, k) * (RG_TT // 2) - 1, 0), 0)),
                  pl.BlockSpec((B, W), lambda d, k: (jnp.minimum((order(d, k) + 1) * RG_TT, ltot - 1), 0)),
                  pl.BlockSpec((RG_CONV_W, W), lambda d, k: (0, 0)),
                  pl.BlockSpec((1, W), lambda d, k: (0, 0)),
                  pl.BlockSpec((1, nch, RG_CHUNK, 2 * RG_CHUNK), lambda d, k: (d, 0, 0, 0)),
                  pl.BlockSpec((1, nch, 1, 2 * RG_CHUNK), lambda d, k: (d, 0, 0, 0)),
                  pl.BlockSpec((1, 1, W), lambda d, k: (d, 0, 0))],
        out_specs=pl.BlockSpec((1, R, W), lambda d, k: (d, order(d, k), 0)),
        out_shape=jax.ShapeDtypeStruct((2, ltot * B, W), BF16),
        scratch_shapes=[pltpu.VMEM((3 * B + R, W), F32),
                        pltpu.VMEM((R, W), F32),
                        pltpu.VMEM((R, W), F32),
                        pltpu.VMEM((B, W), F32)],
        compiler_params=_cparams(("arbitrary", "arbitrary")),
        name="rg_scan",
    )(u_tm, u_tm, u_tm, conv_w, conv_b, wg, gb, lam)


def _rg_gate_weights(gate_w, gate_b):
    nb = gate_w.shape[2]
    per = RG_CHUNK // RG_BLOCK_W
    nch = nb // per
    gw = gate_w.reshape(2, 2, nch, per, RG_BLOCK_W, RG_BLOCK_W)
    eye = jnp.eye(per, dtype=gate_w.dtype)
    bd = jnp.einsum('dgcnij,nm->dgcnimj', gw, eye).reshape(2, 2, nch, RG_CHUNK, RG_CHUNK)
    wg = (0.5 * jnp.concatenate([bd[:, 0], bd[:, 1]], axis=-1)).astype(BF16)
    gb = gate_b.reshape(2, 2, nch, 1, RG_CHUNK)
    gb = 0.5 * jnp.concatenate([gb[:, 0], gb[:, 1]], axis=-1)
    return wg, gb


def _rg_out_kernel(g_ref, hf_ref, hb_ref, perm_ref, w_ref, x_ref, mod_ref, gain_ref, wr_ref, br_ref,
                   xo_ref, f_ref, r_ref, cnt_ref, *, B, ctx_tiles, t0):
    is_ctx = pl.program_id(0) + t0 < ctx_tiles
    hsum = hf_ref[0].astype(F32) + hb_ref[0].astype(F32)
    lhs = (g_ref[...].astype(F32) * hsum).astype(BF16)
    lhs = _dot(perm_ref[...], lhs).astype(BF16)
    y = _dot(lhs, w_ref[...]).reshape(B, RG_TR, D_MODEL)
    m = lambda row: _time_tile_mod(mod_ref, is_ctx, B, row)
    xn = x_ref[...] + m(2) * y
    xo_ref[...] = xn
    f = _norm_mod(xn, gain_ref[...], m(3), m(4))
    f_ref[...] = f
    lg = _dot3_split(f.reshape(B * RG_TR, D_MODEL), wr_ref[...]) + br_ref[...]
    table, hist = _route(lg)
    r_ref[...] = table.reshape(B, RG_TR, 128)
    _accumulate_counts(cnt_ref, hist)


def _rg_out(st, latent_only, gate, hs, perm, w_out, x, mod, gain, wr, br):
    B = st.B
    R = B * RG_TR
    t0 = st.C // RG_TR if latent_only else 0
    lo = st.L if latent_only else st.ltot
    nt = lo // RG_TR
    const = lambda t: (0, 0)
    blk = lambda w: pl.BlockSpec((B, RG_TR, w), lambda t: (0, t, 0))
    xo, f, route, counts = pl.pallas_call(
        functools.partial(_rg_out_kernel, B=B, ctx_tiles=st.C // RG_TR, t0=t0),
        grid=(nt,),
        in_specs=[pl.BlockSpec((R, D_MODEL), lambda t: (t + t0, 0)),
                  pl.BlockSpec((1, R, D_MODEL), lambda t: (0, t + t0, 0)),
                  pl.BlockSpec((1, R, D_MODEL), lambda t: (1, t + t0, 0)),
                  pl.BlockSpec((R, R), const),
                  pl.BlockSpec((D_MODEL, D_MODEL), const),
                  pl.BlockSpec((B, RG_TR, D_MODEL), lambda t: (0, t + t0, 0)),
                  pl.BlockSpec((B + 1, 8, D_MODEL), lambda t: (0, 0, 0)),
                  pl.BlockSpec((1, D_MODEL), const),
                  pl.BlockSpec((D_MODEL, 256), const),
                  pl.BlockSpec((1, 128), const)],
        out_specs=[blk(D_MODEL), blk(D_MODEL), blk(128), pl.BlockSpec((1, 128), const)],
        out_shape=[jax.ShapeDtypeStruct((B, lo, D_MODEL), F32),
                   jax.ShapeDtypeStruct((B, lo, D_MODEL), F32),
                   jax.ShapeDtypeStruct((B, lo, 128), F32),
                   jax.ShapeDtypeStruct((1, 128), F32)],
        compiler_params=_cparams(("arbitrary",)),
        name="rg_out",
    )(gate, hs, hs, perm, w_out, x.reshape(B, st.ltot, D_MODEL), mod, gain, wr, br)
    return xo.reshape(B * lo, D_MODEL), f.reshape(B * lo, D_MODEL), route.reshape(B * lo, 128), counts


_ROPE_HALF = ROPE_AXIS_DIM // 2
_MLA_SRC_DIM = (list(range(MLA_NOPE + ROPE_AXIS_DIM)) + list(range(MLA_NOPE, MLA_NOPE + _ROPE_HALF))
                + list(range(MLA_NOPE + ROPE_AXIS_DIM, MLA_QK))
                + list(range(MLA_NOPE + ROPE_AXIS_DIM, MLA_NOPE + ROPE_AXIS_DIM + _ROPE_HALF)))
_MLA_REAL_LANE = ([1.0] * (MLA_NOPE + ROPE_AXIS_DIM) + [0.0] * _ROPE_HALF + [1.0] * ROPE_AXIS_DIM
                  + [0.0] * _ROPE_HALF + [0.0] * (MLA_HP - len(_MLA_SRC_DIM)))


def _mla_up_kernel(dn_ref, qn_ref, kvn_ref, wq_ref, wk_ref, we_ref, wv_ref, real_ref, cq_ref, sq_ref, ck_ref, sk_ref,
                   q_ref, k_ref, v_ref):
    dn = dn_ref[...]
    cq = _rms(dn[:, :MLA_Q_RANK]) * qn_ref[...]
    ckv = _rms(dn[:, MLA_Q_RANK:MLA_Q_RANK + MLA_KV_RANK]) * kvn_ref[...]
    kr = dn[:, MLA_Q_RANK + MLA_KV_RANK:]
    kr_hi = kr.astype(BF16)
    kr_lo = (kr - kr_hi.astype(F32)).astype(BF16)
    ckvb = ckv.astype(BF16)
    q_pre = _dot(cq.astype(BF16), wq_ref[...])
    k_pre = _dot(ckvb, wk_ref[...]) + (_dot(kr_hi, we_ref[...]) + _dot(kr_lo, we_ref[...]))
    v_ref[...] = _dot(ckvb, wv_ref[...]).astype(v_ref.dtype)
    real = real_ref[...]

    cos_q, sin_q, cos_k, sin_k = cq_ref[...], sq_ref[...], ck_ref[...], sk_ref[...]
    jobs = []
    for h in range(MLA_HEADS):
        sl = slice(h * MLA_HP, (h + 1) * MLA_HP)
        jobs.append((q_ref, sl, q_pre[:, sl], cos_q, sin_q))
        jobs.append((k_ref, sl, k_pre[:, sl], cos_k, sin_k))
    ms = [jnp.sum(x * x * real, axis=-1, keepdims=True) * (1.0 / MLA_QK) for _, _, x, _, _ in jobs]
    xr = [job[2] * lax.rsqrt(m + RMS_EPS) for job, m in zip(jobs, ms)]
    rolled = [pltpu.roll(v, MLA_HP - _ROPE_HALF, 1) for v in xr]
    for (o_ref, sl, _, cos_g, sin_g), v, r in zip(jobs, xr, rolled):
        o_ref[:, sl] = (v * cos_g + r * sin_g).astype(o_ref.dtype)


def _mla_up(st, rows, down, q_norm, kv_norm, wq, wk, we, wv, tables):
    hw = MLA_HEADS * MLA_HP
    const = lambda i: (0, 0)
    tpb, ct, lt = st.tpb, st.ct, st.lt
    rope_idx = lambda i: (jnp.where(i % tpb < ct, lt, i % tpb - ct), 0)
    real = jnp.asarray(_MLA_REAL_LANE, F32).reshape(1, MLA_HP)
    return pl.pallas_call(
        _mla_up_kernel,
        grid=(rows.n,),
        in_specs=[pl.BlockSpec((TM, 512), lambda i: (i, 0)),
                  pl.BlockSpec((1, MLA_Q_RANK), const),
                  pl.BlockSpec((1, MLA_KV_RANK), const),
                  pl.BlockSpec((MLA_Q_RANK, hw), const),
                  pl.BlockSpec((MLA_KV_RANK, hw), const),
                  pl.BlockSpec((128, hw), const),
                  pl.BlockSpec((MLA_KV_RANK, MLA_HEADS * MLA_V), const),
                  pl.BlockSpec((1, MLA_HP), const)] + [pl.BlockSpec((TM, MLA_HP), rope_idx)] * 4,
        out_specs=[pl.BlockSpec((TM, hw), lambda i: (i, 0)),
                   pl.BlockSpec((TM, hw), lambda i: (i, 0)),
                   pl.BlockSpec((TM, MLA_HEADS * MLA_V), lambda i: (i, 0))],
        out_shape=[jax.ShapeDtypeStruct((st.NT, hw), BF16),
                   jax.ShapeDtypeStruct((st.NT, hw), BF16),
                   jax.ShapeDtypeStruct((st.NT, MLA_HEADS * MLA_V), BF16)],
        compiler_params=_cparams(("arbitrary",)),
        name="mla_up_proj",
    )(down, q_norm, kv_norm, wq, wk, we, wv, real, *tables)


def _attn_kernel(q_ref, k_ref, v_ref, o_ref, vaug_ref, *, C, ct):
    qi = pl.program_id(2)

    @pl.when(qi == 0)
    def _():
        lane = lax.broadcasted_iota(jnp.int32, (k_ref.shape[0], 2 * MLA_V), 1)
        for hh in range(ATT_HEADS):
            pair = v_ref[:, (hh // 2) * 2 * MLA_V:(hh // 2 + 1) * 2 * MLA_V].astype(F32)
            if hh % 2 == 0:
                aug = jnp.where(lane < MLA_V, pair, jnp.where(lane == MLA_V, 1.0, 0.0))
            else:
                aug = jnp.where(lane >= MLA_V, pair, jnp.where(lane == 0, 1.0, 0.0))
            vaug_ref[hh] = aug.astype(BF16)

    def attend(nkeys):
        lane = lax.broadcasted_iota(jnp.int32, (TM, 2 * MLA_V), 1)

        def scores(hh):
            sl = slice(hh * MLA_HP, (hh + 1) * MLA_HP)
            return _dot_t(q_ref[:, sl], k_ref[0:nkeys, sl])

        s_next = scores(0)
        outs = []
        for hh in range(ATT_HEADS):
            s = s_next
            if hh + 1 < ATT_HEADS:
                s_next = scores(hh + 1)
            p = jnp.exp2(s - jnp.max(s, axis=-1, keepdims=True)).astype(BF16)
            o = _dot(p, vaug_ref[hh, 0:nkeys, :])
            rowsum = o[:, MLA_V:MLA_V + 1] if hh % 2 == 0 else o[:, 0:1]
            outs.append(o * (1.0 / rowsum))
            if hh % 2 == 1:
                pair = hh // 2
                o_ref[:, pair * 2 * MLA_V:(pair + 1) * 2 * MLA_V] = jnp.where(
                    lane < MLA_V, outs[hh - 1], outs[hh]).astype(o_ref.dtype)

    pl.when(qi < ct)(lambda: attend(C))
    pl.when(qi >= ct)(lambda: attend(k_ref.shape[0]))


def _attention(st, q, k, v):
    B, ltot, tpb = st.B, st.ltot, st.tpb
    hg = MLA_HEADS // ATT_HEADS
    return pl.pallas_call(
        functools.partial(_attn_kernel, C=st.C, ct=st.ct),
        grid=(B, hg, tpb),
        in_specs=[pl.BlockSpec((TM, ATT_HEADS * MLA_HP), lambda b, h, i: (b * tpb + i, h)),
                  pl.BlockSpec((ltot, ATT_HEADS * MLA_HP), lambda b, h, i: (b, h)),
                  pl.BlockSpec((ltot, ATT_HEADS * MLA_V), lambda b, h, i: (b, h))],
        out_specs=pl.BlockSpec((TM, ATT_HEADS * MLA_V), lambda b, h, i: (b * tpb + i, h)),
        out_shape=jax.ShapeDtypeStruct((st.NT, MLA_HEADS * MLA_V), BF16),
        scratch_shapes=[pltpu.VMEM((ATT_HEADS, ltot, 2 * MLA_V), BF16)],
        compiler_params=_cparams(("arbitrary", "arbitrary", "arbitrary")),
        name="mla_attention",
    )(q, k, v)


def _mla_weights(w_uq, w_ukv):
    H = MLA_HEADS
    src = jnp.asarray(_MLA_SRC_DIM, jnp.int32)
    pad = MLA_HP - len(_MLA_SRC_DIM)
    wq = jnp.pad(w_uq.reshape(MLA_Q_RANK, H, MLA_QK)[:, :, src], ((0, 0), (0, 0), (0, pad)))
    wkv = w_ukv.reshape(MLA_KV_RANK, H, MLA_NOPE + MLA_V)
    wk = jnp.pad(wkv[:, :, :MLA_NOPE], ((0, 0), (0, 0), (0, MLA_HP - MLA_NOPE)))
    wv = wkv[:, :, MLA_NOPE:]
    r = jnp.arange(128)[:, None]
    lane_dim = jnp.pad(src, (0, pad), constant_values=-1)[None, :]
    place = (lane_dim == r + MLA_NOPE).astype(BF16)
    we = jnp.tile(place, (1, H))
    return (wq.reshape(MLA_Q_RANK, H * MLA_HP).astype(BF16), wk.reshape(MLA_KV_RANK, H * MLA_HP).astype(BF16),
            we, wv.reshape(MLA_KV_RANK, H * MLA_V).astype(BF16))


def _rope_tables(L, qk_norm):
    rows = L // GRID_W
    row = jnp.broadcast_to(jnp.arange(rows, dtype=F32)[:, None], (rows, GRID_W)).reshape(L)
    col = jnp.broadcast_to(jnp.arange(GRID_W, dtype=F32)[None, :], (rows, GRID_W)).reshape(L)
    inv_freq = ROPE_BASE ** (-jnp.arange(0, ROPE_AXIS_DIM, 2, dtype=F32) / ROPE_AXIS_DIM)
    ar = row[:, None] * inv_freq
    ac = col[:, None] * inv_freq
    h8 = _ROPE_HALF
    one = jnp.ones((L, MLA_NOPE), F32)
    z8 = jnp.zeros((L, h8), F32)
    zpad = jnp.zeros((L, MLA_HP - len(_MLA_SRC_DIM)), F32)
    cos_t = jnp.concatenate([one, jnp.cos(ar), jnp.cos(ar), z8, jnp.cos(ac), jnp.cos(ac), z8, zpad], axis=1)
    sin_t = jnp.concatenate([0 * one, -jnp.sin(ar), jnp.sin(ar), z8, -jnp.sin(ac), jnp.sin(ac), z8, zpad], axis=1)
    ident = jnp.asarray(_MLA_REAL_LANE, F32)[None, :]
    cos_t = jnp.concatenate([cos_t, jnp.broadcast_to(ident, (TM, MLA_HP))], axis=0)
    sin_t = jnp.concatenate([sin_t, jnp.zeros((TM, MLA_HP), F32)], axis=0)
    src = jnp.asarray(_MLA_SRC_DIM, jnp.int32)
    pad = MLA_HP - len(_MLA_SRC_DIM)
    scale = MLA_QK ** -0.5 * math.log2(math.e)
    tables = []
    for g, s in ((qk_norm[0], scale), (qk_norm[1], 1.0)):
        g_lane = jnp.pad(g[src], (0, pad))
        g_partner = jnp.roll(g_lane, -h8)
        tables += [cos_t * (g_lane * s)[None, :], sin_t * (g_partner * s)[None, :]]
    return tables


def _log_sigmoid(x):
    return jnp.minimum(x, 0.0) - jnp.log1p(jnp.exp(-jnp.abs(x)))


def _mlstm_kernel(qf_ref, kf_ref, vf_ref, gf_ref, qb_ref, kb_ref, vb_ref, gb_ref, bias_ref, tril_ref, triu_ref,
                  of_ref, ob_ref, c_ref, n_ref, m_ref):
    T = ML_TC

    @pl.when(pl.program_id(1) == 0)
    def _():
        c_ref[...] = jnp.zeros_like(c_ref)
        n_ref[...] = jnp.zeros_like(n_ref)
        m_ref[...] = jnp.full(m_ref.shape, ML_M_INIT, F32)

    ti = lax.broadcasted_iota(jnp.int32, (T, T), 0)
    si = lax.broadcasted_iota(jnp.int32, (T, T), 1)
    dirs = ((qf_ref, kf_ref, vf_ref, gf_ref, of_ref, tril_ref), (qb_ref, kb_ref, vb_ref, gb_ref, ob_ref, triu_ref))
    for d, (q_ref, k_ref, v_ref, g_ref, o_ref, tri_ref) in enumerate(dirs):
        tri = (si <= ti) if d == 0 else (si >= ti)
        g = g_ref[...] + bias_ref[...]
        g_t = g.T
        lsg = _log_sigmoid(g)
        lsg_hi = lsg.astype(BF16)
        lsg_lo = (lsg - lsg_hi.astype(F32)).astype(BF16)
        cum = _dot(tri_ref[...], lsg_hi) + _dot(tri_ref[...], lsg_lo)
        cum_t = cum.T
        last = T - 1 if d == 0 else 0
        for h in range(ML_HEADS):
            st = d * ML_HEADS + h
            li, lf_ = (2 * d) * ML_HEADS + h, (2 * d + 1) * ML_HEADS + h
            ig_col = g[:, li:li + 1]
            ig_row = g_t[li:li + 1, :]
            b_col = cum[:, lf_:lf_ + 1]
            b_row = cum_t[lf_:lf_ + 1, :]
            total = cum[last:last + 1, lf_:lf_ + 1]
            m_old = m_ref[st, 0:1, 0:1]
            d_log = jnp.where(tri, b_col - b_row + ig_row, -jnp.inf)
            inter_log = b_col + m_old
            m_t = jnp.maximum(inter_log, jnp.max(d_log, axis=1, keepdims=True))
            qh = q_ref[:, h * ML_DQK:(h + 1) * ML_DQK] * (ML_DQK ** -0.5)
            kh = k_ref[:, h * ML_DQK:(h + 1) * ML_DQK]
            vh = v_ref[:, h * ML_DV:(h + 1) * ML_DV].astype(BF16)
            qb16 = qh.astype(BF16)
            s_mat = _dot_t(qb16, kh.astype(BF16)) * jnp.exp(d_log - m_t)
            inter = jnp.exp(inter_log - m_t)
            c_old = c_ref[st]
            n_old = n_ref[st, 0:1, :]
            num = _dot(s_mat.astype(BF16), vh) + inter * _dot(qb16, c_old.astype(BF16))
            den = jnp.sum(s_mat, axis=1, keepdims=True) + inter * jnp.sum(qh * n_old, axis=1, keepdims=True)
            o_ref[:, h * ML_DV:(h + 1) * ML_DV] = num / jnp.maximum(jnp.abs(den), jnp.exp(-m_t))
            w_log = total - b_col + ig_col
            m_new = jnp.maximum(total + m_old, jnp.max(w_log, axis=0, keepdims=True))
            w = jnp.exp(w_log - m_new)
            decay = jnp.exp(total + m_old - m_new)
            kw = kh * w
            c_ref[st] = decay * c_old + _dot(kw.T.astype(BF16), vh)
            n_ref[st, 0:1, :] = decay * n_old + jnp.sum(kw, axis=0, keepdims=True)
            m_ref[st] = jnp.broadcast_to(m_new, m_ref.shape[1:])


def _mlstm(st, proj, gate_bias):
    B = st.B
    assert st.L % ML_TC == 0 and st.C % ML_TC == 0
    cc = st.C // ML_TC
    nch = st.ltot // ML_TC
    qw = ML_HEADS * ML_DQK
    vw = ML_HEADS * ML_DV
    gcol = (2 * qw + 2 * vw) // 128

    def rb(d, b, k):
        chunk = k if d == 0 else jnp.where(k < cc, cc - 1 - k, nch - 1 - (k - cc))
        return b * nch + chunk

    def specs(d):
        return [pl.BlockSpec((ML_TC, qw), lambda b, k: (rb(d, b, k), 0)),
                pl.BlockSpec((ML_TC, qw), lambda b, k: (rb(d, b, k), 1)),
                pl.BlockSpec((ML_TC, vw), lambda b, k: (rb(d, b, k), (2 * qw) // vw)),
                pl.BlockSpec((ML_TC, 128), lambda b, k: (rb(d, b, k), gcol))]

    nst = 2 * ML_HEADS
    out = jax.ShapeDtypeStruct((st.NT, vw), F32)
    tril = jnp.tril(jnp.ones((ML_TC, ML_TC), BF16))
    return pl.pallas_call(
        _mlstm_kernel,
        grid=(B, nch),
        in_specs=specs(0) + specs(1) + [pl.BlockSpec((1, 128), lambda b, k: (0, 0)),
                                        pl.BlockSpec((ML_TC, ML_TC), lambda b, k: (0, 0)),
                                        pl.BlockSpec((ML_TC, ML_TC), lambda b, k: (0, 0))],
        out_specs=[pl.BlockSpec((ML_TC, vw), lambda b, k: (rb(0, b, k), 0)),
                   pl.BlockSpec((ML_TC, vw), lambda b, k: (rb(1, b, k), 0))],
        out_shape=[out, out],
        scratch_shapes=[pltpu.VMEM((nst, ML_DQK, ML_DV), F32),
                        pltpu.VMEM((nst, 8, ML_DQK), F32),
                        pltpu.VMEM((nst, 8, 128), F32)],
        compiler_params=_cparams(("arbitrary", "arbitrary")),
        name="mlstm_chunks",
    )(proj, proj, proj, proj, proj, proj, proj, proj, gate_bias, tril, tril.T)


def _route(lg):
    lane_i = lax.broadcasted_iota(jnp.int32, lg.shape, 1)
    lane = lane_i.astype(F32)
    neg = -jnp.inf
    gl = jnp.where(lane_i < MOE_GROUPS, lg, neg)
    gmax = jnp.max(gl, axis=-1, keepdims=True)
    gsum = jnp.sum(jnp.where(lane_i < MOE_GROUPS, jnp.exp(lg - gmax), 0.0), axis=-1, keepdims=True)
    p_top = 1.0 / gsum
    g_sel = jnp.min(jnp.where(gl == gmax, lane, 128.0), axis=-1, keepdims=True)
    group_of_lane = (lane_i >> 3).astype(F32) - 1.0
    el = jnp.where(group_of_lane == g_sel, lg, neg)
    e1 = jnp.max(el, axis=-1, keepdims=True)
    i1 = jnp.min(jnp.where(el == e1, lane, 128.0), axis=-1, keepdims=True)
    el2 = jnp.where(lane == i1, neg, el)
    e2 = jnp.max(el2, axis=-1, keepdims=True)
    i2 = jnp.min(jnp.where(el2 == e2, lane, 128.0), axis=-1, keepdims=True)
    t = jnp.exp(e2 - e1)
    w1 = p_top / (1.0 + t)
    w2 = w1 * t
    id1 = i1 - MOE_GROUPS
    id2 = i2 - MOE_GROUPS
    table = jnp.where(lane_i == 0, id1, jnp.where(lane_i == 1, id2,
                                                  jnp.where(lane_i == 2, w1, jnp.where(lane_i == 3, w2, 0.0))))
    chosen = jnp.where(lane == i1, 1.0, 0.0) + jnp.where(lane == i2, 1.0, 0.0)
    return table, jnp.sum(chosen, axis=0, keepdims=True)


def _accumulate_counts(cnt_ref, hist):
    @pl.when(pl.program_id(0) == 0)
    def _():
        cnt_ref[...] = jnp.zeros_like(cnt_ref)
    cnt_ref[...] += hist


def _lhs_mla(o_ref):
    return o_ref[...]


def _lhs_mlstm(hf_ref, hb_ref, og_ref, onorm_ref):
    hs = hf_ref[...] + hb_ref[...]
    og = _sigmoid(og_ref[...])
    parts = []
    for h in range(ML_HEADS):
        sl = slice(h * ML_DV, (h + 1) * ML_DV)
        parts.append(_rms(hs[:, sl]) * onorm_ref[:, sl] * og[:, sl])
    return jnp.concatenate(parts, axis=1)


def _mixer_out_kernel(*refs, n_lhs, lhs_fn):
    lhs_refs = refs[:n_lhs]
    w_ref, x_ref, mod_ref, gain_ref, wr_ref, br_ref, xo_ref, f_ref, r_ref, cnt_ref = refs[n_lhs:]
    y = _dot(lhs_fn(*lhs_refs).astype(BF16), w_ref[...])
    mod = mod_ref[0]
    xn = x_ref[...] + mod[2:3, :] * y
    xo_ref[...] = xn
    f = _norm_mod(xn, gain_ref[...], mod[3:4, :], mod[4:5, :])
    f_ref[...] = f
    r_ref[...], hist = _route(_dot3_split(f, wr_ref[...]) + br_ref[...])
    _accumulate_counts(cnt_ref, hist)


def _mixer_out(rows, lhs_fn, lhs_args, lhs_specs, w_out, x, mod, gain, wr, br, name):
    n = rows.n * TM
    const = lambda k: (0, 0)
    out = lambda w: pl.BlockSpec((TM, w), lambda k: (k, 0))
    return pl.pallas_call(
        functools.partial(_mixer_out_kernel, n_lhs=len(lhs_args), lhs_fn=lhs_fn),
        grid=(rows.n,),
        in_specs=list(lhs_specs) + [
            pl.BlockSpec((D_MODEL, D_MODEL), const),
            pl.BlockSpec((TM, D_MODEL), lambda k: (rows.src(k), 0)),
            pl.BlockSpec((1, 8, D_MODEL), lambda k: (rows.mod(k), 0, 0)),
            pl.BlockSpec((1, D_MODEL), const),
            pl.BlockSpec((D_MODEL, 256), const),
            pl.BlockSpec((1, 128), const)],
        out_specs=[out(D_MODEL), out(D_MODEL), out(128), pl.BlockSpec((1, 128), const)],
        out_shape=[jax.ShapeDtypeStruct((n, D_MODEL), F32),
                   jax.ShapeDtypeStruct((n, D_MODEL), F32),
                   jax.ShapeDtypeStruct((n, 128), F32),
                   jax.ShapeDtypeStruct((1, 128), F32)],
        compiler_params=_cparams(("arbitrary",)),
        name=name,
    )(*lhs_args, w_out, x, mod, gain, wr, br)


RANK_TILES = 4


def _rank_kernel(r_ref, start_ref, tril_ref, pos_ref, carry_ref):
    @pl.when(pl.program_id(0) == 0)
    def _():
        carry_ref[...] = jnp.zeros_like(carry_ref)

    r = r_ref[...]
    lane = lax.broadcasted_iota(jnp.int32, r.shape, 1).astype(F32)
    chosen = [jnp.where(lane == r[:, k:k + 1] + MOE_GROUPS, 1.0, 0.0) for k in range(MOE_TOPK)]
    both = functools.reduce(jnp.add, chosen)
    before = _dot(tril_ref[...], both.astype(BF16)) + (start_ref[...] + carry_ref[...])
    lane_i = lax.broadcasted_iota(jnp.int32, r.shape, 1)
    pos = jnp.zeros(r.shape, F32)
    for k in range(MOE_TOPK):
        pos = jnp.where(lane_i == k, jnp.sum(chosen[k] * before, axis=-1, keepdims=True), pos)
    pos_t = pos.T[0:8, :].astype(jnp.int32)
    for j in range(RANK_TILES):
        pos_ref[j] = pos_t[:, j * TM:(j + 1) * TM]
    carry_ref[...] += jnp.sum(both, axis=0, keepdims=True)


def _assignment_slots(ntiles, route, starts):
    assert ntiles % RANK_TILES == 0
    rows = RANK_TILES * TM
    tril = jnp.tril(jnp.ones((rows, rows), BF16), -1)
    return pl.pallas_call(
        _rank_kernel,
        grid=(ntiles // RANK_TILES,),
        in_specs=[pl.BlockSpec((rows, 128), lambda i: (i, 0)),
                  pl.BlockSpec((1, 128), lambda i: (0, 0)),
                  pl.BlockSpec((rows, rows), lambda i: (0, 0))],
        out_specs=pl.BlockSpec((RANK_TILES, 8, TM), lambda i: (i, 0, 0)),
        out_shape=jax.ShapeDtypeStruct((ntiles, 8, TM), jnp.int32),
        scratch_shapes=[pltpu.VMEM((1, 128), F32)],
        compiler_params=_cparams(("arbitrary",)),
        name="moe_rank",
    )(route, starts, tril)


def _row_wait(hbm, buf, sem):
    pltpu.make_async_copy(hbm.at[pl.ds(0, TM), :], buf, sem).wait()


def _dispatch_kernel(pos_ref, f_ref, xs_hbm, buf, sem, *, ntiles):
    i = pl.program_id(0)

    def step(s):
        @pl.when(i >= 2)
        def _():
            for _ in range(MOE_TOPK):
                _row_wait(xs_hbm, buf.at[s], sem.at[s])
        buf[s] = f_ref[...]
        for r in range(TM):
            for k in range(MOE_TOPK):
                pltpu.make_async_copy(buf.at[s, pl.ds(r, 1), :], xs_hbm.at[pl.ds(pos_ref[0, k, r], 1), :],
                                      sem.at[s]).start(priority=k % 2)

        @pl.when(i == ntiles - 1)
        def _():
            for slot in ((1 - s, s) if ntiles >= 2 else (s,)):
                for _ in range(MOE_TOPK):
                    _row_wait(xs_hbm, buf.at[slot], sem.at[slot])

    for s in range(2):
        pl.when(i % 2 == s)(functools.partial(step, s))


def _dispatch(ntiles, f, pos):
    n = ntiles * TM
    return pl.pallas_call(
        functools.partial(_dispatch_kernel, ntiles=ntiles),
        grid=(ntiles,),
        in_specs=[pl.BlockSpec((1, 8, TM), lambda i: (i, 0, 0), memory_space=pltpu.SMEM),
                  pl.BlockSpec((TM, D_MODEL), lambda i: (i, 0))],
        out_specs=pl.BlockSpec(memory_space=pl.ANY),
        out_shape=jax.ShapeDtypeStruct((MOE_TOPK * n, D_MODEL), F32),
        scratch_shapes=[pltpu.VMEM((2, TM, D_MODEL), F32), pltpu.SemaphoreType.DMA((2,))],
        compiler_params=_cparams(("arbitrary",)),
        name="moe_dispatch",
    )(pos, f)


def _expert_kernel(vb_ref, ve_ref, lo_ref, hi_ref, first_ref, newexp_ref, eslot_ref, enext_ref, x_ref, wgu_hbm,
                   wd_hbm, y_ref, wgu_f, wd_f, wgu_b, wd_b, sem, *, layer):
    v = pl.program_id(0)

    def fetch(e, s):
        return (pltpu.make_async_copy(wgu_hbm.at[layer, e], wgu_f.at[s], sem.at[0, s]),
                pltpu.make_async_copy(wd_hbm.at[layer, e], wd_f.at[s], sem.at[1, s]))

    @pl.when(newexp_ref[v] == 1)
    def _():
        for s in range(2):
            @pl.when(eslot_ref[v] == s)
            def _():
                @pl.when(v == 0)
                def _():
                    for c in fetch(ve_ref[v], s):
                        c.start()

                @pl.when(enext_ref[v] >= 0)
                def _():
                    for c in fetch(enext_ref[v], 1 - s):
                        c.start()
                for c in fetch(ve_ref[v], s):
                    c.wait()
                wgu_b[...] = wgu_f[s].astype(BF16)
                wd_b[...] = wd_f[s].astype(BF16)

    @pl.when(hi_ref[v] > lo_ref[v])
    def _():
        gu = _dot(x_ref[...].astype(BF16), wgu_b[...])
        gate = gu[:, :MOE_FF]
        act = gate * _sigmoid(gate) * gu[:, MOE_FF:]
        y = _dot(act.astype(BF16), wd_b[...])
        r = lax.broadcasted_iota(jnp.int32, (MOE_BM, 1), 0)
        mine = jnp.logical_and(r >= lo_ref[v], r < hi_ref[v])
        y = jnp.where(mine, y, 0.0)

        @pl.when(first_ref[v] == 1)
        def _():
            y_ref[...] = y

        @pl.when(first_ref[v] == 0)
        def _():
            y_ref[...] += y


def _expert_ffn(xs, visits, layer, w_gate_up, w_down):
    nvis = visits[0].shape[0]
    blk_idx = lambda v, vb, *_: (vb[v], 0)
    grid_spec = pltpu.PrefetchScalarGridSpec(
        num_scalar_prefetch=8,
        grid=(nvis,),
        in_specs=[pl.BlockSpec((MOE_BM, D_MODEL), blk_idx),
                  pl.BlockSpec(memory_space=pl.ANY),
                  pl.BlockSpec(memory_space=pl.ANY)],
        out_specs=pl.BlockSpec((MOE_BM, D_MODEL), blk_idx),
        scratch_shapes=[pltpu.VMEM((2, D_MODEL, 2 * MOE_FF), F32), pltpu.VMEM((2, MOE_FF, D_MODEL), F32),
                        pltpu.VMEM((D_MODEL, 2 * MOE_FF), BF16), pltpu.VMEM((MOE_FF, D_MODEL), BF16),
                        pltpu.SemaphoreType.DMA((2, 2))],
    )
    return pl.pallas_call(
        functools.partial(_expert_kernel, layer=layer),
        grid_spec=grid_spec,
        out_shape=jax.ShapeDtypeStruct(xs.shape, F32),
        compiler_params=_cparams(("arbitrary",)),
        name="moe_expert_ffn",
    )(*visits, xs, w_gate_up, w_down)


def _combine_kernel(pos_ref, nxt_ref, x_ref, r_ref, mod_ref, ys_hbm, o_ref, ybuf, sem, *, ntiles):
    i = pl.program_id(0)

    def gather(table, s):
        for r in range(TM):
            for k in range(MOE_TOPK):
                pltpu.make_async_copy(ys_hbm.at[pl.ds(table[0, k, r], 1), :], ybuf.at[s, k, pl.ds(r, 1), :],
                                      sem.at[s]).start(priority=k % 2)

    @pl.when(i == 0)
    def _():
        gather(pos_ref, 0)

    def step(s):
        @pl.when(i + 1 < ntiles)
        def _():
            gather(nxt_ref, 1 - s)
        for k in range(MOE_TOPK):
            _row_wait(ys_hbm, ybuf.at[s, k], sem.at[s])
        w = r_ref[...]
        y = w[:, MOE_TOPK:MOE_TOPK + 1] * ybuf[s, 0]
        for k in range(1, MOE_TOPK):
            y = y + w[:, MOE_TOPK + k:MOE_TOPK + k + 1] * ybuf[s, k]
        o_ref[...] = x_ref[...] + mod_ref[0][5:6, :] * y

    for s in range(2):
        pl.when(i % 2 == s)(functools.partial(step, s))


def _combine(rows, x, ys, pos, route, mod):
    n = rows.n
    spec = pl.BlockSpec((TM, D_MODEL), lambda i: (i, 0))
    return pl.pallas_call(
        functools.partial(_combine_kernel, ntiles=n),
        grid=(n,),
        in_specs=[pl.BlockSpec((1, 8, TM), lambda i: (i, 0, 0), memory_space=pltpu.SMEM),
                  pl.BlockSpec((1, 8, TM), lambda i: (jnp.minimum(i + 1, n - 1), 0, 0), memory_space=pltpu.SMEM),
                  spec,
                  pl.BlockSpec((TM, 128), lambda i: (i, 0)),
                  pl.BlockSpec((1, 8, D_MODEL), lambda i: (rows.mod(i), 0, 0)),
                  pl.BlockSpec(memory_space=pl.ANY)],
        out_specs=spec,
        out_shape=jax.ShapeDtypeStruct((n * TM, D_MODEL), F32),
        scratch_shapes=[pltpu.VMEM((2, MOE_TOPK, TM, D_MODEL), F32), pltpu.SemaphoreType.DMA((2,))],
        compiler_params=_cparams(("arbitrary",)),
        name="moe_combine",
    )(pos, pos, x, route, mod, ys)


def _visit_tables(bounds, nk):
    E = MOE_EXPERTS
    nblk = nk // MOE_BM
    nvis = nblk + E
    starts, ends = bounds[:-1], bounds[1:]
    fb = starts // MOE_BM
    nv = jnp.where(ends > starts, (ends - 1) // MOE_BM - fb + 1, 0)
    cum = jnp.cumsum(nv)
    total = cum[-1]
    v = jnp.arange(nvis, dtype=jnp.int32)
    active = v < total
    vc = jnp.minimum(v, total - 1)
    ve = jnp.minimum(jnp.sum((cum[None, :] <= vc[:, None]).astype(jnp.int32), axis=1), E - 1)
    vb = fb[ve] + (vc - (cum - nv)[ve])
    lo = jnp.where(active, jnp.maximum(starts[ve], vb * MOE_BM) - vb * MOE_BM, 0)
    hi = jnp.where(active, jnp.minimum(ends[ve], (vb + 1) * MOE_BM) - vb * MOE_BM, 0)
    prev_b = jnp.concatenate([jnp.full((1,), -1, jnp.int32), vb[:-1]])
    first = jnp.logical_and(active, vb != prev_b)
    prev_e = jnp.concatenate([jnp.full((1,), -1, jnp.int32), ve[:-1]])
    new_expert = jnp.logical_and(active, ve != prev_e)
    eslot = (jnp.cumsum(new_expert.astype(jnp.int32)) - 1) % 2
    at = jnp.where(new_expert, v, nvis)
    nxt = jnp.concatenate([lax.cummin(at[::-1])[::-1][1:], jnp.full((1,), nvis, jnp.int32)])
    enext = jnp.where(nxt < nvis, ve[jnp.minimum(nxt, nvis - 1)], -1)
    i32 = lambda a: a.astype(jnp.int32)
    return i32(vb), i32(ve), i32(lo), i32(hi), i32(first), i32(new_expert), i32(eslot), i32(enext)


def _moe(rows, x, f, route, counts, mod, layer, w_gate_up, w_down):
    n = rows.n * TM
    cum = jnp.cumsum(counts[0])
    starts = (cum - counts[0]).reshape(1, 128)
    bounds = jnp.concatenate([starts[0, MOE_GROUPS:MOE_GROUPS + MOE_EXPERTS], cum[-1:]]).astype(jnp.int32)
    pos = _assignment_slots(rows.n, route, starts)
    xs = _dispatch(rows.n, f, pos)
    ys = _expert_ffn(xs, _visit_tables(bounds, n * MOE_TOPK), layer, w_gate_up, w_down)
    return _combine(rows, x, ys, pos, route, mod)


def kernel(x, c, ctx, c_ctx, ada_w, ada_b, norm_mix, norm_ffn, rg_w_in, rg_conv_w, rg_conv_b, rg_gate_w, rg_gate_b, rg_lambda, rg_w_out, mla_w_down, mla_q_norm, mla_kv_norm, mla_w_uq, mla_w_ukv, mla_qk_norm, mla_w_o, ml_w_in, ml_gate_b, ml_out_norm, ml_w_out, moe_w_group, moe_b_group, moe_w_expert, moe_b_expert, moe_w_gate_up, moe_w_down):
    B, L, D = x.shape
    C = ctx.shape[1]
    depth = ada_w.shape[0]
    assert D == D_MODEL
    st = _Stream(B, L, C)

    xs = jnp.concatenate([ctx, x], axis=1).reshape(st.NT, D)
    cc = jnp.zeros((16, D), F32).at[:B].set(c).at[B].set(c_ctx)
    mod_all = _modulation(cc, ada_w, ada_b)
    mod_all = jnp.pad(mod_all[:, :B + 1].reshape(depth, B + 1, 6, D), ((0, 0), (0, 0), (0, 2), (0, 0)))
    perm = _row_permutation(B)

    row = lambda a: a.reshape(1, -1)

    for i in range(depth):
        last = i == depth - 1
        mod = mod_all[i]
        kind, j = i % 3, i // 3
        all_rows = st.all_rows()
        out_rows = st.latent_rows() if last else all_rows
        tile_spec = lambda w: pl.BlockSpec((TM, w), lambda k: (out_rows.src(k), 0))
        wr = jnp.zeros((D, 128), F32).at[:, :MOE_GROUPS].set(moe_w_group[i]) \
            .at[:, MOE_GROUPS:MOE_GROUPS + MOE_EXPERTS].set(moe_w_expert[i])
        wr_hi = wr.astype(BF16)
        wr = jnp.concatenate([wr_hi, (wr - wr_hi.astype(F32)).astype(BF16)], axis=1)
        br = jnp.zeros((1, 128), F32).at[0, :MOE_GROUPS].set(moe_b_group[i]) \
            .at[0, MOE_GROUPS:MOE_GROUPS + MOE_EXPERTS].set(moe_b_expert[i])
        out_args = (xs, mod, row(norm_ffn[i]), wr, br)

        if kind == 0:
            gate, u = _rg_in(st, xs, mod, row(norm_mix[i]), perm, rg_w_in[j].astype(BF16))
            wg, gb = _rg_gate_weights(rg_gate_w[j], rg_gate_b[j])
            hs = _rg_scan(st, u, rg_conv_w[j], row(rg_conv_b[j]), wg, gb, rg_lambda[j].reshape(2, 1, D))
            xs, f, route, counts = _rg_out(st, last, gate, hs, perm.T, rg_w_out[j].astype(BF16), *out_args)
        elif kind == 1:
            w_down = jnp.pad(mla_w_down[j], ((0, 0), (0, 512 - mla_w_down.shape[2]))).astype(BF16)
            down = _norm_proj(all_rows, xs, mod, row(norm_mix[i]), w_down, name="mla_down_proj")
            wq, wk, we, wv = _mla_weights(mla_w_uq[j], mla_w_ukv[j])
            q, k, v = _mla_up(st, all_rows, down, row(mla_q_norm[j]), row(mla_kv_norm[j]), wq, wk, we, wv,
                              _rope_tables(L, mla_qk_norm[j]))
            o = _attention(st, q, k, v)
            xs, f, route, counts = _mixer_out(out_rows, _lhs_mla, (o,), (tile_spec(D),),
                                      mla_w_o[j].astype(BF16), *out_args, name="mla_out")
        else:
            n_in = ml_w_in.shape[2]
            w_in = jnp.pad(ml_w_in[j], ((0, 0), (0, ML_NP - n_in))).astype(BF16)
            proj = _norm_proj(all_rows, xs, mod, row(norm_mix[i]), w_in, name="mlstm_in_proj")
            gate_bias = jnp.pad(ml_gate_b[j].reshape(1, -1), ((0, 0), (0, 128 - 4 * ML_HEADS)))
            hf, hb = _mlstm(st, proj, gate_bias)
            og_spec = pl.BlockSpec((TM, D), lambda k: (out_rows.src(k), 2))
            xs, f, route, counts = _mixer_out(out_rows, _lhs_mlstm, (hf, hb, proj, row(ml_out_norm[j])),
                                      (tile_spec(D), tile_spec(D), og_spec, pl.BlockSpec((1, D), lambda k: (0, 0))),
                                      ml_w_out[j].astype(BF16), *out_args, name="mlstm_out")

        moe_rows = st.dense_latent_rows() if last else all_rows
        xs = _moe(moe_rows, xs, f, route, counts, mod, i, moe_w_gate_up, moe_w_down)

    return xs.reshape(B, L, D)
```

```python
import functools
import math

import jax
import jax.numpy as jnp
from jax import lax
from jax.experimental import pallas as pl
from jax.experimental.pallas import tpu as pltpu

F32 = jnp.float32
BF16 = jnp.bfloat16

D_MODEL = 1024
RMS_EPS = 1e-6

TM = 256
VMEM_LIMIT = 48 * 1024 * 1024

RG_BLOCK_W = 64
RG_CHUNK = 256
RG_CONV_W = 4
RG_C = 8.0
RG_TT = 64

MLA_HEADS = 16
MLA_Q_RANK = 256
MLA_KV_RANK = 128
MLA_NOPE = 64
MLA_ROPE = 32
MLA_V = 64
MLA_QK = MLA_NOPE + MLA_ROPE
MLA_HP = 128
ROPE_AXIS_DIM = MLA_ROPE // 2
ROPE_BASE = 10000.0
GRID_W = 64
ATT_HEADS = 8

ML_HEADS = 4
ML_DV = 256
ML_DQK = 128
ML_TC = 256
ML_M_INIT = -1e30
ML_NP = 3200

MOE_GROUPS = 8
MOE_PER_GROUP = 8
MOE_EXPERTS = 64
MOE_TOPK = 2
MOE_FF = 256
MOE_BM = 512


def _cparams(sem):
    return pltpu.CompilerParams(dimension_semantics=sem, vmem_limit_bytes=VMEM_LIMIT)


def _dot(a, b):
    return jnp.dot(a, b, preferred_element_type=F32)


def _dot_t(a, b):
    return lax.dot_general(a, b, (((1,), (1,)), ((), ())), preferred_element_type=F32)


def _dot3(a, b):
    ah = a.astype(BF16)
    al = (a - ah.astype(F32)).astype(BF16)
    bh = b.astype(BF16)
    bl = (b - bh.astype(F32)).astype(BF16)
    return _dot(ah, bh) + (_dot(al, bh) + _dot(ah, bl))


def _dot3_split(a, w2):
    n = w2.shape[1] // 2
    ah = a.astype(BF16)
    al = (a - ah.astype(F32)).astype(BF16)
    both = _dot(ah, w2)
    return both[:, :n] + (_dot(al, w2[:, :n]) + both[:, n:])


def _sigmoid(x):
    return 0.5 * jnp.tanh(0.5 * x) + 0.5


def _softplus(x):
    return jnp.maximum(x, 0.0) + jnp.log1p(jnp.exp(-jnp.abs(x)))


def _gelu_tanh(x):
    return 0.5 * x * (1.0 + jnp.tanh(0.7978845608028654 * (x + 0.044715 * (x * x * x))))


def _rms(x, n=None):
    n = x.shape[-1] if n is None else n
    ms = jnp.sum(x * x, axis=-1, keepdims=True) * (1.0 / n)
    return x * lax.rsqrt(ms + RMS_EPS)


class _Rows:
    def __init__(self, n, src, mod):
        self.n, self.src, self.mod = n, src, mod


class _Stream:
    def __init__(self, B, L, C):
        assert L % TM == 0 and C % TM == 0 and B % 8 == 0
        self.B, self.L, self.C = B, L, C
        self.ltot = L + C
        self.lt, self.ct = L // TM, C // TM
        self.tpb = self.lt + self.ct
        self.NT = B * self.ltot

    def all_rows(self):
        tpb, ct, B = self.tpb, self.ct, self.B
        return _Rows(B * tpb, lambda k: k, lambda k: jnp.where(k % tpb < ct, B, k // tpb))

    def latent_rows(self):
        tpb, ct, lt = self.tpb, self.ct, self.lt
        return _Rows(self.B * lt, lambda k: (k // lt) * tpb + ct + k % lt, lambda k: k // lt)

    def dense_latent_rows(self):
        lt = self.lt
        return _Rows(self.B * lt, lambda k: k, lambda k: k // lt)


def _mod_kernel(c_ref, w_ref, b_ref, o_ref):
    c = c_ref[...]
    o_ref[0] = _dot3(c * _sigmoid(c), w_ref[0]) + b_ref[0]


def _modulation(cc, ada_w, ada_b):
    depth, d, n = ada_w.shape
    tn = 1536
    return pl.pallas_call(
        _mod_kernel,
        grid=(depth, n // tn),
        in_specs=[pl.BlockSpec((16, d), lambda l, j: (0, 0)),
                  pl.BlockSpec((1, d, tn), lambda l, j: (l, 0, j)),
                  pl.BlockSpec((1, 1, tn), lambda l, j: (l, 0, j))],
        out_specs=pl.BlockSpec((1, 16, tn), lambda l, j: (l, 0, j)),
        out_shape=jax.ShapeDtypeStruct((depth, 16, n), F32),
        compiler_params=_cparams(("arbitrary", "arbitrary")),
        name="ada_modulation",
    )(cc, ada_w, ada_b.reshape(depth, 1, n))


def _norm_mod(x, gain, shift, scale):
    return _rms(x) * gain * (1.0 + scale) + shift


def _norm_proj_kernel(x_ref, mod_ref, g_ref, w_ref, o_ref):
    mod = mod_ref[0]
    h = _norm_mod(x_ref[...], g_ref[...], mod[0:1, :], mod[1:2, :]).astype(BF16)
    o_ref[...] = _dot(h, w_ref[...]).astype(o_ref.dtype)


def _norm_proj(rows, x, mod, gain, w, name):
    n = w.shape[1]
    const = lambda k: (0, 0)
    return pl.pallas_call(
        _norm_proj_kernel,
        grid=(rows.n,),
        in_specs=[pl.BlockSpec((TM, D_MODEL), lambda k: (rows.src(k), 0)),
                  pl.BlockSpec((1, 8, D_MODEL), lambda k: (rows.mod(k), 0, 0)),
                  pl.BlockSpec((1, D_MODEL), const),
                  pl.BlockSpec((D_MODEL, n), const)],
        out_specs=pl.BlockSpec((TM, n), lambda k: (k, 0)),
        out_shape=jax.ShapeDtypeStruct((rows.n * TM, n), F32),
        compiler_params=_cparams(("arbitrary",)),
        name=name,
    )(x, mod, gain, w)


RG_TR = 32


def _row_permutation(B):
    n = B * RG_TR
    r = jnp.arange(n)
    src = (r % B) * RG_TR + r // B
    return (src[:, None] == jnp.arange(n)[None, :]).astype(BF16)


def _time_tile_mod(mod_ref, is_ctx, B, row):
    return jnp.where(is_ctx, mod_ref[B:B + 1, row:row + 1, :], mod_ref[0:B, row:row + 1, :])


def _rg_in_kernel(x_ref, mod_ref, g_ref, perm_ref, wg_ref, wu_ref, gate_ref, u_ref, *, B, ctx_tiles):
    is_ctx = pl.program_id(0) < ctx_tiles
    h = _norm_mod(x_ref[...], g_ref[...], _time_tile_mod(mod_ref, is_ctx, B, 0), _time_tile_mod(mod_ref, is_ctx, B, 1))
    h = h.reshape(B * RG_TR, D_MODEL).astype(BF16)
    h = _dot(perm_ref[...], h).astype(BF16)
    gate_ref[...] = _gelu_tanh(_dot(h, wg_ref[...])).astype(gate_ref.dtype)
    u_ref[...] = _dot(h, wu_ref[...])


def _rg_in(st, x, mod, gain, perm, w_in):
    B, W = st.B, D_MODEL
    R = B * RG_TR
    nt = st.ltot // RG_TR
    const = lambda t: (0, 0)
    return pl.pallas_call(
        functools.partial(_rg_in_kernel, B=B, ctx_tiles=st.C // RG_TR),
        grid=(nt,),
        in_specs=[pl.BlockSpec((B, RG_TR, D_MODEL), lambda t: (0, t, 0)),
                  pl.BlockSpec((B + 1, 8, D_MODEL), lambda t: (0, 0, 0)),
                  pl.BlockSpec((1, D_MODEL), const),
                  pl.BlockSpec((R, R), const),
                  pl.BlockSpec((D_MODEL, W), const),
                  pl.BlockSpec((D_MODEL, W), lambda t: (0, 1))],
        out_specs=[pl.BlockSpec((R, W), lambda t: (t, 0)),
                   pl.BlockSpec((R, W), lambda t: (t, 0))],
        out_shape=[jax.ShapeDtypeStruct((st.ltot * B, W), BF16), jax.ShapeDtypeStruct((st.ltot * B, W), F32)],
        compiler_params=_cparams(("arbitrary",)),
        name="rg_in_proj",
    )(x.reshape(B, st.ltot, D_MODEL), mod, gain, perm, w_in, w_in)


def _rg_tile_order(d, k, ct, ntt):
    bwd = jnp.where(k < ct, ct - 1 - k, ntt - 1 - (k - ct))
    return jnp.where(d == 0, k, bwd)


def _rg_scan_kernel(um_ref, up_ref, un_ref, cw_ref, cb_ref, wg_ref, gb_ref, lam_ref, o_ref,
                    ext_ref, a_ref, b_ref, h_ref, *, B, ct, ntt):
    d = pl.program_id(0)
    k = pl.program_id(1)
    tile = _rg_tile_order(d, k, ct, ntt)
    R = RG_TT * B

    @pl.when(k == 0)
    def _():
        h_ref[...] = jnp.zeros_like(h_ref)

    seq_start = jnp.logical_or(tile == 0, tile == ct)
    seq_end = jnp.logical_or(tile == ct - 1, tile == ntt - 1)
    ext_ref[0:2 * B, :] = jnp.where(seq_start, 0.0, up_ref[...])
    ext_ref[2 * B:2 * B + R, :] = um_ref[...]
    ext_ref[2 * B + R:3 * B + R, :] = jnp.where(seq_end, 0.0, un_ref[...])
    cw = cw_ref[...]
    uc = cb_ref[...] + cw[0:1, :] * ext_ref[0:R, :]
    for j in range(1, RG_CONV_W):
        uc = uc + cw[j:j + 1, :] * ext_ref[j * B:j * B + R, :]
    ucb = uc.astype(BF16)
    c_half = (-0.5 * RG_C) * _softplus(-lam_ref[0])
    chunks = [slice(c * RG_CHUNK, (c + 1) * RG_CHUNK) for c in range(D_MODEL // RG_CHUNK)]
    zs = [_dot(ucb[:, sl], wg_ref[0, c]) + gb_ref[0, c] for c, sl in enumerate(chunks)]
    for sl, z in zip(chunks, zs):
        log_a = c_half[:, sl] * jnp.tanh(z[:, :RG_CHUNK]) + c_half[:, sl]
        u_half = 0.5 * uc[:, sl]
        gated_u = jnp.tanh(z[:, RG_CHUNK:]) * u_half + u_half
        a = jnp.exp(log_a)
        one_minus_a2 = -jnp.tanh(log_a) * (a * a + 1.0)
        a_ref[:, sl] = a
        b_ref[:, sl] = jnp.sqrt(one_minus_a2) * gated_u

    def scan(times):
        for c in range(D_MODEL // 128):
            cs = slice(c * 128, (c + 1) * 128)
            h = h_ref[:, cs]
            for t in times:
                rs = slice(t * B, (t + 1) * B)
                h = a_ref[rs, cs] * h + b_ref[rs, cs]
                b_ref[rs, cs] = h
            h_ref[:, cs] = h

    pl.when(d == 0)(lambda: scan(range(RG_TT)))
    pl.when(d == 1)(lambda: scan(range(RG_TT - 1, -1, -1)))
    o_ref[0] = b_ref[...].astype(o_ref.dtype)


def _rg_scan(st, u_tm, conv_w, conv_b, wg, gb, lam):
    B, W = st.B, D_MODEL
    assert st.C % RG_TT == 0 and st.L % RG_TT == 0
    ltot = st.ltot
    ntt, ct = ltot // RG_TT, st.C // RG_TT
    R = RG_TT * B
    order = functools.partial(_rg_tile_order, ct=ct, ntt=ntt)
    nch = W // RG_CHUNK
    return pl.pallas_call(
        functools.partial(_rg_scan_kernel, B=B, ct=ct, ntt=ntt),
        grid=(2, ntt),
        in_specs=[pl.BlockSpec((R, W), lambda d, k: (order(d, k), 0)),
                  pl.BlockSpec((2 * B, W), lambda d, k: (jnp.maximum(order(d, k) * (RG_TT // 2) - 1, 0), 0)),
                  pl.BlockSpec((B, W), lambda d, k: (jnp.minimum((order(d, k) + 1) * RG_TT, ltot - 1), 0)),
                  pl.BlockSpec((RG_CONV_W, W), lambda d, k: (0, 0)),
                  pl.BlockSpec((1, W), lambda d, k: (0, 0)),
                  pl.BlockSpec((1, nch, RG_CHUNK, 2 * RG_CHUNK), lambda d, k: (d, 0, 0, 0)),
                  pl.BlockSpec((1, nch, 1, 2 * RG_CHUNK), lambda d, k: (d, 0, 0, 0)),
                  pl.BlockSpec((1, 1, W), lambda d, k: (d, 0, 0))],
        out_specs=pl.BlockSpec((1, R, W), lambda d, k: (d, order(d, k), 0)),
        out_shape=jax.ShapeDtypeStruct((2, ltot * B, W), BF16),
        scratch_shapes=[pltpu.VMEM((3 * B + R, W), F32),
                        pltpu.VMEM((R, W), F32),
                        pltpu.VMEM((R, W), F32),
                        pltpu.VMEM((B, W), F32)],
        compiler_params=_cparams(("arbitrary", "arbitrary")),
        name="rg_scan",
    )(u_tm, u_tm, u_tm, conv_w, conv_b, wg, gb, lam)


def _rg_gate_weights(gate_w, gate_b):
    nb = gate_w.shape[2]
    per = RG_CHUNK // RG_BLOCK_W
    nch = nb // per
    gw = gate_w.reshape(2, 2, nch, per, RG_BLOCK_W, RG_BLOCK_W)
    eye = jnp.eye(per, dtype=gate_w.dtype)
    bd = jnp.einsum('dgcnij,nm->dgcnimj', gw, eye).reshape(2, 2, nch, RG_CHUNK, RG_CHUNK)
    wg = (0.5 * jnp.concatenate([bd[:, 0], bd[:, 1]], axis=-1)).astype(BF16)
    gb = gate_b.reshape(2, 2, nch, 1, RG_CHUNK)
    gb = 0.5 * jnp.concatenate([gb[:, 0], gb[:, 1]], axis=-1)
    return wg, gb


def _rg_out_kernel(g_ref, hf_ref, hb_ref, perm_ref, w_ref, x_ref, mod_ref, gain_ref, wr_ref, br_ref,
                   xo_ref, f_ref, r_ref, cnt_ref, *, B, ctx_tiles, t0):
    is_ctx = pl.program_id(0) + t0 < ctx_tiles
    hsum = hf_ref[0].astype(F32) + hb_ref[0].astype(F32)
    lhs = (g_ref[...].astype(F32) * hsum).astype(BF16)
    lhs = _dot(perm_ref[...], lhs).astype(BF16)
    y = _dot(lhs, w_ref[...]).reshape(B, RG_TR, D_MODEL)
    m = lambda row: _time_tile_mod(mod_ref, is_ctx, B, row)
    xn = x_ref[...] + m(2) * y
    xo_ref[...] = xn
    f = _norm_mod(xn, gain_ref[...], m(3), m(4))
    f_ref[...] = f
    lg = _dot3_split(f.reshape(B * RG_TR, D_MODEL), wr_ref[...]) + br_ref[...]
    table, hist = _route(lg)
    r_ref[...] = table.reshape(B, RG_TR, 128)
    _accumulate_counts(cnt_ref, hist)


def _rg_out(st, latent_only, gate, hs, perm, w_out, x, mod, gain, wr, br):
    B = st.B
    R = B * RG_TR
    t0 = st.C // RG_TR if latent_only else 0
    lo = st.L if latent_only else st.ltot
    nt = lo // RG_TR
    const = lambda t: (0, 0)
    blk = lambda w: pl.BlockSpec((B, RG_TR, w), lambda t: (0, t, 0))
    xo, f, route, counts = pl.pallas_call(
        functools.partial(_rg_out_kernel, B=B, ctx_tiles=st.C // RG_TR, t0=t0),
        grid=(nt,),
        in_specs=[pl.BlockSpec((R, D_MODEL), lambda t: (t + t0, 0)),
                  pl.BlockSpec((1, R, D_MODEL), lambda t: (0, t + t0, 0)),
                  pl.BlockSpec((1, R, D_MODEL), lambda t: (1, t + t0, 0)),
                  pl.BlockSpec((R, R), const),
                  pl.BlockSpec((D_MODEL, D_MODEL), const),
                  pl.BlockSpec((B, RG_TR, D_MODEL), lambda t: (0, t + t0, 0)),
                  pl.BlockSpec((B + 1, 8, D_MODEL), lambda t: (0, 0, 0)),
                  pl.BlockSpec((1, D_MODEL), const),
                  pl.BlockSpec((D_MODEL, 256), const),
                  pl.BlockSpec((1, 128), const)],
        out_specs=[blk(D_MODEL), blk(D_MODEL), blk(128), pl.BlockSpec((1, 128), const)],
        out_shape=[jax.ShapeDtypeStruct((B, lo, D_MODEL), F32),
                   jax.ShapeDtypeStruct((B, lo, D_MODEL), F32),
                   jax.ShapeDtypeStruct((B, lo, 128), F32),
                   jax.ShapeDtypeStruct((1, 128), F32)],
        compiler_params=_cparams(("arbitrary",)),
        name="rg_out",
    )(gate, hs, hs, perm, w_out, x.reshape(B, st.ltot, D_MODEL), mod, gain, wr, br)
    return xo.reshape(B * lo, D_MODEL), f.reshape(B * lo, D_MODEL), route.reshape(B * lo, 128), counts


_ROPE_HALF = ROPE_AXIS_DIM // 2
_MLA_SRC_DIM = (list(range(MLA_NOPE + ROPE_AXIS_DIM)) + list(range(MLA_NOPE, MLA_NOPE + _ROPE_HALF))
                + list(range(MLA_NOPE + ROPE_AXIS_DIM, MLA_QK))
                + list(range(MLA_NOPE + ROPE_AXIS_DIM, MLA_NOPE + ROPE_AXIS_DIM + _ROPE_HALF)))
_MLA_REAL_LANE = ([1.0] * (MLA_NOPE + ROPE_AXIS_DIM) + [0.0] * _ROPE_HALF + [1.0] * ROPE_AXIS_DIM
                  + [0.0] * _ROPE_HALF + [0.0] * (MLA_HP - len(_MLA_SRC_DIM)))


def _mla_up_kernel(dn_ref, qn_ref, kvn_ref, wq_ref, wk_ref, we_ref, wv_ref, real_ref, cq_ref, sq_ref, ck_ref, sk_ref,
                   q_ref, k_ref, v_ref):
    dn = dn_ref[...]
    cq = _rms(dn[:, :MLA_Q_RANK]) * qn_ref[...]
    ckv = _rms(dn[:, MLA_Q_RANK:MLA_Q_RANK + MLA_KV_RANK]) * kvn_ref[...]
    kr = dn[:, MLA_Q_RANK + MLA_KV_RANK:]
    kr_hi = kr.astype(BF16)
    kr_lo = (kr - kr_hi.astype(F32)).astype(BF16)
    ckvb = ckv.astype(BF16)
    q_pre = _dot(cq.astype(BF16), wq_ref[...])
    k_pre = _dot(ckvb, wk_ref[...]) + (_dot(kr_hi, we_ref[...]) + _dot(kr_lo, we_ref[...]))
    v_ref[...] = _dot(ckvb, wv_ref[...]).astype(v_ref.dtype)
    real = real_ref[...]

    cos_q, sin_q, cos_k, sin_k = cq_ref[...], sq_ref[...], ck_ref[...], sk_ref[...]
    jobs = []
    for h in range(MLA_HEADS):
        sl = slice(h * MLA_HP, (h + 1) * MLA_HP)
        jobs.append((q_ref, sl, q_pre[:, sl], cos_q, sin_q))
        jobs.append((k_ref, sl, k_pre[:, sl], cos_k, sin_k))
    ms = [jnp.sum(x * x * real, axis=-1, keepdims=True) * (1.0 / MLA_QK) for _, _, x, _, _ in jobs]
    xr = [job[2] * lax.rsqrt(m + RMS_EPS) for job, m in zip(jobs, ms)]
    rolled = [pltpu.roll(v, MLA_HP - _ROPE_HALF, 1) for v in xr]
    for (o_ref, sl, _, cos_g, sin_g), v, r in zip(jobs, xr, rolled):
        o_ref[:, sl] = (v * cos_g + r * sin_g).astype(o_ref.dtype)


def _mla_up(st, rows, down, q_norm, kv_norm, wq, wk, we, wv, tables):
    hw = MLA_HEADS * MLA_HP
    const = lambda i: (0, 0)
    tpb, ct, lt = st.tpb, st.ct, st.lt
    rope_idx = lambda i: (jnp.where(i % tpb < ct, lt, i % tpb - ct), 0)
    real = jnp.asarray(_MLA_REAL_LANE, F32).reshape(1, MLA_HP)
    return pl.pallas_call(
        _mla_up_kernel,
        grid=(rows.n,),
        in_specs=[pl.BlockSpec((TM, 512), lambda i: (i, 0)),
                  pl.BlockSpec((1, MLA_Q_RANK), const),
                  pl.BlockSpec((1, MLA_KV_RANK), const),
                  pl.BlockSpec((MLA_Q_RANK, hw), const),
                  pl.BlockSpec((MLA_KV_RANK, hw), const),
                  pl.BlockSpec((128, hw), const),
                  pl.BlockSpec((MLA_KV_RANK, MLA_HEADS * MLA_V), const),
                  pl.BlockSpec((1, MLA_HP), const)] + [pl.BlockSpec((TM, MLA_HP), rope_idx)] * 4,
        out_specs=[pl.BlockSpec((TM, hw), lambda i: (i, 0)),
                   pl.BlockSpec((TM, hw), lambda i: (i, 0)),
                   pl.BlockSpec((TM, MLA_HEADS * MLA_V), lambda i: (i, 0))],
        out_shape=[jax.ShapeDtypeStruct((st.NT, hw), BF16),
                   jax.ShapeDtypeStruct((st.NT, hw), BF16),
                   jax.ShapeDtypeStruct((st.NT, MLA_HEADS * MLA_V), BF16)],
        compiler_params=_cparams(("arbitrary",)),
        name="mla_up_proj",
    )(down, q_norm, kv_norm, wq, wk, we, wv, real, *tables)


def _attn_kernel(q_ref, k_ref, v_ref, o_ref, vaug_ref, *, C, ct):
    qi = pl.program_id(2)

    @pl.when(qi == 0)
    def _():
        lane = lax.broadcasted_iota(jnp.int32, (k_ref.shape[0], 2 * MLA_V), 1)
        for hh in range(ATT_HEADS):
            pair = v_ref[:, (hh // 2) * 2 * MLA_V:(hh // 2 + 1) * 2 * MLA_V].astype(F32)
            if hh % 2 == 0:
                aug = jnp.where(lane < MLA_V, pair, jnp.where(lane == MLA_V, 1.0, 0.0))
            else:
                aug = jnp.where(lane >= MLA_V, pair, jnp.where(lane == 0, 1.0, 0.0))
            vaug_ref[hh] = aug.astype(BF16)

    def attend(nkeys):
        lane = lax.broadcasted_iota(jnp.int32, (TM, 2 * MLA_V), 1)

        def scores(hh):
            sl = slice(hh * MLA_HP, (hh + 1) * MLA_HP)
            return _dot_t(q_ref[:, sl], k_ref[0:nkeys, sl])

        s_next = scores(0)
        outs = []
        for hh in range(ATT_HEADS):
            s = s_next
            if hh + 1 < ATT_HEADS:
                s_next = scores(hh + 1)
            p = jnp.exp2(s - jnp.max(s, axis=-1, keepdims=True)).astype(BF16)
            o = _dot(p, vaug_ref[hh, 0:nkeys, :])
            rowsum = o[:, MLA_V:MLA_V + 1] if hh % 2 == 0 else o[:, 0:1]
            outs.append(o * (1.0 / rowsum))
            if hh % 2 == 1:
                pair = hh // 2
                o_ref[:, pair * 2 * MLA_V:(pair + 1) * 2 * MLA_V] = jnp.where(
                    lane < MLA_V, outs[hh - 1], outs[hh]).astype(o_ref.dtype)

    pl.when(qi < ct)(lambda: attend(C))
    pl.when(qi >= ct)(lambda: attend(k_ref.shape[0]))


def _attention(st, q, k, v):
    B, ltot, tpb = st.B, st.ltot, st.tpb
    hg = MLA_HEADS // ATT_HEADS
    return pl.pallas_call(
        functools.partial(_attn_kernel, C=st.C, ct=st.ct),
        grid=(B, hg, tpb),
        in_specs=[pl.BlockSpec((TM, ATT_HEADS * MLA_HP), lambda b, h, i: (b * tpb + i, h)),
                  pl.BlockSpec((ltot, ATT_HEADS * MLA_HP), lambda b, h, i: (b, h)),
                  pl.BlockSpec((ltot, ATT_HEADS * MLA_V), lambda b, h, i: (b, h))],
        out_specs=pl.BlockSpec((TM, ATT_HEADS * MLA_V), lambda b, h, i: (b * tpb + i, h)),
        out_shape=jax.ShapeDtypeStruct((st.NT, MLA_HEADS * MLA_V), BF16),
        scratch_shapes=[pltpu.VMEM((ATT_HEADS, ltot, 2 * MLA_V), BF16)],
        compiler_params=_cparams(("arbitrary", "arbitrary", "arbitrary")),
        name="mla_attention",
    )(q, k, v)


def _mla_weights(w_uq, w_ukv):
    H = MLA_HEADS
    src = jnp.asarray(_MLA_SRC_DIM, jnp.int32)
    pad = MLA_HP - len(_MLA_SRC_DIM)
    wq = jnp.pad(w_uq.reshape(MLA_Q_RANK, H, MLA_QK)[:, :, src], ((0, 0), (0, 0), (0, pad)))
    wkv = w_ukv.reshape(MLA_KV_RANK, H, MLA_NOPE + MLA_V)
    wk = jnp.pad(wkv[:, :, :MLA_NOPE], ((0, 0), (0, 0), (0, MLA_HP - MLA_NOPE)))
    wv = wkv[:, :, MLA_NOPE:]
    r = jnp.arange(128)[:, None]
    lane_dim = jnp.pad(src, (0, pad), constant_values=-1)[None, :]
    place = (lane_dim == r + MLA_NOPE).astype(BF16)
    we = jnp.tile(place, (1, H))
    return (wq.reshape(MLA_Q_RANK, H * MLA_HP).astype(BF16), wk.reshape(MLA_KV_RANK, H * MLA_HP).astype(BF16),
            we, wv.reshape(MLA_KV_RANK, H * MLA_V).astype(BF16))


def _rope_tables(L, qk_norm):
    rows = L // GRID_W
    row = jnp.broadcast_to(jnp.arange(rows, dtype=F32)[:, None], (rows, GRID_W)).reshape(L)
    col = jnp.broadcast_to(jnp.arange(GRID_W, dtype=F32)[None, :], (rows, GRID_W)).reshape(L)
    inv_freq = ROPE_BASE ** (-jnp.arange(0, ROPE_AXIS_DIM, 2, dtype=F32) / ROPE_AXIS_DIM)
    ar = row[:, None] * inv_freq
    ac = col[:, None] * inv_freq
    h8 = _ROPE_HALF
    one = jnp.ones((L, MLA_NOPE), F32)
    z8 = jnp.zeros((L, h8), F32)
    zpad = jnp.zeros((L, MLA_HP - len(_MLA_SRC_DIM)), F32)
    cos_t = jnp.concatenate([one, jnp.cos(ar), jnp.cos(ar), z8, jnp.cos(ac), jnp.cos(ac), z8, zpad], axis=1)
    sin_t = jnp.concatenate([0 * one, -jnp.sin(ar), jnp.sin(ar), z8, -jnp.sin(ac), jnp.sin(ac), z8, zpad], axis=1)
    ident = jnp.asarray(_MLA_REAL_LANE, F32)[None, :]
    cos_t = jnp.concatenate([cos_t, jnp.broadcast_to(ident, (TM, MLA_HP))], axis=0)
    sin_t = jnp.concatenate([sin_t, jnp.zeros((TM, MLA_HP), F32)], axis=0)
    src = jnp.asarray(_MLA_SRC_DIM, jnp.int32)
    pad = MLA_HP - len(_MLA_SRC_DIM)
    scale = MLA_QK ** -0.5 * math.log2(math.e)
    tables = []
    for g, s in ((qk_norm[0], scale), (qk_norm[1], 1.0)):
        g_lane = jnp.pad(g[src], (0, pad))
        g_partner = jnp.roll(g_lane, -h8)
        tables += [cos_t * (g_lane * s)[None, :], sin_t * (g_partner * s)[None, :]]
    return tables


def _log_sigmoid(x):
    return jnp.minimum(x, 0.0) - jnp.log1p(jnp.exp(-jnp.abs(x)))


def _mlstm_kernel(qf_ref, kf_ref, vf_ref, gf_ref, qb_ref, kb_ref, vb_ref, gb_ref, bias_ref, tril_ref, triu_ref,
                  of_ref, ob_ref, c_ref, n_ref, m_ref):
    T = ML_TC

    @pl.when(pl.program_id(1) == 0)
    def _():
        c_ref[...] = jnp.zeros_like(c_ref)
        n_ref[...] = jnp.zeros_like(n_ref)
        m_ref[...] = jnp.full(m_ref.shape, ML_M_INIT, F32)

    ti = lax.broadcasted_iota(jnp.int32, (T, T), 0)
    si = lax.broadcasted_iota(jnp.int32, (T, T), 1)
    dirs = ((qf_ref, kf_ref, vf_ref, gf_ref, of_ref, tril_ref), (qb_ref, kb_ref, vb_ref, gb_ref, ob_ref, triu_ref))
    for d, (q_ref, k_ref, v_ref, g_ref, o_ref, tri_ref) in enumerate(dirs):
        tri = (si <= ti) if d == 0 else (si >= ti)
        g = g_ref[...] + bias_ref[...]
        g_t = g.T
        lsg = _log_sigmoid(g)
        lsg_hi = lsg.astype(BF16)
        lsg_lo = (lsg - lsg_hi.astype(F32)).astype(BF16)
        cum = _dot(tri_ref[...], lsg_hi) + _dot(tri_ref[...], lsg_lo)
        cum_t = cum.T
        last = T - 1 if d == 0 else 0
        for h in range(ML_HEADS):
            st = d * ML_HEADS + h
            li, lf_ = (2 * d) * ML_HEADS + h, (2 * d + 1) * ML_HEADS + h
            ig_col = g[:, li:li + 1]
            ig_row = g_t[li:li + 1, :]
            b_col = cum[:, lf_:lf_ + 1]
            b_row = cum_t[lf_:lf_ + 1, :]
            total = cum[last:last + 1, lf_:lf_ + 1]
            m_old = m_ref[st, 0:1, 0:1]
            d_log = jnp.where(tri, b_col - b_row + ig_row, -jnp.inf)
            inter_log = b_col + m_old
            m_t = jnp.maximum(inter_log, jnp.max(d_log, axis=1, keepdims=True))
            qh = q_ref[:, h * ML_DQK:(h + 1) * ML_DQK] * (ML_DQK ** -0.5)
            kh = k_ref[:, h * ML_DQK:(h + 1) * ML_DQK]
            vh = v_ref[:, h * ML_DV:(h + 1) * ML_DV].astype(BF16)
            qb16 = qh.astype(BF16)
            s_mat = _dot_t(qb16, kh.astype(BF16)) * jnp.exp(d_log - m_t)
            inter = jnp.exp(inter_log - m_t)
            c_old = c_ref[st]
            n_old = n_ref[st, 0:1, :]
            num = _dot(s_mat.astype(BF16), vh) + inter * _dot(qb16, c_old.astype(BF16))
            den = jnp.sum(s_mat, axis=1, keepdims=True) + inter * jnp.sum(qh * n_old, axis=1, keepdims=True)
            o_ref[:, h * ML_DV:(h + 1) * ML_DV] = num / jnp.maximum(jnp.abs(den), jnp.exp(-m_t))
            w_log = total - b_col + ig_col
            m_new = jnp.maximum(total + m_old, jnp.max(w_log, axis=0, keepdims=True))
            w = jnp.exp(w_log - m_new)
            decay = jnp.exp(total + m_old - m_new)
            kw = kh * w
            c_ref[st] = decay * c_old + _dot(kw.T.astype(BF16), vh)
            n_ref[st, 0:1, :] = decay * n_old + jnp.sum(kw, axis=0, keepdims=True)
            m_ref[st] = jnp.broadcast_to(m_new, m_ref.shape[1:])


def _mlstm(st, proj, gate_bias):
    B = st.B
    assert st.L % ML_TC == 0 and st.C % ML_TC == 0
    cc = st.C // ML_TC
    nch = st.ltot // ML_TC
    qw = ML_HEADS * ML_DQK
    vw = ML_HEADS * ML_DV
    gcol = (2 * qw + 2 * vw) // 128

    def rb(d, b, k):
        chunk = k if d == 0 else jnp.where(k < cc, cc - 1 - k, nch - 1 - (k - cc))
        return b * nch + chunk

    def specs(d):
        return [pl.BlockSpec((ML_TC, qw), lambda b, k: (rb(d, b, k), 0)),
                pl.BlockSpec((ML_TC, qw), lambda b, k: (rb(d, b, k), 1)),
                pl.BlockSpec((ML_TC, vw), lambda b, k: (rb(d, b, k), (2 * qw) // vw)),
                pl.BlockSpec((ML_TC, 128), lambda b, k: (rb(d, b, k), gcol))]

    nst = 2 * ML_HEADS
    out = jax.ShapeDtypeStruct((st.NT, vw), F32)
    tril = jnp.tril(jnp.ones((ML_TC, ML_TC), BF16))
    return pl.pallas_call(
        _mlstm_kernel,
        grid=(B, nch),
        in_specs=specs(0) + specs(1) + [pl.BlockSpec((1, 128), lambda b, k: (0, 0)),
                                        pl.BlockSpec((ML_TC, ML_TC), lambda b, k: (0, 0)),
                                        pl.BlockSpec((ML_TC, ML_TC), lambda b, k: (0, 0))],
        out_specs=[pl.BlockSpec((ML_TC, vw), lambda b, k: (rb(0, b, k), 0)),
                   pl.BlockSpec((ML_TC, vw), lambda b, k: (rb(1, b, k), 0))],
        out_shape=[out, out],
        scratch_shapes=[pltpu.VMEM((nst, ML_DQK, ML_DV), F32),
                        pltpu.VMEM((nst, 8, ML_DQK), F32),
                        pltpu.VMEM((nst, 8, 128), F32)],
        compiler_params=_cparams(("arbitrary", "arbitrary")),
        name="mlstm_chunks",
    )(proj, proj, proj, proj, proj, proj, proj, proj, gate_bias, tril, tril.T)


def _route(lg):
    lane_i = lax.broadcasted_iota(jnp.int32, lg.shape, 1)
    lane = lane_i.astype(F32)
    neg = -jnp.inf
    gl = jnp.where(lane_i < MOE_GROUPS, lg, neg)
    gmax = jnp.max(gl, axis=-1, keepdims=True)
    gsum = jnp.sum(jnp.where(lane_i < MOE_GROUPS, jnp.exp(lg - gmax), 0.0), axis=-1, keepdims=True)
    p_top = 1.0 / gsum
    g_sel = jnp.min(jnp.where(gl == gmax, lane, 128.0), axis=-1, keepdims=True)
    group_of_lane = (lane_i >> 3).astype(F32) - 1.0
    el = jnp.where(group_of_lane == g_sel, lg, neg)
    e1 = jnp.max(el, axis=-1, keepdims=True)
    i1 = jnp.min(jnp.where(el == e1, lane, 128.0), axis=-1, keepdims=True)
    el2 = jnp.where(lane == i1, neg, el)
    e2 = jnp.max(el2, axis=-1, keepdims=True)
    i2 = jnp.min(jnp.where(el2 == e2, lane, 128.0), axis=-1, keepdims=True)
    t = jnp.exp(e2 - e1)
    w1 = p_top / (1.0 + t)
    w2 = w1 * t
    id1 = i1 - MOE_GROUPS
    id2 = i2 - MOE_GROUPS
    table = jnp.where(lane_i == 0, id1, jnp.where(lane_i == 1, id2,
                                                  jnp.where(lane_i == 2, w1, jnp.where(lane_i == 3, w2, 0.0))))
    chosen = jnp.where(lane == i1, 1.0, 0.0) + jnp.where(lane == i2, 1.0, 0.0)
    return table, jnp.sum(chosen, axis=0, keepdims=True)


def _accumulate_counts(cnt_ref, hist):
    @pl.when(pl.program_id(0) == 0)
    def _():
        cnt_ref[...] = jnp.zeros_like(cnt_ref)
    cnt_ref[...] += hist


def _lhs_mla(o_ref):
    return o_ref[...]


def _lhs_mlstm(hf_ref, hb_ref, og_ref, onorm_ref):
    hs = hf_ref[...] + hb_ref[...]
    og = _sigmoid(og_ref[...])
    parts = []
    for h in range(ML_HEADS):
        sl = slice(h * ML_DV, (h + 1) * ML_DV)
        parts.append(_rms(hs[:, sl]) * onorm_ref[:, sl] * og[:, sl])
    return jnp.concatenate(parts, axis=1)


def _mixer_out_kernel(*refs, n_lhs, lhs_fn):
    lhs_refs = refs[:n_lhs]
    w_ref, x_ref, mod_ref, gain_ref, wr_ref, br_ref, xo_ref, f_ref, r_ref, cnt_ref = refs[n_lhs:]
    y = _dot(lhs_fn(*lhs_refs).astype(BF16), w_ref[...])
    mod = mod_ref[0]
    xn = x_ref[...] + mod[2:3, :] * y
    xo_ref[...] = xn
    f = _norm_mod(xn, gain_ref[...], mod[3:4, :], mod[4:5, :])
    f_ref[...] = f
    r_ref[...], hist = _route(_dot3_split(f, wr_ref[...]) + br_ref[...])
    _accumulate_counts(cnt_ref, hist)


def _mixer_out(rows, lhs_fn, lhs_args, lhs_specs, w_out, x, mod, gain, wr, br, name):
    n = rows.n * TM
    const = lambda k: (0, 0)
    out = lambda w: pl.BlockSpec((TM, w), lambda k: (k, 0))
    return pl.pallas_call(
        functools.partial(_mixer_out_kernel, n_lhs=len(lhs_args), lhs_fn=lhs_fn),
        grid=(rows.n,),
        in_specs=list(lhs_specs) + [
            pl.BlockSpec((D_MODEL, D_MODEL), const),
            pl.BlockSpec((TM, D_MODEL), lambda k: (rows.src(k), 0)),
            pl.BlockSpec((1, 8, D_MODEL), lambda k: (rows.mod(k), 0, 0)),
            pl.BlockSpec((1, D_MODEL), const),
            pl.BlockSpec((D_MODEL, 256), const),
            pl.BlockSpec((1, 128), const)],
        out_specs=[out(D_MODEL), out(D_MODEL), out(128), pl.BlockSpec((1, 128), const)],
        out_shape=[jax.ShapeDtypeStruct((n, D_MODEL), F32),
                   jax.ShapeDtypeStruct((n, D_MODEL), F32),
                   jax.ShapeDtypeStruct((n, 128), F32),
                   jax.ShapeDtypeStruct((1, 128), F32)],
        compiler_params=_cparams(("arbitrary",)),
        name=name,
    )(*lhs_args, w_out, x, mod, gain, wr, br)


RANK_TILES = 4


def _rank_kernel(r_ref, start_ref, tril_ref, pos_ref, carry_ref):
    @pl.when(pl.program_id(0) == 0)
    def _():
        carry_ref[...] = jnp.zeros_like(carry_ref)

    r = r_ref[...]
    lane = lax.broadcasted_iota(jnp.int32, r.shape, 1).astype(F32)
    chosen = [jnp.where(lane == r[:, k:k + 1] + MOE_GROUPS, 1.0, 0.0) for k in range(MOE_TOPK)]
    both = functools.reduce(jnp.add, chosen)
    before = _dot(tril_ref[...], both.astype(BF16)) + (start_ref[...] + carry_ref[...])
    lane_i = lax.broadcasted_iota(jnp.int32, r.shape, 1)
    pos = jnp.zeros(r.shape, F32)
    for k in range(MOE_TOPK):
        pos = jnp.where(lane_i == k, jnp.sum(chosen[k] * before, axis=-1, keepdims=True), pos)
    pos_t = pos.T[0:8, :].astype(jnp.int32)
    for j in range(RANK_TILES):
        pos_ref[j] = pos_t[:, j * TM:(j + 1) * TM]
    carry_ref[...] += jnp.sum(both, axis=0, keepdims=True)


def _assignment_slots(ntiles, route, starts):
    assert ntiles % RANK_TILES == 0
    rows = RANK_TILES * TM
    tril = jnp.tril(jnp.ones((rows, rows), BF16), -1)
    return pl.pallas_call(
        _rank_kernel,
        grid=(ntiles // RANK_TILES,),
        in_specs=[pl.BlockSpec((rows, 128), lambda i: (i, 0)),
                  pl.BlockSpec((1, 128), lambda i: (0, 0)),
                  pl.BlockSpec((rows, rows), lambda i: (0, 0))],
        out_specs=pl.BlockSpec((RANK_TILES, 8, TM), lambda i: (i, 0, 0)),
        out_shape=jax.ShapeDtypeStruct((ntiles, 8, TM), jnp.int32),
        scratch_shapes=[pltpu.VMEM((1, 128), F32)],
        compiler_params=_cparams(("arbitrary",)),
        name="moe_rank",
    )(route, starts, tril)


def _row_wait(hbm, buf, sem):
    pltpu.make_async_copy(hbm.at[pl.ds(0, TM), :], buf, sem).wait()


def _dispatch_kernel(pos_ref, f_ref, xs_hbm, buf, sem, *, ntiles):
    i = pl.program_id(0)

    def step(s):
        @pl.when(i >= 2)
        def _():
            for _ in range(MOE_TOPK):
                _row_wait(xs_hbm, buf.at[s], sem.at[s])
        buf[s] = f_ref[...]
        for r in range(TM):
            for k in range(MOE_TOPK):
                pltpu.make_async_copy(buf.at[s, pl.ds(r, 1), :], xs_hbm.at[pl.ds(pos_ref[0, k, r], 1), :],
                                      sem.at[s]).start(priority=k % 2)

        @pl.when(i == ntiles - 1)
        def _():
            for slot in ((1 - s, s) if ntiles >= 2 else (s,)):
                for _ in range(MOE_TOPK):
                    _row_wait(xs_hbm, buf.at[slot], sem.at[slot])

    for s in range(2):
        pl.when(i % 2 == s)(functools.partial(step, s))


def _dispatch(ntiles, f, pos):
    n = ntiles * TM
    return pl.pallas_call(
        functools.partial(_dispatch_kernel, ntiles=ntiles),
        grid=(ntiles,),
        in_specs=[pl.BlockSpec((1, 8, TM), lambda i: (i, 0, 0), memory_space=pltpu.SMEM),
                  pl.BlockSpec((TM, D_MODEL), lambda i: (i, 0))],
        out_specs=pl.BlockSpec(memory_space=pl.ANY),
        out_shape=jax.ShapeDtypeStruct((MOE_TOPK * n, D_MODEL), F32),
        scratch_shapes=[pltpu.VMEM((2, TM, D_MODEL), F32), pltpu.SemaphoreType.DMA((2,))],
        compiler_params=_cparams(("arbitrary",)),
        name="moe_dispatch",
    )(pos, f)


def _expert_kernel(vb_ref, ve_ref, lo_ref, hi_ref, first_ref, newexp_ref, eslot_ref, enext_ref, x_ref, wgu_hbm,
                   wd_hbm, y_ref, wgu_f, wd_f, wgu_b, wd_b, sem, *, layer):
    v = pl.program_id(0)

    def fetch(e, s):
        return (pltpu.make_async_copy(wgu_hbm.at[layer, e], wgu_f.at[s], sem.at[0, s]),
                pltpu.make_async_copy(wd_hbm.at[layer, e], wd_f.at[s], sem.at[1, s]))

    @pl.when(newexp_ref[v] == 1)
    def _():
        for s in range(2):
            @pl.when(eslot_ref[v] == s)
            def _():
                @pl.when(v == 0)
                def _():
                    for c in fetch(ve_ref[v], s):
                        c.start()

                @pl.when(enext_ref[v] >= 0)
                def _():
                    for c in fetch(enext_ref[v], 1 - s):
                        c.start()
                for c in fetch(ve_ref[v], s):
                    c.wait()
                wgu_b[...] = wgu_f[s].astype(BF16)
                wd_b[...] = wd_f[s].astype(BF16)

    @pl.when(hi_ref[v] > lo_ref[v])
    def _():
        gu = _dot(x_ref[...].astype(BF16), wgu_b[...])
        gate = gu[:, :MOE_FF]
        act = gate * _sigmoid(gate) * gu[:, MOE_FF:]
        y = _dot(act.astype(BF16), wd_b[...])
        r = lax.broadcasted_iota(jnp.int32, (MOE_BM, 1), 0)
        mine = jnp.logical_and(r >= lo_ref[v], r < hi_ref[v])
        y = jnp.where(mine, y, 0.0)

        @pl.when(first_ref[v] == 1)
        def _():
            y_ref[...] = y

        @pl.when(first_ref[v] == 0)
        def _():
            y_ref[...] += y


def _expert_ffn(xs, visits, layer, w_gate_up, w_down):
    nvis = visits[0].shape[0]
    blk_idx = lambda v, vb, *_: (vb[v], 0)
    grid_spec = pltpu.PrefetchScalarGridSpec(
        num_scalar_prefetch=8,
        grid=(nvis,),
        in_specs=[pl.BlockSpec((MOE_BM, D_MODEL), blk_idx),
                  pl.BlockSpec(memory_space=pl.ANY),
                  pl.BlockSpec(memory_space=pl.ANY)],
        out_specs=pl.BlockSpec((MOE_BM, D_MODEL), blk_idx),
        scratch_shapes=[pltpu.VMEM((2, D_MODEL, 2 * MOE_FF), F32), pltpu.VMEM((2, MOE_FF, D_MODEL), F32),
                        pltpu.VMEM((D_MODEL, 2 * MOE_FF), BF16), pltpu.VMEM((MOE_FF, D_MODEL), BF16),
                        pltpu.SemaphoreType.DMA((2, 2))],
    )
    return pl.pallas_call(
        functools.partial(_expert_kernel, layer=layer),
        grid_spec=grid_spec,
        out_shape=jax.ShapeDtypeStruct(xs.shape, F32),
        compiler_params=_cparams(("arbitrary",)),
        name="moe_expert_ffn",
    )(*visits, xs, w_gate_up, w_down)


def _combine_kernel(pos_ref, nxt_ref, x_ref, r_ref, mod_ref, ys_hbm, o_ref, ybuf, sem, *, ntiles):
    i = pl.program_id(0)

    def gather(table, s):
        for r in range(TM):
            for k in range(MOE_TOPK):
                pltpu.make_async_copy(ys_hbm.at[pl.ds(table[0, k, r], 1), :], ybuf.at[s, k, pl.ds(r, 1), :],
                                      sem.at[s]).start(priority=k % 2)

    @pl.when(i == 0)
    def _():
        gather(pos_ref, 0)

    def step(s):
        @pl.when(i + 1 < ntiles)
        def _():
            gather(nxt_ref, 1 - s)
        for k in range(MOE_TOPK):
            _row_wait(ys_hbm, ybuf.at[s, k], sem.at[s])
        w = r_ref[...]
        y = w[:, MOE_TOPK:MOE_TOPK + 1] * ybuf[s, 0]
        for k in range(1, MOE_TOPK):
            y = y + w[:, MOE_TOPK + k:MOE_TOPK + k + 1] * ybuf[s, k]
        o_ref[...] = x_ref[...] + mod_ref[0][5:6, :] * y

    for s in range(2):
        pl.when(i % 2 == s)(functools.partial(step, s))


def _combine(rows, x, ys, pos, route, mod):
    n = rows.n
    spec = pl.BlockSpec((TM, D_MODEL), lambda i: (i, 0))
    return pl.pallas_call(
        functools.partial(_combine_kernel, ntiles=n),
        grid=(n,),
        in_specs=[pl.BlockSpec((1, 8, TM), lambda i: (i, 0, 0), memory_space=pltpu.SMEM),
                  pl.BlockSpec((1, 8, TM), lambda i: (jnp.minimum(i + 1, n - 1), 0, 0), memory_space=pltpu.SMEM),
                  spec,
                  pl.BlockSpec((TM, 128), lambda i: (i, 0)),
                  pl.BlockSpec((1, 8, D_MODEL), lambda i: (rows.mod(i), 0, 0)),
                  pl.BlockSpec(memory_space=pl.ANY)],
        out_specs=spec,
        out_shape=jax.ShapeDtypeStruct((n * TM, D_MODEL), F32),
        scratch_shapes=[pltpu.VMEM((2, MOE_TOPK, TM, D_MODEL), F32), pltpu.SemaphoreType.DMA((2,))],
        compiler_params=_cparams(("arbitrary",)),
        name="moe_combine",
    )(pos, pos, x, route, mod, ys)


def _visit_tables(bounds, nk):
    E = MOE_EXPERTS
    nblk = nk // MOE_BM
    nvis = nblk + E
    starts, ends = bounds[:-1], bounds[1:]
    fb = starts // MOE_BM
    nv = jnp.where(ends > starts, (ends - 1) // MOE_BM - fb + 1, 0)
    cum = jnp.cumsum(nv)
    total = cum[-1]
    v = jnp.arange(nvis, dtype=jnp.int32)
    active = v < total
    vc = jnp.minimum(v, total - 1)
    ve = jnp.minimum(jnp.sum((cum[None, :] <= vc[:, None]).astype(jnp.int32), axis=1), E - 1)
    vb = fb[ve] + (vc - (cum - nv)[ve])
    lo = jnp.where(active, jnp.maximum(starts[ve], vb * MOE_BM) - vb * MOE_BM, 0)
    hi = jnp.where(active, jnp.minimum(ends[ve], (vb + 1) * MOE_BM) - vb * MOE_BM, 0)
    prev_b = jnp.concatenate([jnp.full((1,), -1, jnp.int32), vb[:-1]])
    first = jnp.logical_and(active, vb != prev_b)
    prev_e = jnp.concatenate([jnp.full((1,), -1, jnp.int32), ve[:-1]])
    new_expert = jnp.logical_and(active, ve != prev_e)
    eslot = (jnp.cumsum(new_expert.astype(jnp.int32)) - 1) % 2
    at = jnp.where(new_expert, v, nvis)
    nxt = jnp.concatenate([lax.cummin(at[::-1])[::-1][1:], jnp.full((1,), nvis, jnp.int32)])
    enext = jnp.where(nxt < nvis, ve[jnp.minimum(nxt, nvis - 1)], -1)
    i32 = lambda a: a.astype(jnp.int32)
    return i32(vb), i32(ve), i32(lo), i32(hi), i32(first), i32(new_expert), i32(eslot), i32(enext)


def _moe(rows, x, f, route, counts, mod, layer, w_gate_up, w_down):
    n = rows.n * TM
    cum = jnp.cumsum(counts[0])
    starts = (cum - counts[0]).reshape(1, 128)
    bounds = jnp.concatenate([starts[0, MOE_GROUPS:MOE_GROUPS + MOE_EXPERTS], cum[-1:]]).astype(jnp.int32)
    pos = _assignment_slots(rows.n, route, starts)
    xs = _dispatch(rows.n, f, pos)
    ys = _expert_ffn(xs, _visit_tables(bounds, n * MOE_TOPK), layer, w_gate_up, w_down)
    return _combine(rows, x, ys, pos, route, mod)


def kernel(x, c, ctx, c_ctx, ada_w, ada_b, norm_mix, norm_ffn, rg_w_in, rg_conv_w, rg_conv_b, rg_gate_w, rg_gate_b, rg_lambda, rg_w_out, mla_w_down, mla_q_norm, mla_kv_norm, mla_w_uq, mla_w_ukv, mla_qk_norm, mla_w_o, ml_w_in, ml_gate_b, ml_out_norm, ml_w_out, moe_w_group, moe_b_group, moe_w_expert, moe_b_expert, moe_w_gate_up, moe_w_down):
    B, L, D = x.shape
    C = ctx.shape[1]
    depth = ada_w.shape[0]
    assert D == D_MODEL
    st = _Stream(B, L, C)

    xs = jnp.concatenate([ctx, x], axis=1).reshape(st.NT, D)
    cc = jnp.zeros((16, D), F32).at[:B].set(c).at[B].set(c_ctx)
    mod_all = _modulation(cc, ada_w, ada_b)
    mod_all = jnp.pad(mod_all[:, :B + 1].reshape(depth, B + 1, 6, D), ((0, 0), (0, 0), (0, 2), (0, 0)))
    perm = _row_permutation(B)

    row = lambda a: a.reshape(1, -1)

    for i in range(depth):
        last = i == depth - 1
        mod = mod_all[i]
        kind, j = i % 3, i // 3
        all_rows = st.all_rows()
        out_rows = st.latent_rows() if last else all_rows
        tile_spec = lambda w: pl.BlockSpec((TM, w), lambda k: (out_rows.src(k), 0))
        wr = jnp.zeros((D, 128), F32).at[:, :MOE_GROUPS].set(moe_w_group[i]) \
            .at[:, MOE_GROUPS:MOE_GROUPS + MOE_EXPERTS].set(moe_w_expert[i])
        wr_hi = wr.astype(BF16)
        wr = jnp.concatenate([wr_hi, (wr - wr_hi.astype(F32)).astype(BF16)], axis=1)
        br = jnp.zeros((1, 128), F32).at[0, :MOE_GROUPS].set(moe_b_group[i]) \
            .at[0, MOE_GROUPS:MOE_GROUPS + MOE_EXPERTS].set(moe_b_expert[i])
        out_args = (xs, mod, row(norm_ffn[i]), wr, br)

        if kind == 0:
            gate, u = _rg_in(st, xs, mod, row(norm_mix[i]), perm, rg_w_in[j].astype(BF16))
            wg, gb = _rg_gate_weights(rg_gate_w[j], rg_gate_b[j])
            hs = _rg_scan(st, u, rg_conv_w[j], row(rg_conv_b[j]), wg, gb, rg_lambda[j].reshape(2, 1, D))
            xs, f, route, counts = _rg_out(st, last, gate, hs, perm.T, rg_w_out[j].astype(BF16), *out_args)
        elif kind == 1:
            w_down = jnp.pad(mla_w_down[j], ((0, 0), (0, 512 - mla_w_down.shape[2]))).astype(BF16)
            down = _norm_proj(all_rows, xs, mod, row(norm_mix[i]), w_down, name="mla_down_proj")
            wq, wk, we, wv = _mla_weights(mla_w_uq[j], mla_w_ukv[j])
            q, k, v = _mla_up(st, all_rows, down, row(mla_q_norm[j]), row(mla_kv_norm[j]), wq, wk, we, wv,
                              _rope_tables(L, mla_qk_norm[j]))
            o = _attention(st, q, k, v)
            xs, f, route, counts = _mixer_out(out_rows, _lhs_mla, (o,), (tile_spec(D),),
                                      mla_w_o[j].astype(BF16), *out_args, name="mla_out")
        else:
            n_in = ml_w_in.shape[2]
            w_in = jnp.pad(ml_w_in[j], ((0, 0), (0, ML_NP - n_in))).astype(BF16)
            proj = _norm_proj(all_rows, xs, mod, row(norm_mix[i]), w_in, name="mlstm_in_proj")
            gate_bias = jnp.pad(ml_gate_b[j].reshape(1, -1), ((0, 0), (0, 128 - 4 * ML_HEADS)))
            hf, hb = _mlstm(st, proj, gate_bias)
            og_spec = pl.BlockSpec((TM, D), lambda k: (out_rows.src(k), 2))
            xs, f, route, counts = _mixer_out(out_rows, _lhs_mlstm, (hf, hb, proj, row(ml_out_norm[j])),
                                      (tile_spec(D), tile_spec(D), og_spec, pl.BlockSpec((1, D), lambda k: (0, 0))),
                                      ml_w_out[j].astype(BF16), *out_args, name="mlstm_out")

        moe_rows = st.dense_latent_rows() if last else all_rows
        xs = _moe(moe_rows, xs, f, route, counts, mod, i, moe_w_gate_up, moe_w_down)

    return xs.reshape(B, L, D)
```
